```python
import jax, jax.numpy as jnp
from jax import lax
import numpy as np

D_MODEL = 1024
BATCH = 8
SEQ = 8192
DEPTH = 2

N_MIXERS = 2
N_META = 16
FOX_HEADS = 16
FOX_HEAD_DIM = D_MODEL // FOX_HEADS
FOX_Q_BLOCK = 128
HGRN_EXPAND = 128
HGRN_HEADS = D_MODEL // HGRN_EXPAND
HGRN_DK = HGRN_EXPAND
HGRN_DV = D_MODEL // HGRN_HEADS
HGRN_CHUNK = 64
FFN_HIDDEN = -(-8 * D_MODEL // (3 * 256)) * 256
N_FOX = (DEPTH + N_MIXERS - 1) // N_MIXERS
N_HGRN = DEPTH // N_MIXERS
EPS = 1e-6

kernel_name = "fox_hgrn2_interleaved_meta_trunk"


def rms_norm(x, gain):
    xf = x.astype(jnp.float32)
    y = xf * lax.rsqrt(jnp.mean(xf * xf, axis=-1, keepdims=True) + EPS)
    return (y * gain.astype(jnp.float32)).astype(x.dtype)


def _fox_attend(q_blk, c_q, pos_q, k, v, c_k, pos_k):
    s = jnp.einsum('bhqd,bhkd->bhqk', q_blk, k).astype(jnp.float32) * (FOX_HEAD_DIM ** -0.5)
    s = s + (c_q[..., :, None] - c_k[..., None, :])
    s = jnp.where(pos_k[None, :] <= pos_q[:, None], s, -jnp.inf)
    p = jax.nn.softmax(s, axis=-1)
    return jnp.einsum('bhqk,bhkd->bhqd', p, v.astype(jnp.float32))


def fox_mixer(h, w_in, b_f, q_gain, k_gain, w_out):
    B, L, D = h.shape
    n_blk = (L - N_META) // FOX_Q_BLOCK
    proj = h @ w_in
    q, k, v, gate, f_logit = jnp.split(proj, [D, 2 * D, 3 * D, 4 * D], axis=-1)
    q = rms_norm(q.reshape(B, L, FOX_HEADS, FOX_HEAD_DIM), q_gain)
    k = rms_norm(k.reshape(B, L, FOX_HEADS, FOX_HEAD_DIM), k_gain)
    v = v.reshape(B, L, FOX_HEADS, FOX_HEAD_DIM)
    log_f = jax.nn.log_sigmoid(f_logit.astype(jnp.float32) + b_f.astype(jnp.float32))
    c = jnp.cumsum(log_f, axis=1).transpose(0, 2, 1)
    q, k, v = (t.transpose(0, 2, 1, 3) for t in (q, k, v))
    pos = jnp.arange(L)
    o_meta = _fox_attend(q[:, :, :N_META], c[:, :, :N_META], pos[:N_META],
                         k[:, :, :N_META], v[:, :, :N_META], c[:, :, :N_META], pos[:N_META])
    qb = q[:, :, N_META:].reshape(B, FOX_HEADS, n_blk, FOX_Q_BLOCK, FOX_HEAD_DIM).transpose(2, 0, 1, 3, 4)
    cb = c[:, :, N_META:].reshape(B, FOX_HEADS, n_blk, FOX_Q_BLOCK).transpose(2, 0, 1, 3)
    pb = pos[N_META:].reshape(n_blk, FOX_Q_BLOCK)
    o_real = lax.map(lambda a: _fox_attend(a[0], a[1], a[2], k, v, c, pos), (qb, cb, pb))
    o_real = o_real.transpose(1, 2, 0, 3, 4).reshape(B, FOX_HEADS, L - N_META, FOX_HEAD_DIM)
    o = jnp.concatenate([o_meta, o_real], axis=2).transpose(0, 2, 1, 3).reshape(B, L, D)
    o = o * jax.nn.sigmoid(gate.astype(jnp.float32))
    return (o.astype(h.dtype) @ w_out).astype(h.dtype)


def _hgrn_chunk(S, inp):
    q, k, v, g = inp
    C = q.shape[2]
    b = jnp.cumsum(g, axis=2)
    causal = jnp.tril(jnp.ones((C, C), dtype=bool))
    diff = b[:, :, :, None, :] - b[:, :, None, :, :]
    decay = jnp.exp(jnp.where(causal[..., None], diff, -jnp.inf))
    attn = jnp.einsum('bhtd,bhsd,bhtsd->bhts', q, k, decay)
    o = jnp.einsum('bhts,bhsv->bhtv', attn, v) + jnp.einsum('bhtd,bhdv->bhtv', q * jnp.exp(b), S)
    b_last = b[:, :, -1:, :]
    S_new = jnp.exp(b_last[:, :, 0, :])[..., None] * S + jnp.einsum(
        'bhsd,bhsv->bhdv', k * jnp.exp(b_last - b), v)
    return S_new, o


def hgrn2_mixer(h, w_in, lb, g_gain, w_out):
    B, L, D = h.shape
    n_chunk = (L - N_META) // HGRN_CHUNK
    proj = h @ w_in
    q, f_logit, i, g_out = jnp.split(proj, 4, axis=-1)
    z = f_logit.astype(jnp.float32)
    lbf = lb.astype(jnp.float32)
    log_f = jnp.logaddexp(jnp.log(lbf), jnp.log1p(-lbf) + jax.nn.log_sigmoid(z))
    k = (1.0 - lbf) * jax.nn.sigmoid(-z)
    q = jax.nn.silu(q.astype(jnp.float32))
    v = i.astype(jnp.float32)
    heads = lambda t, d: t.reshape(B, L, HGRN_HEADS, d).transpose(0, 2, 1, 3)
    q, k, log_f, v = heads(q, HGRN_DK), heads(k, HGRN_DK), heads(log_f, HGRN_DK), heads(v, HGRN_DV)
    S0 = jnp.zeros((B, HGRN_HEADS, HGRN_DK, HGRN_DV), jnp.float32)
    S_meta, o_meta = _hgrn_chunk(S0, (q[:, :, :N_META], k[:, :, :N_META], v[:, :, :N_META], log_f[:, :, :N_META]))
    chunks = lambda t: t[:, :, N_META:].reshape(B, HGRN_HEADS, n_chunk, HGRN_CHUNK, t.shape[-1]).transpose(2, 0, 1, 3, 4)
    _, o_real = lax.scan(_hgrn_chunk, S_meta, (chunks(q), chunks(k), chunks(v), chunks(log_f)))
    o_real = o_real.transpose(1, 2, 0, 3, 4).reshape(B, HGRN_HEADS, L - N_META, HGRN_DV)
    o = jnp.concatenate([o_meta, o_real], axis=2).transpose(0, 2, 1, 3)
    o = rms_norm(o, g_gain) * jax.nn.silu(g_out.astype(jnp.float32).reshape(B, L, HGRN_HEADS, HGRN_DV))
    return (o.reshape(B, L, D).astype(h.dtype) @ w_out).astype(h.dtype)


def swiglu(h, w_in, w_out):
    gate, up = jnp.split(h @ w_in, 2, axis=-1)
    return ((jax.nn.silu(gate) * up) @ w_out).astype(h.dtype)


def _fwd_setup_inputs(seed: int = 0) -> dict:
    key = jax.random.key(seed)
    ks = jax.random.split(key, 18)
    D = D_MODEL
    nrm = lambda k, shape, fan: jax.random.normal(k, shape, jnp.float32) * fan ** -0.5
    gain = lambda k, shape: 1.0 + 0.02 * jax.random.normal(k, shape, jnp.float32)
    return {
        "x": jax.random.normal(ks[0], (BATCH, SEQ, D), jnp.float32),
        "meta_tokens": jax.random.normal(ks[1], (N_META, D), jnp.float32),
        "attn_norm": gain(ks[2], (DEPTH, D)),
        "ffn_norm": gain(ks[3], (DEPTH, D)),
        "final_norm": gain(ks[4], (D,)),
        "fox_w_in": nrm(ks[5], (N_FOX, D, 4 * D + FOX_HEADS), D),
        "fox_b_f": jax.random.uniform(ks[6], (N_FOX, FOX_HEADS), jnp.float32, 1.0, 4.0),
        "fox_q_norm": gain(ks[7], (N_FOX, FOX_HEAD_DIM)),
        "fox_k_norm": gain(ks[8], (N_FOX, FOX_HEAD_DIM)),
        "fox_w_out": nrm(ks[9], (N_FOX, D, D), D),
        "hgrn_w_in": nrm(ks[10], (N_HGRN, D, 4 * D), D),
        "hgrn_lower_bounds": 0.1 * jax.random.normal(ks[11], (DEPTH, D), jnp.float32),
        "hgrn_g_norm": gain(ks[12], (N_HGRN, HGRN_DV)),
        "hgrn_w_out": nrm(ks[13], (N_HGRN, D, D), D),
        "ffn_w_in": nrm(ks[14], (DEPTH, D, 2 * FFN_HIDDEN), D),
        "ffn_w_out": nrm(ks[15], (DEPTH, FFN_HIDDEN, D), FFN_HIDDEN),
    }


def _fwd_reference(x, meta_tokens, attn_norm, ffn_norm, final_norm, fox_w_in, fox_b_f, fox_q_norm,
              fox_k_norm, fox_w_out, hgrn_w_in, hgrn_lower_bounds, hgrn_g_norm, hgrn_w_out,
              ffn_w_in, ffn_w_out):
    B = x.shape[0]
    meta = jnp.broadcast_to(meta_tokens[None].astype(x.dtype), (B, N_META, D_MODEL))
    h = jnp.concatenate([meta, x], axis=1)
    lb_soft = jax.nn.softmax(hgrn_lower_bounds.astype(jnp.float32), axis=0)
    lower_bounds = jnp.cumsum(lb_soft, axis=0) - lb_soft[0]
    for i in range(DEPTH):
        hn = rms_norm(h, attn_norm[i])
        j = i // N_MIXERS
        if i % N_MIXERS == 0:
            h = h + fox_mixer(hn, fox_w_in[j], fox_b_f[j], fox_q_norm[j], fox_k_norm[j], fox_w_out[j])
        else:
            h = h + hgrn2_mixer(hn, hgrn_w_in[j], lower_bounds[i], hgrn_g_norm[j], hgrn_w_out[j])
        h = h + swiglu(rms_norm(h, ffn_norm[i]), ffn_w_in[i], ffn_w_out[i])
    h = rms_norm(h, final_norm)
    return h[:, N_META:]


import jax as _jax
import jax.numpy as _jnp

TWIN_FORMAT = 'train_step'
FWD_PARAMS = ['x', 'meta_tokens', 'attn_norm', 'ffn_norm', 'final_norm', 'fox_w_in', 'fox_b_f', 'fox_q_norm', 'fox_k_norm', 'fox_w_out', 'hgrn_w_in', 'hgrn_lower_bounds', 'hgrn_g_norm', 'hgrn_w_out', 'ffn_w_in', 'ffn_w_out']
TWIN_WEIGHTS = ['meta_tokens', 'attn_norm', 'ffn_norm', 'final_norm', 'fox_w_in', 'fox_b_f', 'fox_q_norm', 'fox_k_norm', 'fox_w_out', 'hgrn_w_in', 'hgrn_lower_bounds', 'hgrn_g_norm', 'hgrn_w_out', 'ffn_w_in', 'ffn_w_out']
TWIN_DIFF_INPUT = 'x'
TWIN_INPUTS = ['x', 'meta_tokens', 'attn_norm', 'ffn_norm', 'final_norm', 'fox_w_in', 'fox_b_f', 'fox_q_norm', 'fox_k_norm', 'fox_w_out', 'hgrn_w_in', 'hgrn_lower_bounds', 'hgrn_g_norm', 'hgrn_w_out', 'ffn_w_in', 'ffn_w_out', 'loss_target', 'm_meta_tokens', 'm_attn_norm', 'm_ffn_norm', 'm_final_norm', 'm_fox_w_in', 'm_fox_b_f', 'm_fox_q_norm', 'm_fox_k_norm', 'm_fox_w_out', 'm_hgrn_w_in', 'm_hgrn_lower_bounds', 'm_hgrn_g_norm', 'm_hgrn_w_out', 'm_ffn_w_in', 'm_ffn_w_out', 'v_meta_tokens', 'v_attn_norm', 'v_ffn_norm', 'v_final_norm', 'v_fox_w_in', 'v_fox_b_f', 'v_fox_q_norm', 'v_fox_k_norm', 'v_fox_w_out', 'v_hgrn_w_in', 'v_hgrn_lower_bounds', 'v_hgrn_g_norm', 'v_hgrn_w_out', 'v_ffn_w_in', 'v_ffn_w_out']
TWIN_OUTPUTS = ['loss', 'grad_x', 'grad_meta_tokens', 'grad_attn_norm', 'grad_ffn_norm', 'grad_final_norm', 'grad_fox_w_in', 'grad_fox_b_f', 'grad_fox_q_norm', 'grad_fox_k_norm', 'grad_fox_w_out', 'grad_hgrn_w_in', 'grad_hgrn_lower_bounds', 'grad_hgrn_g_norm', 'grad_hgrn_w_out', 'grad_ffn_w_in', 'grad_ffn_w_out', 'delta_meta_tokens', 'delta_attn_norm', 'delta_ffn_norm', 'delta_final_norm', 'delta_fox_w_in', 'delta_fox_b_f', 'delta_fox_q_norm', 'delta_fox_k_norm', 'delta_fox_w_out', 'delta_hgrn_w_in', 'delta_hgrn_lower_bounds', 'delta_hgrn_g_norm', 'delta_hgrn_w_out', 'delta_ffn_w_in', 'delta_ffn_w_out', 'new_m_meta_tokens', 'new_m_attn_norm', 'new_m_ffn_norm', 'new_m_final_norm', 'new_m_fox_w_in', 'new_m_fox_b_f', 'new_m_fox_q_norm', 'new_m_fox_k_norm', 'new_m_fox_w_out', 'new_m_hgrn_w_in', 'new_m_hgrn_lower_bounds', 'new_m_hgrn_g_norm', 'new_m_hgrn_w_out', 'new_m_ffn_w_in', 'new_m_ffn_w_out', 'new_v_meta_tokens', 'new_v_attn_norm', 'new_v_ffn_norm', 'new_v_final_norm', 'new_v_fox_w_in', 'new_v_fox_b_f', 'new_v_fox_q_norm', 'new_v_fox_k_norm', 'new_v_fox_w_out', 'new_v_hgrn_w_in', 'new_v_hgrn_lower_bounds', 'new_v_hgrn_g_norm', 'new_v_hgrn_w_out', 'new_v_ffn_w_in', 'new_v_ffn_w_out']
TWIN_LEAF_KINDS = {'loss': 'loss', 'grad_x': 'grad_x', 'grad_meta_tokens': 'grad_w', 'grad_attn_norm': 'grad_w', 'grad_ffn_norm': 'grad_w', 'grad_final_norm': 'grad_w', 'grad_fox_w_in': 'grad_w', 'grad_fox_b_f': 'grad_w', 'grad_fox_q_norm': 'grad_w', 'grad_fox_k_norm': 'grad_w', 'grad_fox_w_out': 'grad_w', 'grad_hgrn_w_in': 'grad_w', 'grad_hgrn_lower_bounds': 'grad_w', 'grad_hgrn_g_norm': 'grad_w', 'grad_hgrn_w_out': 'grad_w', 'grad_ffn_w_in': 'grad_w', 'grad_ffn_w_out': 'grad_w', 'delta_meta_tokens': 'delta_w', 'delta_attn_norm': 'delta_w', 'delta_ffn_norm': 'delta_w', 'delta_final_norm': 'delta_w', 'delta_fox_w_in': 'delta_w', 'delta_fox_b_f': 'delta_w', 'delta_fox_q_norm': 'delta_w', 'delta_fox_k_norm': 'delta_w', 'delta_fox_w_out': 'delta_w', 'delta_hgrn_w_in': 'delta_w', 'delta_hgrn_lower_bounds': 'delta_w', 'delta_hgrn_g_norm': 'delta_w', 'delta_hgrn_w_out': 'delta_w', 'delta_ffn_w_in': 'delta_w', 'delta_ffn_w_out': 'delta_w', 'new_m_meta_tokens': 'new_m', 'new_m_attn_norm': 'new_m', 'new_m_ffn_norm': 'new_m', 'new_m_final_norm': 'new_m', 'new_m_fox_w_in': 'new_m', 'new_m_fox_b_f': 'new_m', 'new_m_fox_q_norm': 'new_m', 'new_m_fox_k_norm': 'new_m', 'new_m_fox_w_out': 'new_m', 'new_m_hgrn_w_in': 'new_m', 'new_m_hgrn_lower_bounds': 'new_m', 'new_m_hgrn_g_norm': 'new_m', 'new_m_hgrn_w_out': 'new_m', 'new_m_ffn_w_in': 'new_m', 'new_m_ffn_w_out': 'new_m', 'new_v_meta_tokens': 'new_v', 'new_v_attn_norm': 'new_v', 'new_v_ffn_norm': 'new_v', 'new_v_final_norm': 'new_v', 'new_v_fox_w_in': 'new_v', 'new_v_fox_b_f': 'new_v', 'new_v_fox_q_norm': 'new_v', 'new_v_fox_k_norm': 'new_v', 'new_v_fox_w_out': 'new_v', 'new_v_hgrn_w_in': 'new_v', 'new_v_hgrn_lower_bounds': 'new_v', 'new_v_hgrn_g_norm': 'new_v', 'new_v_hgrn_w_out': 'new_v', 'new_v_ffn_w_in': 'new_v', 'new_v_ffn_w_out': 'new_v'}


def _forward(args):
    return _fwd_reference(*[args[k] for k in FWD_PARAMS])


def _output_shape():
    def fwd():
        inp = _fwd_setup_inputs(0)
        return _fwd_reference(*[inp[k] for k in FWD_PARAMS])
    out = _jax.eval_shape(fwd)
    return out.shape, out.dtype

N_MICROBATCH = 1
ADAM_LR = 0.001
ADAM_B1 = 0.9
ADAM_B2 = 0.999
ADAM_EPS = 1e-08
ADAM_WD = 0.01
ADAM_STEP = 10
PER_EXAMPLE_BATCH_AXIS = {'x': 0, 'loss_target': 0}
SHARED_INPUTS = []
_WEIGHT_DTYPES = {'meta_tokens': _jnp.float32, 'attn_norm': _jnp.float32, 'ffn_norm': _jnp.float32, 'final_norm': _jnp.float32, 'fox_w_in': _jnp.float32, 'fox_b_f': _jnp.float32, 'fox_q_norm': _jnp.float32, 'fox_k_norm': _jnp.float32, 'fox_w_out': _jnp.float32, 'hgrn_w_in': _jnp.float32, 'hgrn_lower_bounds': _jnp.float32, 'hgrn_g_norm': _jnp.float32, 'hgrn_w_out': _jnp.float32, 'ffn_w_in': _jnp.float32, 'ffn_w_out': _jnp.float32}
MOMENT_SCALE = {'meta_tokens': 4.387938e-03, 'attn_norm': 1.462683e-01, 'ffn_norm': 1.886383e-01, 'final_norm': 6.394174e+01, 'fox_w_in': 4.863076e-02, 'fox_b_f': 2.873778e-01, 'fox_q_norm': 2.087559e-01, 'fox_k_norm': 2.069663e-01, 'fox_w_out': 6.472476e-02, 'hgrn_w_in': 8.949356e-02, 'hgrn_lower_bounds': 1.146854e-02, 'hgrn_g_norm': 3.273762e-01, 'hgrn_w_out': 1.230012e-01, 'ffn_w_in': 7.972726e-02, 'ffn_w_out': 1.300218e-01}


def _to_microbatches(a, axis):
    t = _jnp.moveaxis(a, axis, 0)
    t = t.reshape((N_MICROBATCH, t.shape[0] // N_MICROBATCH) + t.shape[1:])
    return _jnp.moveaxis(t, 1, axis + 1)


def setup_inputs(seed: int = 0) -> dict:
    inp = _fwd_setup_inputs(seed)
    key = _jax.random.fold_in(_jax.random.key(seed), 7919)
    shape, _ = _output_shape()
    out = dict(inp)
    out["loss_target"] = _jax.random.normal(_jax.random.fold_in(key, 0), shape, _jnp.float32)
    for i, name in enumerate(TWIN_WEIGHTS):
        w = inp[name].astype(_jnp.float32)
        if MOMENT_SCALE is None:
            s = _jnp.sqrt(_jnp.mean(_jnp.square(w)) + 1e-30)
        else:
            s = MOMENT_SCALE[name]
        km, kv = _jax.random.split(_jax.random.fold_in(key, i + 1))
        out[name] = w
        out["m_" + name] = s * _jax.random.normal(km, w.shape, _jnp.float32)
        out["v_" + name] = (s * s) * _jax.random.uniform(kv, w.shape, _jnp.float32, 0.5, 1.5)
    if N_MICROBATCH > 1:
        for name, axis in PER_EXAMPLE_BATCH_AXIS.items():
            out[name] = _to_microbatches(out[name], axis)
    return {'x': out['x'], 'meta_tokens': out['meta_tokens'], 'attn_norm': out['attn_norm'], 'ffn_norm': out['ffn_norm'], 'final_norm': out['final_norm'], 'fox_w_in': out['fox_w_in'], 'fox_b_f': out['fox_b_f'], 'fox_q_norm': out['fox_q_norm'], 'fox_k_norm': out['fox_k_norm'], 'fox_w_out': out['fox_w_out'], 'hgrn_w_in': out['hgrn_w_in'], 'hgrn_lower_bounds': out['hgrn_lower_bounds'], 'hgrn_g_norm': out['hgrn_g_norm'], 'hgrn_w_out': out['hgrn_w_out'], 'ffn_w_in': out['ffn_w_in'], 'ffn_w_out': out['ffn_w_out'], 'loss_target': out['loss_target'], 'm_meta_tokens': out['m_meta_tokens'], 'm_attn_norm': out['m_attn_norm'], 'm_ffn_norm': out['m_ffn_norm'], 'm_final_norm': out['m_final_norm'], 'm_fox_w_in': out['m_fox_w_in'], 'm_fox_b_f': out['m_fox_b_f'], 'm_fox_q_norm': out['m_fox_q_norm'], 'm_fox_k_norm': out['m_fox_k_norm'], 'm_fox_w_out': out['m_fox_w_out'], 'm_hgrn_w_in': out['m_hgrn_w_in'], 'm_hgrn_lower_bounds': out['m_hgrn_lower_bounds'], 'm_hgrn_g_norm': out['m_hgrn_g_norm'], 'm_hgrn_w_out': out['m_hgrn_w_out'], 'm_ffn_w_in': out['m_ffn_w_in'], 'm_ffn_w_out': out['m_ffn_w_out'], 'v_meta_tokens': out['v_meta_tokens'], 'v_attn_norm': out['v_attn_norm'], 'v_ffn_norm': out['v_ffn_norm'], 'v_final_norm': out['v_final_norm'], 'v_fox_w_in': out['v_fox_w_in'], 'v_fox_b_f': out['v_fox_b_f'], 'v_fox_q_norm': out['v_fox_q_norm'], 'v_fox_k_norm': out['v_fox_k_norm'], 'v_fox_w_out': out['v_fox_w_out'], 'v_hgrn_w_in': out['v_hgrn_w_in'], 'v_hgrn_lower_bounds': out['v_hgrn_lower_bounds'], 'v_hgrn_g_norm': out['v_hgrn_g_norm'], 'v_hgrn_w_out': out['v_hgrn_w_out'], 'v_ffn_w_in': out['v_ffn_w_in'], 'v_ffn_w_out': out['v_ffn_w_out']}


def _loss(weights, diff, rest, loss_target):
    with _jax.named_scope("forward"):
        args = {**rest, TWIN_DIFF_INPUT: diff, **{k: w.astype(_WEIGHT_DTYPES[k]) for k, w in weights.items()}}
        y = _forward(args)
    with _jax.named_scope("loss_head"):
        err = _jnp.square(y.astype(_jnp.float32) - loss_target)
        return 0.5 * _jnp.sum(_jnp.mean(err, axis=-1)) if err.ndim else 0.5 * err


def _adamw(w, g, m, v):
    m = ADAM_B1 * m + (1.0 - ADAM_B1) * g
    v = ADAM_B2 * v + (1.0 - ADAM_B2) * _jnp.square(g)
    m_hat = m / (1.0 - ADAM_B1 ** ADAM_STEP)
    v_hat = v / (1.0 - ADAM_B2 ** ADAM_STEP)
    delta = -ADAM_LR * (m_hat / (_jnp.sqrt(v_hat) + ADAM_EPS) + ADAM_WD * w)
    return delta, m, v


def reference(x, meta_tokens, attn_norm, ffn_norm, final_norm, fox_w_in, fox_b_f, fox_q_norm, fox_k_norm, fox_w_out, hgrn_w_in, hgrn_lower_bounds, hgrn_g_norm, hgrn_w_out, ffn_w_in, ffn_w_out, loss_target, m_meta_tokens, m_attn_norm, m_ffn_norm, m_final_norm, m_fox_w_in, m_fox_b_f, m_fox_q_norm, m_fox_k_norm, m_fox_w_out, m_hgrn_w_in, m_hgrn_lower_bounds, m_hgrn_g_norm, m_hgrn_w_out, m_ffn_w_in, m_ffn_w_out, v_meta_tokens, v_attn_norm, v_ffn_norm, v_final_norm, v_fox_w_in, v_fox_b_f, v_fox_q_norm, v_fox_k_norm, v_fox_w_out, v_hgrn_w_in, v_hgrn_lower_bounds, v_hgrn_g_norm, v_hgrn_w_out, v_ffn_w_in, v_ffn_w_out):
    given = dict(x=x, meta_tokens=meta_tokens, attn_norm=attn_norm, ffn_norm=ffn_norm, final_norm=final_norm, fox_w_in=fox_w_in, fox_b_f=fox_b_f, fox_q_norm=fox_q_norm, fox_k_norm=fox_k_norm, fox_w_out=fox_w_out, hgrn_w_in=hgrn_w_in, hgrn_lower_bounds=hgrn_lower_bounds, hgrn_g_norm=hgrn_g_norm, hgrn_w_out=hgrn_w_out, ffn_w_in=ffn_w_in, ffn_w_out=ffn_w_out, loss_target=loss_target, m_meta_tokens=m_meta_tokens, m_attn_norm=m_attn_norm, m_ffn_norm=m_ffn_norm, m_final_norm=m_final_norm, m_fox_w_in=m_fox_w_in, m_fox_b_f=m_fox_b_f, m_fox_q_norm=m_fox_q_norm, m_fox_k_norm=m_fox_k_norm, m_fox_w_out=m_fox_w_out, m_hgrn_w_in=m_hgrn_w_in, m_hgrn_lower_bounds=m_hgrn_lower_bounds, m_hgrn_g_norm=m_hgrn_g_norm, m_hgrn_w_out=m_hgrn_w_out, m_ffn_w_in=m_ffn_w_in, m_ffn_w_out=m_ffn_w_out, v_meta_tokens=v_meta_tokens, v_attn_norm=v_attn_norm, v_ffn_norm=v_ffn_norm, v_final_norm=v_final_norm, v_fox_w_in=v_fox_w_in, v_fox_b_f=v_fox_b_f, v_fox_q_norm=v_fox_q_norm, v_fox_k_norm=v_fox_k_norm, v_fox_w_out=v_fox_w_out, v_hgrn_w_in=v_hgrn_w_in, v_hgrn_lower_bounds=v_hgrn_lower_bounds, v_hgrn_g_norm=v_hgrn_g_norm, v_hgrn_w_out=v_hgrn_w_out, v_ffn_w_in=v_ffn_w_in, v_ffn_w_out=v_ffn_w_out)
    weights = {n: given[n] for n in TWIN_WEIGHTS}
    shared = {n: given[n] for n in SHARED_INPUTS}
    per_example = {n: given[n] for n in ['x']}
    grad_fn = _jax.value_and_grad(_loss, argnums=(0, 1))

    def one_microbatch(ex, loss_target):
        ex = dict(ex)
        diff = ex.pop(TWIN_DIFF_INPUT)
        return grad_fn(weights, diff, {**shared, **ex}, loss_target)

    if N_MICROBATCH == 1:
        loss, (grad_w, grad_x) = one_microbatch(per_example, given["loss_target"])
    else:
        def body(carry, xs):
            loss_sum, grad_sum = carry
            l_k, (gw_k, gx_k) = one_microbatch(xs[0], xs[1])
            with _jax.named_scope("update"):
                return (loss_sum + l_k, _jax.tree.map(_jnp.add, grad_sum, gw_k)), gx_k

        init = (_jnp.zeros((), _jnp.float32), _jax.tree.map(_jnp.zeros_like, weights))
        (loss, grad_w), grad_x = _jax.lax.scan(body, init, (per_example, given["loss_target"]))
    with _jax.named_scope("update"):
        delta_w, new_m, new_v = {}, {}, {}
        for n in TWIN_WEIGHTS:
            delta_w[n], new_m[n], new_v[n] = _adamw(weights[n], grad_w[n], given["m_" + n], given["v_" + n])
    return (loss, grad_x, *[grad_w[n] for n in TWIN_WEIGHTS], *[delta_w[n] for n in TWIN_WEIGHTS],
            *[new_m[n] for n in TWIN_WEIGHTS], *[new_v[n] for n in TWIN_WEIGHTS])
```

```python
import functools

import numpy as np
import jax
import jax.numpy as jnp
from jax import lax
from jax.experimental import pallas as pl
from jax.experimental.pallas import tpu as pltpu

F32, BF16 = jnp.float32, jnp.bfloat16
HIGHEST = lax.Precision.HIGHEST

D = 1024
N_META = 16
PAD = 128
ROW0 = PAD - N_META
FOX_H, FOX_DH = 16, 64
HG_H, HG_D = 8, 128
HG_C = 128
HG_LEV = 7
FFN = 2816
EPS = 1e-6
BIG = 1e30
LANES = 128
VMEM_LIMIT = 48 * 1024 * 1024
ROW_TILES = (640, 512, 384, 320, 256, 128, 64, 32, 16, 8)

ADAM_LR, ADAM_B1, ADAM_B2, ADAM_EPS, ADAM_WD, ADAM_STEP = 0.001, 0.9, 0.999, 1e-08, 0.01, 10

MESH = pl.DeviceIdType.MESH
ANY = pl.BlockSpec(memory_space=pl.ANY)
NT = (((1,), (1,)), ((), ()))
TN = (((0,), (0,)), ((), ()))


def _tile(n, cands=ROW_TILES, cap=None):
    for c in cands:
        if n % c == 0 and (cap is None or c <= cap):
            return c
    return n


def _cparams(sem):
    return pltpu.CompilerParams(dimension_semantics=sem, vmem_limit_bytes=VMEM_LIMIT)


def _sigmoid(x):
    return jax.nn.sigmoid(x)


def _log_sigmoid(x):
    return jnp.minimum(x, 0.0) - jnp.log(1.0 + jnp.exp(-jnp.abs(x)))


def _iota(shape, dim):
    return lax.broadcasted_iota(jnp.int32, shape, dim)


def _matmul(a, b, *, ta=False, tb=False, out_dtype=F32, add=None, name):
    if ta:
        kdim, m = a.shape
    else:
        m, kdim = a.shape
    n = b.shape[0] if tb else b.shape[1]
    if ta:
        tm = m if m <= 1024 else _tile(m, (1408, 1024, 512, 256, 128))
        tk = _tile(kdim)
    else:
        tm = _tile(m)
        tk = kdim if kdim <= 4096 else _tile(kdim, (2048, 1024, 512))
    tn = n if n <= 1024 else _tile(n, (1408, 1024, 512, 256, 128))
    nk = kdim // tk
    dn = (((0 if ta else 1,), (1 if tb else 0,)), ((), ()))

    def body(*refs):
        if add is None:
            a_ref, b_ref, o_ref, acc_ref = refs
        else:
            a_ref, b_ref, add_ref, o_ref, acc_ref = refs
        k = pl.program_id(2)

        @pl.when(k == 0)
        def _():
            acc_ref[...] = jnp.zeros_like(acc_ref)

        acc_ref[...] += lax.dot_general(a_ref[...].astype(BF16), b_ref[...].astype(BF16), dn,
                                        preferred_element_type=F32)

        @pl.when(k == nk - 1)
        def _():
            r = acc_ref[...]
            if add is not None:
                r = r + add_ref[...].astype(F32)
            o_ref[...] = r.astype(o_ref.dtype)

    a_spec = pl.BlockSpec((tk, tm), lambda j, i, k: (k, i)) if ta else pl.BlockSpec((tm, tk), lambda j, i, k: (i, k))
    b_spec = pl.BlockSpec((tn, tk), lambda j, i, k: (j, k)) if tb else pl.BlockSpec((tk, tn), lambda j, i, k: (k, j))
    o_spec = pl.BlockSpec((tm, tn), lambda j, i, k: (i, j))
    ins, specs = [a, b], [a_spec, b_spec]
    if add is not None:
        ins.append(add)
        specs.append(o_spec)
    return pl.pallas_call(
        body, name=name, grid=(n // tn, m // tm, nk), in_specs=specs, out_specs=o_spec,
        out_shape=jax.ShapeDtypeStruct((m, n), out_dtype),
        scratch_shapes=[pltpu.VMEM((tm, tn), F32)],
        compiler_params=_cparams(("parallel", "parallel", "arbitrary")),
    )(*ins)


def _rowwise(fn, ins, bcast, outs, accs, *, name, reverse=False, carry=None):
    rows = ins[0].shape[0]
    per_row = sum(x.shape[1] * x.dtype.itemsize for x in ins) + sum(c * jnp.dtype(d).itemsize for c, d in outs)
    tm = _tile(rows, cap=max(8, (10 * 1024 * 1024) // per_row))
    n = rows // tm
    n_in, n_b, n_o, n_a = len(ins), len(bcast), len(outs), len(accs)

    def body(*refs):
        in_refs = refs[:n_in]
        b_refs = refs[n_in:n_in + n_b]
        o_refs = refs[n_in + n_b:n_in + n_b + n_o]
        a_refs = refs[n_in + n_b + n_o:n_in + n_b + n_o + n_a]
        c_refs = refs[n_in + n_b + n_o + n_a:]
        i = pl.program_id(0)
        blk = (n - 1 - i) if reverse else i
        if c_refs:
            @pl.when(i == 0)
            def _():
                c_refs[0][...] = jnp.zeros_like(c_refs[0])
        args = ([r[...] for r in in_refs], [r[...] for r in b_refs])
        o_vals, a_vals = fn(blk * tm, *args, *c_refs)
        for r, v in zip(o_refs, o_vals):
            r[...] = v.astype(r.dtype)
        if n_a:
            @pl.when(i == 0)
            def _():
                for r in a_refs:
                    r[...] = jnp.zeros_like(r)
            for r, v in zip(a_refs, a_vals):
                r[...] += v

    def row_map(i):
        return ((n - 1 - i) if reverse else i, 0)

    in_specs = [pl.BlockSpec((tm, x.shape[1]), row_map) for x in ins]
    in_specs += [pl.BlockSpec(x.shape, lambda i, nd=x.ndim: (0,) * nd) for x in bcast]
    out_specs = [pl.BlockSpec((tm, c), row_map) for c, _ in outs]
    out_specs += [pl.BlockSpec(s, lambda i: (0, 0)) for s in accs]
    out_shape = [jax.ShapeDtypeStruct((rows, c), d) for c, d in outs]
    out_shape += [jax.ShapeDtypeStruct(s, F32) for s in accs]
    res = pl.pallas_call(
        body, name=name, grid=(n,), in_specs=in_specs, out_specs=out_specs, out_shape=out_shape,
        scratch_shapes=[pltpu.VMEM(carry, F32)] if carry else [],
        compiler_params=_cparams(("arbitrary",)),
    )(*ins, *bcast)
    return res[:n_o], res[n_o:]


def _row_ids(row0, tm):
    return row0 + _iota((tm, 1), 0)


def _rms_fwd(x, gain, name):
    def fn(row0, ins, bc):
        (xv,), (g,) = ins, bc
        r = lax.rsqrt(jnp.mean(xv * xv, axis=-1, keepdims=True) + EPS)
        return [xv * r * g], []
    return _rowwise(fn, [x], [gain], [(D, BF16)], [], name=name)[0][0]


def _rms_bwd(x, dxn, gain, dh_up, name):
    def fn(row0, ins, bc):
        xv, dy, up = ins
        (g,) = bc
        dy = dy.astype(F32)
        r = lax.rsqrt(jnp.mean(xv * xv, axis=-1, keepdims=True) + EPS)
        xh = xv * r
        dxh = dy * g
        dx = r * (dxh - xh * jnp.mean(dxh * xh, axis=-1, keepdims=True))
        keep = _row_ids(row0, xv.shape[0]) >= ROW0
        return [jnp.where(keep, up + dx, 0.0)], [jnp.sum(dy * xh, axis=0, keepdims=True)]
    (dh,), (dgain,) = _rowwise(fn, [x, dxn, dh_up], [gain], [(D, F32)], [(1, D)], name=name)
    return dh, dgain


def _loss_bwd(h, tgt, gain):
    def fn(row0, ins, bc):
        xv, t = ins
        (g,) = bc
        r = lax.rsqrt(jnp.mean(xv * xv, axis=-1, keepdims=True) + EPS)
        xh = xv * r
        keep = _row_ids(row0, xv.shape[0]) >= PAD
        err = jnp.where(keep, xh * g - t, 0.0)
        per_row = jnp.mean(err * err, axis=-1, keepdims=True)
        loss = 0.5 * jnp.sum(per_row, axis=0, keepdims=True)
        dy = err * (1.0 / D)
        dxh = dy * g
        dx = r * (dxh - xh * jnp.mean(dxh * xh, axis=-1, keepdims=True))
        return [dx], [jnp.broadcast_to(loss, (1, LANES)), jnp.sum(dy * xh, axis=0, keepdims=True)]
    (dh,), (loss, dgain) = _rowwise(fn, [h, tgt], [gain], [(D, F32)], [(1, LANES), (1, D)], name="loss_bwd")
    return loss[0, 0], dh, dgain


def _swiglu_fwd(g, u, name):
    def fn(row0, ins, bc):
        gv, uv = (v.astype(F32) for v in ins)
        return [gv * _sigmoid(gv) * uv], []
    return _rowwise(fn, [g, u], [], [(g.shape[1], BF16)], [], name=name)[0][0]


def _swiglu_bwd(dact, g, u, name):
    def fn(row0, ins, bc):
        da, gv, uv = (v.astype(F32) for v in ins)
        s = _sigmoid(gv)
        return [da * uv * (s * (1.0 + gv * (1.0 - s))), da * gv * s], []
    n = g.shape[1]
    return _rowwise(fn, [dact, g, u], [], [(n, BF16), (n, BF16)], [], name=name)[0]


def _adamw(w, g, m, v, name):
    def fn(row0, ins, bc):
        wv, gv, mv, vv = ins
        mn = ADAM_B1 * mv + (1.0 - ADAM_B1) * gv
        vn = ADAM_B2 * vv + (1.0 - ADAM_B2) * (gv * gv)
        m_hat = mn / (1.0 - ADAM_B1 ** ADAM_STEP)
        v_hat = vn / (1.0 - ADAM_B2 ** ADAM_STEP)
        delta = -ADAM_LR * (m_hat / (jnp.sqrt(v_hat) + ADAM_EPS) + ADAM_WD * wv)
        return [delta, mn, vn], []
    c = w.shape[1]
    return _rowwise(fn, [w, g, m, v], [], [(c, F32)] * 3, [], name=name)[0]


def _add(xs, name):
    def fn(row0, ins, bc):
        r = ins[0]
        for v in ins[1:]:
            r = r + v
        return [r], []
    return _rowwise(fn, list(xs), [], [(xs[0].shape[1], F32)], [], name=name)[0][0]


def _head_sum(x, gmat):
    hi = x.astype(BF16)
    lo = (x - hi.astype(F32)).astype(BF16)
    return jnp.dot(hi, gmat, preferred_element_type=F32) + jnp.dot(lo, gmat, preferred_element_type=F32)


def _fox_prep_fwd(q_raw, k_raw, qg, kg, gmat):
    def fn(row0, ins, bc):
        g_q, g_k, gm = bc
        outs = []
        for xv, g in zip(ins, (g_q, g_k)):
            r = lax.rsqrt(_head_sum(xv * xv, gm) * (1.0 / FOX_DH) + EPS)
            outs.append(xv * r * g)
        return outs, []
    return _rowwise(fn, [q_raw, k_raw], [qg, kg, gmat], [(D, BF16), (D, BF16)], [], name="fox_prep_fwd")[0]


def _fox_prep_bwd(q_raw, k_raw, dqn, dkn, qg, kg, gmat):
    def fn(row0, ins, bc):
        qv, kv, dq, dk = ins
        g_q, g_k, gm = bc
        outs, accs = [], []
        for xv, dy, g in ((qv, dq, g_q), (kv, dk, g_k)):
            r = lax.rsqrt(_head_sum(xv * xv, gm) * (1.0 / FOX_DH) + EPS)
            xh = xv * r
            dxh = dy * g
            outs.append(r * (dxh - xh * (_head_sum(dxh * xh, gm) * (1.0 / FOX_DH))))
            accs.append(jnp.sum(dy * xh, axis=0, keepdims=True))
        return outs, accs
    return _rowwise(fn, [q_raw, k_raw, dqn, dkn], [qg, kg, gmat], [(D, BF16), (D, BF16)],
                    [(1, D), (1, D)], name="fox_prep_bwd")


def _fox_cumsum_fwd(flog, bf):
    def fn(row0, ins, bc, carry):
        (f,), (b,) = ins, bc
        tm = f.shape[0]
        keep = _row_ids(row0, tm) >= ROW0
        lf = jnp.where(keep, _log_sigmoid(f + b), 0.0)
        tri = (_iota((tm, tm), 0) >= _iota((tm, tm), 1)).astype(F32)
        c = jnp.dot(tri, lf, precision=HIGHEST, preferred_element_type=F32) + carry[...]
        carry[...] = carry[...] + jnp.sum(lf, axis=0, keepdims=True)
        return [c, jnp.where(keep, c, BIG)], []
    return _rowwise(fn, [flog], [bf], [(LANES, F32), (LANES, F32)], [], name="fox_cumsum_fwd",
                    carry=(1, LANES))[0]


def _fox_cumsum_bwd(dc_q, dc_k, flog, bf):
    def fn(row0, ins, bc, carry):
        (dq, dk, f), (b,) = ins, bc
        d = dq + dk
        tm = f.shape[0]
        keep = _row_ids(row0, tm) >= ROW0
        triu = (_iota((tm, tm), 0) <= _iota((tm, tm), 1)).astype(F32)
        dlf = jnp.dot(triu, d, precision=HIGHEST, preferred_element_type=F32) + carry[...]
        carry[...] = carry[...] + jnp.sum(d, axis=0, keepdims=True)
        dfl = jnp.where(keep, dlf * _sigmoid(-(f + b)), 0.0)
        return [dfl], [jnp.sum(dfl, axis=0, keepdims=True)]
    (dflog,), (dbf,) = _rowwise(fn, [dc_q, dc_k, flog], [bf], [(LANES, F32)], [(1, LANES)], name="fox_cumsum_bwd",
                                reverse=True, carry=(1, LANES))
    return dflog, dbf


def _fox_gate_bwd(dog, o, gate):
    def fn(row0, ins, bc):
        dv, ov, gv = (v.astype(F32) for v in ins)
        s = _sigmoid(gv)
        return [dv * s, dv * ov * s * (1.0 - s)], []
    return _rowwise(fn, [dog, o, gate], [], [(D, BF16), (D, BF16)], [], name="fox_gate_bwd")[0]


def _head_masks():
    lane = _iota((1, LANES), 1)
    return [lane < FOX_DH, lane >= FOX_DH]


def _fox_attn_fwd(qn, kn, v, gate, cq, ck):
    L = qn.shape[0]
    t = _tile(L, (640, 512, 256, 128))
    n = L // t
    P = FOX_H // 2
    scale = FOX_DH ** -0.5

    def body(q_ref, k_ref, v_ref, g_ref, cq_ref, ck_ref, o_ref, og_ref, lse_ref, m_sc, l_sc, acc):
        i, j = pl.program_id(1), pl.program_id(2)

        @pl.when(j == 0)
        def _():
            m_sc[...] = jnp.full_like(m_sc, -BIG)
            l_sc[...] = jnp.zeros_like(l_sc)
            acc[...] = jnp.zeros_like(acc)

        @pl.when(j <= i)
        def _():
            q, k, vv = q_ref[...], k_ref[...], v_ref[...]
            ok = (j < i) | (_iota((t, t), 1) <= _iota((t, t), 0))
            masks = _head_masks()
            alphas, pvs = [], []
            for hh, msk in enumerate(masks):
                s = lax.dot_general(jnp.where(msk, q, 0), k, NT, preferred_element_type=F32) * scale
                s = s + cq_ref[0, :, hh:hh + 1] - ck_ref[0, hh:hh + 1, :]
                s = jnp.where(ok, s, -jnp.inf)
                m_old = m_sc[hh]
                m_new = jnp.maximum(m_old, jnp.max(s, axis=-1, keepdims=True))
                a = jnp.exp(m_old - m_new)
                p = jnp.exp(s - m_new)
                l_sc[hh] = a * l_sc[hh] + jnp.sum(p, axis=-1, keepdims=True)
                m_sc[hh] = m_new
                alphas.append(a)
                pvs.append(jnp.dot(p.astype(BF16), jnp.where(msk, vv, 0), preferred_element_type=F32))
            acc[...] = acc[...] * jnp.where(masks[0], alphas[0], alphas[1]) + pvs[0] + pvs[1]

        @pl.when(j == n - 1)
        def _():
            masks = _head_masks()
            o = acc[...] / jnp.where(masks[0], l_sc[0], l_sc[1])
            o_ref[...] = o.astype(o_ref.dtype)
            og_ref[...] = (o * _sigmoid(g_ref[...].astype(F32))).astype(og_ref.dtype)
            two = _iota((1, 2), 1)
            lse_ref[0] = jnp.where(two == 0, m_sc[0] + jnp.log(l_sc[0]), m_sc[1] + jnp.log(l_sc[1]))

    qspec = pl.BlockSpec((t, LANES), lambda p, i, j: (i, p))
    kspec = pl.BlockSpec((t, LANES), lambda p, i, j: (jnp.minimum(j, i), p))
    return pl.pallas_call(
        body, name="fox_attn_fwd", grid=(P, n, n),
        in_specs=[qspec, kspec, kspec, qspec,
                  pl.BlockSpec((1, t, 2), lambda p, i, j: (p, i, 0)),
                  pl.BlockSpec((1, 2, t), lambda p, i, j: (p, 0, jnp.minimum(j, i)))],
        out_specs=[qspec, qspec, pl.BlockSpec((1, t, 2), lambda p, i, j: (p, i, 0))],
        out_shape=[jax.ShapeDtypeStruct((L, D), BF16), jax.ShapeDtypeStruct((L, D), BF16),
                   jax.ShapeDtypeStruct((P, L, 2), F32)],
        scratch_shapes=[pltpu.VMEM((2, t, 1), F32), pltpu.VMEM((2, t, 1), F32), pltpu.VMEM((t, LANES), F32)],
        compiler_params=_cparams(("parallel", "parallel", "arbitrary")),
    )(qn, kn, v, gate, cq, ck)


def _fox_attn_bwd(qn, kn, v, o, do, lse, cq, ck):
    L = qn.shape[0]
    t = _tile(L, (640, 512, 256, 128))
    n = L // t
    P = FOX_H // 2
    scale = FOX_DH ** -0.5

    def body(q_ref, k_ref, v_ref, o_ref, do_ref, lse_ref, cq_ref, ck_ref,
             dq_ref, dk_ref, dv_ref, dcq_ref, dck_ref, dk_acc, dv_acc, dck_acc):
        j, i = pl.program_id(1), pl.program_id(2)

        @pl.when((j == 0) & (i == 0))
        def _():
            dq_ref[...] = jnp.zeros_like(dq_ref)
            dcq_ref[...] = jnp.zeros_like(dcq_ref)

        @pl.when(i == 0)
        def _():
            dk_acc[...] = jnp.zeros_like(dk_acc)
            dv_acc[...] = jnp.zeros_like(dv_acc)
            dck_acc[...] = jnp.zeros_like(dck_acc)

        @pl.when(i >= j)
        def _():
            q, k, vv = q_ref[...], k_ref[...], v_ref[...]
            dov = do_ref[...]
            od = o_ref[...].astype(F32) * dov.astype(F32)
            ok = (j < i) | (_iota((t, t), 1) <= _iota((t, t), 0))
            dq_new = jnp.zeros((t, LANES), F32)
            row_sums = []
            for hh, msk in enumerate(_head_masks()):
                qh, kh = jnp.where(msk, q, 0), jnp.where(msk, k, 0)
                doh = jnp.where(msk, dov, 0)
                s = lax.dot_general(qh, k, NT, preferred_element_type=F32) * scale
                s = s + cq_ref[0, :, hh:hh + 1] - ck_ref[0, hh:hh + 1, :]
                s = jnp.where(ok, s, -jnp.inf)
                p = jnp.exp(s - lse_ref[0, :, hh:hh + 1])
                dp = lax.dot_general(doh, vv, NT, preferred_element_type=F32)
                delta = jnp.sum(jnp.where(msk, od, 0.0), axis=-1, keepdims=True)
                ds = p * (dp - delta)
                dsb = ds.astype(BF16)
                dv_acc[...] += lax.dot_general(p.astype(BF16), doh, TN, preferred_element_type=F32)
                dk_acc[...] += lax.dot_general(dsb, qh, TN, preferred_element_type=F32) * scale
                dq_new = dq_new + jnp.dot(dsb, kh, preferred_element_type=F32) * scale
                dck_acc[hh] = dck_acc[hh] - jnp.sum(ds, axis=0, keepdims=True)
                row_sums.append(jnp.sum(ds, axis=-1, keepdims=True))
            rows = pl.ds(pl.multiple_of(i * t, LANES), t)
            dq_ref[rows, :] += dq_new
            dcq_ref[0, rows, :] += jnp.where(_iota((1, 2), 1) == 0, row_sums[0], row_sums[1])

        @pl.when(i == n - 1)
        def _():
            dk_ref[...] = dk_acc[...]
            dv_ref[...] = dv_acc[...]
            two = _iota((2, 1), 0)
            dck_ref[0] = jnp.where(two == 0, dck_acc[0], dck_acc[1])

    qspec = pl.BlockSpec((t, LANES), lambda p, j, i: (jnp.maximum(i, j), p))
    kspec = pl.BlockSpec((t, LANES), lambda p, j, i: (j, p))
    cqspec = pl.BlockSpec((1, t, 2), lambda p, j, i: (p, jnp.maximum(i, j), 0))
    ckspec = pl.BlockSpec((1, 2, t), lambda p, j, i: (p, 0, j))
    return pl.pallas_call(
        body, name="fox_attn_bwd", grid=(P, n, n),
        in_specs=[qspec, kspec, kspec, qspec, qspec, cqspec, cqspec, ckspec],
        out_specs=[pl.BlockSpec((L, LANES), lambda p, j, i: (0, p)), kspec, kspec,
                   pl.BlockSpec((1, L, 2), lambda p, j, i: (p, 0, 0)), ckspec],
        out_shape=[jax.ShapeDtypeStruct((L, D), F32), jax.ShapeDtypeStruct((L, D), F32),
                   jax.ShapeDtypeStruct((L, D), F32), jax.ShapeDtypeStruct((P, L, 2), F32),
                   jax.ShapeDtypeStruct((P, 2, L), F32)],
        scratch_shapes=[pltpu.VMEM((t, LANES), F32), pltpu.VMEM((t, LANES), F32), pltpu.VMEM((2, 1, t), F32)],
        compiler_params=_cparams(("parallel", "arbitrary", "arbitrary")),
    )(qn, kn, v, o, do, lse, cq, ck)


def _hgrn_consts():
    C = HG_C
    r = np.arange(C)[:, None]
    j = np.arange(C)[None, :]
    mats = [j <= r, j > r]
    masks = []
    n = C
    while n >= 2:
        half = n // 2
        mid = (r // n) * n + half - 1
        second = (r % n) >= half
        mats.append(np.where(second, (j > mid) & (j <= r), (j > r) & (j <= mid)))
        masks.append(((r // n) == (j // n)) & ((r % n) >= half) & ((j % n) < half))
        n //= 2
    return (jnp.asarray(np.concatenate(mats, 0).astype(np.float32), BF16),
            jnp.asarray(np.stack(masks).astype(np.float32), F32))


def _hg_pre(hq, hz, hlb_ref):
    h0, h1 = hlb_ref[0:1, :], hlb_ref[1:2, :]
    mx = jnp.maximum(h0, h1)
    e0, e1 = jnp.exp(h0 - mx), jnp.exp(h1 - mx)
    lb = e1 / (e0 + e1)
    sq = _sigmoid(hq)
    sz, snz = _sigmoid(hz), _sigmoid(-hz)
    k = (1.0 - lb) * snz
    a = jnp.log(lb)
    b = jnp.log1p(-lb) + _log_sigmoid(hz)
    g = jnp.maximum(a, b) + jnp.log(1.0 + jnp.exp(-jnp.abs(a - b)))
    return lb, hq * sq, sq, k, sz, snz, g


def _hg_decays(g, rmat):
    hi = g.astype(BF16)
    lo = (g - hi.astype(F32)).astype(BF16)
    d = jnp.dot(rmat, jnp.concatenate([hi, lo], axis=1), preferred_element_type=F32)
    return jnp.exp(d[:, :HG_D] + d[:, HG_D:])


def _hg_intra(q, k, fall, masks):
    C = HG_C
    eye = _iota((C, C), 0) == _iota((C, C), 1)
    a = jnp.where(eye, jnp.sum(q * k, axis=-1, keepdims=True), 0.0)
    for l in range(HG_LEV):
        f = fall[(2 + l) * C:(3 + l) * C]
        a = a + masks[l] * lax.dot_general((q * f).astype(BF16), (k * f).astype(BF16), NT,
                                           preferred_element_type=F32)
    return a


def _hgrn_specs(n_chunks, reverse):
    C = HG_C

    def col(off):
        if reverse:
            return pl.BlockSpec((C, HG_D), lambda h, c: (n_chunks - 1 - c, off + h))
        return pl.BlockSpec((C, HG_D), lambda h, c: (c, off + h))

    st = pl.BlockSpec((1, 1, HG_D, HG_D),
                      (lambda h, c: (h, n_chunks - 1 - c, 0, 0)) if reverse else (lambda h, c: (h, c, 0, 0)))
    consts = [pl.BlockSpec((2, HG_D), lambda h, c: (0, h)), pl.BlockSpec((1, HG_D), lambda h, c: (0, 0)),
              pl.BlockSpec(((2 + HG_LEV) * C, C), lambda h, c: (0, 0)),
              pl.BlockSpec((HG_LEV, C, C), lambda h, c: (0, 0, 0))]
    return col, st, consts


def _hgrn_fwd(proj, hlb, gg, rmat, masks):
    L = proj.shape[0]
    C = HG_C
    nc = L // C
    col, st, consts = _hgrn_specs(nc, False)

    def body(hq_ref, hz_ref, hi_ref, hg_ref, hlb_ref, gg_ref, r_ref, m_ref, og_ref, st_ref, state):
        c = pl.program_id(1)

        @pl.when(c == 0)
        def _():
            state[...] = jnp.zeros_like(state)

        v, hg = hi_ref[...], hg_ref[...]
        _, q, _, k, _, _, g = _hg_pre(hq_ref[...], hz_ref[...], hlb_ref)
        fall = _hg_decays(g, r_ref[...])
        fb, fe = fall[0:C], fall[C:2 * C]
        st0 = state[...]
        st_ref[0, 0] = st0
        a = _hg_intra(q, k, fall, m_ref[...])
        vb = v.astype(BF16)
        o = jnp.dot(a.astype(BF16), vb, preferred_element_type=F32)
        o = o + lax.dot_general((q * fb).astype(BF16), st0.astype(BF16), NT, preferred_element_type=F32)
        ebc = jnp.exp(jnp.sum(g, axis=0, keepdims=True))
        state[...] = st0 * ebc + lax.dot_general(vb, (k * fe).astype(BF16), TN, preferred_element_type=F32)
        r = lax.rsqrt(jnp.mean(o * o, axis=-1, keepdims=True) + EPS)
        og_ref[...] = (o * r * gg_ref[...] * (hg * _sigmoid(hg))).astype(og_ref.dtype)

    return pl.pallas_call(
        body, name="hgrn_fwd", grid=(HG_H, nc),
        in_specs=[col(0), col(HG_H), col(2 * HG_H), col(3 * HG_H)] + consts,
        out_specs=[col(0), st],
        out_shape=[jax.ShapeDtypeStruct((L, D), BF16), jax.ShapeDtypeStruct((HG_H, nc, HG_D, HG_D), F32)],
        scratch_shapes=[pltpu.VMEM((HG_D, HG_D), F32)],
        compiler_params=_cparams(("parallel", "arbitrary")),
    )(proj, proj, proj, proj, hlb, gg, rmat, masks)


def _hgrn_bwd(proj, dog, states, hlb, gg, rmat, masks):
    L = proj.shape[0]
    C = HG_C
    nc = L // C
    col, st, consts = _hgrn_specs(nc, True)

    def body(hq_ref, hz_ref, hi_ref, hg_ref, do_ref, hlb_ref, gg_ref, r_ref, m_ref, st_ref,
             dq_ref, dz_ref, di_ref, dg_ref, dlb_ref, dgg_ref, dstate):
        c = pl.program_id(1)

        @pl.when(c == 0)
        def _():
            dstate[...] = jnp.zeros_like(dstate)
            dlb_ref[...] = jnp.zeros_like(dlb_ref)
            dgg_ref[...] = jnp.zeros_like(dgg_ref)

        hq, hz, v, hg = hq_ref[...], hz_ref[...], hi_ref[...], hg_ref[...]
        dout = do_ref[...].astype(F32)
        gain = gg_ref[...]
        masks_v = m_ref[...]
        lb, q, sq, k, sz, snz, g = _hg_pre(hq, hz, hlb_ref)
        fall = _hg_decays(g, r_ref[...])
        fb, fe = fall[0:C], fall[C:2 * C]
        a = _hg_intra(q, k, fall, masks_v)
        st0 = st_ref[0, 0]
        st0b = st0.astype(BF16)
        ebc = jnp.exp(jnp.sum(g, axis=0, keepdims=True))
        qb, ke, vb = (q * fb).astype(BF16), (k * fe).astype(BF16), v.astype(BF16)
        ab = a.astype(BF16)
        o = jnp.dot(ab, vb, preferred_element_type=F32) + lax.dot_general(qb, st0b, NT, preferred_element_type=F32)
        r = lax.rsqrt(jnp.mean(o * o, axis=-1, keepdims=True) + EPS)
        oh = o * r
        sg = _sigmoid(hg)
        d_on = dout * (hg * sg)
        dhg = dout * (oh * gain) * (sg * (1.0 + hg * (1.0 - sg)))
        dgg_ref[0] += jnp.sum(d_on * oh, axis=0, keepdims=True)
        dxh = d_on * gain
        do = r * (dxh - oh * jnp.mean(dxh * oh, axis=-1, keepdims=True))
        dob = do.astype(BF16)
        dsp = dstate[...]
        dspb = dsp.astype(BF16)
        causal = _iota((C, C), 0) >= _iota((C, C), 1)
        da = jnp.where(causal, lax.dot_general(dob, vb, NT, preferred_element_type=F32), 0.0)
        diag = jnp.sum(do * v, axis=-1, keepdims=True)
        dv = lax.dot_general(ab, dob, TN, preferred_element_type=F32)
        dv = dv + lax.dot_general(ke, dspb, NT, preferred_element_type=F32)
        xq = jnp.dot(dob, st0b, preferred_element_type=F32)
        xk = jnp.dot(vb, dspb, preferred_element_type=F32)
        dq = diag * k + fb * xq
        dk = diag * q + fe * xk
        ke_xk = ke.astype(F32) * xk
        db = qb.astype(F32) * xq - ke_xk
        for l in range(HG_LEV):
            f = fall[(2 + l) * C:(3 + l) * C]
            dal = (masks_v[l] * da).astype(BF16)
            ql, kl = (q * f).astype(BF16), (k * f).astype(BF16)
            xq = jnp.dot(dal, kl, preferred_element_type=F32)
            xk = lax.dot_general(dal, ql, TN, preferred_element_type=F32)
            dq = dq + f * xq
            dk = dk + f * xk
            db = db + ql.astype(F32) * xq - kl.astype(F32) * xk
        dstate[...] = dsp * ebc + lax.dot_general(dob, qb, TN, preferred_element_type=F32)
        triu = (_iota((C, C), 0) <= _iota((C, C), 1)).astype(F32)
        dg = jnp.dot(triu, db, precision=HIGHEST, preferred_element_type=F32)
        dg = dg + jnp.sum(st0 * ebc * dsp, axis=0, keepdims=True) + jnp.sum(ke_xk, axis=0, keepdims=True)
        keep = _row_ids((nc - 1 - c) * C, C) >= ROW0
        dg = jnp.where(keep, dg, 0.0)
        dk = jnp.where(keep, dk, 0.0)
        f_gate = lb + (1.0 - lb) * sz
        dfdz = (1.0 - lb) * sz * snz
        dz_ref[...] = (dg * dfdz / f_gate - dk * dfdz).astype(dz_ref.dtype)
        dlb_ref[...] += jnp.sum(dg * snz / f_gate - dk * snz, axis=0, keepdims=True)
        dq_ref[...] = jnp.where(keep, dq * (sq * (1.0 + hq * (1.0 - sq))), 0.0).astype(dq_ref.dtype)
        di_ref[...] = jnp.where(keep, dv, 0.0).astype(di_ref.dtype)
        dg_ref[...] = jnp.where(keep, dhg, 0.0).astype(dg_ref.dtype)

    outs = pl.pallas_call(
        body, name="hgrn_bwd", grid=(HG_H, nc),
        in_specs=[col(0), col(HG_H), col(2 * HG_H), col(3 * HG_H), col(0)] + consts + [st],
        out_specs=[col(0), col(0), col(0), col(0), pl.BlockSpec((1, HG_D), lambda h, c: (0, h)),
                   pl.BlockSpec((1, 1, HG_D), lambda h, c: (h, 0, 0))],
        out_shape=[jax.ShapeDtypeStruct((L, D), BF16)] * 4 + [jax.ShapeDtypeStruct((1, D), F32),
                                                              jax.ShapeDtypeStruct((HG_H, 1, HG_D), F32)],
        scratch_shapes=[pltpu.VMEM((HG_D, HG_D), F32)],
        compiler_params=_cparams(("parallel", "arbitrary")),
    )(proj, proj, proj, proj, dog, hlb, gg, rmat, masks, states)
    return outs


def _ffn_fwd(h, norm_gain, wg, wu, wo, tag):
    hn = _rms_fwd(h, norm_gain, f"{tag}_norm")
    g = _matmul(hn, wg, out_dtype=BF16, name=f"{tag}_gate")
    u = _matmul(hn, wu, out_dtype=BF16, name=f"{tag}_up")
    act = _swiglu_fwd(g, u, f"{tag}_act")
    h_out = _matmul(act, wo, add=h, name=f"{tag}_out")
    return h_out, (h, hn, g, u, act)


def _ffn_bwd(dh, saved, norm_gain, wg, wu, wo, tag):
    h, hn, g, u, act = saved
    dact = _matmul(dh, wo, tb=True, out_dtype=BF16, name=f"{tag}_dact")
    d_wo = _matmul(act, dh, ta=True, name=f"{tag}_dwo")
    dg, du = _swiglu_bwd(dact, g, u, f"{tag}_dact_bwd")
    dhn = _matmul(dg, wg, tb=True, name=f"{tag}_dhn_g")
    dhn = _matmul(du, wu, tb=True, add=dhn, name=f"{tag}_dhn_u")
    d_wg = _matmul(hn, dg, ta=True, name=f"{tag}_dwg")
    d_wu = _matmul(hn, du, ta=True, name=f"{tag}_dwu")
    dh, d_gain = _rms_bwd(h, dhn, norm_gain, dh, f"{tag}_norm_bwd")
    return dh, d_gain, jnp.concatenate([d_wg, d_wu], axis=1), d_wo


def _local_step(h0, tgt, w):
    L = h0.shape[0]
    P = FOX_H // 2
    gmat = jnp.asarray(np.kron(np.eye(FOX_H), np.ones((FOX_DH, FOX_DH))).astype(np.float32), BF16)
    rmat, lmasks = _hgrn_consts()
    an, fn_ = w["attn_norm"], w["ffn_norm"]
    qg = jnp.tile(w["fox_q_norm"], (1, FOX_H))
    kg = jnp.tile(w["fox_k_norm"], (1, FOX_H))
    bf = jnp.pad(w["fox_b_f"], ((0, 0), (0, LANES - FOX_H)))
    fw = w["fox_w_in"]
    f_wq, f_wk, f_wv, f_wg = (fw[:, i * D:(i + 1) * D] for i in range(4))
    f_wf = jnp.pad(fw[:, 4 * D:], ((0, 0), (0, LANES - FOX_H)))
    f_wo = w["fox_w_out"]
    h_wi, h_wo = w["hgrn_w_in"], w["hgrn_w_out"]
    ffw = [(w["ffn_w_in"][i][:, :FFN], w["ffn_w_in"][i][:, FFN:], w["ffn_w_out"][i]) for i in range(2)]

    hn0 = _rms_fwd(h0, an[0:1], "fox_norm")
    q_raw = _matmul(hn0, f_wq, name="fox_q")
    k_raw = _matmul(hn0, f_wk, name="fox_k")
    v = _matmul(hn0, f_wv, out_dtype=BF16, name="fox_v")
    gate = _matmul(hn0, f_wg, out_dtype=BF16, name="fox_gate")
    flog = _matmul(hn0, f_wf, name="fox_flog")
    qn, kn = _fox_prep_fwd(q_raw, k_raw, qg, kg, gmat)
    cq, ck = _fox_cumsum_fwd(flog, bf)
    cq3 = cq[:, :FOX_H].reshape(L, P, 2).transpose(1, 0, 2)
    ck3 = ck[:, :FOX_H].reshape(L, P, 2).transpose(1, 2, 0)
    o, og, lse = _fox_attn_fwd(qn, kn, v, gate, cq3, ck3)
    h1 = _matmul(og, f_wo, add=h0, name="fox_out")
    h2, ffn0 = _ffn_fwd(h1, fn_[0:1], *ffw[0], "ffn0")

    hn2 = _rms_fwd(h2, an[1:2], "hgrn_norm")
    proj = _matmul(hn2, h_wi, name="hgrn_in")
    og1, states = _hgrn_fwd(proj, w["hgrn_lower_bounds"], w["hgrn_g_norm"], rmat, lmasks)
    h3 = _matmul(og1, h_wo, add=h2, name="hgrn_out")
    h4, ffn1 = _ffn_fwd(h3, fn_[1:2], *ffw[1], "ffn1")

    loss, dh, d_final = _loss_bwd(h4, tgt, w["final_norm"])

    dh, d_fn1, d_ffn_in1, d_ffn_out1 = _ffn_bwd(dh, ffn1, fn_[1:2], *ffw[1], "ffn1")
    dog1 = _matmul(dh, h_wo, tb=True, out_dtype=BF16, name="hgrn_dog")
    d_h_wo = _matmul(og1, dh, ta=True, name="hgrn_dwo")
    dpq, dpz, dpi, dpg, d_lb, d_gg = _hgrn_bwd(proj, dog1, states, w["hgrn_lower_bounds"], w["hgrn_g_norm"],
                                               rmat, lmasks)
    dproj = jnp.concatenate([dpq, dpz, dpi, dpg], axis=1)
    dhn2 = _matmul(dproj, h_wi, tb=True, name="hgrn_dhn")
    d_h_wi = _matmul(hn2, dproj, ta=True, name="hgrn_dwi")
    dh, d_an1 = _rms_bwd(h2, dhn2, an[1:2], dh, "hgrn_norm_bwd")

    dh, d_fn0, d_ffn_in0, d_ffn_out0 = _ffn_bwd(dh, ffn0, fn_[0:1], *ffw[0], "ffn0")
    dog = _matmul(dh, f_wo, tb=True, out_dtype=BF16, name="fox_dog")
    d_f_wo = _matmul(og, dh, ta=True, name="fox_dwo")
    do, dgate = _fox_gate_bwd(dog, o, gate)
    dqn, dkn, dv, dcq, dck = _fox_attn_bwd(qn, kn, v, o, do, lse, cq3, ck3)
    (dq_raw, dk_raw), (d_qg, d_kg) = _fox_prep_bwd(q_raw, k_raw, dqn, dkn, qg, kg, gmat)
    dc_q = jnp.pad(dcq.transpose(1, 0, 2).reshape(L, FOX_H), ((0, 0), (0, LANES - FOX_H)))
    dc_k = jnp.pad(dck.transpose(2, 0, 1).reshape(L, FOX_H), ((0, 0), (0, LANES - FOX_H)))
    dflog, d_bf = _fox_cumsum_bwd(dc_q, dc_k, flog, bf)
    dproj0 = jnp.concatenate([dq_raw, dk_raw, dv.astype(BF16), dgate, dflog.astype(BF16)], axis=1)
    f_wall = jnp.concatenate([f_wq, f_wk, f_wv, f_wg, f_wf], axis=1)
    dhn0 = _matmul(dproj0, f_wall, tb=True, name="fox_dhn")
    d_f_wall = _matmul(hn0, dproj0, ta=True, name="fox_dwi")
    d_f_wi = d_f_wall[:, :4 * D + FOX_H]
    dh, d_an0 = _rms_bwd(h0, dhn0, an[0:1], dh, "fox_norm_bwd")

    big = dict(fox_w_in=d_f_wi, fox_w_out=d_f_wo, hgrn_w_in=d_h_wi, hgrn_w_out=d_h_wo,
               ffn_w_in=jnp.stack([d_ffn_in0, d_ffn_in1]), ffn_w_out=jnp.stack([d_ffn_out0, d_ffn_out1]))
    small = dict(attn_norm=jnp.concatenate([d_an0, d_an1]), ffn_norm=jnp.concatenate([d_fn0, d_fn1]),
                 final_norm=d_final, lb_raw=d_lb, q_gain=d_qg, k_gain=d_kg, b_f=d_bf,
                 g_gain=d_gg.reshape(1, D))
    return loss, dh, big, small


def _me():
    return lax.axis_index("x"), lax.axis_index("y"), lax.axis_index("c")


def _flip(v, bit):
    return 1 - v if bit else v


def _chip_allgather(arrs):
    n = len(arrs)

    def body(*refs):
        ins, outs = refs[:n], refs[n:2 * n]
        ssem, rsem, lsem = refs[2 * n:]
        x, y, c = _me()
        peers = [(1 - x, y, c), (x, 1 - y, c), (1 - x, 1 - y, c)]
        local, sends = [], []
        for a in range(n):
            cp = pltpu.make_async_copy(ins[a], outs[a].at[2 * x + y], lsem.at[a])
            cp.start()
            local.append(cp)
            for k, peer in enumerate(peers):
                cp = pltpu.make_async_remote_copy(ins[a], outs[a].at[2 * x + y], ssem.at[a, k], rsem.at[a, k],
                                                  device_id=peer, device_id_type=MESH)
                cp.start()
                sends.append(cp)
        for cp in local:
            cp.wait()
        for cp in sends:
            cp.wait_send()
        for a in range(n):
            for k, peer in enumerate(peers):
                pltpu.make_async_remote_copy(ins[a], outs[a].at[2 * peer[0] + peer[1]], ssem.at[a, k], rsem.at[a, k],
                                             device_id=peer, device_id_type=MESH).wait_recv()

    return pl.pallas_call(
        body, name="chip_allgather", in_specs=[ANY] * n, out_specs=[ANY] * n,
        out_shape=[jax.ShapeDtypeStruct((4,) + a.shape, a.dtype) for a in arrs],
        scratch_shapes=[pltpu.SemaphoreType.DMA((n, 3)), pltpu.SemaphoreType.DMA((n, 3)),
                        pltpu.SemaphoreType.DMA((n,))],
    )(*arrs)


def _device_allgather(arr):
    def body(in_ref, out_ref, ssem, rsem, lsem):
        x, y, c = _me()
        me = 4 * x + 2 * y + c
        peers = [(_flip(x, k & 4), _flip(y, k & 2), _flip(c, k & 1)) for k in range(1, 8)]
        local = pltpu.make_async_copy(in_ref, out_ref.at[me], lsem)
        local.start()
        sends = []
        for k, peer in enumerate(peers):
            cp = pltpu.make_async_remote_copy(in_ref, out_ref.at[me], ssem.at[k], rsem.at[k],
                                              device_id=peer, device_id_type=MESH)
            cp.start()
            sends.append(cp)
        local.wait()
        for cp in sends:
            cp.wait_send()
        for k, peer in enumerate(peers):
            pltpu.make_async_remote_copy(in_ref, out_ref.at[4 * peer[0] + 2 * peer[1] + peer[2]], ssem.at[k],
                                         rsem.at[k], device_id=peer, device_id_type=MESH).wait_recv()

    return pl.pallas_call(
        body, name="device_allgather", in_specs=[ANY], out_specs=ANY,
        out_shape=jax.ShapeDtypeStruct((8,) + arr.shape, arr.dtype),
        scratch_shapes=[pltpu.SemaphoreType.DMA((7,)), pltpu.SemaphoreType.DMA((7,)), pltpu.SemaphoreType.DMA],
    )(arr)


def _sibling_send_other_half(arrs):
    n = len(arrs)

    def body(*refs):
        ins, outs = refs[:n], refs[n:2 * n]
        ssem, rsem = refs[2 * n:]
        x, y, c = _me()
        cps = []
        for a in range(n):
            half = ins[a].shape[1] // 2
            src = ins[a].at[:, pl.ds((1 - c) * half, half), :]
            cp = pltpu.make_async_remote_copy(src, outs[a], ssem.at[a], rsem.at[a],
                                              device_id=(x, y, 1 - c), device_id_type=MESH)
            cp.start()
            cps.append(cp)
        for cp in cps:
            cp.wait()

    return pl.pallas_call(
        body, name="grad_sibling_swap", in_specs=[ANY] * n, out_specs=[ANY] * n,
        out_shape=[jax.ShapeDtypeStruct((4, a.shape[1] // 2, a.shape[2]), a.dtype) for a in arrs],
        scratch_shapes=[pltpu.SemaphoreType.DMA((n,)), pltpu.SemaphoreType.DMA((n,))],
    )(*arrs)


def _chip_scatter(arrs):
    n = len(arrs)

    def body(*refs):
        ins, outs = refs[:n], refs[n:2 * n]
        ssem, rsem = refs[2 * n:]
        x, y, c = _me()
        peers = [(1 - x, y, c), (x, 1 - y, c), (1 - x, 1 - y, c)]
        cps = []
        for a in range(n):
            for k, peer in enumerate(peers):
                cp = pltpu.make_async_remote_copy(ins[a].at[2 * peer[0] + peer[1]], outs[a].at[k], ssem.at[a, k],
                                                  rsem.at[a, k], device_id=peer, device_id_type=MESH)
                cp.start()
                cps.append(cp)
        for cp in cps:
            cp.wait()

    return pl.pallas_call(
        body, name="grad_chip_scatter", in_specs=[ANY] * n, out_specs=[ANY] * n,
        out_shape=[jax.ShapeDtypeStruct((3,) + a.shape[1:], a.dtype) for a in arrs],
        scratch_shapes=[pltpu.SemaphoreType.DMA((n, 3)), pltpu.SemaphoreType.DMA((n, 3))],
    )(*arrs)


def _sibling_allgather(arrs):
    n = len(arrs)

    def body(*refs):
        ins, outs = refs[:n], refs[n:2 * n]
        ssem, rsem, lsem = refs[2 * n:]
        x, y, c = _me()
        cps = []
        for a in range(n):
            cp = pltpu.make_async_copy(ins[a], outs[a].at[c], lsem.at[a])
            cp.start()
            cps.append(cp)
            cp = pltpu.make_async_remote_copy(ins[a], outs[a].at[c], ssem.at[a], rsem.at[a],
                                              device_id=(x, y, 1 - c), device_id_type=MESH)
            cp.start()
            cps.append(cp)
        for cp in cps:
            cp.wait()

    return pl.pallas_call(
        body, name="grad_sibling_allgather", in_specs=[ANY] * n, out_specs=[ANY] * n,
        out_shape=[jax.ShapeDtypeStruct((2,) + a.shape, a.dtype) for a in arrs],
        scratch_shapes=[pltpu.SemaphoreType.DMA((n,)), pltpu.SemaphoreType.DMA((n,)),
                        pltpu.SemaphoreType.DMA((n,))],
    )(*arrs)


def _reduce_scatter(grads):
    x, y, c = _me()
    got = _sibling_send_other_half(grads)
    pair = []
    for i, (g, t) in enumerate(zip(grads, got)):
        half = g.shape[1] // 2
        mine = lax.dynamic_slice_in_dim(g, c * half, half, axis=1)
        pair.append(_add([mine.reshape(4 * half, -1), t.reshape(4 * half, -1)], f"grad_pair_add{i}").reshape(t.shape))
    recv = _chip_scatter(pair)
    halves = []
    for i, (p, r) in enumerate(zip(pair, recv)):
        own = lax.dynamic_index_in_dim(p, 2 * x + y, axis=0, keepdims=False)
        halves.append(_add([own, r[0], r[1], r[2]], f"grad_chip_add{i}"))
    full = _sibling_allgather(halves)
    return [f.reshape(2 * f.shape[1], f.shape[2]) for f in full]


SMALL_ROWS = 32


def _small_finalize(gathered, hlb, fold64, fold128):
    def body(g_ref, hlb_ref, f64_ref, f128_ref, rows_ref, qk_ref, gg_ref, lb_ref):
        tot = g_ref[0]
        for d in range(1, 8):
            tot = tot + g_ref[d]
        rows_ref[...] = tot
        qk_ref[...] = jnp.dot(rows_ref[6:8, :], f64_ref[...], precision=HIGHEST, preferred_element_type=F32)
        gg_ref[...] = jnp.dot(rows_ref[9:10, :], f128_ref[...], precision=HIGHEST, preferred_element_type=F32)
        h0, h1 = hlb_ref[0:1, :], hlb_ref[1:2, :]
        mx = jnp.maximum(h0, h1)
        e0, e1 = jnp.exp(h0 - mx), jnp.exp(h1 - mx)
        lb = e1 / (e0 + e1)
        d1 = rows_ref[5:6, :] * lb * (1.0 - lb)
        lb_ref[...] = jnp.where(_iota((2, 1), 0) == 0, -d1, d1)

    return pl.pallas_call(
        body, name="small_finalize",
        out_shape=[jax.ShapeDtypeStruct((SMALL_ROWS, D), F32), jax.ShapeDtypeStruct((2, FOX_DH), F32),
                   jax.ShapeDtypeStruct((1, HG_D), F32), jax.ShapeDtypeStruct((2, D), F32)],
    )(gathered, hlb, fold64, fold128)


BIG_NAMES = ("fox_w_in", "fox_w_out", "hgrn_w_in", "hgrn_w_out", "ffn_w_in", "ffn_w_out")
COL_SHARDED = ("fox_w_in", "hgrn_w_in", "ffn_w_in")


def _shard2d(name, a):
    return a.reshape(-1, a.shape[-1])


def _unshard(name, g, layers):
    if name in COL_SHARDED:
        k = g.shape[1] // layers
        return g.reshape(4, layers, k, g.shape[2]).transpose(1, 2, 0, 3).reshape(layers, k, 4 * g.shape[2])
    r = g.shape[1] // layers
    return g.reshape(4, layers, r, g.shape[2]).transpose(1, 0, 2, 3).reshape(layers, 4 * r, g.shape[2])


def _to_shards(name, g):
    layers = g.shape[0]
    if name in COL_SHARDED:
        k, n = g.shape[1], g.shape[2] // 4
        return g.reshape(layers, k, 4, n).transpose(2, 0, 1, 3).reshape(4, layers * k, n)
    r = g.shape[1] // 4
    return g.reshape(layers, 4, r, g.shape[2]).transpose(1, 0, 2, 3).reshape(4, layers * r, g.shape[2])


def kernel(x, meta_tokens, attn_norm, ffn_norm, final_norm, fox_w_in, fox_b_f, fox_q_norm, fox_k_norm, fox_w_out, hgrn_w_in, hgrn_lower_bounds, hgrn_g_norm, hgrn_w_out, ffn_w_in, ffn_w_out, loss_target, m_meta_tokens, m_attn_norm, m_ffn_norm, m_final_norm, m_fox_w_in, m_fox_b_f, m_fox_q_norm, m_fox_k_norm, m_fox_w_out, m_hgrn_w_in, m_hgrn_lower_bounds, m_hgrn_g_norm, m_hgrn_w_out, m_ffn_w_in, m_ffn_w_out, v_meta_tokens, v_attn_norm, v_ffn_norm, v_final_norm, v_fox_w_in, v_fox_b_f, v_fox_q_norm, v_fox_k_norm, v_fox_w_out, v_hgrn_w_in, v_hgrn_lower_bounds, v_hgrn_g_norm, v_hgrn_w_out, v_ffn_w_in, v_ffn_w_out):
    params = dict(meta_tokens=meta_tokens, attn_norm=attn_norm, ffn_norm=ffn_norm, final_norm=final_norm,
                  fox_w_in=fox_w_in, fox_b_f=fox_b_f, fox_q_norm=fox_q_norm, fox_k_norm=fox_k_norm,
                  fox_w_out=fox_w_out, hgrn_w_in=hgrn_w_in, hgrn_lower_bounds=hgrn_lower_bounds,
                  hgrn_g_norm=hgrn_g_norm, hgrn_w_out=hgrn_w_out, ffn_w_in=ffn_w_in, ffn_w_out=ffn_w_out)
    mom_m = dict(meta_tokens=m_meta_tokens, attn_norm=m_attn_norm, ffn_norm=m_ffn_norm, final_norm=m_final_norm,
                 fox_w_in=m_fox_w_in, fox_b_f=m_fox_b_f, fox_q_norm=m_fox_q_norm, fox_k_norm=m_fox_k_norm,
                 fox_w_out=m_fox_w_out, hgrn_w_in=m_hgrn_w_in, hgrn_lower_bounds=m_hgrn_lower_bounds,
                 hgrn_g_norm=m_hgrn_g_norm, hgrn_w_out=m_hgrn_w_out, ffn_w_in=m_ffn_w_in, ffn_w_out=m_ffn_w_out)
    mom_v = dict(meta_tokens=v_meta_tokens, attn_norm=v_attn_norm, ffn_norm=v_ffn_norm, final_norm=v_final_norm,
                 fox_w_in=v_fox_w_in, fox_b_f=v_fox_b_f, fox_q_norm=v_fox_q_norm, fox_k_norm=v_fox_k_norm,
                 fox_w_out=v_fox_w_out, hgrn_w_in=v_hgrn_w_in, hgrn_lower_bounds=v_hgrn_lower_bounds,
                 hgrn_g_norm=v_hgrn_g_norm, hgrn_w_out=v_hgrn_w_out, ffn_w_in=v_ffn_w_in, ffn_w_out=v_ffn_w_out)
    names = list(params)
    seq = x.shape[1]
    xi, yi, ci = _me()

    shards = [_shard2d(n, params[n]).astype(BF16) for n in BIG_NAMES] + [meta_tokens]
    gathered = _chip_allgather(shards)
    w = {n: _unshard(n, g, params[n].shape[0]) for n, g in zip(BIG_NAMES, gathered[:-1])}
    w = {n: (a[0] if a.shape[0] == 1 else a) for n, a in w.items()}
    meta_full = gathered[-1].transpose(1, 0, 2).reshape(N_META, D)
    w.update(attn_norm=attn_norm, ffn_norm=ffn_norm, final_norm=final_norm.reshape(1, D), fox_b_f=fox_b_f,
             fox_q_norm=fox_q_norm, fox_k_norm=fox_k_norm, hgrn_lower_bounds=hgrn_lower_bounds,
             hgrn_g_norm=hgrn_g_norm)

    h0 = jnp.concatenate([jnp.zeros((ROW0, D), F32), meta_full, x[0]], axis=0)
    tgt = jnp.concatenate([jnp.zeros((PAD, D), F32), loss_target[0]], axis=0)
    loss, dh0, big, small = _local_step(h0, tgt, w)
    loss = lax.psum(loss, ("x", "y", "c"))
    grad_x = dh0[PAD:][None]

    gsh = [_to_shards(n, big[n] if big[n].ndim == 3 else big[n][None]) for n in BIG_NAMES]
    red = _reduce_scatter(gsh)
    grads = {n: r.reshape(params[n].shape) for n, r in zip(BIG_NAMES, red)}

    rows = jnp.concatenate([small["attn_norm"], small["ffn_norm"], small["final_norm"], small["lb_raw"],
                            small["q_gain"], small["k_gain"],
                            jnp.pad(small["b_f"], ((0, 0), (0, D - LANES))), small["g_gain"],
                            dh0[ROW0:PAD], jnp.zeros((SMALL_ROWS - 10 - N_META, D), F32)], axis=0)
    allrows = _device_allgather(rows)
    fold64 = jnp.asarray(np.tile(np.eye(FOX_DH, dtype=np.float32), (FOX_H, 1)))
    fold128 = jnp.asarray(np.tile(np.eye(HG_D, dtype=np.float32), (HG_H, 1)))
    tot, qk, gg, dlb = _small_finalize(allrows, hgrn_lower_bounds, fold64, fold128)
    grads.update(attn_norm=tot[0:2], ffn_norm=tot[2:4], final_norm=tot[4], hgrn_lower_bounds=dlb,
                 fox_q_norm=qk[0:1], fox_k_norm=qk[1:2], fox_b_f=tot[8:9, :FOX_H], hgrn_g_norm=gg,
                 meta_tokens=lax.dynamic_slice_in_dim(tot[10:10 + N_META], (2 * xi + yi) * (D // 4), D // 4, axis=1))

    delta, new_m, new_v = {}, {}, {}
    for n in BIG_NAMES + ("meta_tokens",):
        d_, m_, v_ = _adamw(_shard2d(n, params[n]), _shard2d(n, grads[n]), _shard2d(n, mom_m[n]),
                            _shard2d(n, mom_v[n]), f"adamw_{n}")
        delta[n], new_m[n], new_v[n] = (t.reshape(params[n].shape) for t in (d_, m_, v_))
    small_names = [n for n in names if n not in BIG_NAMES and n != "meta_tokens"]

    def pack(d):
        return jnp.concatenate([jnp.pad(d[n].reshape(-1, d[n].shape[-1]), ((0, 0), (0, D - d[n].shape[-1])))
                                for n in small_names], axis=0)

    packed = [pack(t) for t in (params, grads, mom_m, mom_v)]
    n_rows = packed[0].shape[0]
    packed = [jnp.pad(t, ((0, 16 - n_rows), (0, 0))) for t in packed]
    res = _adamw(*packed, "adamw_small")
    r0 = 0
    for n in small_names:
        nr = params[n].reshape(-1, params[n].shape[-1]).shape[0]
        for dst, src in zip((delta, new_m, new_v), res):
            dst[n] = src[r0:r0 + nr, :params[n].shape[-1]].reshape(params[n].shape)
        r0 += nr

    return (loss, grad_x, *[grads[n] for n in names], *[delta[n] for n in names],
            *[new_m[n] for n in names], *[new_v[n] for n in names])
```

```python
import functools

import numpy as np
import jax
import jax.numpy as jnp
from jax import lax
from jax.experimental import pallas as pl
from jax.experimental.pallas import tpu as pltpu

F32, BF16 = jnp.float32, jnp.bfloat16
HIGHEST = lax.Precision.HIGHEST

D = 1024
N_META = 16
PAD = 128
ROW0 = PAD - N_META
FOX_H, FOX_DH = 16, 64
HG_H, HG_D = 8, 128
HG_C = 128
HG_LEV = 7
FFN = 2816
EPS = 1e-6
BIG = 1e30
LOG2E = 1.4426950408889634
LANES = 128
VMEM_LIMIT = 48 * 1024 * 1024
ROW_TILES = (640, 512, 384, 320, 256, 128, 64, 32, 16, 8)

ADAM_LR, ADAM_B1, ADAM_B2, ADAM_EPS, ADAM_WD, ADAM_STEP = 0.001, 0.9, 0.999, 1e-08, 0.01, 10

MESH = pl.DeviceIdType.MESH
ANY = pl.BlockSpec(memory_space=pl.ANY)
NT = (((1,), (1,)), ((), ()))
TN = (((0,), (0,)), ((), ()))


def _tile(n, cands=ROW_TILES, cap=None):
    for c in cands:
        if n % c == 0 and (cap is None or c <= cap):
            return c
    return n


def _cparams(sem):
    return pltpu.CompilerParams(dimension_semantics=sem, vmem_limit_bytes=VMEM_LIMIT)


def _sigmoid(x):
    return jax.nn.sigmoid(x)


def _log_sigmoid(x):
    return jnp.minimum(x, 0.0) - jnp.log(1.0 + jnp.exp(-jnp.abs(x)))


def _iota(shape, dim):
    return lax.broadcasted_iota(jnp.int32, shape, dim)


def _matmul(a, b, *, ta=False, tb=False, out_dtype=F32, add=None, name):
    if ta:
        kdim, m = a.shape
    else:
        m, kdim = a.shape
    n = b.shape[0] if tb else b.shape[1]
    if ta:
        tm = m if m <= 1024 else _tile(m, (1408, 1024, 512, 256, 128))
        tk = _tile(kdim)
    else:
        tm = _tile(m)
        tk = kdim if kdim <= 4096 else _tile(kdim, (2048, 1024, 512))
    tn = n if n <= 1024 else _tile(n, (1408, 1024, 512, 256, 128))
    nk = kdim // tk
    dn = (((0 if ta else 1,), (1 if tb else 0,)), ((), ()))

    def body(*refs):
        if add is None:
            a_ref, b_ref, o_ref, acc_ref = refs
        else:
            a_ref, b_ref, add_ref, o_ref, acc_ref = refs
        k = pl.program_id(2)

        @pl.when(k == 0)
        def _():
            acc_ref[...] = jnp.zeros_like(acc_ref)

        acc_ref[...] += lax.dot_general(a_ref[...].astype(BF16), b_ref[...].astype(BF16), dn,
                                        preferred_element_type=F32)

        @pl.when(k == nk - 1)
        def _():
            r = acc_ref[...]
            if add is not None:
                r = r + add_ref[...].astype(F32)
            o_ref[...] = r.astype(o_ref.dtype)

    a_spec = pl.BlockSpec((tk, tm), lambda j, i, k: (k, i)) if ta else pl.BlockSpec((tm, tk), lambda j, i, k: (i, k))
    b_spec = pl.BlockSpec((tn, tk), lambda j, i, k: (j, k)) if tb else pl.BlockSpec((tk, tn), lambda j, i, k: (k, j))
    o_spec = pl.BlockSpec((tm, tn), lambda j, i, k: (i, j))
    ins, specs = [a, b], [a_spec, b_spec]
    if add is not None:
        ins.append(add)
        specs.append(o_spec)
    return pl.pallas_call(
        body, name=name, grid=(n // tn, m // tm, nk), in_specs=specs, out_specs=o_spec,
        out_shape=jax.ShapeDtypeStruct((m, n), out_dtype),
        scratch_shapes=[pltpu.VMEM((tm, tn), F32)],
        compiler_params=_cparams(("parallel", "parallel", "arbitrary")),
    )(*ins)


def _rowwise(fn, ins, bcast, outs, accs, *, name, reverse=False, carry=None, as_refs=False):
    rows = ins[0].shape[0]
    per_row = sum(x.shape[1] * x.dtype.itemsize for x in ins) + sum(c * jnp.dtype(d).itemsize for c, d in outs)
    tm = _tile(rows, cap=max(8, (10 * 1024 * 1024) // per_row))
    n = rows // tm
    n_in, n_b, n_o, n_a = len(ins), len(bcast), len(outs), len(accs)

    def body(*refs):
        in_refs = refs[:n_in]
        b_refs = refs[n_in:n_in + n_b]
        o_refs = refs[n_in + n_b:n_in + n_b + n_o]
        a_refs = refs[n_in + n_b + n_o:n_in + n_b + n_o + n_a]
        c_refs = refs[n_in + n_b + n_o + n_a:]
        i = pl.program_id(0)
        blk = (n - 1 - i) if reverse else i
        if c_refs:
            @pl.when(i == 0)
            def _():
                c_refs[0][...] = jnp.zeros_like(c_refs[0])
        args = (list(in_refs) if as_refs else [r[...] for r in in_refs], [r[...] for r in b_refs])
        o_vals, a_vals = fn(blk * tm, *args, *c_refs)
        for r, v in zip(o_refs, o_vals):
            r[...] = v.astype(r.dtype)
        if n_a:
            @pl.when(i == 0)
            def _():
                for r in a_refs:
                    r[...] = jnp.zeros_like(r)
            for r, v in zip(a_refs, a_vals):
                r[...] += v

    def row_map(i):
        return ((n - 1 - i) if reverse else i, 0)

    in_specs = [pl.BlockSpec((tm, x.shape[1]), row_map) for x in ins]
    in_specs += [pl.BlockSpec(x.shape, lambda i, nd=x.ndim: (0,) * nd) for x in bcast]
    out_specs = [pl.BlockSpec((tm, c), row_map) for c, _ in outs]
    out_specs += [pl.BlockSpec(s, lambda i: (0, 0)) for s in accs]
    out_shape = [jax.ShapeDtypeStruct((rows, c), d) for c, d in outs]
    out_shape += [jax.ShapeDtypeStruct(s, F32) for s in accs]
    res = pl.pallas_call(
        body, name=name, grid=(n,), in_specs=in_specs, out_specs=out_specs, out_shape=out_shape,
        scratch_shapes=[pltpu.VMEM(carry, F32)] if carry else [],
        compiler_params=_cparams(("arbitrary",)),
    )(*ins, *bcast)
    return res[:n_o], res[n_o:]


def _row_ids(row0, tm):
    return row0 + _iota((tm, 1), 0)


def _rms_fwd(x, gain, name):
    def fn(row0, ins, bc):
        (xv,), (g,) = ins, bc
        r = lax.rsqrt(jnp.mean(xv * xv, axis=-1, keepdims=True) + EPS)
        return [xv * r * g], []
    return _rowwise(fn, [x], [gain], [(D, BF16)], [], name=name)[0][0]


def _rms_bwd(x, dxn, gain, dh_up, name):
    def fn(row0, ins, bc):
        xv, dy, up = ins
        (g,) = bc
        dy = dy.astype(F32)
        r = lax.rsqrt(jnp.mean(xv * xv, axis=-1, keepdims=True) + EPS)
        xh = xv * r
        dxh = dy * g
        dx = r * (dxh - xh * jnp.mean(dxh * xh, axis=-1, keepdims=True))
        keep = _row_ids(row0, xv.shape[0]) >= ROW0
        return [jnp.where(keep, up + dx, 0.0)], [jnp.sum(dy * xh, axis=0, keepdims=True)]
    (dh,), (dgain,) = _rowwise(fn, [x, dxn, dh_up], [gain], [(D, F32)], [(1, D)], name=name)
    return dh, dgain


def _loss_bwd(h, tgt, gain):
    def fn(row0, ins, bc):
        xv, t = ins
        (g,) = bc
        r = lax.rsqrt(jnp.mean(xv * xv, axis=-1, keepdims=True) + EPS)
        xh = xv * r
        keep = _row_ids(row0, xv.shape[0]) >= PAD
        err = jnp.where(keep, xh * g - t, 0.0)
        per_row = jnp.mean(err * err, axis=-1, keepdims=True)
        loss = 0.5 * jnp.sum(per_row, axis=0, keepdims=True)
        dy = err * (1.0 / D)
        dxh = dy * g
        dx = r * (dxh - xh * jnp.mean(dxh * xh, axis=-1, keepdims=True))
        return [dx], [jnp.broadcast_to(loss, (1, LANES)), jnp.sum(dy * xh, axis=0, keepdims=True)]
    (dh,), (loss, dgain) = _rowwise(fn, [h, tgt], [gain], [(D, F32)], [(1, LANES), (1, D)], name="loss_bwd")
    return loss[0, 0], dh, dgain


def _swiglu_fwd(g, u, name):
    def fn(row0, ins, bc):
        gv, uv = (v.astype(F32) for v in ins)
        return [gv * _sigmoid(gv) * uv], []
    return _rowwise(fn, [g, u], [], [(g.shape[1], BF16)], [], name=name)[0][0]


def _swiglu_bwd(dact, g, u, name):
    def fn(row0, ins, bc):
        da, gv, uv = (v.astype(F32) for v in ins)
        s = _sigmoid(gv)
        return [da * uv * (s * (1.0 + gv * (1.0 - s))), da * gv * s], []
    n = g.shape[1]
    return _rowwise(fn, [dact, g, u], [], [(n, BF16), (n, BF16)], [], name=name)[0]


def _adamw(w, g, m, v, name):
    def fn(row0, ins, bc):
        wv, gv, mv, vv = ins
        mn = ADAM_B1 * mv + (1.0 - ADAM_B1) * gv
        vn = ADAM_B2 * vv + (1.0 - ADAM_B2) * (gv * gv)
        m_hat = mn / (1.0 - ADAM_B1 ** ADAM_STEP)
        v_hat = vn / (1.0 - ADAM_B2 ** ADAM_STEP)
        delta = -ADAM_LR * (m_hat / (jnp.sqrt(v_hat) + ADAM_EPS) + ADAM_WD * wv)
        return [delta, mn, vn], []
    c = w.shape[1]
    return _rowwise(fn, [w, g, m, v], [], [(c, F32)] * 3, [], name=name)[0]


def _add(xs, name):
    def fn(row0, ins, bc):
        r = ins[0].astype(F32)
        for v in ins[1:]:
            r = r + v.astype(F32)
        return [r], []
    return _rowwise(fn, list(xs), [], [(xs[0].shape[1], F32)], [], name=name)[0][0]


def _head_sum(x, gmat):
    hi = x.astype(BF16)
    lo = (x - hi.astype(F32)).astype(BF16)
    return jnp.dot(hi, gmat, preferred_element_type=F32) + jnp.dot(lo, gmat, preferred_element_type=F32)


def _split3(x):
    hi = x.astype(BF16).astype(F32)
    r = x - hi
    mid = r.astype(BF16).astype(F32)
    return hi, mid, r - mid


def _extra_base(hh):
    return FOX_DH * (1 - hh)


def _data_mask(hh):
    lane = _iota((1, LANES), 1)
    return (lane >= FOX_DH * hh) & (lane < FOX_DH * (hh + 1))


def _with_extras(data, hh, vals):
    lane = _iota((1, LANES), 1)
    x = jnp.zeros_like(data)
    for e, v in enumerate(vals):
        x = jnp.where(lane == _extra_base(hh) + e, v, x)
    return jnp.where(_data_mask(hh), data, x)


def _fox_pack_fwd(q_raw, k_raw, v, cq, ck, qg, kg, gmat):
    scale2 = FOX_DH ** -0.5 * LOG2E

    def fn(row0, refs, bc):
        q_ref, k_ref, v_ref, cq_ref, ck_ref = refs
        g_q, g_k, gm = bc
        qv, kv = q_ref[...], k_ref[...]
        qn = qv * lax.rsqrt(_head_sum(qv * qv, gm) * (1.0 / FOX_DH) + EPS) * (g_q * scale2)
        kn = kv * lax.rsqrt(_head_sum(kv * kv, gm) * (1.0 / FOX_DH) + EPS) * g_k
        qs, ks, vs = [], [], []
        for h in range(FOX_H):
            p, hh = divmod(h, 2)
            sl = slice(p * LANES, (p + 1) * LANES)
            cq3 = _split3(cq_ref[:, h:h + 1] * LOG2E)
            ck3 = _split3(ck_ref[:, h:h + 1] * (-LOG2E))
            qs.append(_with_extras(qn[:, sl], hh, [*cq3, 1.0, 1.0, 1.0]))
            ks.append(_with_extras(kn[:, sl], hh, [1.0, 1.0, 1.0, *ck3]))
            vs.append(_with_extras(v_ref[:, sl].astype(F32), hh, [1.0, 1.0]))
        return [jnp.concatenate(qs, axis=1), jnp.concatenate(ks, axis=1), jnp.concatenate(vs, axis=1)], []

    w = FOX_H * LANES
    return _rowwise(fn, [q_raw, k_raw, v, cq, ck], [qg, kg, gmat], [(w, BF16)] * 3, [], name="fox_pack_fwd",
                    as_refs=True)[0]


def _fox_pack_bias(qp, cq, lse2):
    def fn(row0, refs, bc):
        q_ref, cq_ref, lse_ref = refs
        lane = _iota((1, LANES), 1)
        outs = []
        for h in range(FOX_H):
            blk = q_ref[:, h * LANES:(h + 1) * LANES].astype(F32)
            for e, part in enumerate(_split3(cq_ref[:, h:h + 1] * LOG2E - lse_ref[:, h:h + 1])):
                blk = jnp.where(lane == _extra_base(h % 2) + e, part, blk)
            outs.append(blk)
        return [jnp.concatenate(outs, axis=1)], []
    return _rowwise(fn, [qp, cq, lse2], [], [(FOX_H * LANES, BF16)], [], name="fox_pack_bias", as_refs=True)[0][0]


def _fox_pack_bwd(dog, o, gate):
    def fn(row0, refs, bc):
        d_ref, o_ref, g_ref = refs
        dos, dgs = [], []
        for p in range(FOX_H // 2):
            sl = slice(p * LANES, (p + 1) * LANES)
            dv, ov, gv = (r[:, sl].astype(F32) for r in (d_ref, o_ref, g_ref))
            s = _sigmoid(gv)
            do = dv * s
            dgs.append(dv * ov * s * (1.0 - s))
            od = ov * do
            for hh in range(2):
                delta = jnp.sum(jnp.where(_data_mask(hh), od, 0.0), axis=-1, keepdims=True)
                hi = delta.astype(BF16).astype(F32)
                dos.append(_with_extras(do, hh, [-hi, hi - delta]))
        return [jnp.concatenate(dos, axis=1), jnp.concatenate(dgs, axis=1)], []
    return _rowwise(fn, [dog, o, gate], [], [(FOX_H * LANES, BF16), (D, BF16)], [], name="fox_pack_bwd",
                    as_refs=True)[0]


def _fox_unpack_bwd(q_raw, k_raw, dqp, dk, qg, kg, gmat):
    scale = FOX_DH ** -0.5

    def fn(row0, refs, bc):
        q_ref, k_ref, dq_ref, dk_ref = refs
        g_q, g_k, gm = bc
        lane = _iota((1, LANES), 1)
        dqs = []
        dcq = jnp.zeros((q_ref.shape[0], LANES), F32)
        for p in range(FOX_H // 2):
            even = dq_ref[:, (2 * p) * LANES:(2 * p + 1) * LANES]
            odd = dq_ref[:, (2 * p + 1) * LANES:(2 * p + 2) * LANES]
            dqs.append(jnp.where(_data_mask(0), even, odd) * scale)
            for hh in range(2):
                col = (2 * p + hh) * LANES + _extra_base(hh)
                dcq = jnp.where(lane == 2 * p + hh, dq_ref[:, col:col + 1], dcq)
        outs, accs = [], []
        for xv, dy, g in ((q_ref[...], jnp.concatenate(dqs, axis=1), g_q), (k_ref[...], dk_ref[...] * (1.0 / LOG2E), g_k)):
            r = lax.rsqrt(_head_sum(xv * xv, gm) * (1.0 / FOX_DH) + EPS)
            xh = xv * r
            dxh = dy * g
            outs.append(r * (dxh - xh * (_head_sum(dxh * xh, gm) * (1.0 / FOX_DH))))
            accs.append(jnp.sum(dy * xh, axis=0, keepdims=True))
        return outs + [dcq], accs
    return _rowwise(fn, [q_raw, k_raw, dqp, dk], [qg, kg, gmat], [(D, BF16), (D, BF16), (LANES, F32)],
                    [(1, D), (1, D)], name="fox_unpack_bwd", as_refs=True)


def _fox_cumsum_fwd(flog, bf):
    def fn(row0, ins, bc, carry):
        (f,), (b,) = ins, bc
        tm = f.shape[0]
        keep = _row_ids(row0, tm) >= ROW0
        lf = jnp.where(keep, _log_sigmoid(f + b), 0.0)
        tri = (_iota((tm, tm), 0) >= _iota((tm, tm), 1)).astype(F32)
        c = jnp.dot(tri, lf, precision=HIGHEST, preferred_element_type=F32) + carry[...]
        carry[...] = carry[...] + jnp.sum(lf, axis=0, keepdims=True)
        return [c, jnp.where(keep, c, BIG)], []
    return _rowwise(fn, [flog], [bf], [(LANES, F32), (LANES, F32)], [], name="fox_cumsum_fwd",
                    carry=(1, LANES))[0]


def _fox_cumsum_bwd(dc_q, dc_k, flog, bf):
    def fn(row0, ins, bc, carry):
        (dq, dk, f), (b,) = ins, bc
        d = dq + dk
        tm = f.shape[0]
        keep = _row_ids(row0, tm) >= ROW0
        triu = (_iota((tm, tm), 0) <= _iota((tm, tm), 1)).astype(F32)
        dlf = jnp.dot(triu, d, precision=HIGHEST, preferred_element_type=F32) + carry[...]
        carry[...] = carry[...] + jnp.sum(d, axis=0, keepdims=True)
        dfl = jnp.where(keep, dlf * _sigmoid(-(f + b)), 0.0)
        return [dfl], [jnp.sum(dfl, axis=0, keepdims=True)]
    (dflog,), (dbf,) = _rowwise(fn, [dc_q, dc_k, flog], [bf], [(LANES, F32)], [(1, LANES)], name="fox_cumsum_bwd",
                                reverse=True, carry=(1, LANES))
    return dflog, dbf


def _causal_steps(n, key_major):
    if key_major:
        pairs = [(i, j) for j in range(n) for i in range(j, n)]
    else:
        pairs = [(i, j) for i in range(n) for j in range(i + 1)]
    return (jnp.asarray(np.array([p[0] for p in pairs], np.int32)),
            jnp.asarray(np.array([p[1] for p in pairs], np.int32)))


def _fox_attn_fwd(qp, kp, vp, gate):
    L = qp.shape[0]
    t = _tile(L, (640, 512, 256, 128))
    n = L // t
    P = FOX_H // 2
    it, jt = _causal_steps(n, False)

    def body(it_ref, jt_ref, q_ref, k_ref, v_ref, g_ref, o_ref, og_ref, lse_ref, m_sc, acc):
        step = pl.program_id(1)
        i, j = it_ref[step], jt_ref[step]

        @pl.when(j == 0)
        def _():
            m_sc[...] = jnp.full_like(m_sc, -3.0e38)
            acc[...] = jnp.zeros_like(acc)

        def update(masked):
            for hh in range(2):
                sl = slice(hh * LANES, (hh + 1) * LANES)
                s2 = lax.dot_general(q_ref[:, sl], k_ref[:, sl], NT, preferred_element_type=F32)
                if masked:
                    s2 = jnp.where(_iota((t, t), 1) <= _iota((t, t), 0), s2, -jnp.inf)
                m_old = m_sc[hh]
                m_new = jnp.maximum(m_old, jnp.max(s2, axis=-1, keepdims=True))
                p = jnp.exp2(s2 - m_new)
                acc[hh] = jnp.exp2(m_old - m_new) * acc[hh] + jnp.dot(p.astype(BF16), v_ref[:, sl],
                                                                      preferred_element_type=F32)
                m_sc[hh] = m_new

        @pl.when(j < i)
        def _():
            update(False)

        @pl.when(j == i)
        def _():
            update(True)
            outs, lses = [], []
            for hh in range(2):
                l = acc[hh, :, _extra_base(hh):_extra_base(hh) + 1]
                outs.append(acc[hh] / l)
                lses.append(m_sc[hh] + jnp.log2(l))
            o = jnp.where(_data_mask(0), outs[0], outs[1])
            o_ref[...] = o.astype(o_ref.dtype)
            og_ref[...] = (o * _sigmoid(g_ref[...].astype(F32))).astype(og_ref.dtype)
            lse_ref[0] = jnp.where(_iota((1, 2), 1) == 0, lses[0], lses[1])

    qspec = pl.BlockSpec((t, 2 * LANES), lambda p, s, it, jt: (it[s], p))
    kspec = pl.BlockSpec((t, 2 * LANES), lambda p, s, it, jt: (jt[s], p))
    ospec = pl.BlockSpec((t, LANES), lambda p, s, it, jt: (it[s], p))
    lspec = pl.BlockSpec((1, t, 2), lambda p, s, it, jt: (p, it[s], 0))
    return pl.pallas_call(
        body, name="fox_attn_fwd",
        grid_spec=pltpu.PrefetchScalarGridSpec(
            num_scalar_prefetch=2, grid=(P, it.shape[0]),
            in_specs=[qspec, kspec, kspec, ospec], out_specs=[ospec, ospec, lspec],
            scratch_shapes=[pltpu.VMEM((2, t, 1), F32), pltpu.VMEM((2, t, LANES), F32)]),
        out_shape=[jax.ShapeDtypeStruct((L, D), BF16), jax.ShapeDtypeStruct((L, D), BF16),
                   jax.ShapeDtypeStruct((P, L, 2), F32)],
        compiler_params=_cparams(("parallel", "arbitrary")),
    )(it, jt, qp, kp, vp, gate)


def _fox_attn_bwd(qb, kp, vp, dop):
    L = qb.shape[0]
    t = _tile(L, (640, 512, 256, 128))
    n = L // t
    P = FOX_H // 2
    it, jt = _causal_steps(n, True)

    def body(it_ref, jt_ref, q_ref, k_ref, v_ref, do_ref, dq_ref, dk_ref, dv_ref, dck_ref, dk_acc, dv_acc):
        step = pl.program_id(1)
        i, j = it_ref[step], jt_ref[step]

        @pl.when(step == 0)
        def _():
            dq_ref[...] = jnp.zeros_like(dq_ref)

        @pl.when(i == j)
        def _():
            dk_acc[...] = jnp.zeros_like(dk_acc)
            dv_acc[...] = jnp.zeros_like(dv_acc)

        def update(masked):
            rows = pl.ds(pl.multiple_of(i * t, LANES), t)
            for hh in range(2):
                sl = slice(hh * LANES, (hh + 1) * LANES)
                q, k, dov = q_ref[:, sl], k_ref[:, sl], do_ref[:, sl]
                s2 = lax.dot_general(q, k, NT, preferred_element_type=F32)
                if masked:
                    s2 = jnp.where(_iota((t, t), 1) <= _iota((t, t), 0), s2, -jnp.inf)
                p = jnp.exp2(s2)
                ds = (p * lax.dot_general(dov, v_ref[:, sl], NT, preferred_element_type=F32)).astype(BF16)
                dv_acc[hh] += lax.dot_general(p.astype(BF16), dov, TN, preferred_element_type=F32)
                dk_acc[hh] += lax.dot_general(ds, q, TN, preferred_element_type=F32)
                dq_ref[rows, sl] += jnp.dot(ds, k, preferred_element_type=F32)

        @pl.when(i > j)
        def _():
            update(False)

        @pl.when(i == j)
        def _():
            update(True)

        @pl.when(i == n - 1)
        def _():
            dk_ref[...] = jnp.where(_data_mask(0), dk_acc[0], dk_acc[1])
            dv_ref[...] = jnp.where(_data_mask(0), dv_acc[0], dv_acc[1]).astype(dv_ref.dtype)
            col_sums = [dk_acc[hh, :, _extra_base(hh) + 3:_extra_base(hh) + 4] for hh in range(2)]
            dck_ref[0] = -jnp.where(_iota((1, 2), 1) == 0, col_sums[0], col_sums[1])

    qspec = pl.BlockSpec((t, 2 * LANES), lambda p, s, it, jt: (it[s], p))
    kspec = pl.BlockSpec((t, 2 * LANES), lambda p, s, it, jt: (jt[s], p))
    ospec = pl.BlockSpec((t, LANES), lambda p, s, it, jt: (jt[s], p))
    return pl.pallas_call(
        body, name="fox_attn_bwd",
        grid_spec=pltpu.PrefetchScalarGridSpec(
            num_scalar_prefetch=2, grid=(P, it.shape[0]),
            in_specs=[qspec, kspec, kspec, qspec],
            out_specs=[pl.BlockSpec((L, 2 * LANES), lambda p, s, it, jt: (0, p)), ospec, ospec,
                       pl.BlockSpec((1, t, 2), lambda p, s, it, jt: (p, jt[s], 0))],
            scratch_shapes=[pltpu.VMEM((2, t, LANES), F32), pltpu.VMEM((2, t, LANES), F32)]),
        out_shape=[jax.ShapeDtypeStruct((L, FOX_H * LANES), F32), jax.ShapeDtypeStruct((L, D), F32),
                   jax.ShapeDtypeStruct((L, D), BF16), jax.ShapeDtypeStruct((P, L, 2), F32)],
        compiler_params=_cparams(("parallel", "arbitrary")),
    )(it, jt, qb, kp, vp, dop)


def _hgrn_consts():
    C = HG_C
    r = np.arange(C)[:, None]
    j = np.arange(C)[None, :]
    mats = [j <= r, j > r]
    masks = []
    n = C
    while n >= 2:
        half = n // 2
        mid = (r // n) * n + half - 1
        second = (r % n) >= half
        mats.append(np.where(second, (j > mid) & (j <= r), (j > r) & (j <= mid)))
        masks.append(((r // n) == (j // n)) & ((r % n) >= half) & ((j % n) < half))
        n //= 2
    return (jnp.asarray(np.concatenate(mats, 0).astype(np.float32), BF16),
            jnp.asarray(np.stack(masks).astype(np.float32), F32))


def _hg_pre(hq, hz, hlb_ref):
    h0, h1 = hlb_ref[0:1, :], hlb_ref[1:2, :]
    mx = jnp.maximum(h0, h1)
    e0, e1 = jnp.exp(h0 - mx), jnp.exp(h1 - mx)
    lb = e1 / (e0 + e1)
    sq = _sigmoid(hq)
    sz, snz = _sigmoid(hz), _sigmoid(-hz)
    k = (1.0 - lb) * snz
    a = jnp.log(lb)
    b = jnp.log1p(-lb) + _log_sigmoid(hz)
    g = jnp.maximum(a, b) + jnp.log(1.0 + jnp.exp(-jnp.abs(a - b)))
    return lb, hq * sq, sq, k, sz, snz, g


def _hg_decays(g, rmat):
    hi = g.astype(BF16)
    lo = (g - hi.astype(F32)).astype(BF16)
    d = jnp.dot(rmat, jnp.concatenate([hi, lo], axis=1), preferred_element_type=F32)
    return jnp.exp(d[:, :HG_D] + d[:, HG_D:])


def _hg_intra(q, k, fall, masks):
    C = HG_C
    eye = _iota((C, C), 0) == _iota((C, C), 1)
    a = jnp.where(eye, jnp.sum(q * k, axis=-1, keepdims=True), 0.0)
    for l in range(HG_LEV):
        f = fall[(2 + l) * C:(3 + l) * C]
        a = a + masks[l] * lax.dot_general((q * f).astype(BF16), (k * f).astype(BF16), NT,
                                           preferred_element_type=F32)
    return a


def _hgrn_specs(n_chunks, reverse):
    C = HG_C

    def col(off):
        if reverse:
            return pl.BlockSpec((C, HG_D), lambda h, c: (n_chunks - 1 - c, off + h))
        return pl.BlockSpec((C, HG_D), lambda h, c: (c, off + h))

    st = pl.BlockSpec((1, 1, HG_D, HG_D),
                      (lambda h, c: (h, n_chunks - 1 - c, 0, 0)) if reverse else (lambda h, c: (h, c, 0, 0)))
    consts = [pl.BlockSpec((2, HG_D), lambda h, c: (0, h)), pl.BlockSpec((1, HG_D), lambda h, c: (0, 0)),
              pl.BlockSpec(((2 + HG_LEV) * C, C), lambda h, c: (0, 0)),
              pl.BlockSpec((HG_LEV, C, C), lambda h, c: (0, 0, 0))]
    return col, st, consts


def _hgrn_fwd(proj, hlb, gg, rmat, masks):
    L = proj.shape[0]
    C = HG_C
    nc = L // C
    col, st, consts = _hgrn_specs(nc, False)

    def body(hq_ref, hz_ref, hi_ref, hg_ref, hlb_ref, gg_ref, r_ref, m_ref, og_ref, st_ref, state):
        c = pl.program_id(1)

        @pl.when(c == 0)
        def _():
            state[...] = jnp.zeros_like(state)

        v, hg = hi_ref[...], hg_ref[...]
        _, q, _, k, _, _, g = _hg_pre(hq_ref[...], hz_ref[...], hlb_ref)
        fall = _hg_decays(g, r_ref[...])
        fb, fe = fall[0:C], fall[C:2 * C]
        st0 = state[...]
        st_ref[0, 0] = st0
        a = _hg_intra(q, k, fall, m_ref[...])
        vb = v.astype(BF16)
        o = jnp.dot(a.astype(BF16), vb, preferred_element_type=F32)
        o = o + lax.dot_general((q * fb).astype(BF16), st0.astype(BF16), NT, preferred_element_type=F32)
        ebc = jnp.exp(jnp.sum(g, axis=0, keepdims=True))
        state[...] = st0 * ebc + lax.dot_general(vb, (k * fe).astype(BF16), TN, preferred_element_type=F32)
        r = lax.rsqrt(jnp.mean(o * o, axis=-1, keepdims=True) + EPS)
        og_ref[...] = (o * r * gg_ref[...] * (hg * _sigmoid(hg))).astype(og_ref.dtype)

    return pl.pallas_call(
        body, name="hgrn_fwd", grid=(HG_H, nc),
        in_specs=[col(0), col(HG_H), col(2 * HG_H), col(3 * HG_H)] + consts,
        out_specs=[col(0), st],
        out_shape=[jax.ShapeDtypeStruct((L, D), BF16), jax.ShapeDtypeStruct((HG_H, nc, HG_D, HG_D), F32)],
        scratch_shapes=[pltpu.VMEM((HG_D, HG_D), F32)],
        compiler_params=_cparams(("parallel", "arbitrary")),
    )(proj, proj, proj, proj, hlb, gg, rmat, masks)


def _hgrn_bwd(proj, dog, states, hlb, gg, rmat, masks):
    L = proj.shape[0]
    C = HG_C
    nc = L // C
    col, st, consts = _hgrn_specs(nc, True)

    def body(hq_ref, hz_ref, hi_ref, hg_ref, do_ref, hlb_ref, gg_ref, r_ref, m_ref, st_ref,
             dq_ref, dz_ref, di_ref, dg_ref, dlb_ref, dgg_ref, dstate):
        c = pl.program_id(1)

        @pl.when(c == 0)
        def _():
            dstate[...] = jnp.zeros_like(dstate)
            dlb_ref[...] = jnp.zeros_like(dlb_ref)
            dgg_ref[...] = jnp.zeros_like(dgg_ref)

        hq, hz, v, hg = hq_ref[...], hz_ref[...], hi_ref[...], hg_ref[...]
        dout = do_ref[...].astype(F32)
        gain = gg_ref[...]
        masks_v = m_ref[...]
        lb, q, sq, k, sz, snz, g = _hg_pre(hq, hz, hlb_ref)
        fall = _hg_decays(g, r_ref[...])
        fb, fe = fall[0:C], fall[C:2 * C]
        a = _hg_intra(q, k, fall, masks_v)
        st0 = st_ref[0, 0]
        st0b = st0.astype(BF16)
        ebc = jnp.exp(jnp.sum(g, axis=0, keepdims=True))
        qb, ke, vb = (q * fb).astype(BF16), (k * fe).astype(BF16), v.astype(BF16)
        ab = a.astype(BF16)
        o = jnp.dot(ab, vb, preferred_element_type=F32) + lax.dot_general(qb, st0b, NT, preferred_element_type=F32)
        r = lax.rsqrt(jnp.mean(o * o, axis=-1, keepdims=True) + EPS)
        oh = o * r
        sg = _sigmoid(hg)
        d_on = dout * (hg * sg)
        dhg = dout * (oh * gain) * (sg * (1.0 + hg * (1.0 - sg)))
        dgg_ref[0] += jnp.sum(d_on * oh, axis=0, keepdims=True)
        dxh = d_on * gain
        do = r * (dxh - oh * jnp.mean(dxh * oh, axis=-1, keepdims=True))
        dob = do.astype(BF16)
        dsp = dstate[...]
        dspb = dsp.astype(BF16)
        causal = _iota((C, C), 0) >= _iota((C, C), 1)
        da = jnp.where(causal, lax.dot_general(dob, vb, NT, preferred_element_type=F32), 0.0)
        diag = jnp.sum(do * v, axis=-1, keepdims=True)
        dv = lax.dot_general(ab, dob, TN, preferred_element_type=F32)
        dv = dv + lax.dot_general(ke, dspb, NT, preferred_element_type=F32)
        xq = jnp.dot(dob, st0b, preferred_element_type=F32)
        xk = jnp.dot(vb, dspb, preferred_element_type=F32)
        dq = diag * k + fb * xq
        dk = diag * q + fe * xk
        ke_xk = ke.astype(F32) * xk
        db = qb.astype(F32) * xq - ke_xk
        for l in range(HG_LEV):
            f = fall[(2 + l) * C:(3 + l) * C]
            dal = (masks_v[l] * da).astype(BF16)
            ql, kl = (q * f).astype(BF16), (k * f).astype(BF16)
            xq = jnp.dot(dal, kl, preferred_element_type=F32)
            xk = lax.dot_general(dal, ql, TN, preferred_element_type=F32)
            dq = dq + f * xq
            dk = dk + f * xk
            db = db + ql.astype(F32) * xq - kl.astype(F32) * xk
        dstate[...] = dsp * ebc + lax.dot_general(dob, qb, TN, preferred_element_type=F32)
        triu = (_iota((C, C), 0) <= _iota((C, C), 1)).astype(F32)
        dg = jnp.dot(triu, db, precision=HIGHEST, preferred_element_type=F32)
        dg = dg + jnp.sum(st0 * ebc * dsp, axis=0, keepdims=True) + jnp.sum(ke_xk, axis=0, keepdims=True)
        keep = _row_ids((nc - 1 - c) * C, C) >= ROW0
        dg = jnp.where(keep, dg, 0.0)
        dk = jnp.where(keep, dk, 0.0)
        f_gate = lb + (1.0 - lb) * sz
        dfdz = (1.0 - lb) * sz * snz
        dz_ref[...] = (dg * dfdz / f_gate - dk * dfdz).astype(dz_ref.dtype)
        dlb_ref[...] += jnp.sum(dg * snz / f_gate - dk * snz, axis=0, keepdims=True)
        dq_ref[...] = jnp.where(keep, dq * (sq * (1.0 + hq * (1.0 - sq))), 0.0).astype(dq_ref.dtype)
        di_ref[...] = jnp.where(keep, dv, 0.0).astype(di_ref.dtype)
        dg_ref[...] = jnp.where(keep, dhg, 0.0).astype(dg_ref.dtype)

    outs = pl.pallas_call(
        body, name="hgrn_bwd", grid=(HG_H, nc),
        in_specs=[col(0), col(HG_H), col(2 * HG_H), col(3 * HG_H), col(0)] + consts + [st],
        out_specs=[col(0), col(0), col(0), col(0), pl.BlockSpec((1, HG_D), lambda h, c: (0, h)),
                   pl.BlockSpec((1, 1, HG_D), lambda h, c: (h, 0, 0))],
        out_shape=[jax.ShapeDtypeStruct((L, D), BF16)] * 4 + [jax.ShapeDtypeStruct((1, D), F32),
                                                              jax.ShapeDtypeStruct((HG_H, 1, HG_D), F32)],
        scratch_shapes=[pltpu.VMEM((HG_D, HG_D), F32)],
        compiler_params=_cparams(("parallel", "arbitrary")),
    )(proj, proj, proj, proj, dog, hlb, gg, rmat, masks, states)
    return outs


def _ffn_fwd(h, norm_gain, wg, wu, wo, tag):
    hn = _rms_fwd(h, norm_gain, f"{tag}_norm")
    g = _matmul(hn, wg, out_dtype=BF16, name=f"{tag}_gate")
    u = _matmul(hn, wu, out_dtype=BF16, name=f"{tag}_up")
    act = _swiglu_fwd(g, u, f"{tag}_act")
    h_out = _matmul(act, wo, add=h, name=f"{tag}_out")
    return h_out, (h, hn, g, u, act)


def _ffn_bwd(dh, saved, norm_gain, wg, wu, wo, tag):
    h, hn, g, u, act = saved
    dact = _matmul(dh, wo, tb=True, out_dtype=BF16, name=f"{tag}_dact")
    d_wo = _matmul(act, dh, ta=True, name=f"{tag}_dwo")
    dg, du = _swiglu_bwd(dact, g, u, f"{tag}_dact_bwd")
    dhn = _matmul(dg, wg, tb=True, name=f"{tag}_dhn_g")
    dhn = _matmul(du, wu, tb=True, add=dhn, name=f"{tag}_dhn_u")
    d_wg = _matmul(hn, dg, ta=True, name=f"{tag}_dwg")
    d_wu = _matmul(hn, du, ta=True, name=f"{tag}_dwu")
    dh, d_gain = _rms_bwd(h, dhn, norm_gain, dh, f"{tag}_norm_bwd")
    return dh, d_gain, jnp.concatenate([d_wg, d_wu], axis=1), d_wo


def _local_step(h0, tgt, w):
    L = h0.shape[0]
    gmat = jnp.asarray(np.kron(np.eye(FOX_H), np.ones((FOX_DH, FOX_DH))).astype(np.float32), BF16)
    rmat, lmasks = _hgrn_consts()
    an, fn_ = w["attn_norm"], w["ffn_norm"]
    qg = jnp.tile(w["fox_q_norm"], (1, FOX_H))
    kg = jnp.tile(w["fox_k_norm"], (1, FOX_H))
    bf = jnp.pad(w["fox_b_f"], ((0, 0), (0, LANES - FOX_H)))
    fw = w["fox_w_in"]
    f_wq, f_wk, f_wv, f_wg = (fw[:, i * D:(i + 1) * D] for i in range(4))
    f_wf = jnp.pad(fw[:, 4 * D:], ((0, 0), (0, LANES - FOX_H)))
    f_wo = w["fox_w_out"]
    h_wi, h_wo = w["hgrn_w_in"], w["hgrn_w_out"]
    ffw = [(w["ffn_w_in"][i][:, :FFN], w["ffn_w_in"][i][:, FFN:], w["ffn_w_out"][i]) for i in range(2)]

    hn0 = _rms_fwd(h0, an[0:1], "fox_norm")
    q_raw = _matmul(hn0, f_wq, name="fox_q")
    k_raw = _matmul(hn0, f_wk, name="fox_k")
    v = _matmul(hn0, f_wv, out_dtype=BF16, name="fox_v")
    gate = _matmul(hn0, f_wg, out_dtype=BF16, name="fox_gate")
    flog = _matmul(hn0, f_wf, name="fox_flog")
    cq, ck = _fox_cumsum_fwd(flog, bf)
    qp, kp, vp = _fox_pack_fwd(q_raw, k_raw, v, cq, ck, qg, kg, gmat)
    o, og, lse2 = _fox_attn_fwd(qp, kp, vp, gate)
    h1 = _matmul(og, f_wo, add=h0, name="fox_out")
    h2, ffn0 = _ffn_fwd(h1, fn_[0:1], *ffw[0], "ffn0")

    hn2 = _rms_fwd(h2, an[1:2], "hgrn_norm")
    proj = _matmul(hn2, h_wi, name="hgrn_in")
    og1, states = _hgrn_fwd(proj, w["hgrn_lower_bounds"], w["hgrn_g_norm"], rmat, lmasks)
    h3 = _matmul(og1, h_wo, add=h2, name="hgrn_out")
    h4, ffn1 = _ffn_fwd(h3, fn_[1:2], *ffw[1], "ffn1")

    loss, dh, d_final = _loss_bwd(h4, tgt, w["final_norm"])

    dh, d_fn1, d_ffn_in1, d_ffn_out1 = _ffn_bwd(dh, ffn1, fn_[1:2], *ffw[1], "ffn1")
    dog1 = _matmul(dh, h_wo, tb=True, out_dtype=BF16, name="hgrn_dog")
    d_h_wo = _matmul(og1, dh, ta=True, name="hgrn_dwo")
    dpq, dpz, dpi, dpg, d_lb, d_gg = _hgrn_bwd(proj, dog1, states, w["hgrn_lower_bounds"], w["hgrn_g_norm"],
                                               rmat, lmasks)
    dproj = jnp.concatenate([dpq, dpz, dpi, dpg], axis=1)
    dhn2 = _matmul(dproj, h_wi, tb=True, name="hgrn_dhn")
    d_h_wi = _matmul(hn2, dproj, ta=True, name="hgrn_dwi")
    dh, d_an1 = _rms_bwd(h2, dhn2, an[1:2], dh, "hgrn_norm_bwd")

    dh, d_fn0, d_ffn_in0, d_ffn_out0 = _ffn_bwd(dh, ffn0, fn_[0:1], *ffw[0], "ffn0")
    dog = _matmul(dh, f_wo, tb=True, out_dtype=BF16, name="fox_dog")
    d_f_wo = _matmul(og, dh, ta=True, name="fox_dwo")
    def by_head(a):
        return jnp.pad(a.transpose(1, 0, 2).reshape(L, FOX_H), ((0, 0), (0, LANES - FOX_H)))

    qb = _fox_pack_bias(qp, cq, by_head(lse2))
    dop, dgate = _fox_pack_bwd(dog, o, gate)
    dqp, dk, dv, dck = _fox_attn_bwd(qb, kp, vp, dop)
    (dq_raw, dk_raw, dc_q), (d_qg, d_kg) = _fox_unpack_bwd(q_raw, k_raw, dqp, dk, qg, kg, gmat)
    dflog, d_bf = _fox_cumsum_bwd(dc_q, by_head(dck), flog, bf)
    dproj0 = jnp.concatenate([dq_raw, dk_raw, dv, dgate, dflog.astype(BF16)], axis=1)
    f_wall = jnp.concatenate([f_wq, f_wk, f_wv, f_wg, f_wf], axis=1)
    dhn0 = _matmul(dproj0, f_wall, tb=True, name="fox_dhn")
    d_f_wall = _matmul(hn0, dproj0, ta=True, name="fox_dwi")
    d_f_wi = d_f_wall[:, :4 * D + FOX_H]
    dh, d_an0 = _rms_bwd(h0, dhn0, an[0:1], dh, "fox_norm_bwd")

    big = dict(fox_w_in=d_f_wi, fox_w_out=d_f_wo, hgrn_w_in=d_h_wi, hgrn_w_out=d_h_wo,
               ffn_w_in=jnp.stack([d_ffn_in0, d_ffn_in1]), ffn_w_out=jnp.stack([d_ffn_out0, d_ffn_out1]))
    small = dict(attn_norm=jnp.concatenate([d_an0, d_an1]), ffn_norm=jnp.concatenate([d_fn0, d_fn1]),
                 final_norm=d_final, lb_raw=d_lb, q_gain=d_qg, k_gain=d_kg, b_f=d_bf,
                 g_gain=d_gg.reshape(1, D))
    return loss, dh, big, small


def _me():
    return lax.axis_index("x"), lax.axis_index("y"), lax.axis_index("c")


def _flip(v, bit):
    return 1 - v if bit else v


def _chip_allgather(arrs):
    n = len(arrs)

    def body(*refs):
        ins, outs = refs[:n], refs[n:2 * n]
        ssem, rsem, lsem = refs[2 * n:]
        x, y, c = _me()
        peers = [(1 - x, y, c), (x, 1 - y, c), (1 - x, 1 - y, c)]
        local, sends = [], []
        for a in range(n):
            cp = pltpu.make_async_copy(ins[a], outs[a].at[2 * x + y], lsem.at[a])
            cp.start()
            local.append(cp)
            for k, peer in enumerate(peers):
                cp = pltpu.make_async_remote_copy(ins[a], outs[a].at[2 * x + y], ssem.at[a, k], rsem.at[a, k],
                                                  device_id=peer, device_id_type=MESH)
                cp.start()
                sends.append(cp)
        for cp in local:
            cp.wait()
        for cp in sends:
            cp.wait_send()
        for a in range(n):
            for k, peer in enumerate(peers):
                pltpu.make_async_remote_copy(ins[a], outs[a].at[2 * peer[0] + peer[1]], ssem.at[a, k], rsem.at[a, k],
                                             device_id=peer, device_id_type=MESH).wait_recv()

    return pl.pallas_call(
        body, name="chip_allgather", in_specs=[ANY] * n, out_specs=[ANY] * n,
        out_shape=[jax.ShapeDtypeStruct((4,) + a.shape, a.dtype) for a in arrs],
        scratch_shapes=[pltpu.SemaphoreType.DMA((n, 3)), pltpu.SemaphoreType.DMA((n, 3)),
                        pltpu.SemaphoreType.DMA((n,))],
    )(*arrs)


def _device_allgather(arr):
    def body(in_ref, out_ref, ssem, rsem, lsem):
        x, y, c = _me()
        me = 4 * x + 2 * y + c
        peers = [(_flip(x, k & 4), _flip(y, k & 2), _flip(c, k & 1)) for k in range(1, 8)]
        local = pltpu.make_async_copy(in_ref, out_ref.at[me], lsem)
        local.start()
        sends = []
        for k, peer in enumerate(peers):
            cp = pltpu.make_async_remote_copy(in_ref, out_ref.at[me], ssem.at[k], rsem.at[k],
                                              device_id=peer, device_id_type=MESH)
            cp.start()
            sends.append(cp)
        local.wait()
        for cp in sends:
            cp.wait_send()
        for k, peer in enumerate(peers):
            pltpu.make_async_remote_copy(in_ref, out_ref.at[4 * peer[0] + 2 * peer[1] + peer[2]], ssem.at[k],
                                         rsem.at[k], device_id=peer, device_id_type=MESH).wait_recv()

    return pl.pallas_call(
        body, name="device_allgather", in_specs=[ANY], out_specs=ANY,
        out_shape=jax.ShapeDtypeStruct((8,) + arr.shape, arr.dtype),
        scratch_shapes=[pltpu.SemaphoreType.DMA((7,)), pltpu.SemaphoreType.DMA((7,)), pltpu.SemaphoreType.DMA],
    )(arr)


def _sibling_send_other_half(arrs):
    n = len(arrs)

    def body(*refs):
        ins, outs = refs[:n], refs[n:2 * n]
        ssem, rsem = refs[2 * n:]
        x, y, c = _me()
        cps = []
        for a in range(n):
            half = ins[a].shape[1] // 2
            src = ins[a].at[:, pl.ds((1 - c) * half, half), :]
            cp = pltpu.make_async_remote_copy(src, outs[a], ssem.at[a], rsem.at[a],
                                              device_id=(x, y, 1 - c), device_id_type=MESH)
            cp.start()
            cps.append(cp)
        for cp in cps:
            cp.wait()

    return pl.pallas_call(
        body, name="grad_sibling_swap", in_specs=[ANY] * n, out_specs=[ANY] * n,
        out_shape=[jax.ShapeDtypeStruct((4, a.shape[1] // 2, a.shape[2]), a.dtype) for a in arrs],
        scratch_shapes=[pltpu.SemaphoreType.DMA((n,)), pltpu.SemaphoreType.DMA((n,))],
    )(*arrs)


def _chip_scatter(arrs):
    n = len(arrs)

    def body(*refs):
        ins, outs = refs[:n], refs[n:2 * n]
        ssem, rsem = refs[2 * n:]
        x, y, c = _me()
        peers = [(1 - x, y, c), (x, 1 - y, c), (1 - x, 1 - y, c)]
        cps = []
        for a in range(n):
            for k, peer in enumerate(peers):
                cp = pltpu.make_async_remote_copy(ins[a].at[2 * peer[0] + peer[1]], outs[a].at[k], ssem.at[a, k],
                                                  rsem.at[a, k], device_id=peer, device_id_type=MESH)
                cp.start()
                cps.append(cp)
        for cp in cps:
            cp.wait()

    return pl.pallas_call(
        body, name="grad_chip_scatter", in_specs=[ANY] * n, out_specs=[ANY] * n,
        out_shape=[jax.ShapeDtypeStruct((3,) + a.shape[1:], a.dtype) for a in arrs],
        scratch_shapes=[pltpu.SemaphoreType.DMA((n, 3)), pltpu.SemaphoreType.DMA((n, 3))],
    )(*arrs)


def _sibling_allgather(arrs):
    n = len(arrs)

    def body(*refs):
        ins, outs = refs[:n], refs[n:2 * n]
        ssem, rsem, lsem = refs[2 * n:]
        x, y, c = _me()
        cps = []
        for a in range(n):
            cp = pltpu.make_async_copy(ins[a], outs[a].at[c], lsem.at[a])
            cp.start()
            cps.append(cp)
            cp = pltpu.make_async_remote_copy(ins[a], outs[a].at[c], ssem.at[a], rsem.at[a],
                                              device_id=(x, y, 1 - c), device_id_type=MESH)
            cp.start()
            cps.append(cp)
        for cp in cps:
            cp.wait()

    return pl.pallas_call(
        body, name="grad_sibling_allgather", in_specs=[ANY] * n, out_specs=[ANY] * n,
        out_shape=[jax.ShapeDtypeStruct((2,) + a.shape, a.dtype) for a in arrs],
        scratch_shapes=[pltpu.SemaphoreType.DMA((n,)), pltpu.SemaphoreType.DMA((n,)),
                        pltpu.SemaphoreType.DMA((n,))],
    )(*arrs)


def _reduce_scatter(grads):
    x, y, c = _me()
    got = _sibling_send_other_half(grads)
    pair = []
    for i, (g, t) in enumerate(zip(grads, got)):
        half = g.shape[1] // 2
        mine = lax.dynamic_slice_in_dim(g, c * half, half, axis=1)
        pair.append(_add([mine.reshape(4 * half, -1), t.reshape(4 * half, -1)], f"grad_pair_add{i}").reshape(t.shape))
    recv = _chip_scatter([p.astype(BF16) for p in pair])
    halves = []
    for i, (p, r) in enumerate(zip(pair, recv)):
        own = lax.dynamic_index_in_dim(p, 2 * x + y, axis=0, keepdims=False)
        halves.append(_add([own, r[0], r[1], r[2]], f"grad_chip_add{i}"))
    full = _sibling_allgather(halves)
    return [f.reshape(2 * f.shape[1], f.shape[2]) for f in full]


SMALL_ROWS = 32


def _small_finalize(gathered, hlb, fold64, fold128):
    def body(g_ref, hlb_ref, f64_ref, f128_ref, rows_ref, qk_ref, gg_ref, lb_ref):
        tot = g_ref[0]
        for d in range(1, 8):
            tot = tot + g_ref[d]
        rows_ref[...] = tot
        qk_ref[...] = jnp.dot(rows_ref[6:8, :], f64_ref[...], precision=HIGHEST, preferred_element_type=F32)
        gg_ref[...] = jnp.dot(rows_ref[9:10, :], f128_ref[...], precision=HIGHEST, preferred_element_type=F32)
        h0, h1 = hlb_ref[0:1, :], hlb_ref[1:2, :]
        mx = jnp.maximum(h0, h1)
        e0, e1 = jnp.exp(h0 - mx), jnp.exp(h1 - mx)
        lb = e1 / (e0 + e1)
        d1 = rows_ref[5:6, :] * lb * (1.0 - lb)
        lb_ref[...] = jnp.where(_iota((2, 1), 0) == 0, -d1, d1)

    return pl.pallas_call(
        body, name="small_finalize",
        out_shape=[jax.ShapeDtypeStruct((SMALL_ROWS, D), F32), jax.ShapeDtypeStruct((2, FOX_DH), F32),
                   jax.ShapeDtypeStruct((1, HG_D), F32), jax.ShapeDtypeStruct((2, D), F32)],
    )(gathered, hlb, fold64, fold128)


BIG_NAMES = ("fox_w_in", "fox_w_out", "hgrn_w_in", "hgrn_w_out", "ffn_w_in", "ffn_w_out")
COL_SHARDED = ("fox_w_in", "hgrn_w_in", "ffn_w_in")


def _shard2d(name, a):
    return a.reshape(-1, a.shape[-1])


def _unshard(name, g, layers):
    if name in COL_SHARDED:
        k = g.shape[1] // layers
        return g.reshape(4, layers, k, g.shape[2]).transpose(1, 2, 0, 3).reshape(layers, k, 4 * g.shape[2])
    r = g.shape[1] // layers
    return g.reshape(4, layers, r, g.shape[2]).transpose(1, 0, 2, 3).reshape(layers, 4 * r, g.shape[2])


def _to_shards(name, g):
    layers = g.shape[0]
    if name in COL_SHARDED:
        k, n = g.shape[1], g.shape[2] // 4
        return g.reshape(layers, k, 4, n).transpose(2, 0, 1, 3).reshape(4, layers * k, n)
    r = g.shape[1] // 4
    return g.reshape(layers, 4, r, g.shape[2]).transpose(1, 0, 2, 3).reshape(4, layers * r, g.shape[2])


def kernel(x, meta_tokens, attn_norm, ffn_norm, final_norm, fox_w_in, fox_b_f, fox_q_norm, fox_k_norm, fox_w_out, hgrn_w_in, hgrn_lower_bounds, hgrn_g_norm, hgrn_w_out, ffn_w_in, ffn_w_out, loss_target, m_meta_tokens, m_attn_norm, m_ffn_norm, m_final_norm, m_fox_w_in, m_fox_b_f, m_fox_q_norm, m_fox_k_norm, m_fox_w_out, m_hgrn_w_in, m_hgrn_lower_bounds, m_hgrn_g_norm, m_hgrn_w_out, m_ffn_w_in, m_ffn_w_out, v_meta_tokens, v_attn_norm, v_ffn_norm, v_final_norm, v_fox_w_in, v_fox_b_f, v_fox_q_norm, v_fox_k_norm, v_fox_w_out, v_hgrn_w_in, v_hgrn_lower_bounds, v_hgrn_g_norm, v_hgrn_w_out, v_ffn_w_in, v_ffn_w_out):
    params = dict(meta_tokens=meta_tokens, attn_norm=attn_norm, ffn_norm=ffn_norm, final_norm=final_norm,
                  fox_w_in=fox_w_in, fox_b_f=fox_b_f, fox_q_norm=fox_q_norm, fox_k_norm=fox_k_norm,
                  fox_w_out=fox_w_out, hgrn_w_in=hgrn_w_in, hgrn_lower_bounds=hgrn_lower_bounds,
                  hgrn_g_norm=hgrn_g_norm, hgrn_w_out=hgrn_w_out, ffn_w_in=ffn_w_in, ffn_w_out=ffn_w_out)
    mom_m = dict(meta_tokens=m_meta_tokens, attn_norm=m_attn_norm, ffn_norm=m_ffn_norm, final_norm=m_final_norm,
                 fox_w_in=m_fox_w_in, fox_b_f=m_fox_b_f, fox_q_norm=m_fox_q_norm, fox_k_norm=m_fox_k_norm,
                 fox_w_out=m_fox_w_out, hgrn_w_in=m_hgrn_w_in, hgrn_lower_bounds=m_hgrn_lower_bounds,
                 hgrn_g_norm=m_hgrn_g_norm, hgrn_w_out=m_hgrn_w_out, ffn_w_in=m_ffn_w_in, ffn_w_out=m_ffn_w_out)
    mom_v = dict(meta_tokens=v_meta_tokens, attn_norm=v_attn_norm, ffn_norm=v_ffn_norm, final_norm=v_final_norm,
                 fox_w_in=v_fox_w_in, fox_b_f=v_fox_b_f, fox_q_norm=v_fox_q_norm, fox_k_norm=v_fox_k_norm,
                 fox_w_out=v_fox_w_out, hgrn_w_in=v_hgrn_w_in, hgrn_lower_bounds=v_hgrn_lower_bounds,
                 hgrn_g_norm=v_hgrn_g_norm, hgrn_w_out=v_hgrn_w_out, ffn_w_in=v_ffn_w_in, ffn_w_out=v_ffn_w_out)
    names = list(params)
    seq = x.shape[1]
    xi, yi, ci = _me()

    shards = [_shard2d(n, params[n]).astype(BF16) for n in BIG_NAMES] + [meta_tokens]
    gathered = _chip_allgather(shards)
    w = {n: _unshard(n, g, params[n].shape[0]) for n, g in zip(BIG_NAMES, gathered[:-1])}
    w = {n: (a[0] if a.shape[0] == 1 else a) for n, a in w.items()}
    meta_full = gathered[-1].transpose(1, 0, 2).reshape(N_META, D)
    w.update(attn_norm=attn_norm, ffn_norm=ffn_norm, final_norm=final_norm.reshape(1, D), fox_b_f=fox_b_f,
             fox_q_norm=fox_q_norm, fox_k_norm=fox_k_norm, hgrn_lower_bounds=hgrn_lower_bounds,
             hgrn_g_norm=hgrn_g_norm)

    h0 = jnp.concatenate([jnp.zeros((ROW0, D), F32), meta_full, x[0]], axis=0)
    tgt = jnp.concatenate([jnp.zeros((PAD, D), F32), loss_target[0]], axis=0)
    loss, dh0, big, small = _local_step(h0, tgt, w)
    loss = lax.psum(loss, ("x", "y", "c"))
    grad_x = dh0[PAD:][None]

    gsh = [_to_shards(n, big[n] if big[n].ndim == 3 else big[n][None]) for n in BIG_NAMES]
    red = _reduce_scatter(gsh)
    grads = {n: r.reshape(params[n].shape) for n, r in zip(BIG_NAMES, red)}

    rows = jnp.concatenate([small["attn_norm"], small["ffn_norm"], small["final_norm"], small["lb_raw"],
                            small["q_gain"], small["k_gain"],
                            jnp.pad(small["b_f"], ((0, 0), (0, D - LANES))), small["g_gain"],
                            dh0[ROW0:PAD], jnp.zeros((SMALL_ROWS - 10 - N_META, D), F32)], axis=0)
    allrows = _device_allgather(rows)
    fold64 = jnp.asarray(np.tile(np.eye(FOX_DH, dtype=np.float32), (FOX_H, 1)))
    fold128 = jnp.asarray(np.tile(np.eye(HG_D, dtype=np.float32), (HG_H, 1)))
    tot, qk, gg, dlb = _small_finalize(allrows, hgrn_lower_bounds, fold64, fold128)
    grads.update(attn_norm=tot[0:2], ffn_norm=tot[2:4], final_norm=tot[4], hgrn_lower_bounds=dlb,
                 fox_q_norm=qk[0:1], fox_k_norm=qk[1:2], fox_b_f=tot[8:9, :FOX_H], hgrn_g_norm=gg,
                 meta_tokens=lax.dynamic_slice_in_dim(tot[10:10 + N_META], (2 * xi + yi) * (D // 4), D // 4, axis=1))

    delta, new_m, new_v = {}, {}, {}
    for n in BIG_NAMES + ("meta_tokens",):
        d_, m_, v_ = _adamw(_shard2d(n, params[n]), _shard2d(n, grads[n]), _shard2d(n, mom_m[n]),
                            _shard2d(n, mom_v[n]), f"adamw_{n}")
        delta[n], new_m[n], new_v[n] = (t.reshape(params[n].shape) for t in (d_, m_, v_))
    small_names = [n for n in names if n not in BIG_NAMES and n != "meta_tokens"]

    def pack(d):
        return jnp.concatenate([jnp.pad(d[n].reshape(-1, d[n].shape[-1]), ((0, 0), (0, D - d[n].shape[-1])))
                                for n in small_names], axis=0)

    packed = [pack(t) for t in (params, grads, mom_m, mom_v)]
    n_rows = packed[0].shape[0]
    packed = [jnp.pad(t, ((0, 16 - n_rows), (0, 0))) for t in packed]
    res = _adamw(*packed, "adamw_small")
    r0 = 0
    for n in small_names:
        nr = params[n].reshape(-1, params[n].shape[-1]).shape[0]
        for dst, src in zip((delta, new_m, new_v), res):
            dst[n] = src[r0:r0 + nr, :params[n].shape[-1]].reshape(params[n].shape)
        r0 += nr

    return (loss, grad_x, *[grads[n] for n in names], *[delta[n] for n in names],
            *[new_m[n] for n in names], *[new_v[n] for n in names])
```

```python
import functools

import numpy as np
import jax
import jax.numpy as jnp
from jax import lax
from jax.experimental import pallas as pl
from jax.experimental.pallas import tpu as pltpu

F32, BF16 = jnp.float32, jnp.bfloat16
HIGHEST = lax.Precision.HIGHEST

D = 1024
N_META = 16
PAD = 128
ROW0 = PAD - N_META
FOX_H, FOX_DH = 16, 64
HG_H, HG_D = 8, 128
HG_C = 128
HG_LEV = 7
FFN = 2816
EPS = 1e-6
BIG = 1e30
LOG2E = 1.4426950408889634
LANES = 128
VMEM_LIMIT = 48 * 1024 * 1024
ROW_TILES = (640, 512, 384, 320, 256, 128, 64, 32, 16, 8)

ADAM_LR, ADAM_B1, ADAM_B2, ADAM_EPS, ADAM_WD, ADAM_STEP = 0.001, 0.9, 0.999, 1e-08, 0.01, 10

MESH = pl.DeviceIdType.MESH
ANY = pl.BlockSpec(memory_space=pl.ANY)
NT = (((1,), (1,)), ((), ()))
TN = (((0,), (0,)), ((), ()))


def _tile(n, cands=ROW_TILES, cap=None):
    for c in cands:
        if n % c == 0 and (cap is None or c <= cap):
            return c
    return n


def _cparams(sem):
    return pltpu.CompilerParams(dimension_semantics=sem, vmem_limit_bytes=VMEM_LIMIT)


def _sigmoid(x):
    return jax.nn.sigmoid(x)


def _log_sigmoid(x):
    return jnp.minimum(x, 0.0) - jnp.log(1.0 + jnp.exp(-jnp.abs(x)))


def _iota(shape, dim):
    return lax.broadcasted_iota(jnp.int32, shape, dim)


def _matmul(a, b, *, ta=False, tb=False, out_dtype=F32, add=None, name):
    if ta:
        kdim, m = a.shape
    else:
        m, kdim = a.shape
    n = b.shape[0] if tb else b.shape[1]
    if ta:
        tm = m if m <= 1024 else _tile(m, (1408, 1024, 512, 256, 128))
        tk = _tile(kdim)
    else:
        tm = _tile(m)
        tk = kdim if kdim <= 4096 else _tile(kdim, (2048, 1024, 512))
    tn = n if n <= 1024 else _tile(n, (1408, 1024, 512, 256, 128))
    nk = kdim // tk
    dn = (((0 if ta else 1,), (1 if tb else 0,)), ((), ()))

    def body(*refs):
        if add is None:
            a_ref, b_ref, o_ref, acc_ref = refs
        else:
            a_ref, b_ref, add_ref, o_ref, acc_ref = refs
        k = pl.program_id(2)

        @pl.when(k == 0)
        def _():
            acc_ref[...] = jnp.zeros_like(acc_ref)

        acc_ref[...] += lax.dot_general(a_ref[...].astype(BF16), b_ref[...].astype(BF16), dn,
                                        preferred_element_type=F32)

        @pl.when(k == nk - 1)
        def _():
            r = acc_ref[...]
            if add is not None:
                r = r + add_ref[...].astype(F32)
            o_ref[...] = r.astype(o_ref.dtype)

    a_spec = pl.BlockSpec((tk, tm), lambda j, i, k: (k, i)) if ta else pl.BlockSpec((tm, tk), lambda j, i, k: (i, k))
    b_spec = pl.BlockSpec((tn, tk), lambda j, i, k: (j, k)) if tb else pl.BlockSpec((tk, tn), lambda j, i, k: (k, j))
    o_spec = pl.BlockSpec((tm, tn), lambda j, i, k: (i, j))
    ins, specs = [a, b], [a_spec, b_spec]
    if add is not None:
        ins.append(add)
        specs.append(o_spec)
    return pl.pallas_call(
        body, name=name, grid=(n // tn, m // tm, nk), in_specs=specs, out_specs=o_spec,
        out_shape=jax.ShapeDtypeStruct((m, n), out_dtype),
        scratch_shapes=[pltpu.VMEM((tm, tn), F32)],
        compiler_params=_cparams(("parallel", "parallel", "arbitrary")),
    )(*ins)


def _rowwise(fn, ins, bcast, outs, accs, *, name, reverse=False, carry=None, as_refs=False):
    rows = ins[0].shape[0]
    per_row = sum(x.shape[1] * x.dtype.itemsize for x in ins) + sum(c * jnp.dtype(d).itemsize for c, d in outs)
    tm = _tile(rows, cap=max(8, (10 * 1024 * 1024) // per_row))
    n = rows // tm
    n_in, n_b, n_o, n_a = len(ins), len(bcast), len(outs), len(accs)

    def body(*refs):
        in_refs = refs[:n_in]
        b_refs = refs[n_in:n_in + n_b]
        o_refs = refs[n_in + n_b:n_in + n_b + n_o]
        a_refs = refs[n_in + n_b + n_o:n_in + n_b + n_o + n_a]
        c_refs = refs[n_in + n_b + n_o + n_a:]
        i = pl.program_id(0)
        blk = (n - 1 - i) if reverse else i
        if c_refs:
            @pl.when(i == 0)
            def _():
                c_refs[0][...] = jnp.zeros_like(c_refs[0])
        args = (list(in_refs) if as_refs else [r[...] for r in in_refs], [r[...] for r in b_refs])
        o_vals, a_vals = fn(blk * tm, *args, *c_refs)
        for r, v in zip(o_refs, o_vals):
            r[...] = v.astype(r.dtype)
        if n_a:
            @pl.when(i == 0)
            def _():
                for r in a_refs:
                    r[...] = jnp.zeros_like(r)
            for r, v in zip(a_refs, a_vals):
                r[...] += v

    def row_map(i):
        return ((n - 1 - i) if reverse else i, 0)

    in_specs = [pl.BlockSpec((tm, x.shape[1]), row_map) for x in ins]
    in_specs += [pl.BlockSpec(x.shape, lambda i, nd=x.ndim: (0,) * nd) for x in bcast]
    out_specs = [pl.BlockSpec((tm, c), row_map) for c, _ in outs]
    out_specs += [pl.BlockSpec(s, lambda i: (0, 0)) for s in accs]
    out_shape = [jax.ShapeDtypeStruct((rows, c), d) for c, d in outs]
    out_shape += [jax.ShapeDtypeStruct(s, F32) for s in accs]
    res = pl.pallas_call(
        body, name=name, grid=(n,), in_specs=in_specs, out_specs=out_specs, out_shape=out_shape,
        scratch_shapes=[pltpu.VMEM(carry, F32)] if carry else [],
        compiler_params=_cparams(("arbitrary",)),
    )(*ins, *bcast)
    return res[:n_o], res[n_o:]


def _row_ids(row0, tm):
    return row0 + _iota((tm, 1), 0)


def _rms_fwd(x, gain, name):
    def fn(row0, ins, bc):
        (xv,), (g,) = ins, bc
        r = lax.rsqrt(jnp.mean(xv * xv, axis=-1, keepdims=True) + EPS)
        return [xv * r * g], []
    return _rowwise(fn, [x], [gain], [(D, BF16)], [], name=name)[0][0]


def _rms_bwd(x, dxn, gain, dh_up, name):
    def fn(row0, ins, bc):
        xv, dy, up = ins
        (g,) = bc
        dy = dy.astype(F32)
        r = lax.rsqrt(jnp.mean(xv * xv, axis=-1, keepdims=True) + EPS)
        xh = xv * r
        dxh = dy * g
        dx = r * (dxh - xh * jnp.mean(dxh * xh, axis=-1, keepdims=True))
        keep = _row_ids(row0, xv.shape[0]) >= ROW0
        return [jnp.where(keep, up + dx, 0.0)], [jnp.sum(dy * xh, axis=0, keepdims=True)]
    (dh,), (dgain,) = _rowwise(fn, [x, dxn, dh_up], [gain], [(D, F32)], [(1, D)], name=name)
    return dh, dgain


def _loss_bwd(h, tgt, gain):
    def fn(row0, ins, bc):
        xv, t = ins
        (g,) = bc
        r = lax.rsqrt(jnp.mean(xv * xv, axis=-1, keepdims=True) + EPS)
        xh = xv * r
        keep = _row_ids(row0, xv.shape[0]) >= PAD
        err = jnp.where(keep, xh * g - t, 0.0)
        per_row = jnp.mean(err * err, axis=-1, keepdims=True)
        loss = 0.5 * jnp.sum(per_row, axis=0, keepdims=True)
        dy = err * (1.0 / D)
        dxh = dy * g
        dx = r * (dxh - xh * jnp.mean(dxh * xh, axis=-1, keepdims=True))
        return [dx], [jnp.broadcast_to(loss, (1, LANES)), jnp.sum(dy * xh, axis=0, keepdims=True)]
    (dh,), (loss, dgain) = _rowwise(fn, [h, tgt], [gain], [(D, F32)], [(1, LANES), (1, D)], name="loss_bwd")
    return loss[0, 0], dh, dgain


def _swiglu_fwd(g, u, name):
    def fn(row0, ins, bc):
        gv, uv = (v.astype(F32) for v in ins)
        return [gv * _sigmoid(gv) * uv], []
    return _rowwise(fn, [g, u], [], [(g.shape[1], BF16)], [], name=name)[0][0]


def _swiglu_bwd(dact, g, u, name):
    def fn(row0, ins, bc):
        da, gv, uv = (v.astype(F32) for v in ins)
        s = _sigmoid(gv)
        return [da * uv * (s * (1.0 + gv * (1.0 - s))), da * gv * s], []
    n = g.shape[1]
    return _rowwise(fn, [dact, g, u], [], [(n, BF16), (n, BF16)], [], name=name)[0]


def _adamw(w, g, m, v, name):
    def fn(row0, ins, bc):
        wv, gv, mv, vv = ins
        mn = ADAM_B1 * mv + (1.0 - ADAM_B1) * gv
        vn = ADAM_B2 * vv + (1.0 - ADAM_B2) * (gv * gv)
        m_hat = mn / (1.0 - ADAM_B1 ** ADAM_STEP)
        v_hat = vn / (1.0 - ADAM_B2 ** ADAM_STEP)
        delta = -ADAM_LR * (m_hat / (jnp.sqrt(v_hat) + ADAM_EPS) + ADAM_WD * wv)
        return [delta, mn, vn], []
    c = w.shape[1]
    return _rowwise(fn, [w, g, m, v], [], [(c, F32)] * 3, [], name=name)[0]


def _add(xs, name):
    def fn(row0, ins, bc):
        r = ins[0].astype(F32)
        for v in ins[1:]:
            r = r + v.astype(F32)
        return [r], []
    return _rowwise(fn, list(xs), [], [(xs[0].shape[1], F32)], [], name=name)[0][0]


def _head_sum(x, gmat):
    hi = x.astype(BF16)
    lo = (x - hi.astype(F32)).astype(BF16)
    return jnp.dot(hi, gmat, preferred_element_type=F32) + jnp.dot(lo, gmat, preferred_element_type=F32)


def _split3(x):
    hi = x.astype(BF16).astype(F32)
    r = x - hi
    mid = r.astype(BF16).astype(F32)
    return hi, mid, r - mid


def _extra_base(hh):
    return FOX_DH * (1 - hh)


def _data_mask(hh):
    lane = _iota((1, LANES), 1)
    return (lane >= FOX_DH * hh) & (lane < FOX_DH * (hh + 1))


def _with_extras(data, hh, vals):
    lane = _iota((1, LANES), 1)
    x = jnp.zeros_like(data)
    for e, v in enumerate(vals):
        x = jnp.where(lane == _extra_base(hh) + e, v, x)
    return jnp.where(_data_mask(hh), data, x)


def _fox_pack_fwd(q_raw, k_raw, v, cq, ck, qg, kg, gmat):
    scale2 = FOX_DH ** -0.5 * LOG2E

    def fn(row0, refs, bc):
        q_ref, k_ref, v_ref, cq_ref, ck_ref = refs
        g_q, g_k, gm = bc
        qv, kv = q_ref[...], k_ref[...]
        qn = qv * lax.rsqrt(_head_sum(qv * qv, gm) * (1.0 / FOX_DH) + EPS) * (g_q * scale2)
        kn = kv * lax.rsqrt(_head_sum(kv * kv, gm) * (1.0 / FOX_DH) + EPS) * g_k
        qs, ks, vs = [], [], []
        for h in range(FOX_H):
            p, hh = divmod(h, 2)
            sl = slice(p * LANES, (p + 1) * LANES)
            cq3 = _split3(cq_ref[:, h:h + 1] * LOG2E)
            ck3 = _split3(ck_ref[:, h:h + 1] * (-LOG2E))
            qs.append(_with_extras(qn[:, sl], hh, [*cq3, 1.0, 1.0, 1.0]))
            ks.append(_with_extras(kn[:, sl], hh, [1.0, 1.0, 1.0, *ck3]))
            vs.append(_with_extras(v_ref[:, sl].astype(F32), hh, [1.0, 1.0]))
        return [jnp.concatenate(qs, axis=1), jnp.concatenate(ks, axis=1), jnp.concatenate(vs, axis=1)], []

    w = FOX_H * LANES
    return _rowwise(fn, [q_raw, k_raw, v, cq, ck], [qg, kg, gmat], [(w, BF16)] * 3, [], name="fox_pack_fwd",
                    as_refs=True)[0]


def _fox_pack_bias(qp, cq, lse2):
    def fn(row0, refs, bc):
        q_ref, cq_ref, lse_ref = refs
        lane = _iota((1, LANES), 1)
        outs = []
        for h in range(FOX_H):
            blk = q_ref[:, h * LANES:(h + 1) * LANES].astype(F32)
            for e, part in enumerate(_split3(cq_ref[:, h:h + 1] * LOG2E - lse_ref[:, h:h + 1])):
                blk = jnp.where(lane == _extra_base(h % 2) + e, part, blk)
            outs.append(blk)
        return [jnp.concatenate(outs, axis=1)], []
    return _rowwise(fn, [qp, cq, lse2], [], [(FOX_H * LANES, BF16)], [], name="fox_pack_bias", as_refs=True)[0][0]


def _fox_pack_bwd(dog, o, gate):
    def fn(row0, refs, bc):
        d_ref, o_ref, g_ref = refs
        dos, dgs = [], []
        for p in range(FOX_H // 2):
            sl = slice(p * LANES, (p + 1) * LANES)
            dv, ov, gv = (r[:, sl].astype(F32) for r in (d_ref, o_ref, g_ref))
            s = _sigmoid(gv)
            do = dv * s
            dgs.append(dv * ov * s * (1.0 - s))
            od = ov * do
            for hh in range(2):
                delta = jnp.sum(jnp.where(_data_mask(hh), od, 0.0), axis=-1, keepdims=True)
                hi = delta.astype(BF16).astype(F32)
                dos.append(_with_extras(do, hh, [-hi, hi - delta]))
        return [jnp.concatenate(dos, axis=1), jnp.concatenate(dgs, axis=1)], []
    return _rowwise(fn, [dog, o, gate], [], [(FOX_H * LANES, BF16), (D, BF16)], [], name="fox_pack_bwd",
                    as_refs=True)[0]


def _fox_unpack_bwd(q_raw, k_raw, dqp, dk, qg, kg, gmat):
    scale = FOX_DH ** -0.5

    def fn(row0, refs, bc):
        q_ref, k_ref, dq_ref, dk_ref = refs
        g_q, g_k, gm = bc
        lane = _iota((1, LANES), 1)
        dqs = []
        dcq = jnp.zeros((q_ref.shape[0], LANES), F32)
        for p in range(FOX_H // 2):
            even = dq_ref[:, (2 * p) * LANES:(2 * p + 1) * LANES]
            odd = dq_ref[:, (2 * p + 1) * LANES:(2 * p + 2) * LANES]
            dqs.append(jnp.where(_data_mask(0), even, odd) * scale)
            for hh in range(2):
                col = (2 * p + hh) * LANES + _extra_base(hh)
                dcq = jnp.where(lane == 2 * p + hh, dq_ref[:, col:col + 1], dcq)
        outs, accs = [], []
        for xv, dy, g in ((q_ref[...], jnp.concatenate(dqs, axis=1), g_q), (k_ref[...], dk_ref[...] * (1.0 / LOG2E), g_k)):
            r = lax.rsqrt(_head_sum(xv * xv, gm) * (1.0 / FOX_DH) + EPS)
            xh = xv * r
            dxh = dy * g
            outs.append(r * (dxh - xh * (_head_sum(dxh * xh, gm) * (1.0 / FOX_DH))))
            accs.append(jnp.sum(dy * xh, axis=0, keepdims=True))
        return outs + [dcq], accs
    return _rowwise(fn, [q_raw, k_raw, dqp, dk], [qg, kg, gmat], [(D, BF16), (D, BF16), (LANES, F32)],
                    [(1, D), (1, D)], name="fox_unpack_bwd", as_refs=True)


def _fox_cumsum_fwd(flog, bf):
    def fn(row0, ins, bc, carry):
        (f,), (b,) = ins, bc
        tm = f.shape[0]
        keep = _row_ids(row0, tm) >= ROW0
        lf = jnp.where(keep, _log_sigmoid(f + b), 0.0)
        tri = (_iota((tm, tm), 0) >= _iota((tm, tm), 1)).astype(F32)
        c = jnp.dot(tri, lf, precision=HIGHEST, preferred_element_type=F32) + carry[...]
        carry[...] = carry[...] + jnp.sum(lf, axis=0, keepdims=True)
        return [c, jnp.where(keep, c, BIG)], []
    return _rowwise(fn, [flog], [bf], [(LANES, F32), (LANES, F32)], [], name="fox_cumsum_fwd",
                    carry=(1, LANES))[0]


def _fox_cumsum_bwd(dc_q, dc_k, flog, bf):
    def fn(row0, ins, bc, carry):
        (dq, dk, f), (b,) = ins, bc
        d = dq + dk
        tm = f.shape[0]
        keep = _row_ids(row0, tm) >= ROW0
        triu = (_iota((tm, tm), 0) <= _iota((tm, tm), 1)).astype(F32)
        dlf = jnp.dot(triu, d, precision=HIGHEST, preferred_element_type=F32) + carry[...]
        carry[...] = carry[...] + jnp.sum(d, axis=0, keepdims=True)
        dfl = jnp.where(keep, dlf * _sigmoid(-(f + b)), 0.0)
        return [dfl], [jnp.sum(dfl, axis=0, keepdims=True)]
    (dflog,), (dbf,) = _rowwise(fn, [dc_q, dc_k, flog], [bf], [(LANES, F32)], [(1, LANES)], name="fox_cumsum_bwd",
                                reverse=True, carry=(1, LANES))
    return dflog, dbf


def _causal_steps(n, key_major):
    if key_major:
        pairs = [(i, j) for j in range(n) for i in range(j, n)]
    else:
        pairs = [(i, j) for i in range(n) for j in range(i + 1)]
    return (jnp.asarray(np.array([p[0] for p in pairs], np.int32)),
            jnp.asarray(np.array([p[1] for p in pairs], np.int32)))


def _fox_attn_fwd(qp, kp, vp, gate, shards):
    L = qp.shape[0]
    t = _tile(L, (640, 512, 256, 128))
    n = L // t
    P = FOX_H // 2
    it, jt = _causal_steps(n, False)
    n_steps = it.shape[0]
    ns = len(shards)

    def body(it_ref, jt_ref, q_ref, k_ref, v_ref, g_ref, *rest):
        sh_in, (o_ref, og_ref, lse_ref), sh_out = rest[:ns], rest[ns:ns + 3], rest[ns + 3:2 * ns + 3]
        m_sc, acc, ssem, rsem, lsem = rest[2 * ns + 3:]
        step = pl.program_id(1)
        i, j = it_ref[step], jt_ref[step]
        first = (pl.program_id(0) == 0) & (step == 0)
        last = (pl.program_id(0) == P - 1) & (step == n_steps - 1)

        @pl.when(first)
        def _():
            _gather_start(_gather_copies(sh_in, sh_out, ssem, rsem, lsem))

        @pl.when(j == 0)
        def _():
            m_sc[...] = jnp.full_like(m_sc, -3.0e38)
            acc[...] = jnp.zeros_like(acc)

        def update(masked):
            for hh in range(2):
                sl = slice(hh * LANES, (hh + 1) * LANES)
                s2 = lax.dot_general(q_ref[:, sl], k_ref[:, sl], NT, preferred_element_type=F32)
                if masked:
                    s2 = jnp.where(_iota((t, t), 1) <= _iota((t, t), 0), s2, -jnp.inf)
                m_old = m_sc[hh]
                m_new = jnp.maximum(m_old, jnp.max(s2, axis=-1, keepdims=True))
                p = jnp.exp2(s2 - m_new)
                acc[hh] = jnp.exp2(m_old - m_new) * acc[hh] + jnp.dot(p.astype(BF16), v_ref[:, sl],
                                                                      preferred_element_type=F32)
                m_sc[hh] = m_new

        @pl.when(j < i)
        def _():
            update(False)

        @pl.when(j == i)
        def _():
            update(True)
            outs, lses = [], []
            for hh in range(2):
                l = acc[hh, :, _extra_base(hh):_extra_base(hh) + 1]
                outs.append(acc[hh] / l)
                lses.append(m_sc[hh] + jnp.log2(l))
            o = jnp.where(_data_mask(0), outs[0], outs[1])
            o_ref[...] = o.astype(o_ref.dtype)
            og_ref[...] = (o * _sigmoid(g_ref[...].astype(F32))).astype(og_ref.dtype)
            lse_ref[0] = jnp.where(_iota((1, 2), 1) == 0, lses[0], lses[1])

        @pl.when(last)
        def _():
            _gather_wait(_gather_copies(sh_in, sh_out, ssem, rsem, lsem))

    qspec = pl.BlockSpec((t, 2 * LANES), lambda p, s, it, jt: (it[s], p))
    kspec = pl.BlockSpec((t, 2 * LANES), lambda p, s, it, jt: (jt[s], p))
    ospec = pl.BlockSpec((t, LANES), lambda p, s, it, jt: (it[s], p))
    lspec = pl.BlockSpec((1, t, 2), lambda p, s, it, jt: (p, it[s], 0))
    res = pl.pallas_call(
        body, name="fox_attn_fwd",
        grid_spec=pltpu.PrefetchScalarGridSpec(
            num_scalar_prefetch=2, grid=(P, n_steps),
            in_specs=[qspec, kspec, kspec, ospec] + [ANY] * ns, out_specs=[ospec, ospec, lspec] + [ANY] * ns,
            scratch_shapes=[pltpu.VMEM((2, t, 1), F32), pltpu.VMEM((2, t, LANES), F32)] + _gather_sems(ns)),
        out_shape=[jax.ShapeDtypeStruct((L, D), BF16), jax.ShapeDtypeStruct((L, D), BF16),
                   jax.ShapeDtypeStruct((P, L, 2), F32)]
        + [jax.ShapeDtypeStruct((4,) + a.shape, a.dtype) for a in shards],
        compiler_params=_cparams(("arbitrary", "arbitrary")),
    )(it, jt, qp, kp, vp, gate, *shards)
    return res[0], res[1], res[2], res[3:]


def _fox_attn_bwd(qb, kp, vp, dop, slabs):
    L = qb.shape[0]
    t = _tile(L, (640, 512, 256, 128))
    n = L // t
    P = FOX_H // 2
    it, jt = _causal_steps(n, True)
    n_steps = it.shape[0]
    ns = len(slabs)

    def body(it_ref, jt_ref, q_ref, k_ref, v_ref, do_ref, *rest):
        sl_in, (dq_ref, dk_ref, dv_ref, dck_ref), sl_out = rest[:ns], rest[ns:ns + 4], rest[ns + 4:2 * ns + 4]
        dk_acc, dv_acc, ssem, rsem = rest[2 * ns + 4:]
        step = pl.program_id(1)
        i, j = it_ref[step], jt_ref[step]

        @pl.when((pl.program_id(0) == 0) & (step == 0))
        def _():
            for cp in _scatter_copies(sl_in, sl_out, ssem, rsem):
                cp.start()

        @pl.when(step == 0)
        def _():
            dq_ref[...] = jnp.zeros_like(dq_ref)

        @pl.when(i == j)
        def _():
            dk_acc[...] = jnp.zeros_like(dk_acc)
            dv_acc[...] = jnp.zeros_like(dv_acc)

        def update(masked):
            rows = pl.ds(pl.multiple_of(i * t, LANES), t)
            for hh in range(2):
                sl = slice(hh * LANES, (hh + 1) * LANES)
                q, k, dov = q_ref[:, sl], k_ref[:, sl], do_ref[:, sl]
                s2 = lax.dot_general(q, k, NT, preferred_element_type=F32)
                if masked:
                    s2 = jnp.where(_iota((t, t), 1) <= _iota((t, t), 0), s2, -jnp.inf)
                p = jnp.exp2(s2)
                ds = (p * lax.dot_general(dov, v_ref[:, sl], NT, preferred_element_type=F32)).astype(BF16)
                dv_acc[hh] += lax.dot_general(p.astype(BF16), dov, TN, preferred_element_type=F32)
                dk_acc[hh] += lax.dot_general(ds, q, TN, preferred_element_type=F32)
                dq_ref[rows, sl] += jnp.dot(ds, k, preferred_element_type=F32)

        @pl.when(i > j)
        def _():
            update(False)

        @pl.when(i == j)
        def _():
            update(True)

        @pl.when(i == n - 1)
        def _():
            dk_ref[...] = jnp.where(_data_mask(0), dk_acc[0], dk_acc[1])
            dv_ref[...] = jnp.where(_data_mask(0), dv_acc[0], dv_acc[1]).astype(dv_ref.dtype)
            col_sums = [dk_acc[hh, :, _extra_base(hh) + 3:_extra_base(hh) + 4] for hh in range(2)]
            dck_ref[0] = -jnp.where(_iota((1, 2), 1) == 0, col_sums[0], col_sums[1])

        @pl.when((pl.program_id(0) == P - 1) & (step == n_steps - 1))
        def _():
            for cp in _scatter_copies(sl_in, sl_out, ssem, rsem):
                cp.wait()

    qspec = pl.BlockSpec((t, 2 * LANES), lambda p, s, it, jt: (it[s], p))
    kspec = pl.BlockSpec((t, 2 * LANES), lambda p, s, it, jt: (jt[s], p))
    ospec = pl.BlockSpec((t, LANES), lambda p, s, it, jt: (jt[s], p))
    res = pl.pallas_call(
        body, name="fox_attn_bwd",
        grid_spec=pltpu.PrefetchScalarGridSpec(
            num_scalar_prefetch=2, grid=(P, n_steps),
            in_specs=[qspec, kspec, kspec, qspec] + [ANY] * ns,
            out_specs=[pl.BlockSpec((L, 2 * LANES), lambda p, s, it, jt: (0, p)), ospec, ospec,
                       pl.BlockSpec((1, t, 2), lambda p, s, it, jt: (p, jt[s], 0))] + [ANY] * ns,
            scratch_shapes=[pltpu.VMEM((2, t, LANES), F32), pltpu.VMEM((2, t, LANES), F32),
                            pltpu.SemaphoreType.DMA((ns, 3)), pltpu.SemaphoreType.DMA((ns, 3))]),
        out_shape=[jax.ShapeDtypeStruct((L, FOX_H * LANES), F32), jax.ShapeDtypeStruct((L, D), F32),
                   jax.ShapeDtypeStruct((L, D), BF16), jax.ShapeDtypeStruct((P, L, 2), F32)]
        + [jax.ShapeDtypeStruct((3,) + a.shape[1:], a.dtype) for a in slabs],
        compiler_params=_cparams(("arbitrary", "arbitrary")),
    )(it, jt, qb, kp, vp, dop, *slabs)
    return res[0], res[1], res[2], res[3], res[4:]


def _hgrn_consts():
    C = HG_C
    r = np.arange(C)[:, None]
    j = np.arange(C)[None, :]
    mats = [j <= r, j > r]
    masks = []
    n = C
    while n >= 2:
        half = n // 2
        mid = (r // n) * n + half - 1
        second = (r % n) >= half
        mats.append(np.where(second, (j > mid) & (j <= r), (j > r) & (j <= mid)))
        masks.append(((r // n) == (j // n)) & ((r % n) >= half) & ((j % n) < half))
        n //= 2
    return (jnp.asarray(np.concatenate(mats, 0).astype(np.float32), BF16),
            jnp.asarray(np.stack(masks).astype(np.float32), F32))


def _hg_pre(hq, hz, hlb_ref):
    h0, h1 = hlb_ref[0:1, :], hlb_ref[1:2, :]
    mx = jnp.maximum(h0, h1)
    e0, e1 = jnp.exp(h0 - mx), jnp.exp(h1 - mx)
    lb = e1 / (e0 + e1)
    sq = _sigmoid(hq)
    sz, snz = _sigmoid(hz), _sigmoid(-hz)
    k = (1.0 - lb) * snz
    a = jnp.log(lb)
    b = jnp.log1p(-lb) + _log_sigmoid(hz)
    g = jnp.maximum(a, b) + jnp.log(1.0 + jnp.exp(-jnp.abs(a - b)))
    return lb, hq * sq, sq, k, sz, snz, g


def _hg_decays(g, rmat):
    hi = g.astype(BF16)
    lo = (g - hi.astype(F32)).astype(BF16)
    d = jnp.dot(rmat, jnp.concatenate([hi, lo], axis=1), preferred_element_type=F32)
    return jnp.exp(d[:, :HG_D] + d[:, HG_D:])


def _hg_intra(q, k, fall, masks):
    C = HG_C
    eye = _iota((C, C), 0) == _iota((C, C), 1)
    a = jnp.where(eye, jnp.sum(q * k, axis=-1, keepdims=True), 0.0)
    for l in range(HG_LEV):
        f = fall[(2 + l) * C:(3 + l) * C]
        a = a + masks[l] * lax.dot_general((q * f).astype(BF16), (k * f).astype(BF16), NT,
                                           preferred_element_type=F32)
    return a


def _hgrn_specs(n_chunks, reverse):
    C = HG_C

    def col(off):
        if reverse:
            return pl.BlockSpec((C, HG_D), lambda h, c: (n_chunks - 1 - c, off + h))
        return pl.BlockSpec((C, HG_D), lambda h, c: (c, off + h))

    st = pl.BlockSpec((1, 1, HG_D, HG_D),
                      (lambda h, c: (h, n_chunks - 1 - c, 0, 0)) if reverse else (lambda h, c: (h, c, 0, 0)))
    consts = [pl.BlockSpec((2, HG_D), lambda h, c: (0, h)), pl.BlockSpec((1, HG_D), lambda h, c: (0, 0)),
              pl.BlockSpec(((2 + HG_LEV) * C, C), lambda h, c: (0, 0)),
              pl.BlockSpec((HG_LEV, C, C), lambda h, c: (0, 0, 0))]
    return col, st, consts


def _hgrn_fwd(proj, hlb, gg, rmat, masks):
    L = proj.shape[0]
    C = HG_C
    nc = L // C
    col, st, consts = _hgrn_specs(nc, False)

    def body(hq_ref, hz_ref, hi_ref, hg_ref, hlb_ref, gg_ref, r_ref, m_ref, og_ref, st_ref, state):
        c = pl.program_id(1)

        @pl.when(c == 0)
        def _():
            state[...] = jnp.zeros_like(state)

        v, hg = hi_ref[...], hg_ref[...]
        _, q, _, k, _, _, g = _hg_pre(hq_ref[...], hz_ref[...], hlb_ref)
        fall = _hg_decays(g, r_ref[...])
        fb, fe = fall[0:C], fall[C:2 * C]
        st0 = state[...]
        st_ref[0, 0] = st0
        a = _hg_intra(q, k, fall, m_ref[...])
        vb = v.astype(BF16)
        o = jnp.dot(a.astype(BF16), vb, preferred_element_type=F32)
        o = o + lax.dot_general((q * fb).astype(BF16), st0.astype(BF16), NT, preferred_element_type=F32)
        ebc = jnp.exp(jnp.sum(g, axis=0, keepdims=True))
        state[...] = st0 * ebc + lax.dot_general(vb, (k * fe).astype(BF16), TN, preferred_element_type=F32)
        r = lax.rsqrt(jnp.mean(o * o, axis=-1, keepdims=True) + EPS)
        og_ref[...] = (o * r * gg_ref[...] * (hg * _sigmoid(hg))).astype(og_ref.dtype)

    return pl.pallas_call(
        body, name="hgrn_fwd", grid=(HG_H, nc),
        in_specs=[col(0), col(HG_H), col(2 * HG_H), col(3 * HG_H)] + consts,
        out_specs=[col(0), st],
        out_shape=[jax.ShapeDtypeStruct((L, D), BF16), jax.ShapeDtypeStruct((HG_H, nc, HG_D, HG_D), F32)],
        scratch_shapes=[pltpu.VMEM((HG_D, HG_D), F32)],
        compiler_params=_cparams(("parallel", "arbitrary")),
    )(proj, proj, proj, proj, hlb, gg, rmat, masks)


def _hgrn_bwd(proj, dog, states, hlb, gg, rmat, masks):
    L = proj.shape[0]
    C = HG_C
    nc = L // C
    col, st, consts = _hgrn_specs(nc, True)

    def body(hq_ref, hz_ref, hi_ref, hg_ref, do_ref, hlb_ref, gg_ref, r_ref, m_ref, st_ref,
             dq_ref, dz_ref, di_ref, dg_ref, dlb_ref, dgg_ref, dstate):
        c = pl.program_id(1)

        @pl.when(c == 0)
        def _():
            dstate[...] = jnp.zeros_like(dstate)
            dlb_ref[...] = jnp.zeros_like(dlb_ref)
            dgg_ref[...] = jnp.zeros_like(dgg_ref)

        hq, hz, v, hg = hq_ref[...], hz_ref[...], hi_ref[...], hg_ref[...]
        dout = do_ref[...].astype(F32)
        gain = gg_ref[...]
        masks_v = m_ref[...]
        lb, q, sq, k, sz, snz, g = _hg_pre(hq, hz, hlb_ref)
        fall = _hg_decays(g, r_ref[...])
        fb, fe = fall[0:C], fall[C:2 * C]
        a = _hg_intra(q, k, fall, masks_v)
        st0 = st_ref[0, 0]
        st0b = st0.astype(BF16)
        ebc = jnp.exp(jnp.sum(g, axis=0, keepdims=True))
        qb, ke, vb = (q * fb).astype(BF16), (k * fe).astype(BF16), v.astype(BF16)
        ab = a.astype(BF16)
        o = jnp.dot(ab, vb, preferred_element_type=F32) + lax.dot_general(qb, st0b, NT, preferred_element_type=F32)
        r = lax.rsqrt(jnp.mean(o * o, axis=-1, keepdims=True) + EPS)
        oh = o * r
        sg = _sigmoid(hg)
        d_on = dout * (hg * sg)
        dhg = dout * (oh * gain) * (sg * (1.0 + hg * (1.0 - sg)))
        dgg_ref[0] += jnp.sum(d_on * oh, axis=0, keepdims=True)
        dxh = d_on * gain
        do = r * (dxh - oh * jnp.mean(dxh * oh, axis=-1, keepdims=True))
        dob = do.astype(BF16)
        dsp = dstate[...]
        dspb = dsp.astype(BF16)
        causal = _iota((C, C), 0) >= _iota((C, C), 1)
        da = jnp.where(causal, lax.dot_general(dob, vb, NT, preferred_element_type=F32), 0.0)
        diag = jnp.sum(do * v, axis=-1, keepdims=True)
        dv = lax.dot_general(ab, dob, TN, preferred_element_type=F32)
        dv = dv + lax.dot_general(ke, dspb, NT, preferred_element_type=F32)
        xq = jnp.dot(dob, st0b, preferred_element_type=F32)
        xk = jnp.dot(vb, dspb, preferred_element_type=F32)
        dq = diag * k + fb * xq
        dk = diag * q + fe * xk
        ke_xk = ke.astype(F32) * xk
        db = qb.astype(F32) * xq - ke_xk
        for l in range(HG_LEV):
            f = fall[(2 + l) * C:(3 + l) * C]
            dal = (masks_v[l] * da).astype(BF16)
            ql, kl = (q * f).astype(BF16), (k * f).astype(BF16)
            xq = jnp.dot(dal, kl, preferred_element_type=F32)
            xk = lax.dot_general(dal, ql, TN, preferred_element_type=F32)
            dq = dq + f * xq
            dk = dk + f * xk
            db = db + ql.astype(F32) * xq - kl.astype(F32) * xk
        dstate[...] = dsp * ebc + lax.dot_general(dob, qb, TN, preferred_element_type=F32)
        triu = (_iota((C, C), 0) <= _iota((C, C), 1)).astype(F32)
        dg = jnp.dot(triu, db, precision=HIGHEST, preferred_element_type=F32)
        dg = dg + jnp.sum(st0 * ebc * dsp, axis=0, keepdims=True) + jnp.sum(ke_xk, axis=0, keepdims=True)
        keep = _row_ids((nc - 1 - c) * C, C) >= ROW0
        dg = jnp.where(keep, dg, 0.0)
        dk = jnp.where(keep, dk, 0.0)
        f_gate = lb + (1.0 - lb) * sz
        dfdz = (1.0 - lb) * sz * snz
        dz_ref[...] = (dg * dfdz / f_gate - dk * dfdz).astype(dz_ref.dtype)
        dlb_ref[...] += jnp.sum(dg * snz / f_gate - dk * snz, axis=0, keepdims=True)
        dq_ref[...] = jnp.where(keep, dq * (sq * (1.0 + hq * (1.0 - sq))), 0.0).astype(dq_ref.dtype)
        di_ref[...] = jnp.where(keep, dv, 0.0).astype(di_ref.dtype)
        dg_ref[...] = jnp.where(keep, dhg, 0.0).astype(dg_ref.dtype)

    outs = pl.pallas_call(
        body, name="hgrn_bwd", grid=(HG_H, nc),
        in_specs=[col(0), col(HG_H), col(2 * HG_H), col(3 * HG_H), col(0)] + consts + [st],
        out_specs=[col(0), col(0), col(0), col(0), pl.BlockSpec((1, HG_D), lambda h, c: (0, h)),
                   pl.BlockSpec((1, 1, HG_D), lambda h, c: (h, 0, 0))],
        out_shape=[jax.ShapeDtypeStruct((L, D), BF16)] * 4 + [jax.ShapeDtypeStruct((1, D), F32),
                                                              jax.ShapeDtypeStruct((HG_H, 1, HG_D), F32)],
        scratch_shapes=[pltpu.VMEM((HG_D, HG_D), F32)],
        compiler_params=_cparams(("parallel", "arbitrary")),
    )(proj, proj, proj, proj, dog, hlb, gg, rmat, masks, states)
    return outs


def _ffn_fwd(h, norm_gain, wg, wu, wo, tag):
    hn = _rms_fwd(h, norm_gain, f"{tag}_norm")
    g = _matmul(hn, wg, out_dtype=BF16, name=f"{tag}_gate")
    u = _matmul(hn, wu, out_dtype=BF16, name=f"{tag}_up")
    act = _swiglu_fwd(g, u, f"{tag}_act")
    h_out = _matmul(act, wo, add=h, name=f"{tag}_out")
    return h_out, (h, hn, g, u, act)


def _ffn_bwd(dh, saved, norm_gain, wg, wu, wo, tag):
    h, hn, g, u, act = saved
    dact = _matmul(dh, wo, tb=True, out_dtype=BF16, name=f"{tag}_dact")
    d_wo = _matmul(act, dh, ta=True, name=f"{tag}_dwo")
    dg, du = _swiglu_bwd(dact, g, u, f"{tag}_dact_bwd")
    dhn = _matmul(dg, wg, tb=True, name=f"{tag}_dhn_g")
    dhn = _matmul(du, wu, tb=True, add=dhn, name=f"{tag}_dhn_u")
    d_wg = _matmul(hn, dg, ta=True, name=f"{tag}_dwg")
    d_wu = _matmul(hn, du, ta=True, name=f"{tag}_dwu")
    dh, d_gain = _rms_bwd(h, dhn, norm_gain, dh, f"{tag}_norm_bwd")
    return dh, d_gain, jnp.concatenate([d_wg, d_wu], axis=1), d_wo


def _local_step(h0, tgt, w, late_shards, layers):
    L = h0.shape[0]
    gmat = jnp.asarray(np.kron(np.eye(FOX_H), np.ones((FOX_DH, FOX_DH))).astype(np.float32), BF16)
    rmat, lmasks = _hgrn_consts()
    an, fn_ = w["attn_norm"], w["ffn_norm"]
    qg = jnp.tile(w["fox_q_norm"], (1, FOX_H))
    kg = jnp.tile(w["fox_k_norm"], (1, FOX_H))
    bf = jnp.pad(w["fox_b_f"], ((0, 0), (0, LANES - FOX_H)))
    fw = w["fox_w_in"]
    f_wq, f_wk, f_wv, f_wg = (fw[:, i * D:(i + 1) * D] for i in range(4))
    f_wf = jnp.pad(fw[:, 4 * D:], ((0, 0), (0, LANES - FOX_H)))
    f_wo = w["fox_w_out"]

    hn0 = _rms_fwd(h0, an[0:1], "fox_norm")
    q_raw = _matmul(hn0, f_wq, name="fox_q")
    k_raw = _matmul(hn0, f_wk, name="fox_k")
    v = _matmul(hn0, f_wv, out_dtype=BF16, name="fox_v")
    gate = _matmul(hn0, f_wg, out_dtype=BF16, name="fox_gate")
    flog = _matmul(hn0, f_wf, name="fox_flog")
    cq, ck = _fox_cumsum_fwd(flog, bf)
    qp, kp, vp = _fox_pack_fwd(q_raw, k_raw, v, cq, ck, qg, kg, gmat)
    o, og, lse2, gathered = _fox_attn_fwd(qp, kp, vp, gate, [late_shards[n] for n in LATE_NAMES])
    late = {n: _unshard(n, g, layers[n]) for n, g in zip(LATE_NAMES, gathered)}
    h_wi, h_wo = late["hgrn_w_in"][0], late["hgrn_w_out"][0]
    ffw = [(late["ffn_w_in"][i][:, :FFN], late["ffn_w_in"][i][:, FFN:], late["ffn_w_out"][i]) for i in range(2)]
    h1 = _matmul(og, f_wo, add=h0, name="fox_out")
    h2, ffn0 = _ffn_fwd(h1, fn_[0:1], *ffw[0], "ffn0")

    hn2 = _rms_fwd(h2, an[1:2], "hgrn_norm")
    proj = _matmul(hn2, h_wi, name="hgrn_in")
    og1, states = _hgrn_fwd(proj, w["hgrn_lower_bounds"], w["hgrn_g_norm"], rmat, lmasks)
    h3 = _matmul(og1, h_wo, add=h2, name="hgrn_out")
    h4, ffn1 = _ffn_fwd(h3, fn_[1:2], *ffw[1], "ffn1")

    loss, dh, d_final = _loss_bwd(h4, tgt, w["final_norm"])

    dh, d_fn1, d_ffn_in1, d_ffn_out1 = _ffn_bwd(dh, ffn1, fn_[1:2], *ffw[1], "ffn1")
    dog1 = _matmul(dh, h_wo, tb=True, out_dtype=BF16, name="hgrn_dog")
    d_h_wo = _matmul(og1, dh, ta=True, name="hgrn_dwo")
    dpq, dpz, dpi, dpg, d_lb, d_gg = _hgrn_bwd(proj, dog1, states, w["hgrn_lower_bounds"], w["hgrn_g_norm"],
                                               rmat, lmasks)
    dproj = jnp.concatenate([dpq, dpz, dpi, dpg], axis=1)
    dhn2 = _matmul(dproj, h_wi, tb=True, name="hgrn_dhn")
    d_h_wi = _matmul(hn2, dproj, ta=True, name="hgrn_dwi")
    dh, d_an1 = _rms_bwd(h2, dhn2, an[1:2], dh, "hgrn_norm_bwd")

    dh, d_fn0, d_ffn_in0, d_ffn_out0 = _ffn_bwd(dh, ffn0, fn_[0:1], *ffw[0], "ffn0")
    late_grads = dict(hgrn_w_in=d_h_wi[None], hgrn_w_out=d_h_wo[None],
                      ffn_w_in=jnp.stack([d_ffn_in0, d_ffn_in1]), ffn_w_out=jnp.stack([d_ffn_out0, d_ffn_out1]))
    pair_late, send_late = _pair_sums([_to_shards(n, late_grads[n]) for n in LATE_NAMES], "late")

    dog = _matmul(dh, f_wo, tb=True, out_dtype=BF16, name="fox_dog")
    d_f_wo = _matmul(og, dh, ta=True, name="fox_dwo")

    def by_head(a):
        return jnp.pad(a.transpose(1, 0, 2).reshape(L, FOX_H), ((0, 0), (0, LANES - FOX_H)))

    qb = _fox_pack_bias(qp, cq, by_head(lse2))
    dop, dgate = _fox_pack_bwd(dog, o, gate)
    dqp, dk, dv, dck, recv_late = _fox_attn_bwd(qb, kp, vp, dop, send_late)
    (dq_raw, dk_raw, dc_q), (d_qg, d_kg) = _fox_unpack_bwd(q_raw, k_raw, dqp, dk, qg, kg, gmat)
    dflog, d_bf = _fox_cumsum_bwd(dc_q, by_head(dck), flog, bf)
    dproj0 = jnp.concatenate([dq_raw, dk_raw, dv, dgate, dflog.astype(BF16)], axis=1)
    f_wall = jnp.concatenate([f_wq, f_wk, f_wv, f_wg, f_wf], axis=1)
    dhn0 = _matmul(dproj0, f_wall, tb=True, name="fox_dhn")
    d_f_wall = _matmul(hn0, dproj0, ta=True, name="fox_dwi")
    d_f_wi = d_f_wall[:, :4 * D + FOX_H]
    dh, d_an0 = _rms_bwd(h0, dhn0, an[0:1], dh, "fox_norm_bwd")

    fox_grads = dict(fox_w_in=d_f_wi[None], fox_w_out=d_f_wo[None])
    pair_fox, send_fox = _pair_sums([_to_shards(n, fox_grads[n]) for n in FOX_NAMES], "fox")
    recv_fox = _chip_scatter(send_fox, "fox")
    halves = _chip_sums(pair_fox, recv_fox, "fox") + _chip_sums(pair_late, recv_late, "late")
    full = _sibling_allgather(halves)
    big = {n: f.reshape(2 * f.shape[1], f.shape[2]) for n, f in zip(FOX_NAMES + LATE_NAMES, full)}
    small = dict(attn_norm=jnp.concatenate([d_an0, d_an1]), ffn_norm=jnp.concatenate([d_fn0, d_fn1]),
                 final_norm=d_final, lb_raw=d_lb, q_gain=d_qg, k_gain=d_kg, b_f=d_bf,
                 g_gain=d_gg.reshape(1, D))
    return loss, dh, big, small


def _me():
    return lax.axis_index("x"), lax.axis_index("y"), lax.axis_index("c")


def _flip(v, bit):
    return 1 - v if bit else v


def _chip_allgather(arrs):
    n = len(arrs)

    def body(*refs):
        copies = _gather_copies(refs[:n], refs[n:2 * n], *refs[2 * n:])
        _gather_start(copies)
        _gather_wait(copies)

    return pl.pallas_call(
        body, name="chip_allgather", in_specs=[ANY] * n, out_specs=[ANY] * n,
        out_shape=[jax.ShapeDtypeStruct((4,) + a.shape, a.dtype) for a in arrs],
        scratch_shapes=_gather_sems(n),
    )(*arrs)


def _chip_peers():
    x, y, c = _me()
    return [(1 - x, y, c), (x, 1 - y, c), (1 - x, 1 - y, c)]


def _gather_sems(n):
    return [pltpu.SemaphoreType.DMA((n, 3)), pltpu.SemaphoreType.DMA((n, 3)), pltpu.SemaphoreType.DMA((n,))]


def _gather_copies(ins, outs, ssem, rsem, lsem):
    x, y, _ = _me()
    local, sends, recvs = [], [], []
    for a in range(len(ins)):
        local.append(pltpu.make_async_copy(ins[a], outs[a].at[2 * x + y], lsem.at[a]))
        for k, peer in enumerate(_chip_peers()):
            sends.append(pltpu.make_async_remote_copy(ins[a], outs[a].at[2 * x + y], ssem.at[a, k], rsem.at[a, k],
                                                      device_id=peer, device_id_type=MESH))
            recvs.append(pltpu.make_async_remote_copy(ins[a], outs[a].at[2 * peer[0] + peer[1]], ssem.at[a, k],
                                                      rsem.at[a, k], device_id=peer, device_id_type=MESH))
    return local, sends, recvs


def _gather_start(copies):
    local, sends, _ = copies
    for cp in local + sends:
        cp.start()


def _gather_wait(copies):
    local, sends, recvs = copies
    for cp in local:
        cp.wait()
    for cp in sends:
        cp.wait_send()
    for cp in recvs:
        cp.wait_recv()


def _scatter_copies(ins, outs, ssem, rsem):
    copies = []
    for a in range(len(ins)):
        for k, peer in enumerate(_chip_peers()):
            copies.append(pltpu.make_async_remote_copy(ins[a].at[2 * peer[0] + peer[1]], outs[a].at[k], ssem.at[a, k],
                                                       rsem.at[a, k], device_id=peer, device_id_type=MESH))
    return copies


def _device_allgather(arr):
    def body(in_ref, out_ref, ssem, rsem, lsem):
        x, y, c = _me()
        me = 4 * x + 2 * y + c
        peers = [(_flip(x, k & 4), _flip(y, k & 2), _flip(c, k & 1)) for k in range(1, 8)]
        local = pltpu.make_async_copy(in_ref, out_ref.at[me], lsem)
        local.start()
        sends = []
        for k, peer in enumerate(peers):
            cp = pltpu.make_async_remote_copy(in_ref, out_ref.at[me], ssem.at[k], rsem.at[k],
                                              device_id=peer, device_id_type=MESH)
            cp.start()
            sends.append(cp)
        local.wait()
        for cp in sends:
            cp.wait_send()
        for k, peer in enumerate(peers):
            pltpu.make_async_remote_copy(in_ref, out_ref.at[4 * peer[0] + 2 * peer[1] + peer[2]], ssem.at[k],
                                         rsem.at[k], device_id=peer, device_id_type=MESH).wait_recv()

    return pl.pallas_call(
        body, name="device_allgather", in_specs=[ANY], out_specs=ANY,
        out_shape=jax.ShapeDtypeStruct((8,) + arr.shape, arr.dtype),
        scratch_shapes=[pltpu.SemaphoreType.DMA((7,)), pltpu.SemaphoreType.DMA((7,)), pltpu.SemaphoreType.DMA],
    )(arr)


def _sibling_send_other_half(arrs, tag):
    n = len(arrs)

    def body(*refs):
        ins, outs = refs[:n], refs[n:2 * n]
        ssem, rsem = refs[2 * n:]
        x, y, c = _me()
        cps = []
        for a in range(n):
            half = ins[a].shape[1] // 2
            src = ins[a].at[:, pl.ds((1 - c) * half, half), :]
            cp = pltpu.make_async_remote_copy(src, outs[a], ssem.at[a], rsem.at[a],
                                              device_id=(x, y, 1 - c), device_id_type=MESH)
            cp.start()
            cps.append(cp)
        for cp in cps:
            cp.wait()

    return pl.pallas_call(
        body, name=f"grad_sibling_swap_{tag}", in_specs=[ANY] * n, out_specs=[ANY] * n,
        out_shape=[jax.ShapeDtypeStruct((4, a.shape[1] // 2, a.shape[2]), a.dtype) for a in arrs],
        scratch_shapes=[pltpu.SemaphoreType.DMA((n,)), pltpu.SemaphoreType.DMA((n,))],
    )(*arrs)


def _chip_scatter(arrs, tag):
    n = len(arrs)

    def body(*refs):
        cps = _scatter_copies(refs[:n], refs[n:2 * n], *refs[2 * n:])
        for cp in cps:
            cp.start()
        for cp in cps:
            cp.wait()

    return pl.pallas_call(
        body, name=f"grad_chip_scatter_{tag}", in_specs=[ANY] * n, out_specs=[ANY] * n,
        out_shape=[jax.ShapeDtypeStruct((3,) + a.shape[1:], a.dtype) for a in arrs],
        scratch_shapes=[pltpu.SemaphoreType.DMA((n, 3)), pltpu.SemaphoreType.DMA((n, 3))],
    )(*arrs)


def _sibling_allgather(arrs):
    n = len(arrs)

    def body(*refs):
        ins, outs = refs[:n], refs[n:2 * n]
        ssem, rsem, lsem = refs[2 * n:]
        x, y, c = _me()
        cps = []
        for a in range(n):
            cp = pltpu.make_async_copy(ins[a], outs[a].at[c], lsem.at[a])
            cp.start()
            cps.append(cp)
            cp = pltpu.make_async_remote_copy(ins[a], outs[a].at[c], ssem.at[a], rsem.at[a],
                                              device_id=(x, y, 1 - c), device_id_type=MESH)
            cp.start()
            cps.append(cp)
        for cp in cps:
            cp.wait()

    return pl.pallas_call(
        body, name="grad_sibling_allgather", in_specs=[ANY] * n, out_specs=[ANY] * n,
        out_shape=[jax.ShapeDtypeStruct((2,) + a.shape, a.dtype) for a in arrs],
        scratch_shapes=[pltpu.SemaphoreType.DMA((n,)), pltpu.SemaphoreType.DMA((n,)),
                        pltpu.SemaphoreType.DMA((n,))],
    )(*arrs)


def _pair_sums(grads, tag):
    c = lax.axis_index("c")
    got = _sibling_send_other_half(grads, tag)
    pair = []
    for i, (g, t) in enumerate(zip(grads, got)):
        half = g.shape[1] // 2
        mine = lax.dynamic_slice_in_dim(g, c * half, half, axis=1)
        pair.append(_add([mine.reshape(4 * half, -1), t.reshape(4 * half, -1)], f"grad_pair_add_{tag}{i}")
                    .reshape(t.shape))
    return pair, [p.astype(BF16) for p in pair]


def _chip_sums(pair, recv, tag):
    x, y, _ = _me()
    out = []
    for i, (p, r) in enumerate(zip(pair, recv)):
        own = lax.dynamic_index_in_dim(p, 2 * x + y, axis=0, keepdims=False)
        out.append(_add([own, r[0], r[1], r[2]], f"grad_chip_add_{tag}{i}"))
    return out


SMALL_ROWS = 32


def _small_finalize(gathered, hlb, fold64, fold128):
    def body(g_ref, hlb_ref, f64_ref, f128_ref, rows_ref, qk_ref, gg_ref, lb_ref):
        tot = g_ref[0]
        for d in range(1, 8):
            tot = tot + g_ref[d]
        rows_ref[...] = tot
        qk_ref[...] = jnp.dot(rows_ref[6:8, :], f64_ref[...], precision=HIGHEST, preferred_element_type=F32)
        gg_ref[...] = jnp.dot(rows_ref[9:10, :], f128_ref[...], precision=HIGHEST, preferred_element_type=F32)
        h0, h1 = hlb_ref[0:1, :], hlb_ref[1:2, :]
        mx = jnp.maximum(h0, h1)
        e0, e1 = jnp.exp(h0 - mx), jnp.exp(h1 - mx)
        lb = e1 / (e0 + e1)
        d1 = rows_ref[5:6, :] * lb * (1.0 - lb)
        lb_ref[...] = jnp.where(_iota((2, 1), 0) == 0, -d1, d1)

    return pl.pallas_call(
        body, name="small_finalize",
        out_shape=[jax.ShapeDtypeStruct((SMALL_ROWS, D), F32), jax.ShapeDtypeStruct((2, FOX_DH), F32),
                   jax.ShapeDtypeStruct((1, HG_D), F32), jax.ShapeDtypeStruct((2, D), F32)],
    )(gathered, hlb, fold64, fold128)


FOX_NAMES = ("fox_w_in", "fox_w_out")
LATE_NAMES = ("hgrn_w_in", "hgrn_w_out", "ffn_w_in", "ffn_w_out")
BIG_NAMES = FOX_NAMES + LATE_NAMES
COL_SHARDED = ("fox_w_in", "hgrn_w_in", "ffn_w_in")


def _shard2d(name, a):
    return a.reshape(-1, a.shape[-1])


def _unshard(name, g, layers):
    if name in COL_SHARDED:
        k = g.shape[1] // layers
        return g.reshape(4, layers, k, g.shape[2]).transpose(1, 2, 0, 3).reshape(layers, k, 4 * g.shape[2])
    r = g.shape[1] // layers
    return g.reshape(4, layers, r, g.shape[2]).transpose(1, 0, 2, 3).reshape(layers, 4 * r, g.shape[2])


def _to_shards(name, g):
    layers = g.shape[0]
    if name in COL_SHARDED:
        k, n = g.shape[1], g.shape[2] // 4
        return g.reshape(layers, k, 4, n).transpose(2, 0, 1, 3).reshape(4, layers * k, n)
    r = g.shape[1] // 4
    return g.reshape(layers, 4, r, g.shape[2]).transpose(1, 0, 2, 3).reshape(4, layers * r, g.shape[2])


def kernel(x, meta_tokens, attn_norm, ffn_norm, final_norm, fox_w_in, fox_b_f, fox_q_norm, fox_k_norm, fox_w_out, hgrn_w_in, hgrn_lower_bounds, hgrn_g_norm, hgrn_w_out, ffn_w_in, ffn_w_out, loss_target, m_meta_tokens, m_attn_norm, m_ffn_norm, m_final_norm, m_fox_w_in, m_fox_b_f, m_fox_q_norm, m_fox_k_norm, m_fox_w_out, m_hgrn_w_in, m_hgrn_lower_bounds, m_hgrn_g_norm, m_hgrn_w_out, m_ffn_w_in, m_ffn_w_out, v_meta_tokens, v_attn_norm, v_ffn_norm, v_final_norm, v_fox_w_in, v_fox_b_f, v_fox_q_norm, v_fox_k_norm, v_fox_w_out, v_hgrn_w_in, v_hgrn_lower_bounds, v_hgrn_g_norm, v_hgrn_w_out, v_ffn_w_in, v_ffn_w_out):
    params = dict(meta_tokens=meta_tokens, attn_norm=attn_norm, ffn_norm=ffn_norm, final_norm=final_norm,
                  fox_w_in=fox_w_in, fox_b_f=fox_b_f, fox_q_norm=fox_q_norm, fox_k_norm=fox_k_norm,
                  fox_w_out=fox_w_out, hgrn_w_in=hgrn_w_in, hgrn_lower_bounds=hgrn_lower_bounds,
                  hgrn_g_norm=hgrn_g_norm, hgrn_w_out=hgrn_w_out, ffn_w_in=ffn_w_in, ffn_w_out=ffn_w_out)
    mom_m = dict(meta_tokens=m_meta_tokens, attn_norm=m_attn_norm, ffn_norm=m_ffn_norm, final_norm=m_final_norm,
                 fox_w_in=m_fox_w_in, fox_b_f=m_fox_b_f, fox_q_norm=m_fox_q_norm, fox_k_norm=m_fox_k_norm,
                 fox_w_out=m_fox_w_out, hgrn_w_in=m_hgrn_w_in, hgrn_lower_bounds=m_hgrn_lower_bounds,
                 hgrn_g_norm=m_hgrn_g_norm, hgrn_w_out=m_hgrn_w_out, ffn_w_in=m_ffn_w_in, ffn_w_out=m_ffn_w_out)
    mom_v = dict(meta_tokens=v_meta_tokens, attn_norm=v_attn_norm, ffn_norm=v_ffn_norm, final_norm=v_final_norm,
                 fox_w_in=v_fox_w_in, fox_b_f=v_fox_b_f, fox_q_norm=v_fox_q_norm, fox_k_norm=v_fox_k_norm,
                 fox_w_out=v_fox_w_out, hgrn_w_in=v_hgrn_w_in, hgrn_lower_bounds=v_hgrn_lower_bounds,
                 hgrn_g_norm=v_hgrn_g_norm, hgrn_w_out=v_hgrn_w_out, ffn_w_in=v_ffn_w_in, ffn_w_out=v_ffn_w_out)
    names = list(params)
    seq = x.shape[1]
    xi, yi, ci = _me()

    shards = {n: _shard2d(n, params[n]).astype(BF16) for n in BIG_NAMES}
    layers = {n: params[n].shape[0] for n in BIG_NAMES}
    gathered = _chip_allgather([shards[n] for n in FOX_NAMES] + [meta_tokens])
    w = {n: _unshard(n, g, 1)[0] for n, g in zip(FOX_NAMES, gathered[:-1])}
    meta_full = gathered[-1].transpose(1, 0, 2).reshape(N_META, D)
    w.update(attn_norm=attn_norm, ffn_norm=ffn_norm, final_norm=final_norm.reshape(1, D), fox_b_f=fox_b_f,
             fox_q_norm=fox_q_norm, fox_k_norm=fox_k_norm, hgrn_lower_bounds=hgrn_lower_bounds,
             hgrn_g_norm=hgrn_g_norm)

    h0 = jnp.concatenate([jnp.zeros((ROW0, D), F32), meta_full, x[0]], axis=0)
    tgt = jnp.concatenate([jnp.zeros((PAD, D), F32), loss_target[0]], axis=0)
    loss, dh0, big, small = _local_step(h0, tgt, w, {n: shards[n] for n in LATE_NAMES}, layers)
    loss = lax.psum(loss, ("x", "y", "c"))
    grad_x = dh0[PAD:][None]
    grads = {n: big[n].reshape(params[n].shape) for n in BIG_NAMES}

    rows = jnp.concatenate([small["attn_norm"], small["ffn_norm"], small["final_norm"], small["lb_raw"],
                            small["q_gain"], small["k_gain"],
                            jnp.pad(small["b_f"], ((0, 0), (0, D - LANES))), small["g_gain"],
                            dh0[ROW0:PAD], jnp.zeros((SMALL_ROWS - 10 - N_META, D), F32)], axis=0)
    allrows = _device_allgather(rows)
    fold64 = jnp.asarray(np.tile(np.eye(FOX_DH, dtype=np.float32), (FOX_H, 1)))
    fold128 = jnp.asarray(np.tile(np.eye(HG_D, dtype=np.float32), (HG_H, 1)))
    tot, qk, gg, dlb = _small_finalize(allrows, hgrn_lower_bounds, fold64, fold128)
    grads.update(attn_norm=tot[0:2], ffn_norm=tot[2:4], final_norm=tot[4], hgrn_lower_bounds=dlb,
                 fox_q_norm=qk[0:1], fox_k_norm=qk[1:2], fox_b_f=tot[8:9, :FOX_H], hgrn_g_norm=gg,
                 meta_tokens=lax.dynamic_slice_in_dim(tot[10:10 + N_META], (2 * xi + yi) * (D // 4), D // 4, axis=1))

    delta, new_m, new_v = {}, {}, {}
    for n in BIG_NAMES + ("meta_tokens",):
        d_, m_, v_ = _adamw(_shard2d(n, params[n]), _shard2d(n, grads[n]), _shard2d(n, mom_m[n]),
                            _shard2d(n, mom_v[n]), f"adamw_{n}")
        delta[n], new_m[n], new_v[n] = (t.reshape(params[n].shape) for t in (d_, m_, v_))
    small_names = [n for n in names if n not in BIG_NAMES and n != "meta_tokens"]

    def pack(d):
        return jnp.concatenate([jnp.pad(d[n].reshape(-1, d[n].shape[-1]), ((0, 0), (0, D - d[n].shape[-1])))
                                for n in small_names], axis=0)

    packed = [pack(t) for t in (params, grads, mom_m, mom_v)]
    n_rows = packed[0].shape[0]
    packed = [jnp.pad(t, ((0, 16 - n_rows), (0, 0))) for t in packed]
    res = _adamw(*packed, "adamw_small")
    r0 = 0
    for n in small_names:
        nr = params[n].reshape(-1, params[n].shape[-1]).shape[0]
        for dst, src in zip((delta, new_m, new_v), res):
            dst[n] = src[r0:r0 + nr, :params[n].shape[-1]].reshape(params[n].shape)
        r0 += nr

    return (loss, grad_x, *[grads[n] for n in names], *[delta[n] for n in names],
            *[new_m[n] for n in names], *[new_v[n] for n in names])
```

```python
import functools

import numpy as np
import jax
import jax.numpy as jnp
from jax import lax
from jax.experimental import pallas as pl
from jax.experimental.pallas import tpu as pltpu

F32, BF16 = jnp.float32, jnp.bfloat16
HIGHEST = lax.Precision.HIGHEST

D = 1024
N_META = 16
PAD = 128
ROW0 = PAD - N_META
FOX_H, FOX_DH = 16, 64
HG_H, HG_D = 8, 128
HG_C = 128
HG_LEV = 7
HG_HPS = 4
FFN = 2816
EPS = 1e-6
BIG = 1e30
LOG2E = 1.4426950408889634
LANES = 128
VMEM_LIMIT = 48 * 1024 * 1024
ROW_TILES = (640, 512, 384, 320, 256, 128, 64, 32, 16, 8)

ADAM_LR, ADAM_B1, ADAM_B2, ADAM_EPS, ADAM_WD, ADAM_STEP = 0.001, 0.9, 0.999, 1e-08, 0.01, 10

MESH = pl.DeviceIdType.MESH
ANY = pl.BlockSpec(memory_space=pl.ANY)
NT = (((1,), (1,)), ((), ()))
TN = (((0,), (0,)), ((), ()))


def _tile(n, cands=ROW_TILES, cap=None):
    for c in cands:
        if n % c == 0 and (cap is None or c <= cap):
            return c
    return n


def _cparams(sem):
    return pltpu.CompilerParams(dimension_semantics=sem, vmem_limit_bytes=VMEM_LIMIT)


def _sigmoid(x):
    return jax.nn.sigmoid(x)


def _log_sigmoid(x):
    return jnp.minimum(x, 0.0) - jnp.log(1.0 + jnp.exp(-jnp.abs(x)))


def _iota(shape, dim):
    return lax.broadcasted_iota(jnp.int32, shape, dim)


def _matmul(a, b, *, ta=False, tb=False, out_dtype=F32, add=None, name):
    if ta:
        kdim, m = a.shape
    else:
        m, kdim = a.shape
    n = b.shape[0] if tb else b.shape[1]
    if ta:
        tm = m if m <= 1024 else _tile(m, (1408, 1024, 512, 256, 128))
        tk = _tile(kdim)
    else:
        tm = _tile(m)
        tk = kdim if kdim <= 4096 else _tile(kdim, (2048, 1024, 512))
    tn = n if n <= 1024 else _tile(n, (1408, 1024, 512, 256, 128))
    nk = kdim // tk
    dn = (((0 if ta else 1,), (1 if tb else 0,)), ((), ()))

    def body(*refs):
        if add is None:
            a_ref, b_ref, o_ref, acc_ref = refs
        else:
            a_ref, b_ref, add_ref, o_ref, acc_ref = refs
        k = pl.program_id(2)

        @pl.when(k == 0)
        def _():
            acc_ref[...] = jnp.zeros_like(acc_ref)

        acc_ref[...] += lax.dot_general(a_ref[...].astype(BF16), b_ref[...].astype(BF16), dn,
                                        preferred_element_type=F32)

        @pl.when(k == nk - 1)
        def _():
            r = acc_ref[...]
            if add is not None:
                r = r + add_ref[...].astype(F32)
            o_ref[...] = r.astype(o_ref.dtype)

    a_spec = pl.BlockSpec((tk, tm), lambda j, i, k: (k, i)) if ta else pl.BlockSpec((tm, tk), lambda j, i, k: (i, k))
    b_spec = pl.BlockSpec((tn, tk), lambda j, i, k: (j, k)) if tb else pl.BlockSpec((tk, tn), lambda j, i, k: (k, j))
    o_spec = pl.BlockSpec((tm, tn), lambda j, i, k: (i, j))
    ins, specs = [a, b], [a_spec, b_spec]
    if add is not None:
        ins.append(add)
        specs.append(o_spec)
    return pl.pallas_call(
        body, name=name, grid=(n // tn, m // tm, nk), in_specs=specs, out_specs=o_spec,
        out_shape=jax.ShapeDtypeStruct((m, n), out_dtype),
        scratch_shapes=[pltpu.VMEM((tm, tn), F32)],
        compiler_params=_cparams(("parallel", "parallel", "arbitrary")),
    )(*ins)


def _rowwise(fn, ins, bcast, outs, accs, *, name, reverse=False, carry=None, as_refs=False):
    rows = ins[0].shape[0]
    per_row = sum(x.shape[1] * x.dtype.itemsize for x in ins) + sum(c * jnp.dtype(d).itemsize for c, d in outs)
    tm = _tile(rows, cap=max(8, (10 * 1024 * 1024) // per_row))
    n = rows // tm
    n_in, n_b, n_o, n_a = len(ins), len(bcast), len(outs), len(accs)

    def body(*refs):
        in_refs = refs[:n_in]
        b_refs = refs[n_in:n_in + n_b]
        o_refs = refs[n_in + n_b:n_in + n_b + n_o]
        a_refs = refs[n_in + n_b + n_o:n_in + n_b + n_o + n_a]
        c_refs = refs[n_in + n_b + n_o + n_a:]
        i = pl.program_id(0)
        blk = (n - 1 - i) if reverse else i
        if c_refs:
            @pl.when(i == 0)
            def _():
                c_refs[0][...] = jnp.zeros_like(c_refs[0])
        args = (list(in_refs) if as_refs else [r[...] for r in in_refs], [r[...] for r in b_refs])
        o_vals, a_vals = fn(blk * tm, *args, *c_refs)
        for r, v in zip(o_refs, o_vals):
            r[...] = v.astype(r.dtype)
        if n_a:
            @pl.when(i == 0)
            def _():
                for r in a_refs:
                    r[...] = jnp.zeros_like(r)
            for r, v in zip(a_refs, a_vals):
                r[...] += v

    def row_map(i):
        return ((n - 1 - i) if reverse else i, 0)

    in_specs = [pl.BlockSpec((tm, x.shape[1]), row_map) for x in ins]
    in_specs += [pl.BlockSpec(x.shape, lambda i, nd=x.ndim: (0,) * nd) for x in bcast]
    out_specs = [pl.BlockSpec((tm, c), row_map) for c, _ in outs]
    out_specs += [pl.BlockSpec(s, lambda i: (0, 0)) for s in accs]
    out_shape = [jax.ShapeDtypeStruct((rows, c), d) for c, d in outs]
    out_shape += [jax.ShapeDtypeStruct(s, F32) for s in accs]
    res = pl.pallas_call(
        body, name=name, grid=(n,), in_specs=in_specs, out_specs=out_specs, out_shape=out_shape,
        scratch_shapes=[pltpu.VMEM(carry, F32)] if carry else [],
        compiler_params=_cparams(("arbitrary",)),
    )(*ins, *bcast)
    return res[:n_o], res[n_o:]


def _row_ids(row0, tm):
    return row0 + _iota((tm, 1), 0)


def _rms_fwd(x, gain, name):
    def fn(row0, ins, bc):
        (xv,), (g,) = ins, bc
        r = lax.rsqrt(jnp.mean(xv * xv, axis=-1, keepdims=True) + EPS)
        return [xv * r * g], []
    return _rowwise(fn, [x], [gain], [(D, BF16)], [], name=name)[0][0]


def _rms_bwd(x, dxn, gain, dh_up, name):
    def fn(row0, ins, bc):
        xv, dy, up = ins
        (g,) = bc
        dy = dy.astype(F32)
        r = lax.rsqrt(jnp.mean(xv * xv, axis=-1, keepdims=True) + EPS)
        xh = xv * r
        dxh = dy * g
        dx = r * (dxh - xh * jnp.mean(dxh * xh, axis=-1, keepdims=True))
        keep = _row_ids(row0, xv.shape[0]) >= ROW0
        return [jnp.where(keep, up + dx, 0.0)], [jnp.sum(dy * xh, axis=0, keepdims=True)]
    (dh,), (dgain,) = _rowwise(fn, [x, dxn, dh_up], [gain], [(D, F32)], [(1, D)], name=name)
    return dh, dgain


def _loss_bwd(h, tgt, gain):
    def fn(row0, ins, bc):
        xv, t = ins
        (g,) = bc
        r = lax.rsqrt(jnp.mean(xv * xv, axis=-1, keepdims=True) + EPS)
        xh = xv * r
        keep = _row_ids(row0, xv.shape[0]) >= PAD
        err = jnp.where(keep, xh * g - t, 0.0)
        per_row = jnp.mean(err * err, axis=-1, keepdims=True)
        loss = 0.5 * jnp.sum(per_row, axis=0, keepdims=True)
        dy = err * (1.0 / D)
        dxh = dy * g
        dx = r * (dxh - xh * jnp.mean(dxh * xh, axis=-1, keepdims=True))
        return [dx], [jnp.broadcast_to(loss, (1, LANES)), jnp.sum(dy * xh, axis=0, keepdims=True)]
    (dh,), (loss, dgain) = _rowwise(fn, [h, tgt], [gain], [(D, F32)], [(1, LANES), (1, D)], name="loss_bwd")
    return loss[0, 0], dh, dgain


def _swiglu_fwd(g, u, name):
    def fn(row0, ins, bc):
        gv, uv = (v.astype(F32) for v in ins)
        return [gv * _sigmoid(gv) * uv], []
    return _rowwise(fn, [g, u], [], [(g.shape[1], BF16)], [], name=name)[0][0]


def _swiglu_bwd(dact, g, u, name):
    def fn(row0, ins, bc):
        da, gv, uv = (v.astype(F32) for v in ins)
        s = _sigmoid(gv)
        return [da * uv * (s * (1.0 + gv * (1.0 - s))), da * gv * s], []
    n = g.shape[1]
    return _rowwise(fn, [dact, g, u], [], [(n, BF16), (n, BF16)], [], name=name)[0]


def _adamw(w, g, m, v, name):
    def fn(row0, ins, bc):
        wv, gv, mv, vv = ins
        mn = ADAM_B1 * mv + (1.0 - ADAM_B1) * gv
        vn = ADAM_B2 * vv + (1.0 - ADAM_B2) * (gv * gv)
        m_hat = mn / (1.0 - ADAM_B1 ** ADAM_STEP)
        v_hat = vn / (1.0 - ADAM_B2 ** ADAM_STEP)
        delta = -ADAM_LR * (m_hat / (jnp.sqrt(v_hat) + ADAM_EPS) + ADAM_WD * wv)
        return [delta, mn, vn], []
    c = w.shape[1]
    return _rowwise(fn, [w, g, m, v], [], [(c, F32)] * 3, [], name=name)[0]


def _add(xs, name):
    def fn(row0, ins, bc):
        r = ins[0].astype(F32)
        for v in ins[1:]:
            r = r + v.astype(F32)
        return [r], []
    return _rowwise(fn, list(xs), [], [(xs[0].shape[1], F32)], [], name=name)[0][0]


def _head_sum(x, gmat):
    hi = x.astype(BF16)
    lo = (x - hi.astype(F32)).astype(BF16)
    return jnp.dot(hi, gmat, preferred_element_type=F32) + jnp.dot(lo, gmat, preferred_element_type=F32)


def _split3(x):
    hi = x.astype(BF16).astype(F32)
    r = x - hi
    mid = r.astype(BF16).astype(F32)
    return hi, mid, r - mid


def _extra_base(hh):
    return FOX_DH * (1 - hh)


def _data_mask(hh):
    lane = _iota((1, LANES), 1)
    return (lane >= FOX_DH * hh) & (lane < FOX_DH * (hh + 1))


def _with_extras(data, hh, vals):
    lane = _iota((1, LANES), 1)
    x = jnp.zeros_like(data)
    for e, v in enumerate(vals):
        x = jnp.where(lane == _extra_base(hh) + e, v, x)
    return jnp.where(_data_mask(hh), data, x)


def _fox_pack_fwd(q_raw, k_raw, v, cq, ck, qg, kg, gmat):
    scale2 = FOX_DH ** -0.5 * LOG2E

    def fn(row0, refs, bc):
        q_ref, k_ref, v_ref, cq_ref, ck_ref = refs
        g_q, g_k, gm = bc
        qv, kv = q_ref[...], k_ref[...]
        qn = qv * lax.rsqrt(_head_sum(qv * qv, gm) * (1.0 / FOX_DH) + EPS) * (g_q * scale2)
        kn = kv * lax.rsqrt(_head_sum(kv * kv, gm) * (1.0 / FOX_DH) + EPS) * g_k
        qs, ks, vs = [], [], []
        for h in range(FOX_H):
            p, hh = divmod(h, 2)
            sl = slice(p * LANES, (p + 1) * LANES)
            cq3 = _split3(cq_ref[:, h:h + 1] * LOG2E)
            ck3 = _split3(ck_ref[:, h:h + 1] * (-LOG2E))
            qs.append(_with_extras(qn[:, sl], hh, [*cq3, 1.0, 1.0, 1.0]))
            ks.append(_with_extras(kn[:, sl], hh, [1.0, 1.0, 1.0, *ck3]))
            vs.append(_with_extras(v_ref[:, sl].astype(F32), hh, [1.0, 1.0]))
        return [jnp.concatenate(qs, axis=1), jnp.concatenate(ks, axis=1), jnp.concatenate(vs, axis=1)], []

    w = FOX_H * LANES
    return _rowwise(fn, [q_raw, k_raw, v, cq, ck], [qg, kg, gmat], [(w, BF16)] * 3, [], name="fox_pack_fwd",
                    as_refs=True)[0]


def _fox_pack_bias(qp, cq, lse2):
    def fn(row0, refs, bc):
        q_ref, cq_ref, lse_ref = refs
        lane = _iota((1, LANES), 1)
        outs = []
        for h in range(FOX_H):
            blk = q_ref[:, h * LANES:(h + 1) * LANES].astype(F32)
            for e, part in enumerate(_split3(cq_ref[:, h:h + 1] * LOG2E - lse_ref[:, h:h + 1])):
                blk = jnp.where(lane == _extra_base(h % 2) + e, part, blk)
            outs.append(blk)
        return [jnp.concatenate(outs, axis=1)], []
    return _rowwise(fn, [qp, cq, lse2], [], [(FOX_H * LANES, BF16)], [], name="fox_pack_bias", as_refs=True)[0][0]


def _fox_pack_bwd(dog, o, gate):
    def fn(row0, refs, bc):
        d_ref, o_ref, g_ref = refs
        dos, dgs = [], []
        for p in range(FOX_H // 2):
            sl = slice(p * LANES, (p + 1) * LANES)
            dv, ov, gv = (r[:, sl].astype(F32) for r in (d_ref, o_ref, g_ref))
            s = _sigmoid(gv)
            do = dv * s
            dgs.append(dv * ov * s * (1.0 - s))
            od = ov * do
            for hh in range(2):
                delta = jnp.sum(jnp.where(_data_mask(hh), od, 0.0), axis=-1, keepdims=True)
                hi = delta.astype(BF16).astype(F32)
                dos.append(_with_extras(do, hh, [-hi, hi - delta]))
        return [jnp.concatenate(dos, axis=1), jnp.concatenate(dgs, axis=1)], []
    return _rowwise(fn, [dog, o, gate], [], [(FOX_H * LANES, BF16), (D, BF16)], [], name="fox_pack_bwd",
                    as_refs=True)[0]


def _fox_unpack_bwd(q_raw, k_raw, dqp, dk, qg, kg, gmat):
    scale = FOX_DH ** -0.5

    def fn(row0, refs, bc):
        q_ref, k_ref, dq_ref, dk_ref = refs
        g_q, g_k, gm = bc
        lane = _iota((1, LANES), 1)
        dqs = []
        dcq = jnp.zeros((q_ref.shape[0], LANES), F32)
        for p in range(FOX_H // 2):
            even = dq_ref[:, (2 * p) * LANES:(2 * p + 1) * LANES]
            odd = dq_ref[:, (2 * p + 1) * LANES:(2 * p + 2) * LANES]
            dqs.append(jnp.where(_data_mask(0), even, odd) * scale)
            for hh in range(2):
                col = (2 * p + hh) * LANES + _extra_base(hh)
                dcq = jnp.where(lane == 2 * p + hh, dq_ref[:, col:col + 1], dcq)
        outs, accs = [], []
        for xv, dy, g in ((q_ref[...], jnp.concatenate(dqs, axis=1), g_q), (k_ref[...], dk_ref[...] * (1.0 / LOG2E), g_k)):
            r = lax.rsqrt(_head_sum(xv * xv, gm) * (1.0 / FOX_DH) + EPS)
            xh = xv * r
            dxh = dy * g
            outs.append(r * (dxh - xh * (_head_sum(dxh * xh, gm) * (1.0 / FOX_DH))))
            accs.append(jnp.sum(dy * xh, axis=0, keepdims=True))
        return outs + [dcq], accs
    return _rowwise(fn, [q_raw, k_raw, dqp, dk], [qg, kg, gmat], [(D, BF16), (D, BF16), (LANES, F32)],
                    [(1, D), (1, D)], name="fox_unpack_bwd", as_refs=True)


def _fox_cumsum_fwd(flog, bf):
    def fn(row0, ins, bc, carry):
        (f,), (b,) = ins, bc
        tm = f.shape[0]
        keep = _row_ids(row0, tm) >= ROW0
        lf = jnp.where(keep, _log_sigmoid(f + b), 0.0)
        tri = (_iota((tm, tm), 0) >= _iota((tm, tm), 1)).astype(F32)
        c = jnp.dot(tri, lf, precision=HIGHEST, preferred_element_type=F32) + carry[...]
        carry[...] = carry[...] + jnp.sum(lf, axis=0, keepdims=True)
        return [c, jnp.where(keep, c, BIG)], []
    return _rowwise(fn, [flog], [bf], [(LANES, F32), (LANES, F32)], [], name="fox_cumsum_fwd",
                    carry=(1, LANES))[0]


def _fox_cumsum_bwd(dc_q, dc_k, flog, bf):
    def fn(row0, ins, bc, carry):
        (dq, dk, f), (b,) = ins, bc
        d = dq + dk
        tm = f.shape[0]
        keep = _row_ids(row0, tm) >= ROW0
        triu = (_iota((tm, tm), 0) <= _iota((tm, tm), 1)).astype(F32)
        dlf = jnp.dot(triu, d, precision=HIGHEST, preferred_element_type=F32) + carry[...]
        carry[...] = carry[...] + jnp.sum(d, axis=0, keepdims=True)
        dfl = jnp.where(keep, dlf * _sigmoid(-(f + b)), 0.0)
        return [dfl], [jnp.sum(dfl, axis=0, keepdims=True)]
    (dflog,), (dbf,) = _rowwise(fn, [dc_q, dc_k, flog], [bf], [(LANES, F32)], [(1, LANES)], name="fox_cumsum_bwd",
                                reverse=True, carry=(1, LANES))
    return dflog, dbf


def _causal_steps(n, key_major):
    if key_major:
        pairs = [(i, j) for j in range(n) for i in range(j, n)]
    else:
        pairs = [(i, j) for i in range(n) for j in range(i + 1)]
    return (jnp.asarray(np.array([p[0] for p in pairs], np.int32)),
            jnp.asarray(np.array([p[1] for p in pairs], np.int32)))


def _fox_attn_fwd(qp, kp, vp, gate, shards):
    L = qp.shape[0]
    t = _tile(L, (640, 512, 256, 128))
    n = L // t
    P = FOX_H // 2
    it, jt = _causal_steps(n, False)
    n_steps = it.shape[0]
    ns = len(shards)

    def body(it_ref, jt_ref, q_ref, k_ref, v_ref, g_ref, *rest):
        sh_in, (o_ref, og_ref, lse_ref), sh_out = rest[:ns], rest[ns:ns + 3], rest[ns + 3:2 * ns + 3]
        m_sc, acc, ssem, rsem, lsem = rest[2 * ns + 3:]
        step = pl.program_id(1)
        i, j = it_ref[step], jt_ref[step]
        first = (pl.program_id(0) == 0) & (step == 0)
        last = (pl.program_id(0) == P - 1) & (step == n_steps - 1)

        @pl.when(first)
        def _():
            _gather_start(sh_in, sh_out, ssem, rsem, lsem)

        @pl.when(j == 0)
        def _():
            m_sc[...] = jnp.full_like(m_sc, -3.0e38)
            acc[...] = jnp.zeros_like(acc)

        def update(masked):
            for hh in range(2):
                sl = slice(hh * LANES, (hh + 1) * LANES)
                s2 = lax.dot_general(q_ref[:, sl], k_ref[:, sl], NT, preferred_element_type=F32)
                if masked:
                    s2 = jnp.where(_iota((t, t), 1) <= _iota((t, t), 0), s2, -jnp.inf)
                m_old = m_sc[hh]
                m_new = jnp.maximum(m_old, jnp.max(s2, axis=-1, keepdims=True))
                p = jnp.exp2(s2 - m_new)
                acc[hh] = jnp.exp2(m_old - m_new) * acc[hh] + jnp.dot(p.astype(BF16), v_ref[:, sl],
                                                                      preferred_element_type=F32)
                m_sc[hh] = m_new

        @pl.when(j < i)
        def _():
            update(False)

        @pl.when(j == i)
        def _():
            update(True)
            outs, lses = [], []
            for hh in range(2):
                l = acc[hh, :, _extra_base(hh):_extra_base(hh) + 1]
                outs.append(acc[hh] / l)
                lses.append(m_sc[hh] + jnp.log2(l))
            o = jnp.where(_data_mask(0), outs[0], outs[1])
            o_ref[...] = o.astype(o_ref.dtype)
            og_ref[...] = (o * _sigmoid(g_ref[...].astype(F32))).astype(og_ref.dtype)
            lse_ref[0] = jnp.where(_iota((1, 2), 1) == 0, lses[0], lses[1])

        @pl.when(last)
        def _():
            _gather_wait(sh_in, sh_out, ssem, rsem, lsem)

    qspec = pl.BlockSpec((t, 2 * LANES), lambda p, s, it, jt: (it[s], p))
    kspec = pl.BlockSpec((t, 2 * LANES), lambda p, s, it, jt: (jt[s], p))
    ospec = pl.BlockSpec((t, LANES), lambda p, s, it, jt: (it[s], p))
    lspec = pl.BlockSpec((1, t, 2), lambda p, s, it, jt: (p, it[s], 0))
    res = pl.pallas_call(
        body, name="fox_attn_fwd",
        grid_spec=pltpu.PrefetchScalarGridSpec(
            num_scalar_prefetch=2, grid=(P, n_steps),
            in_specs=[qspec, kspec, kspec, ospec] + [ANY] * ns, out_specs=[ospec, ospec, lspec] + [ANY] * ns,
            scratch_shapes=[pltpu.VMEM((2, t, 1), F32), pltpu.VMEM((2, t, LANES), F32)] + _gather_sems(ns)),
        out_shape=[jax.ShapeDtypeStruct((L, D), BF16), jax.ShapeDtypeStruct((L, D), BF16),
                   jax.ShapeDtypeStruct((P, L, 2), F32)]
        + [jax.ShapeDtypeStruct((4,) + a.shape, a.dtype) for a in shards],
        compiler_params=_cparams(("arbitrary", "arbitrary")),
    )(it, jt, qp, kp, vp, gate, *shards)
    return res[0], res[1], res[2], res[3:]


def _fox_attn_bwd(qb, kp, vp, dop, slabs):
    L = qb.shape[0]
    t = _tile(L, (640, 512, 256, 128))
    n = L // t
    P = FOX_H // 2
    it, jt = _causal_steps(n, True)
    n_steps = it.shape[0]
    ns = len(slabs)

    def body(it_ref, jt_ref, q_ref, k_ref, v_ref, do_ref, *rest):
        sl_in, (dq_ref, dk_ref, dv_ref, dck_ref), sl_out = rest[:ns], rest[ns:ns + 4], rest[ns + 4:2 * ns + 4]
        dk_acc, dv_acc, ssem, rsem = rest[2 * ns + 4:]
        step = pl.program_id(1)
        i, j = it_ref[step], jt_ref[step]

        @pl.when((pl.program_id(0) == 0) & (step == 0))
        def _():
            for cp in _scatter_copies(sl_in, sl_out, ssem, rsem):
                cp.start()

        @pl.when(step == 0)
        def _():
            dq_ref[...] = jnp.zeros_like(dq_ref)

        @pl.when(i == j)
        def _():
            dk_acc[...] = jnp.zeros_like(dk_acc)
            dv_acc[...] = jnp.zeros_like(dv_acc)

        def update(masked):
            rows = pl.ds(pl.multiple_of(i * t, LANES), t)
            for hh in range(2):
                sl = slice(hh * LANES, (hh + 1) * LANES)
                q, k, dov = q_ref[:, sl], k_ref[:, sl], do_ref[:, sl]
                s2 = lax.dot_general(q, k, NT, preferred_element_type=F32)
                if masked:
                    s2 = jnp.where(_iota((t, t), 1) <= _iota((t, t), 0), s2, -jnp.inf)
                p = jnp.exp2(s2)
                ds = (p * lax.dot_general(dov, v_ref[:, sl], NT, preferred_element_type=F32)).astype(BF16)
                dv_acc[hh] += lax.dot_general(p.astype(BF16), dov, TN, preferred_element_type=F32)
                dk_acc[hh] += lax.dot_general(ds, q, TN, preferred_element_type=F32)
                dq_ref[rows, sl] += jnp.dot(ds, k, preferred_element_type=F32)

        @pl.when(i > j)
        def _():
            update(False)

        @pl.when(i == j)
        def _():
            update(True)

        @pl.when(i == n - 1)
        def _():
            dk_ref[...] = jnp.where(_data_mask(0), dk_acc[0], dk_acc[1])
            dv_ref[...] = jnp.where(_data_mask(0), dv_acc[0], dv_acc[1]).astype(dv_ref.dtype)
            col_sums = [dk_acc[hh, :, _extra_base(hh) + 3:_extra_base(hh) + 4] for hh in range(2)]
            dck_ref[0] = -jnp.where(_iota((1, 2), 1) == 0, col_sums[0], col_sums[1])

        @pl.when((pl.program_id(0) == P - 1) & (step == n_steps - 1))
        def _():
            for cp in _scatter_copies(sl_in, sl_out, ssem, rsem):
                cp.wait()

    qspec = pl.BlockSpec((t, 2 * LANES), lambda p, s, it, jt: (it[s], p))
    kspec = pl.BlockSpec((t, 2 * LANES), lambda p, s, it, jt: (jt[s], p))
    ospec = pl.BlockSpec((t, LANES), lambda p, s, it, jt: (jt[s], p))
    res = pl.pallas_call(
        body, name="fox_attn_bwd",
        grid_spec=pltpu.PrefetchScalarGridSpec(
            num_scalar_prefetch=2, grid=(P, n_steps),
            in_specs=[qspec, kspec, kspec, qspec] + [ANY] * ns,
            out_specs=[pl.BlockSpec((L, 2 * LANES), lambda p, s, it, jt: (0, p)), ospec, ospec,
                       pl.BlockSpec((1, t, 2), lambda p, s, it, jt: (p, jt[s], 0))] + [ANY] * ns,
            scratch_shapes=[pltpu.VMEM((2, t, LANES), F32), pltpu.VMEM((2, t, LANES), F32),
                            pltpu.SemaphoreType.DMA((ns, 3)), pltpu.SemaphoreType.DMA((ns, 3))]),
        out_shape=[jax.ShapeDtypeStruct((L, FOX_H * LANES), F32), jax.ShapeDtypeStruct((L, D), F32),
                   jax.ShapeDtypeStruct((L, D), BF16), jax.ShapeDtypeStruct((P, L, 2), F32)]
        + [jax.ShapeDtypeStruct((3,) + a.shape[1:], a.dtype) for a in slabs],
        compiler_params=_cparams(("arbitrary", "arbitrary")),
    )(it, jt, qb, kp, vp, dop, *slabs)
    return res[0], res[1], res[2], res[3], res[4:]


def _hgrn_consts():
    C = HG_C
    r = np.arange(C)[:, None]
    j = np.arange(C)[None, :]
    mats = [j <= r, j > r]
    masks = []
    n = C
    while n >= 2:
        half = n // 2
        mid = (r // n) * n + half - 1
        second = (r % n) >= half
        mats.append(np.where(second, (j > mid) & (j <= r), (j > r) & (j <= mid)))
        masks.append(((r // n) == (j // n)) & ((r % n) >= half) & ((j % n) < half))
        n //= 2
    return (jnp.asarray(np.concatenate(mats, 0).astype(np.float32), BF16),
            jnp.asarray(np.stack(masks).astype(np.float32), F32))


def _hg_pre(hq, hz, h0, h1):
    mx = jnp.maximum(h0, h1)
    e0, e1 = jnp.exp(h0 - mx), jnp.exp(h1 - mx)
    lb = e1 / (e0 + e1)
    sq = _sigmoid(hq)
    sz, snz = _sigmoid(hz), _sigmoid(-hz)
    k = (1.0 - lb) * snz
    a = jnp.log(lb)
    b = jnp.log1p(-lb) + _log_sigmoid(hz)
    g = jnp.maximum(a, b) + jnp.log(1.0 + jnp.exp(-jnp.abs(a - b)))
    return lb, hq * sq, sq, k, sz, snz, g


def _hg_decays(g, rmat):
    hi = g.astype(BF16)
    lo = (g - hi.astype(F32)).astype(BF16)
    d = jnp.dot(rmat, jnp.concatenate([hi, lo], axis=1), preferred_element_type=F32)
    return jnp.exp(d[:, :HG_D] + d[:, HG_D:])


def _hg_intra(q, k, fall, masks):
    C = HG_C
    eye = _iota((C, C), 0) == _iota((C, C), 1)
    a = jnp.where(eye, jnp.sum(q * k, axis=-1, keepdims=True), 0.0)
    for l in range(HG_LEV):
        f = fall[(2 + l) * C:(3 + l) * C]
        a = a + masks[l] * lax.dot_general((q * f).astype(BF16), (k * f).astype(BF16), NT,
                                           preferred_element_type=F32)
    return a


def _hgrn_specs(n_chunks, reverse):
    C = HG_C
    w = HG_HPS * HG_D

    def col(first_head):
        off = first_head // HG_HPS
        if reverse:
            return pl.BlockSpec((C, w), lambda h, c: (n_chunks - 1 - c, off + h))
        return pl.BlockSpec((C, w), lambda h, c: (c, off + h))

    st = pl.BlockSpec((HG_HPS, 1, HG_D, HG_D),
                      (lambda h, c: (h, n_chunks - 1 - c, 0, 0)) if reverse else (lambda h, c: (h, c, 0, 0)))
    consts = [pl.BlockSpec((2, w), lambda h, c: (0, h)), pl.BlockSpec((1, HG_D), lambda h, c: (0, 0)),
              pl.BlockSpec(((2 + HG_LEV) * C, C), lambda h, c: (0, 0)),
              pl.BlockSpec((HG_LEV, C, C), lambda h, c: (0, 0, 0))]
    return col, st, consts


def _hgrn_fwd(proj, hlb, gg, rmat, masks):
    L = proj.shape[0]
    C = HG_C
    nc = L // C
    col, st, consts = _hgrn_specs(nc, False)

    def body(hq_ref, hz_ref, hi_ref, hg_ref, hlb_ref, gg_ref, r_ref, m_ref, og_ref, st_ref, state):
        c = pl.program_id(1)

        @pl.when(c == 0)
        def _():
            state[...] = jnp.zeros_like(state)

        for hh in range(HG_HPS):
            sl = slice(hh * HG_D, (hh + 1) * HG_D)
            v, hg = hi_ref[:, sl], hg_ref[:, sl]
            _, q, _, k, _, _, g = _hg_pre(hq_ref[:, sl], hz_ref[:, sl], hlb_ref[0:1, sl], hlb_ref[1:2, sl])
            fall = _hg_decays(g, r_ref[...])
            fb, fe = fall[0:C], fall[C:2 * C]
            st0 = state[hh]
            st_ref[hh, 0] = st0
            a = _hg_intra(q, k, fall, m_ref[...])
            vb = v.astype(BF16)
            o = jnp.dot(a.astype(BF16), vb, preferred_element_type=F32)
            o = o + lax.dot_general((q * fb).astype(BF16), st0.astype(BF16), NT, preferred_element_type=F32)
            ebc = jnp.exp(jnp.sum(g, axis=0, keepdims=True))
            state[hh] = st0 * ebc + lax.dot_general(vb, (k * fe).astype(BF16), TN, preferred_element_type=F32)
            r = lax.rsqrt(jnp.mean(o * o, axis=-1, keepdims=True) + EPS)
            og_ref[:, sl] = (o * r * gg_ref[...] * (hg * _sigmoid(hg))).astype(og_ref.dtype)

    return pl.pallas_call(
        body, name="hgrn_fwd", grid=(HG_H // HG_HPS, nc),
        in_specs=[col(0), col(HG_H), col(2 * HG_H), col(3 * HG_H)] + consts,
        out_specs=[col(0), st],
        out_shape=[jax.ShapeDtypeStruct((L, D), BF16), jax.ShapeDtypeStruct((HG_H, nc, HG_D, HG_D), F32)],
        scratch_shapes=[pltpu.VMEM((HG_HPS, HG_D, HG_D), F32)],
        compiler_params=_cparams(("parallel", "arbitrary")),
    )(proj, proj, proj, proj, hlb, gg, rmat, masks)


def _hgrn_bwd(proj, dog, states, hlb, gg, rmat, masks):
    L = proj.shape[0]
    C = HG_C
    nc = L // C
    col, st, consts = _hgrn_specs(nc, True)

    def body(hq_ref, hz_ref, hi_ref, hg_ref, do_ref, hlb_ref, gg_ref, r_ref, m_ref, st_ref,
             dq_ref, dz_ref, di_ref, dg_ref, dlb_ref, dgg_ref, dstate):
        c = pl.program_id(1)

        @pl.when(c == 0)
        def _():
            dstate[...] = jnp.zeros_like(dstate)
            dlb_ref[...] = jnp.zeros_like(dlb_ref)
            dgg_ref[...] = jnp.zeros_like(dgg_ref)

        for hh in range(HG_HPS):
            bwd_head(c, hh, slice(hh * HG_D, (hh + 1) * HG_D), hq_ref, hz_ref, hi_ref, hg_ref, do_ref, hlb_ref, gg_ref,
                     r_ref, m_ref, st_ref, dq_ref, dz_ref, di_ref, dg_ref, dlb_ref, dgg_ref, dstate)

    def bwd_head(c, hh, sl, hq_ref, hz_ref, hi_ref, hg_ref, do_ref, hlb_ref, gg_ref, r_ref, m_ref, st_ref,
                 dq_ref, dz_ref, di_ref, dg_ref, dlb_ref, dgg_ref, dstate):
        hq, hz, v, hg = hq_ref[:, sl], hz_ref[:, sl], hi_ref[:, sl], hg_ref[:, sl]
        dout = do_ref[:, sl].astype(F32)
        gain = gg_ref[...]
        masks_v = m_ref[...]
        lb, q, sq, k, sz, snz, g = _hg_pre(hq, hz, hlb_ref[0:1, sl], hlb_ref[1:2, sl])
        fall = _hg_decays(g, r_ref[...])
        fb, fe = fall[0:C], fall[C:2 * C]
        a = _hg_intra(q, k, fall, masks_v)
        st0 = st_ref[hh, 0]
        st0b = st0.astype(BF16)
        ebc = jnp.exp(jnp.sum(g, axis=0, keepdims=True))
        qb, ke, vb = (q * fb).astype(BF16), (k * fe).astype(BF16), v.astype(BF16)
        ab = a.astype(BF16)
        o = jnp.dot(ab, vb, preferred_element_type=F32) + lax.dot_general(qb, st0b, NT, preferred_element_type=F32)
        r = lax.rsqrt(jnp.mean(o * o, axis=-1, keepdims=True) + EPS)
        oh = o * r
        sg = _sigmoid(hg)
        d_on = dout * (hg * sg)
        dhg = dout * (oh * gain) * (sg * (1.0 + hg * (1.0 - sg)))
        dgg_ref[hh] += jnp.sum(d_on * oh, axis=0, keepdims=True)
        dxh = d_on * gain
        do = r * (dxh - oh * jnp.mean(dxh * oh, axis=-1, keepdims=True))
        dob = do.astype(BF16)
        dsp = dstate[hh]
        dspb = dsp.astype(BF16)
        causal = _iota((C, C), 0) >= _iota((C, C), 1)
        da = jnp.where(causal, lax.dot_general(dob, vb, NT, preferred_element_type=F32), 0.0)
        diag = jnp.sum(do * v, axis=-1, keepdims=True)
        dv = lax.dot_general(ab, dob, TN, preferred_element_type=F32)
        dv = dv + lax.dot_general(ke, dspb, NT, preferred_element_type=F32)
        xq = jnp.dot(dob, st0b, preferred_element_type=F32)
        xk = jnp.dot(vb, dspb, preferred_element_type=F32)
        dq = diag * k + fb * xq
        dk = diag * q + fe * xk
        ke_xk = ke.astype(F32) * xk
        db = qb.astype(F32) * xq - ke_xk
        for l in range(HG_LEV):
            f = fall[(2 + l) * C:(3 + l) * C]
            dal = (masks_v[l] * da).astype(BF16)
            ql, kl = (q * f).astype(BF16), (k * f).astype(BF16)
            xq = jnp.dot(dal, kl, preferred_element_type=F32)
            xk = lax.dot_general(dal, ql, TN, preferred_element_type=F32)
            dq = dq + f * xq
            dk = dk + f * xk
            db = db + ql.astype(F32) * xq - kl.astype(F32) * xk
        dstate[hh] = dsp * ebc + lax.dot_general(dob, qb, TN, preferred_element_type=F32)
        triu = (_iota((C, C), 0) <= _iota((C, C), 1)).astype(F32)
        dg = jnp.dot(triu, db, precision=HIGHEST, preferred_element_type=F32)
        dg = dg + jnp.sum(st0 * ebc * dsp, axis=0, keepdims=True) + jnp.sum(ke_xk, axis=0, keepdims=True)
        keep = _row_ids((nc - 1 - c) * C, C) >= ROW0
        dg = jnp.where(keep, dg, 0.0)
        dk = jnp.where(keep, dk, 0.0)
        f_gate = lb + (1.0 - lb) * sz
        dfdz = (1.0 - lb) * sz * snz
        dz_ref[:, sl] = (dg * dfdz / f_gate - dk * dfdz).astype(dz_ref.dtype)
        dlb_ref[:, sl] += jnp.sum(dg * snz / f_gate - dk * snz, axis=0, keepdims=True)
        dq_ref[:, sl] = jnp.where(keep, dq * (sq * (1.0 + hq * (1.0 - sq))), 0.0).astype(dq_ref.dtype)
        di_ref[:, sl] = jnp.where(keep, dv, 0.0).astype(di_ref.dtype)
        dg_ref[:, sl] = jnp.where(keep, dhg, 0.0).astype(dg_ref.dtype)

    w = HG_HPS * HG_D
    outs = pl.pallas_call(
        body, name="hgrn_bwd", grid=(HG_H // HG_HPS, nc),
        in_specs=[col(0), col(HG_H), col(2 * HG_H), col(3 * HG_H), col(0)] + consts + [st],
        out_specs=[col(0), col(0), col(0), col(0), pl.BlockSpec((1, w), lambda h, c: (0, h)),
                   pl.BlockSpec((HG_HPS, 1, HG_D), lambda h, c: (h, 0, 0))],
        out_shape=[jax.ShapeDtypeStruct((L, D), BF16)] * 4 + [jax.ShapeDtypeStruct((1, D), F32),
                                                              jax.ShapeDtypeStruct((HG_H, 1, HG_D), F32)],
        scratch_shapes=[pltpu.VMEM((HG_HPS, HG_D, HG_D), F32)],
        compiler_params=_cparams(("parallel", "arbitrary")),
    )(proj, proj, proj, proj, dog, hlb, gg, rmat, masks, states)
    return outs


def _ffn_fwd(h, norm_gain, wg, wu, wo, tag):
    hn = _rms_fwd(h, norm_gain, f"{tag}_norm")
    g = _matmul(hn, wg, out_dtype=BF16, name=f"{tag}_gate")
    u = _matmul(hn, wu, out_dtype=BF16, name=f"{tag}_up")
    act = _swiglu_fwd(g, u, f"{tag}_act")
    h_out = _matmul(act, wo, add=h, name=f"{tag}_out")
    return h_out, (h, hn, g, u, act)


def _ffn_bwd(dh, saved, norm_gain, wg, wu, wo, tag):
    h, hn, g, u, act = saved
    dact = _matmul(dh, wo, tb=True, out_dtype=BF16, name=f"{tag}_dact")
    d_wo = _matmul(act, dh, ta=True, name=f"{tag}_dwo")
    dg, du = _swiglu_bwd(dact, g, u, f"{tag}_dact_bwd")
    dhn = _matmul(dg, wg, tb=True, name=f"{tag}_dhn_g")
    dhn = _matmul(du, wu, tb=True, add=dhn, name=f"{tag}_dhn_u")
    d_wg = _matmul(hn, dg, ta=True, name=f"{tag}_dwg")
    d_wu = _matmul(hn, du, ta=True, name=f"{tag}_dwu")
    dh, d_gain = _rms_bwd(h, dhn, norm_gain, dh, f"{tag}_norm_bwd")
    return dh, d_gain, jnp.concatenate([d_wg, d_wu], axis=1), d_wo


def _local_step(h0, tgt, w, late_shards, layers):
    L = h0.shape[0]
    gmat = jnp.asarray(np.kron(np.eye(FOX_H), np.ones((FOX_DH, FOX_DH))).astype(np.float32), BF16)
    rmat, lmasks = _hgrn_consts()
    an, fn_ = w["attn_norm"], w["ffn_norm"]
    qg = jnp.tile(w["fox_q_norm"], (1, FOX_H))
    kg = jnp.tile(w["fox_k_norm"], (1, FOX_H))
    bf = jnp.pad(w["fox_b_f"], ((0, 0), (0, LANES - FOX_H)))
    fw = w["fox_w_in"]
    f_wq, f_wk, f_wv, f_wg = (fw[:, i * D:(i + 1) * D] for i in range(4))
    f_wf = jnp.pad(fw[:, 4 * D:], ((0, 0), (0, LANES - FOX_H)))
    f_wo = w["fox_w_out"]

    hn0 = _rms_fwd(h0, an[0:1], "fox_norm")
    q_raw = _matmul(hn0, f_wq, name="fox_q")
    k_raw = _matmul(hn0, f_wk, name="fox_k")
    v = _matmul(hn0, f_wv, out_dtype=BF16, name="fox_v")
    gate = _matmul(hn0, f_wg, out_dtype=BF16, name="fox_gate")
    flog = _matmul(hn0, f_wf, name="fox_flog")
    cq, ck = _fox_cumsum_fwd(flog, bf)
    qp, kp, vp = _fox_pack_fwd(q_raw, k_raw, v, cq, ck, qg, kg, gmat)
    o, og, lse2, gathered = _fox_attn_fwd(qp, kp, vp, gate, [late_shards[n] for n in LATE_NAMES])
    late = {n: _unshard(n, g, layers[n]) for n, g in zip(LATE_NAMES, gathered)}
    h_wi, h_wo = late["hgrn_w_in"][0], late["hgrn_w_out"][0]
    ffw = [(late["ffn_w_in"][i][:, :FFN], late["ffn_w_in"][i][:, FFN:], late["ffn_w_out"][i]) for i in range(2)]
    h1 = _matmul(og, f_wo, add=h0, name="fox_out")
    h2, ffn0 = _ffn_fwd(h1, fn_[0:1], *ffw[0], "ffn0")

    hn2 = _rms_fwd(h2, an[1:2], "hgrn_norm")
    proj = _matmul(hn2, h_wi, name="hgrn_in")
    og1, states = _hgrn_fwd(proj, w["hgrn_lower_bounds"], w["hgrn_g_norm"], rmat, lmasks)
    h3 = _matmul(og1, h_wo, add=h2, name="hgrn_out")
    h4, ffn1 = _ffn_fwd(h3, fn_[1:2], *ffw[1], "ffn1")

    loss, dh, d_final = _loss_bwd(h4, tgt, w["final_norm"])

    dh, d_fn1, d_ffn_in1, d_ffn_out1 = _ffn_bwd(dh, ffn1, fn_[1:2], *ffw[1], "ffn1")
    dog1 = _matmul(dh, h_wo, tb=True, out_dtype=BF16, name="hgrn_dog")
    d_h_wo = _matmul(og1, dh, ta=True, name="hgrn_dwo")
    dpq, dpz, dpi, dpg, d_lb, d_gg = _hgrn_bwd(proj, dog1, states, w["hgrn_lower_bounds"], w["hgrn_g_norm"],
                                               rmat, lmasks)
    dproj = jnp.concatenate([dpq, dpz, dpi, dpg], axis=1)
    dhn2 = _matmul(dproj, h_wi, tb=True, name="hgrn_dhn")
    d_h_wi = _matmul(hn2, dproj, ta=True, name="hgrn_dwi")
    dh, d_an1 = _rms_bwd(h2, dhn2, an[1:2], dh, "hgrn_norm_bwd")

    dh, d_fn0, d_ffn_in0, d_ffn_out0 = _ffn_bwd(dh, ffn0, fn_[0:1], *ffw[0], "ffn0")
    late_grads = dict(hgrn_w_in=d_h_wi[None], hgrn_w_out=d_h_wo[None],
                      ffn_w_in=jnp.stack([d_ffn_in0, d_ffn_in1]), ffn_w_out=jnp.stack([d_ffn_out0, d_ffn_out1]))
    pair_late, send_late = _pair_sums([_to_shards(n, late_grads[n]) for n in LATE_NAMES], "late")

    dog = _matmul(dh, f_wo, tb=True, out_dtype=BF16, name="fox_dog")
    d_f_wo = _matmul(og, dh, ta=True, name="fox_dwo")

    def by_head(a):
        return jnp.pad(a.transpose(1, 0, 2).reshape(L, FOX_H), ((0, 0), (0, LANES - FOX_H)))

    qb = _fox_pack_bias(qp, cq, by_head(lse2))
    dop, dgate = _fox_pack_bwd(dog, o, gate)
    dqp, dk, dv, dck, recv_late = _fox_attn_bwd(qb, kp, vp, dop, send_late)
    (dq_raw, dk_raw, dc_q), (d_qg, d_kg) = _fox_unpack_bwd(q_raw, k_raw, dqp, dk, qg, kg, gmat)
    dflog, d_bf = _fox_cumsum_bwd(dc_q, by_head(dck), flog, bf)
    dproj0 = jnp.concatenate([dq_raw, dk_raw, dv, dgate, dflog.astype(BF16)], axis=1)
    f_wall = jnp.concatenate([f_wq, f_wk, f_wv, f_wg, f_wf], axis=1)
    dhn0 = _matmul(dproj0, f_wall, tb=True, name="fox_dhn")
    d_f_wall = _matmul(hn0, dproj0, ta=True, name="fox_dwi")
    d_f_wi = d_f_wall[:, :4 * D + FOX_H]
    dh, d_an0 = _rms_bwd(h0, dhn0, an[0:1], dh, "fox_norm_bwd")

    fox_grads = dict(fox_w_in=d_f_wi[None], fox_w_out=d_f_wo[None])
    pair_fox, send_fox = _pair_sums([_to_shards(n, fox_grads[n]) for n in FOX_NAMES], "fox")
    recv_fox = _chip_scatter(send_fox, "fox")
    halves = _chip_sums(pair_fox, recv_fox, "fox") + _chip_sums(pair_late, recv_late, "late")
    theirs = _sibling_exchange(halves)
    south = lax.axis_index("c") == 0
    big = {n: jnp.where(south, jnp.concatenate([m, t]), jnp.concatenate([t, m]))
           for n, m, t in zip(FOX_NAMES + LATE_NAMES, halves, theirs)}
    small = dict(attn_norm=jnp.concatenate([d_an0, d_an1]), ffn_norm=jnp.concatenate([d_fn0, d_fn1]),
                 final_norm=d_final, lb_raw=d_lb, q_gain=d_qg, k_gain=d_kg, b_f=d_bf,
                 g_gain=d_gg.reshape(1, D))
    return loss, dh, big, small


def _me():
    return lax.axis_index("x"), lax.axis_index("y"), lax.axis_index("c")


def _flip(v, bit):
    return 1 - v if bit else v


def _chip_allgather(arrs):
    n = len(arrs)

    def body(*refs):
        _gather_start(refs[:n], refs[n:2 * n], *refs[2 * n:])
        _gather_wait(refs[:n], refs[n:2 * n], *refs[2 * n:])

    return pl.pallas_call(
        body, name="chip_allgather", in_specs=[ANY] * n, out_specs=[ANY] * n,
        out_shape=[jax.ShapeDtypeStruct((4,) + a.shape, a.dtype) for a in arrs],
        scratch_shapes=_gather_sems(n),
    )(*arrs)


def _chip_peers():
    x, y, c = _me()
    return [(1 - x, y, c), (x, 1 - y, c), (1 - x, 1 - y, c)]


def _gather_sems(n):
    return [pltpu.SemaphoreType.DMA((n, 3)), pltpu.SemaphoreType.DMA((n, 3)), pltpu.SemaphoreType.DMA((n,))]


def _gather_copies(ins, outs, ssem, rsem, lsem, with_recvs):
    x, y, _ = _me()
    local, sends, recvs = [], [], []
    for a in range(len(ins)):
        local.append(pltpu.make_async_copy(ins[a], outs[a].at[2 * x + y], lsem.at[a]))
        for k, peer in enumerate(_chip_peers()):
            sends.append(pltpu.make_async_remote_copy(ins[a], outs[a].at[2 * x + y], ssem.at[a, k], rsem.at[a, k],
                                                      device_id=peer, device_id_type=MESH))
            if with_recvs:
                recvs.append(pltpu.make_async_remote_copy(ins[a], outs[a].at[2 * peer[0] + peer[1]], ssem.at[a, k],
                                                          rsem.at[a, k], device_id=peer, device_id_type=MESH))
    return local, sends, recvs


def _gather_start(ins, outs, ssem, rsem, lsem):
    local, sends, _ = _gather_copies(ins, outs, ssem, rsem, lsem, False)
    for cp in local + sends:
        cp.start()


def _gather_wait(ins, outs, ssem, rsem, lsem):
    local, sends, recvs = _gather_copies(ins, outs, ssem, rsem, lsem, True)
    for cp in local:
        cp.wait()
    for cp in sends:
        cp.wait_send()
    for cp in recvs:
        cp.wait_recv()


def _scatter_copies(ins, outs, ssem, rsem):
    copies = []
    for a in range(len(ins)):
        for k, peer in enumerate(_chip_peers()):
            copies.append(pltpu.make_async_remote_copy(ins[a].at[2 * peer[0] + peer[1]], outs[a].at[k], ssem.at[a, k],
                                                       rsem.at[a, k], device_id=peer, device_id_type=MESH))
    return copies


def _device_allgather(arr):
    def body(in_ref, out_ref, ssem, rsem, lsem):
        x, y, c = _me()
        me = 4 * x + 2 * y + c
        peers = [(_flip(x, k & 4), _flip(y, k & 2), _flip(c, k & 1)) for k in range(1, 8)]
        local = pltpu.make_async_copy(in_ref, out_ref.at[me], lsem)
        local.start()
        sends = []
        for k, peer in enumerate(peers):
            cp = pltpu.make_async_remote_copy(in_ref, out_ref.at[me], ssem.at[k], rsem.at[k],
                                              device_id=peer, device_id_type=MESH)
            cp.start()
            sends.append(cp)
        local.wait()
        for cp in sends:
            cp.wait_send()
        for k, peer in enumerate(peers):
            pltpu.make_async_remote_copy(in_ref, out_ref.at[4 * peer[0] + 2 * peer[1] + peer[2]], ssem.at[k],
                                         rsem.at[k], device_id=peer, device_id_type=MESH).wait_recv()

    return pl.pallas_call(
        body, name="device_allgather", in_specs=[ANY], out_specs=ANY,
        out_shape=jax.ShapeDtypeStruct((8,) + arr.shape, arr.dtype),
        scratch_shapes=[pltpu.SemaphoreType.DMA((7,)), pltpu.SemaphoreType.DMA((7,)), pltpu.SemaphoreType.DMA],
    )(arr)


def _sibling_send_other_half(arrs, tag):
    n = len(arrs)

    def body(*refs):
        ins, outs = refs[:n], refs[n:2 * n]
        ssem, rsem = refs[2 * n:]
        x, y, c = _me()
        cps = []
        for a in range(n):
            half = ins[a].shape[1] // 2
            src = ins[a].at[:, pl.ds((1 - c) * half, half), :]
            cp = pltpu.make_async_remote_copy(src, outs[a], ssem.at[a], rsem.at[a],
                                              device_id=(x, y, 1 - c), device_id_type=MESH)
            cp.start()
            cps.append(cp)
        for cp in cps:
            cp.wait()

    return pl.pallas_call(
        body, name=f"grad_sibling_swap_{tag}", in_specs=[ANY] * n, out_specs=[ANY] * n,
        out_shape=[jax.ShapeDtypeStruct((4, a.shape[1] // 2, a.shape[2]), a.dtype) for a in arrs],
        scratch_shapes=[pltpu.SemaphoreType.DMA((n,)), pltpu.SemaphoreType.DMA((n,))],
    )(*arrs)


def _chip_scatter(arrs, tag):
    n = len(arrs)

    def body(*refs):
        cps = _scatter_copies(refs[:n], refs[n:2 * n], *refs[2 * n:])
        for cp in cps:
            cp.start()
        for cp in cps:
            cp.wait()

    return pl.pallas_call(
        body, name=f"grad_chip_scatter_{tag}", in_specs=[ANY] * n, out_specs=[ANY] * n,
        out_shape=[jax.ShapeDtypeStruct((3,) + a.shape[1:], a.dtype) for a in arrs],
        scratch_shapes=[pltpu.SemaphoreType.DMA((n, 3)), pltpu.SemaphoreType.DMA((n, 3))],
    )(*arrs)


def _sibling_exchange(arrs):
    n = len(arrs)

    def body(*refs):
        ins, outs = refs[:n], refs[n:2 * n]
        ssem, rsem = refs[2 * n:]
        x, y, c = _me()
        cps = [pltpu.make_async_remote_copy(ins[a], outs[a], ssem.at[a], rsem.at[a], device_id=(x, y, 1 - c),
                                            device_id_type=MESH) for a in range(n)]
        for cp in cps:
            cp.start()
        for cp in cps:
            cp.wait()

    return pl.pallas_call(
        body, name="grad_sibling_exchange", in_specs=[ANY] * n, out_specs=[ANY] * n,
        out_shape=[jax.ShapeDtypeStruct(a.shape, a.dtype) for a in arrs],
        scratch_shapes=[pltpu.SemaphoreType.DMA((n,)), pltpu.SemaphoreType.DMA((n,))],
    )(*arrs)


def _pair_sums(grads, tag):
    c = lax.axis_index("c")
    got = _sibling_send_other_half(grads, tag)
    pair = []
    for i, (g, t) in enumerate(zip(grads, got)):
        half = g.shape[1] // 2
        mine = lax.dynamic_slice_in_dim(g, c * half, half, axis=1)
        pair.append(_add([mine.reshape(4 * half, -1), t.reshape(4 * half, -1)], f"grad_pair_add_{tag}{i}")
                    .reshape(t.shape))
    return pair, [p.astype(BF16) for p in pair]


def _chip_sums(pair, recv, tag):
    x, y, _ = _me()
    out = []
    for i, (p, r) in enumerate(zip(pair, recv)):
        own = lax.dynamic_index_in_dim(p, 2 * x + y, axis=0, keepdims=False)
        out.append(_add([own, r[0], r[1], r[2]], f"grad_chip_add_{tag}{i}"))
    return out


SMALL_ROWS = 32


def _small_finalize(gathered, hlb, fold64, fold128):
    def body(g_ref, hlb_ref, f64_ref, f128_ref, rows_ref, qk_ref, gg_ref, lb_ref):
        tot = g_ref[0]
        for d in range(1, 8):
            tot = tot + g_ref[d]
        rows_ref[...] = tot
        qk_ref[...] = jnp.dot(rows_ref[6:8, :], f64_ref[...], precision=HIGHEST, preferred_element_type=F32)
        gg_ref[...] = jnp.dot(rows_ref[9:10, :], f128_ref[...], precision=HIGHEST, preferred_element_type=F32)
        h0, h1 = hlb_ref[0:1, :], hlb_ref[1:2, :]
        mx = jnp.maximum(h0, h1)
        e0, e1 = jnp.exp(h0 - mx), jnp.exp(h1 - mx)
        lb = e1 / (e0 + e1)
        d1 = rows_ref[5:6, :] * lb * (1.0 - lb)
        lb_ref[...] = jnp.where(_iota((2, 1), 0) == 0, -d1, d1)

    return pl.pallas_call(
        body, name="small_finalize",
        out_shape=[jax.ShapeDtypeStruct((SMALL_ROWS, D), F32), jax.ShapeDtypeStruct((2, FOX_DH), F32),
                   jax.ShapeDtypeStruct((1, HG_D), F32), jax.ShapeDtypeStruct((2, D), F32)],
    )(gathered, hlb, fold64, fold128)


FOX_NAMES = ("fox_w_in", "fox_w_out")
LATE_NAMES = ("hgrn_w_in", "hgrn_w_out", "ffn_w_in", "ffn_w_out")
BIG_NAMES = FOX_NAMES + LATE_NAMES
COL_SHARDED = ("fox_w_in", "hgrn_w_in", "ffn_w_in")


def _shard2d(name, a):
    return a.reshape(-1, a.shape[-1])


def _unshard(name, g, layers):
    if name in COL_SHARDED:
        k = g.shape[1] // layers
        return g.reshape(4, layers, k, g.shape[2]).transpose(1, 2, 0, 3).reshape(layers, k, 4 * g.shape[2])
    r = g.shape[1] // layers
    return g.reshape(4, layers, r, g.shape[2]).transpose(1, 0, 2, 3).reshape(layers, 4 * r, g.shape[2])


def _to_shards(name, g):
    layers = g.shape[0]
    if name in COL_SHARDED:
        k, n = g.shape[1], g.shape[2] // 4
        return g.reshape(layers, k, 4, n).transpose(2, 0, 1, 3).reshape(4, layers * k, n)
    r = g.shape[1] // 4
    return g.reshape(layers, 4, r, g.shape[2]).transpose(1, 0, 2, 3).reshape(4, layers * r, g.shape[2])


def kernel(x, meta_tokens, attn_norm, ffn_norm, final_norm, fox_w_in, fox_b_f, fox_q_norm, fox_k_norm, fox_w_out, hgrn_w_in, hgrn_lower_bounds, hgrn_g_norm, hgrn_w_out, ffn_w_in, ffn_w_out, loss_target, m_meta_tokens, m_attn_norm, m_ffn_norm, m_final_norm, m_fox_w_in, m_fox_b_f, m_fox_q_norm, m_fox_k_norm, m_fox_w_out, m_hgrn_w_in, m_hgrn_lower_bounds, m_hgrn_g_norm, m_hgrn_w_out, m_ffn_w_in, m_ffn_w_out, v_meta_tokens, v_attn_norm, v_ffn_norm, v_final_norm, v_fox_w_in, v_fox_b_f, v_fox_q_norm, v_fox_k_norm, v_fox_w_out, v_hgrn_w_in, v_hgrn_lower_bounds, v_hgrn_g_norm, v_hgrn_w_out, v_ffn_w_in, v_ffn_w_out):
    params = dict(meta_tokens=meta_tokens, attn_norm=attn_norm, ffn_norm=ffn_norm, final_norm=final_norm,
                  fox_w_in=fox_w_in, fox_b_f=fox_b_f, fox_q_norm=fox_q_norm, fox_k_norm=fox_k_norm,
                  fox_w_out=fox_w_out, hgrn_w_in=hgrn_w_in, hgrn_lower_bounds=hgrn_lower_bounds,
                  hgrn_g_norm=hgrn_g_norm, hgrn_w_out=hgrn_w_out, ffn_w_in=ffn_w_in, ffn_w_out=ffn_w_out)
    mom_m = dict(meta_tokens=m_meta_tokens, attn_norm=m_attn_norm, ffn_norm=m_ffn_norm, final_norm=m_final_norm,
                 fox_w_in=m_fox_w_in, fox_b_f=m_fox_b_f, fox_q_norm=m_fox_q_norm, fox_k_norm=m_fox_k_norm,
                 fox_w_out=m_fox_w_out, hgrn_w_in=m_hgrn_w_in, hgrn_lower_bounds=m_hgrn_lower_bounds,
                 hgrn_g_norm=m_hgrn_g_norm, hgrn_w_out=m_hgrn_w_out, ffn_w_in=m_ffn_w_in, ffn_w_out=m_ffn_w_out)
    mom_v = dict(meta_tokens=v_meta_tokens, attn_norm=v_attn_norm, ffn_norm=v_ffn_norm, final_norm=v_final_norm,
                 fox_w_in=v_fox_w_in, fox_b_f=v_fox_b_f, fox_q_norm=v_fox_q_norm, fox_k_norm=v_fox_k_norm,
                 fox_w_out=v_fox_w_out, hgrn_w_in=v_hgrn_w_in, hgrn_lower_bounds=v_hgrn_lower_bounds,
                 hgrn_g_norm=v_hgrn_g_norm, hgrn_w_out=v_hgrn_w_out, ffn_w_in=v_ffn_w_in, ffn_w_out=v_ffn_w_out)
    names = list(params)
    seq = x.shape[1]
    xi, yi, ci = _me()

    shards = {n: _shard2d(n, params[n]).astype(BF16) for n in BIG_NAMES}
    layers = {n: params[n].shape[0] for n in BIG_NAMES}
    gathered = _chip_allgather([shards[n] for n in FOX_NAMES] + [meta_tokens])
    w = {n: _unshard(n, g, 1)[0] for n, g in zip(FOX_NAMES, gathered[:-1])}
    meta_full = gathered[-1].transpose(1, 0, 2).reshape(N_META, D)
    w.update(attn_norm=attn_norm, ffn_norm=ffn_norm, final_norm=final_norm.reshape(1, D), fox_b_f=fox_b_f,
             fox_q_norm=fox_q_norm, fox_k_norm=fox_k_norm, hgrn_lower_bounds=hgrn_lower_bounds,
             hgrn_g_norm=hgrn_g_norm)

    h0 = jnp.concatenate([jnp.zeros((ROW0, D), F32), meta_full, x[0]], axis=0)
    tgt = jnp.concatenate([jnp.zeros((PAD, D), F32), loss_target[0]], axis=0)
    loss, dh0, big, small = _local_step(h0, tgt, w, {n: shards[n] for n in LATE_NAMES}, layers)
    loss = lax.psum(loss, ("x", "y", "c"))
    grad_x = dh0[PAD:][None]
    grads = {n: big[n].reshape(params[n].shape) for n in BIG_NAMES}

    rows = jnp.concatenate([small["attn_norm"], small["ffn_norm"], small["final_norm"], small["lb_raw"],
                            small["q_gain"], small["k_gain"],
                            jnp.pad(small["b_f"], ((0, 0), (0, D - LANES))), small["g_gain"],
                            dh0[ROW0:PAD], jnp.zeros((SMALL_ROWS - 10 - N_META, D), F32)], axis=0)
    allrows = _device_allgather(rows)
    fold64 = jnp.asarray(np.tile(np.eye(FOX_DH, dtype=np.float32), (FOX_H, 1)))
    fold128 = jnp.asarray(np.tile(np.eye(HG_D, dtype=np.float32), (HG_H, 1)))
    tot, qk, gg, dlb = _small_finalize(allrows, hgrn_lower_bounds, fold64, fold128)
    grads.update(attn_norm=tot[0:2], ffn_norm=tot[2:4], final_norm=tot[4], hgrn_lower_bounds=dlb,
                 fox_q_norm=qk[0:1], fox_k_norm=qk[1:2], fox_b_f=tot[8:9, :FOX_H], hgrn_g_norm=gg,
                 meta_tokens=lax.dynamic_slice_in_dim(tot[10:10 + N_META], (2 * xi + yi) * (D // 4), D // 4, axis=1))

    delta, new_m, new_v = {}, {}, {}
    for n in BIG_NAMES + ("meta_tokens",):
        d_, m_, v_ = _adamw(_shard2d(n, params[n]), _shard2d(n, grads[n]), _shard2d(n, mom_m[n]),
                            _shard2d(n, mom_v[n]), f"adamw_{n}")
        delta[n], new_m[n], new_v[n] = (t.reshape(params[n].shape) for t in (d_, m_, v_))
    small_names = [n for n in names if n not in BIG_NAMES and n != "meta_tokens"]

    def pack(d):
        return jnp.concatenate([jnp.pad(d[n].reshape(-1, d[n].shape[-1]), ((0, 0), (0, D - d[n].shape[-1])))
                                for n in small_names], axis=0)

    packed = [pack(t) for t in (params, grads, mom_m, mom_v)]
    n_rows = packed[0].shape[0]
    packed = [jnp.pad(t, ((0, 16 - n_rows), (0, 0))) for t in packed]
    res = _adamw(*packed, "adamw_small")
    r0 = 0
    for n in small_names:
        nr = params[n].reshape(-1, params[n].shape[-1]).shape[0]
        for dst, src in zip((delta, new_m, new_v), res):
            dst[n] = src[r0:r0 + nr, :params[n].shape[-1]].reshape(params[n].shape)
        r0 += nr

    return (loss, grad_x, *[grads[n] for n in names], *[delta[n] for n in names],
            *[new_m[n] for n in names], *[new_v[n] for n in names])
```

```python
import functools

import numpy as np
import jax
import jax.numpy as jnp
from jax import lax
from jax.experimental import pallas as pl
from jax.experimental.pallas import tpu as pltpu

F32, BF16 = jnp.float32, jnp.bfloat16
HIGHEST = lax.Precision.HIGHEST

D = 1024
N_META = 16
PAD = 128
ROW0 = PAD - N_META
FOX_H, FOX_DH = 16, 64
HG_H, HG_D = 8, 128
HG_C = 128
HG_LEV = 7
HG_HPS = 4
FFN = 2816
EPS = 1e-6
BIG = 1e30
LOG2E = 1.4426950408889634
LANES = 128
MXU_N = 256
VMEM_LIMIT = 48 * 1024 * 1024
ROW_TILES = (640, 512, 384, 320, 256, 128, 64, 32, 16, 8)

ADAM_LR, ADAM_B1, ADAM_B2, ADAM_EPS, ADAM_WD, ADAM_STEP = 0.001, 0.9, 0.999, 1e-08, 0.01, 10

MESH = pl.DeviceIdType.MESH
ANY = pl.BlockSpec(memory_space=pl.ANY)
NT = (((1,), (1,)), ((), ()))
TN = (((0,), (0,)), ((), ()))


def _tile(n, cands=ROW_TILES, cap=None):
    for c in cands:
        if n % c == 0 and (cap is None or c <= cap):
            return c
    return n


def _cparams(sem):
    return pltpu.CompilerParams(dimension_semantics=sem, vmem_limit_bytes=VMEM_LIMIT)


def _sigmoid(x):
    return jax.nn.sigmoid(x)


def _log_sigmoid(x):
    return jnp.minimum(x, 0.0) - jnp.log(1.0 + jnp.exp(-jnp.abs(x)))


def _iota(shape, dim):
    return lax.broadcasted_iota(jnp.int32, shape, dim)


def _matmul(a, b, *, ta=False, tb=False, out_dtype=F32, add=None, name):
    if ta:
        kdim, m = a.shape
    else:
        m, kdim = a.shape
    n = b.shape[0] if tb else b.shape[1]
    if ta:
        tm = m if m <= 1024 else _tile(m, (1408, 1024, 512, 256, 128))
        tk = _tile(kdim)
    else:
        tm = _tile(m)
        tk = kdim if kdim <= 4096 else _tile(kdim, (2048, 1024, 512))
    tn = n if n <= 1024 else _tile(n, (1408, 1024, 512, 256, 128))
    nk = kdim // tk
    dn = (((0 if ta else 1,), (1 if tb else 0,)), ((), ()))

    def body(*refs):
        if add is None:
            a_ref, b_ref, o_ref, acc_ref = refs
        else:
            a_ref, b_ref, add_ref, o_ref, acc_ref = refs
        k = pl.program_id(2)

        @pl.when(k == 0)
        def _():
            acc_ref[...] = jnp.zeros_like(acc_ref)

        acc_ref[...] += lax.dot_general(a_ref[...].astype(BF16), b_ref[...].astype(BF16), dn,
                                        preferred_element_type=F32)

        @pl.when(k == nk - 1)
        def _():
            r = acc_ref[...]
            if add is not None:
                r = r + add_ref[...].astype(F32)
            o_ref[...] = r.astype(o_ref.dtype)

    a_spec = pl.BlockSpec((tk, tm), lambda j, i, k: (k, i)) if ta else pl.BlockSpec((tm, tk), lambda j, i, k: (i, k))
    b_spec = pl.BlockSpec((tn, tk), lambda j, i, k: (j, k)) if tb else pl.BlockSpec((tk, tn), lambda j, i, k: (k, j))
    o_spec = pl.BlockSpec((tm, tn), lambda j, i, k: (i, j))
    ins, specs = [a, b], [a_spec, b_spec]
    if add is not None:
        ins.append(add)
        specs.append(o_spec)
    return pl.pallas_call(
        body, name=name, grid=(n // tn, m // tm, nk), in_specs=specs, out_specs=o_spec,
        out_shape=jax.ShapeDtypeStruct((m, n), out_dtype),
        scratch_shapes=[pltpu.VMEM((tm, tn), F32)],
        compiler_params=_cparams(("parallel", "parallel", "arbitrary")),
    )(*ins)


def _rowwise(fn, ins, bcast, outs, accs, *, name, reverse=False, carry=None, as_refs=False):
    rows = ins[0].shape[0]
    per_row = sum(x.shape[1] * x.dtype.itemsize for x in ins) + sum(c * jnp.dtype(d).itemsize for c, d in outs)
    tm = _tile(rows, cap=max(8, (10 * 1024 * 1024) // per_row))
    n = rows // tm
    n_in, n_b, n_o, n_a = len(ins), len(bcast), len(outs), len(accs)

    def body(*refs):
        in_refs = refs[:n_in]
        b_refs = refs[n_in:n_in + n_b]
        o_refs = refs[n_in + n_b:n_in + n_b + n_o]
        a_refs = refs[n_in + n_b + n_o:n_in + n_b + n_o + n_a]
        c_refs = refs[n_in + n_b + n_o + n_a:]
        i = pl.program_id(0)
        blk = (n - 1 - i) if reverse else i
        if c_refs:
            @pl.when(i == 0)
            def _():
                c_refs[0][...] = jnp.zeros_like(c_refs[0])
        args = (list(in_refs) if as_refs else [r[...] for r in in_refs], [r[...] for r in b_refs])
        o_vals, a_vals = fn(blk * tm, *args, *c_refs)
        for r, v in zip(o_refs, o_vals):
            r[...] = v.astype(r.dtype)
        if n_a:
            @pl.when(i == 0)
            def _():
                for r in a_refs:
                    r[...] = jnp.zeros_like(r)
            for r, v in zip(a_refs, a_vals):
                r[...] += v

    def row_map(i):
        return ((n - 1 - i) if reverse else i, 0)

    in_specs = [pl.BlockSpec((tm, x.shape[1]), row_map) for x in ins]
    in_specs += [pl.BlockSpec(x.shape, lambda i, nd=x.ndim: (0,) * nd) for x in bcast]
    out_specs = [pl.BlockSpec((tm, c), row_map) for c, _ in outs]
    out_specs += [pl.BlockSpec(s, lambda i: (0, 0)) for s in accs]
    out_shape = [jax.ShapeDtypeStruct((rows, c), d) for c, d in outs]
    out_shape += [jax.ShapeDtypeStruct(s, F32) for s in accs]
    res = pl.pallas_call(
        body, name=name, grid=(n,), in_specs=in_specs, out_specs=out_specs, out_shape=out_shape,
        scratch_shapes=[pltpu.VMEM(carry, F32)] if carry else [],
        compiler_params=_cparams(("arbitrary",)),
    )(*ins, *bcast)
    return res[:n_o], res[n_o:]


def _row_ids(row0, tm):
    return row0 + _iota((tm, 1), 0)


def _rms_fwd(x, gain, name):
    def fn(row0, ins, bc):
        (xv,), (g,) = ins, bc
        r = lax.rsqrt(jnp.mean(xv * xv, axis=-1, keepdims=True) + EPS)
        return [xv * r * g], []
    return _rowwise(fn, [x], [gain], [(D, BF16)], [], name=name)[0][0]


def _rms_bwd(x, dxn, gain, dh_up, name):
    def fn(row0, ins, bc):
        xv, dy, up = ins
        (g,) = bc
        dy = dy.astype(F32)
        r = lax.rsqrt(jnp.mean(xv * xv, axis=-1, keepdims=True) + EPS)
        xh = xv * r
        dxh = dy * g
        dx = r * (dxh - xh * jnp.mean(dxh * xh, axis=-1, keepdims=True))
        keep = _row_ids(row0, xv.shape[0]) >= ROW0
        return [jnp.where(keep, up + dx, 0.0)], [jnp.sum(dy * xh, axis=0, keepdims=True)]
    (dh,), (dgain,) = _rowwise(fn, [x, dxn, dh_up], [gain], [(D, F32)], [(1, D)], name=name)
    return dh, dgain


def _loss_bwd(h, tgt, gain):
    def fn(row0, ins, bc):
        xv, t = ins
        (g,) = bc
        r = lax.rsqrt(jnp.mean(xv * xv, axis=-1, keepdims=True) + EPS)
        xh = xv * r
        keep = _row_ids(row0, xv.shape[0]) >= PAD
        err = jnp.where(keep, xh * g - t, 0.0)
        per_row = jnp.mean(err * err, axis=-1, keepdims=True)
        loss = 0.5 * jnp.sum(per_row, axis=0, keepdims=True)
        dy = err * (1.0 / D)
        dxh = dy * g
        dx = r * (dxh - xh * jnp.mean(dxh * xh, axis=-1, keepdims=True))
        return [dx], [jnp.broadcast_to(loss, (1, LANES)), jnp.sum(dy * xh, axis=0, keepdims=True)]
    (dh,), (loss, dgain) = _rowwise(fn, [h, tgt], [gain], [(D, F32)], [(1, LANES), (1, D)], name="loss_bwd")
    return loss[0, 0], dh, dgain


FFN_TILES = dict(rows=(320, 256, 128), cols=(1408, 1024, 512, 256, 128))


def _ffn_in(hn, wg, wu, name):
    m, kdim = hn.shape
    n = wg.shape[1]
    tm, tn = _tile(m, FFN_TILES["rows"]), _tile(n, FFN_TILES["cols"])

    def body(a_ref, wg_ref, wu_ref, g_ref, u_ref, act_ref):
        a = a_ref[...]
        g = jnp.dot(a, wg_ref[...], preferred_element_type=F32)
        u = jnp.dot(a, wu_ref[...], preferred_element_type=F32)
        g_ref[...] = g.astype(g_ref.dtype)
        u_ref[...] = u.astype(u_ref.dtype)
        act_ref[...] = (g * _sigmoid(g) * u).astype(act_ref.dtype)

    wspec = pl.BlockSpec((kdim, tn), lambda j, i: (0, j))
    ospec = pl.BlockSpec((tm, tn), lambda j, i: (i, j))
    return pl.pallas_call(
        body, name=name, grid=(n // tn, m // tm),
        in_specs=[pl.BlockSpec((tm, kdim), lambda j, i: (i, 0)), wspec, wspec], out_specs=[ospec] * 3,
        out_shape=[jax.ShapeDtypeStruct((m, n), BF16)] * 3,
        compiler_params=_cparams(("parallel", "parallel")),
    )(hn, wg, wu)


def _ffn_dact(dh, wo, g, u, name):
    m, kdim = dh.shape
    n = wo.shape[0]
    tm, tn = _tile(m, FFN_TILES["rows"]), _tile(n, FFN_TILES["cols"])

    def body(a_ref, w_ref, g_ref, u_ref, dg_ref, du_ref):
        da = lax.dot_general(a_ref[...].astype(BF16), w_ref[...], NT, preferred_element_type=F32)
        gv, uv = g_ref[...].astype(F32), u_ref[...].astype(F32)
        s = _sigmoid(gv)
        dg_ref[...] = (da * uv * (s * (1.0 + gv * (1.0 - s)))).astype(dg_ref.dtype)
        du_ref[...] = (da * gv * s).astype(du_ref.dtype)

    ospec = pl.BlockSpec((tm, tn), lambda j, i: (i, j))
    return pl.pallas_call(
        body, name=name, grid=(n // tn, m // tm),
        in_specs=[pl.BlockSpec((tm, kdim), lambda j, i: (i, 0)), pl.BlockSpec((tn, kdim), lambda j, i: (j, 0)),
                  ospec, ospec],
        out_specs=[ospec] * 2, out_shape=[jax.ShapeDtypeStruct((m, n), BF16)] * 2,
        compiler_params=_cparams(("parallel", "parallel")),
    )(dh, wo, g, u)


def _adamw(w, g, m, v, name):
    def fn(row0, ins, bc):
        wv, gv, mv, vv = ins
        mn = ADAM_B1 * mv + (1.0 - ADAM_B1) * gv
        vn = ADAM_B2 * vv + (1.0 - ADAM_B2) * (gv * gv)
        m_hat = mn / (1.0 - ADAM_B1 ** ADAM_STEP)
        v_hat = vn / (1.0 - ADAM_B2 ** ADAM_STEP)
        delta = -ADAM_LR * (m_hat / (jnp.sqrt(v_hat) + ADAM_EPS) + ADAM_WD * wv)
        return [delta, mn, vn], []
    c = w.shape[1]
    return _rowwise(fn, [w, g, m, v], [], [(c, F32)] * 3, [], name=name)[0]


def _add(xs, name):
    def fn(row0, ins, bc):
        r = ins[0].astype(F32)
        for v in ins[1:]:
            r = r + v.astype(F32)
        return [r], []
    return _rowwise(fn, list(xs), [], [(xs[0].shape[1], F32)], [], name=name)[0][0]


def _head_sum(x, gmat):
    hi = x.astype(BF16)
    lo = (x - hi.astype(F32)).astype(BF16)
    w = gmat.shape[0]
    return jnp.concatenate(
        [jnp.dot(hi[:, b:b + w], gmat, preferred_element_type=F32) + jnp.dot(lo[:, b:b + w], gmat,
                                                                             preferred_element_type=F32)
         for b in range(0, x.shape[1], w)], axis=1)


def _split3(x):
    hi = x.astype(BF16).astype(F32)
    r = x - hi
    mid = r.astype(BF16).astype(F32)
    return hi, mid, r - mid


def _extra_base(hh):
    return FOX_DH * (1 - hh)


def _data_mask(hh):
    lane = _iota((1, LANES), 1)
    return (lane >= FOX_DH * hh) & (lane < FOX_DH * (hh + 1))


def _with_extras(data, hh, vals):
    lane = _iota((1, LANES), 1)
    x = jnp.zeros_like(data)
    for e, v in enumerate(vals):
        x = jnp.where(lane == _extra_base(hh) + e, v, x)
    return jnp.where(_data_mask(hh), data, x)


def _fox_pack_fwd(q_raw, k_raw, v, cq, ck, qg, kg, gmat):
    scale2 = FOX_DH ** -0.5 * LOG2E

    def fn(row0, refs, bc):
        q_ref, k_ref, v_ref, cq_ref, ck_ref = refs
        g_q, g_k, gm = bc
        qv, kv = q_ref[...], k_ref[...]
        qn = qv * lax.rsqrt(_head_sum(qv * qv, gm) * (1.0 / FOX_DH) + EPS) * (g_q * scale2)
        kn = kv * lax.rsqrt(_head_sum(kv * kv, gm) * (1.0 / FOX_DH) + EPS) * g_k
        qs, ks, vs = [], [], []
        for h in range(FOX_H):
            p, hh = divmod(h, 2)
            sl = slice(p * LANES, (p + 1) * LANES)
            cq3 = _split3(cq_ref[:, h:h + 1] * LOG2E)
            ck3 = _split3(ck_ref[:, h:h + 1] * (-LOG2E))
            qs.append(_with_extras(qn[:, sl], hh, [*cq3, 1.0, 1.0, 1.0]))
            ks.append(_with_extras(kn[:, sl], hh, [1.0, 1.0, 1.0, *ck3]))
            vs.append(_with_extras(v_ref[:, sl].astype(F32), hh, [1.0, 1.0]))
        return [jnp.concatenate(qs, axis=1), jnp.concatenate(ks, axis=1), jnp.concatenate(vs, axis=1)], []

    w = FOX_H * LANES
    return _rowwise(fn, [q_raw, k_raw, v, cq, ck], [qg, kg, gmat], [(w, BF16)] * 3, [], name="fox_pack_fwd",
                    as_refs=True)[0]


def _fox_pack_bias(qp, cq, lse2):
    def fn(row0, refs, bc):
        q_ref, cq_ref, lse_ref = refs
        lane = _iota((1, LANES), 1)
        outs = []
        for h in range(FOX_H):
            blk = q_ref[:, h * LANES:(h + 1) * LANES].astype(F32)
            for e, part in enumerate(_split3(cq_ref[:, h:h + 1] * LOG2E - lse_ref[:, h:h + 1])):
                blk = jnp.where(lane == _extra_base(h % 2) + e, part, blk)
            outs.append(blk)
        return [jnp.concatenate(outs, axis=1)], []
    return _rowwise(fn, [qp, cq, lse2], [], [(FOX_H * LANES, BF16)], [], name="fox_pack_bias", as_refs=True)[0][0]


def _fox_pack_bwd(dog, o, gate):
    def fn(row0, refs, bc):
        d_ref, o_ref, g_ref = refs
        dos, dgs = [], []
        for p in range(FOX_H // 2):
            sl = slice(p * LANES, (p + 1) * LANES)
            dv, ov, gv = (r[:, sl].astype(F32) for r in (d_ref, o_ref, g_ref))
            s = _sigmoid(gv)
            do = dv * s
            dgs.append(dv * ov * s * (1.0 - s))
            od = ov * do
            for hh in range(2):
                delta = jnp.sum(jnp.where(_data_mask(hh), od, 0.0), axis=-1, keepdims=True)
                hi = delta.astype(BF16).astype(F32)
                dos.append(_with_extras(do, hh, [-hi, hi - delta]))
        return [jnp.concatenate(dos, axis=1), jnp.concatenate(dgs, axis=1)], []
    return _rowwise(fn, [dog, o, gate], [], [(FOX_H * LANES, BF16), (D, BF16)], [], name="fox_pack_bwd",
                    as_refs=True)[0]


def _fox_unpack_bwd(q_raw, k_raw, dqp, dk, qg, kg, gmat):
    scale = FOX_DH ** -0.5

    def fn(row0, refs, bc):
        q_ref, k_ref, dq_ref, dk_ref = refs
        g_q, g_k, gm = bc
        lane = _iota((1, LANES), 1)
        dqs = []
        dcq = jnp.zeros((q_ref.shape[0], LANES), F32)
        for p in range(FOX_H // 2):
            even = dq_ref[:, (2 * p) * LANES:(2 * p + 1) * LANES]
            odd = dq_ref[:, (2 * p + 1) * LANES:(2 * p + 2) * LANES]
            dqs.append(jnp.where(_data_mask(0), even, odd) * scale)
            for hh in range(2):
                col = (2 * p + hh) * LANES + _extra_base(hh)
                dcq = jnp.where(lane == 2 * p + hh, dq_ref[:, col:col + 1], dcq)
        outs, accs = [], []
        for xv, dy, g in ((q_ref[...], jnp.concatenate(dqs, axis=1), g_q), (k_ref[...], dk_ref[...] * (1.0 / LOG2E), g_k)):
            r = lax.rsqrt(_head_sum(xv * xv, gm) * (1.0 / FOX_DH) + EPS)
            xh = xv * r
            dxh = dy * g
            outs.append(r * (dxh - xh * (_head_sum(dxh * xh, gm) * (1.0 / FOX_DH))))
            accs.append(jnp.sum(dy * xh, axis=0, keepdims=True))
        return outs + [dcq], accs
    return _rowwise(fn, [q_raw, k_raw, dqp, dk], [qg, kg, gmat], [(D, BF16), (D, BF16), (LANES, F32)],
                    [(1, D), (1, D)], name="fox_unpack_bwd", as_refs=True)


def _fox_cumsum_fwd(flog, bf):
    def fn(row0, ins, bc, carry):
        (f,), (b,) = ins, bc
        tm = f.shape[0]
        keep = _row_ids(row0, tm) >= ROW0
        lf = jnp.where(keep, _log_sigmoid(f + b), 0.0)
        tri = (_iota((tm, tm), 0) >= _iota((tm, tm), 1)).astype(F32)
        c = jnp.dot(tri, lf, precision=HIGHEST, preferred_element_type=F32) + carry[...]
        carry[...] = carry[...] + jnp.sum(lf, axis=0, keepdims=True)
        return [c, jnp.where(keep, c, BIG)], []
    return _rowwise(fn, [flog], [bf], [(LANES, F32), (LANES, F32)], [], name="fox_cumsum_fwd",
                    carry=(1, LANES))[0]


def _fox_cumsum_bwd(dc_q, dc_k, flog, bf):
    def fn(row0, ins, bc, carry):
        (dq, dk, f), (b,) = ins, bc
        d = dq + dk
        tm = f.shape[0]
        keep = _row_ids(row0, tm) >= ROW0
        triu = (_iota((tm, tm), 0) <= _iota((tm, tm), 1)).astype(F32)
        dlf = jnp.dot(triu, d, precision=HIGHEST, preferred_element_type=F32) + carry[...]
        carry[...] = carry[...] + jnp.sum(d, axis=0, keepdims=True)
        dfl = jnp.where(keep, dlf * _sigmoid(-(f + b)), 0.0)
        return [dfl], [jnp.sum(dfl, axis=0, keepdims=True)]
    (dflog,), (dbf,) = _rowwise(fn, [dc_q, dc_k, flog], [bf], [(LANES, F32)], [(1, LANES)], name="fox_cumsum_bwd",
                                reverse=True, carry=(1, LANES))
    return dflog, dbf


def _causal_steps(n, key_major):
    if key_major:
        pairs = [(i, j) for j in range(n) for i in range(j, n)]
    else:
        pairs = [(i, j) for i in range(n) for j in range(i + 1)]
    return (jnp.asarray(np.array([p[0] for p in pairs], np.int32)),
            jnp.asarray(np.array([p[1] for p in pairs], np.int32)))


def _fox_attn_fwd(qp, kp, vp, gate, shards):
    L = qp.shape[0]
    t = _tile(L, (640, 512, 256, 128))
    n = L // t
    P = FOX_H // 2
    it, jt = _causal_steps(n, False)
    n_steps = it.shape[0]
    ns = len(shards)

    def body(it_ref, jt_ref, q_ref, k_ref, v_ref, g_ref, *rest):
        sh_in, (o_ref, og_ref, lse_ref), sh_out = rest[:ns], rest[ns:ns + 3], rest[ns + 3:2 * ns + 3]
        m_sc, acc, ssem, rsem, lsem = rest[2 * ns + 3:]
        step = pl.program_id(1)
        i, j = it_ref[step], jt_ref[step]
        first = (pl.program_id(0) == 0) & (step == 0)
        last = (pl.program_id(0) == P - 1) & (step == n_steps - 1)

        @pl.when(first)
        def _():
            _gather_start(sh_in, sh_out, ssem, rsem, lsem)

        @pl.when(j == 0)
        def _():
            m_sc[...] = jnp.full_like(m_sc, -3.0e38)
            acc[...] = jnp.zeros_like(acc)

        def update(masked):
            for hh in range(2):
                sl = slice(hh * LANES, (hh + 1) * LANES)
                s2 = lax.dot_general(k_ref[:, sl], q_ref[:, sl], NT, preferred_element_type=F32)
                if masked:
                    s2 = jnp.where(_iota((t, t), 0) <= _iota((t, t), 1), s2, -jnp.inf)
                m_old = m_sc[hh]
                m_new = jnp.maximum(m_old, jnp.max(s2, axis=0, keepdims=True))
                p = jnp.exp2(s2 - m_new)
                acc[hh] = jnp.exp2(m_old - m_new) * acc[hh] + lax.dot_general(v_ref[:, sl], p.astype(BF16), TN,
                                                                              preferred_element_type=F32)
                m_sc[hh] = m_new

        @pl.when(j < i)
        def _():
            update(False)

        @pl.when(j == i)
        def _():
            update(True)
            outs = []
            for hh in range(2):
                l = acc[hh, _extra_base(hh):_extra_base(hh) + 1, :]
                outs.append((acc[hh] / l).T)
                lse_ref[0, hh:hh + 1, :] = m_sc[hh] + jnp.log2(l)
            o = jnp.where(_data_mask(0), outs[0], outs[1])
            o_ref[...] = o.astype(o_ref.dtype)
            og_ref[...] = (o * _sigmoid(g_ref[...].astype(F32))).astype(og_ref.dtype)

        @pl.when(last)
        def _():
            _gather_wait(sh_in, sh_out, ssem, rsem, lsem)

    qspec = pl.BlockSpec((t, 2 * LANES), lambda p, s, it, jt: (it[s], p))
    kspec = pl.BlockSpec((t, 2 * LANES), lambda p, s, it, jt: (jt[s], p))
    ospec = pl.BlockSpec((t, LANES), lambda p, s, it, jt: (it[s], p))
    lspec = pl.BlockSpec((1, 2, t), lambda p, s, it, jt: (p, 0, it[s]))
    res = pl.pallas_call(
        body, name="fox_attn_fwd",
        grid_spec=pltpu.PrefetchScalarGridSpec(
            num_scalar_prefetch=2, grid=(P, n_steps),
            in_specs=[qspec, kspec, kspec, ospec] + [ANY] * ns, out_specs=[ospec, ospec, lspec] + [ANY] * ns,
            scratch_shapes=[pltpu.VMEM((2, 1, t), F32), pltpu.VMEM((2, LANES, t), F32)] + _gather_sems(ns)),
        out_shape=[jax.ShapeDtypeStruct((L, D), BF16), jax.ShapeDtypeStruct((L, D), BF16),
                   jax.ShapeDtypeStruct((P, 2, L), F32)]
        + [jax.ShapeDtypeStruct((4,) + a.shape, a.dtype) for a in shards],
        compiler_params=_cparams(("arbitrary", "arbitrary")),
    )(it, jt, qp, kp, vp, gate, *shards)
    return res[0], res[1], res[2], res[3:]


def _fox_attn_bwd(qb, kp, vp, dop, slabs):
    L = qb.shape[0]
    t = _tile(L, (640, 512, 256, 128))
    n = L // t
    P = FOX_H // 2
    it, jt = _causal_steps(n, True)
    n_steps = it.shape[0]
    ns = len(slabs)

    def body(it_ref, jt_ref, q_ref, k_ref, v_ref, do_ref, *rest):
        sl_in, (dq_ref, dk_ref, dv_ref, dck_ref), sl_out = rest[:ns], rest[ns:ns + 4], rest[ns + 4:2 * ns + 4]
        dk_acc, dv_acc, ssem, rsem = rest[2 * ns + 4:]
        step = pl.program_id(1)
        i, j = it_ref[step], jt_ref[step]

        @pl.when((pl.program_id(0) == 0) & (step == 0))
        def _():
            for cp in _scatter_copies(sl_in, sl_out, ssem, rsem):
                cp.start()

        @pl.when(step == 0)
        def _():
            dq_ref[...] = jnp.zeros_like(dq_ref)

        @pl.when(i == j)
        def _():
            dk_acc[...] = jnp.zeros_like(dk_acc)
            dv_acc[...] = jnp.zeros_like(dv_acc)

        def update(masked):
            rows = pl.ds(pl.multiple_of(i * t, LANES), t)
            for hh in range(2):
                sl = slice(hh * LANES, (hh + 1) * LANES)
                q, k, dov = q_ref[:, sl], k_ref[:, sl], do_ref[:, sl]
                s2 = lax.dot_general(k, q, NT, preferred_element_type=F32)
                if masked:
                    s2 = jnp.where(_iota((t, t), 0) <= _iota((t, t), 1), s2, -jnp.inf)
                p = jnp.exp2(s2)
                ds = (p * lax.dot_general(v_ref[:, sl], dov, NT, preferred_element_type=F32)).astype(BF16)
                dv_acc[hh] += jnp.dot(p.astype(BF16), dov, preferred_element_type=F32)
                dk_acc[hh] += jnp.dot(ds, q, preferred_element_type=F32)
                dq_ref[rows, sl] += lax.dot_general(ds, k, TN, preferred_element_type=F32)

        @pl.when(i > j)
        def _():
            update(False)

        @pl.when(i == j)
        def _():
            update(True)

        @pl.when(i == n - 1)
        def _():
            dk_ref[...] = jnp.where(_data_mask(0), dk_acc[0], dk_acc[1])
            dv_ref[...] = jnp.where(_data_mask(0), dv_acc[0], dv_acc[1]).astype(dv_ref.dtype)
            col_sums = [dk_acc[hh, :, _extra_base(hh) + 3:_extra_base(hh) + 4] for hh in range(2)]
            dck_ref[0] = -jnp.where(_iota((1, 2), 1) == 0, col_sums[0], col_sums[1])

        @pl.when((pl.program_id(0) == P - 1) & (step == n_steps - 1))
        def _():
            for cp in _scatter_copies(sl_in, sl_out, ssem, rsem):
                cp.wait()

    qspec = pl.BlockSpec((t, 2 * LANES), lambda p, s, it, jt: (it[s], p))
    kspec = pl.BlockSpec((t, 2 * LANES), lambda p, s, it, jt: (jt[s], p))
    ospec = pl.BlockSpec((t, LANES), lambda p, s, it, jt: (jt[s], p))
    res = pl.pallas_call(
        body, name="fox_attn_bwd",
        grid_spec=pltpu.PrefetchScalarGridSpec(
            num_scalar_prefetch=2, grid=(P, n_steps),
            in_specs=[qspec, kspec, kspec, qspec] + [ANY] * ns,
            out_specs=[pl.BlockSpec((L, 2 * LANES), lambda p, s, it, jt: (0, p)), ospec, ospec,
                       pl.BlockSpec((1, t, 2), lambda p, s, it, jt: (p, jt[s], 0))] + [ANY] * ns,
            scratch_shapes=[pltpu.VMEM((2, t, LANES), F32), pltpu.VMEM((2, t, LANES), F32),
                            pltpu.SemaphoreType.DMA((ns, 3)), pltpu.SemaphoreType.DMA((ns, 3))]),
        out_shape=[jax.ShapeDtypeStruct((L, FOX_H * LANES), F32), jax.ShapeDtypeStruct((L, D), F32),
                   jax.ShapeDtypeStruct((L, D), BF16), jax.ShapeDtypeStruct((P, L, 2), F32)]
        + [jax.ShapeDtypeStruct((3,) + a.shape[1:], a.dtype) for a in slabs],
        compiler_params=_cparams(("arbitrary", "arbitrary")),
    )(it, jt, qb, kp, vp, dop, *slabs)
    return res[0], res[1], res[2], res[3], res[4:]


def _hgrn_consts():
    C = HG_C
    r = np.arange(C)[:, None]
    j = np.arange(C)[None, :]
    mats = [j <= r, j > r]
    masks = []
    n = C
    while n >= 2:
        half = n // 2
        mid = (r // n) * n + half - 1
        second = (r % n) >= half
        mats.append(np.where(second, (j > mid) & (j <= r), (j > r) & (j <= mid)))
        masks.append(((r // n) == (j // n)) & ((r % n) >= half) & ((j % n) < half))
        n //= 2
    return (jnp.asarray(np.concatenate(mats, 0).astype(np.float32), BF16),
            jnp.asarray(np.stack(masks).astype(np.float32), F32))


def _hg_pre(hq, hz, h0, h1):
    mx = jnp.maximum(h0, h1)
    e0, e1 = jnp.exp(h0 - mx), jnp.exp(h1 - mx)
    lb = e1 / (e0 + e1)
    sq = _sigmoid(hq)
    sz = _sigmoid(hz)
    snz = 1.0 - sz
    k = (1.0 - lb) * snz
    g = jnp.maximum(jnp.log(lb + (1.0 - lb) * sz), -BIG)
    return lb, hq * sq, sq, k, sz, snz, g


def _hg_decays(g, rmat):
    hi = g.astype(BF16)
    lo = (g - hi.astype(F32)).astype(BF16)
    d = jnp.dot(rmat, jnp.concatenate([hi, lo], axis=1), preferred_element_type=F32)
    return jnp.exp(d[:, :HG_D] + d[:, HG_D:])


def _hg_intra(q, k, fall, masks):
    C = HG_C
    eye = _iota((C, C), 0) == _iota((C, C), 1)
    a = jnp.where(eye, jnp.sum(q * k, axis=-1, keepdims=True), 0.0)
    for l in range(HG_LEV):
        f = fall[(2 + l) * C:(3 + l) * C]
        a = a + masks[l] * lax.dot_general((q * f).astype(BF16), (k * f).astype(BF16), NT,
                                           preferred_element_type=F32)
    return a


def _hgrn_specs(n_chunks, reverse):
    C = HG_C
    w = HG_HPS * HG_D

    def col(first_head):
        off = first_head // HG_HPS
        if reverse:
            return pl.BlockSpec((C, w), lambda h, c: (n_chunks - 1 - c, off + h))
        return pl.BlockSpec((C, w), lambda h, c: (c, off + h))

    st = pl.BlockSpec((HG_HPS, 1, HG_D, HG_D),
                      (lambda h, c: (h, n_chunks - 1 - c, 0, 0)) if reverse else (lambda h, c: (h, c, 0, 0)))
    consts = [pl.BlockSpec((2, w), lambda h, c: (0, h)), pl.BlockSpec((1, HG_D), lambda h, c: (0, 0)),
              pl.BlockSpec(((2 + HG_LEV) * C, C), lambda h, c: (0, 0)),
              pl.BlockSpec((HG_LEV, C, C), lambda h, c: (0, 0, 0))]
    return col, st, consts


def _hgrn_fwd(proj, hlb, gg, rmat, masks):
    L = proj.shape[0]
    C = HG_C
    nc = L // C
    col, st, consts = _hgrn_specs(nc, False)

    def body(hq_ref, hz_ref, hi_ref, hg_ref, hlb_ref, gg_ref, r_ref, m_ref, og_ref, st_ref, state):
        c = pl.program_id(1)

        @pl.when(c == 0)
        def _():
            state[...] = jnp.zeros_like(state)

        for hh in range(HG_HPS):
            sl = slice(hh * HG_D, (hh + 1) * HG_D)
            v, hg = hi_ref[:, sl], hg_ref[:, sl]
            _, q, _, k, _, _, g = _hg_pre(hq_ref[:, sl], hz_ref[:, sl], hlb_ref[0:1, sl], hlb_ref[1:2, sl])
            fall = _hg_decays(g, r_ref[...])
            fb, fe = fall[0:C], fall[C:2 * C]
            st0 = state[hh]
            st_ref[hh, 0] = st0
            a = _hg_intra(q, k, fall, m_ref[...])
            vb = v.astype(BF16)
            o = jnp.dot(a.astype(BF16), vb, preferred_element_type=F32)
            o = o + lax.dot_general((q * fb).astype(BF16), st0.astype(BF16), NT, preferred_element_type=F32)
            ebc = jnp.exp(jnp.sum(g, axis=0, keepdims=True))
            state[hh] = st0 * ebc + lax.dot_general(vb, (k * fe).astype(BF16), TN, preferred_element_type=F32)
            r = lax.rsqrt(jnp.mean(o * o, axis=-1, keepdims=True) + EPS)
            og_ref[:, sl] = (o * r * gg_ref[...] * (hg * _sigmoid(hg))).astype(og_ref.dtype)

    return pl.pallas_call(
        body, name="hgrn_fwd", grid=(HG_H // HG_HPS, nc),
        in_specs=[col(0), col(HG_H), col(2 * HG_H), col(3 * HG_H)] + consts,
        out_specs=[col(0), st],
        out_shape=[jax.ShapeDtypeStruct((L, D), BF16), jax.ShapeDtypeStruct((HG_H, nc, HG_D, HG_D), F32)],
        scratch_shapes=[pltpu.VMEM((HG_HPS, HG_D, HG_D), F32)],
        compiler_params=_cparams(("parallel", "arbitrary")),
    )(proj, proj, proj, proj, hlb, gg, rmat, masks)


def _hgrn_bwd(proj, dog, states, hlb, gg, rmat, masks):
    L = proj.shape[0]
    C = HG_C
    nc = L // C
    col, st, consts = _hgrn_specs(nc, True)

    def body(hq_ref, hz_ref, hi_ref, hg_ref, do_ref, hlb_ref, gg_ref, r_ref, m_ref, st_ref,
             dq_ref, dz_ref, di_ref, dg_ref, dlb_ref, dgg_ref, dstate):
        c = pl.program_id(1)

        @pl.when(c == 0)
        def _():
            dstate[...] = jnp.zeros_like(dstate)
            dlb_ref[...] = jnp.zeros_like(dlb_ref)
            dgg_ref[...] = jnp.zeros_like(dgg_ref)

        for hh in range(HG_HPS):
            bwd_head(c, hh, slice(hh * HG_D, (hh + 1) * HG_D), hq_ref, hz_ref, hi_ref, hg_ref, do_ref, hlb_ref, gg_ref,
                     r_ref, m_ref, st_ref, dq_ref, dz_ref, di_ref, dg_ref, dlb_ref, dgg_ref, dstate)

    def bwd_head(c, hh, sl, hq_ref, hz_ref, hi_ref, hg_ref, do_ref, hlb_ref, gg_ref, r_ref, m_ref, st_ref,
                 dq_ref, dz_ref, di_ref, dg_ref, dlb_ref, dgg_ref, dstate):
        hq, hz, v, hg = hq_ref[:, sl], hz_ref[:, sl], hi_ref[:, sl], hg_ref[:, sl]
        dout = do_ref[:, sl].astype(F32)
        gain = gg_ref[...]
        masks_v = m_ref[...]
        lb, q, sq, k, sz, snz, g = _hg_pre(hq, hz, hlb_ref[0:1, sl], hlb_ref[1:2, sl])
        fall = _hg_decays(g, r_ref[...])
        fb, fe = fall[0:C], fall[C:2 * C]
        a = _hg_intra(q, k, fall, masks_v)
        st0 = st_ref[hh, 0]
        st0b = st0.astype(BF16)
        ebc = jnp.exp(jnp.sum(g, axis=0, keepdims=True))
        qb, ke, vb = (q * fb).astype(BF16), (k * fe).astype(BF16), v.astype(BF16)
        ab = a.astype(BF16)
        o = jnp.dot(ab, vb, preferred_element_type=F32) + lax.dot_general(qb, st0b, NT, preferred_element_type=F32)
        r = lax.rsqrt(jnp.mean(o * o, axis=-1, keepdims=True) + EPS)
        oh = o * r
        sg = _sigmoid(hg)
        d_on = dout * (hg * sg)
        dhg = dout * (oh * gain) * (sg * (1.0 + hg * (1.0 - sg)))
        dgg_ref[hh] += jnp.sum(d_on * oh, axis=0, keepdims=True)
        dxh = d_on * gain
        do = r * (dxh - oh * jnp.mean(dxh * oh, axis=-1, keepdims=True))
        dob = do.astype(BF16)
        dsp = dstate[hh]
        dspb = dsp.astype(BF16)
        causal = _iota((C, C), 0) >= _iota((C, C), 1)
        da = jnp.where(causal, lax.dot_general(dob, vb, NT, preferred_element_type=F32), 0.0)
        diag = jnp.sum(do * v, axis=-1, keepdims=True)
        dv = lax.dot_general(ab, dob, TN, preferred_element_type=F32)
        dv = dv + lax.dot_general(ke, dspb, NT, preferred_element_type=F32)
        xq = jnp.dot(dob, st0b, preferred_element_type=F32)
        xk = jnp.dot(vb, dspb, preferred_element_type=F32)
        dq = diag * k + fb * xq
        dk = diag * q + fe * xk
        ke_xk = ke.astype(F32) * xk
        db = qb.astype(F32) * xq - ke_xk
        for l in range(HG_LEV):
            f = fall[(2 + l) * C:(3 + l) * C]
            dal = (masks_v[l] * da).astype(BF16)
            ql, kl = (q * f).astype(BF16), (k * f).astype(BF16)
            xq = jnp.dot(dal, kl, preferred_element_type=F32)
            xk = lax.dot_general(dal, ql, TN, preferred_element_type=F32)
            dq = dq + f * xq
            dk = dk + f * xk
            db = db + ql.astype(F32) * xq - kl.astype(F32) * xk
        dstate[hh] = dsp * ebc + lax.dot_general(dob, qb, TN, preferred_element_type=F32)
        triu = (_iota((C, C), 0) <= _iota((C, C), 1)).astype(F32)
        dg = jnp.dot(triu, db, precision=HIGHEST, preferred_element_type=F32)
        dg = dg + jnp.sum(st0 * ebc * dsp, axis=0, keepdims=True) + jnp.sum(ke_xk, axis=0, keepdims=True)
        keep = _row_ids((nc - 1 - c) * C, C) >= ROW0
        dg = jnp.where(keep, dg, 0.0)
        dk = jnp.where(keep, dk, 0.0)
        f_gate = lb + (1.0 - lb) * sz
        dfdz = (1.0 - lb) * sz * snz
        dz_ref[:, sl] = (dg * dfdz / f_gate - dk * dfdz).astype(dz_ref.dtype)
        dlb_ref[:, sl] += jnp.sum(dg * snz / f_gate - dk * snz, axis=0, keepdims=True)
        dq_ref[:, sl] = jnp.where(keep, dq * (sq * (1.0 + hq * (1.0 - sq))), 0.0).astype(dq_ref.dtype)
        di_ref[:, sl] = jnp.where(keep, dv, 0.0).astype(di_ref.dtype)
        dg_ref[:, sl] = jnp.where(keep, dhg, 0.0).astype(dg_ref.dtype)

    w = HG_HPS * HG_D
    outs = pl.pallas_call(
        body, name="hgrn_bwd", grid=(HG_H // HG_HPS, nc),
        in_specs=[col(0), col(HG_H), col(2 * HG_H), col(3 * HG_H), col(0)] + consts + [st],
        out_specs=[col(0), col(0), col(0), col(0), pl.BlockSpec((1, w), lambda h, c: (0, h)),
                   pl.BlockSpec((HG_HPS, 1, HG_D), lambda h, c: (h, 0, 0))],
        out_shape=[jax.ShapeDtypeStruct((L, D), BF16)] * 4 + [jax.ShapeDtypeStruct((1, D), F32),
                                                              jax.ShapeDtypeStruct((HG_H, 1, HG_D), F32)],
        scratch_shapes=[pltpu.VMEM((HG_HPS, HG_D, HG_D), F32)],
        compiler_params=_cparams(("parallel", "arbitrary")),
    )(proj, proj, proj, proj, dog, hlb, gg, rmat, masks, states)
    return outs


def _ffn_fwd(h, norm_gain, wg, wu, wo, tag):
    hn = _rms_fwd(h, norm_gain, f"{tag}_norm")
    g, u, act = _ffn_in(hn, wg, wu, f"{tag}_in")
    h_out = _matmul(act, wo, add=h, name=f"{tag}_out")
    return h_out, (h, hn, g, u, act)


def _ffn_bwd(dh, saved, norm_gain, wg, wu, wo, tag):
    h, hn, g, u, act = saved
    dg, du = _ffn_dact(dh, wo, g, u, f"{tag}_dact")
    d_wo = _matmul(act, dh, ta=True, name=f"{tag}_dwo")
    dhn = _matmul(dg, wg, tb=True, name=f"{tag}_dhn_g")
    dhn = _matmul(du, wu, tb=True, add=dhn, name=f"{tag}_dhn_u")
    d_wg = _matmul(hn, dg, ta=True, name=f"{tag}_dwg")
    d_wu = _matmul(hn, du, ta=True, name=f"{tag}_dwu")
    dh, d_gain = _rms_bwd(h, dhn, norm_gain, dh, f"{tag}_norm_bwd")
    return dh, d_gain, jnp.concatenate([d_wg, d_wu], axis=1), d_wo


def _local_step(h0, tgt, w, late_shards, layers):
    L = h0.shape[0]
    gmat = jnp.asarray(np.kron(np.eye(MXU_N // FOX_DH), np.ones((FOX_DH, FOX_DH))).astype(np.float32), BF16)
    rmat, lmasks = _hgrn_consts()
    an, fn_ = w["attn_norm"], w["ffn_norm"]
    qg = jnp.tile(w["fox_q_norm"], (1, FOX_H))
    kg = jnp.tile(w["fox_k_norm"], (1, FOX_H))
    bf = jnp.pad(w["fox_b_f"], ((0, 0), (0, LANES - FOX_H)))
    fw = w["fox_w_in"]
    f_wq, f_wk, f_wv, f_wg = (fw[:, i * D:(i + 1) * D] for i in range(4))
    f_wf = jnp.pad(fw[:, 4 * D:], ((0, 0), (0, LANES - FOX_H)))
    f_wo = w["fox_w_out"]

    hn0 = _rms_fwd(h0, an[0:1], "fox_norm")
    q_raw = _matmul(hn0, f_wq, name="fox_q")
    k_raw = _matmul(hn0, f_wk, name="fox_k")
    v = _matmul(hn0, f_wv, out_dtype=BF16, name="fox_v")
    gate = _matmul(hn0, f_wg, out_dtype=BF16, name="fox_gate")
    flog = _matmul(hn0, f_wf, name="fox_flog")
    cq, ck = _fox_cumsum_fwd(flog, bf)
    qp, kp, vp = _fox_pack_fwd(q_raw, k_raw, v, cq, ck, qg, kg, gmat)
    o, og, lse2, gathered = _fox_attn_fwd(qp, kp, vp, gate, [late_shards[n] for n in LATE_NAMES])
    late = {n: _unshard(n, g, layers[n]) for n, g in zip(LATE_NAMES, gathered)}
    h_wi, h_wo = late["hgrn_w_in"][0], late["hgrn_w_out"][0]
    ffw = [(late["ffn_w_in"][i][:, :FFN], late["ffn_w_in"][i][:, FFN:], late["ffn_w_out"][i]) for i in range(2)]
    h1 = _matmul(og, f_wo, add=h0, name="fox_out")
    h2, ffn0 = _ffn_fwd(h1, fn_[0:1], *ffw[0], "ffn0")

    hn2 = _rms_fwd(h2, an[1:2], "hgrn_norm")
    proj = _matmul(hn2, h_wi, name="hgrn_in")
    og1, states = _hgrn_fwd(proj, w["hgrn_lower_bounds"], w["hgrn_g_norm"], rmat, lmasks)
    h3 = _matmul(og1, h_wo, add=h2, name="hgrn_out")
    h4, ffn1 = _ffn_fwd(h3, fn_[1:2], *ffw[1], "ffn1")

    loss, dh, d_final = _loss_bwd(h4, tgt, w["final_norm"])

    dh, d_fn1, d_ffn_in1, d_ffn_out1 = _ffn_bwd(dh, ffn1, fn_[1:2], *ffw[1], "ffn1")
    dog1 = _matmul(dh, h_wo, tb=True, out_dtype=BF16, name="hgrn_dog")
    d_h_wo = _matmul(og1, dh, ta=True, name="hgrn_dwo")
    dpq, dpz, dpi, dpg, d_lb, d_gg = _hgrn_bwd(proj, dog1, states, w["hgrn_lower_bounds"], w["hgrn_g_norm"],
                                               rmat, lmasks)
    dproj = jnp.concatenate([dpq, dpz, dpi, dpg], axis=1)
    dhn2 = _matmul(dproj, h_wi, tb=True, name="hgrn_dhn")
    d_h_wi = _matmul(hn2, dproj, ta=True, name="hgrn_dwi")
    dh, d_an1 = _rms_bwd(h2, dhn2, an[1:2], dh, "hgrn_norm_bwd")

    dh, d_fn0, d_ffn_in0, d_ffn_out0 = _ffn_bwd(dh, ffn0, fn_[0:1], *ffw[0], "ffn0")
    late_grads = dict(hgrn_w_in=d_h_wi[None], hgrn_w_out=d_h_wo[None],
                      ffn_w_in=jnp.stack([d_ffn_in0, d_ffn_in1]), ffn_w_out=jnp.stack([d_ffn_out0, d_ffn_out1]))
    pair_late, send_late = _pair_sums([_to_shards(n, late_grads[n]) for n in LATE_NAMES], "late")

    dog = _matmul(dh, f_wo, tb=True, out_dtype=BF16, name="fox_dog")
    d_f_wo = _matmul(og, dh, ta=True, name="fox_dwo")

    def by_head(a):
        return jnp.pad(a.transpose(1, 0, 2).reshape(L, FOX_H), ((0, 0), (0, LANES - FOX_H)))

    qb = _fox_pack_bias(qp, cq, by_head(lse2.transpose(0, 2, 1)))
    dop, dgate = _fox_pack_bwd(dog, o, gate)
    dqp, dk, dv, dck, recv_late = _fox_attn_bwd(qb, kp, vp, dop, send_late)
    (dq_raw, dk_raw, dc_q), (d_qg, d_kg) = _fox_unpack_bwd(q_raw, k_raw, dqp, dk, qg, kg, gmat)
    dflog, d_bf = _fox_cumsum_bwd(dc_q, by_head(dck), flog, bf)
    dproj0 = jnp.concatenate([dq_raw, dk_raw, dv, dgate, dflog.astype(BF16)], axis=1)
    f_wall = jnp.concatenate([f_wq, f_wk, f_wv, f_wg, f_wf], axis=1)
    dhn0 = _matmul(dproj0, f_wall, tb=True, name="fox_dhn")
    d_f_wall = _matmul(hn0, dproj0, ta=True, name="fox_dwi")
    d_f_wi = d_f_wall[:, :4 * D + FOX_H]
    dh, d_an0 = _rms_bwd(h0, dhn0, an[0:1], dh, "fox_norm_bwd")

    fox_grads = dict(fox_w_in=d_f_wi[None], fox_w_out=d_f_wo[None])
    pair_fox, send_fox = _pair_sums([_to_shards(n, fox_grads[n]) for n in FOX_NAMES], "fox")
    recv_fox = _chip_scatter(send_fox, "fox")
    halves = _chip_sums(pair_fox, recv_fox, "fox") + _chip_sums(pair_late, recv_late, "late")
    theirs = _sibling_exchange(halves)
    south = lax.axis_index("c") == 0
    big = {n: jnp.where(south, jnp.concatenate([m, t]), jnp.concatenate([t, m]))
           for n, m, t in zip(FOX_NAMES + LATE_NAMES, halves, theirs)}
    small = dict(attn_norm=jnp.concatenate([d_an0, d_an1]), ffn_norm=jnp.concatenate([d_fn0, d_fn1]),
                 final_norm=d_final, lb_raw=d_lb, q_gain=d_qg, k_gain=d_kg, b_f=d_bf,
                 g_gain=d_gg.reshape(1, D))
    return loss, dh, big, small


def _me():
    return lax.axis_index("x"), lax.axis_index("y"), lax.axis_index("c")


def _flip(v, bit):
    return 1 - v if bit else v


def _chip_allgather(arrs):
    n = len(arrs)

    def body(*refs):
        _gather_start(refs[:n], refs[n:2 * n], *refs[2 * n:])
        _gather_wait(refs[:n], refs[n:2 * n], *refs[2 * n:])

    return pl.pallas_call(
        body, name="chip_allgather", in_specs=[ANY] * n, out_specs=[ANY] * n,
        out_shape=[jax.ShapeDtypeStruct((4,) + a.shape, a.dtype) for a in arrs],
        scratch_shapes=_gather_sems(n),
    )(*arrs)


def _chip_peers():
    x, y, c = _me()
    return [(1 - x, y, c), (x, 1 - y, c), (1 - x, 1 - y, c)]


def _gather_sems(n):
    return [pltpu.SemaphoreType.DMA((n, 3)), pltpu.SemaphoreType.DMA((n, 3)), pltpu.SemaphoreType.DMA((n,))]


def _gather_copies(ins, outs, ssem, rsem, lsem, with_recvs):
    x, y, _ = _me()
    local, sends, recvs = [], [], []
    for a in range(len(ins)):
        local.append(pltpu.make_async_copy(ins[a], outs[a].at[2 * x + y], lsem.at[a]))
        for k, peer in enumerate(_chip_peers()):
            sends.append(pltpu.make_async_remote_copy(ins[a], outs[a].at[2 * x + y], ssem.at[a, k], rsem.at[a, k],
                                                      device_id=peer, device_id_type=MESH))
            if with_recvs:
                recvs.append(pltpu.make_async_remote_copy(ins[a], outs[a].at[2 * peer[0] + peer[1]], ssem.at[a, k],
                                                          rsem.at[a, k], device_id=peer, device_id_type=MESH))
    return local, sends, recvs


def _gather_start(ins, outs, ssem, rsem, lsem):
    local, sends, _ = _gather_copies(ins, outs, ssem, rsem, lsem, False)
    for cp in local + sends:
        cp.start()


def _gather_wait(ins, outs, ssem, rsem, lsem):
    local, sends, recvs = _gather_copies(ins, outs, ssem, rsem, lsem, True)
    for cp in local:
        cp.wait()
    for cp in sends:
        cp.wait_send()
    for cp in recvs:
        cp.wait_recv()


def _scatter_copies(ins, outs, ssem, rsem):
    copies = []
    for a in range(len(ins)):
        for k, peer in enumerate(_chip_peers()):
            copies.append(pltpu.make_async_remote_copy(ins[a].at[2 * peer[0] + peer[1]], outs[a].at[k], ssem.at[a, k],
                                                       rsem.at[a, k], device_id=peer, device_id_type=MESH))
    return copies


def _device_allgather(arr):
    def body(in_ref, out_ref, ssem, rsem, lsem):
        x, y, c = _me()
        me = 4 * x + 2 * y + c
        peers = [(_flip(x, k & 4), _flip(y, k & 2), _flip(c, k & 1)) for k in range(1, 8)]
        local = pltpu.make_async_copy(in_ref, out_ref.at[me], lsem)
        local.start()
        sends = []
        for k, peer in enumerate(peers):
            cp = pltpu.make_async_remote_copy(in_ref, out_ref.at[me], ssem.at[k], rsem.at[k],
                                              device_id=peer, device_id_type=MESH)
            cp.start()
            sends.append(cp)
        local.wait()
        for cp in sends:
            cp.wait_send()
        for k, peer in enumerate(peers):
            pltpu.make_async_remote_copy(in_ref, out_ref.at[4 * peer[0] + 2 * peer[1] + peer[2]], ssem.at[k],
                                         rsem.at[k], device_id=peer, device_id_type=MESH).wait_recv()

    return pl.pallas_call(
        body, name="device_allgather", in_specs=[ANY], out_specs=ANY,
        out_shape=jax.ShapeDtypeStruct((8,) + arr.shape, arr.dtype),
        scratch_shapes=[pltpu.SemaphoreType.DMA((7,)), pltpu.SemaphoreType.DMA((7,)), pltpu.SemaphoreType.DMA],
    )(arr)


def _sibling_send_other_half(arrs, tag):
    n = len(arrs)

    def body(*refs):
        ins, outs = refs[:n], refs[n:2 * n]
        ssem, rsem = refs[2 * n:]
        x, y, c = _me()
        cps = []
        for a in range(n):
            half = ins[a].shape[1] // 2
            src = ins[a].at[:, pl.ds((1 - c) * half, half), :]
            cp = pltpu.make_async_remote_copy(src, outs[a], ssem.at[a], rsem.at[a],
                                              device_id=(x, y, 1 - c), device_id_type=MESH)
            cp.start()
            cps.append(cp)
        for cp in cps:
            cp.wait()

    return pl.pallas_call(
        body, name=f"grad_sibling_swap_{tag}", in_specs=[ANY] * n, out_specs=[ANY] * n,
        out_shape=[jax.ShapeDtypeStruct((4, a.shape[1] // 2, a.shape[2]), a.dtype) for a in arrs],
        scratch_shapes=[pltpu.SemaphoreType.DMA((n,)), pltpu.SemaphoreType.DMA((n,))],
    )(*arrs)


def _chip_scatter(arrs, tag):
    n = len(arrs)

    def body(*refs):
        cps = _scatter_copies(refs[:n], refs[n:2 * n], *refs[2 * n:])
        for cp in cps:
            cp.start()
        for cp in cps:
            cp.wait()

    return pl.pallas_call(
        body, name=f"grad_chip_scatter_{tag}", in_specs=[ANY] * n, out_specs=[ANY] * n,
        out_shape=[jax.ShapeDtypeStruct((3,) + a.shape[1:], a.dtype) for a in arrs],
        scratch_shapes=[pltpu.SemaphoreType.DMA((n, 3)), pltpu.SemaphoreType.DMA((n, 3))],
    )(*arrs)


def _sibling_exchange(arrs):
    n = len(arrs)

    def body(*refs):
        ins, outs = refs[:n], refs[n:2 * n]
        ssem, rsem = refs[2 * n:]
        x, y, c = _me()
        cps = [pltpu.make_async_remote_copy(ins[a], outs[a], ssem.at[a], rsem.at[a], device_id=(x, y, 1 - c),
                                            device_id_type=MESH) for a in range(n)]
        for cp in cps:
            cp.start()
        for cp in cps:
            cp.wait()

    return pl.pallas_call(
        body, name="grad_sibling_exchange", in_specs=[ANY] * n, out_specs=[ANY] * n,
        out_shape=[jax.ShapeDtypeStruct(a.shape, a.dtype) for a in arrs],
        scratch_shapes=[pltpu.SemaphoreType.DMA((n,)), pltpu.SemaphoreType.DMA((n,))],
    )(*arrs)


def _pair_sums(grads, tag):
    c = lax.axis_index("c")
    got = _sibling_send_other_half(grads, tag)
    pair = []
    for i, (g, t) in enumerate(zip(grads, got)):
        half = g.shape[1] // 2
        mine = lax.dynamic_slice_in_dim(g, c * half, half, axis=1)
        pair.append(_add([mine.reshape(4 * half, -1), t.reshape(4 * half, -1)], f"grad_pair_add_{tag}{i}")
                    .reshape(t.shape))
    return pair, [p.astype(BF16) for p in pair]


def _chip_sums(pair, recv, tag):
    x, y, _ = _me()
    out = []
    for i, (p, r) in enumerate(zip(pair, recv)):
        own = lax.dynamic_index_in_dim(p, 2 * x + y, axis=0, keepdims=False)
        out.append(_add([own, r[0], r[1], r[2]], f"grad_chip_add_{tag}{i}"))
    return out


SMALL_ROWS = 32


def _small_finalize(gathered, hlb, fold64, fold128):
    def body(g_ref, hlb_ref, f64_ref, f128_ref, rows_ref, qk_ref, gg_ref, lb_ref):
        tot = g_ref[0]
        for d in range(1, 8):
            tot = tot + g_ref[d]
        rows_ref[...] = tot
        qk_ref[...] = jnp.dot(rows_ref[6:8, :], f64_ref[...], precision=HIGHEST, preferred_element_type=F32)
        gg_ref[...] = jnp.dot(rows_ref[9:10, :], f128_ref[...], precision=HIGHEST, preferred_element_type=F32)
        h0, h1 = hlb_ref[0:1, :], hlb_ref[1:2, :]
        mx = jnp.maximum(h0, h1)
        e0, e1 = jnp.exp(h0 - mx), jnp.exp(h1 - mx)
        lb = e1 / (e0 + e1)
        d1 = rows_ref[5:6, :] * lb * (1.0 - lb)
        lb_ref[...] = jnp.where(_iota((2, 1), 0) == 0, -d1, d1)

    return pl.pallas_call(
        body, name="small_finalize",
        out_shape=[jax.ShapeDtypeStruct((SMALL_ROWS, D), F32), jax.ShapeDtypeStruct((2, FOX_DH), F32),
                   jax.ShapeDtypeStruct((1, HG_D), F32), jax.ShapeDtypeStruct((2, D), F32)],
    )(gathered, hlb, fold64, fold128)


FOX_NAMES = ("fox_w_in", "fox_w_out")
LATE_NAMES = ("hgrn_w_in", "hgrn_w_out", "ffn_w_in", "ffn_w_out")
BIG_NAMES = FOX_NAMES + LATE_NAMES
COL_SHARDED = ("fox_w_in", "hgrn_w_in", "ffn_w_in")


def _shard2d(name, a):
    return a.reshape(-1, a.shape[-1])


def _unshard(name, g, layers):
    if name in COL_SHARDED:
        k = g.shape[1] // layers
        return g.reshape(4, layers, k, g.shape[2]).transpose(1, 2, 0, 3).reshape(layers, k, 4 * g.shape[2])
    r = g.shape[1] // layers
    return g.reshape(4, layers, r, g.shape[2]).transpose(1, 0, 2, 3).reshape(layers, 4 * r, g.shape[2])


def _to_shards(name, g):
    layers = g.shape[0]
    if name in COL_SHARDED:
        k, n = g.shape[1], g.shape[2] // 4
        return g.reshape(layers, k, 4, n).transpose(2, 0, 1, 3).reshape(4, layers * k, n)
    r = g.shape[1] // 4
    return g.reshape(layers, 4, r, g.shape[2]).transpose(1, 0, 2, 3).reshape(4, layers * r, g.shape[2])


def kernel(x, meta_tokens, attn_norm, ffn_norm, final_norm, fox_w_in, fox_b_f, fox_q_norm, fox_k_norm, fox_w_out, hgrn_w_in, hgrn_lower_bounds, hgrn_g_norm, hgrn_w_out, ffn_w_in, ffn_w_out, loss_target, m_meta_tokens, m_attn_norm, m_ffn_norm, m_final_norm, m_fox_w_in, m_fox_b_f, m_fox_q_norm, m_fox_k_norm, m_fox_w_out, m_hgrn_w_in, m_hgrn_lower_bounds, m_hgrn_g_norm, m_hgrn_w_out, m_ffn_w_in, m_ffn_w_out, v_meta_tokens, v_attn_norm, v_ffn_norm, v_final_norm, v_fox_w_in, v_fox_b_f, v_fox_q_norm, v_fox_k_norm, v_fox_w_out, v_hgrn_w_in, v_hgrn_lower_bounds, v_hgrn_g_norm, v_hgrn_w_out, v_ffn_w_in, v_ffn_w_out):
    params = dict(meta_tokens=meta_tokens, attn_norm=attn_norm, ffn_norm=ffn_norm, final_norm=final_norm,
                  fox_w_in=fox_w_in, fox_b_f=fox_b_f, fox_q_norm=fox_q_norm, fox_k_norm=fox_k_norm,
                  fox_w_out=fox_w_out, hgrn_w_in=hgrn_w_in, hgrn_lower_bounds=hgrn_lower_bounds,
                  hgrn_g_norm=hgrn_g_norm, hgrn_w_out=hgrn_w_out, ffn_w_in=ffn_w_in, ffn_w_out=ffn_w_out)
    mom_m = dict(meta_tokens=m_meta_tokens, attn_norm=m_attn_norm, ffn_norm=m_ffn_norm, final_norm=m_final_norm,
                 fox_w_in=m_fox_w_in, fox_b_f=m_fox_b_f, fox_q_norm=m_fox_q_norm, fox_k_norm=m_fox_k_norm,
                 fox_w_out=m_fox_w_out, hgrn_w_in=m_hgrn_w_in, hgrn_lower_bounds=m_hgrn_lower_bounds,
                 hgrn_g_norm=m_hgrn_g_norm, hgrn_w_out=m_hgrn_w_out, ffn_w_in=m_ffn_w_in, ffn_w_out=m_ffn_w_out)
    mom_v = dict(meta_tokens=v_meta_tokens, attn_norm=v_attn_norm, ffn_norm=v_ffn_norm, final_norm=v_final_norm,
                 fox_w_in=v_fox_w_in, fox_b_f=v_fox_b_f, fox_q_norm=v_fox_q_norm, fox_k_norm=v_fox_k_norm,
                 fox_w_out=v_fox_w_out, hgrn_w_in=v_hgrn_w_in, hgrn_lower_bounds=v_hgrn_lower_bounds,
                 hgrn_g_norm=v_hgrn_g_norm, hgrn_w_out=v_hgrn_w_out, ffn_w_in=v_ffn_w_in, ffn_w_out=v_ffn_w_out)
    names = list(params)
    seq = x.shape[1]
    xi, yi, ci = _me()

    shards = {n: _shard2d(n, params[n]).astype(BF16) for n in BIG_NAMES}
    layers = {n: params[n].shape[0] for n in BIG_NAMES}
    gathered = _chip_allgather([shards[n] for n in FOX_NAMES] + [meta_tokens])
    w = {n: _unshard(n, g, 1)[0] for n, g in zip(FOX_NAMES, gathered[:-1])}
    meta_full = gathered[-1].transpose(1, 0, 2).reshape(N_META, D)
    w.update(attn_norm=attn_norm, ffn_norm=ffn_norm, final_norm=final_norm.reshape(1, D), fox_b_f=fox_b_f,
             fox_q_norm=fox_q_norm, fox_k_norm=fox_k_norm, hgrn_lower_bounds=hgrn_lower_bounds,
             hgrn_g_norm=hgrn_g_norm)

    h0 = jnp.concatenate([jnp.zeros((ROW0, D), F32), meta_full, x[0]], axis=0)
    tgt = jnp.concatenate([jnp.zeros((PAD, D), F32), loss_target[0]], axis=0)
    loss, dh0, big, small = _local_step(h0, tgt, w, {n: shards[n] for n in LATE_NAMES}, layers)
    loss = lax.psum(loss, ("x", "y", "c"))
    grad_x = dh0[PAD:][None]
    grads = {n: big[n].reshape(params[n].shape) for n in BIG_NAMES}

    rows = jnp.concatenate([small["attn_norm"], small["ffn_norm"], small["final_norm"], small["lb_raw"],
                            small["q_gain"], small["k_gain"],
                            jnp.pad(small["b_f"], ((0, 0), (0, D - LANES))), small["g_gain"],
                            dh0[ROW0:PAD], jnp.zeros((SMALL_ROWS - 10 - N_META, D), F32)], axis=0)
    allrows = _device_allgather(rows)
    fold64 = jnp.asarray(np.tile(np.eye(FOX_DH, dtype=np.float32), (FOX_H, 1)))
    fold128 = jnp.asarray(np.tile(np.eye(HG_D, dtype=np.float32), (HG_H, 1)))
    tot, qk, gg, dlb = _small_finalize(allrows, hgrn_lower_bounds, fold64, fold128)
    grads.update(attn_norm=tot[0:2], ffn_norm=tot[2:4], final_norm=tot[4], hgrn_lower_bounds=dlb,
                 fox_q_norm=qk[0:1], fox_k_norm=qk[1:2], fox_b_f=tot[8:9, :FOX_H], hgrn_g_norm=gg,
                 meta_tokens=lax.dynamic_slice_in_dim(tot[10:10 + N_META], (2 * xi + yi) * (D // 4), D // 4, axis=1))

    delta, new_m, new_v = {}, {}, {}
    for n in BIG_NAMES + ("meta_tokens",):
        d_, m_, v_ = _adamw(_shard2d(n, params[n]), _shard2d(n, grads[n]), _shard2d(n, mom_m[n]),
                            _shard2d(n, mom_v[n]), f"adamw_{n}")
        delta[n], new_m[n], new_v[n] = (t.reshape(params[n].shape) for t in (d_, m_, v_))
    small_names = [n for n in names if n not in BIG_NAMES and n != "meta_tokens"]

    def pack(d):
        return jnp.concatenate([jnp.pad(d[n].reshape(-1, d[n].shape[-1]), ((0, 0), (0, D - d[n].shape[-1])))
                                for n in small_names], axis=0)

    packed = [pack(t) for t in (params, grads, mom_m, mom_v)]
    n_rows = packed[0].shape[0]
    packed = [jnp.pad(t, ((0, 16 - n_rows), (0, 0))) for t in packed]
    res = _adamw(*packed, "adamw_small")
    r0 = 0
    for n in small_names:
        nr = params[n].reshape(-1, params[n].shape[-1]).shape[0]
        for dst, src in zip((delta, new_m, new_v), res):
            dst[n] = src[r0:r0 + nr, :params[n].shape[-1]].reshape(params[n].shape)
        r0 += nr

    return (loss, grad_x, *[grads[n] for n in names], *[delta[n] for n in names],
            *[new_m[n] for n in names], *[new_v[n] for n in names])
```

```python
import functools

import numpy as np
import jax
import jax.numpy as jnp
from jax import lax
from jax.experimental import pallas as pl
from jax.experimental.pallas import tpu as pltpu

F32, BF16 = jnp.float32, jnp.bfloat16
HIGHEST = lax.Precision.HIGHEST

D = 1024
N_META = 16
PAD = 128
ROW0 = PAD - N_META
FOX_H, FOX_DH = 16, 64
HG_H, HG_D = 8, 128
HG_C = 128
HG_LEV = 7
HG_HPS = 4
FFN = 2816
EPS = 1e-6
BIG = 1e30
LOG2E = 1.4426950408889634
LANES = 128
MXU_N = 256
VMEM_LIMIT = 48 * 1024 * 1024
ROW_TILES = (640, 512, 384, 320, 256, 128, 64, 32, 16, 8)
ATTN_TILES = (640, 512, 256, 128)
FOX_HPS_FWD = 8

ADAM_LR, ADAM_B1, ADAM_B2, ADAM_EPS, ADAM_WD, ADAM_STEP = 0.001, 0.9, 0.999, 1e-08, 0.01, 10

MESH = pl.DeviceIdType.MESH
ANY = pl.BlockSpec(memory_space=pl.ANY)
NT = (((1,), (1,)), ((), ()))
TN = (((0,), (0,)), ((), ()))


def _tile(n, cands=ROW_TILES, cap=None):
    for c in cands:
        if n % c == 0 and (cap is None or c <= cap):
            return c
    return n


def _cparams(sem):
    return pltpu.CompilerParams(dimension_semantics=sem, vmem_limit_bytes=VMEM_LIMIT)


def _sigmoid(x):
    return jax.nn.sigmoid(x)


def _log_sigmoid(x):
    return jnp.minimum(x, 0.0) - jnp.log(1.0 + jnp.exp(-jnp.abs(x)))


def _iota(shape, dim):
    return lax.broadcasted_iota(jnp.int32, shape, dim)


def _matmul(a, b, *, ta=False, tb=False, out_dtype=F32, add=None, name):
    if ta:
        kdim, m = a.shape
    else:
        m, kdim = a.shape
    n = b.shape[0] if tb else b.shape[1]
    if ta:
        tm = m if m <= 1024 else _tile(m, (1408, 1024, 512, 256, 128))
        tk = _tile(kdim)
    else:
        tm = _tile(m)
        tk = kdim if kdim <= 4096 else _tile(kdim, (2048, 1024, 512))
    tn = n if n <= 1024 else _tile(n, (1408, 1024, 512, 256, 128))
    nk = kdim // tk
    dn = (((0 if ta else 1,), (1 if tb else 0,)), ((), ()))

    def body(*refs):
        if add is None:
            a_ref, b_ref, o_ref, acc_ref = refs
        else:
            a_ref, b_ref, add_ref, o_ref, acc_ref = refs
        k = pl.program_id(2)

        @pl.when(k == 0)
        def _():
            acc_ref[...] = jnp.zeros_like(acc_ref)

        acc_ref[...] += lax.dot_general(a_ref[...].astype(BF16), b_ref[...].astype(BF16), dn,
                                        preferred_element_type=F32)

        @pl.when(k == nk - 1)
        def _():
            r = acc_ref[...]
            if add is not None:
                r = r + add_ref[...].astype(F32)
            o_ref[...] = r.astype(o_ref.dtype)

    a_spec = pl.BlockSpec((tk, tm), lambda j, i, k: (k, i)) if ta else pl.BlockSpec((tm, tk), lambda j, i, k: (i, k))
    b_spec = pl.BlockSpec((tn, tk), lambda j, i, k: (j, k)) if tb else pl.BlockSpec((tk, tn), lambda j, i, k: (k, j))
    o_spec = pl.BlockSpec((tm, tn), lambda j, i, k: (i, j))
    ins, specs = [a, b], [a_spec, b_spec]
    if add is not None:
        ins.append(add)
        specs.append(o_spec)
    return pl.pallas_call(
        body, name=name, grid=(n // tn, m // tm, nk), in_specs=specs, out_specs=o_spec,
        out_shape=jax.ShapeDtypeStruct((m, n), out_dtype),
        scratch_shapes=[pltpu.VMEM((tm, tn), F32)],
        compiler_params=_cparams(("parallel", "parallel", "arbitrary")),
    )(*ins)


def _rowwise(fn, ins, bcast, outs, accs, *, name, reverse=False, carry=None, as_refs=False):
    rows = ins[0].shape[0]
    per_row = sum(x.shape[1] * x.dtype.itemsize for x in ins) + sum(c * jnp.dtype(d).itemsize for c, d in outs)
    tm = _tile(rows, cap=max(8, (10 * 1024 * 1024) // per_row))
    n = rows // tm
    n_in, n_b, n_o, n_a = len(ins), len(bcast), len(outs), len(accs)

    def body(*refs):
        in_refs = refs[:n_in]
        b_refs = refs[n_in:n_in + n_b]
        o_refs = refs[n_in + n_b:n_in + n_b + n_o]
        a_refs = refs[n_in + n_b + n_o:n_in + n_b + n_o + n_a]
        c_refs = refs[n_in + n_b + n_o + n_a:]
        i = pl.program_id(0)
        blk = (n - 1 - i) if reverse else i
        if c_refs:
            @pl.when(i == 0)
            def _():
                c_refs[0][...] = jnp.zeros_like(c_refs[0])
        args = (list(in_refs) if as_refs else [r[...] for r in in_refs], [r[...] for r in b_refs])
        o_vals, a_vals = fn(blk * tm, *args, *c_refs)
        for r, v in zip(o_refs, o_vals):
            r[...] = v.astype(r.dtype)
        if n_a:
            @pl.when(i == 0)
            def _():
                for r in a_refs:
                    r[...] = jnp.zeros_like(r)
            for r, v in zip(a_refs, a_vals):
                r[...] += v

    def row_map(i):
        return ((n - 1 - i) if reverse else i, 0)

    in_specs = [pl.BlockSpec((tm, x.shape[1]), row_map) for x in ins]
    in_specs += [pl.BlockSpec(x.shape, lambda i, nd=x.ndim: (0,) * nd) for x in bcast]
    out_specs = [pl.BlockSpec((tm, c), row_map) for c, _ in outs]
    out_specs += [pl.BlockSpec(s, lambda i: (0, 0)) for s in accs]
    out_shape = [jax.ShapeDtypeStruct((rows, c), d) for c, d in outs]
    out_shape += [jax.ShapeDtypeStruct(s, F32) for s in accs]
    res = pl.pallas_call(
        body, name=name, grid=(n,), in_specs=in_specs, out_specs=out_specs, out_shape=out_shape,
        scratch_shapes=[pltpu.VMEM(carry, F32)] if carry else [],
        compiler_params=_cparams(("arbitrary",)),
    )(*ins, *bcast)
    return res[:n_o], res[n_o:]


def _row_ids(row0, tm):
    return row0 + _iota((tm, 1), 0)


def _rms_fwd(x, gain, name):
    def fn(row0, ins, bc):
        (xv,), (g,) = ins, bc
        r = lax.rsqrt(jnp.mean(xv * xv, axis=-1, keepdims=True) + EPS)
        return [xv * r * g], []
    return _rowwise(fn, [x], [gain], [(D, BF16)], [], name=name)[0][0]


def _rms_bwd(x, dxn, gain, dh_up, name):
    def fn(row0, ins, bc):
        xv, dy, up = ins
        (g,) = bc
        dy = dy.astype(F32)
        r = lax.rsqrt(jnp.mean(xv * xv, axis=-1, keepdims=True) + EPS)
        xh = xv * r
        dxh = dy * g
        dx = r * (dxh - xh * jnp.mean(dxh * xh, axis=-1, keepdims=True))
        keep = _row_ids(row0, xv.shape[0]) >= ROW0
        return [jnp.where(keep, up + dx, 0.0)], [jnp.sum(dy * xh, axis=0, keepdims=True)]
    (dh,), (dgain,) = _rowwise(fn, [x, dxn, dh_up], [gain], [(D, F32)], [(1, D)], name=name)
    return dh, dgain


def _loss_bwd(h, tgt, gain):
    def fn(row0, ins, bc):
        xv, t = ins
        (g,) = bc
        r = lax.rsqrt(jnp.mean(xv * xv, axis=-1, keepdims=True) + EPS)
        xh = xv * r
        keep = _row_ids(row0, xv.shape[0]) >= PAD
        err = jnp.where(keep, xh * g - t, 0.0)
        per_row = jnp.mean(err * err, axis=-1, keepdims=True)
        loss = 0.5 * jnp.sum(per_row, axis=0, keepdims=True)
        dy = err * (1.0 / D)
        dxh = dy * g
        dx = r * (dxh - xh * jnp.mean(dxh * xh, axis=-1, keepdims=True))
        return [dx], [jnp.broadcast_to(loss, (1, LANES)), jnp.sum(dy * xh, axis=0, keepdims=True)]
    (dh,), (loss, dgain) = _rowwise(fn, [h, tgt], [gain], [(D, F32)], [(1, LANES), (1, D)], name="loss_bwd")
    return loss[0, 0], dh, dgain


FFN_TILES = dict(rows=(320, 256, 128), cols=(1408, 1024, 512, 256, 128))


def _ffn_in(hn, wg, wu, name):
    m, kdim = hn.shape
    n = wg.shape[1]
    tm, tn = _tile(m, FFN_TILES["rows"]), _tile(n, FFN_TILES["cols"])

    def body(a_ref, wg_ref, wu_ref, g_ref, u_ref, act_ref):
        a = a_ref[...]
        g = jnp.dot(a, wg_ref[...], preferred_element_type=F32)
        u = jnp.dot(a, wu_ref[...], preferred_element_type=F32)
        g_ref[...] = g.astype(g_ref.dtype)
        u_ref[...] = u.astype(u_ref.dtype)
        act_ref[...] = (g * _sigmoid(g) * u).astype(act_ref.dtype)

    wspec = pl.BlockSpec((kdim, tn), lambda j, i: (0, j))
    ospec = pl.BlockSpec((tm, tn), lambda j, i: (i, j))
    return pl.pallas_call(
        body, name=name, grid=(n // tn, m // tm),
        in_specs=[pl.BlockSpec((tm, kdim), lambda j, i: (i, 0)), wspec, wspec], out_specs=[ospec] * 3,
        out_shape=[jax.ShapeDtypeStruct((m, n), BF16)] * 3,
        compiler_params=_cparams(("parallel", "parallel")),
    )(hn, wg, wu)


def _ffn_dact(dh, wo, g, u, name):
    m, kdim = dh.shape
    n = wo.shape[0]
    tm, tn = _tile(m, FFN_TILES["rows"]), _tile(n, FFN_TILES["cols"])

    def body(a_ref, w_ref, g_ref, u_ref, dg_ref, du_ref):
        da = lax.dot_general(a_ref[...].astype(BF16), w_ref[...], NT, preferred_element_type=F32)
        gv, uv = g_ref[...].astype(F32), u_ref[...].astype(F32)
        s = _sigmoid(gv)
        dg_ref[...] = (da * uv * (s * (1.0 + gv * (1.0 - s)))).astype(dg_ref.dtype)
        du_ref[...] = (da * gv * s).astype(du_ref.dtype)

    ospec = pl.BlockSpec((tm, tn), lambda j, i: (i, j))
    return pl.pallas_call(
        body, name=name, grid=(n // tn, m // tm),
        in_specs=[pl.BlockSpec((tm, kdim), lambda j, i: (i, 0)), pl.BlockSpec((tn, kdim), lambda j, i: (j, 0)),
                  ospec, ospec],
        out_specs=[ospec] * 2, out_shape=[jax.ShapeDtypeStruct((m, n), BF16)] * 2,
        compiler_params=_cparams(("parallel", "parallel")),
    )(dh, wo, g, u)


def _adamw(w, g, m, v, name):
    def fn(row0, ins, bc):
        wv, gv, mv, vv = ins
        mn = ADAM_B1 * mv + (1.0 - ADAM_B1) * gv
        vn = ADAM_B2 * vv + (1.0 - ADAM_B2) * (gv * gv)
        m_hat = mn / (1.0 - ADAM_B1 ** ADAM_STEP)
        v_hat = vn / (1.0 - ADAM_B2 ** ADAM_STEP)
        delta = -ADAM_LR * (m_hat / (jnp.sqrt(v_hat) + ADAM_EPS) + ADAM_WD * wv)
        return [delta, mn, vn], []
    c = w.shape[1]
    return _rowwise(fn, [w, g, m, v], [], [(c, F32)] * 3, [], name=name)[0]


def _add(xs, name):
    def fn(row0, ins, bc):
        r = ins[0].astype(F32)
        for v in ins[1:]:
            r = r + v.astype(F32)
        return [r], []
    return _rowwise(fn, list(xs), [], [(xs[0].shape[1], F32)], [], name=name)[0][0]


def _head_sum(x, gmat):
    hi = x.astype(BF16)
    lo = (x - hi.astype(F32)).astype(BF16)
    w = gmat.shape[0]
    return jnp.concatenate(
        [jnp.dot(hi[:, b:b + w], gmat, preferred_element_type=F32) + jnp.dot(lo[:, b:b + w], gmat,
                                                                             preferred_element_type=F32)
         for b in range(0, x.shape[1], w)], axis=1)


def _split3(x):
    hi = x.astype(BF16).astype(F32)
    r = x - hi
    mid = r.astype(BF16).astype(F32)
    return hi, mid, r - mid


def _extra_base(hh):
    return FOX_DH * (1 - hh)


def _data_mask(hh):
    lane = _iota((1, LANES), 1)
    return (lane >= FOX_DH * hh) & (lane < FOX_DH * (hh + 1))


def _with_extras(data, hh, vals):
    lane = _iota((1, LANES), 1)
    x = jnp.zeros_like(data)
    for e, v in enumerate(vals):
        x = jnp.where(lane == _extra_base(hh) + e, v, x)
    return jnp.where(_data_mask(hh), data, x)


def _fox_pack_fwd(q_raw, k_raw, v, cq, ck, qg, kg, gmat):
    scale2 = FOX_DH ** -0.5 * LOG2E

    def fn(row0, refs, bc):
        q_ref, k_ref, v_ref, cq_ref, ck_ref = refs
        g_q, g_k, gm = bc
        qv, kv = q_ref[...], k_ref[...]
        qn = qv * lax.rsqrt(_head_sum(qv * qv, gm) * (1.0 / FOX_DH) + EPS) * (g_q * scale2)
        kn = kv * lax.rsqrt(_head_sum(kv * kv, gm) * (1.0 / FOX_DH) + EPS) * g_k
        qs, ks, vs = [], [], []
        for h in range(FOX_H):
            p, hh = divmod(h, 2)
            sl = slice(p * LANES, (p + 1) * LANES)
            cq3 = _split3(cq_ref[:, h:h + 1] * LOG2E)
            ck3 = _split3(ck_ref[:, h:h + 1] * (-LOG2E))
            qs.append(_with_extras(qn[:, sl], hh, [*cq3, 1.0, 1.0, 1.0]))
            ks.append(_with_extras(kn[:, sl], hh, [1.0, 1.0, 1.0, *ck3]))
            vs.append(_with_extras(v_ref[:, sl].astype(F32), hh, [1.0, 1.0]))
        return [jnp.concatenate(qs, axis=1), jnp.concatenate(ks, axis=1), jnp.concatenate(vs, axis=1)], []

    w = FOX_H * LANES
    return _rowwise(fn, [q_raw, k_raw, v, cq, ck], [qg, kg, gmat], [(w, BF16)] * 3, [], name="fox_pack_fwd",
                    as_refs=True)[0]


def _fox_pack_bias(qp, cq, lse2):
    def fn(row0, refs, bc):
        q_ref, cq_ref, lse_ref = refs
        lane = _iota((1, LANES), 1)
        outs = []
        for h in range(FOX_H):
            blk = q_ref[:, h * LANES:(h + 1) * LANES].astype(F32)
            for e, part in enumerate(_split3(cq_ref[:, h:h + 1] * LOG2E - lse_ref[:, h:h + 1])):
                blk = jnp.where(lane == _extra_base(h % 2) + e, part, blk)
            outs.append(blk)
        return [jnp.concatenate(outs, axis=1)], []
    return _rowwise(fn, [qp, cq, lse2], [], [(FOX_H * LANES, BF16)], [], name="fox_pack_bias", as_refs=True)[0][0]


def _fox_pack_bwd(dog, o, gate):
    def fn(row0, refs, bc):
        d_ref, o_ref, g_ref = refs
        dos, dgs = [], []
        for p in range(FOX_H // 2):
            sl = slice(p * LANES, (p + 1) * LANES)
            dv, ov, gv = (r[:, sl].astype(F32) for r in (d_ref, o_ref, g_ref))
            s = _sigmoid(gv)
            do = dv * s
            dgs.append(dv * ov * s * (1.0 - s))
            od = ov * do
            for hh in range(2):
                delta = jnp.sum(jnp.where(_data_mask(hh), od, 0.0), axis=-1, keepdims=True)
                hi = delta.astype(BF16).astype(F32)
                dos.append(_with_extras(do, hh, [-hi, hi - delta]))
        return [jnp.concatenate(dos, axis=1), jnp.concatenate(dgs, axis=1)], []
    return _rowwise(fn, [dog, o, gate], [], [(FOX_H * LANES, BF16), (D, BF16)], [], name="fox_pack_bwd",
                    as_refs=True)[0]


def _fox_unpack_bwd(q_raw, k_raw, dqp, dk, qg, kg, gmat):
    scale = FOX_DH ** -0.5

    def fn(row0, refs, bc):
        q_ref, k_ref, dq_ref, dk_ref = refs
        g_q, g_k, gm = bc
        lane = _iota((1, LANES), 1)
        dqs = []
        dcq = jnp.zeros((q_ref.shape[0], LANES), F32)
        for p in range(FOX_H // 2):
            even = dq_ref[:, (2 * p) * LANES:(2 * p + 1) * LANES]
            odd = dq_ref[:, (2 * p + 1) * LANES:(2 * p + 2) * LANES]
            dqs.append(jnp.where(_data_mask(0), even, odd) * scale)
            for hh in range(2):
                col = (2 * p + hh) * LANES + _extra_base(hh)
                dcq = jnp.where(lane == 2 * p + hh, dq_ref[:, col:col + 1], dcq)
        outs, accs = [], []
        for xv, dy, g in ((q_ref[...], jnp.concatenate(dqs, axis=1), g_q), (k_ref[...], dk_ref[...] * (1.0 / LOG2E), g_k)):
            r = lax.rsqrt(_head_sum(xv * xv, gm) * (1.0 / FOX_DH) + EPS)
            xh = xv * r
            dxh = dy * g
            outs.append(r * (dxh - xh * (_head_sum(dxh * xh, gm) * (1.0 / FOX_DH))))
            accs.append(jnp.sum(dy * xh, axis=0, keepdims=True))
        return outs + [dcq], accs
    return _rowwise(fn, [q_raw, k_raw, dqp, dk], [qg, kg, gmat], [(D, BF16), (D, BF16), (LANES, F32)],
                    [(1, D), (1, D)], name="fox_unpack_bwd", as_refs=True)


def _fox_cumsum_fwd(flog, bf):
    def fn(row0, ins, bc, carry):
        (f,), (b,) = ins, bc
        tm = f.shape[0]
        keep = _row_ids(row0, tm) >= ROW0
        lf = jnp.where(keep, _log_sigmoid(f + b), 0.0)
        tri = (_iota((tm, tm), 0) >= _iota((tm, tm), 1)).astype(F32)
        c = jnp.dot(tri, lf, precision=HIGHEST, preferred_element_type=F32) + carry[...]
        carry[...] = carry[...] + jnp.sum(lf, axis=0, keepdims=True)
        return [c, jnp.where(keep, c, BIG)], []
    return _rowwise(fn, [flog], [bf], [(LANES, F32), (LANES, F32)], [], name="fox_cumsum_fwd",
                    carry=(1, LANES))[0]


def _fox_cumsum_bwd(dc_q, dc_k, flog, bf):
    def fn(row0, ins, bc, carry):
        (dq, dk, f), (b,) = ins, bc
        d = dq + dk
        tm = f.shape[0]
        keep = _row_ids(row0, tm) >= ROW0
        triu = (_iota((tm, tm), 0) <= _iota((tm, tm), 1)).astype(F32)
        dlf = jnp.dot(triu, d, precision=HIGHEST, preferred_element_type=F32) + carry[...]
        carry[...] = carry[...] + jnp.sum(d, axis=0, keepdims=True)
        dfl = jnp.where(keep, dlf * _sigmoid(-(f + b)), 0.0)
        return [dfl], [jnp.sum(dfl, axis=0, keepdims=True)]
    (dflog,), (dbf,) = _rowwise(fn, [dc_q, dc_k, flog], [bf], [(LANES, F32)], [(1, LANES)], name="fox_cumsum_bwd",
                                reverse=True, carry=(1, LANES))
    return dflog, dbf


def _causal_steps(n, key_major):
    if key_major:
        pairs = [(i, j) for j in range(n) for i in range(j, n)]
    else:
        pairs = [(i, j) for i in range(n) for j in range(i + 1)]
    return (jnp.asarray(np.array([p[0] for p in pairs], np.int32)),
            jnp.asarray(np.array([p[1] for p in pairs], np.int32)))


def _fox_attn_fwd(qp, kp, vp, gate, shards):
    L = qp.shape[0]
    t = _tile(L, ATTN_TILES)
    n = L // t
    hps = FOX_HPS_FWD
    P = FOX_H // hps
    it, jt = _causal_steps(n, False)
    n_steps = it.shape[0]
    ns = len(shards)

    def body(it_ref, jt_ref, q_ref, k_ref, v_ref, g_ref, *rest):
        sh_in, (o_ref, og_ref, lse_ref), sh_out = rest[:ns], rest[ns:ns + 3], rest[ns + 3:2 * ns + 3]
        m_sc, acc, ssem, rsem, lsem = rest[2 * ns + 3:]
        step = pl.program_id(1)
        i, j = it_ref[step], jt_ref[step]
        first = (pl.program_id(0) == 0) & (step == 0)
        last = (pl.program_id(0) == P - 1) & (step == n_steps - 1)

        @pl.when(first)
        def _():
            _gather_start(sh_in, sh_out, ssem, rsem, lsem)

        @pl.when(j == 0)
        def _():
            m_sc[...] = jnp.full_like(m_sc, -3.0e38)
            acc[...] = jnp.zeros_like(acc)

        def update(masked):
            for hh in range(hps):
                sl = slice(hh * LANES, (hh + 1) * LANES)
                s2 = lax.dot_general(k_ref[:, sl], q_ref[:, sl], NT, preferred_element_type=F32)
                if masked:
                    s2 = jnp.where(_iota((t, t), 0) <= _iota((t, t), 1), s2, -jnp.inf)
                m_old = m_sc[hh]
                m_new = jnp.maximum(m_old, jnp.max(s2, axis=0, keepdims=True))
                p = jnp.exp2(s2 - m_new)
                acc[hh] = jnp.exp2(m_old - m_new) * acc[hh] + lax.dot_general(v_ref[:, sl], p.astype(BF16), TN,
                                                                              preferred_element_type=F32)
                m_sc[hh] = m_new

        @pl.when(j < i)
        def _():
            update(False)

        @pl.when(j == i)
        def _():
            update(True)
            outs = []
            for hh in range(hps):
                base = _extra_base(hh % 2)
                l = acc[hh, base:base + 1, :]
                outs.append((acc[hh] / l).T)
                lse_ref[0, hh:hh + 1, :] = m_sc[hh] + jnp.log2(l)
            o = jnp.concatenate([jnp.where(_data_mask(0), outs[a], outs[a + 1]) for a in range(0, hps, 2)], axis=1)
            o_ref[...] = o.astype(o_ref.dtype)
            og_ref[...] = (o * _sigmoid(g_ref[...].astype(F32))).astype(og_ref.dtype)

        @pl.when(last)
        def _():
            _gather_wait(sh_in, sh_out, ssem, rsem, lsem)

    qspec = pl.BlockSpec((t, hps * LANES), lambda p, s, it, jt: (it[s], p))
    kspec = pl.BlockSpec((t, hps * LANES), lambda p, s, it, jt: (jt[s], p))
    ospec = pl.BlockSpec((t, hps * FOX_DH), lambda p, s, it, jt: (it[s], p))
    lspec = pl.BlockSpec((1, hps, t), lambda p, s, it, jt: (p, 0, it[s]))
    res = pl.pallas_call(
        body, name="fox_attn_fwd",
        grid_spec=pltpu.PrefetchScalarGridSpec(
            num_scalar_prefetch=2, grid=(P, n_steps),
            in_specs=[qspec, kspec, kspec, ospec] + [ANY] * ns, out_specs=[ospec, ospec, lspec] + [ANY] * ns,
            scratch_shapes=[pltpu.VMEM((hps, 1, t), F32), pltpu.VMEM((hps, LANES, t), F32)] + _gather_sems(ns)),
        out_shape=[jax.ShapeDtypeStruct((L, D), BF16), jax.ShapeDtypeStruct((L, D), BF16),
                   jax.ShapeDtypeStruct((P, hps, L), F32)]
        + [jax.ShapeDtypeStruct((4,) + a.shape, a.dtype) for a in shards],
        compiler_params=_cparams(("arbitrary", "arbitrary")),
    )(it, jt, qp, kp, vp, gate, *shards)
    return res[0], res[1], res[2], res[3:]


def _fox_attn_bwd(qb, kp, vp, dop, slabs):
    L = qb.shape[0]
    t = _tile(L, ATTN_TILES)
    n = L // t
    P = FOX_H // 2
    it, jt = _causal_steps(n, True)
    n_steps = it.shape[0]
    ns = len(slabs)

    def body(it_ref, jt_ref, q_ref, k_ref, v_ref, do_ref, *rest):
        sl_in, (dq_ref, dk_ref, dv_ref, dck_ref), sl_out = rest[:ns], rest[ns:ns + 4], rest[ns + 4:2 * ns + 4]
        dk_acc, dv_acc, ssem, rsem = rest[2 * ns + 4:]
        step = pl.program_id(1)
        i, j = it_ref[step], jt_ref[step]

        @pl.when((pl.program_id(0) == 0) & (step == 0))
        def _():
            for cp in _scatter_copies(sl_in, sl_out, ssem, rsem):
                cp.start()

        @pl.when(step == 0)
        def _():
            dq_ref[...] = jnp.zeros_like(dq_ref)

        @pl.when(i == j)
        def _():
            dk_acc[...] = jnp.zeros_like(dk_acc)
            dv_acc[...] = jnp.zeros_like(dv_acc)

        def update(masked):
            rows = pl.ds(pl.multiple_of(i * t, LANES), t)
            for hh in range(2):
                sl = slice(hh * LANES, (hh + 1) * LANES)
                q, k, dov = q_ref[:, sl], k_ref[:, sl], do_ref[:, sl]
                s2 = lax.dot_general(k, q, NT, preferred_element_type=F32)
                if masked:
                    s2 = jnp.where(_iota((t, t), 0) <= _iota((t, t), 1), s2, -jnp.inf)
                p = jnp.exp2(s2)
                ds = (p * lax.dot_general(v_ref[:, sl], dov, NT, preferred_element_type=F32)).astype(BF16)
                dv_acc[hh] += jnp.dot(p.astype(BF16), dov, preferred_element_type=F32)
                dk_acc[hh] += jnp.dot(ds, q, preferred_element_type=F32)
                dq_ref[rows, sl] += lax.dot_general(ds, k, TN, preferred_element_type=F32)

        @pl.when(i > j)
        def _():
            update(False)

        @pl.when(i == j)
        def _():
            update(True)

        @pl.when(i == n - 1)
        def _():
            dk_ref[...] = jnp.where(_data_mask(0), dk_acc[0], dk_acc[1])
            dv_ref[...] = jnp.where(_data_mask(0), dv_acc[0], dv_acc[1]).astype(dv_ref.dtype)
            col_sums = [dk_acc[hh, :, _extra_base(hh) + 3:_extra_base(hh) + 4] for hh in range(2)]
            dck_ref[0] = -jnp.where(_iota((1, 2), 1) == 0, col_sums[0], col_sums[1])

        @pl.when((pl.program_id(0) == P - 1) & (step == n_steps - 1))
        def _():
            for cp in _scatter_copies(sl_in, sl_out, ssem, rsem):
                cp.wait()

    qspec = pl.BlockSpec((t, 2 * LANES), lambda p, s, it, jt: (it[s], p))
    kspec = pl.BlockSpec((t, 2 * LANES), lambda p, s, it, jt: (jt[s], p))
    ospec = pl.BlockSpec((t, LANES), lambda p, s, it, jt: (jt[s], p))
    res = pl.pallas_call(
        body, name="fox_attn_bwd",
        grid_spec=pltpu.PrefetchScalarGridSpec(
            num_scalar_prefetch=2, grid=(P, n_steps),
            in_specs=[qspec, kspec, kspec, qspec] + [ANY] * ns,
            out_specs=[pl.BlockSpec((L, 2 * LANES), lambda p, s, it, jt: (0, p)), ospec, ospec,
                       pl.BlockSpec((1, t, 2), lambda p, s, it, jt: (p, jt[s], 0))] + [ANY] * ns,
            scratch_shapes=[pltpu.VMEM((2, t, LANES), F32), pltpu.VMEM((2, t, LANES), F32),
                            pltpu.SemaphoreType.DMA((ns, 3)), pltpu.SemaphoreType.DMA((ns, 3))]),
        out_shape=[jax.ShapeDtypeStruct((L, FOX_H * LANES), F32), jax.ShapeDtypeStruct((L, D), F32),
                   jax.ShapeDtypeStruct((L, D), BF16), jax.ShapeDtypeStruct((P, L, 2), F32)]
        + [jax.ShapeDtypeStruct((3,) + a.shape[1:], a.dtype) for a in slabs],
        compiler_params=_cparams(("arbitrary", "arbitrary")),
    )(it, jt, qb, kp, vp, dop, *slabs)
    return res[0], res[1], res[2], res[3], res[4:]


def _hgrn_consts():
    C = HG_C
    r = np.arange(C)[:, None]
    j = np.arange(C)[None, :]
    mats = [j <= r, j > r]
    masks = []
    n = C
    while n >= 2:
        half = n // 2
        mid = (r // n) * n + half - 1
        second = (r % n) >= half
        mats.append(np.where(second, (j > mid) & (j <= r), (j > r) & (j <= mid)))
        masks.append(((r // n) == (j // n)) & ((r % n) >= half) & ((j % n) < half))
        n //= 2
    return (jnp.asarray(np.concatenate(mats, 0).astype(np.float32), BF16),
            jnp.asarray(np.stack(masks).astype(np.float32), F32))


def _hg_pre(hq, hz, h0, h1):
    mx = jnp.maximum(h0, h1)
    e0, e1 = jnp.exp(h0 - mx), jnp.exp(h1 - mx)
    lb = e1 / (e0 + e1)
    sq = _sigmoid(hq)
    sz = _sigmoid(hz)
    snz = 1.0 - sz
    k = (1.0 - lb) * snz
    g = jnp.maximum(jnp.log(lb + (1.0 - lb) * sz), -BIG)
    return lb, hq * sq, sq, k, sz, snz, g


def _hg_decays(g, rmat):
    hi = g.astype(BF16)
    lo = (g - hi.astype(F32)).astype(BF16)
    d = jnp.dot(rmat, jnp.concatenate([hi, lo], axis=1), preferred_element_type=F32)
    return jnp.exp(d[:, :HG_D] + d[:, HG_D:])


def _hg_intra(q, k, fall, masks):
    C = HG_C
    eye = _iota((C, C), 0) == _iota((C, C), 1)
    a = jnp.where(eye, jnp.sum(q * k, axis=-1, keepdims=True), 0.0)
    for l in range(HG_LEV):
        f = fall[(2 + l) * C:(3 + l) * C]
        a = a + masks[l] * lax.dot_general((q * f).astype(BF16), (k * f).astype(BF16), NT,
                                           preferred_element_type=F32)
    return a


def _hgrn_specs(n_chunks, reverse):
    C = HG_C
    w = HG_HPS * HG_D

    def col(first_head):
        off = first_head // HG_HPS
        if reverse:
            return pl.BlockSpec((C, w), lambda h, c: (n_chunks - 1 - c, off + h))
        return pl.BlockSpec((C, w), lambda h, c: (c, off + h))

    st = pl.BlockSpec((HG_HPS, 1, HG_D, HG_D),
                      (lambda h, c: (h, n_chunks - 1 - c, 0, 0)) if reverse else (lambda h, c: (h, c, 0, 0)))
    consts = [pl.BlockSpec((2, w), lambda h, c: (0, h)), pl.BlockSpec((1, HG_D), lambda h, c: (0, 0)),
              pl.BlockSpec(((2 + HG_LEV) * C, C), lambda h, c: (0, 0)),
              pl.BlockSpec((HG_LEV, C, C), lambda h, c: (0, 0, 0))]
    return col, st, consts


def _hgrn_fwd(proj, hlb, gg, rmat, masks):
    L = proj.shape[0]
    C = HG_C
    nc = L // C
    col, st, consts = _hgrn_specs(nc, False)

    def body(hq_ref, hz_ref, hi_ref, hg_ref, hlb_ref, gg_ref, r_ref, m_ref, og_ref, st_ref, state):
        c = pl.program_id(1)

        @pl.when(c == 0)
        def _():
            state[...] = jnp.zeros_like(state)

        for hh in range(HG_HPS):
            sl = slice(hh * HG_D, (hh + 1) * HG_D)
            v, hg = hi_ref[:, sl], hg_ref[:, sl]
            _, q, _, k, _, _, g = _hg_pre(hq_ref[:, sl], hz_ref[:, sl], hlb_ref[0:1, sl], hlb_ref[1:2, sl])
            fall = _hg_decays(g, r_ref[...])
            fb, fe = fall[0:C], fall[C:2 * C]
            st0 = state[hh]
            st_ref[hh, 0] = st0
            a = _hg_intra(q, k, fall, m_ref[...])
            vb = v.astype(BF16)
            o = jnp.dot(a.astype(BF16), vb, preferred_element_type=F32)
            o = o + lax.dot_general((q * fb).astype(BF16), st0.astype(BF16), NT, preferred_element_type=F32)
            ebc = jnp.exp(jnp.sum(g, axis=0, keepdims=True))
            state[hh] = st0 * ebc + lax.dot_general(vb, (k * fe).astype(BF16), TN, preferred_element_type=F32)
            r = lax.rsqrt(jnp.mean(o * o, axis=-1, keepdims=True) + EPS)
            og_ref[:, sl] = (o * r * gg_ref[...] * (hg * _sigmoid(hg))).astype(og_ref.dtype)

    return pl.pallas_call(
        body, name="hgrn_fwd", grid=(HG_H // HG_HPS, nc),
        in_specs=[col(0), col(HG_H), col(2 * HG_H), col(3 * HG_H)] + consts,
        out_specs=[col(0), st],
        out_shape=[jax.ShapeDtypeStruct((L, D), BF16), jax.ShapeDtypeStruct((HG_H, nc, HG_D, HG_D), F32)],
        scratch_shapes=[pltpu.VMEM((HG_HPS, HG_D, HG_D), F32)],
        compiler_params=_cparams(("parallel", "arbitrary")),
    )(proj, proj, proj, proj, hlb, gg, rmat, masks)


def _hgrn_bwd(proj, dog, states, hlb, gg, rmat, masks):
    L = proj.shape[0]
    C = HG_C
    nc = L // C
    col, st, consts = _hgrn_specs(nc, True)

    def body(hq_ref, hz_ref, hi_ref, hg_ref, do_ref, hlb_ref, gg_ref, r_ref, m_ref, st_ref,
             dq_ref, dz_ref, di_ref, dg_ref, dlb_ref, dgg_ref, dstate):
        c = pl.program_id(1)

        @pl.when(c == 0)
        def _():
            dstate[...] = jnp.zeros_like(dstate)
            dlb_ref[...] = jnp.zeros_like(dlb_ref)
            dgg_ref[...] = jnp.zeros_like(dgg_ref)

        for hh in range(HG_HPS):
            bwd_head(c, hh, slice(hh * HG_D, (hh + 1) * HG_D), hq_ref, hz_ref, hi_ref, hg_ref, do_ref, hlb_ref, gg_ref,
                     r_ref, m_ref, st_ref, dq_ref, dz_ref, di_ref, dg_ref, dlb_ref, dgg_ref, dstate)

    def bwd_head(c, hh, sl, hq_ref, hz_ref, hi_ref, hg_ref, do_ref, hlb_ref, gg_ref, r_ref, m_ref, st_ref,
                 dq_ref, dz_ref, di_ref, dg_ref, dlb_ref, dgg_ref, dstate):
        hq, hz, v, hg = hq_ref[:, sl], hz_ref[:, sl], hi_ref[:, sl], hg_ref[:, sl]
        dout = do_ref[:, sl].astype(F32)
        gain = gg_ref[...]
        masks_v = m_ref[...]
        lb, q, sq, k, sz, snz, g = _hg_pre(hq, hz, hlb_ref[0:1, sl], hlb_ref[1:2, sl])
        fall = _hg_decays(g, r_ref[...])
        fb, fe = fall[0:C], fall[C:2 * C]
        a = _hg_intra(q, k, fall, masks_v)
        st0 = st_ref[hh, 0]
        st0b = st0.astype(BF16)
        ebc = jnp.exp(jnp.sum(g, axis=0, keepdims=True))
        qb, ke, vb = (q * fb).astype(BF16), (k * fe).astype(BF16), v.astype(BF16)
        ab = a.astype(BF16)
        o = jnp.dot(ab, vb, preferred_element_type=F32) + lax.dot_general(qb, st0b, NT, preferred_element_type=F32)
        r = lax.rsqrt(jnp.mean(o * o, axis=-1, keepdims=True) + EPS)
        oh = o * r
        sg = _sigmoid(hg)
        d_on = dout * (hg * sg)
        dhg = dout * (oh * gain) * (sg * (1.0 + hg * (1.0 - sg)))
        dgg_ref[hh] += jnp.sum(d_on * oh, axis=0, keepdims=True)
        dxh = d_on * gain
        do = r * (dxh - oh * jnp.mean(dxh * oh, axis=-1, keepdims=True))
        dob = do.astype(BF16)
        dsp = dstate[hh]
        dspb = dsp.astype(BF16)
        causal = _iota((C, C), 0) >= _iota((C, C), 1)
        da = jnp.where(causal, lax.dot_general(dob, vb, NT, preferred_element_type=F32), 0.0)
        diag = jnp.sum(do * v, axis=-1, keepdims=True)
        dv = lax.dot_general(ab, dob, TN, preferred_element_type=F32)
        dv = dv + lax.dot_general(ke, dspb, NT, preferred_element_type=F32)
        xq = jnp.dot(dob, st0b, preferred_element_type=F32)
        xk = jnp.dot(vb, dspb, preferred_element_type=F32)
        dq = diag * k + fb * xq
        dk = diag * q + fe * xk
        ke_xk = ke.astype(F32) * xk
        db = qb.astype(F32) * xq - ke_xk
        for l in range(HG_LEV):
            f = fall[(2 + l) * C:(3 + l) * C]
            dal = (masks_v[l] * da).astype(BF16)
            ql, kl = (q * f).astype(BF16), (k * f).astype(BF16)
            xq = jnp.dot(dal, kl, preferred_element_type=F32)
            xk = lax.dot_general(dal, ql, TN, preferred_element_type=F32)
            dq = dq + f * xq
            dk = dk + f * xk
            db = db + ql.astype(F32) * xq - kl.astype(F32) * xk
        dstate[hh] = dsp * ebc + lax.dot_general(dob, qb, TN, preferred_element_type=F32)
        triu = (_iota((C, C), 0) <= _iota((C, C), 1)).astype(F32)
        dg = jnp.dot(triu, db, precision=HIGHEST, preferred_element_type=F32)
        dg = dg + jnp.sum(st0 * ebc * dsp, axis=0, keepdims=True) + jnp.sum(ke_xk, axis=0, keepdims=True)
        keep = _row_ids((nc - 1 - c) * C, C) >= ROW0
        dg = jnp.where(keep, dg, 0.0)
        dk = jnp.where(keep, dk, 0.0)
        f_gate = lb + (1.0 - lb) * sz
        dfdz = (1.0 - lb) * sz * snz
        dz_ref[:, sl] = (dg * dfdz / f_gate - dk * dfdz).astype(dz_ref.dtype)
        dlb_ref[:, sl] += jnp.sum(dg * snz / f_gate - dk * snz, axis=0, keepdims=True)
        dq_ref[:, sl] = jnp.where(keep, dq * (sq * (1.0 + hq * (1.0 - sq))), 0.0).astype(dq_ref.dtype)
        di_ref[:, sl] = jnp.where(keep, dv, 0.0).astype(di_ref.dtype)
        dg_ref[:, sl] = jnp.where(keep, dhg, 0.0).astype(dg_ref.dtype)

    w = HG_HPS * HG_D
    outs = pl.pallas_call(
        body, name="hgrn_bwd", grid=(HG_H // HG_HPS, nc),
        in_specs=[col(0), col(HG_H), col(2 * HG_H), col(3 * HG_H), col(0)] + consts + [st],
        out_specs=[col(0), col(0), col(0), col(0), pl.BlockSpec((1, w), lambda h, c: (0, h)),
                   pl.BlockSpec((HG_HPS, 1, HG_D), lambda h, c: (h, 0, 0))],
        out_shape=[jax.ShapeDtypeStruct((L, D), BF16)] * 4 + [jax.ShapeDtypeStruct((1, D), F32),
                                                              jax.ShapeDtypeStruct((HG_H, 1, HG_D), F32)],
        scratch_shapes=[pltpu.VMEM((HG_HPS, HG_D, HG_D), F32)],
        compiler_params=_cparams(("parallel", "arbitrary")),
    )(proj, proj, proj, proj, dog, hlb, gg, rmat, masks, states)
    return outs


def _ffn_fwd(h, norm_gain, wg, wu, wo, tag):
    hn = _rms_fwd(h, norm_gain, f"{tag}_norm")
    g, u, act = _ffn_in(hn, wg, wu, f"{tag}_in")
    h_out = _matmul(act, wo, add=h, name=f"{tag}_out")
    return h_out, (h, hn, g, u, act)


def _ffn_bwd(dh, saved, norm_gain, wg, wu, wo, tag):
    h, hn, g, u, act = saved
    dg, du = _ffn_dact(dh, wo, g, u, f"{tag}_dact")
    d_wo = _matmul(act, dh, ta=True, name=f"{tag}_dwo")
    dhn = _matmul(dg, wg, tb=True, name=f"{tag}_dhn_g")
    dhn = _matmul(du, wu, tb=True, add=dhn, name=f"{tag}_dhn_u")
    d_wg = _matmul(hn, dg, ta=True, name=f"{tag}_dwg")
    d_wu = _matmul(hn, du, ta=True, name=f"{tag}_dwu")
    dh, d_gain = _rms_bwd(h, dhn, norm_gain, dh, f"{tag}_norm_bwd")
    return dh, d_gain, (d_wg, d_wu, d_wo)


def _local_step(h0, tgt, w, late_shards):
    L = h0.shape[0]
    gmat = jnp.asarray(np.kron(np.eye(MXU_N // FOX_DH), np.ones((FOX_DH, FOX_DH))).astype(np.float32), BF16)
    rmat, lmasks = _hgrn_consts()
    an, fn_ = w["attn_norm"], w["ffn_norm"]
    qg = jnp.tile(w["fox_q_norm"], (1, FOX_H))
    kg = jnp.tile(w["fox_k_norm"], (1, FOX_H))
    bf = jnp.pad(w["fox_b_f"], ((0, 0), (0, LANES - FOX_H)))
    fw = w["fox_w_in"]
    f_wq, f_wk, f_wv, f_wg = (fw[:, i * D:(i + 1) * D] for i in range(4))
    f_wf = jnp.pad(fw[:, 4 * D:], ((0, 0), (0, LANES - FOX_H)))

    hn0 = _rms_fwd(h0, an[0:1], "fox_norm")
    q_raw = _matmul(hn0, f_wq, name="fox_q")
    k_raw = _matmul(hn0, f_wk, name="fox_k")
    v = _matmul(hn0, f_wv, out_dtype=BF16, name="fox_v")
    gate = _matmul(hn0, f_wg, out_dtype=BF16, name="fox_gate")
    flog = _matmul(hn0, f_wf, name="fox_flog")
    cq, ck = _fox_cumsum_fwd(flog, bf)
    qp, kp, vp = _fox_pack_fwd(q_raw, k_raw, v, cq, ck, qg, kg, gmat)
    o, og, lse2, gathered = _fox_attn_fwd(qp, kp, vp, gate, [late_shards[n] for n in GATHER_LATE])
    late = dict(zip(GATHER_LATE, gathered))
    f_wo = late["fox_w_out"].reshape(D, D)
    h_wo = late["hgrn_w_out"].reshape(D, D)
    h_wi = jnp.concatenate(list(late["hgrn_w_in"]), axis=1)
    g_in, g_out = late["ffn_w_in"], late["ffn_w_out"]
    ffw = []
    for i in range(2):
        rows_in, rows_out = slice(i * D, (i + 1) * D), slice(i * FFN // 4, (i + 1) * FFN // 4)
        ffw.append((jnp.concatenate([g_in[0, rows_in], g_in[1, rows_in]], axis=1),
                    jnp.concatenate([g_in[2, rows_in], g_in[3, rows_in]], axis=1),
                    jnp.concatenate([g_out[j, rows_out] for j in range(4)], axis=0)))
    h1 = _matmul(og, f_wo, add=h0, name="fox_out")
    h2, ffn0 = _ffn_fwd(h1, fn_[0:1], *ffw[0], "ffn0")

    hn2 = _rms_fwd(h2, an[1:2], "hgrn_norm")
    proj = _matmul(hn2, h_wi, name="hgrn_in")
    og1, states = _hgrn_fwd(proj, w["hgrn_lower_bounds"], w["hgrn_g_norm"], rmat, lmasks)
    h3 = _matmul(og1, h_wo, add=h2, name="hgrn_out")
    h4, ffn1 = _ffn_fwd(h3, fn_[1:2], *ffw[1], "ffn1")

    loss, dh, d_final = _loss_bwd(h4, tgt, w["final_norm"])

    dh, d_fn1, d_ffn1 = _ffn_bwd(dh, ffn1, fn_[1:2], *ffw[1], "ffn1")
    dog1 = _matmul(dh, h_wo, tb=True, out_dtype=BF16, name="hgrn_dog")
    d_h_wo = _matmul(og1, dh, ta=True, name="hgrn_dwo")
    dpq, dpz, dpi, dpg, d_lb, d_gg = _hgrn_bwd(proj, dog1, states, w["hgrn_lower_bounds"], w["hgrn_g_norm"],
                                               rmat, lmasks)
    dproj = jnp.concatenate([dpq, dpz, dpi, dpg], axis=1)
    dhn2 = _matmul(dproj, h_wi, tb=True, name="hgrn_dhn")
    d_h_wi = _matmul(hn2, dproj, ta=True, name="hgrn_dwi")
    dh, d_an1 = _rms_bwd(h2, dhn2, an[1:2], dh, "hgrn_norm_bwd")

    dh, d_fn0, d_ffn0 = _ffn_bwd(dh, ffn0, fn_[0:1], *ffw[0], "ffn0")
    n_in, n_out = 2 * FFN // 4, FFN // 4
    d_ffn = [d_ffn0, d_ffn1]
    late_grads = dict(
        hgrn_w_in=_to_shards("hgrn_w_in", d_h_wi[None]), hgrn_w_out=d_h_wo.reshape(4, D // 4, D),
        ffn_w_in=jnp.stack([jnp.concatenate([d[j // 2][:, (j % 2) * n_in:(j % 2 + 1) * n_in] for d in d_ffn], axis=0)
                            for j in range(4)]),
        ffn_w_out=jnp.stack([jnp.concatenate([d[2][j * n_out:(j + 1) * n_out] for d in d_ffn], axis=0)
                             for j in range(4)]))
    pair_late, send_late = _pair_sums([late_grads[n] for n in LATE_NAMES], "late")

    dog = _matmul(dh, f_wo, tb=True, out_dtype=BF16, name="fox_dog")
    d_f_wo = _matmul(og, dh, ta=True, name="fox_dwo")

    def by_head(a):
        return jnp.pad(a.transpose(1, 0, 2).reshape(L, FOX_H), ((0, 0), (0, LANES - FOX_H)))

    qb = _fox_pack_bias(qp, cq, by_head(lse2.transpose(0, 2, 1)))
    dop, dgate = _fox_pack_bwd(dog, o, gate)
    dqp, dk, dv, dck, recv_late = _fox_attn_bwd(qb, kp, vp, dop, send_late)
    (dq_raw, dk_raw, dc_q), (d_qg, d_kg) = _fox_unpack_bwd(q_raw, k_raw, dqp, dk, qg, kg, gmat)
    dflog, d_bf = _fox_cumsum_bwd(dc_q, by_head(dck), flog, bf)
    dproj0 = jnp.concatenate([dq_raw, dk_raw, dv, dgate, dflog.astype(BF16)], axis=1)
    f_wall = jnp.concatenate([f_wq, f_wk, f_wv, f_wg, f_wf], axis=1)
    dhn0 = _matmul(dproj0, f_wall, tb=True, name="fox_dhn")
    d_f_wall = _matmul(hn0, dproj0, ta=True, name="fox_dwi")
    d_f_wi = d_f_wall[:, :4 * D + FOX_H]
    dh, d_an0 = _rms_bwd(h0, dhn0, an[0:1], dh, "fox_norm_bwd")

    fox_grads = dict(fox_w_in=d_f_wi[None], fox_w_out=d_f_wo[None])
    pair_fox, send_fox = _pair_sums([_to_shards(n, fox_grads[n]) for n in FOX_NAMES], "fox")
    recv_fox = _chip_scatter(send_fox, "fox")
    halves = _chip_sums(pair_fox, recv_fox, "fox") + _chip_sums(pair_late, recv_late, "late")
    theirs = _sibling_exchange(halves)
    south = lax.axis_index("c") == 0
    big = {n: jnp.where(south, jnp.concatenate([m, t]), jnp.concatenate([t, m]))
           for n, m, t in zip(FOX_NAMES + LATE_NAMES, halves, theirs)}
    small = dict(attn_norm=jnp.concatenate([d_an0, d_an1]), ffn_norm=jnp.concatenate([d_fn0, d_fn1]),
                 final_norm=d_final, lb_raw=d_lb, q_gain=d_qg, k_gain=d_kg, b_f=d_bf,
                 g_gain=d_gg.reshape(1, D))
    return loss, dh, big, small


def _me():
    return lax.axis_index("x"), lax.axis_index("y"), lax.axis_index("c")


def _flip(v, bit):
    return 1 - v if bit else v


def _chip_allgather(arrs):
    n = len(arrs)

    def body(*refs):
        _gather_start(refs[:n], refs[n:2 * n], *refs[2 * n:])
        _gather_wait(refs[:n], refs[n:2 * n], *refs[2 * n:])

    return pl.pallas_call(
        body, name="chip_allgather", in_specs=[ANY] * n, out_specs=[ANY] * n,
        out_shape=[jax.ShapeDtypeStruct((4,) + a.shape, a.dtype) for a in arrs],
        scratch_shapes=_gather_sems(n),
    )(*arrs)


def _chip_peers():
    x, y, c = _me()
    return [(1 - x, y, c), (x, 1 - y, c), (1 - x, 1 - y, c)]


def _gather_sems(n):
    return [pltpu.SemaphoreType.DMA((n, 3)), pltpu.SemaphoreType.DMA((n, 3)), pltpu.SemaphoreType.DMA((n,))]


def _gather_copies(ins, outs, ssem, rsem, lsem, with_recvs):
    x, y, _ = _me()
    local, sends, recvs = [], [], []
    for a in range(len(ins)):
        local.append(pltpu.make_async_copy(ins[a], outs[a].at[2 * x + y], lsem.at[a]))
        for k, peer in enumerate(_chip_peers()):
            sends.append(pltpu.make_async_remote_copy(ins[a], outs[a].at[2 * x + y], ssem.at[a, k], rsem.at[a, k],
                                                      device_id=peer, device_id_type=MESH))
            if with_recvs:
                recvs.append(pltpu.make_async_remote_copy(ins[a], outs[a].at[2 * peer[0] + peer[1]], ssem.at[a, k],
                                                          rsem.at[a, k], device_id=peer, device_id_type=MESH))
    return local, sends, recvs


def _gather_start(ins, outs, ssem, rsem, lsem):
    local, sends, _ = _gather_copies(ins, outs, ssem, rsem, lsem, False)
    for cp in local + sends:
        cp.start()


def _gather_wait(ins, outs, ssem, rsem, lsem):
    local, sends, recvs = _gather_copies(ins, outs, ssem, rsem, lsem, True)
    for cp in local:
        cp.wait()
    for cp in sends:
        cp.wait_send()
    for cp in recvs:
        cp.wait_recv()


def _scatter_copies(ins, outs, ssem, rsem):
    copies = []
    for a in range(len(ins)):
        for k, peer in enumerate(_chip_peers()):
            copies.append(pltpu.make_async_remote_copy(ins[a].at[2 * peer[0] + peer[1]], outs[a].at[k], ssem.at[a, k],
                                                       rsem.at[a, k], device_id=peer, device_id_type=MESH))
    return copies


def _device_allgather(arr):
    def body(in_ref, out_ref, ssem, rsem, lsem):
        x, y, c = _me()
        me = 4 * x + 2 * y + c
        peers = [(_flip(x, k & 4), _flip(y, k & 2), _flip(c, k & 1)) for k in range(1, 8)]
        local = pltpu.make_async_copy(in_ref, out_ref.at[me], lsem)
        local.start()
        sends = []
        for k, peer in enumerate(peers):
            cp = pltpu.make_async_remote_copy(in_ref, out_ref.at[me], ssem.at[k], rsem.at[k],
                                              device_id=peer, device_id_type=MESH)
            cp.start()
            sends.append(cp)
        local.wait()
        for cp in sends:
            cp.wait_send()
        for k, peer in enumerate(peers):
            pltpu.make_async_remote_copy(in_ref, out_ref.at[4 * peer[0] + 2 * peer[1] + peer[2]], ssem.at[k],
                                         rsem.at[k], device_id=peer, device_id_type=MESH).wait_recv()

    return pl.pallas_call(
        body, name="device_allgather", in_specs=[ANY], out_specs=ANY,
        out_shape=jax.ShapeDtypeStruct((8,) + arr.shape, arr.dtype),
        scratch_shapes=[pltpu.SemaphoreType.DMA((7,)), pltpu.SemaphoreType.DMA((7,)), pltpu.SemaphoreType.DMA],
    )(arr)


def _sibling_send_other_half(arrs, tag):
    n = len(arrs)

    def body(*refs):
        ins, outs = refs[:n], refs[n:2 * n]
        ssem, rsem = refs[2 * n:]
        x, y, c = _me()
        cps = []
        for a in range(n):
            half = ins[a].shape[1] // 2
            src = ins[a].at[:, pl.ds((1 - c) * half, half), :]
            cp = pltpu.make_async_remote_copy(src, outs[a], ssem.at[a], rsem.at[a],
                                              device_id=(x, y, 1 - c), device_id_type=MESH)
            cp.start()
            cps.append(cp)
        for cp in cps:
            cp.wait()

    return pl.pallas_call(
        body, name=f"grad_sibling_swap_{tag}", in_specs=[ANY] * n, out_specs=[ANY] * n,
        out_shape=[jax.ShapeDtypeStruct((4, a.shape[1] // 2, a.shape[2]), a.dtype) for a in arrs],
        scratch_shapes=[pltpu.SemaphoreType.DMA((n,)), pltpu.SemaphoreType.DMA((n,))],
    )(*arrs)


def _chip_scatter(arrs, tag):
    n = len(arrs)

    def body(*refs):
        cps = _scatter_copies(refs[:n], refs[n:2 * n], *refs[2 * n:])
        for cp in cps:
            cp.start()
        for cp in cps:
            cp.wait()

    return pl.pallas_call(
        body, name=f"grad_chip_scatter_{tag}", in_specs=[ANY] * n, out_specs=[ANY] * n,
        out_shape=[jax.ShapeDtypeStruct((3,) + a.shape[1:], a.dtype) for a in arrs],
        scratch_shapes=[pltpu.SemaphoreType.DMA((n, 3)), pltpu.SemaphoreType.DMA((n, 3))],
    )(*arrs)


def _sibling_exchange(arrs):
    n = len(arrs)

    def body(*refs):
        ins, outs = refs[:n], refs[n:2 * n]
        ssem, rsem = refs[2 * n:]
        x, y, c = _me()
        cps = [pltpu.make_async_remote_copy(ins[a], outs[a], ssem.at[a], rsem.at[a], device_id=(x, y, 1 - c),
                                            device_id_type=MESH) for a in range(n)]
        for cp in cps:
            cp.start()
        for cp in cps:
            cp.wait()

    return pl.pallas_call(
        body, name="grad_sibling_exchange", in_specs=[ANY] * n, out_specs=[ANY] * n,
        out_shape=[jax.ShapeDtypeStruct(a.shape, a.dtype) for a in arrs],
        scratch_shapes=[pltpu.SemaphoreType.DMA((n,)), pltpu.SemaphoreType.DMA((n,))],
    )(*arrs)


def _pair_sums(grads, tag):
    c = lax.axis_index("c")
    got = _sibling_send_other_half(grads, tag)
    pair = []
    for i, (g, t) in enumerate(zip(grads, got)):
        half = g.shape[1] // 2
        mine = lax.dynamic_slice_in_dim(g, c * half, half, axis=1)
        pair.append(_add([mine.reshape(4 * half, -1), t.reshape(4 * half, -1)], f"grad_pair_add_{tag}{i}")
                    .reshape(t.shape))
    return pair, [p.astype(BF16) for p in pair]


def _chip_sums(pair, recv, tag):
    x, y, _ = _me()
    out = []
    for i, (p, r) in enumerate(zip(pair, recv)):
        own = lax.dynamic_index_in_dim(p, 2 * x + y, axis=0, keepdims=False)
        out.append(_add([own, r[0], r[1], r[2]], f"grad_chip_add_{tag}{i}"))
    return out


SMALL_ROWS = 32


def _small_finalize(gathered, hlb, fold64, fold128):
    def body(g_ref, hlb_ref, f64_ref, f128_ref, rows_ref, qk_ref, gg_ref, lb_ref):
        tot = g_ref[0]
        for d in range(1, 8):
            tot = tot + g_ref[d]
        rows_ref[...] = tot
        qk_ref[...] = jnp.dot(rows_ref[6:8, :], f64_ref[...], precision=HIGHEST, preferred_element_type=F32)
        gg_ref[...] = jnp.dot(rows_ref[9:10, :], f128_ref[...], precision=HIGHEST, preferred_element_type=F32)
        h0, h1 = hlb_ref[0:1, :], hlb_ref[1:2, :]
        mx = jnp.maximum(h0, h1)
        e0, e1 = jnp.exp(h0 - mx), jnp.exp(h1 - mx)
        lb = e1 / (e0 + e1)
        d1 = rows_ref[5:6, :] * lb * (1.0 - lb)
        lb_ref[...] = jnp.where(_iota((2, 1), 0) == 0, -d1, d1)

    return pl.pallas_call(
        body, name="small_finalize",
        out_shape=[jax.ShapeDtypeStruct((SMALL_ROWS, D), F32), jax.ShapeDtypeStruct((2, FOX_DH), F32),
                   jax.ShapeDtypeStruct((1, HG_D), F32), jax.ShapeDtypeStruct((2, D), F32)],
    )(gathered, hlb, fold64, fold128)


FOX_NAMES = ("fox_w_in", "fox_w_out")
LATE_NAMES = ("hgrn_w_in", "hgrn_w_out", "ffn_w_in", "ffn_w_out")
BIG_NAMES = FOX_NAMES + LATE_NAMES
GATHER_LATE = ("fox_w_out",) + LATE_NAMES
COL_SHARDED = ("fox_w_in", "hgrn_w_in", "ffn_w_in")


def _shard2d(name, a):
    return a.reshape(-1, a.shape[-1])


def _to_shards(name, g):
    layers = g.shape[0]
    if name in COL_SHARDED:
        k, n = g.shape[1], g.shape[2] // 4
        return g.reshape(layers, k, 4, n).transpose(2, 0, 1, 3).reshape(4, layers * k, n)
    r = g.shape[1] // 4
    return g.reshape(layers, 4, r, g.shape[2]).transpose(1, 0, 2, 3).reshape(4, layers * r, g.shape[2])


def kernel(x, meta_tokens, attn_norm, ffn_norm, final_norm, fox_w_in, fox_b_f, fox_q_norm, fox_k_norm, fox_w_out, hgrn_w_in, hgrn_lower_bounds, hgrn_g_norm, hgrn_w_out, ffn_w_in, ffn_w_out, loss_target, m_meta_tokens, m_attn_norm, m_ffn_norm, m_final_norm, m_fox_w_in, m_fox_b_f, m_fox_q_norm, m_fox_k_norm, m_fox_w_out, m_hgrn_w_in, m_hgrn_lower_bounds, m_hgrn_g_norm, m_hgrn_w_out, m_ffn_w_in, m_ffn_w_out, v_meta_tokens, v_attn_norm, v_ffn_norm, v_final_norm, v_fox_w_in, v_fox_b_f, v_fox_q_norm, v_fox_k_norm, v_fox_w_out, v_hgrn_w_in, v_hgrn_lower_bounds, v_hgrn_g_norm, v_hgrn_w_out, v_ffn_w_in, v_ffn_w_out):
    params = dict(meta_tokens=meta_tokens, attn_norm=attn_norm, ffn_norm=ffn_norm, final_norm=final_norm,
                  fox_w_in=fox_w_in, fox_b_f=fox_b_f, fox_q_norm=fox_q_norm, fox_k_norm=fox_k_norm,
                  fox_w_out=fox_w_out, hgrn_w_in=hgrn_w_in, hgrn_lower_bounds=hgrn_lower_bounds,
                  hgrn_g_norm=hgrn_g_norm, hgrn_w_out=hgrn_w_out, ffn_w_in=ffn_w_in, ffn_w_out=ffn_w_out)
    mom_m = dict(meta_tokens=m_meta_tokens, attn_norm=m_attn_norm, ffn_norm=m_ffn_norm, final_norm=m_final_norm,
                 fox_w_in=m_fox_w_in, fox_b_f=m_fox_b_f, fox_q_norm=m_fox_q_norm, fox_k_norm=m_fox_k_norm,
                 fox_w_out=m_fox_w_out, hgrn_w_in=m_hgrn_w_in, hgrn_lower_bounds=m_hgrn_lower_bounds,
                 hgrn_g_norm=m_hgrn_g_norm, hgrn_w_out=m_hgrn_w_out, ffn_w_in=m_ffn_w_in, ffn_w_out=m_ffn_w_out)
    mom_v = dict(meta_tokens=v_meta_tokens, attn_norm=v_attn_norm, ffn_norm=v_ffn_norm, final_norm=v_final_norm,
                 fox_w_in=v_fox_w_in, fox_b_f=v_fox_b_f, fox_q_norm=v_fox_q_norm, fox_k_norm=v_fox_k_norm,
                 fox_w_out=v_fox_w_out, hgrn_w_in=v_hgrn_w_in, hgrn_lower_bounds=v_hgrn_lower_bounds,
                 hgrn_g_norm=v_hgrn_g_norm, hgrn_w_out=v_hgrn_w_out, ffn_w_in=v_ffn_w_in, ffn_w_out=v_ffn_w_out)
    names = list(params)
    xi, yi, _ = _me()

    shards = {n: _shard2d(n, params[n]).astype(BF16) for n in BIG_NAMES}
    w_in_g, meta_g = _chip_allgather([shards["fox_w_in"], meta_tokens])
    w = dict(fox_w_in=jnp.concatenate(list(w_in_g), axis=1))
    meta_full = jnp.concatenate(list(meta_g), axis=1)
    w.update(attn_norm=attn_norm, ffn_norm=ffn_norm, final_norm=final_norm.reshape(1, D), fox_b_f=fox_b_f,
             fox_q_norm=fox_q_norm, fox_k_norm=fox_k_norm, hgrn_lower_bounds=hgrn_lower_bounds,
             hgrn_g_norm=hgrn_g_norm)

    h0 = jnp.concatenate([jnp.zeros((ROW0, D), F32), meta_full, x[0]], axis=0)
    tgt = jnp.concatenate([jnp.zeros((PAD, D), F32), loss_target[0]], axis=0)
    loss, dh0, big, small = _local_step(h0, tgt, w, {n: shards[n] for n in GATHER_LATE})
    loss = lax.psum(loss, ("x", "y", "c"))
    grad_x = dh0[PAD:][None]
    grads = {n: big[n].reshape(params[n].shape) for n in BIG_NAMES}

    rows = jnp.concatenate([small["attn_norm"], small["ffn_norm"], small["final_norm"], small["lb_raw"],
                            small["q_gain"], small["k_gain"],
                            jnp.pad(small["b_f"], ((0, 0), (0, D - LANES))), small["g_gain"],
                            dh0[ROW0:PAD], jnp.zeros((SMALL_ROWS - 10 - N_META, D), F32)], axis=0)
    allrows = _device_allgather(rows)
    fold64 = jnp.asarray(np.tile(np.eye(FOX_DH, dtype=np.float32), (FOX_H, 1)))
    fold128 = jnp.asarray(np.tile(np.eye(HG_D, dtype=np.float32), (HG_H, 1)))
    tot, qk, gg, dlb = _small_finalize(allrows, hgrn_lower_bounds, fold64, fold128)
    grads.update(attn_norm=tot[0:2], ffn_norm=tot[2:4], final_norm=tot[4], hgrn_lower_bounds=dlb,
                 fox_q_norm=qk[0:1], fox_k_norm=qk[1:2], fox_b_f=tot[8:9, :FOX_H], hgrn_g_norm=gg,
                 meta_tokens=lax.dynamic_slice_in_dim(tot[10:10 + N_META], (2 * xi + yi) * (D // 4), D // 4, axis=1))

    delta, new_m, new_v = {}, {}, {}
    for n in BIG_NAMES + ("meta_tokens",):
        d_, m_, v_ = _adamw(_shard2d(n, params[n]), _shard2d(n, grads[n]), _shard2d(n, mom_m[n]),
                            _shard2d(n, mom_v[n]), f"adamw_{n}")
        delta[n], new_m[n], new_v[n] = (t.reshape(params[n].shape) for t in (d_, m_, v_))
    small_names = [n for n in names if n not in BIG_NAMES and n != "meta_tokens"]

    def pack(d):
        return jnp.concatenate([jnp.pad(d[n].reshape(-1, d[n].shape[-1]), ((0, 0), (0, D - d[n].shape[-1])))
                                for n in small_names], axis=0)

    packed = [pack(t) for t in (params, grads, mom_m, mom_v)]
    n_rows = packed[0].shape[0]
    packed = [jnp.pad(t, ((0, 16 - n_rows), (0, 0))) for t in packed]
    res = _adamw(*packed, "adamw_small")
    r0 = 0
    for n in small_names:
        nr = params[n].reshape(-1, params[n].shape[-1]).shape[0]
        for dst, src in zip((delta, new_m, new_v), res):
            dst[n] = src[r0:r0 + nr, :params[n].shape[-1]].reshape(params[n].shape)
        r0 += nr

    return (loss, grad_x, *[grads[n] for n in names], *[delta[n] for n in names],
            *[new_m[n] for n in names], *[new_v[n] for n in names])
```

```python
import functools

import numpy as np
import jax
import jax.numpy as jnp
from jax import lax
from jax.experimental import pallas as pl
from jax.experimental.pallas import tpu as pltpu

F32, BF16 = jnp.float32, jnp.bfloat16
HIGHEST = lax.Precision.HIGHEST

D = 1024
N_META = 16
PAD = 128
ROW0 = PAD - N_META
FOX_H, FOX_DH = 16, 64
HG_H, HG_D = 8, 128
HG_C = 128
HG_LEV = 7
HG_HPS = 4
FFN = 2816
EPS = 1e-6
BIG = 1e30
LOG2E = 1.4426950408889634
LANES = 128
MXU_N = 256
VMEM_LIMIT = 48 * 1024 * 1024
ROW_TILES = (640, 512, 384, 320, 256, 128, 64, 32, 16, 8)
ATTN_TILES = (640, 512, 256, 128)
FOX_HPS_FWD = 8

ADAM_LR, ADAM_B1, ADAM_B2, ADAM_EPS, ADAM_WD, ADAM_STEP = 0.001, 0.9, 0.999, 1e-08, 0.01, 10

MESH = pl.DeviceIdType.MESH
ANY = pl.BlockSpec(memory_space=pl.ANY)
NT = (((1,), (1,)), ((), ()))
TN = (((0,), (0,)), ((), ()))


def _tile(n, cands=ROW_TILES, cap=None):
    for c in cands:
        if n % c == 0 and (cap is None or c <= cap):
            return c
    return n


def _cparams(sem):
    return pltpu.CompilerParams(dimension_semantics=sem, vmem_limit_bytes=VMEM_LIMIT)


def _sigmoid(x):
    return jax.nn.sigmoid(x)


def _log_sigmoid(x):
    return jnp.minimum(x, 0.0) - jnp.log(1.0 + jnp.exp(-jnp.abs(x)))


def _iota(shape, dim):
    return lax.broadcasted_iota(jnp.int32, shape, dim)


def _matmul(a, b, *, ta=False, tb=False, out_dtype=F32, add=None, name):
    if ta:
        kdim, m = a.shape
    else:
        m, kdim = a.shape
    n = b.shape[0] if tb else b.shape[1]
    if ta:
        tm = m if m <= 1024 else _tile(m, (1408, 1024, 512, 256, 128))
        tk = _tile(kdim)
    else:
        tm = _tile(m)
        tk = kdim if kdim <= 4096 else _tile(kdim, (2048, 1024, 512))
    tn = n if n <= 1024 else _tile(n, (1408, 1024, 512, 256, 128))
    nk = kdim // tk
    dn = (((0 if ta else 1,), (1 if tb else 0,)), ((), ()))

    def body(*refs):
        if add is None:
            a_ref, b_ref, o_ref, acc_ref = refs
        else:
            a_ref, b_ref, add_ref, o_ref, acc_ref = refs
        k = pl.program_id(2)

        @pl.when(k == 0)
        def _():
            acc_ref[...] = jnp.zeros_like(acc_ref)

        acc_ref[...] += lax.dot_general(a_ref[...].astype(BF16), b_ref[...].astype(BF16), dn,
                                        preferred_element_type=F32)

        @pl.when(k == nk - 1)
        def _():
            r = acc_ref[...]
            if add is not None:
                r = r + add_ref[...].astype(F32)
            o_ref[...] = r.astype(o_ref.dtype)

    a_spec = pl.BlockSpec((tk, tm), lambda j, i, k: (k, i)) if ta else pl.BlockSpec((tm, tk), lambda j, i, k: (i, k))
    b_spec = pl.BlockSpec((tn, tk), lambda j, i, k: (j, k)) if tb else pl.BlockSpec((tk, tn), lambda j, i, k: (k, j))
    o_spec = pl.BlockSpec((tm, tn), lambda j, i, k: (i, j))
    ins, specs = [a, b], [a_spec, b_spec]
    if add is not None:
        ins.append(add)
        specs.append(o_spec)
    return pl.pallas_call(
        body, name=name, grid=(n // tn, m // tm, nk), in_specs=specs, out_specs=o_spec,
        out_shape=jax.ShapeDtypeStruct((m, n), out_dtype),
        scratch_shapes=[pltpu.VMEM((tm, tn), F32)],
        compiler_params=_cparams(("parallel", "parallel", "arbitrary")),
    )(*ins)


def _rowwise(fn, ins, bcast, outs, accs, *, name, reverse=False, carry=None, as_refs=False):
    rows = ins[0].shape[0]
    per_row = sum(x.shape[1] * x.dtype.itemsize for x in ins) + sum(c * jnp.dtype(d).itemsize for c, d in outs)
    tm = _tile(rows, cap=max(8, (10 * 1024 * 1024) // per_row))
    n = rows // tm
    n_in, n_b, n_o, n_a = len(ins), len(bcast), len(outs), len(accs)

    def body(*refs):
        in_refs = refs[:n_in]
        b_refs = refs[n_in:n_in + n_b]
        o_refs = refs[n_in + n_b:n_in + n_b + n_o]
        a_refs = refs[n_in + n_b + n_o:n_in + n_b + n_o + n_a]
        c_refs = refs[n_in + n_b + n_o + n_a:]
        i = pl.program_id(0)
        blk = (n - 1 - i) if reverse else i
        if c_refs:
            @pl.when(i == 0)
            def _():
                c_refs[0][...] = jnp.zeros_like(c_refs[0])
        args = (list(in_refs) if as_refs else [r[...] for r in in_refs], [r[...] for r in b_refs])
        o_vals, a_vals = fn(blk * tm, *args, *c_refs)
        for r, v in zip(o_refs, o_vals):
            r[...] = v.astype(r.dtype)
        if n_a:
            @pl.when(i == 0)
            def _():
                for r in a_refs:
                    r[...] = jnp.zeros_like(r)
            for r, v in zip(a_refs, a_vals):
                r[...] += v

    def row_map(i):
        return ((n - 1 - i) if reverse else i, 0)

    in_specs = [pl.BlockSpec((tm, x.shape[1]), row_map) for x in ins]
    in_specs += [pl.BlockSpec(x.shape, lambda i, nd=x.ndim: (0,) * nd) for x in bcast]
    out_specs = [pl.BlockSpec((tm, c), row_map) for c, _ in outs]
    out_specs += [pl.BlockSpec(s, lambda i: (0, 0)) for s in accs]
    out_shape = [jax.ShapeDtypeStruct((rows, c), d) for c, d in outs]
    out_shape += [jax.ShapeDtypeStruct(s, F32) for s in accs]
    res = pl.pallas_call(
        body, name=name, grid=(n,), in_specs=in_specs, out_specs=out_specs, out_shape=out_shape,
        scratch_shapes=[pltpu.VMEM(carry, F32)] if carry else [],
        compiler_params=_cparams(("arbitrary",)),
    )(*ins, *bcast)
    return res[:n_o], res[n_o:]


def _row_ids(row0, tm):
    return row0 + _iota((tm, 1), 0)


def _rms_fwd(x, gain, name):
    def fn(row0, ins, bc):
        (xv,), (g,) = ins, bc
        r = lax.rsqrt(jnp.mean(xv * xv, axis=-1, keepdims=True) + EPS)
        return [xv * r * g], []
    return _rowwise(fn, [x], [gain], [(D, BF16)], [], name=name)[0][0]


def _rms_bwd(x, dxn, gain, dh_up, name):
    def fn(row0, ins, bc):
        xv, dy, up = ins
        (g,) = bc
        dy = dy.astype(F32)
        r = lax.rsqrt(jnp.mean(xv * xv, axis=-1, keepdims=True) + EPS)
        xh = xv * r
        dxh = dy * g
        dx = r * (dxh - xh * jnp.mean(dxh * xh, axis=-1, keepdims=True))
        keep = _row_ids(row0, xv.shape[0]) >= ROW0
        return [jnp.where(keep, up + dx, 0.0)], [jnp.sum(dy * xh, axis=0, keepdims=True)]
    (dh,), (dgain,) = _rowwise(fn, [x, dxn, dh_up], [gain], [(D, F32)], [(1, D)], name=name)
    return dh, dgain


def _loss_bwd(h, tgt, gain):
    tm = PAD
    n = h.shape[0] // tm

    def body(x_ref, t_ref, g_ref, dx_ref, loss_ref, dg_ref):
        i = pl.program_id(0)

        @pl.when(i == 0)
        def _():
            loss_ref[...] = jnp.zeros_like(loss_ref)
            dg_ref[...] = jnp.zeros_like(dg_ref)

        xv, g = x_ref[...], g_ref[...]
        r = lax.rsqrt(jnp.mean(xv * xv, axis=-1, keepdims=True) + EPS)
        xh = xv * r
        err = jnp.where(i >= 1, xh * g - t_ref[...], 0.0)
        per_row = jnp.mean(err * err, axis=-1, keepdims=True)
        loss_ref[...] += jnp.broadcast_to(0.5 * jnp.sum(per_row, axis=0, keepdims=True), (1, LANES))
        dy = err * (1.0 / D)
        dxh = dy * g
        dx_ref[...] = r * (dxh - xh * jnp.mean(dxh * xh, axis=-1, keepdims=True))
        dg_ref[...] += jnp.sum(dy * xh, axis=0, keepdims=True)

    dh, loss, dgain = pl.pallas_call(
        body, name="loss_bwd", grid=(n,),
        in_specs=[pl.BlockSpec((tm, D), lambda i: (i, 0)), pl.BlockSpec((tm, D), lambda i: (jnp.maximum(i - 1, 0), 0)),
                  pl.BlockSpec((1, D), lambda i: (0, 0))],
        out_specs=[pl.BlockSpec((tm, D), lambda i: (i, 0)), pl.BlockSpec((1, LANES), lambda i: (0, 0)),
                   pl.BlockSpec((1, D), lambda i: (0, 0))],
        out_shape=[jax.ShapeDtypeStruct(h.shape, F32), jax.ShapeDtypeStruct((1, LANES), F32),
                   jax.ShapeDtypeStruct((1, D), F32)],
        compiler_params=_cparams(("arbitrary",)),
    )(h, tgt, gain)
    return loss[0, 0], dh, dgain


FFN_TILES = dict(rows=(320, 256, 128), cols=(1408, 1024, 512, 256, 128))


def _ffn_in(hn, wg, wu, name):
    m, kdim = hn.shape
    n = wg.shape[1]
    tm, tn = _tile(m, FFN_TILES["rows"]), _tile(n, FFN_TILES["cols"])

    def body(a_ref, wg_ref, wu_ref, g_ref, u_ref, act_ref):
        a = a_ref[...]
        g = jnp.dot(a, wg_ref[...], preferred_element_type=F32)
        u = jnp.dot(a, wu_ref[...], preferred_element_type=F32)
        g_ref[...] = g.astype(g_ref.dtype)
        u_ref[...] = u.astype(u_ref.dtype)
        act_ref[...] = (g * _sigmoid(g) * u).astype(act_ref.dtype)

    wspec = pl.BlockSpec((kdim, tn), lambda j, i: (0, j))
    ospec = pl.BlockSpec((tm, tn), lambda j, i: (i, j))
    return pl.pallas_call(
        body, name=name, grid=(n // tn, m // tm),
        in_specs=[pl.BlockSpec((tm, kdim), lambda j, i: (i, 0)), wspec, wspec], out_specs=[ospec] * 3,
        out_shape=[jax.ShapeDtypeStruct((m, n), BF16)] * 3,
        compiler_params=_cparams(("parallel", "parallel")),
    )(hn, wg, wu)


def _ffn_dact(dh, wo, g, u, name):
    m, kdim = dh.shape
    n = wo.shape[0]
    tm, tn = _tile(m, FFN_TILES["rows"]), _tile(n, FFN_TILES["cols"])

    def body(a_ref, w_ref, g_ref, u_ref, dg_ref, du_ref):
        da = lax.dot_general(a_ref[...].astype(BF16), w_ref[...], NT, preferred_element_type=F32)
        gv, uv = g_ref[...].astype(F32), u_ref[...].astype(F32)
        s = _sigmoid(gv)
        dg_ref[...] = (da * uv * (s * (1.0 + gv * (1.0 - s)))).astype(dg_ref.dtype)
        du_ref[...] = (da * gv * s).astype(du_ref.dtype)

    ospec = pl.BlockSpec((tm, tn), lambda j, i: (i, j))
    return pl.pallas_call(
        body, name=name, grid=(n // tn, m // tm),
        in_specs=[pl.BlockSpec((tm, kdim), lambda j, i: (i, 0)), pl.BlockSpec((tn, kdim), lambda j, i: (j, 0)),
                  ospec, ospec],
        out_specs=[ospec] * 2, out_shape=[jax.ShapeDtypeStruct((m, n), BF16)] * 2,
        compiler_params=_cparams(("parallel", "parallel")),
    )(dh, wo, g, u)


def _adamw_math(wv, gv, mv, vv):
    mn = ADAM_B1 * mv + (1.0 - ADAM_B1) * gv
    vn = ADAM_B2 * vv + (1.0 - ADAM_B2) * (gv * gv)
    m_hat = mn / (1.0 - ADAM_B1 ** ADAM_STEP)
    v_hat = vn / (1.0 - ADAM_B2 ** ADAM_STEP)
    return -ADAM_LR * (m_hat / (jnp.sqrt(v_hat) + ADAM_EPS) + ADAM_WD * wv), mn, vn


def _adamw(w, g, m, v, name):
    def fn(row0, ins, bc):
        return list(_adamw_math(*ins)), []
    c = w.shape[1]
    return _rowwise(fn, [w, g, m, v], [], [(c, F32)] * 3, [], name=name)[0]


def _adamw_halves(w, mine, theirs, m, v, name):
    rows, cols = w.shape
    half = rows // 2
    tm = _tile(half, cap=(10 * 1024 * 1024) // (9 * 4 * cols))
    nb = half // tm

    def body(c_ref, w_ref, g1_ref, g2_ref, m_ref, v_ref, g_out, d_out, m_out, v_out):
        own = (pl.program_id(0) // nb) == c_ref[0]
        g = jnp.where(own, g1_ref[...], g2_ref[...])
        delta, mn, vn = _adamw_math(w_ref[...], g, m_ref[...], v_ref[...])
        g_out[...] = g
        d_out[...] = delta
        m_out[...] = mn
        v_out[...] = vn

    full = pl.BlockSpec((tm, cols), lambda i, c: (i, 0))
    part = pl.BlockSpec((tm, cols), lambda i, c: (lax.rem(i, nb), 0))
    return pl.pallas_call(
        body, name=name,
        grid_spec=pltpu.PrefetchScalarGridSpec(num_scalar_prefetch=1, grid=(2 * nb,),
                                               in_specs=[full, part, part, full, full], out_specs=[full] * 4),
        out_shape=[jax.ShapeDtypeStruct((rows, cols), F32)] * 4,
        compiler_params=_cparams(("parallel",)),
    )(_mesh_scalar(lax.axis_index("c")), w, mine, theirs, m, v)


def _head_sum(x, gmat):
    hi = x.astype(BF16)
    lo = (x - hi.astype(F32)).astype(BF16)
    w = gmat.shape[0]
    return jnp.concatenate(
        [jnp.dot(hi[:, b:b + w], gmat, preferred_element_type=F32) + jnp.dot(lo[:, b:b + w], gmat,
                                                                             preferred_element_type=F32)
         for b in range(0, x.shape[1], w)], axis=1)


def _split3(x):
    hi = x.astype(BF16).astype(F32)
    r = x - hi
    mid = r.astype(BF16).astype(F32)
    return hi, mid, r - mid


def _extra_base(hh):
    return FOX_DH * (1 - hh)


def _data_mask(hh):
    lane = _iota((1, LANES), 1)
    return (lane >= FOX_DH * hh) & (lane < FOX_DH * (hh + 1))


def _with_extras(data, hh, vals):
    lane = _iota((1, LANES), 1)
    x = jnp.zeros_like(data)
    for e, v in enumerate(vals):
        x = jnp.where(lane == _extra_base(hh) + e, v, x)
    return jnp.where(_data_mask(hh), data, x)


def _fox_pack_fwd(q_raw, k_raw, v, cq, ck, qg, kg, gmat):
    scale2 = FOX_DH ** -0.5 * LOG2E

    def fn(row0, refs, bc):
        q_ref, k_ref, v_ref, cq_ref, ck_ref = refs
        g_q, g_k, gm = bc
        qv, kv = q_ref[...], k_ref[...]
        qn = qv * lax.rsqrt(_head_sum(qv * qv, gm) * (1.0 / FOX_DH) + EPS) * (g_q * scale2)
        kn = kv * lax.rsqrt(_head_sum(kv * kv, gm) * (1.0 / FOX_DH) + EPS) * g_k
        qs, ks, vs = [], [], []
        for h in range(FOX_H):
            p, hh = divmod(h, 2)
            sl = slice(p * LANES, (p + 1) * LANES)
            cq3 = _split3(cq_ref[:, h:h + 1] * LOG2E)
            ck3 = _split3(ck_ref[:, h:h + 1] * (-LOG2E))
            qs.append(_with_extras(qn[:, sl], hh, [*cq3, 1.0, 1.0, 1.0]))
            ks.append(_with_extras(kn[:, sl], hh, [1.0, 1.0, 1.0, *ck3]))
            vs.append(_with_extras(v_ref[:, sl].astype(F32), hh, [1.0, 1.0]))
        return [jnp.concatenate(qs, axis=1), jnp.concatenate(ks, axis=1), jnp.concatenate(vs, axis=1)], []

    w = FOX_H * LANES
    return _rowwise(fn, [q_raw, k_raw, v, cq, ck], [qg, kg, gmat], [(w, BF16)] * 3, [], name="fox_pack_fwd",
                    as_refs=True)[0]


def _fox_pack_bias(qp, cq, lse2):
    def fn(row0, refs, bc):
        q_ref, cq_ref, lse_ref = refs
        lane = _iota((1, LANES), 1)
        outs = []
        for h in range(FOX_H):
            blk = q_ref[:, h * LANES:(h + 1) * LANES].astype(F32)
            for e, part in enumerate(_split3(cq_ref[:, h:h + 1] * LOG2E - lse_ref[:, h:h + 1])):
                blk = jnp.where(lane == _extra_base(h % 2) + e, part, blk)
            outs.append(blk)
        return [jnp.concatenate(outs, axis=1)], []
    return _rowwise(fn, [qp, cq, lse2], [], [(FOX_H * LANES, BF16)], [], name="fox_pack_bias", as_refs=True)[0][0]


def _fox_pack_bwd(dog, o, gate):
    def fn(row0, refs, bc):
        d_ref, o_ref, g_ref = refs
        dos, dgs = [], []
        for p in range(FOX_H // 2):
            sl = slice(p * LANES, (p + 1) * LANES)
            dv, ov, gv = (r[:, sl].astype(F32) for r in (d_ref, o_ref, g_ref))
            s = _sigmoid(gv)
            do = dv * s
            dgs.append(dv * ov * s * (1.0 - s))
            od = ov * do
            for hh in range(2):
                delta = jnp.sum(jnp.where(_data_mask(hh), od, 0.0), axis=-1, keepdims=True)
                hi = delta.astype(BF16).astype(F32)
                dos.append(_with_extras(do, hh, [-hi, hi - delta]))
        return [jnp.concatenate(dos, axis=1), jnp.concatenate(dgs, axis=1)], []
    return _rowwise(fn, [dog, o, gate], [], [(FOX_H * LANES, BF16), (D, BF16)], [], name="fox_pack_bwd",
                    as_refs=True)[0]


def _fox_unpack_bwd(q_raw, k_raw, dqp, dk, qg, kg, gmat):
    scale = FOX_DH ** -0.5

    def fn(row0, refs, bc):
        q_ref, k_ref, dq_ref, dk_ref = refs
        g_q, g_k, gm = bc
        lane = _iota((1, LANES), 1)
        dqs = []
        dcq = jnp.zeros((q_ref.shape[0], LANES), F32)
        for p in range(FOX_H // 2):
            even = dq_ref[:, (2 * p) * LANES:(2 * p + 1) * LANES]
            odd = dq_ref[:, (2 * p + 1) * LANES:(2 * p + 2) * LANES]
            dqs.append(jnp.where(_data_mask(0), even, odd) * scale)
            for hh in range(2):
                col = (2 * p + hh) * LANES + _extra_base(hh)
                dcq = jnp.where(lane == 2 * p + hh, dq_ref[:, col:col + 1], dcq)
        outs, accs = [], []
        for xv, dy, g in ((q_ref[...], jnp.concatenate(dqs, axis=1), g_q), (k_ref[...], dk_ref[...] * (1.0 / LOG2E), g_k)):
            r = lax.rsqrt(_head_sum(xv * xv, gm) * (1.0 / FOX_DH) + EPS)
            xh = xv * r
            dxh = dy * g
            outs.append(r * (dxh - xh * (_head_sum(dxh * xh, gm) * (1.0 / FOX_DH))))
            accs.append(jnp.sum(dy * xh, axis=0, keepdims=True))
        return outs + [dcq], accs
    return _rowwise(fn, [q_raw, k_raw, dqp, dk], [qg, kg, gmat], [(D, BF16), (D, BF16), (LANES, F32)],
                    [(1, D), (1, D)], name="fox_unpack_bwd", as_refs=True)


def _fox_cumsum_fwd(flog, bf):
    def fn(row0, ins, bc, carry):
        (f,), (b,) = ins, bc
        tm = f.shape[0]
        keep = _row_ids(row0, tm) >= ROW0
        lf = jnp.where(keep, _log_sigmoid(f + b), 0.0)
        tri = (_iota((tm, tm), 0) >= _iota((tm, tm), 1)).astype(F32)
        c = jnp.dot(tri, lf, precision=HIGHEST, preferred_element_type=F32) + carry[...]
        carry[...] = carry[...] + jnp.sum(lf, axis=0, keepdims=True)
        return [c, jnp.where(keep, c, BIG)], []
    return _rowwise(fn, [flog], [bf], [(LANES, F32), (LANES, F32)], [], name="fox_cumsum_fwd",
                    carry=(1, LANES))[0]


def _fox_cumsum_bwd(dc_q, dc_k, flog, bf):
    def fn(row0, ins, bc, carry):
        (dq, dk, f), (b,) = ins, bc
        d = dq + dk
        tm = f.shape[0]
        keep = _row_ids(row0, tm) >= ROW0
        triu = (_iota((tm, tm), 0) <= _iota((tm, tm), 1)).astype(F32)
        dlf = jnp.dot(triu, d, precision=HIGHEST, preferred_element_type=F32) + carry[...]
        carry[...] = carry[...] + jnp.sum(d, axis=0, keepdims=True)
        dfl = jnp.where(keep, dlf * _sigmoid(-(f + b)), 0.0)
        return [dfl], [jnp.sum(dfl, axis=0, keepdims=True)]
    (dflog,), (dbf,) = _rowwise(fn, [dc_q, dc_k, flog], [bf], [(LANES, F32)], [(1, LANES)], name="fox_cumsum_bwd",
                                reverse=True, carry=(1, LANES))
    return dflog, dbf


def _causal_steps(n, key_major):
    if key_major:
        pairs = [(i, j) for j in range(n) for i in range(j, n)]
    else:
        pairs = [(i, j) for i in range(n) for j in range(i + 1)]
    return (jnp.asarray(np.array([p[0] for p in pairs], np.int32)),
            jnp.asarray(np.array([p[1] for p in pairs], np.int32)))


def _fox_attn_fwd(qp, kp, vp, gate, shards):
    L = qp.shape[0]
    t = _tile(L, ATTN_TILES)
    n = L // t
    hps = FOX_HPS_FWD
    P = FOX_H // hps
    it, jt = _causal_steps(n, False)
    n_steps = it.shape[0]
    ns = len(shards)

    def body(it_ref, jt_ref, q_ref, k_ref, v_ref, g_ref, *rest):
        sh_in, (o_ref, og_ref, lse_ref), sh_out = rest[:ns], rest[ns:ns + 3], rest[ns + 3:2 * ns + 3]
        m_sc, acc, ssem, rsem, lsem = rest[2 * ns + 3:]
        step = pl.program_id(1)
        i, j = it_ref[step], jt_ref[step]
        first = (pl.program_id(0) == 0) & (step == 0)
        last = (pl.program_id(0) == P - 1) & (step == n_steps - 1)

        @pl.when(first)
        def _():
            _gather_start(sh_in, sh_out, ssem, rsem, lsem)

        @pl.when(j == 0)
        def _():
            m_sc[...] = jnp.full_like(m_sc, -3.0e38)
            acc[...] = jnp.zeros_like(acc)

        def update(masked):
            for hh in range(hps):
                sl = slice(hh * LANES, (hh + 1) * LANES)
                s2 = lax.dot_general(k_ref[:, sl], q_ref[:, sl], NT, preferred_element_type=F32)
                if masked:
                    s2 = jnp.where(_iota((t, t), 0) <= _iota((t, t), 1), s2, -jnp.inf)
                m_old = m_sc[hh]
                m_new = jnp.maximum(m_old, jnp.max(s2, axis=0, keepdims=True))
                p = jnp.exp2(s2 - m_new)
                acc[hh] = jnp.exp2(m_old - m_new) * acc[hh] + lax.dot_general(v_ref[:, sl], p.astype(BF16), TN,
                                                                              preferred_element_type=F32)
                m_sc[hh] = m_new

        @pl.when(j < i)
        def _():
            update(False)

        @pl.when(j == i)
        def _():
            update(True)
            outs = []
            for hh in range(hps):
                base = _extra_base(hh % 2)
                l = acc[hh, base:base + 1, :]
                outs.append((acc[hh] / l).T)
                lse_ref[0, hh:hh + 1, :] = m_sc[hh] + jnp.log2(l)
            o = jnp.concatenate([jnp.where(_data_mask(0), outs[a], outs[a + 1]) for a in range(0, hps, 2)], axis=1)
            o_ref[...] = o.astype(o_ref.dtype)
            og_ref[...] = (o * _sigmoid(g_ref[...].astype(F32))).astype(og_ref.dtype)

        @pl.when(last)
        def _():
            _gather_wait(sh_in, sh_out, ssem, rsem, lsem)

    qspec = pl.BlockSpec((t, hps * LANES), lambda p, s, it, jt: (it[s], p))
    kspec = pl.BlockSpec((t, hps * LANES), lambda p, s, it, jt: (jt[s], p))
    ospec = pl.BlockSpec((t, hps * FOX_DH), lambda p, s, it, jt: (it[s], p))
    lspec = pl.BlockSpec((1, hps, t), lambda p, s, it, jt: (p, 0, it[s]))
    res = pl.pallas_call(
        body, name="fox_attn_fwd",
        grid_spec=pltpu.PrefetchScalarGridSpec(
            num_scalar_prefetch=2, grid=(P, n_steps),
            in_specs=[qspec, kspec, kspec, ospec] + [ANY] * ns, out_specs=[ospec, ospec, lspec] + [ANY] * ns,
            scratch_shapes=[pltpu.VMEM((hps, 1, t), F32), pltpu.VMEM((hps, LANES, t), F32)] + _gather_sems(ns)),
        out_shape=[jax.ShapeDtypeStruct((L, D), BF16), jax.ShapeDtypeStruct((L, D), BF16),
                   jax.ShapeDtypeStruct((P, hps, L), F32)]
        + [jax.ShapeDtypeStruct((4,) + a.shape, a.dtype) for a in shards],
        compiler_params=_cparams(("arbitrary", "arbitrary")),
    )(it, jt, qp, kp, vp, gate, *shards)
    return res[0], res[1], res[2], res[3:]


def _fox_attn_bwd(qb, kp, vp, dop, slabs):
    L = qb.shape[0]
    t = _tile(L, ATTN_TILES)
    n = L // t
    P = FOX_H // 2
    it, jt = _causal_steps(n, True)
    n_steps = it.shape[0]
    ns = len(slabs)

    def body(it_ref, jt_ref, q_ref, k_ref, v_ref, do_ref, *rest):
        sl_in, (dq_ref, dk_ref, dv_ref, dck_ref), sl_out = rest[:ns], rest[ns:ns + 4], rest[ns + 4:2 * ns + 4]
        dk_acc, dv_acc, ssem, rsem = rest[2 * ns + 4:]
        step = pl.program_id(1)
        i, j = it_ref[step], jt_ref[step]

        @pl.when((pl.program_id(0) == 0) & (step == 0))
        def _():
            for cp in _scatter_copies(sl_in, sl_out, ssem, rsem):
                cp.start()

        @pl.when(step == 0)
        def _():
            dq_ref[...] = jnp.zeros_like(dq_ref)

        @pl.when(i == j)
        def _():
            dk_acc[...] = jnp.zeros_like(dk_acc)
            dv_acc[...] = jnp.zeros_like(dv_acc)

        def update(masked):
            rows = pl.ds(pl.multiple_of(i * t, LANES), t)
            for hh in range(2):
                sl = slice(hh * LANES, (hh + 1) * LANES)
                q, k, dov = q_ref[:, sl], k_ref[:, sl], do_ref[:, sl]
                s2 = lax.dot_general(k, q, NT, preferred_element_type=F32)
                if masked:
                    s2 = jnp.where(_iota((t, t), 0) <= _iota((t, t), 1), s2, -jnp.inf)
                p = jnp.exp2(s2)
                ds = (p * lax.dot_general(v_ref[:, sl], dov, NT, preferred_element_type=F32)).astype(BF16)
                dv_acc[hh] += jnp.dot(p.astype(BF16), dov, preferred_element_type=F32)
                dk_acc[hh] += jnp.dot(ds, q, preferred_element_type=F32)
                dq_ref[rows, sl] += lax.dot_general(ds, k, TN, preferred_element_type=F32)

        @pl.when(i > j)
        def _():
            update(False)

        @pl.when(i == j)
        def _():
            update(True)

        @pl.when(i == n - 1)
        def _():
            dk_ref[...] = jnp.where(_data_mask(0), dk_acc[0], dk_acc[1])
            dv_ref[...] = jnp.where(_data_mask(0), dv_acc[0], dv_acc[1]).astype(dv_ref.dtype)
            col_sums = [dk_acc[hh, :, _extra_base(hh) + 3:_extra_base(hh) + 4] for hh in range(2)]
            dck_ref[0] = -jnp.where(_iota((1, 2), 1) == 0, col_sums[0], col_sums[1])

        @pl.when((pl.program_id(0) == P - 1) & (step == n_steps - 1))
        def _():
            for cp in _scatter_copies(sl_in, sl_out, ssem, rsem):
                cp.wait()

    qspec = pl.BlockSpec((t, 2 * LANES), lambda p, s, it, jt: (it[s], p))
    kspec = pl.BlockSpec((t, 2 * LANES), lambda p, s, it, jt: (jt[s], p))
    ospec = pl.BlockSpec((t, LANES), lambda p, s, it, jt: (jt[s], p))
    res = pl.pallas_call(
        body, name="fox_attn_bwd",
        grid_spec=pltpu.PrefetchScalarGridSpec(
            num_scalar_prefetch=2, grid=(P, n_steps),
            in_specs=[qspec, kspec, kspec, qspec] + [ANY] * ns,
            out_specs=[pl.BlockSpec((L, 2 * LANES), lambda p, s, it, jt: (0, p)), ospec, ospec,
                       pl.BlockSpec((1, t, 2), lambda p, s, it, jt: (p, jt[s], 0))] + [ANY] * ns,
            scratch_shapes=[pltpu.VMEM((2, t, LANES), F32), pltpu.VMEM((2, t, LANES), F32),
                            pltpu.SemaphoreType.DMA((ns, 3)), pltpu.SemaphoreType.DMA((ns, 3))]),
        out_shape=[jax.ShapeDtypeStruct((L, FOX_H * LANES), F32), jax.ShapeDtypeStruct((L, D), F32),
                   jax.ShapeDtypeStruct((L, D), BF16), jax.ShapeDtypeStruct((P, L, 2), F32)]
        + [jax.ShapeDtypeStruct((3,) + a.shape[1:], a.dtype) for a in slabs],
        compiler_params=_cparams(("arbitrary", "arbitrary")),
    )(it, jt, qb, kp, vp, dop, *slabs)
    return res[0], res[1], res[2], res[3], res[4:]


def _hgrn_consts():
    C = HG_C
    r = np.arange(C)[:, None]
    j = np.arange(C)[None, :]
    mats = [j <= r, j > r]
    masks = []
    n = C
    while n >= 2:
        half = n // 2
        mid = (r // n) * n + half - 1
        second = (r % n) >= half
        mats.append(np.where(second, (j > mid) & (j <= r), (j > r) & (j <= mid)))
        masks.append(((r // n) == (j // n)) & ((r % n) >= half) & ((j % n) < half))
        n //= 2
    return (jnp.asarray(np.concatenate(mats, 0).astype(np.float32), BF16),
            jnp.asarray(np.stack(masks).astype(np.float32), F32))


def _hg_pre(hq, hz, h0, h1):
    mx = jnp.maximum(h0, h1)
    e0, e1 = jnp.exp(h0 - mx), jnp.exp(h1 - mx)
    lb = e1 / (e0 + e1)
    sq = _sigmoid(hq)
    sz = _sigmoid(hz)
    snz = 1.0 - sz
    k = (1.0 - lb) * snz
    g = jnp.maximum(jnp.log(lb + (1.0 - lb) * sz), -BIG)
    return lb, hq * sq, sq, k, sz, snz, g


def _hg_decays(g, rmat):
    hi = g.astype(BF16)
    lo = (g - hi.astype(F32)).astype(BF16)
    d = jnp.dot(rmat, jnp.concatenate([hi, lo], axis=1), preferred_element_type=F32)
    return jnp.exp(d[:, :HG_D] + d[:, HG_D:])


def _hg_intra(q, k, fall, masks):
    C = HG_C
    eye = _iota((C, C), 0) == _iota((C, C), 1)
    a = jnp.where(eye, jnp.sum(q * k, axis=-1, keepdims=True), 0.0)
    for l in range(HG_LEV):
        f = fall[(2 + l) * C:(3 + l) * C]
        a = a + masks[l] * lax.dot_general((q * f).astype(BF16), (k * f).astype(BF16), NT,
                                           preferred_element_type=F32)
    return a


def _hgrn_specs(n_chunks, reverse):
    C = HG_C
    w = HG_HPS * HG_D

    def col(first_head):
        off = first_head // HG_HPS
        if reverse:
            return pl.BlockSpec((C, w), lambda h, c: (n_chunks - 1 - c, off + h))
        return pl.BlockSpec((C, w), lambda h, c: (c, off + h))

    st = pl.BlockSpec((HG_HPS, 1, HG_D, HG_D),
                      (lambda h, c: (h, n_chunks - 1 - c, 0, 0)) if reverse else (lambda h, c: (h, c, 0, 0)))
    consts = [pl.BlockSpec((2, w), lambda h, c: (0, h)), pl.BlockSpec((1, HG_D), lambda h, c: (0, 0)),
              pl.BlockSpec(((2 + HG_LEV) * C, C), lambda h, c: (0, 0)),
              pl.BlockSpec((HG_LEV, C, C), lambda h, c: (0, 0, 0))]
    return col, st, consts


def _hgrn_fwd(proj, hlb, gg, rmat, masks):
    L = proj.shape[0]
    C = HG_C
    nc = L // C
    col, st, consts = _hgrn_specs(nc, False)

    def body(hq_ref, hz_ref, hi_ref, hg_ref, hlb_ref, gg_ref, r_ref, m_ref, og_ref, st_ref, state):
        c = pl.program_id(1)

        @pl.when(c == 0)
        def _():
            state[...] = jnp.zeros_like(state)

        for hh in range(HG_HPS):
            sl = slice(hh * HG_D, (hh + 1) * HG_D)
            v, hg = hi_ref[:, sl], hg_ref[:, sl]
            _, q, _, k, _, _, g = _hg_pre(hq_ref[:, sl], hz_ref[:, sl], hlb_ref[0:1, sl], hlb_ref[1:2, sl])
            fall = _hg_decays(g, r_ref[...])
            fb, fe = fall[0:C], fall[C:2 * C]
            st0 = state[hh]
            st_ref[hh, 0] = st0
            a = _hg_intra(q, k, fall, m_ref[...])
            vb = v.astype(BF16)
            o = jnp.dot(a.astype(BF16), vb, preferred_element_type=F32)
            o = o + lax.dot_general((q * fb).astype(BF16), st0.astype(BF16), NT, preferred_element_type=F32)
            ebc = jnp.exp(jnp.sum(g, axis=0, keepdims=True))
            state[hh] = st0 * ebc + lax.dot_general(vb, (k * fe).astype(BF16), TN, preferred_element_type=F32)
            r = lax.rsqrt(jnp.mean(o * o, axis=-1, keepdims=True) + EPS)
            og_ref[:, sl] = (o * r * gg_ref[...] * (hg * _sigmoid(hg))).astype(og_ref.dtype)

    return pl.pallas_call(
        body, name="hgrn_fwd", grid=(HG_H // HG_HPS, nc),
        in_specs=[col(0), col(HG_H), col(2 * HG_H), col(3 * HG_H)] + consts,
        out_specs=[col(0), st],
        out_shape=[jax.ShapeDtypeStruct((L, D), BF16), jax.ShapeDtypeStruct((HG_H, nc, HG_D, HG_D), F32)],
        scratch_shapes=[pltpu.VMEM((HG_HPS, HG_D, HG_D), F32)],
        compiler_params=_cparams(("parallel", "arbitrary")),
    )(proj, proj, proj, proj, hlb, gg, rmat, masks)


def _hgrn_bwd(proj, dog, states, hlb, gg, rmat, masks):
    L = proj.shape[0]
    C = HG_C
    nc = L // C
    col, st, consts = _hgrn_specs(nc, True)

    def body(hq_ref, hz_ref, hi_ref, hg_ref, do_ref, hlb_ref, gg_ref, r_ref, m_ref, st_ref,
             dq_ref, dz_ref, di_ref, dg_ref, dlb_ref, dgg_ref, dstate):
        c = pl.program_id(1)

        @pl.when(c == 0)
        def _():
            dstate[...] = jnp.zeros_like(dstate)
            dlb_ref[...] = jnp.zeros_like(dlb_ref)
            dgg_ref[...] = jnp.zeros_like(dgg_ref)

        for hh in range(HG_HPS):
            bwd_head(c, hh, slice(hh * HG_D, (hh + 1) * HG_D), hq_ref, hz_ref, hi_ref, hg_ref, do_ref, hlb_ref, gg_ref,
                     r_ref, m_ref, st_ref, dq_ref, dz_ref, di_ref, dg_ref, dlb_ref, dgg_ref, dstate)

    def bwd_head(c, hh, sl, hq_ref, hz_ref, hi_ref, hg_ref, do_ref, hlb_ref, gg_ref, r_ref, m_ref, st_ref,
                 dq_ref, dz_ref, di_ref, dg_ref, dlb_ref, dgg_ref, dstate):
        hq, hz, v, hg = hq_ref[:, sl], hz_ref[:, sl], hi_ref[:, sl], hg_ref[:, sl]
        dout = do_ref[:, sl].astype(F32)
        gain = gg_ref[...]
        masks_v = m_ref[...]
        lb, q, sq, k, sz, snz, g = _hg_pre(hq, hz, hlb_ref[0:1, sl], hlb_ref[1:2, sl])
        fall = _hg_decays(g, r_ref[...])
        fb, fe = fall[0:C], fall[C:2 * C]
        a = _hg_intra(q, k, fall, masks_v)
        st0 = st_ref[hh, 0]
        st0b = st0.astype(BF16)
        ebc = jnp.exp(jnp.sum(g, axis=0, keepdims=True))
        qb, ke, vb = (q * fb).astype(BF16), (k * fe).astype(BF16), v.astype(BF16)
        ab = a.astype(BF16)
        o = jnp.dot(ab, vb, preferred_element_type=F32) + lax.dot_general(qb, st0b, NT, preferred_element_type=F32)
        r = lax.rsqrt(jnp.mean(o * o, axis=-1, keepdims=True) + EPS)
        oh = o * r
        sg = _sigmoid(hg)
        d_on = dout * (hg * sg)
        dhg = dout * (oh * gain) * (sg * (1.0 + hg * (1.0 - sg)))
        dgg_ref[hh] += jnp.sum(d_on * oh, axis=0, keepdims=True)
        dxh = d_on * gain
        do = r * (dxh - oh * jnp.mean(dxh * oh, axis=-1, keepdims=True))
        dob = do.astype(BF16)
        dsp = dstate[hh]
        dspb = dsp.astype(BF16)
        causal = _iota((C, C), 0) >= _iota((C, C), 1)
        da = jnp.where(causal, lax.dot_general(dob, vb, NT, preferred_element_type=F32), 0.0)
        diag = jnp.sum(do * v, axis=-1, keepdims=True)
        dv = lax.dot_general(ab, dob, TN, preferred_element_type=F32)
        dv = dv + lax.dot_general(ke, dspb, NT, preferred_element_type=F32)
        xq = jnp.dot(dob, st0b, preferred_element_type=F32)
        xk = jnp.dot(vb, dspb, preferred_element_type=F32)
        dq = diag * k + fb * xq
        dk = diag * q + fe * xk
        ke_xk = ke.astype(F32) * xk
        db = qb.astype(F32) * xq - ke_xk
        for l in range(HG_LEV):
            f = fall[(2 + l) * C:(3 + l) * C]
            dal = (masks_v[l] * da).astype(BF16)
            ql, kl = (q * f).astype(BF16), (k * f).astype(BF16)
            xq = jnp.dot(dal, kl, preferred_element_type=F32)
            xk = lax.dot_general(dal, ql, TN, preferred_element_type=F32)
            dq = dq + f * xq
            dk = dk + f * xk
            db = db + ql.astype(F32) * xq - kl.astype(F32) * xk
        dstate[hh] = dsp * ebc + lax.dot_general(dob, qb, TN, preferred_element_type=F32)
        triu = (_iota((C, C), 0) <= _iota((C, C), 1)).astype(F32)
        dg = jnp.dot(triu, db, precision=HIGHEST, preferred_element_type=F32)
        dg = dg + jnp.sum(st0 * ebc * dsp, axis=0, keepdims=True) + jnp.sum(ke_xk, axis=0, keepdims=True)
        keep = _row_ids((nc - 1 - c) * C, C) >= ROW0
        dg = jnp.where(keep, dg, 0.0)
        dk = jnp.where(keep, dk, 0.0)
        f_gate = lb + (1.0 - lb) * sz
        dfdz = (1.0 - lb) * sz * snz
        dz_ref[:, sl] = (dg * dfdz / f_gate - dk * dfdz).astype(dz_ref.dtype)
        dlb_ref[:, sl] += jnp.sum(dg * snz / f_gate - dk * snz, axis=0, keepdims=True)
        dq_ref[:, sl] = jnp.where(keep, dq * (sq * (1.0 + hq * (1.0 - sq))), 0.0).astype(dq_ref.dtype)
        di_ref[:, sl] = jnp.where(keep, dv, 0.0).astype(di_ref.dtype)
        dg_ref[:, sl] = jnp.where(keep, dhg, 0.0).astype(dg_ref.dtype)

    w = HG_HPS * HG_D
    outs = pl.pallas_call(
        body, name="hgrn_bwd", grid=(HG_H // HG_HPS, nc),
        in_specs=[col(0), col(HG_H), col(2 * HG_H), col(3 * HG_H), col(0)] + consts + [st],
        out_specs=[col(0), col(0), col(0), col(0), pl.BlockSpec((1, w), lambda h, c: (0, h)),
                   pl.BlockSpec((HG_HPS, 1, HG_D), lambda h, c: (h, 0, 0))],
        out_shape=[jax.ShapeDtypeStruct((L, D), BF16)] * 4 + [jax.ShapeDtypeStruct((1, D), F32),
                                                              jax.ShapeDtypeStruct((HG_H, 1, HG_D), F32)],
        scratch_shapes=[pltpu.VMEM((HG_HPS, HG_D, HG_D), F32)],
        compiler_params=_cparams(("parallel", "arbitrary")),
    )(proj, proj, proj, proj, dog, hlb, gg, rmat, masks, states)
    return outs


def _ffn_fwd(h, norm_gain, wg, wu, wo, tag):
    hn = _rms_fwd(h, norm_gain, f"{tag}_norm")
    g, u, act = _ffn_in(hn, wg, wu, f"{tag}_in")
    h_out = _matmul(act, wo, add=h, name=f"{tag}_out")
    return h_out, (h, hn, g, u, act)


def _ffn_bwd(dh, saved, norm_gain, wg, wu, wo, tag):
    h, hn, g, u, act = saved
    dg, du = _ffn_dact(dh, wo, g, u, f"{tag}_dact")
    d_wo = _matmul(act, dh, ta=True, name=f"{tag}_dwo")
    dhn = _matmul(dg, wg, tb=True, name=f"{tag}_dhn_g")
    dhn = _matmul(du, wu, tb=True, add=dhn, name=f"{tag}_dhn_u")
    d_wg = _matmul(hn, dg, ta=True, name=f"{tag}_dwg")
    d_wu = _matmul(hn, du, ta=True, name=f"{tag}_dwu")
    dh, d_gain = _rms_bwd(h, dhn, norm_gain, dh, f"{tag}_norm_bwd")
    return dh, d_gain, (d_wg, d_wu, d_wo)


def _local_step(h0, tgt, w, late_shards):
    L = h0.shape[0]
    gmat = jnp.asarray(np.kron(np.eye(MXU_N // FOX_DH), np.ones((FOX_DH, FOX_DH))).astype(np.float32), BF16)
    rmat, lmasks = _hgrn_consts()
    an, fn_ = w["attn_norm"], w["ffn_norm"]
    qg = jnp.tile(w["fox_q_norm"], (1, FOX_H))
    kg = jnp.tile(w["fox_k_norm"], (1, FOX_H))
    bf = jnp.pad(w["fox_b_f"], ((0, 0), (0, LANES - FOX_H)))
    fw = w["fox_w_in"]
    f_wq, f_wk, f_wv, f_wg = (fw[:, i * D:(i + 1) * D] for i in range(4))
    f_wf = jnp.pad(fw[:, 4 * D:], ((0, 0), (0, LANES - FOX_H)))

    hn0 = _rms_fwd(h0, an[0:1], "fox_norm")
    q_raw = _matmul(hn0, f_wq, name="fox_q")
    k_raw = _matmul(hn0, f_wk, name="fox_k")
    v = _matmul(hn0, f_wv, out_dtype=BF16, name="fox_v")
    gate = _matmul(hn0, f_wg, out_dtype=BF16, name="fox_gate")
    flog = _matmul(hn0, f_wf, name="fox_flog")
    cq, ck = _fox_cumsum_fwd(flog, bf)
    qp, kp, vp = _fox_pack_fwd(q_raw, k_raw, v, cq, ck, qg, kg, gmat)
    o, og, lse2, gathered = _fox_attn_fwd(qp, kp, vp, gate, [late_shards[n] for n in GATHER_LATE])
    late = dict(zip(GATHER_LATE, gathered))
    f_wo = late["fox_w_out"].reshape(D, D)
    h_wo = late["hgrn_w_out"].reshape(D, D)
    h_wi = jnp.concatenate(list(late["hgrn_w_in"]), axis=1)
    g_in, g_out = late["ffn_w_in"], late["ffn_w_out"]
    ffw = []
    for i in range(2):
        rows_in, rows_out = slice(i * D, (i + 1) * D), slice(i * FFN // 4, (i + 1) * FFN // 4)
        ffw.append((jnp.concatenate([g_in[0, rows_in], g_in[1, rows_in]], axis=1),
                    jnp.concatenate([g_in[2, rows_in], g_in[3, rows_in]], axis=1),
                    jnp.concatenate([g_out[j, rows_out] for j in range(4)], axis=0)))
    h1 = _matmul(og, f_wo, add=h0, name="fox_out")
    h2, ffn0 = _ffn_fwd(h1, fn_[0:1], *ffw[0], "ffn0")

    hn2 = _rms_fwd(h2, an[1:2], "hgrn_norm")
    proj = _matmul(hn2, h_wi, name="hgrn_in")
    og1, states = _hgrn_fwd(proj, w["hgrn_lower_bounds"], w["hgrn_g_norm"], rmat, lmasks)
    h3 = _matmul(og1, h_wo, add=h2, name="hgrn_out")
    h4, ffn1 = _ffn_fwd(h3, fn_[1:2], *ffw[1], "ffn1")

    loss, dh, d_final = _loss_bwd(h4, tgt, w["final_norm"])

    dh, d_fn1, d_ffn1 = _ffn_bwd(dh, ffn1, fn_[1:2], *ffw[1], "ffn1")
    dog1 = _matmul(dh, h_wo, tb=True, out_dtype=BF16, name="hgrn_dog")
    d_h_wo = _matmul(og1, dh, ta=True, name="hgrn_dwo")
    dpq, dpz, dpi, dpg, d_lb, d_gg = _hgrn_bwd(proj, dog1, states, w["hgrn_lower_bounds"], w["hgrn_g_norm"],
                                               rmat, lmasks)
    dproj = jnp.concatenate([dpq, dpz, dpi, dpg], axis=1)
    dhn2 = _matmul(dproj, h_wi, tb=True, name="hgrn_dhn")
    d_h_wi = _matmul(hn2, dproj, ta=True, name="hgrn_dwi")
    dh, d_an1 = _rms_bwd(h2, dhn2, an[1:2], dh, "hgrn_norm_bwd")

    dh, d_fn0, d_ffn0 = _ffn_bwd(dh, ffn0, fn_[0:1], *ffw[0], "ffn0")
    n_in, n_out = 2 * FFN // 4, FFN // 4
    d_ffn = [d_ffn0, d_ffn1]
    late_grads = dict(
        hgrn_w_in=_to_shards("hgrn_w_in", d_h_wi[None]), hgrn_w_out=d_h_wo.reshape(4, D // 4, D),
        ffn_w_in=jnp.stack([jnp.concatenate([d[j // 2][:, (j % 2) * n_in:(j % 2 + 1) * n_in] for d in d_ffn], axis=0)
                            for j in range(4)]),
        ffn_w_out=jnp.stack([jnp.concatenate([d[2][j * n_out:(j + 1) * n_out] for d in d_ffn], axis=0)
                             for j in range(4)]))
    pair_late, send_late = _pair_sums([late_grads[n] for n in LATE_NAMES], "late")

    dog = _matmul(dh, f_wo, tb=True, out_dtype=BF16, name="fox_dog")
    d_f_wo = _matmul(og, dh, ta=True, name="fox_dwo")

    def by_head(a):
        return jnp.pad(a.transpose(1, 0, 2).reshape(L, FOX_H), ((0, 0), (0, LANES - FOX_H)))

    qb = _fox_pack_bias(qp, cq, by_head(lse2.transpose(0, 2, 1)))
    dop, dgate = _fox_pack_bwd(dog, o, gate)
    dqp, dk, dv, dck, recv_late = _fox_attn_bwd(qb, kp, vp, dop, send_late)
    (dq_raw, dk_raw, dc_q), (d_qg, d_kg) = _fox_unpack_bwd(q_raw, k_raw, dqp, dk, qg, kg, gmat)
    dflog, d_bf = _fox_cumsum_bwd(dc_q, by_head(dck), flog, bf)
    dproj0 = jnp.concatenate([dq_raw, dk_raw, dv, dgate, dflog.astype(BF16)], axis=1)
    f_wall = jnp.concatenate([f_wq, f_wk, f_wv, f_wg, f_wf], axis=1)
    dhn0 = _matmul(dproj0, f_wall, tb=True, name="fox_dhn")
    d_f_wall = _matmul(hn0, dproj0, ta=True, name="fox_dwi")
    d_f_wi = d_f_wall[:, :4 * D + FOX_H]
    dh, d_an0 = _rms_bwd(h0, dhn0, an[0:1], dh, "fox_norm_bwd")

    fox_grads = dict(fox_w_in=d_f_wi[None], fox_w_out=d_f_wo[None])
    pair_fox, send_fox = _pair_sums([_to_shards(n, fox_grads[n]) for n in FOX_NAMES], "fox")
    recv_fox = _chip_scatter(send_fox, "fox")
    halves = _chip_sums(pair_fox, recv_fox, "fox") + _chip_sums(pair_late, recv_late, "late")
    theirs = _sibling_exchange(halves)
    big = {n: (m, t) for n, m, t in zip(FOX_NAMES + LATE_NAMES, halves, theirs)}
    small = dict(attn_norm=jnp.concatenate([d_an0, d_an1]), ffn_norm=jnp.concatenate([d_fn0, d_fn1]),
                 final_norm=d_final, lb_raw=d_lb, q_gain=d_qg, k_gain=d_kg, b_f=d_bf,
                 g_gain=d_gg.reshape(1, D))
    return loss, dh, big, small


def _me():
    return lax.axis_index("x"), lax.axis_index("y"), lax.axis_index("c")


def _flip(v, bit):
    return 1 - v if bit else v


def _chip_allgather(arrs):
    n = len(arrs)

    def body(*refs):
        _gather_start(refs[:n], refs[n:2 * n], *refs[2 * n:])
        _gather_wait(refs[:n], refs[n:2 * n], *refs[2 * n:])

    return pl.pallas_call(
        body, name="chip_allgather", in_specs=[ANY] * n, out_specs=[ANY] * n,
        out_shape=[jax.ShapeDtypeStruct((4,) + a.shape, a.dtype) for a in arrs],
        scratch_shapes=_gather_sems(n),
    )(*arrs)


def _chip_peers():
    x, y, c = _me()
    return [(1 - x, y, c), (x, 1 - y, c), (1 - x, 1 - y, c)]


def _gather_sems(n):
    return [pltpu.SemaphoreType.DMA((n, 3)), pltpu.SemaphoreType.DMA((n, 3)), pltpu.SemaphoreType.DMA((n,))]


def _gather_copies(ins, outs, ssem, rsem, lsem, with_recvs):
    x, y, _ = _me()
    local, sends, recvs = [], [], []
    for a in range(len(ins)):
        local.append(pltpu.make_async_copy(ins[a], outs[a].at[2 * x + y], lsem.at[a]))
        for k, peer in enumerate(_chip_peers()):
            sends.append(pltpu.make_async_remote_copy(ins[a], outs[a].at[2 * x + y], ssem.at[a, k], rsem.at[a, k],
                                                      device_id=peer, device_id_type=MESH))
            if with_recvs:
                recvs.append(pltpu.make_async_remote_copy(ins[a], outs[a].at[2 * peer[0] + peer[1]], ssem.at[a, k],
                                                          rsem.at[a, k], device_id=peer, device_id_type=MESH))
    return local, sends, recvs


def _gather_start(ins, outs, ssem, rsem, lsem):
    local, sends, _ = _gather_copies(ins, outs, ssem, rsem, lsem, False)
    for cp in local + sends:
        cp.start()


def _gather_wait(ins, outs, ssem, rsem, lsem):
    local, sends, recvs = _gather_copies(ins, outs, ssem, rsem, lsem, True)
    for cp in local:
        cp.wait()
    for cp in sends:
        cp.wait_send()
    for cp in recvs:
        cp.wait_recv()


def _scatter_copies(ins, outs, ssem, rsem):
    copies = []
    for a in range(len(ins)):
        for k, peer in enumerate(_chip_peers()):
            copies.append(pltpu.make_async_remote_copy(ins[a].at[2 * peer[0] + peer[1]], outs[a].at[k], ssem.at[a, k],
                                                       rsem.at[a, k], device_id=peer, device_id_type=MESH))
    return copies


def _device_allgather(arr):
    def body(in_ref, out_ref, ssem, rsem, lsem):
        x, y, c = _me()
        me = 4 * x + 2 * y + c
        peers = [(_flip(x, k & 4), _flip(y, k & 2), _flip(c, k & 1)) for k in range(1, 8)]
        local = pltpu.make_async_copy(in_ref, out_ref.at[me], lsem)
        local.start()
        sends = []
        for k, peer in enumerate(peers):
            cp = pltpu.make_async_remote_copy(in_ref, out_ref.at[me], ssem.at[k], rsem.at[k],
                                              device_id=peer, device_id_type=MESH)
            cp.start()
            sends.append(cp)
        local.wait()
        for cp in sends:
            cp.wait_send()
        for k, peer in enumerate(peers):
            pltpu.make_async_remote_copy(in_ref, out_ref.at[4 * peer[0] + 2 * peer[1] + peer[2]], ssem.at[k],
                                         rsem.at[k], device_id=peer, device_id_type=MESH).wait_recv()

    return pl.pallas_call(
        body, name="device_allgather", in_specs=[ANY], out_specs=ANY,
        out_shape=jax.ShapeDtypeStruct((8,) + arr.shape, arr.dtype),
        scratch_shapes=[pltpu.SemaphoreType.DMA((7,)), pltpu.SemaphoreType.DMA((7,)), pltpu.SemaphoreType.DMA],
    )(arr)


def _sibling_send_other_half(arrs, tag):
    n = len(arrs)

    def body(*refs):
        ins, outs = refs[:n], refs[n:2 * n]
        ssem, rsem = refs[2 * n:]
        x, y, c = _me()
        cps = []
        for a in range(n):
            half = ins[a].shape[1] // 2
            src = ins[a].at[:, pl.ds((1 - c) * half, half), :]
            cp = pltpu.make_async_remote_copy(src, outs[a], ssem.at[a], rsem.at[a],
                                              device_id=(x, y, 1 - c), device_id_type=MESH)
            cp.start()
            cps.append(cp)
        for cp in cps:
            cp.wait()

    return pl.pallas_call(
        body, name=f"grad_sibling_swap_{tag}", in_specs=[ANY] * n, out_specs=[ANY] * n,
        out_shape=[jax.ShapeDtypeStruct((4, a.shape[1] // 2, a.shape[2]), a.dtype) for a in arrs],
        scratch_shapes=[pltpu.SemaphoreType.DMA((n,)), pltpu.SemaphoreType.DMA((n,))],
    )(*arrs)


def _chip_scatter(arrs, tag):
    n = len(arrs)

    def body(*refs):
        cps = _scatter_copies(refs[:n], refs[n:2 * n], *refs[2 * n:])
        for cp in cps:
            cp.start()
        for cp in cps:
            cp.wait()

    return pl.pallas_call(
        body, name=f"grad_chip_scatter_{tag}", in_specs=[ANY] * n, out_specs=[ANY] * n,
        out_shape=[jax.ShapeDtypeStruct((3,) + a.shape[1:], a.dtype) for a in arrs],
        scratch_shapes=[pltpu.SemaphoreType.DMA((n, 3)), pltpu.SemaphoreType.DMA((n, 3))],
    )(*arrs)


def _sibling_exchange(arrs):
    n = len(arrs)

    def body(*refs):
        ins, outs = refs[:n], refs[n:2 * n]
        ssem, rsem = refs[2 * n:]
        x, y, c = _me()
        cps = [pltpu.make_async_remote_copy(ins[a], outs[a], ssem.at[a], rsem.at[a], device_id=(x, y, 1 - c),
                                            device_id_type=MESH) for a in range(n)]
        for cp in cps:
            cp.start()
        for cp in cps:
            cp.wait()

    return pl.pallas_call(
        body, name="grad_sibling_exchange", in_specs=[ANY] * n, out_specs=[ANY] * n,
        out_shape=[jax.ShapeDtypeStruct(a.shape, a.dtype) for a in arrs],
        scratch_shapes=[pltpu.SemaphoreType.DMA((n,)), pltpu.SemaphoreType.DMA((n,))],
    )(*arrs)


def _pair_sums(grads, tag):
    got = _sibling_send_other_half(grads, tag)
    res = [_pair_add(g, t, f"grad_pair_add_{tag}{i}") for i, (g, t) in enumerate(zip(grads, got))]
    return [r[0] for r in res], [r[1] for r in res]


def _mesh_scalar(v):
    return jnp.asarray(v, jnp.int32).reshape(1)


def _pair_add(g, t, name):
    _, rows, cols = g.shape
    half = rows // 2
    tm = _tile(half, cap=(2 * 1024 * 1024) // (4 * cols))

    def body(c_ref, g_ref, t_ref, o_ref, ob_ref):
        s = g_ref[0, 0] + t_ref[0]
        o_ref[0] = s
        ob_ref[0] = s.astype(ob_ref.dtype)

    spec = pl.BlockSpec((1, tm, cols), lambda j, i, c: (j, i, 0))
    return pl.pallas_call(
        body, name=name,
        grid_spec=pltpu.PrefetchScalarGridSpec(
            num_scalar_prefetch=1, grid=(4, half // tm),
            in_specs=[pl.BlockSpec((1, 1, tm, cols), lambda j, i, c: (j, c[0], i, 0)), spec],
            out_specs=[spec, spec]),
        out_shape=[jax.ShapeDtypeStruct(t.shape, F32), jax.ShapeDtypeStruct(t.shape, BF16)],
        compiler_params=_cparams(("parallel", "parallel")),
    )(_mesh_scalar(lax.axis_index("c")), g.reshape(4, 2, half, cols), t)


def _chip_sums(pair, recv, tag):
    x, y, _ = _me()
    out = []
    for n, (p, r) in enumerate(zip(pair, recv)):
        _, half, cols = p.shape
        tm = _tile(half, cap=(2 * 1024 * 1024) // (4 * cols))

        def body(j_ref, p_ref, r_ref, o_ref):
            o_ref[...] = p_ref[0] + r_ref[0].astype(F32) + r_ref[1].astype(F32) + r_ref[2].astype(F32)

        out.append(pl.pallas_call(
            body, name=f"grad_chip_add_{tag}{n}",
            grid_spec=pltpu.PrefetchScalarGridSpec(
                num_scalar_prefetch=1, grid=(half // tm,),
                in_specs=[pl.BlockSpec((1, tm, cols), lambda i, j: (j[0], i, 0)),
                          pl.BlockSpec((3, tm, cols), lambda i, j: (0, i, 0))],
                out_specs=pl.BlockSpec((tm, cols), lambda i, j: (i, 0))),
            out_shape=jax.ShapeDtypeStruct((half, cols), F32),
            compiler_params=_cparams(("parallel",)),
        )(_mesh_scalar(2 * x + y), p, r))
    return out


SMALL_ROWS = 32


def _small_finalize(gathered, hlb, fold64, fold128):
    def body(g_ref, hlb_ref, f64_ref, f128_ref, rows_ref, qk_ref, gg_ref, lb_ref):
        tot = g_ref[0]
        for d in range(1, 8):
            tot = tot + g_ref[d]
        rows_ref[...] = tot
        qk_ref[...] = jnp.dot(rows_ref[6:8, :], f64_ref[...], precision=HIGHEST, preferred_element_type=F32)
        gg_ref[...] = jnp.dot(rows_ref[9:10, :], f128_ref[...], precision=HIGHEST, preferred_element_type=F32)
        h0, h1 = hlb_ref[0:1, :], hlb_ref[1:2, :]
        mx = jnp.maximum(h0, h1)
        e0, e1 = jnp.exp(h0 - mx), jnp.exp(h1 - mx)
        lb = e1 / (e0 + e1)
        d1 = rows_ref[5:6, :] * lb * (1.0 - lb)
        lb_ref[...] = jnp.where(_iota((2, 1), 0) == 0, -d1, d1)

    return pl.pallas_call(
        body, name="small_finalize",
        out_shape=[jax.ShapeDtypeStruct((SMALL_ROWS, D), F32), jax.ShapeDtypeStruct((2, FOX_DH), F32),
                   jax.ShapeDtypeStruct((1, HG_D), F32), jax.ShapeDtypeStruct((2, D), F32)],
    )(gathered, hlb, fold64, fold128)


FOX_NAMES = ("fox_w_in", "fox_w_out")
LATE_NAMES = ("hgrn_w_in", "hgrn_w_out", "ffn_w_in", "ffn_w_out")
BIG_NAMES = FOX_NAMES + LATE_NAMES
GATHER_LATE = ("fox_w_out",) + LATE_NAMES
COL_SHARDED = ("fox_w_in", "hgrn_w_in", "ffn_w_in")


def _shard2d(name, a):
    return a.reshape(-1, a.shape[-1])


def _to_shards(name, g):
    layers = g.shape[0]
    if name in COL_SHARDED:
        k, n = g.shape[1], g.shape[2] // 4
        return g.reshape(layers, k, 4, n).transpose(2, 0, 1, 3).reshape(4, layers * k, n)
    r = g.shape[1] // 4
    return g.reshape(layers, 4, r, g.shape[2]).transpose(1, 0, 2, 3).reshape(4, layers * r, g.shape[2])


def kernel(x, meta_tokens, attn_norm, ffn_norm, final_norm, fox_w_in, fox_b_f, fox_q_norm, fox_k_norm, fox_w_out, hgrn_w_in, hgrn_lower_bounds, hgrn_g_norm, hgrn_w_out, ffn_w_in, ffn_w_out, loss_target, m_meta_tokens, m_attn_norm, m_ffn_norm, m_final_norm, m_fox_w_in, m_fox_b_f, m_fox_q_norm, m_fox_k_norm, m_fox_w_out, m_hgrn_w_in, m_hgrn_lower_bounds, m_hgrn_g_norm, m_hgrn_w_out, m_ffn_w_in, m_ffn_w_out, v_meta_tokens, v_attn_norm, v_ffn_norm, v_final_norm, v_fox_w_in, v_fox_b_f, v_fox_q_norm, v_fox_k_norm, v_fox_w_out, v_hgrn_w_in, v_hgrn_lower_bounds, v_hgrn_g_norm, v_hgrn_w_out, v_ffn_w_in, v_ffn_w_out):
    params = dict(meta_tokens=meta_tokens, attn_norm=attn_norm, ffn_norm=ffn_norm, final_norm=final_norm,
                  fox_w_in=fox_w_in, fox_b_f=fox_b_f, fox_q_norm=fox_q_norm, fox_k_norm=fox_k_norm,
                  fox_w_out=fox_w_out, hgrn_w_in=hgrn_w_in, hgrn_lower_bounds=hgrn_lower_bounds,
                  hgrn_g_norm=hgrn_g_norm, hgrn_w_out=hgrn_w_out, ffn_w_in=ffn_w_in, ffn_w_out=ffn_w_out)
    mom_m = dict(meta_tokens=m_meta_tokens, attn_norm=m_attn_norm, ffn_norm=m_ffn_norm, final_norm=m_final_norm,
                 fox_w_in=m_fox_w_in, fox_b_f=m_fox_b_f, fox_q_norm=m_fox_q_norm, fox_k_norm=m_fox_k_norm,
                 fox_w_out=m_fox_w_out, hgrn_w_in=m_hgrn_w_in, hgrn_lower_bounds=m_hgrn_lower_bounds,
                 hgrn_g_norm=m_hgrn_g_norm, hgrn_w_out=m_hgrn_w_out, ffn_w_in=m_ffn_w_in, ffn_w_out=m_ffn_w_out)
    mom_v = dict(meta_tokens=v_meta_tokens, attn_norm=v_attn_norm, ffn_norm=v_ffn_norm, final_norm=v_final_norm,
                 fox_w_in=v_fox_w_in, fox_b_f=v_fox_b_f, fox_q_norm=v_fox_q_norm, fox_k_norm=v_fox_k_norm,
                 fox_w_out=v_fox_w_out, hgrn_w_in=v_hgrn_w_in, hgrn_lower_bounds=v_hgrn_lower_bounds,
                 hgrn_g_norm=v_hgrn_g_norm, hgrn_w_out=v_hgrn_w_out, ffn_w_in=v_ffn_w_in, ffn_w_out=v_ffn_w_out)
    names = list(params)
    xi, yi, _ = _me()

    shards = {n: _shard2d(n, params[n]).astype(BF16) for n in BIG_NAMES}
    w_in_g, meta_g = _chip_allgather([shards["fox_w_in"], meta_tokens])
    w = dict(fox_w_in=jnp.concatenate(list(w_in_g), axis=1))
    meta_full = jnp.concatenate(list(meta_g), axis=1)
    w.update(attn_norm=attn_norm, ffn_norm=ffn_norm, final_norm=final_norm.reshape(1, D), fox_b_f=fox_b_f,
             fox_q_norm=fox_q_norm, fox_k_norm=fox_k_norm, hgrn_lower_bounds=hgrn_lower_bounds,
             hgrn_g_norm=hgrn_g_norm)

    h0 = jnp.concatenate([jnp.zeros((ROW0, D), F32), meta_full, x[0]], axis=0)
    loss, dh0, big, small = _local_step(h0, loss_target[0], w, {n: shards[n] for n in GATHER_LATE})
    loss = lax.psum(loss, ("x", "y", "c"))
    grad_x = dh0[PAD:][None]
    grads = {}

    rows = jnp.concatenate([small["attn_norm"], small["ffn_norm"], small["final_norm"], small["lb_raw"],
                            small["q_gain"], small["k_gain"],
                            jnp.pad(small["b_f"], ((0, 0), (0, D - LANES))), small["g_gain"],
                            dh0[ROW0:PAD], jnp.zeros((SMALL_ROWS - 10 - N_META, D), F32)], axis=0)
    allrows = _device_allgather(rows)
    fold64 = jnp.asarray(np.tile(np.eye(FOX_DH, dtype=np.float32), (FOX_H, 1)))
    fold128 = jnp.asarray(np.tile(np.eye(HG_D, dtype=np.float32), (HG_H, 1)))
    tot, qk, gg, dlb = _small_finalize(allrows, hgrn_lower_bounds, fold64, fold128)
    grads.update(attn_norm=tot[0:2], ffn_norm=tot[2:4], final_norm=tot[4], hgrn_lower_bounds=dlb,
                 fox_q_norm=qk[0:1], fox_k_norm=qk[1:2], fox_b_f=tot[8:9, :FOX_H], hgrn_g_norm=gg,
                 meta_tokens=lax.dynamic_slice_in_dim(tot[10:10 + N_META], (2 * xi + yi) * (D // 4), D // 4, axis=1))

    delta, new_m, new_v = {}, {}, {}
    for n in BIG_NAMES:
        res = _adamw_halves(_shard2d(n, params[n]), *big[n], _shard2d(n, mom_m[n]), _shard2d(n, mom_v[n]),
                            f"adamw_{n}")
        grads[n], delta[n], new_m[n], new_v[n] = (t.reshape(params[n].shape) for t in res)
    delta["meta_tokens"], new_m["meta_tokens"], new_v["meta_tokens"] = _adamw(
        meta_tokens, grads["meta_tokens"], m_meta_tokens, v_meta_tokens, "adamw_meta_tokens")
    small_names = [n for n in names if n not in BIG_NAMES and n != "meta_tokens"]

    def pack(d):
        return jnp.concatenate([jnp.pad(d[n].reshape(-1, d[n].shape[-1]), ((0, 0), (0, D - d[n].shape[-1])))
                                for n in small_names], axis=0)

    packed = [pack(t) for t in (params, grads, mom_m, mom_v)]
    n_rows = packed[0].shape[0]
    packed = [jnp.pad(t, ((0, 16 - n_rows), (0, 0))) for t in packed]
    res = _adamw(*packed, "adamw_small")
    r0 = 0
    for n in small_names:
        nr = params[n].reshape(-1, params[n].shape[-1]).shape[0]
        for dst, src in zip((delta, new_m, new_v), res):
            dst[n] = src[r0:r0 + nr, :params[n].shape[-1]].reshape(params[n].shape)
        r0 += nr

    return (loss, grad_x, *[grads[n] for n in names], *[delta[n] for n in names],
            *[new_m[n] for n in names], *[new_v[n] for n in names])
```

```python
import functools

import numpy as np
import jax
import jax.numpy as jnp
from jax import lax
from jax.experimental import pallas as pl
from jax.experimental.pallas import tpu as pltpu

F32, BF16 = jnp.float32, jnp.bfloat16
HIGHEST = lax.Precision.HIGHEST

D = 1024
N_META = 16
PAD = 128
ROW0 = PAD - N_META
FOX_H, FOX_DH = 16, 64
HG_H, HG_D = 8, 128
HG_C = 128
HG_LEV = 7
HG_HPS = 4
FFN = 2816
EPS = 1e-6
BIG = 1e30
LOG2E = 1.4426950408889634
LANES = 128
MXU_N = 256
VMEM_LIMIT = 48 * 1024 * 1024
ROW_TILES = (640, 512, 384, 320, 256, 128, 64, 32, 16, 8)
ATTN_TILES = (640, 512, 256, 128)
FOX_HPS_FWD = 8

ADAM_LR, ADAM_B1, ADAM_B2, ADAM_EPS, ADAM_WD, ADAM_STEP = 0.001, 0.9, 0.999, 1e-08, 0.01, 10

MESH = pl.DeviceIdType.MESH
ANY = pl.BlockSpec(memory_space=pl.ANY)
NT = (((1,), (1,)), ((), ()))
TN = (((0,), (0,)), ((), ()))


def _tile(n, cands=ROW_TILES, cap=None):
    for c in cands:
        if n % c == 0 and (cap is None or c <= cap):
            return c
    return n


def _cparams(sem):
    return pltpu.CompilerParams(dimension_semantics=sem, vmem_limit_bytes=VMEM_LIMIT)


def _sigmoid(x):
    return jax.nn.sigmoid(x)


def _log_sigmoid(x):
    return jnp.minimum(x, 0.0) - jnp.log(1.0 + jnp.exp(-jnp.abs(x)))


def _iota(shape, dim):
    return lax.broadcasted_iota(jnp.int32, shape, dim)


def _matmul(a, b, *, ta=False, tb=False, out_dtype=F32, add=None, name):
    if ta:
        kdim, m = a.shape
    else:
        m, kdim = a.shape
    n = b.shape[0] if tb else b.shape[1]
    if ta:
        tm = m if m <= 1024 else _tile(m, (1408, 1024, 512, 256, 128))
        tk = _tile(kdim)
    else:
        tm = _tile(m)
        tk = kdim if kdim <= 4096 else _tile(kdim, (2048, 1024, 512))
    tn = n if n <= 1024 else _tile(n, (1408, 1024, 512, 256, 128))
    nk = kdim // tk
    dn = (((0 if ta else 1,), (1 if tb else 0,)), ((), ()))

    def body(*refs):
        if add is None:
            a_ref, b_ref, o_ref, acc_ref = refs
        else:
            a_ref, b_ref, add_ref, o_ref, acc_ref = refs
        k = pl.program_id(2)

        @pl.when(k == 0)
        def _():
            acc_ref[...] = jnp.zeros_like(acc_ref)

        acc_ref[...] += lax.dot_general(a_ref[...].astype(BF16), b_ref[...].astype(BF16), dn,
                                        preferred_element_type=F32)

        @pl.when(k == nk - 1)
        def _():
            r = acc_ref[...]
            if add is not None:
                r = r + add_ref[...].astype(F32)
            o_ref[...] = r.astype(o_ref.dtype)

    a_spec = pl.BlockSpec((tk, tm), lambda j, i, k: (k, i)) if ta else pl.BlockSpec((tm, tk), lambda j, i, k: (i, k))
    b_spec = pl.BlockSpec((tn, tk), lambda j, i, k: (j, k)) if tb else pl.BlockSpec((tk, tn), lambda j, i, k: (k, j))
    o_spec = pl.BlockSpec((tm, tn), lambda j, i, k: (i, j))
    ins, specs = [a, b], [a_spec, b_spec]
    if add is not None:
        ins.append(add)
        specs.append(o_spec)
    return pl.pallas_call(
        body, name=name, grid=(n // tn, m // tm, nk), in_specs=specs, out_specs=o_spec,
        out_shape=jax.ShapeDtypeStruct((m, n), out_dtype),
        scratch_shapes=[pltpu.VMEM((tm, tn), F32)],
        compiler_params=_cparams(("parallel", "parallel", "arbitrary")),
    )(*ins)


def _rowwise(fn, ins, bcast, outs, accs, *, name, reverse=False, carry=None, as_refs=False):
    rows = ins[0].shape[0]
    per_row = sum(x.shape[1] * x.dtype.itemsize for x in ins) + sum(c * jnp.dtype(d).itemsize for c, d in outs)
    tm = _tile(rows, cap=max(8, (10 * 1024 * 1024) // per_row))
    n = rows // tm
    n_in, n_b, n_o, n_a = len(ins), len(bcast), len(outs), len(accs)

    def body(*refs):
        in_refs = refs[:n_in]
        b_refs = refs[n_in:n_in + n_b]
        o_refs = refs[n_in + n_b:n_in + n_b + n_o]
        a_refs = refs[n_in + n_b + n_o:n_in + n_b + n_o + n_a]
        c_refs = refs[n_in + n_b + n_o + n_a:]
        i = pl.program_id(0)
        blk = (n - 1 - i) if reverse else i
        if c_refs:
            @pl.when(i == 0)
            def _():
                c_refs[0][...] = jnp.zeros_like(c_refs[0])
        args = (list(in_refs) if as_refs else [r[...] for r in in_refs], [r[...] for r in b_refs])
        o_vals, a_vals = fn(blk * tm, *args, *c_refs)
        for r, v in zip(o_refs, o_vals):
            r[...] = v.astype(r.dtype)
        if n_a:
            @pl.when(i == 0)
            def _():
                for r in a_refs:
                    r[...] = jnp.zeros_like(r)
            for r, v in zip(a_refs, a_vals):
                r[...] += v

    def row_map(i):
        return ((n - 1 - i) if reverse else i, 0)

    in_specs = [pl.BlockSpec((tm, x.shape[1]), row_map) for x in ins]
    in_specs += [pl.BlockSpec(x.shape, lambda i, nd=x.ndim: (0,) * nd) for x in bcast]
    out_specs = [pl.BlockSpec((tm, c), row_map) for c, _ in outs]
    out_specs += [pl.BlockSpec(s, lambda i: (0, 0)) for s in accs]
    out_shape = [jax.ShapeDtypeStruct((rows, c), d) for c, d in outs]
    out_shape += [jax.ShapeDtypeStruct(s, F32) for s in accs]
    res = pl.pallas_call(
        body, name=name, grid=(n,), in_specs=in_specs, out_specs=out_specs, out_shape=out_shape,
        scratch_shapes=[pltpu.VMEM(carry, F32)] if carry else [],
        compiler_params=_cparams(("arbitrary",)),
    )(*ins, *bcast)
    return res[:n_o], res[n_o:]


def _row_ids(row0, tm):
    return row0 + _iota((tm, 1), 0)


def _rms_fwd(x, gain, name):
    def fn(row0, ins, bc):
        (xv,), (g,) = ins, bc
        r = lax.rsqrt(jnp.mean(xv * xv, axis=-1, keepdims=True) + EPS)
        return [xv * r * g], []
    return _rowwise(fn, [x], [gain], [(D, BF16)], [], name=name)[0][0]


def _rms_bwd(x, dxn, gain, dh_up, name):
    def fn(row0, ins, bc):
        xv, dy, up = ins
        (g,) = bc
        dy = dy.astype(F32)
        r = lax.rsqrt(jnp.mean(xv * xv, axis=-1, keepdims=True) + EPS)
        xh = xv * r
        dxh = dy * g
        dx = r * (dxh - xh * jnp.mean(dxh * xh, axis=-1, keepdims=True))
        keep = _row_ids(row0, xv.shape[0]) >= ROW0
        return [jnp.where(keep, up + dx, 0.0)], [jnp.sum(dy * xh, axis=0, keepdims=True)]
    (dh,), (dgain,) = _rowwise(fn, [x, dxn, dh_up], [gain], [(D, F32)], [(1, D)], name=name)
    return dh, dgain


def _loss_bwd(h, tgt, gain):
    tm = PAD
    n = h.shape[0] // tm

    def body(x_ref, t_ref, g_ref, dx_ref, loss_ref, dg_ref):
        i = pl.program_id(0)

        @pl.when(i == 0)
        def _():
            loss_ref[...] = jnp.zeros_like(loss_ref)
            dg_ref[...] = jnp.zeros_like(dg_ref)

        xv, g = x_ref[...], g_ref[...]
        r = lax.rsqrt(jnp.mean(xv * xv, axis=-1, keepdims=True) + EPS)
        xh = xv * r
        err = jnp.where(i >= 1, xh * g - t_ref[...], 0.0)
        per_row = jnp.mean(err * err, axis=-1, keepdims=True)
        loss_ref[...] += jnp.broadcast_to(0.5 * jnp.sum(per_row, axis=0, keepdims=True), (1, LANES))
        dy = err * (1.0 / D)
        dxh = dy * g
        dx_ref[...] = r * (dxh - xh * jnp.mean(dxh * xh, axis=-1, keepdims=True))
        dg_ref[...] += jnp.sum(dy * xh, axis=0, keepdims=True)

    dh, loss, dgain = pl.pallas_call(
        body, name="loss_bwd", grid=(n,),
        in_specs=[pl.BlockSpec((tm, D), lambda i: (i, 0)), pl.BlockSpec((tm, D), lambda i: (jnp.maximum(i - 1, 0), 0)),
                  pl.BlockSpec((1, D), lambda i: (0, 0))],
        out_specs=[pl.BlockSpec((tm, D), lambda i: (i, 0)), pl.BlockSpec((1, LANES), lambda i: (0, 0)),
                   pl.BlockSpec((1, D), lambda i: (0, 0))],
        out_shape=[jax.ShapeDtypeStruct(h.shape, F32), jax.ShapeDtypeStruct((1, LANES), F32),
                   jax.ShapeDtypeStruct((1, D), F32)],
        compiler_params=_cparams(("arbitrary",)),
    )(h, tgt, gain)
    return loss[0, 0], dh, dgain


FFN_TILES = dict(rows=(320, 256, 128), cols=(1408, 1024, 512, 256, 128))


def _ffn_in(hn, wg, wu, name):
    m, kdim = hn.shape
    n = wg.shape[1]
    tm, tn = _tile(m, FFN_TILES["rows"]), _tile(n, FFN_TILES["cols"])

    def body(a_ref, wg_ref, wu_ref, g_ref, u_ref, act_ref):
        a = a_ref[...]
        g = jnp.dot(a, wg_ref[...], preferred_element_type=F32)
        u = jnp.dot(a, wu_ref[...], preferred_element_type=F32)
        g_ref[...] = g.astype(g_ref.dtype)
        u_ref[...] = u.astype(u_ref.dtype)
        act_ref[...] = (g * _sigmoid(g) * u).astype(act_ref.dtype)

    wspec = pl.BlockSpec((kdim, tn), lambda j, i: (0, j))
    ospec = pl.BlockSpec((tm, tn), lambda j, i: (i, j))
    return pl.pallas_call(
        body, name=name, grid=(n // tn, m // tm),
        in_specs=[pl.BlockSpec((tm, kdim), lambda j, i: (i, 0)), wspec, wspec], out_specs=[ospec] * 3,
        out_shape=[jax.ShapeDtypeStruct((m, n), BF16)] * 3,
        compiler_params=_cparams(("parallel", "parallel")),
    )(hn, wg, wu)


def _ffn_dact(dh, wo, g, u, name):
    m, kdim = dh.shape
    n = wo.shape[0]
    tm, tn = _tile(m, FFN_TILES["rows"]), _tile(n, FFN_TILES["cols"])

    def body(a_ref, w_ref, g_ref, u_ref, dg_ref, du_ref):
        da = lax.dot_general(a_ref[...].astype(BF16), w_ref[...], NT, preferred_element_type=F32)
        gv, uv = g_ref[...].astype(F32), u_ref[...].astype(F32)
        s = _sigmoid(gv)
        dg_ref[...] = (da * uv * (s * (1.0 + gv * (1.0 - s)))).astype(dg_ref.dtype)
        du_ref[...] = (da * gv * s).astype(du_ref.dtype)

    ospec = pl.BlockSpec((tm, tn), lambda j, i: (i, j))
    return pl.pallas_call(
        body, name=name, grid=(n // tn, m // tm),
        in_specs=[pl.BlockSpec((tm, kdim), lambda j, i: (i, 0)), pl.BlockSpec((tn, kdim), lambda j, i: (j, 0)),
                  ospec, ospec],
        out_specs=[ospec] * 2, out_shape=[jax.ShapeDtypeStruct((m, n), BF16)] * 2,
        compiler_params=_cparams(("parallel", "parallel")),
    )(dh, wo, g, u)


def _adamw_math(wv, gv, mv, vv):
    mn = ADAM_B1 * mv + (1.0 - ADAM_B1) * gv
    vn = ADAM_B2 * vv + (1.0 - ADAM_B2) * (gv * gv)
    m_hat = mn / (1.0 - ADAM_B1 ** ADAM_STEP)
    v_hat = vn / (1.0 - ADAM_B2 ** ADAM_STEP)
    return -ADAM_LR * (m_hat / (jnp.sqrt(v_hat) + ADAM_EPS) + ADAM_WD * wv), mn, vn


def _adamw(w, g, m, v, name):
    def fn(row0, ins, bc):
        return list(_adamw_math(*ins)), []
    c = w.shape[1]
    return _rowwise(fn, [w, g, m, v], [], [(c, F32)] * 3, [], name=name)[0]


def _adamw_halves(w, mine, theirs, m, v, name):
    rows, cols = w.shape
    half = rows // 2
    tm = _tile(half, cap=(10 * 1024 * 1024) // (9 * 4 * cols))
    nb = half // tm

    def body(c_ref, w_ref, g1_ref, g2_ref, m_ref, v_ref, g_out, d_out, m_out, v_out):
        own = (pl.program_id(0) // nb) == c_ref[0]
        g = jnp.where(own, g1_ref[...], g2_ref[...])
        delta, mn, vn = _adamw_math(w_ref[...], g, m_ref[...], v_ref[...])
        g_out[...] = g
        d_out[...] = delta
        m_out[...] = mn
        v_out[...] = vn

    full = pl.BlockSpec((tm, cols), lambda i, c: (i, 0))
    part = pl.BlockSpec((tm, cols), lambda i, c: (lax.rem(i, nb), 0))
    return pl.pallas_call(
        body, name=name,
        grid_spec=pltpu.PrefetchScalarGridSpec(num_scalar_prefetch=1, grid=(2 * nb,),
                                               in_specs=[full, part, part, full, full], out_specs=[full] * 4),
        out_shape=[jax.ShapeDtypeStruct((rows, cols), F32)] * 4,
        compiler_params=_cparams(("parallel",)),
    )(_mesh_scalar(lax.axis_index("c")), w, mine, theirs, m, v)


def _head_sum(x, gmat):
    hi = x.astype(BF16)
    lo = (x - hi.astype(F32)).astype(BF16)
    w = gmat.shape[0]
    return jnp.concatenate(
        [jnp.dot(hi[:, b:b + w], gmat, preferred_element_type=F32) + jnp.dot(lo[:, b:b + w], gmat,
                                                                             preferred_element_type=F32)
         for b in range(0, x.shape[1], w)], axis=1)


def _split3(x):
    hi = x.astype(BF16).astype(F32)
    r = x - hi
    mid = r.astype(BF16).astype(F32)
    return hi, mid, r - mid


def _extra_base(hh):
    return FOX_DH * (1 - hh)


def _data_mask(hh):
    lane = _iota((1, LANES), 1)
    return (lane >= FOX_DH * hh) & (lane < FOX_DH * (hh + 1))


def _with_extras(data, hh, vals):
    lane = _iota((1, LANES), 1)
    x = jnp.zeros_like(data)
    for e, v in enumerate(vals):
        x = jnp.where(lane == _extra_base(hh) + e, v, x)
    return jnp.where(_data_mask(hh), data, x)


def _fox_pack_fwd(q_raw, k_raw, v, cq, ck, qg, kg, gmat):
    scale2 = FOX_DH ** -0.5 * LOG2E

    def fn(row0, refs, bc):
        q_ref, k_ref, v_ref, cq_ref, ck_ref = refs
        g_q, g_k, gm = bc
        qv, kv = q_ref[...], k_ref[...]
        qn = qv * lax.rsqrt(_head_sum(qv * qv, gm) * (1.0 / FOX_DH) + EPS) * (g_q * scale2)
        kn = kv * lax.rsqrt(_head_sum(kv * kv, gm) * (1.0 / FOX_DH) + EPS) * g_k
        qs, ks, vs = [], [], []
        for h in range(FOX_H):
            p, hh = divmod(h, 2)
            sl = slice(p * LANES, (p + 1) * LANES)
            cq3 = _split3(cq_ref[:, h:h + 1] * LOG2E)
            ck3 = _split3(ck_ref[:, h:h + 1] * (-LOG2E))
            qs.append(_with_extras(qn[:, sl], hh, [*cq3, 1.0, 1.0, 1.0]))
            ks.append(_with_extras(kn[:, sl], hh, [1.0, 1.0, 1.0, *ck3]))
            vs.append(_with_extras(v_ref[:, sl].astype(F32), hh, [1.0, 1.0]))
        return [jnp.concatenate(qs, axis=1), jnp.concatenate(ks, axis=1), jnp.concatenate(vs, axis=1)], []

    w = FOX_H * LANES
    return _rowwise(fn, [q_raw, k_raw, v, cq, ck], [qg, kg, gmat], [(w, BF16)] * 3, [], name="fox_pack_fwd",
                    as_refs=True)[0]


def _fox_pack_bias(qp, cq, lse2):
    def fn(row0, refs, bc):
        q_ref, cq_ref, lse_ref = refs
        lane = _iota((1, LANES), 1)
        outs = []
        for h in range(FOX_H):
            blk = q_ref[:, h * LANES:(h + 1) * LANES].astype(F32)
            for e, part in enumerate(_split3(cq_ref[:, h:h + 1] * LOG2E - lse_ref[:, h:h + 1])):
                blk = jnp.where(lane == _extra_base(h % 2) + e, part, blk)
            outs.append(blk)
        return [jnp.concatenate(outs, axis=1)], []
    return _rowwise(fn, [qp, cq, lse2], [], [(FOX_H * LANES, BF16)], [], name="fox_pack_bias", as_refs=True)[0][0]


def _fox_pack_bwd(dog, o, gate):
    def fn(row0, refs, bc):
        d_ref, o_ref, g_ref = refs
        dos, dgs = [], []
        for p in range(FOX_H // 2):
            sl = slice(p * LANES, (p + 1) * LANES)
            dv, ov, gv = (r[:, sl].astype(F32) for r in (d_ref, o_ref, g_ref))
            s = _sigmoid(gv)
            do = dv * s
            dgs.append(dv * ov * s * (1.0 - s))
            od = ov * do
            for hh in range(2):
                delta = jnp.sum(jnp.where(_data_mask(hh), od, 0.0), axis=-1, keepdims=True)
                hi = delta.astype(BF16).astype(F32)
                dos.append(_with_extras(do, hh, [-hi, hi - delta]))
        return [jnp.concatenate(dos, axis=1), jnp.concatenate(dgs, axis=1)], []
    return _rowwise(fn, [dog, o, gate], [], [(FOX_H * LANES, BF16), (D, BF16)], [], name="fox_pack_bwd",
                    as_refs=True)[0]


def _fox_unpack_bwd(q_raw, k_raw, dqp, dk, qg, kg, gmat):
    scale = FOX_DH ** -0.5

    def fn(row0, refs, bc):
        q_ref, k_ref, dq_ref, dk_ref = refs
        g_q, g_k, gm = bc
        lane = _iota((1, LANES), 1)
        dqs = []
        dcq = jnp.zeros((q_ref.shape[0], LANES), F32)
        for p in range(FOX_H // 2):
            even = dq_ref[:, (2 * p) * LANES:(2 * p + 1) * LANES]
            odd = dq_ref[:, (2 * p + 1) * LANES:(2 * p + 2) * LANES]
            dqs.append(jnp.where(_data_mask(0), even, odd) * scale)
            for hh in range(2):
                col = (2 * p + hh) * LANES + _extra_base(hh)
                dcq = jnp.where(lane == 2 * p + hh, dq_ref[:, col:col + 1], dcq)
        outs, accs = [], []
        for xv, dy, g in ((q_ref[...], jnp.concatenate(dqs, axis=1), g_q), (k_ref[...], dk_ref[...] * (1.0 / LOG2E), g_k)):
            r = lax.rsqrt(_head_sum(xv * xv, gm) * (1.0 / FOX_DH) + EPS)
            xh = xv * r
            dxh = dy * g
            outs.append(r * (dxh - xh * (_head_sum(dxh * xh, gm) * (1.0 / FOX_DH))))
            accs.append(jnp.sum(dy * xh, axis=0, keepdims=True))
        return outs + [dcq], accs
    return _rowwise(fn, [q_raw, k_raw, dqp, dk], [qg, kg, gmat], [(D, BF16), (D, BF16), (LANES, F32)],
                    [(1, D), (1, D)], name="fox_unpack_bwd", as_refs=True)


def _fox_cumsum_fwd(flog, bf):
    def fn(row0, ins, bc, carry):
        (f,), (b,) = ins, bc
        tm = f.shape[0]
        keep = _row_ids(row0, tm) >= ROW0
        lf = jnp.where(keep, _log_sigmoid(f + b), 0.0)
        tri = (_iota((tm, tm), 0) >= _iota((tm, tm), 1)).astype(F32)
        c = jnp.dot(tri, lf, precision=HIGHEST, preferred_element_type=F32) + carry[...]
        carry[...] = carry[...] + jnp.sum(lf, axis=0, keepdims=True)
        return [c, jnp.where(keep, c, BIG)], []
    return _rowwise(fn, [flog], [bf], [(LANES, F32), (LANES, F32)], [], name="fox_cumsum_fwd",
                    carry=(1, LANES))[0]


def _fox_cumsum_bwd(dc_q, dc_k, flog, bf):
    def fn(row0, ins, bc, carry):
        (dq, dk, f), (b,) = ins, bc
        d = dq + dk
        tm = f.shape[0]
        keep = _row_ids(row0, tm) >= ROW0
        triu = (_iota((tm, tm), 0) <= _iota((tm, tm), 1)).astype(F32)
        dlf = jnp.dot(triu, d, precision=HIGHEST, preferred_element_type=F32) + carry[...]
        carry[...] = carry[...] + jnp.sum(d, axis=0, keepdims=True)
        dfl = jnp.where(keep, dlf * _sigmoid(-(f + b)), 0.0)
        return [dfl], [jnp.sum(dfl, axis=0, keepdims=True)]
    (dflog,), (dbf,) = _rowwise(fn, [dc_q, dc_k, flog], [bf], [(LANES, F32)], [(1, LANES)], name="fox_cumsum_bwd",
                                reverse=True, carry=(1, LANES))
    return dflog, dbf


def _causal_steps(n, key_major):
    if key_major:
        pairs = [(i, j) for j in range(n) for i in range(j, n)]
    else:
        pairs = [(i, j) for i in range(n) for j in range(i + 1)]
    return (jnp.asarray(np.array([p[0] for p in pairs], np.int32)),
            jnp.asarray(np.array([p[1] for p in pairs], np.int32)))


def _fox_attn_fwd(qp, kp, vp, gate, shards):
    L = qp.shape[0]
    t = _tile(L, ATTN_TILES)
    n = L // t
    hps = FOX_HPS_FWD
    P = FOX_H // hps
    it, jt = _causal_steps(n, False)
    n_steps = it.shape[0]
    ns = len(shards)

    def body(it_ref, jt_ref, q_ref, k_ref, v_ref, g_ref, *rest):
        sh_in, (o_ref, og_ref, lse_ref), sh_out = rest[:ns], rest[ns:ns + 3], rest[ns + 3:2 * ns + 3]
        m_sc, acc, ssem, rsem, lsem = rest[2 * ns + 3:]
        step = pl.program_id(1)
        i, j = it_ref[step], jt_ref[step]
        first = (pl.program_id(0) == 0) & (step == 0)
        last = (pl.program_id(0) == P - 1) & (step == n_steps - 1)

        @pl.when(first)
        def _():
            _gather_start(sh_in, sh_out, ssem, rsem, lsem)

        @pl.when(j == 0)
        def _():
            m_sc[...] = jnp.full_like(m_sc, -3.0e38)
            acc[...] = jnp.zeros_like(acc)

        def update(masked):
            def scores(hh):
                sl = slice(hh * LANES, (hh + 1) * LANES)
                s2 = lax.dot_general(k_ref[:, sl], q_ref[:, sl], NT, preferred_element_type=F32)
                if masked:
                    s2 = jnp.where(_iota((t, t), 0) <= _iota((t, t), 1), s2, -jnp.inf)
                return s2

            nxt = scores(0)
            for hh in range(hps):
                sl = slice(hh * LANES, (hh + 1) * LANES)
                s2 = nxt
                if hh + 1 < hps:
                    nxt = scores(hh + 1)
                m_old = m_sc[hh]
                m_new = jnp.maximum(m_old, jnp.max(s2, axis=0, keepdims=True))
                p = jnp.exp2(s2 - m_new)
                acc[hh] = jnp.exp2(m_old - m_new) * acc[hh] + lax.dot_general(v_ref[:, sl], p.astype(BF16), TN,
                                                                              preferred_element_type=F32)
                m_sc[hh] = m_new

        @pl.when(j < i)
        def _():
            update(False)

        @pl.when(j == i)
        def _():
            update(True)
            outs = []
            for hh in range(hps):
                base = _extra_base(hh % 2)
                l = acc[hh, base:base + 1, :]
                outs.append((acc[hh] / l).T)
                lse_ref[0, hh:hh + 1, :] = m_sc[hh] + jnp.log2(l)
            o = jnp.concatenate([jnp.where(_data_mask(0), outs[a], outs[a + 1]) for a in range(0, hps, 2)], axis=1)
            o_ref[...] = o.astype(o_ref.dtype)
            og_ref[...] = (o * _sigmoid(g_ref[...].astype(F32))).astype(og_ref.dtype)

        @pl.when(last)
        def _():
            _gather_wait(sh_in, sh_out, ssem, rsem, lsem)

    qspec = pl.BlockSpec((t, hps * LANES), lambda p, s, it, jt: (it[s], p))
    kspec = pl.BlockSpec((t, hps * LANES), lambda p, s, it, jt: (jt[s], p))
    ospec = pl.BlockSpec((t, hps * FOX_DH), lambda p, s, it, jt: (it[s], p))
    lspec = pl.BlockSpec((1, hps, t), lambda p, s, it, jt: (p, 0, it[s]))
    res = pl.pallas_call(
        body, name="fox_attn_fwd",
        grid_spec=pltpu.PrefetchScalarGridSpec(
            num_scalar_prefetch=2, grid=(P, n_steps),
            in_specs=[qspec, kspec, kspec, ospec] + [ANY] * ns, out_specs=[ospec, ospec, lspec] + [ANY] * ns,
            scratch_shapes=[pltpu.VMEM((hps, 1, t), F32), pltpu.VMEM((hps, LANES, t), F32)] + _gather_sems(ns)),
        out_shape=[jax.ShapeDtypeStruct((L, D), BF16), jax.ShapeDtypeStruct((L, D), BF16),
                   jax.ShapeDtypeStruct((P, hps, L), F32)]
        + [jax.ShapeDtypeStruct((4,) + a.shape, a.dtype) for a in shards],
        compiler_params=_cparams(("arbitrary", "arbitrary")),
    )(it, jt, qp, kp, vp, gate, *shards)
    return res[0], res[1], res[2], res[3:]


def _fox_attn_bwd(qb, kp, vp, dop, slabs):
    L = qb.shape[0]
    t = _tile(L, ATTN_TILES)
    n = L // t
    P = FOX_H // 2
    it, jt = _causal_steps(n, True)
    n_steps = it.shape[0]
    ns = len(slabs)

    def body(it_ref, jt_ref, q_ref, k_ref, v_ref, do_ref, *rest):
        sl_in, (dq_ref, dk_ref, dv_ref, dck_ref), sl_out = rest[:ns], rest[ns:ns + 4], rest[ns + 4:2 * ns + 4]
        dk_acc, dv_acc, ssem, rsem = rest[2 * ns + 4:]
        step = pl.program_id(1)
        i, j = it_ref[step], jt_ref[step]

        @pl.when((pl.program_id(0) == 0) & (step == 0))
        def _():
            for cp in _scatter_copies(sl_in, sl_out, ssem, rsem):
                cp.start()

        @pl.when(step == 0)
        def _():
            dq_ref[...] = jnp.zeros_like(dq_ref)

        @pl.when(i == j)
        def _():
            dk_acc[...] = jnp.zeros_like(dk_acc)
            dv_acc[...] = jnp.zeros_like(dv_acc)

        def update(masked):
            rows = pl.ds(pl.multiple_of(i * t, LANES), t)
            for hh in range(2):
                sl = slice(hh * LANES, (hh + 1) * LANES)
                q, k, dov = q_ref[:, sl], k_ref[:, sl], do_ref[:, sl]
                s2 = lax.dot_general(k, q, NT, preferred_element_type=F32)
                if masked:
                    s2 = jnp.where(_iota((t, t), 0) <= _iota((t, t), 1), s2, -jnp.inf)
                p = jnp.exp2(s2)
                ds = (p * lax.dot_general(v_ref[:, sl], dov, NT, preferred_element_type=F32)).astype(BF16)
                dv_acc[hh] += jnp.dot(p.astype(BF16), dov, preferred_element_type=F32)
                dk_acc[hh] += jnp.dot(ds, q, preferred_element_type=F32)
                dq_ref[rows, sl] += lax.dot_general(ds, k, TN, preferred_element_type=F32)

        @pl.when(i > j)
        def _():
            update(False)

        @pl.when(i == j)
        def _():
            update(True)

        @pl.when(i == n - 1)
        def _():
            dk_ref[...] = jnp.where(_data_mask(0), dk_acc[0], dk_acc[1])
            dv_ref[...] = jnp.where(_data_mask(0), dv_acc[0], dv_acc[1]).astype(dv_ref.dtype)
            col_sums = [dk_acc[hh, :, _extra_base(hh) + 3:_extra_base(hh) + 4] for hh in range(2)]
            dck_ref[0] = -jnp.where(_iota((1, 2), 1) == 0, col_sums[0], col_sums[1])

        @pl.when((pl.program_id(0) == P - 1) & (step == n_steps - 1))
        def _():
            for cp in _scatter_copies(sl_in, sl_out, ssem, rsem):
                cp.wait()

    qspec = pl.BlockSpec((t, 2 * LANES), lambda p, s, it, jt: (it[s], p))
    kspec = pl.BlockSpec((t, 2 * LANES), lambda p, s, it, jt: (jt[s], p))
    ospec = pl.BlockSpec((t, LANES), lambda p, s, it, jt: (jt[s], p))
    res = pl.pallas_call(
        body, name="fox_attn_bwd",
        grid_spec=pltpu.PrefetchScalarGridSpec(
            num_scalar_prefetch=2, grid=(P, n_steps),
            in_specs=[qspec, kspec, kspec, qspec] + [ANY] * ns,
            out_specs=[pl.BlockSpec((L, 2 * LANES), lambda p, s, it, jt: (0, p)), ospec, ospec,
                       pl.BlockSpec((1, t, 2), lambda p, s, it, jt: (p, jt[s], 0))] + [ANY] * ns,
            scratch_shapes=[pltpu.VMEM((2, t, LANES), F32), pltpu.VMEM((2, t, LANES), F32),
                            pltpu.SemaphoreType.DMA((ns, 3)), pltpu.SemaphoreType.DMA((ns, 3))]),
        out_shape=[jax.ShapeDtypeStruct((L, FOX_H * LANES), F32), jax.ShapeDtypeStruct((L, D), F32),
                   jax.ShapeDtypeStruct((L, D), BF16), jax.ShapeDtypeStruct((P, L, 2), F32)]
        + [jax.ShapeDtypeStruct((3,) + a.shape[1:], a.dtype) for a in slabs],
        compiler_params=_cparams(("arbitrary", "arbitrary")),
    )(it, jt, qb, kp, vp, dop, *slabs)
    return res[0], res[1], res[2], res[3], res[4:]


def _hgrn_consts():
    C = HG_C
    r = np.arange(C)[:, None]
    j = np.arange(C)[None, :]
    mats = [j <= r, j > r]
    masks = []
    n = C
    while n >= 2:
        half = n // 2
        mid = (r // n) * n + half - 1
        second = (r % n) >= half
        mats.append(np.where(second, (j > mid) & (j <= r), (j > r) & (j <= mid)))
        masks.append(((r // n) == (j // n)) & ((r % n) >= half) & ((j % n) < half))
        n //= 2
    return (jnp.asarray(np.concatenate(mats, 0).astype(np.float32), BF16),
            jnp.asarray(np.stack(masks).astype(np.float32), F32))


def _hg_pre(hq, hz, h0, h1):
    mx = jnp.maximum(h0, h1)
    e0, e1 = jnp.exp(h0 - mx), jnp.exp(h1 - mx)
    lb = e1 / (e0 + e1)
    sq = _sigmoid(hq)
    sz = _sigmoid(hz)
    snz = 1.0 - sz
    k = (1.0 - lb) * snz
    g = jnp.maximum(jnp.log(lb + (1.0 - lb) * sz), -BIG)
    return lb, hq * sq, sq, k, sz, snz, g


def _hg_decays(g, rmat):
    hi = g.astype(BF16)
    lo = (g - hi.astype(F32)).astype(BF16)
    d = jnp.dot(rmat, jnp.concatenate([hi, lo], axis=1), preferred_element_type=F32)
    return jnp.exp(d[:, :HG_D] + d[:, HG_D:])


def _interleave(programs):
    live = list(programs)
    while live:
        for g in list(live):
            try:
                next(g)
            except StopIteration:
                live.remove(g)


def _hg_intra_levels(q, k, fall, masks):
    C = HG_C
    eye = _iota((C, C), 0) == _iota((C, C), 1)
    a = jnp.where(eye, jnp.sum(q * k, axis=-1, keepdims=True), 0.0)
    for l in range(HG_LEV):
        f = fall[(2 + l) * C:(3 + l) * C]
        a = a + masks[l] * lax.dot_general((q * f).astype(BF16), (k * f).astype(BF16), NT,
                                           preferred_element_type=F32)
        yield a


def _hgrn_specs(n_chunks, reverse):
    C = HG_C
    w = HG_HPS * HG_D

    def col(first_head):
        off = first_head // HG_HPS
        if reverse:
            return pl.BlockSpec((C, w), lambda h, c: (n_chunks - 1 - c, off + h))
        return pl.BlockSpec((C, w), lambda h, c: (c, off + h))

    st = pl.BlockSpec((HG_HPS, 1, HG_D, HG_D),
                      (lambda h, c: (h, n_chunks - 1 - c, 0, 0)) if reverse else (lambda h, c: (h, c, 0, 0)))
    consts = [pl.BlockSpec((2, w), lambda h, c: (0, h)), pl.BlockSpec((1, HG_D), lambda h, c: (0, 0)),
              pl.BlockSpec(((2 + HG_LEV) * C, C), lambda h, c: (0, 0)),
              pl.BlockSpec((HG_LEV, C, C), lambda h, c: (0, 0, 0))]
    return col, st, consts


def _hgrn_fwd(proj, hlb, gg, rmat, masks):
    L = proj.shape[0]
    C = HG_C
    nc = L // C
    col, st, consts = _hgrn_specs(nc, False)

    def body(hq_ref, hz_ref, hi_ref, hg_ref, hlb_ref, gg_ref, r_ref, m_ref, og_ref, st_ref, state):
        c = pl.program_id(1)

        @pl.when(c == 0)
        def _():
            state[...] = jnp.zeros_like(state)

        def head(hh):
            sl = slice(hh * HG_D, (hh + 1) * HG_D)
            v, hg = hi_ref[:, sl], hg_ref[:, sl]
            _, q, _, k, _, _, g = _hg_pre(hq_ref[:, sl], hz_ref[:, sl], hlb_ref[0:1, sl], hlb_ref[1:2, sl])
            yield
            fall = _hg_decays(g, r_ref[...])
            fb, fe = fall[0:C], fall[C:2 * C]
            st0 = state[hh]
            st_ref[hh, 0] = st0
            yield
            for a in _hg_intra_levels(q, k, fall, m_ref[...]):
                yield
            vb = v.astype(BF16)
            o = jnp.dot(a.astype(BF16), vb, preferred_element_type=F32)
            o = o + lax.dot_general((q * fb).astype(BF16), st0.astype(BF16), NT, preferred_element_type=F32)
            yield
            ebc = jnp.exp(jnp.sum(g, axis=0, keepdims=True))
            state[hh] = st0 * ebc + lax.dot_general(vb, (k * fe).astype(BF16), TN, preferred_element_type=F32)
            r = lax.rsqrt(jnp.mean(o * o, axis=-1, keepdims=True) + EPS)
            og_ref[:, sl] = (o * r * gg_ref[...] * (hg * _sigmoid(hg))).astype(og_ref.dtype)

        _interleave(head(hh) for hh in range(HG_HPS))

    return pl.pallas_call(
        body, name="hgrn_fwd", grid=(HG_H // HG_HPS, nc),
        in_specs=[col(0), col(HG_H), col(2 * HG_H), col(3 * HG_H)] + consts,
        out_specs=[col(0), st],
        out_shape=[jax.ShapeDtypeStruct((L, D), BF16), jax.ShapeDtypeStruct((HG_H, nc, HG_D, HG_D), F32)],
        scratch_shapes=[pltpu.VMEM((HG_HPS, HG_D, HG_D), F32)],
        compiler_params=_cparams(("parallel", "arbitrary")),
    )(proj, proj, proj, proj, hlb, gg, rmat, masks)


def _hgrn_bwd(proj, dog, states, hlb, gg, rmat, masks):
    L = proj.shape[0]
    C = HG_C
    nc = L // C
    col, st, consts = _hgrn_specs(nc, True)

    def body(hq_ref, hz_ref, hi_ref, hg_ref, do_ref, hlb_ref, gg_ref, r_ref, m_ref, st_ref,
             dq_ref, dz_ref, di_ref, dg_ref, dlb_ref, dgg_ref, dstate):
        c = pl.program_id(1)

        @pl.when(c == 0)
        def _():
            dstate[...] = jnp.zeros_like(dstate)
            dlb_ref[...] = jnp.zeros_like(dlb_ref)
            dgg_ref[...] = jnp.zeros_like(dgg_ref)

        _interleave(bwd_head(c, hh, slice(hh * HG_D, (hh + 1) * HG_D), hq_ref, hz_ref, hi_ref, hg_ref, do_ref, hlb_ref,
                             gg_ref, r_ref, m_ref, st_ref, dq_ref, dz_ref, di_ref, dg_ref, dlb_ref, dgg_ref, dstate)
                    for hh in range(HG_HPS))

    def bwd_head(c, hh, sl, hq_ref, hz_ref, hi_ref, hg_ref, do_ref, hlb_ref, gg_ref, r_ref, m_ref, st_ref,
                 dq_ref, dz_ref, di_ref, dg_ref, dlb_ref, dgg_ref, dstate):
        hq, hz, v, hg = hq_ref[:, sl], hz_ref[:, sl], hi_ref[:, sl], hg_ref[:, sl]
        dout = do_ref[:, sl].astype(F32)
        gain = gg_ref[...]
        masks_v = m_ref[...]
        lb, q, sq, k, sz, snz, g = _hg_pre(hq, hz, hlb_ref[0:1, sl], hlb_ref[1:2, sl])
        yield
        fall = _hg_decays(g, r_ref[...])
        fb, fe = fall[0:C], fall[C:2 * C]
        yield
        for a in _hg_intra_levels(q, k, fall, masks_v):
            yield
        st0 = st_ref[hh, 0]
        st0b = st0.astype(BF16)
        ebc = jnp.exp(jnp.sum(g, axis=0, keepdims=True))
        qb, ke, vb = (q * fb).astype(BF16), (k * fe).astype(BF16), v.astype(BF16)
        ab = a.astype(BF16)
        o = jnp.dot(ab, vb, preferred_element_type=F32) + lax.dot_general(qb, st0b, NT, preferred_element_type=F32)
        yield
        r = lax.rsqrt(jnp.mean(o * o, axis=-1, keepdims=True) + EPS)
        oh = o * r
        sg = _sigmoid(hg)
        d_on = dout * (hg * sg)
        dhg = dout * (oh * gain) * (sg * (1.0 + hg * (1.0 - sg)))
        dgg_ref[hh] += jnp.sum(d_on * oh, axis=0, keepdims=True)
        dxh = d_on * gain
        do = r * (dxh - oh * jnp.mean(dxh * oh, axis=-1, keepdims=True))
        dob = do.astype(BF16)
        yield
        dsp = dstate[hh]
        dspb = dsp.astype(BF16)
        causal = _iota((C, C), 0) >= _iota((C, C), 1)
        da = jnp.where(causal, lax.dot_general(dob, vb, NT, preferred_element_type=F32), 0.0)
        diag = jnp.sum(do * v, axis=-1, keepdims=True)
        dv = lax.dot_general(ab, dob, TN, preferred_element_type=F32)
        dv = dv + lax.dot_general(ke, dspb, NT, preferred_element_type=F32)
        xq = jnp.dot(dob, st0b, preferred_element_type=F32)
        xk = jnp.dot(vb, dspb, preferred_element_type=F32)
        dq = diag * k + fb * xq
        dk = diag * q + fe * xk
        ke_xk = ke.astype(F32) * xk
        db = qb.astype(F32) * xq - ke_xk
        yield
        for l in range(HG_LEV):
            f = fall[(2 + l) * C:(3 + l) * C]
            dal = (masks_v[l] * da).astype(BF16)
            ql, kl = (q * f).astype(BF16), (k * f).astype(BF16)
            xq = jnp.dot(dal, kl, preferred_element_type=F32)
            xk = lax.dot_general(dal, ql, TN, preferred_element_type=F32)
            dq = dq + f * xq
            dk = dk + f * xk
            db = db + ql.astype(F32) * xq - kl.astype(F32) * xk
            yield
        dstate[hh] = dsp * ebc + lax.dot_general(dob, qb, TN, preferred_element_type=F32)
        triu = (_iota((C, C), 0) <= _iota((C, C), 1)).astype(F32)
        dg = jnp.dot(triu, db, precision=HIGHEST, preferred_element_type=F32)
        dg = dg + jnp.sum(st0 * ebc * dsp, axis=0, keepdims=True) + jnp.sum(ke_xk, axis=0, keepdims=True)
        keep = _row_ids((nc - 1 - c) * C, C) >= ROW0
        dg = jnp.where(keep, dg, 0.0)
        dk = jnp.where(keep, dk, 0.0)
        f_gate = lb + (1.0 - lb) * sz
        dfdz = (1.0 - lb) * sz * snz
        dz_ref[:, sl] = (dg * dfdz / f_gate - dk * dfdz).astype(dz_ref.dtype)
        dlb_ref[:, sl] += jnp.sum(dg * snz / f_gate - dk * snz, axis=0, keepdims=True)
        dq_ref[:, sl] = jnp.where(keep, dq * (sq * (1.0 + hq * (1.0 - sq))), 0.0).astype(dq_ref.dtype)
        di_ref[:, sl] = jnp.where(keep, dv, 0.0).astype(di_ref.dtype)
        dg_ref[:, sl] = jnp.where(keep, dhg, 0.0).astype(dg_ref.dtype)

    w = HG_HPS * HG_D
    outs = pl.pallas_call(
        body, name="hgrn_bwd", grid=(HG_H // HG_HPS, nc),
        in_specs=[col(0), col(HG_H), col(2 * HG_H), col(3 * HG_H), col(0)] + consts + [st],
        out_specs=[col(0), col(0), col(0), col(0), pl.BlockSpec((1, w), lambda h, c: (0, h)),
                   pl.BlockSpec((HG_HPS, 1, HG_D), lambda h, c: (h, 0, 0))],
        out_shape=[jax.ShapeDtypeStruct((L, D), BF16)] * 4 + [jax.ShapeDtypeStruct((1, D), F32),
                                                              jax.ShapeDtypeStruct((HG_H, 1, HG_D), F32)],
        scratch_shapes=[pltpu.VMEM((HG_HPS, HG_D, HG_D), F32)],
        compiler_params=_cparams(("parallel", "arbitrary")),
    )(proj, proj, proj, proj, dog, hlb, gg, rmat, masks, states)
    return outs


def _ffn_fwd(h, norm_gain, wg, wu, wo, tag):
    hn = _rms_fwd(h, norm_gain, f"{tag}_norm")
    g, u, act = _ffn_in(hn, wg, wu, f"{tag}_in")
    h_out = _matmul(act, wo, add=h, name=f"{tag}_out")
    return h_out, (h, hn, g, u, act)


def _ffn_bwd(dh, saved, norm_gain, wg, wu, wo, tag):
    h, hn, g, u, act = saved
    dg, du = _ffn_dact(dh, wo, g, u, f"{tag}_dact")
    d_wo = _matmul(act, dh, ta=True, name=f"{tag}_dwo")
    dhn = _matmul(dg, wg, tb=True, name=f"{tag}_dhn_g")
    dhn = _matmul(du, wu, tb=True, add=dhn, name=f"{tag}_dhn_u")
    d_wg = _matmul(hn, dg, ta=True, name=f"{tag}_dwg")
    d_wu = _matmul(hn, du, ta=True, name=f"{tag}_dwu")
    dh, d_gain = _rms_bwd(h, dhn, norm_gain, dh, f"{tag}_norm_bwd")
    return dh, d_gain, (d_wg, d_wu, d_wo)


def _local_step(h0, tgt, w, late_shards):
    L = h0.shape[0]
    gmat = jnp.asarray(np.kron(np.eye(MXU_N // FOX_DH), np.ones((FOX_DH, FOX_DH))).astype(np.float32), BF16)
    rmat, lmasks = _hgrn_consts()
    an, fn_ = w["attn_norm"], w["ffn_norm"]
    qg = jnp.tile(w["fox_q_norm"], (1, FOX_H))
    kg = jnp.tile(w["fox_k_norm"], (1, FOX_H))
    bf = jnp.pad(w["fox_b_f"], ((0, 0), (0, LANES - FOX_H)))
    fw = w["fox_w_in"]
    f_wq, f_wk, f_wv, f_wg = (fw[:, i * D:(i + 1) * D] for i in range(4))
    f_wf = jnp.pad(fw[:, 4 * D:], ((0, 0), (0, LANES - FOX_H)))

    hn0 = _rms_fwd(h0, an[0:1], "fox_norm")
    q_raw = _matmul(hn0, f_wq, name="fox_q")
    k_raw = _matmul(hn0, f_wk, name="fox_k")
    v = _matmul(hn0, f_wv, out_dtype=BF16, name="fox_v")
    gate = _matmul(hn0, f_wg, out_dtype=BF16, name="fox_gate")
    flog = _matmul(hn0, f_wf, name="fox_flog")
    cq, ck = _fox_cumsum_fwd(flog, bf)
    qp, kp, vp = _fox_pack_fwd(q_raw, k_raw, v, cq, ck, qg, kg, gmat)
    o, og, lse2, gathered = _fox_attn_fwd(qp, kp, vp, gate, [late_shards[n] for n in GATHER_LATE])
    late = dict(zip(GATHER_LATE, gathered))
    f_wo = late["fox_w_out"].reshape(D, D)
    h_wo = late["hgrn_w_out"].reshape(D, D)
    h_wi = jnp.concatenate(list(late["hgrn_w_in"]), axis=1)
    g_in, g_out = late["ffn_w_in"], late["ffn_w_out"]
    ffw = []
    for i in range(2):
        rows_in, rows_out = slice(i * D, (i + 1) * D), slice(i * FFN // 4, (i + 1) * FFN // 4)
        ffw.append((jnp.concatenate([g_in[0, rows_in], g_in[1, rows_in]], axis=1),
                    jnp.concatenate([g_in[2, rows_in], g_in[3, rows_in]], axis=1),
                    jnp.concatenate([g_out[j, rows_out] for j in range(4)], axis=0)))
    h1 = _matmul(og, f_wo, add=h0, name="fox_out")
    h2, ffn0 = _ffn_fwd(h1, fn_[0:1], *ffw[0], "ffn0")

    hn2 = _rms_fwd(h2, an[1:2], "hgrn_norm")
    proj = _matmul(hn2, h_wi, name="hgrn_in")
    og1, states = _hgrn_fwd(proj, w["hgrn_lower_bounds"], w["hgrn_g_norm"], rmat, lmasks)
    h3 = _matmul(og1, h_wo, add=h2, name="hgrn_out")
    h4, ffn1 = _ffn_fwd(h3, fn_[1:2], *ffw[1], "ffn1")

    loss, dh, d_final = _loss_bwd(h4, tgt, w["final_norm"])

    dh, d_fn1, d_ffn1 = _ffn_bwd(dh, ffn1, fn_[1:2], *ffw[1], "ffn1")
    dog1 = _matmul(dh, h_wo, tb=True, out_dtype=BF16, name="hgrn_dog")
    d_h_wo = _matmul(og1, dh, ta=True, name="hgrn_dwo")
    dpq, dpz, dpi, dpg, d_lb, d_gg = _hgrn_bwd(proj, dog1, states, w["hgrn_lower_bounds"], w["hgrn_g_norm"],
                                               rmat, lmasks)
    dproj = jnp.concatenate([dpq, dpz, dpi, dpg], axis=1)
    dhn2 = _matmul(dproj, h_wi, tb=True, name="hgrn_dhn")
    d_h_wi = _matmul(hn2, dproj, ta=True, name="hgrn_dwi")
    dh, d_an1 = _rms_bwd(h2, dhn2, an[1:2], dh, "hgrn_norm_bwd")

    dh, d_fn0, d_ffn0 = _ffn_bwd(dh, ffn0, fn_[0:1], *ffw[0], "ffn0")
    n_in, n_out = 2 * FFN // 4, FFN // 4
    d_ffn = [d_ffn0, d_ffn1]
    late_grads = dict(
        hgrn_w_in=_to_shards("hgrn_w_in", d_h_wi[None]), hgrn_w_out=d_h_wo.reshape(4, D // 4, D),
        ffn_w_in=jnp.stack([jnp.concatenate([d[j // 2][:, (j % 2) * n_in:(j % 2 + 1) * n_in] for d in d_ffn], axis=0)
                            for j in range(4)]),
        ffn_w_out=jnp.stack([jnp.concatenate([d[2][j * n_out:(j + 1) * n_out] for d in d_ffn], axis=0)
                             for j in range(4)]))
    pair_late, send_late = _pair_sums([late_grads[n] for n in LATE_NAMES], "late")

    dog = _matmul(dh, f_wo, tb=True, out_dtype=BF16, name="fox_dog")
    d_f_wo = _matmul(og, dh, ta=True, name="fox_dwo")

    def by_head(a):
        return jnp.pad(a.transpose(1, 0, 2).reshape(L, FOX_H), ((0, 0), (0, LANES - FOX_H)))

    qb = _fox_pack_bias(qp, cq, by_head(lse2.transpose(0, 2, 1)))
    dop, dgate = _fox_pack_bwd(dog, o, gate)
    dqp, dk, dv, dck, recv_late = _fox_attn_bwd(qb, kp, vp, dop, send_late)
    (dq_raw, dk_raw, dc_q), (d_qg, d_kg) = _fox_unpack_bwd(q_raw, k_raw, dqp, dk, qg, kg, gmat)
    dflog, d_bf = _fox_cumsum_bwd(dc_q, by_head(dck), flog, bf)
    dproj0 = jnp.concatenate([dq_raw, dk_raw, dv, dgate, dflog.astype(BF16)], axis=1)
    f_wall = jnp.concatenate([f_wq, f_wk, f_wv, f_wg, f_wf], axis=1)
    dhn0 = _matmul(dproj0, f_wall, tb=True, name="fox_dhn")
    d_f_wall = _matmul(hn0, dproj0, ta=True, name="fox_dwi")
    d_f_wi = d_f_wall[:, :4 * D + FOX_H]
    dh, d_an0 = _rms_bwd(h0, dhn0, an[0:1], dh, "fox_norm_bwd")

    fox_grads = dict(fox_w_in=d_f_wi[None], fox_w_out=d_f_wo[None])
    pair_fox, send_fox = _pair_sums([_to_shards(n, fox_grads[n]) for n in FOX_NAMES], "fox")
    recv_fox = _chip_scatter(send_fox, "fox")
    halves = _chip_sums(pair_fox, recv_fox, "fox") + _chip_sums(pair_late, recv_late, "late")
    theirs = _sibling_exchange(halves)
    big = {n: (m, t) for n, m, t in zip(FOX_NAMES + LATE_NAMES, halves, theirs)}
    small = dict(attn_norm=jnp.concatenate([d_an0, d_an1]), ffn_norm=jnp.concatenate([d_fn0, d_fn1]),
                 final_norm=d_final, lb_raw=d_lb, q_gain=d_qg, k_gain=d_kg, b_f=d_bf,
                 g_gain=d_gg.reshape(1, D))
    return loss, dh, big, small


def _me():
    return lax.axis_index("x"), lax.axis_index("y"), lax.axis_index("c")


def _flip(v, bit):
    return 1 - v if bit else v


def _chip_allgather(arrs):
    n = len(arrs)

    def body(*refs):
        _gather_start(refs[:n], refs[n:2 * n], *refs[2 * n:])
        _gather_wait(refs[:n], refs[n:2 * n], *refs[2 * n:])

    return pl.pallas_call(
        body, name="chip_allgather", in_specs=[ANY] * n, out_specs=[ANY] * n,
        out_shape=[jax.ShapeDtypeStruct((4,) + a.shape, a.dtype) for a in arrs],
        scratch_shapes=_gather_sems(n),
    )(*arrs)


def _chip_peers():
    x, y, c = _me()
    return [(1 - x, y, c), (x, 1 - y, c), (1 - x, 1 - y, c)]


def _gather_sems(n):
    return [pltpu.SemaphoreType.DMA((n, 3)), pltpu.SemaphoreType.DMA((n, 3)), pltpu.SemaphoreType.DMA((n,))]


def _gather_copies(ins, outs, ssem, rsem, lsem, with_recvs):
    x, y, _ = _me()
    local, sends, recvs = [], [], []
    for a in range(len(ins)):
        local.append(pltpu.make_async_copy(ins[a], outs[a].at[2 * x + y], lsem.at[a]))
        for k, peer in enumerate(_chip_peers()):
            sends.append(pltpu.make_async_remote_copy(ins[a], outs[a].at[2 * x + y], ssem.at[a, k], rsem.at[a, k],
                                                      device_id=peer, device_id_type=MESH))
            if with_recvs:
                recvs.append(pltpu.make_async_remote_copy(ins[a], outs[a].at[2 * peer[0] + peer[1]], ssem.at[a, k],
                                                          rsem.at[a, k], device_id=peer, device_id_type=MESH))
    return local, sends, recvs


def _gather_start(ins, outs, ssem, rsem, lsem):
    local, sends, _ = _gather_copies(ins, outs, ssem, rsem, lsem, False)
    for cp in local + sends:
        cp.start()


def _gather_wait(ins, outs, ssem, rsem, lsem):
    local, sends, recvs = _gather_copies(ins, outs, ssem, rsem, lsem, True)
    for cp in local:
        cp.wait()
    for cp in sends:
        cp.wait_send()
    for cp in recvs:
        cp.wait_recv()


def _scatter_copies(ins, outs, ssem, rsem):
    copies = []
    for a in range(len(ins)):
        for k, peer in enumerate(_chip_peers()):
            copies.append(pltpu.make_async_remote_copy(ins[a].at[2 * peer[0] + peer[1]], outs[a].at[k], ssem.at[a, k],
                                                       rsem.at[a, k], device_id=peer, device_id_type=MESH))
    return copies


def _device_allgather(arr):
    def body(in_ref, out_ref, ssem, rsem, lsem):
        x, y, c = _me()
        me = 4 * x + 2 * y + c
        peers = [(_flip(x, k & 4), _flip(y, k & 2), _flip(c, k & 1)) for k in range(1, 8)]
        local = pltpu.make_async_copy(in_ref, out_ref.at[me], lsem)
        local.start()
        sends = []
        for k, peer in enumerate(peers):
            cp = pltpu.make_async_remote_copy(in_ref, out_ref.at[me], ssem.at[k], rsem.at[k],
                                              device_id=peer, device_id_type=MESH)
            cp.start()
            sends.append(cp)
        local.wait()
        for cp in sends:
            cp.wait_send()
        for k, peer in enumerate(peers):
            pltpu.make_async_remote_copy(in_ref, out_ref.at[4 * peer[0] + 2 * peer[1] + peer[2]], ssem.at[k],
                                         rsem.at[k], device_id=peer, device_id_type=MESH).wait_recv()

    return pl.pallas_call(
        body, name="device_allgather", in_specs=[ANY], out_specs=ANY,
        out_shape=jax.ShapeDtypeStruct((8,) + arr.shape, arr.dtype),
        scratch_shapes=[pltpu.SemaphoreType.DMA((7,)), pltpu.SemaphoreType.DMA((7,)), pltpu.SemaphoreType.DMA],
    )(arr)


def _sibling_send_other_half(arrs, tag):
    n = len(arrs)

    def body(*refs):
        ins, outs = refs[:n], refs[n:2 * n]
        ssem, rsem = refs[2 * n:]
        x, y, c = _me()
        cps = []
        for a in range(n):
            half = ins[a].shape[1] // 2
            src = ins[a].at[:, pl.ds((1 - c) * half, half), :]
            cp = pltpu.make_async_remote_copy(src, outs[a], ssem.at[a], rsem.at[a],
                                              device_id=(x, y, 1 - c), device_id_type=MESH)
            cp.start()
            cps.append(cp)
        for cp in cps:
            cp.wait()

    return pl.pallas_call(
        body, name=f"grad_sibling_swap_{tag}", in_specs=[ANY] * n, out_specs=[ANY] * n,
        out_shape=[jax.ShapeDtypeStruct((4, a.shape[1] // 2, a.shape[2]), a.dtype) for a in arrs],
        scratch_shapes=[pltpu.SemaphoreType.DMA((n,)), pltpu.SemaphoreType.DMA((n,))],
    )(*arrs)


def _chip_scatter(arrs, tag):
    n = len(arrs)

    def body(*refs):
        cps = _scatter_copies(refs[:n], refs[n:2 * n], *refs[2 * n:])
        for cp in cps:
            cp.start()
        for cp in cps:
            cp.wait()

    return pl.pallas_call(
        body, name=f"grad_chip_scatter_{tag}", in_specs=[ANY] * n, out_specs=[ANY] * n,
        out_shape=[jax.ShapeDtypeStruct((3,) + a.shape[1:], a.dtype) for a in arrs],
        scratch_shapes=[pltpu.SemaphoreType.DMA((n, 3)), pltpu.SemaphoreType.DMA((n, 3))],
    )(*arrs)


def _sibling_exchange(arrs):
    n = len(arrs)

    def body(*refs):
        ins, outs = refs[:n], refs[n:2 * n]
        ssem, rsem = refs[2 * n:]
        x, y, c = _me()
        cps = [pltpu.make_async_remote_copy(ins[a], outs[a], ssem.at[a], rsem.at[a], device_id=(x, y, 1 - c),
                                            device_id_type=MESH) for a in range(n)]
        for cp in cps:
            cp.start()
        for cp in cps:
            cp.wait()

    return pl.pallas_call(
        body, name="grad_sibling_exchange", in_specs=[ANY] * n, out_specs=[ANY] * n,
        out_shape=[jax.ShapeDtypeStruct(a.shape, a.dtype) for a in arrs],
        scratch_shapes=[pltpu.SemaphoreType.DMA((n,)), pltpu.SemaphoreType.DMA((n,))],
    )(*arrs)


def _pair_sums(grads, tag):
    got = _sibling_send_other_half(grads, tag)
    res = [_pair_add(g, t, f"grad_pair_add_{tag}{i}") for i, (g, t) in enumerate(zip(grads, got))]
    return [r[0] for r in res], [r[1] for r in res]


def _mesh_scalar(v):
    return jnp.asarray(v, jnp.int32).reshape(1)


def _pair_add(g, t, name):
    _, rows, cols = g.shape
    half = rows // 2
    tm = _tile(half, cap=(2 * 1024 * 1024) // (4 * cols))

    def body(c_ref, g_ref, t_ref, o_ref, ob_ref):
        s = g_ref[0, 0] + t_ref[0]
        o_ref[0] = s
        ob_ref[0] = s.astype(ob_ref.dtype)

    spec = pl.BlockSpec((1, tm, cols), lambda j, i, c: (j, i, 0))
    return pl.pallas_call(
        body, name=name,
        grid_spec=pltpu.PrefetchScalarGridSpec(
            num_scalar_prefetch=1, grid=(4, half // tm),
            in_specs=[pl.BlockSpec((1, 1, tm, cols), lambda j, i, c: (j, c[0], i, 0)), spec],
            out_specs=[spec, spec]),
        out_shape=[jax.ShapeDtypeStruct(t.shape, F32), jax.ShapeDtypeStruct(t.shape, BF16)],
        compiler_params=_cparams(("parallel", "parallel")),
    )(_mesh_scalar(lax.axis_index("c")), g.reshape(4, 2, half, cols), t)


def _chip_sums(pair, recv, tag):
    x, y, _ = _me()
    out = []
    for n, (p, r) in enumerate(zip(pair, recv)):
        _, half, cols = p.shape
        tm = _tile(half, cap=(2 * 1024 * 1024) // (4 * cols))

        def body(j_ref, p_ref, r_ref, o_ref):
            o_ref[...] = p_ref[0] + r_ref[0].astype(F32) + r_ref[1].astype(F32) + r_ref[2].astype(F32)

        out.append(pl.pallas_call(
            body, name=f"grad_chip_add_{tag}{n}",
            grid_spec=pltpu.PrefetchScalarGridSpec(
                num_scalar_prefetch=1, grid=(half // tm,),
                in_specs=[pl.BlockSpec((1, tm, cols), lambda i, j: (j[0], i, 0)),
                          pl.BlockSpec((3, tm, cols), lambda i, j: (0, i, 0))],
                out_specs=pl.BlockSpec((tm, cols), lambda i, j: (i, 0))),
            out_shape=jax.ShapeDtypeStruct((half, cols), F32),
            compiler_params=_cparams(("parallel",)),
        )(_mesh_scalar(2 * x + y), p, r))
    return out


SMALL_ROWS = 32


def _small_finalize(gathered, hlb, fold64, fold128):
    def body(g_ref, hlb_ref, f64_ref, f128_ref, rows_ref, qk_ref, gg_ref, lb_ref):
        tot = g_ref[0]
        for d in range(1, 8):
            tot = tot + g_ref[d]
        rows_ref[...] = tot
        qk_ref[...] = jnp.dot(rows_ref[6:8, :], f64_ref[...], precision=HIGHEST, preferred_element_type=F32)
        gg_ref[...] = jnp.dot(rows_ref[9:10, :], f128_ref[...], precision=HIGHEST, preferred_element_type=F32)
        h0, h1 = hlb_ref[0:1, :], hlb_ref[1:2, :]
        mx = jnp.maximum(h0, h1)
        e0, e1 = jnp.exp(h0 - mx), jnp.exp(h1 - mx)
        lb = e1 / (e0 + e1)
        d1 = rows_ref[5:6, :] * lb * (1.0 - lb)
        lb_ref[...] = jnp.where(_iota((2, 1), 0) == 0, -d1, d1)

    return pl.pallas_call(
        body, name="small_finalize",
        out_shape=[jax.ShapeDtypeStruct((SMALL_ROWS, D), F32), jax.ShapeDtypeStruct((2, FOX_DH), F32),
                   jax.ShapeDtypeStruct((1, HG_D), F32), jax.ShapeDtypeStruct((2, D), F32)],
    )(gathered, hlb, fold64, fold128)


FOX_NAMES = ("fox_w_in", "fox_w_out")
LATE_NAMES = ("hgrn_w_in", "hgrn_w_out", "ffn_w_in", "ffn_w_out")
BIG_NAMES = FOX_NAMES + LATE_NAMES
GATHER_LATE = ("fox_w_out",) + LATE_NAMES
COL_SHARDED = ("fox_w_in", "hgrn_w_in", "ffn_w_in")


def _shard2d(name, a):
    return a.reshape(-1, a.shape[-1])


def _to_shards(name, g):
    layers = g.shape[0]
    if name in COL_SHARDED:
        k, n = g.shape[1], g.shape[2] // 4
        return g.reshape(layers, k, 4, n).transpose(2, 0, 1, 3).reshape(4, layers * k, n)
    r = g.shape[1] // 4
    return g.reshape(layers, 4, r, g.shape[2]).transpose(1, 0, 2, 3).reshape(4, layers * r, g.shape[2])


def kernel(x, meta_tokens, attn_norm, ffn_norm, final_norm, fox_w_in, fox_b_f, fox_q_norm, fox_k_norm, fox_w_out, hgrn_w_in, hgrn_lower_bounds, hgrn_g_norm, hgrn_w_out, ffn_w_in, ffn_w_out, loss_target, m_meta_tokens, m_attn_norm, m_ffn_norm, m_final_norm, m_fox_w_in, m_fox_b_f, m_fox_q_norm, m_fox_k_norm, m_fox_w_out, m_hgrn_w_in, m_hgrn_lower_bounds, m_hgrn_g_norm, m_hgrn_w_out, m_ffn_w_in, m_ffn_w_out, v_meta_tokens, v_attn_norm, v_ffn_norm, v_final_norm, v_fox_w_in, v_fox_b_f, v_fox_q_norm, v_fox_k_norm, v_fox_w_out, v_hgrn_w_in, v_hgrn_lower_bounds, v_hgrn_g_norm, v_hgrn_w_out, v_ffn_w_in, v_ffn_w_out):
    params = dict(meta_tokens=meta_tokens, attn_norm=attn_norm, ffn_norm=ffn_norm, final_norm=final_norm,
                  fox_w_in=fox_w_in, fox_b_f=fox_b_f, fox_q_norm=fox_q_norm, fox_k_norm=fox_k_norm,
                  fox_w_out=fox_w_out, hgrn_w_in=hgrn_w_in, hgrn_lower_bounds=hgrn_lower_bounds,
                  hgrn_g_norm=hgrn_g_norm, hgrn_w_out=hgrn_w_out, ffn_w_in=ffn_w_in, ffn_w_out=ffn_w_out)
    mom_m = dict(meta_tokens=m_meta_tokens, attn_norm=m_attn_norm, ffn_norm=m_ffn_norm, final_norm=m_final_norm,
                 fox_w_in=m_fox_w_in, fox_b_f=m_fox_b_f, fox_q_norm=m_fox_q_norm, fox_k_norm=m_fox_k_norm,
                 fox_w_out=m_fox_w_out, hgrn_w_in=m_hgrn_w_in, hgrn_lower_bounds=m_hgrn_lower_bounds,
                 hgrn_g_norm=m_hgrn_g_norm, hgrn_w_out=m_hgrn_w_out, ffn_w_in=m_ffn_w_in, ffn_w_out=m_ffn_w_out)
    mom_v = dict(meta_tokens=v_meta_tokens, attn_norm=v_attn_norm, ffn_norm=v_ffn_norm, final_norm=v_final_norm,
                 fox_w_in=v_fox_w_in, fox_b_f=v_fox_b_f, fox_q_norm=v_fox_q_norm, fox_k_norm=v_fox_k_norm,
                 fox_w_out=v_fox_w_out, hgrn_w_in=v_hgrn_w_in, hgrn_lower_bounds=v_hgrn_lower_bounds,
                 hgrn_g_norm=v_hgrn_g_norm, hgrn_w_out=v_hgrn_w_out, ffn_w_in=v_ffn_w_in, ffn_w_out=v_ffn_w_out)
    names = list(params)
    xi, yi, _ = _me()

    shards = {n: _shard2d(n, params[n]).astype(BF16) for n in BIG_NAMES}
    w_in_g, meta_g = _chip_allgather([shards["fox_w_in"], meta_tokens])
    w = dict(fox_w_in=jnp.concatenate(list(w_in_g), axis=1))
    meta_full = jnp.concatenate(list(meta_g), axis=1)
    w.update(attn_norm=attn_norm, ffn_norm=ffn_norm, final_norm=final_norm.reshape(1, D), fox_b_f=fox_b_f,
             fox_q_norm=fox_q_norm, fox_k_norm=fox_k_norm, hgrn_lower_bounds=hgrn_lower_bounds,
             hgrn_g_norm=hgrn_g_norm)

    h0 = jnp.concatenate([jnp.zeros((ROW0, D), F32), meta_full, x[0]], axis=0)
    loss, dh0, big, small = _local_step(h0, loss_target[0], w, {n: shards[n] for n in GATHER_LATE})
    loss = lax.psum(loss, ("x", "y", "c"))
    grad_x = dh0[PAD:][None]
    grads = {}

    rows = jnp.concatenate([small["attn_norm"], small["ffn_norm"], small["final_norm"], small["lb_raw"],
                            small["q_gain"], small["k_gain"],
                            jnp.pad(small["b_f"], ((0, 0), (0, D - LANES))), small["g_gain"],
                            dh0[ROW0:PAD], jnp.zeros((SMALL_ROWS - 10 - N_META, D), F32)], axis=0)
    allrows = _device_allgather(rows)
    fold64 = jnp.asarray(np.tile(np.eye(FOX_DH, dtype=np.float32), (FOX_H, 1)))
    fold128 = jnp.asarray(np.tile(np.eye(HG_D, dtype=np.float32), (HG_H, 1)))
    tot, qk, gg, dlb = _small_finalize(allrows, hgrn_lower_bounds, fold64, fold128)
    grads.update(attn_norm=tot[0:2], ffn_norm=tot[2:4], final_norm=tot[4], hgrn_lower_bounds=dlb,
                 fox_q_norm=qk[0:1], fox_k_norm=qk[1:2], fox_b_f=tot[8:9, :FOX_H], hgrn_g_norm=gg,
                 meta_tokens=lax.dynamic_slice_in_dim(tot[10:10 + N_META], (2 * xi + yi) * (D // 4), D // 4, axis=1))

    delta, new_m, new_v = {}, {}, {}
    for n in BIG_NAMES:
        res = _adamw_halves(_shard2d(n, params[n]), *big[n], _shard2d(n, mom_m[n]), _shard2d(n, mom_v[n]),
                            f"adamw_{n}")
        grads[n], delta[n], new_m[n], new_v[n] = (t.reshape(params[n].shape) for t in res)
    delta["meta_tokens"], new_m["meta_tokens"], new_v["meta_tokens"] = _adamw(
        meta_tokens, grads["meta_tokens"], m_meta_tokens, v_meta_tokens, "adamw_meta_tokens")
    small_names = [n for n in names if n not in BIG_NAMES and n != "meta_tokens"]

    def pack(d):
        return jnp.concatenate([jnp.pad(d[n].reshape(-1, d[n].shape[-1]), ((0, 0), (0, D - d[n].shape[-1])))
                                for n in small_names], axis=0)

    packed = [pack(t) for t in (params, grads, mom_m, mom_v)]
    n_rows = packed[0].shape[0]
    packed = [jnp.pad(t, ((0, 16 - n_rows), (0, 0))) for t in packed]
    res = _adamw(*packed, "adamw_small")
    r0 = 0
    for n in small_names:
        nr = params[n].reshape(-1, params[n].shape[-1]).shape[0]
        for dst, src in zip((delta, new_m, new_v), res):
            dst[n] = src[r0:r0 + nr, :params[n].shape[-1]].reshape(params[n].shape)
        r0 += nr

    return (loss, grad_x, *[grads[n] for n in names], *[delta[n] for n in names],
            *[new_m[n] for n in names], *[new_v[n] for n in names])
```

```python
import functools

import numpy as np
import jax
import jax.numpy as jnp
from jax import lax
from jax.experimental import pallas as pl
from jax.experimental.pallas import tpu as pltpu

F32, BF16 = jnp.float32, jnp.bfloat16
HIGHEST = lax.Precision.HIGHEST

D = 1024
N_META = 16
PAD = 128
ROW0 = PAD - N_META
FOX_H, FOX_DH = 16, 64
HG_H, HG_D = 8, 128
HG_C = 128
HG_LEV = 7
HG_HPS = 8
HG_SKEW = 0
FFN = 2816
EPS = 1e-6
BIG = 1e30
LOG2E = 1.4426950408889634
LANES = 128
MXU_N = 256
VMEM_LIMIT = 48 * 1024 * 1024
ROW_TILES = (640, 512, 384, 320, 256, 128, 64, 32, 16, 8)
ATTN_TILES = (640, 512, 256, 128)
FOX_HPS_FWD = 8

ADAM_LR, ADAM_B1, ADAM_B2, ADAM_EPS, ADAM_WD, ADAM_STEP = 0.001, 0.9, 0.999, 1e-08, 0.01, 10

MESH = pl.DeviceIdType.MESH
ANY = pl.BlockSpec(memory_space=pl.ANY)
NT = (((1,), (1,)), ((), ()))
TN = (((0,), (0,)), ((), ()))


def _tile(n, cands=ROW_TILES, cap=None):
    for c in cands:
        if n % c == 0 and (cap is None or c <= cap):
            return c
    return n


def _cparams(sem):
    return pltpu.CompilerParams(dimension_semantics=sem, vmem_limit_bytes=VMEM_LIMIT)


def _sigmoid(x):
    return jax.nn.sigmoid(x)


def _log_sigmoid(x):
    return jnp.minimum(x, 0.0) - jnp.log(1.0 + jnp.exp(-jnp.abs(x)))


def _iota(shape, dim):
    return lax.broadcasted_iota(jnp.int32, shape, dim)


def _matmul(a, b, *, ta=False, tb=False, out_dtype=F32, add=None, name):
    if ta:
        kdim, m = a.shape
    else:
        m, kdim = a.shape
    n = b.shape[0] if tb else b.shape[1]
    if ta:
        tm = m if m <= 1024 else _tile(m, (1408, 1024, 512, 256, 128))
        tk = _tile(kdim, (1664,) + ROW_TILES)
    else:
        tm = _tile(m)
        tk = kdim if kdim <= 4096 else _tile(kdim, (2048, 1024, 512))
    tn = n if n <= 1024 else _tile(n, (1408, 1024, 512, 256, 128))
    nk = kdim // tk
    dn = (((0 if ta else 1,), (1 if tb else 0,)), ((), ()))

    def body(*refs):
        if add is None:
            a_ref, b_ref, o_ref, acc_ref = refs
        else:
            a_ref, b_ref, add_ref, o_ref, acc_ref = refs
        k = pl.program_id(2)

        @pl.when(k == 0)
        def _():
            acc_ref[...] = jnp.zeros_like(acc_ref)

        acc_ref[...] += lax.dot_general(a_ref[...].astype(BF16), b_ref[...].astype(BF16), dn,
                                        preferred_element_type=F32)

        @pl.when(k == nk - 1)
        def _():
            r = acc_ref[...]
            if add is not None:
                r = r + add_ref[...].astype(F32)
            o_ref[...] = r.astype(o_ref.dtype)

    a_spec = pl.BlockSpec((tk, tm), lambda j, i, k: (k, i)) if ta else pl.BlockSpec((tm, tk), lambda j, i, k: (i, k))
    b_spec = pl.BlockSpec((tn, tk), lambda j, i, k: (j, k)) if tb else pl.BlockSpec((tk, tn), lambda j, i, k: (k, j))
    o_spec = pl.BlockSpec((tm, tn), lambda j, i, k: (i, j))
    ins, specs = [a, b], [a_spec, b_spec]
    if add is not None:
        ins.append(add)
        specs.append(o_spec)
    return pl.pallas_call(
        body, name=name, grid=(n // tn, m // tm, nk), in_specs=specs, out_specs=o_spec,
        out_shape=jax.ShapeDtypeStruct((m, n), out_dtype),
        scratch_shapes=[pltpu.VMEM((tm, tn), F32)],
        compiler_params=_cparams(("parallel", "parallel", "arbitrary")),
    )(*ins)


def _rowwise(fn, ins, bcast, outs, accs, *, name, reverse=False, carry=None, as_refs=False):
    rows = ins[0].shape[0]
    per_row = sum(x.shape[1] * x.dtype.itemsize for x in ins) + sum(c * jnp.dtype(d).itemsize for c, d in outs)
    tm = _tile(rows, cap=max(8, (10 * 1024 * 1024) // per_row))
    n = rows // tm
    n_in, n_b, n_o, n_a = len(ins), len(bcast), len(outs), len(accs)

    def body(*refs):
        in_refs = refs[:n_in]
        b_refs = refs[n_in:n_in + n_b]
        o_refs = refs[n_in + n_b:n_in + n_b + n_o]
        a_refs = refs[n_in + n_b + n_o:n_in + n_b + n_o + n_a]
        c_refs = refs[n_in + n_b + n_o + n_a:]
        i = pl.program_id(0)
        blk = (n - 1 - i) if reverse else i
        if c_refs:
            @pl.when(i == 0)
            def _():
                c_refs[0][...] = jnp.zeros_like(c_refs[0])
        args = (list(in_refs) if as_refs else [r[...] for r in in_refs], [r[...] for r in b_refs])
        o_vals, a_vals = fn(blk * tm, *args, *c_refs)
        for r, v in zip(o_refs, o_vals):
            r[...] = v.astype(r.dtype)
        if n_a:
            @pl.when(i == 0)
            def _():
                for r in a_refs:
                    r[...] = jnp.zeros_like(r)
            for r, v in zip(a_refs, a_vals):
                r[...] += v

    def row_map(i):
        return ((n - 1 - i) if reverse else i, 0)

    in_specs = [pl.BlockSpec((tm, x.shape[1]), row_map) for x in ins]
    in_specs += [pl.BlockSpec(x.shape, lambda i, nd=x.ndim: (0,) * nd) for x in bcast]
    out_specs = [pl.BlockSpec((tm, c), row_map) for c, _ in outs]
    out_specs += [pl.BlockSpec(s, lambda i: (0, 0)) for s in accs]
    out_shape = [jax.ShapeDtypeStruct((rows, c), d) for c, d in outs]
    out_shape += [jax.ShapeDtypeStruct(s, F32) for s in accs]
    res = pl.pallas_call(
        body, name=name, grid=(n,), in_specs=in_specs, out_specs=out_specs, out_shape=out_shape,
        scratch_shapes=[pltpu.VMEM(carry, F32)] if carry else [],
        compiler_params=_cparams(("arbitrary",)),
    )(*ins, *bcast)
    return res[:n_o], res[n_o:]


def _row_ids(row0, tm):
    return row0 + _iota((tm, 1), 0)


def _rms_fwd(x, gain, name):
    def fn(row0, ins, bc):
        (xv,), (g,) = ins, bc
        r = lax.rsqrt(jnp.mean(xv * xv, axis=-1, keepdims=True) + EPS)
        return [xv * r * g], []
    return _rowwise(fn, [x], [gain], [(D, BF16)], [], name=name)[0][0]


def _rms_bwd(x, dxn, gain, dh_up, name):
    def fn(row0, ins, bc):
        xv, dy, up = ins
        (g,) = bc
        dy = dy.astype(F32)
        r = lax.rsqrt(jnp.mean(xv * xv, axis=-1, keepdims=True) + EPS)
        xh = xv * r
        dxh = dy * g
        dx = r * (dxh - xh * jnp.mean(dxh * xh, axis=-1, keepdims=True))
        keep = _row_ids(row0, xv.shape[0]) >= ROW0
        return [jnp.where(keep, up + dx, 0.0)], [jnp.sum(dy * xh, axis=0, keepdims=True)]
    (dh,), (dgain,) = _rowwise(fn, [x, dxn, dh_up], [gain], [(D, F32)], [(1, D)], name=name)
    return dh, dgain


def _loss_bwd(h, tgt, gain):
    tm = PAD
    n = h.shape[0] // tm

    def body(x_ref, t_ref, g_ref, dx_ref, loss_ref, dg_ref):
        i = pl.program_id(0)

        @pl.when(i == 0)
        def _():
            loss_ref[...] = jnp.zeros_like(loss_ref)
            dg_ref[...] = jnp.zeros_like(dg_ref)

        xv, g = x_ref[...], g_ref[...]
        r = lax.rsqrt(jnp.mean(xv * xv, axis=-1, keepdims=True) + EPS)
        xh = xv * r
        err = jnp.where(i >= 1, xh * g - t_ref[...], 0.0)
        per_row = jnp.mean(err * err, axis=-1, keepdims=True)
        loss_ref[...] += jnp.broadcast_to(0.5 * jnp.sum(per_row, axis=0, keepdims=True), (1, LANES))
        dy = err * (1.0 / D)
        dxh = dy * g
        dx_ref[...] = r * (dxh - xh * jnp.mean(dxh * xh, axis=-1, keepdims=True))
        dg_ref[...] += jnp.sum(dy * xh, axis=0, keepdims=True)

    dh, loss, dgain = pl.pallas_call(
        body, name="loss_bwd", grid=(n,),
        in_specs=[pl.BlockSpec((tm, D), lambda i: (i, 0)), pl.BlockSpec((tm, D), lambda i: (jnp.maximum(i - 1, 0), 0)),
                  pl.BlockSpec((1, D), lambda i: (0, 0))],
        out_specs=[pl.BlockSpec((tm, D), lambda i: (i, 0)), pl.BlockSpec((1, LANES), lambda i: (0, 0)),
                   pl.BlockSpec((1, D), lambda i: (0, 0))],
        out_shape=[jax.ShapeDtypeStruct(h.shape, F32), jax.ShapeDtypeStruct((1, LANES), F32),
                   jax.ShapeDtypeStruct((1, D), F32)],
        compiler_params=_cparams(("arbitrary",)),
    )(h, tgt, gain)
    return loss[0, 0], dh, dgain


FFN_TILES = dict(rows=(320, 256, 128), cols=(1408, 1024, 512, 256, 128))


def _ffn_in(hn, wg, wu, name):
    m, kdim = hn.shape
    n = wg.shape[1]
    tm, tn = _tile(m, FFN_TILES["rows"]), _tile(n, FFN_TILES["cols"])

    def body(a_ref, wg_ref, wu_ref, g_ref, u_ref, act_ref):
        a = a_ref[...]
        g = jnp.dot(a, wg_ref[...], preferred_element_type=F32)
        u = jnp.dot(a, wu_ref[...], preferred_element_type=F32)
        g_ref[...] = g.astype(g_ref.dtype)
        u_ref[...] = u.astype(u_ref.dtype)
        act_ref[...] = (g * _sigmoid(g) * u).astype(act_ref.dtype)

    wspec = pl.BlockSpec((kdim, tn), lambda j, i: (0, j))
    ospec = pl.BlockSpec((tm, tn), lambda j, i: (i, j))
    return pl.pallas_call(
        body, name=name, grid=(n // tn, m // tm),
        in_specs=[pl.BlockSpec((tm, kdim), lambda j, i: (i, 0)), wspec, wspec], out_specs=[ospec] * 3,
        out_shape=[jax.ShapeDtypeStruct((m, n), BF16)] * 3,
        compiler_params=_cparams(("parallel", "parallel")),
    )(hn, wg, wu)


def _ffn_dact(dh, wo, g, u, name):
    m, kdim = dh.shape
    n = wo.shape[0]
    tm, tn = _tile(m, FFN_TILES["rows"]), _tile(n, FFN_TILES["cols"])

    def body(a_ref, w_ref, g_ref, u_ref, dg_ref, du_ref):
        da = lax.dot_general(a_ref[...].astype(BF16), w_ref[...], NT, preferred_element_type=F32)
        gv, uv = g_ref[...].astype(F32), u_ref[...].astype(F32)
        s = _sigmoid(gv)
        dg_ref[...] = (da * uv * (s * (1.0 + gv * (1.0 - s)))).astype(dg_ref.dtype)
        du_ref[...] = (da * gv * s).astype(du_ref.dtype)

    ospec = pl.BlockSpec((tm, tn), lambda j, i: (i, j))
    return pl.pallas_call(
        body, name=name, grid=(n // tn, m // tm),
        in_specs=[pl.BlockSpec((tm, kdim), lambda j, i: (i, 0)), pl.BlockSpec((tn, kdim), lambda j, i: (j, 0)),
                  ospec, ospec],
        out_specs=[ospec] * 2, out_shape=[jax.ShapeDtypeStruct((m, n), BF16)] * 2,
        compiler_params=_cparams(("parallel", "parallel")),
    )(dh, wo, g, u)


def _adamw_math(wv, gv, mv, vv):
    mn = ADAM_B1 * mv + (1.0 - ADAM_B1) * gv
    vn = ADAM_B2 * vv + (1.0 - ADAM_B2) * (gv * gv)
    m_hat = mn / (1.0 - ADAM_B1 ** ADAM_STEP)
    v_hat = vn / (1.0 - ADAM_B2 ** ADAM_STEP)
    return -ADAM_LR * (m_hat / (jnp.sqrt(v_hat) + ADAM_EPS) + ADAM_WD * wv), mn, vn


def _adamw(w, g, m, v, name):
    def fn(row0, ins, bc):
        return list(_adamw_math(*ins)), []
    c = w.shape[1]
    return _rowwise(fn, [w, g, m, v], [], [(c, F32)] * 3, [], name=name)[0]


def _adamw_halves(w, mine, theirs, m, v, name):
    rows, cols = w.shape
    half = rows // 2
    tm = _tile(half, cap=(10 * 1024 * 1024) // (9 * 4 * cols))
    nb = half // tm

    def body(c_ref, w_ref, g1_ref, g2_ref, m_ref, v_ref, g_out, d_out, m_out, v_out):
        own = (pl.program_id(0) // nb) == c_ref[0]
        g = jnp.where(own, g1_ref[...], g2_ref[...])
        delta, mn, vn = _adamw_math(w_ref[...], g, m_ref[...], v_ref[...])
        g_out[...] = g
        d_out[...] = delta
        m_out[...] = mn
        v_out[...] = vn

    full = pl.BlockSpec((tm, cols), lambda i, c: (i, 0))
    part = pl.BlockSpec((tm, cols), lambda i, c: (lax.rem(i, nb), 0))
    return pl.pallas_call(
        body, name=name,
        grid_spec=pltpu.PrefetchScalarGridSpec(num_scalar_prefetch=1, grid=(2 * nb,),
                                               in_specs=[full, part, part, full, full], out_specs=[full] * 4),
        out_shape=[jax.ShapeDtypeStruct((rows, cols), F32)] * 4,
        compiler_params=_cparams(("parallel",)),
    )(_mesh_scalar(lax.axis_index("c")), w, mine, theirs, m, v)


def _head_sum(x, gmat):
    hi = x.astype(BF16)
    lo = (x - hi.astype(F32)).astype(BF16)
    w = gmat.shape[0]
    return jnp.concatenate(
        [jnp.dot(hi[:, b:b + w], gmat, preferred_element_type=F32) + jnp.dot(lo[:, b:b + w], gmat,
                                                                             preferred_element_type=F32)
         for b in range(0, x.shape[1], w)], axis=1)


def _split3(x):
    hi = x.astype(BF16).astype(F32)
    r = x - hi
    mid = r.astype(BF16).astype(F32)
    return hi, mid, r - mid


def _extra_base(hh):
    return FOX_DH * (1 - hh)


def _data_mask(hh):
    lane = _iota((1, LANES), 1)
    return (lane >= FOX_DH * hh) & (lane < FOX_DH * (hh + 1))


def _with_extras(data, hh, vals):
    lane = _iota((1, LANES), 1)
    x = jnp.zeros_like(data)
    for e, v in enumerate(vals):
        x = jnp.where(lane == _extra_base(hh) + e, v, x)
    return jnp.where(_data_mask(hh), data, x)


def _fox_pack_fwd(q_raw, k_raw, v, cq, ck, qg, kg, gmat):
    scale2 = FOX_DH ** -0.5 * LOG2E

    def fn(row0, refs, bc):
        q_ref, k_ref, v_ref, cq_ref, ck_ref = refs
        g_q, g_k, gm = bc
        qv, kv = q_ref[...], k_ref[...]
        qn = qv * lax.rsqrt(_head_sum(qv * qv, gm) * (1.0 / FOX_DH) + EPS) * (g_q * scale2)
        kn = kv * lax.rsqrt(_head_sum(kv * kv, gm) * (1.0 / FOX_DH) + EPS) * g_k
        qs, ks, vs = [], [], []
        for h in range(FOX_H):
            p, hh = divmod(h, 2)
            sl = slice(p * LANES, (p + 1) * LANES)
            cq3 = _split3(cq_ref[:, h:h + 1] * LOG2E)
            ck3 = _split3(ck_ref[:, h:h + 1] * (-LOG2E))
            qs.append(_with_extras(qn[:, sl], hh, [*cq3, 1.0, 1.0, 1.0]))
            ks.append(_with_extras(kn[:, sl], hh, [1.0, 1.0, 1.0, *ck3]))
            vs.append(_with_extras(v_ref[:, sl].astype(F32), hh, [1.0, 1.0]))
        return [jnp.concatenate(qs, axis=1), jnp.concatenate(ks, axis=1), jnp.concatenate(vs, axis=1)], []

    w = FOX_H * LANES
    return _rowwise(fn, [q_raw, k_raw, v, cq, ck], [qg, kg, gmat], [(w, BF16)] * 3, [], name="fox_pack_fwd",
                    as_refs=True)[0]


def _fox_pack_bias(qp, cq, lse2):
    def fn(row0, refs, bc):
        q_ref, cq_ref, lse_ref = refs
        lane = _iota((1, LANES), 1)
        outs = []
        for h in range(FOX_H):
            blk = q_ref[:, h * LANES:(h + 1) * LANES].astype(F32)
            for e, part in enumerate(_split3(cq_ref[:, h:h + 1] * LOG2E - lse_ref[:, h:h + 1])):
                blk = jnp.where(lane == _extra_base(h % 2) + e, part, blk)
            outs.append(blk)
        return [jnp.concatenate(outs, axis=1)], []
    return _rowwise(fn, [qp, cq, lse2], [], [(FOX_H * LANES, BF16)], [], name="fox_pack_bias", as_refs=True)[0][0]


def _fox_pack_bwd(dog, o, gate):
    def fn(row0, refs, bc):
        d_ref, o_ref, g_ref = refs
        dos, dgs = [], []
        for p in range(FOX_H // 2):
            sl = slice(p * LANES, (p + 1) * LANES)
            dv, ov, gv = (r[:, sl].astype(F32) for r in (d_ref, o_ref, g_ref))
            s = _sigmoid(gv)
            do = dv * s
            dgs.append(dv * ov * s * (1.0 - s))
            od = ov * do
            for hh in range(2):
                delta = jnp.sum(jnp.where(_data_mask(hh), od, 0.0), axis=-1, keepdims=True)
                hi = delta.astype(BF16).astype(F32)
                dos.append(_with_extras(do, hh, [-hi, hi - delta]))
        return [jnp.concatenate(dos, axis=1), jnp.concatenate(dgs, axis=1)], []
    return _rowwise(fn, [dog, o, gate], [], [(FOX_H * LANES, BF16), (D, BF16)], [], name="fox_pack_bwd",
                    as_refs=True)[0]


def _fox_unpack_bwd(q_raw, k_raw, dqp, dk, qg, kg, gmat):
    scale = FOX_DH ** -0.5

    def fn(row0, refs, bc):
        q_ref, k_ref, dq_ref, dk_ref = refs
        g_q, g_k, gm = bc
        lane = _iota((1, LANES), 1)
        dqs = []
        dcq = jnp.zeros((q_ref.shape[0], LANES), F32)
        for p in range(FOX_H // 2):
            even = dq_ref[:, (2 * p) * LANES:(2 * p + 1) * LANES]
            odd = dq_ref[:, (2 * p + 1) * LANES:(2 * p + 2) * LANES]
            dqs.append(jnp.where(_data_mask(0), even, odd) * scale)
            for hh in range(2):
                col = (2 * p + hh) * LANES + _extra_base(hh)
                dcq = jnp.where(lane == 2 * p + hh, dq_ref[:, col:col + 1], dcq)
        outs, accs = [], []
        for xv, dy, g in ((q_ref[...], jnp.concatenate(dqs, axis=1), g_q), (k_ref[...], dk_ref[...] * (1.0 / LOG2E), g_k)):
            r = lax.rsqrt(_head_sum(xv * xv, gm) * (1.0 / FOX_DH) + EPS)
            xh = xv * r
            dxh = dy * g
            outs.append(r * (dxh - xh * (_head_sum(dxh * xh, gm) * (1.0 / FOX_DH))))
            accs.append(jnp.sum(dy * xh, axis=0, keepdims=True))
        return outs + [dcq], accs
    return _rowwise(fn, [q_raw, k_raw, dqp, dk], [qg, kg, gmat], [(D, BF16), (D, BF16), (LANES, F32)],
                    [(1, D), (1, D)], name="fox_unpack_bwd", as_refs=True)


def _fox_cumsum_fwd(flog, bf):
    def fn(row0, ins, bc, carry):
        (f,), (b,) = ins, bc
        tm = f.shape[0]
        keep = _row_ids(row0, tm) >= ROW0
        lf = jnp.where(keep, _log_sigmoid(f + b), 0.0)
        tri = (_iota((tm, tm), 0) >= _iota((tm, tm), 1)).astype(F32)
        c = jnp.dot(tri, lf, precision=HIGHEST, preferred_element_type=F32) + carry[...]
        carry[...] = carry[...] + jnp.sum(lf, axis=0, keepdims=True)
        return [c, jnp.where(keep, c, BIG)], []
    return _rowwise(fn, [flog], [bf], [(LANES, F32), (LANES, F32)], [], name="fox_cumsum_fwd",
                    carry=(1, LANES))[0]


def _fox_cumsum_bwd(dc_q, dc_k, flog, bf):
    def fn(row0, ins, bc, carry):
        (dq, dk, f), (b,) = ins, bc
        d = dq + dk
        tm = f.shape[0]
        keep = _row_ids(row0, tm) >= ROW0
        triu = (_iota((tm, tm), 0) <= _iota((tm, tm), 1)).astype(F32)
        dlf = jnp.dot(triu, d, precision=HIGHEST, preferred_element_type=F32) + carry[...]
        carry[...] = carry[...] + jnp.sum(d, axis=0, keepdims=True)
        dfl = jnp.where(keep, dlf * _sigmoid(-(f + b)), 0.0)
        return [dfl], [jnp.sum(dfl, axis=0, keepdims=True)]
    (dflog,), (dbf,) = _rowwise(fn, [dc_q, dc_k, flog], [bf], [(LANES, F32)], [(1, LANES)], name="fox_cumsum_bwd",
                                reverse=True, carry=(1, LANES))
    return dflog, dbf


def _causal_steps(n, key_major):
    if key_major:
        pairs = [(i, j) for j in range(n) for i in range(j, n)]
    else:
        pairs = [(i, j) for i in range(n) for j in range(i + 1)]
    return (jnp.asarray(np.array([p[0] for p in pairs], np.int32)),
            jnp.asarray(np.array([p[1] for p in pairs], np.int32)))


def _fox_attn_fwd(qp, kp, vp, gate, shards):
    L = qp.shape[0]
    t = _tile(L, ATTN_TILES)
    n = L // t
    hps = FOX_HPS_FWD
    P = FOX_H // hps
    it, jt = _causal_steps(n, False)
    n_steps = it.shape[0]
    ns = len(shards)

    def body(it_ref, jt_ref, q_ref, k_ref, v_ref, g_ref, *rest):
        sh_in, (o_ref, og_ref, lse_ref), sh_out = rest[:ns], rest[ns:ns + 3], rest[ns + 3:2 * ns + 3]
        m_sc, acc, ssem, rsem, lsem = rest[2 * ns + 3:]
        step = pl.program_id(1)
        i, j = it_ref[step], jt_ref[step]
        first = (pl.program_id(0) == 0) & (step == 0)
        last = (pl.program_id(0) == P - 1) & (step == n_steps - 1)

        @pl.when(first)
        def _():
            _gather_start(sh_in, sh_out, ssem, rsem, lsem)

        @pl.when(j == 0)
        def _():
            m_sc[...] = jnp.full_like(m_sc, -3.0e38)
            acc[...] = jnp.zeros_like(acc)

        def update(masked):
            def head(hh):
                sl = slice(hh * LANES, (hh + 1) * LANES)
                s2 = lax.dot_general(k_ref[:, sl], q_ref[:, sl], NT, preferred_element_type=F32)
                if masked:
                    s2 = jnp.where(_iota((t, t), 0) <= _iota((t, t), 1), s2, -jnp.inf)
                yield
                m_old = m_sc[hh]
                m_new = jnp.maximum(m_old, jnp.max(s2, axis=0, keepdims=True))
                p = jnp.exp2(s2 - m_new).astype(BF16)
                yield
                acc[hh] = jnp.exp2(m_old - m_new) * acc[hh] + lax.dot_general(v_ref[:, sl], p, TN,
                                                                              preferred_element_type=F32)
                m_sc[hh] = m_new

            _interleave((head(hh) for hh in range(hps)), skew=1)

        @pl.when(j < i)
        def _():
            update(False)

        @pl.when(j == i)
        def _():
            update(True)
            outs = []
            for hh in range(hps):
                base = _extra_base(hh % 2)
                l = acc[hh, base:base + 1, :]
                outs.append((acc[hh] / l).T)
                lse_ref[0, hh:hh + 1, :] = m_sc[hh] + jnp.log2(l)
            o = jnp.concatenate([jnp.where(_data_mask(0), outs[a], outs[a + 1]) for a in range(0, hps, 2)], axis=1)
            o_ref[...] = o.astype(o_ref.dtype)
            og_ref[...] = (o * _sigmoid(g_ref[...].astype(F32))).astype(og_ref.dtype)

        @pl.when(last)
        def _():
            _gather_wait(sh_in, sh_out, ssem, rsem, lsem)

    qspec = pl.BlockSpec((t, hps * LANES), lambda p, s, it, jt: (it[s], p))
    kspec = pl.BlockSpec((t, hps * LANES), lambda p, s, it, jt: (jt[s], p))
    ospec = pl.BlockSpec((t, hps * FOX_DH), lambda p, s, it, jt: (it[s], p))
    lspec = pl.BlockSpec((1, hps, t), lambda p, s, it, jt: (p, 0, it[s]))
    res = pl.pallas_call(
        body, name="fox_attn_fwd",
        grid_spec=pltpu.PrefetchScalarGridSpec(
            num_scalar_prefetch=2, grid=(P, n_steps),
            in_specs=[qspec, kspec, kspec, ospec] + [ANY] * ns, out_specs=[ospec, ospec, lspec] + [ANY] * ns,
            scratch_shapes=[pltpu.VMEM((hps, 1, t), F32), pltpu.VMEM((hps, LANES, t), F32)] + _gather_sems(ns)),
        out_shape=[jax.ShapeDtypeStruct((L, D), BF16), jax.ShapeDtypeStruct((L, D), BF16),
                   jax.ShapeDtypeStruct((P, hps, L), F32)]
        + [jax.ShapeDtypeStruct((4,) + a.shape, a.dtype) for a in shards],
        compiler_params=_cparams(("arbitrary", "arbitrary")),
    )(it, jt, qp, kp, vp, gate, *shards)
    return res[0], res[1], res[2], res[3:]


def _fox_attn_bwd(qb, kp, vp, dop, slabs):
    L = qb.shape[0]
    t = _tile(L, ATTN_TILES)
    n = L // t
    P = FOX_H // 2
    it, jt = _causal_steps(n, True)
    n_steps = it.shape[0]
    ns = len(slabs)

    def body(it_ref, jt_ref, q_ref, k_ref, v_ref, do_ref, *rest):
        sl_in, (dq_ref, dk_ref, dv_ref, dck_ref), sl_out = rest[:ns], rest[ns:ns + 4], rest[ns + 4:2 * ns + 4]
        dk_acc, dv_acc, ssem, rsem = rest[2 * ns + 4:]
        step = pl.program_id(1)
        i, j = it_ref[step], jt_ref[step]

        @pl.when((pl.program_id(0) == 0) & (step == 0))
        def _():
            for cp in _scatter_copies(sl_in, sl_out, ssem, rsem):
                cp.start()

        @pl.when(step == 0)
        def _():
            dq_ref[...] = jnp.zeros_like(dq_ref)

        @pl.when(i == j)
        def _():
            dk_acc[...] = jnp.zeros_like(dk_acc)
            dv_acc[...] = jnp.zeros_like(dv_acc)

        def update(masked):
            rows = pl.ds(pl.multiple_of(i * t, LANES), t)
            for hh in range(2):
                sl = slice(hh * LANES, (hh + 1) * LANES)
                q, k, dov = q_ref[:, sl], k_ref[:, sl], do_ref[:, sl]
                s2 = lax.dot_general(k, q, NT, preferred_element_type=F32)
                if masked:
                    s2 = jnp.where(_iota((t, t), 0) <= _iota((t, t), 1), s2, -jnp.inf)
                p = jnp.exp2(s2)
                ds = (p * lax.dot_general(v_ref[:, sl], dov, NT, preferred_element_type=F32)).astype(BF16)
                dv_acc[hh] += jnp.dot(p.astype(BF16), dov, preferred_element_type=F32)
                dk_acc[hh] += jnp.dot(ds, q, preferred_element_type=F32)
                dq_ref[rows, sl] += lax.dot_general(ds, k, TN, preferred_element_type=F32)

        @pl.when(i > j)
        def _():
            update(False)

        @pl.when(i == j)
        def _():
            update(True)

        @pl.when(i == n - 1)
        def _():
            dk_ref[...] = jnp.where(_data_mask(0), dk_acc[0], dk_acc[1])
            dv_ref[...] = jnp.where(_data_mask(0), dv_acc[0], dv_acc[1]).astype(dv_ref.dtype)
            col_sums = [dk_acc[hh, :, _extra_base(hh) + 3:_extra_base(hh) + 4] for hh in range(2)]
            dck_ref[0] = -jnp.where(_iota((1, 2), 1) == 0, col_sums[0], col_sums[1])

        @pl.when((pl.program_id(0) == P - 1) & (step == n_steps - 1))
        def _():
            for cp in _scatter_copies(sl_in, sl_out, ssem, rsem):
                cp.wait()

    qspec = pl.BlockSpec((t, 2 * LANES), lambda p, s, it, jt: (it[s], p))
    kspec = pl.BlockSpec((t, 2 * LANES), lambda p, s, it, jt: (jt[s], p))
    ospec = pl.BlockSpec((t, LANES), lambda p, s, it, jt: (jt[s], p))
    res = pl.pallas_call(
        body, name="fox_attn_bwd",
        grid_spec=pltpu.PrefetchScalarGridSpec(
            num_scalar_prefetch=2, grid=(P, n_steps),
            in_specs=[qspec, kspec, kspec, qspec] + [ANY] * ns,
            out_specs=[pl.BlockSpec((L, 2 * LANES), lambda p, s, it, jt: (0, p)), ospec, ospec,
                       pl.BlockSpec((1, t, 2), lambda p, s, it, jt: (p, jt[s], 0))] + [ANY] * ns,
            scratch_shapes=[pltpu.VMEM((2, t, LANES), F32), pltpu.VMEM((2, t, LANES), F32),
                            pltpu.SemaphoreType.DMA((ns, 3)), pltpu.SemaphoreType.DMA((ns, 3))]),
        out_shape=[jax.ShapeDtypeStruct((L, FOX_H * LANES), F32), jax.ShapeDtypeStruct((L, D), F32),
                   jax.ShapeDtypeStruct((L, D), BF16), jax.ShapeDtypeStruct((P, L, 2), F32)]
        + [jax.ShapeDtypeStruct((3,) + a.shape[1:], a.dtype) for a in slabs],
        compiler_params=_cparams(("arbitrary", "arbitrary")),
    )(it, jt, qb, kp, vp, dop, *slabs)
    return res[0], res[1], res[2], res[3], res[4:]


def _hgrn_consts():
    C = HG_C
    r = np.arange(C)[:, None]
    j = np.arange(C)[None, :]
    mats = [j <= r, j > r]
    masks = []
    n = C
    while n >= 2:
        half = n // 2
        mid = (r // n) * n + half - 1
        second = (r % n) >= half
        mats.append(np.where(second, (j > mid) & (j <= r), (j > r) & (j <= mid)))
        masks.append(((r // n) == (j // n)) & ((r % n) >= half) & ((j % n) < half))
        n //= 2
    return (jnp.asarray(np.concatenate(mats, 0).astype(np.float32), BF16),
            jnp.asarray(np.stack(masks).astype(np.float32), F32))


def _hg_pre(hq, hz, h0, h1):
    mx = jnp.maximum(h0, h1)
    e0, e1 = jnp.exp(h0 - mx), jnp.exp(h1 - mx)
    lb = e1 / (e0 + e1)
    sq = _sigmoid(hq)
    sz = _sigmoid(hz)
    snz = 1.0 - sz
    k = (1.0 - lb) * snz
    g = jnp.maximum(jnp.log(lb + (1.0 - lb) * sz), -BIG)
    return lb, hq * sq, sq, k, sz, snz, g


def _hg_decays(g, rmat):
    hi = g.astype(BF16)
    lo = (g - hi.astype(F32)).astype(BF16)
    d = jnp.dot(rmat, jnp.concatenate([hi, lo], axis=1), preferred_element_type=F32)
    return jnp.exp(d[:, :HG_D] + d[:, HG_D:])


def _interleave(programs, skew=0):
    progs = list(programs)
    done = [False] * len(progs)
    tick = 0
    while not all(done):
        for n, g in enumerate(progs):
            if not done[n] and tick >= n * skew:
                try:
                    next(g)
                except StopIteration:
                    done[n] = True
        tick += 1


def _hg_intra_levels(q, k, fall, masks):
    C = HG_C
    eye = _iota((C, C), 0) == _iota((C, C), 1)
    a = jnp.where(eye, jnp.sum(q * k, axis=-1, keepdims=True), 0.0)
    for l in range(HG_LEV):
        f = fall[(2 + l) * C:(3 + l) * C]
        a = a + masks[l] * lax.dot_general((q * f).astype(BF16), (k * f).astype(BF16), NT,
                                           preferred_element_type=F32)
        yield a


def _hgrn_specs(n_chunks, reverse):
    C = HG_C
    w = HG_HPS * HG_D

    def col(first_head):
        off = first_head // HG_HPS
        if reverse:
            return pl.BlockSpec((C, w), lambda h, c: (n_chunks - 1 - c, off + h))
        return pl.BlockSpec((C, w), lambda h, c: (c, off + h))

    st = pl.BlockSpec((HG_HPS, 1, HG_D, HG_D),
                      (lambda h, c: (h, n_chunks - 1 - c, 0, 0)) if reverse else (lambda h, c: (h, c, 0, 0)))
    consts = [pl.BlockSpec((2, w), lambda h, c: (0, h)), pl.BlockSpec((1, HG_D), lambda h, c: (0, 0)),
              pl.BlockSpec(((2 + HG_LEV) * C, C), lambda h, c: (0, 0)),
              pl.BlockSpec((HG_LEV, C, C), lambda h, c: (0, 0, 0))]
    return col, st, consts


def _hgrn_fwd(proj, hlb, gg, rmat, masks):
    L = proj.shape[0]
    C = HG_C
    nc = L // C
    col, st, consts = _hgrn_specs(nc, False)

    def body(hq_ref, hz_ref, hi_ref, hg_ref, hlb_ref, gg_ref, r_ref, m_ref, og_ref, st_ref, state):
        c = pl.program_id(1)

        @pl.when(c == 0)
        def _():
            state[...] = jnp.zeros_like(state)

        def head(hh):
            sl = slice(hh * HG_D, (hh + 1) * HG_D)
            v, hg = hi_ref[:, sl], hg_ref[:, sl]
            _, q, _, k, _, _, g = _hg_pre(hq_ref[:, sl], hz_ref[:, sl], hlb_ref[0:1, sl], hlb_ref[1:2, sl])
            yield
            fall = _hg_decays(g, r_ref[...])
            fb, fe = fall[0:C], fall[C:2 * C]
            st0 = state[hh]
            st_ref[hh, 0] = st0
            yield
            for a in _hg_intra_levels(q, k, fall, m_ref[...]):
                yield
            vb = v.astype(BF16)
            o = jnp.dot(a.astype(BF16), vb, preferred_element_type=F32)
            o = o + lax.dot_general((q * fb).astype(BF16), st0.astype(BF16), NT, preferred_element_type=F32)
            yield
            ebc = jnp.exp(jnp.sum(g, axis=0, keepdims=True))
            state[hh] = st0 * ebc + lax.dot_general(vb, (k * fe).astype(BF16), TN, preferred_element_type=F32)
            r = lax.rsqrt(jnp.mean(o * o, axis=-1, keepdims=True) + EPS)
            og_ref[:, sl] = (o * r * gg_ref[...] * (hg * _sigmoid(hg))).astype(og_ref.dtype)

        _interleave((head(hh) for hh in range(HG_HPS)), skew=HG_SKEW)

    return pl.pallas_call(
        body, name="hgrn_fwd", grid=(HG_H // HG_HPS, nc),
        in_specs=[col(0), col(HG_H), col(2 * HG_H), col(3 * HG_H)] + consts,
        out_specs=[col(0), st],
        out_shape=[jax.ShapeDtypeStruct((L, D), BF16), jax.ShapeDtypeStruct((HG_H, nc, HG_D, HG_D), F32)],
        scratch_shapes=[pltpu.VMEM((HG_HPS, HG_D, HG_D), F32)],
        compiler_params=_cparams(("parallel", "arbitrary")),
    )(proj, proj, proj, proj, hlb, gg, rmat, masks)


def _hgrn_bwd(proj, dog, states, hlb, gg, rmat, masks):
    L = proj.shape[0]
    C = HG_C
    nc = L // C
    col, st, consts = _hgrn_specs(nc, True)

    def body(hq_ref, hz_ref, hi_ref, hg_ref, do_ref, hlb_ref, gg_ref, r_ref, m_ref, st_ref,
             dq_ref, dz_ref, di_ref, dg_ref, dlb_ref, dgg_ref, dstate):
        c = pl.program_id(1)

        @pl.when(c == 0)
        def _():
            dstate[...] = jnp.zeros_like(dstate)
            dlb_ref[...] = jnp.zeros_like(dlb_ref)
            dgg_ref[...] = jnp.zeros_like(dgg_ref)

        _interleave([bwd_head(c, hh, slice(hh * HG_D, (hh + 1) * HG_D), hq_ref, hz_ref, hi_ref, hg_ref, do_ref, hlb_ref,
                              gg_ref, r_ref, m_ref, st_ref, dq_ref, dz_ref, di_ref, dg_ref, dlb_ref, dgg_ref, dstate)
                     for hh in range(HG_HPS)], skew=HG_SKEW)

    def bwd_head(c, hh, sl, hq_ref, hz_ref, hi_ref, hg_ref, do_ref, hlb_ref, gg_ref, r_ref, m_ref, st_ref,
                 dq_ref, dz_ref, di_ref, dg_ref, dlb_ref, dgg_ref, dstate):
        hq, hz, v, hg = hq_ref[:, sl], hz_ref[:, sl], hi_ref[:, sl], hg_ref[:, sl]
        dout = do_ref[:, sl].astype(F32)
        gain = gg_ref[...]
        masks_v = m_ref[...]
        lb, q, sq, k, sz, snz, g = _hg_pre(hq, hz, hlb_ref[0:1, sl], hlb_ref[1:2, sl])
        yield
        fall = _hg_decays(g, r_ref[...])
        fb, fe = fall[0:C], fall[C:2 * C]
        yield
        for a in _hg_intra_levels(q, k, fall, masks_v):
            yield
        st0 = st_ref[hh, 0]
        st0b = st0.astype(BF16)
        ebc = jnp.exp(jnp.sum(g, axis=0, keepdims=True))
        qb, ke, vb = (q * fb).astype(BF16), (k * fe).astype(BF16), v.astype(BF16)
        ab = a.astype(BF16)
        o = jnp.dot(ab, vb, preferred_element_type=F32) + lax.dot_general(qb, st0b, NT, preferred_element_type=F32)
        yield
        r = lax.rsqrt(jnp.mean(o * o, axis=-1, keepdims=True) + EPS)
        oh = o * r
        sg = _sigmoid(hg)
        d_on = dout * (hg * sg)
        dhg = dout * (oh * gain) * (sg * (1.0 + hg * (1.0 - sg)))
        dgg_ref[hh] += jnp.sum(d_on * oh, axis=0, keepdims=True)
        dxh = d_on * gain
        do = r * (dxh - oh * jnp.mean(dxh * oh, axis=-1, keepdims=True))
        dob = do.astype(BF16)
        yield
        dsp = dstate[hh]
        dspb = dsp.astype(BF16)
        causal = _iota((C, C), 0) >= _iota((C, C), 1)
        da = jnp.where(causal, lax.dot_general(dob, vb, NT, preferred_element_type=F32), 0.0)
        diag = jnp.sum(do * v, axis=-1, keepdims=True)
        dv = lax.dot_general(ab, dob, TN, preferred_element_type=F32)
        dv = dv + lax.dot_general(ke, dspb, NT, preferred_element_type=F32)
        xq = jnp.dot(dob, st0b, preferred_element_type=F32)
        xk = jnp.dot(vb, dspb, preferred_element_type=F32)
        dq = diag * k + fb * xq
        dk = diag * q + fe * xk
        ke_xk = ke.astype(F32) * xk
        db = qb.astype(F32) * xq - ke_xk
        yield
        for l in range(HG_LEV):
            f = fall[(2 + l) * C:(3 + l) * C]
            dal = (masks_v[l] * da).astype(BF16)
            ql, kl = (q * f).astype(BF16), (k * f).astype(BF16)
            xq = jnp.dot(dal, kl, preferred_element_type=F32)
            xk = lax.dot_general(dal, ql, TN, preferred_element_type=F32)
            dq = dq + f * xq
            dk = dk + f * xk
            db = db + ql.astype(F32) * xq - kl.astype(F32) * xk
            yield
        dstate[hh] = dsp * ebc + lax.dot_general(dob, qb, TN, preferred_element_type=F32)
        triu = (_iota((C, C), 0) <= _iota((C, C), 1)).astype(F32)
        dg = jnp.dot(triu, db, precision=HIGHEST, preferred_element_type=F32)
        dg = dg + jnp.sum(st0 * ebc * dsp, axis=0, keepdims=True) + jnp.sum(ke_xk, axis=0, keepdims=True)
        keep = _row_ids((nc - 1 - c) * C, C) >= ROW0
        dg = jnp.where(keep, dg, 0.0)
        dk = jnp.where(keep, dk, 0.0)
        f_gate = lb + (1.0 - lb) * sz
        dfdz = (1.0 - lb) * sz * snz
        dz_ref[:, sl] = (dg * dfdz / f_gate - dk * dfdz).astype(dz_ref.dtype)
        dlb_ref[:, sl] += jnp.sum(dg * snz / f_gate - dk * snz, axis=0, keepdims=True)
        dq_ref[:, sl] = jnp.where(keep, dq * (sq * (1.0 + hq * (1.0 - sq))), 0.0).astype(dq_ref.dtype)
        di_ref[:, sl] = jnp.where(keep, dv, 0.0).astype(di_ref.dtype)
        dg_ref[:, sl] = jnp.where(keep, dhg, 0.0).astype(dg_ref.dtype)

    w = HG_HPS * HG_D
    outs = pl.pallas_call(
        body, name="hgrn_bwd", grid=(HG_H // HG_HPS, nc),
        in_specs=[col(0), col(HG_H), col(2 * HG_H), col(3 * HG_H), col(0)] + consts + [st],
        out_specs=[col(0), col(0), col(0), col(0), pl.BlockSpec((1, w), lambda h, c: (0, h)),
                   pl.BlockSpec((HG_HPS, 1, HG_D), lambda h, c: (h, 0, 0))],
        out_shape=[jax.ShapeDtypeStruct((L, D), BF16)] * 4 + [jax.ShapeDtypeStruct((1, D), F32),
                                                              jax.ShapeDtypeStruct((HG_H, 1, HG_D), F32)],
        scratch_shapes=[pltpu.VMEM((HG_HPS, HG_D, HG_D), F32)],
        compiler_params=_cparams(("parallel", "arbitrary")),
    )(proj, proj, proj, proj, dog, hlb, gg, rmat, masks, states)
    return outs


def _ffn_fwd(h, norm_gain, wg, wu, wo, tag):
    hn = _rms_fwd(h, norm_gain, f"{tag}_norm")
    g, u, act = _ffn_in(hn, wg, wu, f"{tag}_in")
    h_out = _matmul(act, wo, add=h, name=f"{tag}_out")
    return h_out, (h, hn, g, u, act)


def _ffn_bwd(dh, saved, norm_gain, wg, wu, wo, tag):
    h, hn, g, u, act = saved
    dg, du = _ffn_dact(dh, wo, g, u, f"{tag}_dact")
    d_wo = _matmul(act, dh, ta=True, name=f"{tag}_dwo")
    dhn = _matmul(dg, wg, tb=True, name=f"{tag}_dhn_g")
    dhn = _matmul(du, wu, tb=True, add=dhn, name=f"{tag}_dhn_u")
    d_wg = _matmul(hn, dg, ta=True, name=f"{tag}_dwg")
    d_wu = _matmul(hn, du, ta=True, name=f"{tag}_dwu")
    dh, d_gain = _rms_bwd(h, dhn, norm_gain, dh, f"{tag}_norm_bwd")
    return dh, d_gain, (d_wg, d_wu, d_wo)


def _local_step(h0, tgt, w, late_shards):
    L = h0.shape[0]
    gmat = jnp.asarray(np.kron(np.eye(MXU_N // FOX_DH), np.ones((FOX_DH, FOX_DH))).astype(np.float32), BF16)
    rmat, lmasks = _hgrn_consts()
    an, fn_ = w["attn_norm"], w["ffn_norm"]
    qg = jnp.tile(w["fox_q_norm"], (1, FOX_H))
    kg = jnp.tile(w["fox_k_norm"], (1, FOX_H))
    bf = jnp.pad(w["fox_b_f"], ((0, 0), (0, LANES - FOX_H)))
    fw = w["fox_w_in"]
    f_wq, f_wk, f_wv, f_wg = (fw[:, i * D:(i + 1) * D] for i in range(4))
    f_wf = jnp.pad(fw[:, 4 * D:], ((0, 0), (0, LANES - FOX_H)))

    hn0 = _rms_fwd(h0, an[0:1], "fox_norm")
    q_raw = _matmul(hn0, f_wq, name="fox_q")
    k_raw = _matmul(hn0, f_wk, name="fox_k")
    v = _matmul(hn0, f_wv, out_dtype=BF16, name="fox_v")
    gate = _matmul(hn0, f_wg, out_dtype=BF16, name="fox_gate")
    flog = _matmul(hn0, f_wf, name="fox_flog")
    cq, ck = _fox_cumsum_fwd(flog, bf)
    qp, kp, vp = _fox_pack_fwd(q_raw, k_raw, v, cq, ck, qg, kg, gmat)
    o, og, lse2, gathered = _fox_attn_fwd(qp, kp, vp, gate, [late_shards[n] for n in GATHER_LATE])
    late = dict(zip(GATHER_LATE, gathered))
    f_wo = late["fox_w_out"].reshape(D, D)
    h_wo = late["hgrn_w_out"].reshape(D, D)
    h_wi = jnp.concatenate(list(late["hgrn_w_in"]), axis=1)
    g_in, g_out = late["ffn_w_in"], late["ffn_w_out"]
    ffw = []
    for i in range(2):
        rows_in, rows_out = slice(i * D, (i + 1) * D), slice(i * FFN // 4, (i + 1) * FFN // 4)
        ffw.append((jnp.concatenate([g_in[0, rows_in], g_in[1, rows_in]], axis=1),
                    jnp.concatenate([g_in[2, rows_in], g_in[3, rows_in]], axis=1),
                    jnp.concatenate([g_out[j, rows_out] for j in range(4)], axis=0)))
    h1 = _matmul(og, f_wo, add=h0, name="fox_out")
    h2, ffn0 = _ffn_fwd(h1, fn_[0:1], *ffw[0], "ffn0")

    hn2 = _rms_fwd(h2, an[1:2], "hgrn_norm")
    proj = _matmul(hn2, h_wi, name="hgrn_in")
    og1, states = _hgrn_fwd(proj, w["hgrn_lower_bounds"], w["hgrn_g_norm"], rmat, lmasks)
    h3 = _matmul(og1, h_wo, add=h2, name="hgrn_out")
    h4, ffn1 = _ffn_fwd(h3, fn_[1:2], *ffw[1], "ffn1")

    loss, dh, d_final = _loss_bwd(h4, tgt, w["final_norm"])

    dh, d_fn1, d_ffn1 = _ffn_bwd(dh, ffn1, fn_[1:2], *ffw[1], "ffn1")
    dog1 = _matmul(dh, h_wo, tb=True, out_dtype=BF16, name="hgrn_dog")
    d_h_wo = _matmul(og1, dh, ta=True, name="hgrn_dwo")
    dpq, dpz, dpi, dpg, d_lb, d_gg = _hgrn_bwd(proj, dog1, states, w["hgrn_lower_bounds"], w["hgrn_g_norm"],
                                               rmat, lmasks)
    dproj = jnp.concatenate([dpq, dpz, dpi, dpg], axis=1)
    dhn2 = _matmul(dproj, h_wi, tb=True, name="hgrn_dhn")
    d_h_wi = _matmul(hn2, dproj, ta=True, name="hgrn_dwi")
    dh, d_an1 = _rms_bwd(h2, dhn2, an[1:2], dh, "hgrn_norm_bwd")

    dh, d_fn0, d_ffn0 = _ffn_bwd(dh, ffn0, fn_[0:1], *ffw[0], "ffn0")
    n_in, n_out = 2 * FFN // 4, FFN // 4
    d_ffn = [d_ffn0, d_ffn1]
    late_grads = dict(
        hgrn_w_in=_to_shards("hgrn_w_in", d_h_wi[None]), hgrn_w_out=d_h_wo.reshape(4, D // 4, D),
        ffn_w_in=jnp.stack([jnp.concatenate([d[j // 2][:, (j % 2) * n_in:(j % 2 + 1) * n_in] for d in d_ffn], axis=0)
                            for j in range(4)]),
        ffn_w_out=jnp.stack([jnp.concatenate([d[2][j * n_out:(j + 1) * n_out] for d in d_ffn], axis=0)
                             for j in range(4)]))
    pair_late, send_late = _pair_sums([late_grads[n] for n in LATE_NAMES], "late")

    dog = _matmul(dh, f_wo, tb=True, out_dtype=BF16, name="fox_dog")
    d_f_wo = _matmul(og, dh, ta=True, name="fox_dwo")

    def by_head(a):
        return jnp.pad(a.transpose(1, 0, 2).reshape(L, FOX_H), ((0, 0), (0, LANES - FOX_H)))

    qb = _fox_pack_bias(qp, cq, by_head(lse2.transpose(0, 2, 1)))
    dop, dgate = _fox_pack_bwd(dog, o, gate)
    dqp, dk, dv, dck, recv_late = _fox_attn_bwd(qb, kp, vp, dop, send_late)
    (dq_raw, dk_raw, dc_q), (d_qg, d_kg) = _fox_unpack_bwd(q_raw, k_raw, dqp, dk, qg, kg, gmat)
    dflog, d_bf = _fox_cumsum_bwd(dc_q, by_head(dck), flog, bf)
    dproj0 = jnp.concatenate([dq_raw, dk_raw, dv, dgate, dflog.astype(BF16)], axis=1)
    f_wall = jnp.concatenate([f_wq, f_wk, f_wv, f_wg, f_wf], axis=1)
    dhn0 = _matmul(dproj0, f_wall, tb=True, name="fox_dhn")
    d_f_wall = _matmul(hn0, dproj0, ta=True, name="fox_dwi")
    d_f_wi = d_f_wall[:, :4 * D + FOX_H]
    dh, d_an0 = _rms_bwd(h0, dhn0, an[0:1], dh, "fox_norm_bwd")

    fox_grads = dict(fox_w_in=d_f_wi[None], fox_w_out=d_f_wo[None])
    pair_fox, send_fox = _pair_sums([_to_shards(n, fox_grads[n]) for n in FOX_NAMES], "fox")
    recv_fox = _chip_scatter(send_fox, "fox")
    halves = _chip_sums(pair_fox, recv_fox, "fox") + _chip_sums(pair_late, recv_late, "late")
    theirs = _sibling_exchange(halves)
    big = {n: (m, t) for n, m, t in zip(FOX_NAMES + LATE_NAMES, halves, theirs)}
    small = dict(attn_norm=jnp.concatenate([d_an0, d_an1]), ffn_norm=jnp.concatenate([d_fn0, d_fn1]),
                 final_norm=d_final, lb_raw=d_lb, q_gain=d_qg, k_gain=d_kg, b_f=d_bf,
                 g_gain=d_gg.reshape(1, D))
    return loss, dh, big, small


def _me():
    return lax.axis_index("x"), lax.axis_index("y"), lax.axis_index("c")


def _flip(v, bit):
    return 1 - v if bit else v


def _chip_allgather(arrs):
    n = len(arrs)

    def body(*refs):
        _gather_start(refs[:n], refs[n:2 * n], *refs[2 * n:])
        _gather_wait(refs[:n], refs[n:2 * n], *refs[2 * n:])

    return pl.pallas_call(
        body, name="chip_allgather", in_specs=[ANY] * n, out_specs=[ANY] * n,
        out_shape=[jax.ShapeDtypeStruct((4,) + a.shape, a.dtype) for a in arrs],
        scratch_shapes=_gather_sems(n),
    )(*arrs)


def _chip_peers():
    x, y, c = _me()
    return [(1 - x, y, c), (x, 1 - y, c), (1 - x, 1 - y, c)]


def _gather_sems(n):
    return [pltpu.SemaphoreType.DMA((n, 3)), pltpu.SemaphoreType.DMA((n, 3)), pltpu.SemaphoreType.DMA((n,))]


def _gather_copies(ins, outs, ssem, rsem, lsem, with_recvs):
    x, y, _ = _me()
    local, sends, recvs = [], [], []
    for a in range(len(ins)):
        local.append(pltpu.make_async_copy(ins[a], outs[a].at[2 * x + y], lsem.at[a]))
        for k, peer in enumerate(_chip_peers()):
            sends.append(pltpu.make_async_remote_copy(ins[a], outs[a].at[2 * x + y], ssem.at[a, k], rsem.at[a, k],
                                                      device_id=peer, device_id_type=MESH))
            if with_recvs:
                recvs.append(pltpu.make_async_remote_copy(ins[a], outs[a].at[2 * peer[0] + peer[1]], ssem.at[a, k],
                                                          rsem.at[a, k], device_id=peer, device_id_type=MESH))
    return local, sends, recvs


def _gather_start(ins, outs, ssem, rsem, lsem):
    local, sends, _ = _gather_copies(ins, outs, ssem, rsem, lsem, False)
    for cp in local + sends:
        cp.start()


def _gather_wait(ins, outs, ssem, rsem, lsem):
    local, sends, recvs = _gather_copies(ins, outs, ssem, rsem, lsem, True)
    for cp in local:
        cp.wait()
    for cp in sends:
        cp.wait_send()
    for cp in recvs:
        cp.wait_recv()


def _scatter_copies(ins, outs, ssem, rsem):
    copies = []
    for a in range(len(ins)):
        for k, peer in enumerate(_chip_peers()):
            copies.append(pltpu.make_async_remote_copy(ins[a].at[2 * peer[0] + peer[1]], outs[a].at[k], ssem.at[a, k],
                                                       rsem.at[a, k], device_id=peer, device_id_type=MESH))
    return copies


def _device_allgather(arr):
    def body(in_ref, out_ref, ssem, rsem, lsem):
        x, y, c = _me()
        me = 4 * x + 2 * y + c
        peers = [(_flip(x, k & 4), _flip(y, k & 2), _flip(c, k & 1)) for k in range(1, 8)]
        local = pltpu.make_async_copy(in_ref, out_ref.at[me], lsem)
        local.start()
        sends = []
        for k, peer in enumerate(peers):
            cp = pltpu.make_async_remote_copy(in_ref, out_ref.at[me], ssem.at[k], rsem.at[k],
                                              device_id=peer, device_id_type=MESH)
            cp.start()
            sends.append(cp)
        local.wait()
        for cp in sends:
            cp.wait_send()
        for k, peer in enumerate(peers):
            pltpu.make_async_remote_copy(in_ref, out_ref.at[4 * peer[0] + 2 * peer[1] + peer[2]], ssem.at[k],
                                         rsem.at[k], device_id=peer, device_id_type=MESH).wait_recv()

    return pl.pallas_call(
        body, name="device_allgather", in_specs=[ANY], out_specs=ANY,
        out_shape=jax.ShapeDtypeStruct((8,) + arr.shape, arr.dtype),
        scratch_shapes=[pltpu.SemaphoreType.DMA((7,)), pltpu.SemaphoreType.DMA((7,)), pltpu.SemaphoreType.DMA],
    )(arr)


def _sibling_send_other_half(arrs, tag):
    n = len(arrs)

    def body(*refs):
        ins, outs = refs[:n], refs[n:2 * n]
        ssem, rsem = refs[2 * n:]
        x, y, c = _me()
        cps = []
        for a in range(n):
            half = ins[a].shape[1] // 2
            src = ins[a].at[:, pl.ds((1 - c) * half, half), :]
            cp = pltpu.make_async_remote_copy(src, outs[a], ssem.at[a], rsem.at[a],
                                              device_id=(x, y, 1 - c), device_id_type=MESH)
            cp.start()
            cps.append(cp)
        for cp in cps:
            cp.wait()

    return pl.pallas_call(
        body, name=f"grad_sibling_swap_{tag}", in_specs=[ANY] * n, out_specs=[ANY] * n,
        out_shape=[jax.ShapeDtypeStruct((4, a.shape[1] // 2, a.shape[2]), a.dtype) for a in arrs],
        scratch_shapes=[pltpu.SemaphoreType.DMA((n,)), pltpu.SemaphoreType.DMA((n,))],
    )(*arrs)


def _chip_scatter(arrs, tag):
    n = len(arrs)

    def body(*refs):
        cps = _scatter_copies(refs[:n], refs[n:2 * n], *refs[2 * n:])
        for cp in cps:
            cp.start()
        for cp in cps:
            cp.wait()

    return pl.pallas_call(
        body, name=f"grad_chip_scatter_{tag}", in_specs=[ANY] * n, out_specs=[ANY] * n,
        out_shape=[jax.ShapeDtypeStruct((3,) + a.shape[1:], a.dtype) for a in arrs],
        scratch_shapes=[pltpu.SemaphoreType.DMA((n, 3)), pltpu.SemaphoreType.DMA((n, 3))],
    )(*arrs)


def _sibling_exchange(arrs):
    n = len(arrs)

    def body(*refs):
        ins, outs = refs[:n], refs[n:2 * n]
        ssem, rsem = refs[2 * n:]
        x, y, c = _me()
        cps = [pltpu.make_async_remote_copy(ins[a], outs[a], ssem.at[a], rsem.at[a], device_id=(x, y, 1 - c),
                                            device_id_type=MESH) for a in range(n)]
        for cp in cps:
            cp.start()
        for cp in cps:
            cp.wait()

    return pl.pallas_call(
        body, name="grad_sibling_exchange", in_specs=[ANY] * n, out_specs=[ANY] * n,
        out_shape=[jax.ShapeDtypeStruct(a.shape, a.dtype) for a in arrs],
        scratch_shapes=[pltpu.SemaphoreType.DMA((n,)), pltpu.SemaphoreType.DMA((n,))],
    )(*arrs)


def _pair_sums(grads, tag):
    got = _sibling_send_other_half(grads, tag)
    res = [_pair_add(g, t, f"grad_pair_add_{tag}{i}") for i, (g, t) in enumerate(zip(grads, got))]
    return [r[0] for r in res], [r[1] for r in res]


def _mesh_scalar(v):
    return jnp.asarray(v, jnp.int32).reshape(1)


def _pair_add(g, t, name):
    _, rows, cols = g.shape
    half = rows // 2
    tm = _tile(half, cap=(2 * 1024 * 1024) // (4 * cols))

    def body(c_ref, g_ref, t_ref, o_ref, ob_ref):
        s = g_ref[0, 0] + t_ref[0]
        o_ref[0] = s
        ob_ref[0] = s.astype(ob_ref.dtype)

    spec = pl.BlockSpec((1, tm, cols), lambda j, i, c: (j, i, 0))
    return pl.pallas_call(
        body, name=name,
        grid_spec=pltpu.PrefetchScalarGridSpec(
            num_scalar_prefetch=1, grid=(4, half // tm),
            in_specs=[pl.BlockSpec((1, 1, tm, cols), lambda j, i, c: (j, c[0], i, 0)), spec],
            out_specs=[spec, spec]),
        out_shape=[jax.ShapeDtypeStruct(t.shape, F32), jax.ShapeDtypeStruct(t.shape, BF16)],
        compiler_params=_cparams(("parallel", "parallel")),
    )(_mesh_scalar(lax.axis_index("c")), g.reshape(4, 2, half, cols), t)


def _chip_sums(pair, recv, tag):
    x, y, _ = _me()
    out = []
    for n, (p, r) in enumerate(zip(pair, recv)):
        _, half, cols = p.shape
        tm = _tile(half, cap=(2 * 1024 * 1024) // (4 * cols))

        def body(j_ref, p_ref, r_ref, o_ref):
            o_ref[...] = p_ref[0] + r_ref[0].astype(F32) + r_ref[1].astype(F32) + r_ref[2].astype(F32)

        out.append(pl.pallas_call(
            body, name=f"grad_chip_add_{tag}{n}",
            grid_spec=pltpu.PrefetchScalarGridSpec(
                num_scalar_prefetch=1, grid=(half // tm,),
                in_specs=[pl.BlockSpec((1, tm, cols), lambda i, j: (j[0], i, 0)),
                          pl.BlockSpec((3, tm, cols), lambda i, j: (0, i, 0))],
                out_specs=pl.BlockSpec((tm, cols), lambda i, j: (i, 0))),
            out_shape=jax.ShapeDtypeStruct((half, cols), F32),
            compiler_params=_cparams(("parallel",)),
        )(_mesh_scalar(2 * x + y), p, r))
    return out


SMALL_ROWS = 32


def _small_finalize(gathered, hlb, fold64, fold128):
    def body(g_ref, hlb_ref, f64_ref, f128_ref, rows_ref, qk_ref, gg_ref, lb_ref):
        tot = g_ref[0]
        for d in range(1, 8):
            tot = tot + g_ref[d]
        rows_ref[...] = tot
        qk_ref[...] = jnp.dot(rows_ref[6:8, :], f64_ref[...], precision=HIGHEST, preferred_element_type=F32)
        gg_ref[...] = jnp.dot(rows_ref[9:10, :], f128_ref[...], precision=HIGHEST, preferred_element_type=F32)
        h0, h1 = hlb_ref[0:1, :], hlb_ref[1:2, :]
        mx = jnp.maximum(h0, h1)
        e0, e1 = jnp.exp(h0 - mx), jnp.exp(h1 - mx)
        lb = e1 / (e0 + e1)
        d1 = rows_ref[5:6, :] * lb * (1.0 - lb)
        lb_ref[...] = jnp.where(_iota((2, 1), 0) == 0, -d1, d1)

    return pl.pallas_call(
        body, name="small_finalize",
        out_shape=[jax.ShapeDtypeStruct((SMALL_ROWS, D), F32), jax.ShapeDtypeStruct((2, FOX_DH), F32),
                   jax.ShapeDtypeStruct((1, HG_D), F32), jax.ShapeDtypeStruct((2, D), F32)],
    )(gathered, hlb, fold64, fold128)


FOX_NAMES = ("fox_w_in", "fox_w_out")
LATE_NAMES = ("hgrn_w_in", "hgrn_w_out", "ffn_w_in", "ffn_w_out")
BIG_NAMES = FOX_NAMES + LATE_NAMES
GATHER_LATE = ("fox_w_out",) + LATE_NAMES
COL_SHARDED = ("fox_w_in", "hgrn_w_in", "ffn_w_in")


def _shard2d(name, a):
    return a.reshape(-1, a.shape[-1])


def _to_shards(name, g):
    layers = g.shape[0]
    if name in COL_SHARDED:
        k, n = g.shape[1], g.shape[2] // 4
        return g.reshape(layers, k, 4, n).transpose(2, 0, 1, 3).reshape(4, layers * k, n)
    r = g.shape[1] // 4
    return g.reshape(layers, 4, r, g.shape[2]).transpose(1, 0, 2, 3).reshape(4, layers * r, g.shape[2])


def kernel(x, meta_tokens, attn_norm, ffn_norm, final_norm, fox_w_in, fox_b_f, fox_q_norm, fox_k_norm, fox_w_out, hgrn_w_in, hgrn_lower_bounds, hgrn_g_norm, hgrn_w_out, ffn_w_in, ffn_w_out, loss_target, m_meta_tokens, m_attn_norm, m_ffn_norm, m_final_norm, m_fox_w_in, m_fox_b_f, m_fox_q_norm, m_fox_k_norm, m_fox_w_out, m_hgrn_w_in, m_hgrn_lower_bounds, m_hgrn_g_norm, m_hgrn_w_out, m_ffn_w_in, m_ffn_w_out, v_meta_tokens, v_attn_norm, v_ffn_norm, v_final_norm, v_fox_w_in, v_fox_b_f, v_fox_q_norm, v_fox_k_norm, v_fox_w_out, v_hgrn_w_in, v_hgrn_lower_bounds, v_hgrn_g_norm, v_hgrn_w_out, v_ffn_w_in, v_ffn_w_out):
    params = dict(meta_tokens=meta_tokens, attn_norm=attn_norm, ffn_norm=ffn_norm, final_norm=final_norm,
                  fox_w_in=fox_w_in, fox_b_f=fox_b_f, fox_q_norm=fox_q_norm, fox_k_norm=fox_k_norm,
                  fox_w_out=fox_w_out, hgrn_w_in=hgrn_w_in, hgrn_lower_bounds=hgrn_lower_bounds,
                  hgrn_g_norm=hgrn_g_norm, hgrn_w_out=hgrn_w_out, ffn_w_in=ffn_w_in, ffn_w_out=ffn_w_out)
    mom_m = dict(meta_tokens=m_meta_tokens, attn_norm=m_attn_norm, ffn_norm=m_ffn_norm, final_norm=m_final_norm,
                 fox_w_in=m_fox_w_in, fox_b_f=m_fox_b_f, fox_q_norm=m_fox_q_norm, fox_k_norm=m_fox_k_norm,
                 fox_w_out=m_fox_w_out, hgrn_w_in=m_hgrn_w_in, hgrn_lower_bounds=m_hgrn_lower_bounds,
                 hgrn_g_norm=m_hgrn_g_norm, hgrn_w_out=m_hgrn_w_out, ffn_w_in=m_ffn_w_in, ffn_w_out=m_ffn_w_out)
    mom_v = dict(meta_tokens=v_meta_tokens, attn_norm=v_attn_norm, ffn_norm=v_ffn_norm, final_norm=v_final_norm,
                 fox_w_in=v_fox_w_in, fox_b_f=v_fox_b_f, fox_q_norm=v_fox_q_norm, fox_k_norm=v_fox_k_norm,
                 fox_w_out=v_fox_w_out, hgrn_w_in=v_hgrn_w_in, hgrn_lower_bounds=v_hgrn_lower_bounds,
                 hgrn_g_norm=v_hgrn_g_norm, hgrn_w_out=v_hgrn_w_out, ffn_w_in=v_ffn_w_in, ffn_w_out=v_ffn_w_out)
    names = list(params)
    xi, yi, _ = _me()

    shards = {n: _shard2d(n, params[n]).astype(BF16) for n in BIG_NAMES}
    w_in_g, meta_g = _chip_allgather([shards["fox_w_in"], meta_tokens])
    w = dict(fox_w_in=jnp.concatenate(list(w_in_g), axis=1))
    meta_full = jnp.concatenate(list(meta_g), axis=1)
    w.update(attn_norm=attn_norm, ffn_norm=ffn_norm, final_norm=final_norm.reshape(1, D), fox_b_f=fox_b_f,
             fox_q_norm=fox_q_norm, fox_k_norm=fox_k_norm, hgrn_lower_bounds=hgrn_lower_bounds,
             hgrn_g_norm=hgrn_g_norm)

    h0 = jnp.concatenate([jnp.zeros((ROW0, D), F32), meta_full, x[0]], axis=0)
    loss, dh0, big, small = _local_step(h0, loss_target[0], w, {n: shards[n] for n in GATHER_LATE})
    loss = lax.psum(loss, ("x", "y", "c"))
    grad_x = dh0[PAD:][None]
    grads = {}

    rows = jnp.concatenate([small["attn_norm"], small["ffn_norm"], small["final_norm"], small["lb_raw"],
                            small["q_gain"], small["k_gain"],
                            jnp.pad(small["b_f"], ((0, 0), (0, D - LANES))), small["g_gain"],
                            dh0[ROW0:PAD], jnp.zeros((SMALL_ROWS - 10 - N_META, D), F32)], axis=0)
    allrows = _device_allgather(rows)
    fold64 = jnp.asarray(np.tile(np.eye(FOX_DH, dtype=np.float32), (FOX_H, 1)))
    fold128 = jnp.asarray(np.tile(np.eye(HG_D, dtype=np.float32), (HG_H, 1)))
    tot, qk, gg, dlb = _small_finalize(allrows, hgrn_lower_bounds, fold64, fold128)
    grads.update(attn_norm=tot[0:2], ffn_norm=tot[2:4], final_norm=tot[4], hgrn_lower_bounds=dlb,
                 fox_q_norm=qk[0:1], fox_k_norm=qk[1:2], fox_b_f=tot[8:9, :FOX_H], hgrn_g_norm=gg,
                 meta_tokens=lax.dynamic_slice_in_dim(tot[10:10 + N_META], (2 * xi + yi) * (D // 4), D // 4, axis=1))

    delta, new_m, new_v = {}, {}, {}
    for n in BIG_NAMES:
        res = _adamw_halves(_shard2d(n, params[n]), *big[n], _shard2d(n, mom_m[n]), _shard2d(n, mom_v[n]),
                            f"adamw_{n}")
        grads[n], delta[n], new_m[n], new_v[n] = (t.reshape(params[n].shape) for t in res)
    delta["meta_tokens"], new_m["meta_tokens"], new_v["meta_tokens"] = _adamw(
        meta_tokens, grads["meta_tokens"], m_meta_tokens, v_meta_tokens, "adamw_meta_tokens")
    small_names = [n for n in names if n not in BIG_NAMES and n != "meta_tokens"]

    def pack(d):
        return jnp.concatenate([jnp.pad(d[n].reshape(-1, d[n].shape[-1]), ((0, 0), (0, D - d[n].shape[-1])))
                                for n in small_names], axis=0)

    packed = [pack(t) for t in (params, grads, mom_m, mom_v)]
    n_rows = packed[0].shape[0]
    packed = [jnp.pad(t, ((0, 16 - n_rows), (0, 0))) for t in packed]
    res = _adamw(*packed, "adamw_small")
    r0 = 0
    for n in small_names:
        nr = params[n].reshape(-1, params[n].shape[-1]).shape[0]
        for dst, src in zip((delta, new_m, new_v), res):
            dst[n] = src[r0:r0 + nr, :params[n].shape[-1]].reshape(params[n].shape)
        r0 += nr

    return (loss, grad_x, *[grads[n] for n in names], *[delta[n] for n in names],
            *[new_m[n] for n in names], *[new_v[n] for n in names])
```

```python
import functools

import numpy as np
import jax
import jax.numpy as jnp
from jax import lax
from jax.experimental import pallas as pl
from jax.experimental.pallas import tpu as pltpu

F32, BF16 = jnp.float32, jnp.bfloat16
HIGHEST = lax.Precision.HIGHEST

D = 1024
N_META = 16
PAD = 128
ROW0 = PAD - N_META
FOX_H, FOX_DH = 16, 64
HG_H, HG_D = 8, 128
HG_C = 128
HG_LEV = 7
HG_HPS = 8
HG_SKEW = 0
FFN = 2816
EPS = 1e-6
BIG = 1e30
LOG2E = 1.4426950408889634
LANES = 128
MXU_N = 256
VMEM_LIMIT = 48 * 1024 * 1024
ROW_TILES = (640, 512, 384, 320, 256, 128, 64, 32, 16, 8)
ATTN_TILES = (640, 512, 256, 128)
FOX_HPS_FWD = 8
FOX_HPS_BWD = 4

ADAM_LR, ADAM_B1, ADAM_B2, ADAM_EPS, ADAM_WD, ADAM_STEP = 0.001, 0.9, 0.999, 1e-08, 0.01, 10

MESH = pl.DeviceIdType.MESH
ANY = pl.BlockSpec(memory_space=pl.ANY)
NT = (((1,), (1,)), ((), ()))
TN = (((0,), (0,)), ((), ()))


def _tile(n, cands=ROW_TILES, cap=None):
    for c in cands:
        if n % c == 0 and (cap is None or c <= cap):
            return c
    return n


def _cparams(sem):
    return pltpu.CompilerParams(dimension_semantics=sem, vmem_limit_bytes=VMEM_LIMIT)


def _sigmoid(x):
    return jax.nn.sigmoid(x)


def _log_sigmoid(x):
    return jnp.minimum(x, 0.0) - jnp.log(1.0 + jnp.exp(-jnp.abs(x)))


def _iota(shape, dim):
    return lax.broadcasted_iota(jnp.int32, shape, dim)


def _matmul(a, b, *, ta=False, tb=False, out_dtype=F32, add=None, name):
    if ta:
        kdim, m = a.shape
    else:
        m, kdim = a.shape
    n = b.shape[0] if tb else b.shape[1]
    if ta:
        tm = m if m <= 1024 else _tile(m, (1408, 1024, 512, 256, 128))
        tk = _tile(kdim, (1664,) + ROW_TILES)
    else:
        tm = _tile(m)
        tk = kdim if kdim <= 4096 else _tile(kdim, (2048, 1024, 512))
    tn = n if n <= 1024 else _tile(n, (1408, 1024, 512, 256, 128))
    nk = kdim // tk
    dn = (((0 if ta else 1,), (1 if tb else 0,)), ((), ()))

    def body(*refs):
        if add is None:
            a_ref, b_ref, o_ref, acc_ref = refs
        else:
            a_ref, b_ref, add_ref, o_ref, acc_ref = refs
        k = pl.program_id(2)

        @pl.when(k == 0)
        def _():
            acc_ref[...] = jnp.zeros_like(acc_ref)

        acc_ref[...] += lax.dot_general(a_ref[...].astype(BF16), b_ref[...].astype(BF16), dn,
                                        preferred_element_type=F32)

        @pl.when(k == nk - 1)
        def _():
            r = acc_ref[...]
            if add is not None:
                r = r + add_ref[...].astype(F32)
            o_ref[...] = r.astype(o_ref.dtype)

    a_spec = pl.BlockSpec((tk, tm), lambda j, i, k: (k, i)) if ta else pl.BlockSpec((tm, tk), lambda j, i, k: (i, k))
    b_spec = pl.BlockSpec((tn, tk), lambda j, i, k: (j, k)) if tb else pl.BlockSpec((tk, tn), lambda j, i, k: (k, j))
    o_spec = pl.BlockSpec((tm, tn), lambda j, i, k: (i, j))
    ins, specs = [a, b], [a_spec, b_spec]
    if add is not None:
        ins.append(add)
        specs.append(o_spec)
    return pl.pallas_call(
        body, name=name, grid=(n // tn, m // tm, nk), in_specs=specs, out_specs=o_spec,
        out_shape=jax.ShapeDtypeStruct((m, n), out_dtype),
        scratch_shapes=[pltpu.VMEM((tm, tn), F32)],
        compiler_params=_cparams(("parallel", "parallel", "arbitrary")),
    )(*ins)


def _rowwise(fn, ins, bcast, outs, accs, *, name, reverse=False, carry=None, as_refs=False):
    rows = ins[0].shape[0]
    per_row = sum(x.shape[1] * x.dtype.itemsize for x in ins) + sum(c * jnp.dtype(d).itemsize for c, d in outs)
    tm = _tile(rows, cap=max(8, (10 * 1024 * 1024) // per_row))
    n = rows // tm
    n_in, n_b, n_o, n_a = len(ins), len(bcast), len(outs), len(accs)

    def body(*refs):
        in_refs = refs[:n_in]
        b_refs = refs[n_in:n_in + n_b]
        o_refs = refs[n_in + n_b:n_in + n_b + n_o]
        a_refs = refs[n_in + n_b + n_o:n_in + n_b + n_o + n_a]
        c_refs = refs[n_in + n_b + n_o + n_a:]
        i = pl.program_id(0)
        blk = (n - 1 - i) if reverse else i
        if c_refs:
            @pl.when(i == 0)
            def _():
                c_refs[0][...] = jnp.zeros_like(c_refs[0])
        args = (list(in_refs) if as_refs else [r[...] for r in in_refs], [r[...] for r in b_refs])
        o_vals, a_vals = fn(blk * tm, *args, *c_refs)
        for r, v in zip(o_refs, o_vals):
            r[...] = v.astype(r.dtype)
        if n_a:
            @pl.when(i == 0)
            def _():
                for r in a_refs:
                    r[...] = jnp.zeros_like(r)
            for r, v in zip(a_refs, a_vals):
                r[...] += v

    def row_map(i):
        return ((n - 1 - i) if reverse else i, 0)

    in_specs = [pl.BlockSpec((tm, x.shape[1]), row_map) for x in ins]
    in_specs += [pl.BlockSpec(x.shape, lambda i, nd=x.ndim: (0,) * nd) for x in bcast]
    out_specs = [pl.BlockSpec((tm, c), row_map) for c, _ in outs]
    out_specs += [pl.BlockSpec(s, lambda i: (0, 0)) for s in accs]
    out_shape = [jax.ShapeDtypeStruct((rows, c), d) for c, d in outs]
    out_shape += [jax.ShapeDtypeStruct(s, F32) for s in accs]
    res = pl.pallas_call(
        body, name=name, grid=(n,), in_specs=in_specs, out_specs=out_specs, out_shape=out_shape,
        scratch_shapes=[pltpu.VMEM(carry, F32)] if carry else [],
        compiler_params=_cparams(("arbitrary",)),
    )(*ins, *bcast)
    return res[:n_o], res[n_o:]


def _row_ids(row0, tm):
    return row0 + _iota((tm, 1), 0)


def _rms_fwd(x, gain, name):
    def fn(row0, ins, bc):
        (xv,), (g,) = ins, bc
        r = lax.rsqrt(jnp.mean(xv * xv, axis=-1, keepdims=True) + EPS)
        return [xv * r * g], []
    return _rowwise(fn, [x], [gain], [(D, BF16)], [], name=name)[0][0]


def _rms_bwd_math(xv, dy, g):
    r = lax.rsqrt(jnp.mean(xv * xv, axis=-1, keepdims=True) + EPS)
    xh = xv * r
    dxh = dy * g
    dx = r * (dxh - xh * jnp.mean(dxh * xh, axis=-1, keepdims=True))
    return dx, jnp.sum(dy * xh, axis=0, keepdims=True)


def _loss_bwd(h, tgt, gain):
    tm = PAD
    n = h.shape[0] // tm

    def body(x_ref, t_ref, g_ref, dx_ref, loss_ref, dg_ref):
        i = pl.program_id(0)

        @pl.when(i == 0)
        def _():
            loss_ref[...] = jnp.zeros_like(loss_ref)
            dg_ref[...] = jnp.zeros_like(dg_ref)

        xv, g = x_ref[...], g_ref[...]
        r = lax.rsqrt(jnp.mean(xv * xv, axis=-1, keepdims=True) + EPS)
        xh = xv * r
        err = jnp.where(i >= 1, xh * g - t_ref[...], 0.0)
        per_row = jnp.mean(err * err, axis=-1, keepdims=True)
        loss_ref[...] += jnp.broadcast_to(0.5 * jnp.sum(per_row, axis=0, keepdims=True), (1, LANES))
        dy = err * (1.0 / D)
        dxh = dy * g
        dx_ref[...] = r * (dxh - xh * jnp.mean(dxh * xh, axis=-1, keepdims=True))
        dg_ref[...] += jnp.sum(dy * xh, axis=0, keepdims=True)

    dh, loss, dgain = pl.pallas_call(
        body, name="loss_bwd", grid=(n,),
        in_specs=[pl.BlockSpec((tm, D), lambda i: (i, 0)), pl.BlockSpec((tm, D), lambda i: (jnp.maximum(i - 1, 0), 0)),
                  pl.BlockSpec((1, D), lambda i: (0, 0))],
        out_specs=[pl.BlockSpec((tm, D), lambda i: (i, 0)), pl.BlockSpec((1, LANES), lambda i: (0, 0)),
                   pl.BlockSpec((1, D), lambda i: (0, 0))],
        out_shape=[jax.ShapeDtypeStruct(h.shape, F32), jax.ShapeDtypeStruct((1, LANES), F32),
                   jax.ShapeDtypeStruct((1, D), F32)],
        compiler_params=_cparams(("arbitrary",)),
    )(h, tgt, gain)
    return loss[0, 0], dh, dgain


FFN_TILES = dict(rows=(320, 256, 128), cols=(1408, 1024, 512, 256, 128))


def _ffn_in(hn, wg, wu, name):
    m, kdim = hn.shape
    n = wg.shape[1]
    tm, tn = _tile(m, FFN_TILES["rows"]), _tile(n, FFN_TILES["cols"])

    def body(a_ref, wg_ref, wu_ref, g_ref, u_ref, act_ref):
        a = a_ref[...]
        g = jnp.dot(a, wg_ref[...], preferred_element_type=F32)
        u = jnp.dot(a, wu_ref[...], preferred_element_type=F32)
        g_ref[...] = g.astype(g_ref.dtype)
        u_ref[...] = u.astype(u_ref.dtype)
        act_ref[...] = (g * _sigmoid(g) * u).astype(act_ref.dtype)

    wspec = pl.BlockSpec((kdim, tn), lambda j, i: (0, j))
    ospec = pl.BlockSpec((tm, tn), lambda j, i: (i, j))
    return pl.pallas_call(
        body, name=name, grid=(n // tn, m // tm),
        in_specs=[pl.BlockSpec((tm, kdim), lambda j, i: (i, 0)), wspec, wspec], out_specs=[ospec] * 3,
        out_shape=[jax.ShapeDtypeStruct((m, n), BF16)] * 3,
        compiler_params=_cparams(("parallel", "parallel")),
    )(hn, wg, wu)


def _dhn_norm(dys, ws, h, dh_up, gain, name):
    m = h.shape[0]
    tm = _tile(m, FFN_TILES["rows"])
    n = len(dys)

    def body(*refs):
        dy_refs, w_refs = refs[:n], refs[n:2 * n]
        h_ref, up_ref, g_ref, dh_ref, dgain_ref = refs[2 * n:]
        i = pl.program_id(0)

        @pl.when(i == 0)
        def _():
            dgain_ref[...] = jnp.zeros_like(dgain_ref)

        dy = sum(lax.dot_general(a[...], w[...], NT, preferred_element_type=F32) for a, w in zip(dy_refs, w_refs))
        dx, dgain = _rms_bwd_math(h_ref[...], dy, g_ref[...])
        keep = _row_ids(i * tm, tm) >= ROW0
        dh_ref[...] = jnp.where(keep, up_ref[...] + dx, 0.0)
        dgain_ref[...] += dgain

    rows = lambda c: pl.BlockSpec((tm, c), lambda i: (i, 0))
    whole = lambda a: pl.BlockSpec(a.shape, lambda i: (0, 0), pipeline_mode=pl.Buffered(1))
    return pl.pallas_call(
        body, name=name, grid=(m // tm,),
        in_specs=[rows(a.shape[1]) for a in dys] + [whole(w) for w in ws]
        + [rows(D), rows(D), pl.BlockSpec((1, D), lambda i: (0, 0))],
        out_specs=[rows(D), pl.BlockSpec((1, D), lambda i: (0, 0))],
        out_shape=[jax.ShapeDtypeStruct((m, D), F32), jax.ShapeDtypeStruct((1, D), F32)],
        compiler_params=_cparams(("arbitrary",)),
    )(*dys, *ws, h, dh_up, gain)


def _ffn_dact(dh, wo, g, u, name):
    m, kdim = dh.shape
    n = wo.shape[0]
    tm, tn = _tile(m, FFN_TILES["rows"]), _tile(n, FFN_TILES["cols"])

    def body(a_ref, w_ref, g_ref, u_ref, dg_ref, du_ref):
        da = lax.dot_general(a_ref[...].astype(BF16), w_ref[...], NT, preferred_element_type=F32)
        gv, uv = g_ref[...].astype(F32), u_ref[...].astype(F32)
        s = _sigmoid(gv)
        dg_ref[...] = (da * uv * (s * (1.0 + gv * (1.0 - s)))).astype(dg_ref.dtype)
        du_ref[...] = (da * gv * s).astype(du_ref.dtype)

    ospec = pl.BlockSpec((tm, tn), lambda j, i: (i, j))
    return pl.pallas_call(
        body, name=name, grid=(n // tn, m // tm),
        in_specs=[pl.BlockSpec((tm, kdim), lambda j, i: (i, 0)), pl.BlockSpec((tn, kdim), lambda j, i: (j, 0)),
                  ospec, ospec],
        out_specs=[ospec] * 2, out_shape=[jax.ShapeDtypeStruct((m, n), BF16)] * 2,
        compiler_params=_cparams(("parallel", "parallel")),
    )(dh, wo, g, u)


def _adamw_math(wv, gv, mv, vv):
    mn = ADAM_B1 * mv + (1.0 - ADAM_B1) * gv
    vn = ADAM_B2 * vv + (1.0 - ADAM_B2) * (gv * gv)
    m_hat = mn / (1.0 - ADAM_B1 ** ADAM_STEP)
    v_hat = vn / (1.0 - ADAM_B2 ** ADAM_STEP)
    return -ADAM_LR * (m_hat / (jnp.sqrt(v_hat) + ADAM_EPS) + ADAM_WD * wv), mn, vn


def _adamw(w, g, m, v, name):
    def fn(row0, ins, bc):
        return list(_adamw_math(*ins)), []
    c = w.shape[1]
    return _rowwise(fn, [w, g, m, v], [], [(c, F32)] * 3, [], name=name)[0]


def _adamw_halves(w, mine, theirs, m, v, name):
    rows, cols = w.shape
    half = rows // 2
    tm = _tile(half, cap=(10 * 1024 * 1024) // (9 * 4 * cols))
    nb = half // tm

    def body(c_ref, w_ref, g1_ref, g2_ref, m_ref, v_ref, g_out, d_out, m_out, v_out):
        own = (pl.program_id(0) // nb) == c_ref[0]
        g = jnp.where(own, g1_ref[...], g2_ref[...])
        delta, mn, vn = _adamw_math(w_ref[...], g, m_ref[...], v_ref[...])
        g_out[...] = g
        d_out[...] = delta
        m_out[...] = mn
        v_out[...] = vn

    full = pl.BlockSpec((tm, cols), lambda i, c: (i, 0))
    part = pl.BlockSpec((tm, cols), lambda i, c: (lax.rem(i, nb), 0))
    return pl.pallas_call(
        body, name=name,
        grid_spec=pltpu.PrefetchScalarGridSpec(num_scalar_prefetch=1, grid=(2 * nb,),
                                               in_specs=[full, part, part, full, full], out_specs=[full] * 4),
        out_shape=[jax.ShapeDtypeStruct((rows, cols), F32)] * 4,
        compiler_params=_cparams(("parallel",)),
    )(_mesh_scalar(lax.axis_index("c")), w, mine, theirs, m, v)


def _head_sum(x, gmat):
    hi = x.astype(BF16)
    lo = (x - hi.astype(F32)).astype(BF16)
    w = gmat.shape[0]
    return jnp.concatenate(
        [jnp.dot(hi[:, b:b + w], gmat, preferred_element_type=F32) + jnp.dot(lo[:, b:b + w], gmat,
                                                                             preferred_element_type=F32)
         for b in range(0, x.shape[1], w)], axis=1)


def _split3(x):
    hi = x.astype(BF16).astype(F32)
    r = x - hi
    mid = r.astype(BF16).astype(F32)
    return hi, mid, r - mid


def _extra_base(hh):
    return FOX_DH * (1 - hh)


def _data_mask(hh):
    lane = _iota((1, LANES), 1)
    return (lane >= FOX_DH * hh) & (lane < FOX_DH * (hh + 1))


def _with_extras(data, hh, vals):
    lane = _iota((1, LANES), 1)
    x = jnp.zeros_like(data)
    for e, v in enumerate(vals):
        x = jnp.where(lane == _extra_base(hh) + e, v, x)
    return jnp.where(_data_mask(hh), data, x)


def _fox_pack_fwd(q_raw, k_raw, v, cq, ck, qg, kg, gmat):
    scale2 = FOX_DH ** -0.5 * LOG2E

    def fn(row0, refs, bc):
        q_ref, k_ref, v_ref, cq_ref, ck_ref = refs
        g_q, g_k, gm = bc
        qv, kv = q_ref[...], k_ref[...]
        qn = qv * lax.rsqrt(_head_sum(qv * qv, gm) * (1.0 / FOX_DH) + EPS) * (g_q * scale2)
        kn = kv * lax.rsqrt(_head_sum(kv * kv, gm) * (1.0 / FOX_DH) + EPS) * g_k
        qs, ks, vs = [], [], []
        for h in range(FOX_H):
            p, hh = divmod(h, 2)
            sl = slice(p * LANES, (p + 1) * LANES)
            cq3 = _split3(cq_ref[:, h:h + 1] * LOG2E)
            ck3 = _split3(ck_ref[:, h:h + 1] * (-LOG2E))
            qs.append(_with_extras(qn[:, sl], hh, [*cq3, 1.0, 1.0, 1.0]))
            ks.append(_with_extras(kn[:, sl], hh, [1.0, 1.0, 1.0, *ck3]))
            vs.append(_with_extras(v_ref[:, sl].astype(F32), hh, [1.0, 1.0]))
        return [jnp.concatenate(qs, axis=1), jnp.concatenate(ks, axis=1), jnp.concatenate(vs, axis=1)], []

    w = FOX_H * LANES
    return _rowwise(fn, [q_raw, k_raw, v, cq, ck], [qg, kg, gmat], [(w, BF16)] * 3, [], name="fox_pack_fwd",
                    as_refs=True)[0]


def _fox_pack_bias(qp, cq, lse2):
    def fn(row0, refs, bc):
        q_ref, cq_ref, lse_ref = refs
        lane = _iota((1, LANES), 1)
        outs = []
        for h in range(FOX_H):
            blk = q_ref[:, h * LANES:(h + 1) * LANES].astype(F32)
            for e, part in enumerate(_split3(cq_ref[:, h:h + 1] * LOG2E - lse_ref[:, h:h + 1])):
                blk = jnp.where(lane == _extra_base(h % 2) + e, part, blk)
            outs.append(blk)
        return [jnp.concatenate(outs, axis=1)], []
    return _rowwise(fn, [qp, cq, lse2], [], [(FOX_H * LANES, BF16)], [], name="fox_pack_bias", as_refs=True)[0][0]


def _fox_pack_bwd(dog, o, gate):
    def fn(row0, refs, bc):
        d_ref, o_ref, g_ref = refs
        dos, dgs = [], []
        for p in range(FOX_H // 2):
            sl = slice(p * LANES, (p + 1) * LANES)
            dv, ov, gv = (r[:, sl].astype(F32) for r in (d_ref, o_ref, g_ref))
            s = _sigmoid(gv)
            do = dv * s
            dgs.append(dv * ov * s * (1.0 - s))
            od = ov * do
            for hh in range(2):
                delta = jnp.sum(jnp.where(_data_mask(hh), od, 0.0), axis=-1, keepdims=True)
                hi = delta.astype(BF16).astype(F32)
                dos.append(_with_extras(do, hh, [-hi, hi - delta]))
        return [jnp.concatenate(dos, axis=1), jnp.concatenate(dgs, axis=1)], []
    return _rowwise(fn, [dog, o, gate], [], [(FOX_H * LANES, BF16), (D, BF16)], [], name="fox_pack_bwd",
                    as_refs=True)[0]


def _fox_unpack_bwd(q_raw, k_raw, dqp, dk, qg, kg, gmat):
    scale = FOX_DH ** -0.5

    def fn(row0, refs, bc):
        q_ref, k_ref, dq_ref, dk_ref = refs
        g_q, g_k, gm = bc
        lane = _iota((1, LANES), 1)
        dqs = []
        dcq = jnp.zeros((q_ref.shape[0], LANES), F32)
        for p in range(FOX_H // 2):
            even = dq_ref[:, (2 * p) * LANES:(2 * p + 1) * LANES]
            odd = dq_ref[:, (2 * p + 1) * LANES:(2 * p + 2) * LANES]
            dqs.append(jnp.where(_data_mask(0), even, odd) * scale)
            for hh in range(2):
                col = (2 * p + hh) * LANES + _extra_base(hh)
                dcq = jnp.where(lane == 2 * p + hh, dq_ref[:, col:col + 1], dcq)
        outs, accs = [], []
        for xv, dy, g in ((q_ref[...], jnp.concatenate(dqs, axis=1), g_q), (k_ref[...], dk_ref[...] * (1.0 / LOG2E), g_k)):
            r = lax.rsqrt(_head_sum(xv * xv, gm) * (1.0 / FOX_DH) + EPS)
            xh = xv * r
            dxh = dy * g
            outs.append(r * (dxh - xh * (_head_sum(dxh * xh, gm) * (1.0 / FOX_DH))))
            accs.append(jnp.sum(dy * xh, axis=0, keepdims=True))
        return outs + [dcq], accs
    return _rowwise(fn, [q_raw, k_raw, dqp, dk], [qg, kg, gmat], [(D, BF16), (D, BF16), (LANES, F32)],
                    [(1, D), (1, D)], name="fox_unpack_bwd", as_refs=True)


def _fox_cumsum_fwd(flog, bf):
    def fn(row0, ins, bc, carry):
        (f,), (b,) = ins, bc
        tm = f.shape[0]
        keep = _row_ids(row0, tm) >= ROW0
        lf = jnp.where(keep, _log_sigmoid(f + b), 0.0)
        tri = (_iota((tm, tm), 0) >= _iota((tm, tm), 1)).astype(F32)
        c = jnp.dot(tri, lf, precision=HIGHEST, preferred_element_type=F32) + carry[...]
        carry[...] = carry[...] + jnp.sum(lf, axis=0, keepdims=True)
        return [c, jnp.where(keep, c, BIG)], []
    return _rowwise(fn, [flog], [bf], [(LANES, F32), (LANES, F32)], [], name="fox_cumsum_fwd",
                    carry=(1, LANES))[0]


def _fox_cumsum_bwd(dc_q, dc_k, flog, bf):
    def fn(row0, ins, bc, carry):
        (dq, dk, f), (b,) = ins, bc
        d = dq + dk
        tm = f.shape[0]
        keep = _row_ids(row0, tm) >= ROW0
        triu = (_iota((tm, tm), 0) <= _iota((tm, tm), 1)).astype(F32)
        dlf = jnp.dot(triu, d, precision=HIGHEST, preferred_element_type=F32) + carry[...]
        carry[...] = carry[...] + jnp.sum(d, axis=0, keepdims=True)
        dfl = jnp.where(keep, dlf * _sigmoid(-(f + b)), 0.0)
        return [dfl], [jnp.sum(dfl, axis=0, keepdims=True)]
    (dflog,), (dbf,) = _rowwise(fn, [dc_q, dc_k, flog], [bf], [(LANES, F32)], [(1, LANES)], name="fox_cumsum_bwd",
                                reverse=True, carry=(1, LANES))
    return dflog, dbf


def _causal_steps(n, key_major):
    if key_major:
        pairs = [(i, j) for j in range(n) for i in range(j, n)]
    else:
        pairs = [(i, j) for i in range(n) for j in range(i + 1)]
    return (jnp.asarray(np.array([p[0] for p in pairs], np.int32)),
            jnp.asarray(np.array([p[1] for p in pairs], np.int32)))


def _fox_attn_fwd(qp, kp, vp, gate, shards):
    L = qp.shape[0]
    t = _tile(L, ATTN_TILES)
    n = L // t
    hps = FOX_HPS_FWD
    P = FOX_H // hps
    it, jt = _causal_steps(n, False)
    n_steps = it.shape[0]
    ns = len(shards)

    def body(it_ref, jt_ref, q_ref, k_ref, v_ref, g_ref, *rest):
        sh_in, (o_ref, og_ref, lse_ref), sh_out = rest[:ns], rest[ns:ns + 3], rest[ns + 3:2 * ns + 3]
        m_sc, acc, ssem, rsem, lsem = rest[2 * ns + 3:]
        step = pl.program_id(1)
        i, j = it_ref[step], jt_ref[step]
        first = (pl.program_id(0) == 0) & (step == 0)
        last = (pl.program_id(0) == P - 1) & (step == n_steps - 1)

        @pl.when(first)
        def _():
            _gather_start(sh_in, sh_out, ssem, rsem, lsem)

        @pl.when(j == 0)
        def _():
            m_sc[...] = jnp.full_like(m_sc, -3.0e38)
            acc[...] = jnp.zeros_like(acc)

        def update(masked):
            def head(hh):
                sl = slice(hh * LANES, (hh + 1) * LANES)
                s2 = lax.dot_general(k_ref[:, sl], q_ref[:, sl], NT, preferred_element_type=F32)
                if masked:
                    s2 = jnp.where(_iota((t, t), 0) <= _iota((t, t), 1), s2, -jnp.inf)
                yield
                m_old = m_sc[hh]
                m_new = jnp.maximum(m_old, jnp.max(s2, axis=0, keepdims=True))
                p = jnp.exp2(s2 - m_new).astype(BF16)
                yield
                acc[hh] = jnp.exp2(m_old - m_new) * acc[hh] + lax.dot_general(v_ref[:, sl], p, TN,
                                                                              preferred_element_type=F32)
                m_sc[hh] = m_new

            _interleave((head(hh) for hh in range(hps)), skew=1)

        @pl.when(j < i)
        def _():
            update(False)

        @pl.when(j == i)
        def _():
            update(True)
            outs = []
            for hh in range(hps):
                base = _extra_base(hh % 2)
                l = acc[hh, base:base + 1, :]
                outs.append((acc[hh] / l).T)
                lse_ref[0, hh:hh + 1, :] = m_sc[hh] + jnp.log2(l)
            o = jnp.concatenate([jnp.where(_data_mask(0), outs[a], outs[a + 1]) for a in range(0, hps, 2)], axis=1)
            o_ref[...] = o.astype(o_ref.dtype)
            og_ref[...] = (o * _sigmoid(g_ref[...].astype(F32))).astype(og_ref.dtype)

        @pl.when(last)
        def _():
            _gather_wait(sh_in, sh_out, ssem, rsem, lsem)

    qspec = pl.BlockSpec((t, hps * LANES), lambda p, s, it, jt: (it[s], p))
    kspec = pl.BlockSpec((t, hps * LANES), lambda p, s, it, jt: (jt[s], p))
    ospec = pl.BlockSpec((t, hps * FOX_DH), lambda p, s, it, jt: (it[s], p))
    lspec = pl.BlockSpec((1, hps, t), lambda p, s, it, jt: (p, 0, it[s]))
    res = pl.pallas_call(
        body, name="fox_attn_fwd",
        grid_spec=pltpu.PrefetchScalarGridSpec(
            num_scalar_prefetch=2, grid=(P, n_steps),
            in_specs=[qspec, kspec, kspec, ospec] + [ANY] * ns, out_specs=[ospec, ospec, lspec] + [ANY] * ns,
            scratch_shapes=[pltpu.VMEM((hps, 1, t), F32), pltpu.VMEM((hps, LANES, t), F32)] + _gather_sems(ns)),
        out_shape=[jax.ShapeDtypeStruct((L, D), BF16), jax.ShapeDtypeStruct((L, D), BF16),
                   jax.ShapeDtypeStruct((P, hps, L), F32)]
        + [jax.ShapeDtypeStruct((4,) + a.shape, a.dtype) for a in shards],
        compiler_params=_cparams(("arbitrary", "arbitrary")),
    )(it, jt, qp, kp, vp, gate, *shards)
    return res[0], res[1], res[2], res[3:]


def _fox_attn_bwd(qb, kp, vp, dop, slabs):
    L = qb.shape[0]
    t = _tile(L, ATTN_TILES)
    n = L // t
    hps = FOX_HPS_BWD
    P = FOX_H // hps
    it, jt = _causal_steps(n, True)
    n_steps = it.shape[0]
    ns = len(slabs)

    def body(it_ref, jt_ref, q_ref, k_ref, v_ref, do_ref, *rest):
        sl_in, (dq_ref, dk_ref, dv_ref, dck_ref), sl_out = rest[:ns], rest[ns:ns + 4], rest[ns + 4:2 * ns + 4]
        dk_acc, dv_acc, ssem, rsem = rest[2 * ns + 4:]
        step = pl.program_id(1)
        i, j = it_ref[step], jt_ref[step]

        @pl.when((pl.program_id(0) == 0) & (step == 0))
        def _():
            for cp in _scatter_copies(sl_in, sl_out, ssem, rsem):
                cp.start()

        @pl.when(step == 0)
        def _():
            dq_ref[...] = jnp.zeros_like(dq_ref)

        @pl.when(i == j)
        def _():
            dk_acc[...] = jnp.zeros_like(dk_acc)
            dv_acc[...] = jnp.zeros_like(dv_acc)

        def update(masked):
            rows = pl.ds(pl.multiple_of(i * t, LANES), t)
            for hh in range(hps):
                sl = slice(hh * LANES, (hh + 1) * LANES)
                q, k, dov = q_ref[:, sl], k_ref[:, sl], do_ref[:, sl]
                s2 = lax.dot_general(k, q, NT, preferred_element_type=F32)
                if masked:
                    s2 = jnp.where(_iota((t, t), 0) <= _iota((t, t), 1), s2, -jnp.inf)
                p = jnp.exp2(s2)
                ds = (p * lax.dot_general(v_ref[:, sl], dov, NT, preferred_element_type=F32)).astype(BF16)
                dv_acc[hh] += jnp.dot(p.astype(BF16), dov, preferred_element_type=F32)
                dk_acc[hh] += jnp.dot(ds, q, preferred_element_type=F32)
                dq_ref[rows, sl] += lax.dot_general(ds, k, TN, preferred_element_type=F32)

        @pl.when(i > j)
        def _():
            update(False)

        @pl.when(i == j)
        def _():
            update(True)

        @pl.when(i == n - 1)
        def _():
            pairs = range(0, hps, 2)
            dk_ref[...] = jnp.concatenate([jnp.where(_data_mask(0), dk_acc[a], dk_acc[a + 1]) for a in pairs], axis=1)
            dv_ref[...] = jnp.concatenate([jnp.where(_data_mask(0), dv_acc[a], dv_acc[a + 1]) for a in pairs],
                                          axis=1).astype(dv_ref.dtype)
            lane = _iota((1, hps), 1)
            col_sums = jnp.zeros((t, hps), F32)
            for hh in range(hps):
                base = _extra_base(hh % 2) + 3
                col_sums = jnp.where(lane == hh, dk_acc[hh, :, base:base + 1], col_sums)
            dck_ref[0] = -col_sums

        @pl.when((pl.program_id(0) == P - 1) & (step == n_steps - 1))
        def _():
            for cp in _scatter_copies(sl_in, sl_out, ssem, rsem):
                cp.wait()

    qspec = pl.BlockSpec((t, hps * LANES), lambda p, s, it, jt: (it[s], p))
    kspec = pl.BlockSpec((t, hps * LANES), lambda p, s, it, jt: (jt[s], p))
    ospec = pl.BlockSpec((t, hps * FOX_DH), lambda p, s, it, jt: (jt[s], p))
    slab = pl.BlockSpec((L, hps * LANES), lambda p, s, it, jt: (0, p), pipeline_mode=pl.Buffered(1))
    res = pl.pallas_call(
        body, name="fox_attn_bwd",
        grid_spec=pltpu.PrefetchScalarGridSpec(
            num_scalar_prefetch=2, grid=(P, n_steps),
            in_specs=[qspec, kspec, kspec, qspec] + [ANY] * ns,
            out_specs=[slab, ospec, ospec,
                       pl.BlockSpec((1, t, hps), lambda p, s, it, jt: (p, jt[s], 0))] + [ANY] * ns,
            scratch_shapes=[pltpu.VMEM((hps, t, LANES), F32), pltpu.VMEM((hps, t, LANES), F32),
                            pltpu.SemaphoreType.DMA((ns, 3)), pltpu.SemaphoreType.DMA((ns, 3))]),
        out_shape=[jax.ShapeDtypeStruct((L, FOX_H * LANES), F32), jax.ShapeDtypeStruct((L, D), F32),
                   jax.ShapeDtypeStruct((L, D), BF16), jax.ShapeDtypeStruct((P, L, hps), F32)]
        + [jax.ShapeDtypeStruct((3,) + a.shape[1:], a.dtype) for a in slabs],
        compiler_params=_cparams(("arbitrary", "arbitrary")),
    )(it, jt, qb, kp, vp, dop, *slabs)
    return res[0], res[1], res[2], res[3], res[4:]


def _hgrn_consts():
    C = HG_C
    r = np.arange(C)[:, None]
    j = np.arange(C)[None, :]
    mats = [j <= r, j > r]
    masks = []
    n = C
    while n >= 2:
        half = n // 2
        mid = (r // n) * n + half - 1
        second = (r % n) >= half
        mats.append(np.where(second, (j > mid) & (j <= r), (j > r) & (j <= mid)))
        masks.append(((r // n) == (j // n)) & ((r % n) >= half) & ((j % n) < half))
        n //= 2
    return (jnp.asarray(np.concatenate(mats, 0).astype(np.float32), BF16),
            jnp.asarray(np.stack(masks).astype(np.float32), F32))


def _hg_pre(hq, hz, h0, h1):
    mx = jnp.maximum(h0, h1)
    e0, e1 = jnp.exp(h0 - mx), jnp.exp(h1 - mx)
    lb = e1 / (e0 + e1)
    sq = _sigmoid(hq)
    sz = _sigmoid(hz)
    snz = 1.0 - sz
    k = (1.0 - lb) * snz
    g = jnp.maximum(jnp.log(lb + (1.0 - lb) * sz), -BIG)
    return lb, hq * sq, sq, k, sz, snz, g


def _hg_decays(g, rmat):
    hi = g.astype(BF16)
    lo = (g - hi.astype(F32)).astype(BF16)
    d = jnp.dot(rmat, jnp.concatenate([hi, lo], axis=1), preferred_element_type=F32)
    return jnp.exp(d[:, :HG_D] + d[:, HG_D:])


def _interleave(programs, skew=0):
    progs = list(programs)
    done = [False] * len(progs)
    tick = 0
    while not all(done):
        for n, g in enumerate(progs):
            if not done[n] and tick >= n * skew:
                try:
                    next(g)
                except StopIteration:
                    done[n] = True
        tick += 1


def _hg_intra_levels(q, k, fall, masks):
    C = HG_C
    eye = _iota((C, C), 0) == _iota((C, C), 1)
    a = jnp.where(eye, jnp.sum(q * k, axis=-1, keepdims=True), 0.0)
    for l in range(HG_LEV):
        f = fall[(2 + l) * C:(3 + l) * C]
        a = a + masks[l] * lax.dot_general((q * f).astype(BF16), (k * f).astype(BF16), NT,
                                           preferred_element_type=F32)
        yield a


def _hgrn_specs(n_chunks, reverse):
    C = HG_C
    w = HG_HPS * HG_D

    def col(first_head):
        off = first_head // HG_HPS
        if reverse:
            return pl.BlockSpec((C, w), lambda h, c: (n_chunks - 1 - c, off + h))
        return pl.BlockSpec((C, w), lambda h, c: (c, off + h))

    st = pl.BlockSpec((HG_HPS, 1, HG_D, HG_D),
                      (lambda h, c: (h, n_chunks - 1 - c, 0, 0)) if reverse else (lambda h, c: (h, c, 0, 0)))
    consts = [pl.BlockSpec((2, w), lambda h, c: (0, h)), pl.BlockSpec((1, HG_D), lambda h, c: (0, 0)),
              pl.BlockSpec(((2 + HG_LEV) * C, C), lambda h, c: (0, 0)),
              pl.BlockSpec((HG_LEV, C, C), lambda h, c: (0, 0, 0))]
    return col, st, consts


def _hgrn_fwd(proj, hlb, gg, rmat, masks):
    L = proj.shape[0]
    C = HG_C
    nc = L // C
    col, st, consts = _hgrn_specs(nc, False)

    def body(hq_ref, hz_ref, hi_ref, hg_ref, hlb_ref, gg_ref, r_ref, m_ref, og_ref, st_ref, state):
        c = pl.program_id(1)

        @pl.when(c == 0)
        def _():
            state[...] = jnp.zeros_like(state)

        def head(hh):
            sl = slice(hh * HG_D, (hh + 1) * HG_D)
            v, hg = hi_ref[:, sl], hg_ref[:, sl]
            _, q, _, k, _, _, g = _hg_pre(hq_ref[:, sl], hz_ref[:, sl], hlb_ref[0:1, sl], hlb_ref[1:2, sl])
            yield
            fall = _hg_decays(g, r_ref[...])
            fb, fe = fall[0:C], fall[C:2 * C]
            st0 = state[hh]
            st_ref[hh, 0] = st0
            yield
            for a in _hg_intra_levels(q, k, fall, m_ref[...]):
                yield
            vb = v.astype(BF16)
            o = jnp.dot(a.astype(BF16), vb, preferred_element_type=F32)
            o = o + lax.dot_general((q * fb).astype(BF16), st0.astype(BF16), NT, preferred_element_type=F32)
            yield
            ebc = jnp.exp(jnp.sum(g, axis=0, keepdims=True))
            state[hh] = st0 * ebc + lax.dot_general(vb, (k * fe).astype(BF16), TN, preferred_element_type=F32)
            r = lax.rsqrt(jnp.mean(o * o, axis=-1, keepdims=True) + EPS)
            og_ref[:, sl] = (o * r * gg_ref[...] * (hg * _sigmoid(hg))).astype(og_ref.dtype)

        _interleave((head(hh) for hh in range(HG_HPS)), skew=HG_SKEW)

    return pl.pallas_call(
        body, name="hgrn_fwd", grid=(HG_H // HG_HPS, nc),
        in_specs=[col(0), col(HG_H), col(2 * HG_H), col(3 * HG_H)] + consts,
        out_specs=[col(0), st],
        out_shape=[jax.ShapeDtypeStruct((L, D), BF16), jax.ShapeDtypeStruct((HG_H, nc, HG_D, HG_D), F32)],
        scratch_shapes=[pltpu.VMEM((HG_HPS, HG_D, HG_D), F32)],
        compiler_params=_cparams(("parallel", "arbitrary")),
    )(proj, proj, proj, proj, hlb, gg, rmat, masks)


def _hgrn_bwd(proj, dog, states, hlb, gg, rmat, masks):
    L = proj.shape[0]
    C = HG_C
    nc = L // C
    col, st, consts = _hgrn_specs(nc, True)

    def body(hq_ref, hz_ref, hi_ref, hg_ref, do_ref, hlb_ref, gg_ref, r_ref, m_ref, st_ref,
             dq_ref, dz_ref, di_ref, dg_ref, dlb_ref, dgg_ref, dstate):
        c = pl.program_id(1)

        @pl.when(c == 0)
        def _():
            dstate[...] = jnp.zeros_like(dstate)
            dlb_ref[...] = jnp.zeros_like(dlb_ref)
            dgg_ref[...] = jnp.zeros_like(dgg_ref)

        _interleave([bwd_head(c, hh, slice(hh * HG_D, (hh + 1) * HG_D), hq_ref, hz_ref, hi_ref, hg_ref, do_ref, hlb_ref,
                              gg_ref, r_ref, m_ref, st_ref, dq_ref, dz_ref, di_ref, dg_ref, dlb_ref, dgg_ref, dstate)
                     for hh in range(HG_HPS)], skew=HG_SKEW)

    def bwd_head(c, hh, sl, hq_ref, hz_ref, hi_ref, hg_ref, do_ref, hlb_ref, gg_ref, r_ref, m_ref, st_ref,
                 dq_ref, dz_ref, di_ref, dg_ref, dlb_ref, dgg_ref, dstate):
        hq, hz, v, hg = hq_ref[:, sl], hz_ref[:, sl], hi_ref[:, sl], hg_ref[:, sl]
        dout = do_ref[:, sl].astype(F32)
        gain = gg_ref[...]
        masks_v = m_ref[...]
        lb, q, sq, k, sz, snz, g = _hg_pre(hq, hz, hlb_ref[0:1, sl], hlb_ref[1:2, sl])
        yield
        fall = _hg_decays(g, r_ref[...])
        fb, fe = fall[0:C], fall[C:2 * C]
        yield
        for a in _hg_intra_levels(q, k, fall, masks_v):
            yield
        st0 = st_ref[hh, 0]
        st0b = st0.astype(BF16)
        ebc = jnp.exp(jnp.sum(g, axis=0, keepdims=True))
        qb, ke, vb = (q * fb).astype(BF16), (k * fe).astype(BF16), v.astype(BF16)
        ab = a.astype(BF16)
        o = jnp.dot(ab, vb, preferred_element_type=F32) + lax.dot_general(qb, st0b, NT, preferred_element_type=F32)
        yield
        r = lax.rsqrt(jnp.mean(o * o, axis=-1, keepdims=True) + EPS)
        oh = o * r
        sg = _sigmoid(hg)
        d_on = dout * (hg * sg)
        dhg = dout * (oh * gain) * (sg * (1.0 + hg * (1.0 - sg)))
        dgg_ref[hh] += jnp.sum(d_on * oh, axis=0, keepdims=True)
        dxh = d_on * gain
        do = r * (dxh - oh * jnp.mean(dxh * oh, axis=-1, keepdims=True))
        dob = do.astype(BF16)
        yield
        dsp = dstate[hh]
        dspb = dsp.astype(BF16)
        causal = _iota((C, C), 0) >= _iota((C, C), 1)
        da = jnp.where(causal, lax.dot_general(dob, vb, NT, preferred_element_type=F32), 0.0)
        diag = jnp.sum(do * v, axis=-1, keepdims=True)
        dv = lax.dot_general(ab, dob, TN, preferred_element_type=F32)
        dv = dv + lax.dot_general(ke, dspb, NT, preferred_element_type=F32)
        xq = jnp.dot(dob, st0b, preferred_element_type=F32)
        xk = jnp.dot(vb, dspb, preferred_element_type=F32)
        dq = diag * k + fb * xq
        dk = diag * q + fe * xk
        ke_xk = ke.astype(F32) * xk
        db = qb.astype(F32) * xq - ke_xk
        yield
        for l in range(HG_LEV):
            f = fall[(2 + l) * C:(3 + l) * C]
            dal = (masks_v[l] * da).astype(BF16)
            ql, kl = (q * f).astype(BF16), (k * f).astype(BF16)
            xq = jnp.dot(dal, kl, preferred_element_type=F32)
            xk = lax.dot_general(dal, ql, TN, preferred_element_type=F32)
            dq = dq + f * xq
            dk = dk + f * xk
            db = db + ql.astype(F32) * xq - kl.astype(F32) * xk
            yield
        dstate[hh] = dsp * ebc + lax.dot_general(dob, qb, TN, preferred_element_type=F32)
        triu = (_iota((C, C), 0) <= _iota((C, C), 1)).astype(F32)
        dg = jnp.dot(triu, db, precision=HIGHEST, preferred_element_type=F32)
        dg = dg + jnp.sum(st0 * ebc * dsp, axis=0, keepdims=True) + jnp.sum(ke_xk, axis=0, keepdims=True)
        keep = _row_ids((nc - 1 - c) * C, C) >= ROW0
        dg = jnp.where(keep, dg, 0.0)
        dk = jnp.where(keep, dk, 0.0)
        f_gate = lb + (1.0 - lb) * sz
        dfdz = (1.0 - lb) * sz * snz
        dz_ref[:, sl] = (dg * dfdz / f_gate - dk * dfdz).astype(dz_ref.dtype)
        dlb_ref[:, sl] += jnp.sum(dg * snz / f_gate - dk * snz, axis=0, keepdims=True)
        dq_ref[:, sl] = jnp.where(keep, dq * (sq * (1.0 + hq * (1.0 - sq))), 0.0).astype(dq_ref.dtype)
        di_ref[:, sl] = jnp.where(keep, dv, 0.0).astype(di_ref.dtype)
        dg_ref[:, sl] = jnp.where(keep, dhg, 0.0).astype(dg_ref.dtype)

    w = HG_HPS * HG_D
    outs = pl.pallas_call(
        body, name="hgrn_bwd", grid=(HG_H // HG_HPS, nc),
        in_specs=[col(0), col(HG_H), col(2 * HG_H), col(3 * HG_H), col(0)] + consts + [st],
        out_specs=[col(0), col(0), col(0), col(0), pl.BlockSpec((1, w), lambda h, c: (0, h)),
                   pl.BlockSpec((HG_HPS, 1, HG_D), lambda h, c: (h, 0, 0))],
        out_shape=[jax.ShapeDtypeStruct((L, D), BF16)] * 4 + [jax.ShapeDtypeStruct((1, D), F32),
                                                              jax.ShapeDtypeStruct((HG_H, 1, HG_D), F32)],
        scratch_shapes=[pltpu.VMEM((HG_HPS, HG_D, HG_D), F32)],
        compiler_params=_cparams(("parallel", "arbitrary")),
    )(proj, proj, proj, proj, dog, hlb, gg, rmat, masks, states)
    return outs


def _ffn_fwd(h, norm_gain, wg, wu, wo, tag):
    hn = _rms_fwd(h, norm_gain, f"{tag}_norm")
    g, u, act = _ffn_in(hn, wg, wu, f"{tag}_in")
    h_out = _matmul(act, wo, add=h, name=f"{tag}_out")
    return h_out, (h, hn, g, u, act)


def _ffn_bwd(dh, saved, norm_gain, wg, wu, wo, tag):
    h, hn, g, u, act = saved
    dg, du = _ffn_dact(dh, wo, g, u, f"{tag}_dact")
    d_wo = _matmul(act, dh, ta=True, name=f"{tag}_dwo")
    d_wg = _matmul(hn, dg, ta=True, name=f"{tag}_dwg")
    d_wu = _matmul(hn, du, ta=True, name=f"{tag}_dwu")
    dh, d_gain = _dhn_norm([dg, du], [wg, wu], h, dh, norm_gain, f"{tag}_dhn_norm")
    return dh, d_gain, (d_wg, d_wu, d_wo)


def _local_step(h0, tgt, w, late_shards):
    L = h0.shape[0]
    gmat = jnp.asarray(np.kron(np.eye(MXU_N // FOX_DH), np.ones((FOX_DH, FOX_DH))).astype(np.float32), BF16)
    rmat, lmasks = _hgrn_consts()
    an, fn_ = w["attn_norm"], w["ffn_norm"]
    qg = jnp.tile(w["fox_q_norm"], (1, FOX_H))
    kg = jnp.tile(w["fox_k_norm"], (1, FOX_H))
    bf = jnp.pad(w["fox_b_f"], ((0, 0), (0, LANES - FOX_H)))
    fw = w["fox_w_in"]
    f_wq, f_wk, f_wv, f_wg = (fw[:, i * D:(i + 1) * D] for i in range(4))
    f_wf = jnp.pad(fw[:, 4 * D:], ((0, 0), (0, LANES - FOX_H)))

    hn0 = _rms_fwd(h0, an[0:1], "fox_norm")
    q_raw = _matmul(hn0, f_wq, name="fox_q")
    k_raw = _matmul(hn0, f_wk, name="fox_k")
    v = _matmul(hn0, f_wv, out_dtype=BF16, name="fox_v")
    gate = _matmul(hn0, f_wg, out_dtype=BF16, name="fox_gate")
    flog = _matmul(hn0, f_wf, name="fox_flog")
    cq, ck = _fox_cumsum_fwd(flog, bf)
    qp, kp, vp = _fox_pack_fwd(q_raw, k_raw, v, cq, ck, qg, kg, gmat)
    o, og, lse2, gathered = _fox_attn_fwd(qp, kp, vp, gate, [late_shards[n] for n in GATHER_LATE])
    late = dict(zip(GATHER_LATE, gathered))
    f_wo = late["fox_w_out"].reshape(D, D)
    h_wo = late["hgrn_w_out"].reshape(D, D)
    h_wi = jnp.concatenate(list(late["hgrn_w_in"]), axis=1)
    g_in, g_out = late["ffn_w_in"], late["ffn_w_out"]
    ffw = []
    for i in range(2):
        rows_in, rows_out = slice(i * D, (i + 1) * D), slice(i * FFN // 4, (i + 1) * FFN // 4)
        ffw.append((jnp.concatenate([g_in[0, rows_in], g_in[1, rows_in]], axis=1),
                    jnp.concatenate([g_in[2, rows_in], g_in[3, rows_in]], axis=1),
                    jnp.concatenate([g_out[j, rows_out] for j in range(4)], axis=0)))
    h1 = _matmul(og, f_wo, add=h0, name="fox_out")
    h2, ffn0 = _ffn_fwd(h1, fn_[0:1], *ffw[0], "ffn0")

    hn2 = _rms_fwd(h2, an[1:2], "hgrn_norm")
    proj = _matmul(hn2, h_wi, name="hgrn_in")
    og1, states = _hgrn_fwd(proj, w["hgrn_lower_bounds"], w["hgrn_g_norm"], rmat, lmasks)
    h3 = _matmul(og1, h_wo, add=h2, name="hgrn_out")
    h4, ffn1 = _ffn_fwd(h3, fn_[1:2], *ffw[1], "ffn1")

    loss, dh, d_final = _loss_bwd(h4, tgt, w["final_norm"])

    dh, d_fn1, d_ffn1 = _ffn_bwd(dh, ffn1, fn_[1:2], *ffw[1], "ffn1")
    dog1 = _matmul(dh, h_wo, tb=True, out_dtype=BF16, name="hgrn_dog")
    d_h_wo = _matmul(og1, dh, ta=True, name="hgrn_dwo")
    dpq, dpz, dpi, dpg, d_lb, d_gg = _hgrn_bwd(proj, dog1, states, w["hgrn_lower_bounds"], w["hgrn_g_norm"],
                                               rmat, lmasks)
    dproj = jnp.concatenate([dpq, dpz, dpi, dpg], axis=1)
    d_h_wi = _matmul(hn2, dproj, ta=True, name="hgrn_dwi")
    dh, d_an1 = _dhn_norm([dproj], [h_wi], h2, dh, an[1:2], "hgrn_dhn_norm")

    dh, d_fn0, d_ffn0 = _ffn_bwd(dh, ffn0, fn_[0:1], *ffw[0], "ffn0")
    n_in, n_out = 2 * FFN // 4, FFN // 4
    d_ffn = [d_ffn0, d_ffn1]
    late_grads = dict(
        hgrn_w_in=_to_shards("hgrn_w_in", d_h_wi[None]), hgrn_w_out=d_h_wo.reshape(4, D // 4, D),
        ffn_w_in=jnp.stack([jnp.concatenate([d[j // 2][:, (j % 2) * n_in:(j % 2 + 1) * n_in] for d in d_ffn], axis=0)
                            for j in range(4)]),
        ffn_w_out=jnp.stack([jnp.concatenate([d[2][j * n_out:(j + 1) * n_out] for d in d_ffn], axis=0)
                             for j in range(4)]))
    pair_late, send_late = _pair_sums([late_grads[n] for n in LATE_NAMES], "late")

    dog = _matmul(dh, f_wo, tb=True, out_dtype=BF16, name="fox_dog")
    d_f_wo = _matmul(og, dh, ta=True, name="fox_dwo")

    def by_head(a):
        return jnp.pad(a.transpose(1, 0, 2).reshape(L, FOX_H), ((0, 0), (0, LANES - FOX_H)))

    qb = _fox_pack_bias(qp, cq, by_head(lse2.transpose(0, 2, 1)))
    dop, dgate = _fox_pack_bwd(dog, o, gate)
    dqp, dk, dv, dck, recv_late = _fox_attn_bwd(qb, kp, vp, dop, send_late)
    (dq_raw, dk_raw, dc_q), (d_qg, d_kg) = _fox_unpack_bwd(q_raw, k_raw, dqp, dk, qg, kg, gmat)
    dflog, d_bf = _fox_cumsum_bwd(dc_q, by_head(dck), flog, bf)
    dproj0 = jnp.concatenate([dq_raw, dk_raw, dv, dgate, dflog.astype(BF16)], axis=1)
    f_wall = jnp.concatenate([f_wq, f_wk, f_wv, f_wg, f_wf], axis=1)
    d_f_wall = _matmul(hn0, dproj0, ta=True, name="fox_dwi")
    d_f_wi = d_f_wall[:, :4 * D + FOX_H]
    dh, d_an0 = _dhn_norm([dproj0], [f_wall], h0, dh, an[0:1], "fox_dhn_norm")

    fox_grads = dict(fox_w_in=d_f_wi[None], fox_w_out=d_f_wo[None])
    pair_fox, send_fox = _pair_sums([_to_shards(n, fox_grads[n]) for n in FOX_NAMES], "fox")
    recv_fox = _chip_scatter(send_fox, "fox")
    halves = _chip_sums(pair_fox, recv_fox, "fox") + _chip_sums(pair_late, recv_late, "late")
    theirs = _sibling_exchange(halves)
    big = {n: (m, t) for n, m, t in zip(FOX_NAMES + LATE_NAMES, halves, theirs)}
    small = dict(attn_norm=jnp.concatenate([d_an0, d_an1]), ffn_norm=jnp.concatenate([d_fn0, d_fn1]),
                 final_norm=d_final, lb_raw=d_lb, q_gain=d_qg, k_gain=d_kg, b_f=d_bf,
                 g_gain=d_gg.reshape(1, D))
    return loss, dh, big, small


def _me():
    return lax.axis_index("x"), lax.axis_index("y"), lax.axis_index("c")


def _flip(v, bit):
    return 1 - v if bit else v


def _chip_allgather(arrs):
    n = len(arrs)

    def body(*refs):
        _gather_start(refs[:n], refs[n:2 * n], *refs[2 * n:])
        _gather_wait(refs[:n], refs[n:2 * n], *refs[2 * n:])

    return pl.pallas_call(
        body, name="chip_allgather", in_specs=[ANY] * n, out_specs=[ANY] * n,
        out_shape=[jax.ShapeDtypeStruct((4,) + a.shape, a.dtype) for a in arrs],
        scratch_shapes=_gather_sems(n),
    )(*arrs)


def _chip_peers():
    x, y, c = _me()
    return [(1 - x, y, c), (x, 1 - y, c), (1 - x, 1 - y, c)]


def _gather_sems(n):
    return [pltpu.SemaphoreType.DMA((n, 3)), pltpu.SemaphoreType.DMA((n, 3)), pltpu.SemaphoreType.DMA((n,))]


def _gather_copies(ins, outs, ssem, rsem, lsem, with_recvs):
    x, y, _ = _me()
    local, sends, recvs = [], [], []
    for a in range(len(ins)):
        local.append(pltpu.make_async_copy(ins[a], outs[a].at[2 * x + y], lsem.at[a]))
        for k, peer in enumerate(_chip_peers()):
            sends.append(pltpu.make_async_remote_copy(ins[a], outs[a].at[2 * x + y], ssem.at[a, k], rsem.at[a, k],
                                                      device_id=peer, device_id_type=MESH))
            if with_recvs:
                recvs.append(pltpu.make_async_remote_copy(ins[a], outs[a].at[2 * peer[0] + peer[1]], ssem.at[a, k],
                                                          rsem.at[a, k], device_id=peer, device_id_type=MESH))
    return local, sends, recvs


def _gather_start(ins, outs, ssem, rsem, lsem):
    local, sends, _ = _gather_copies(ins, outs, ssem, rsem, lsem, False)
    for cp in local + sends:
        cp.start()


def _gather_wait(ins, outs, ssem, rsem, lsem):
    local, sends, recvs = _gather_copies(ins, outs, ssem, rsem, lsem, True)
    for cp in local:
        cp.wait()
    for cp in sends:
        cp.wait_send()
    for cp in recvs:
        cp.wait_recv()


def _scatter_copies(ins, outs, ssem, rsem):
    copies = []
    for a in range(len(ins)):
        for k, peer in enumerate(_chip_peers()):
            copies.append(pltpu.make_async_remote_copy(ins[a].at[2 * peer[0] + peer[1]], outs[a].at[k], ssem.at[a, k],
                                                       rsem.at[a, k], device_id=peer, device_id_type=MESH))
    return copies


def _device_allgather(arr):
    def body(in_ref, out_ref, ssem, rsem, lsem):
        x, y, c = _me()
        me = 4 * x + 2 * y + c
        peers = [(_flip(x, k & 4), _flip(y, k & 2), _flip(c, k & 1)) for k in range(1, 8)]
        local = pltpu.make_async_copy(in_ref, out_ref.at[me], lsem)
        local.start()
        sends = []
        for k, peer in enumerate(peers):
            cp = pltpu.make_async_remote_copy(in_ref, out_ref.at[me], ssem.at[k], rsem.at[k],
                                              device_id=peer, device_id_type=MESH)
            cp.start()
            sends.append(cp)
        local.wait()
        for cp in sends:
            cp.wait_send()
        for k, peer in enumerate(peers):
            pltpu.make_async_remote_copy(in_ref, out_ref.at[4 * peer[0] + 2 * peer[1] + peer[2]], ssem.at[k],
                                         rsem.at[k], device_id=peer, device_id_type=MESH).wait_recv()

    return pl.pallas_call(
        body, name="device_allgather", in_specs=[ANY], out_specs=ANY,
        out_shape=jax.ShapeDtypeStruct((8,) + arr.shape, arr.dtype),
        scratch_shapes=[pltpu.SemaphoreType.DMA((7,)), pltpu.SemaphoreType.DMA((7,)), pltpu.SemaphoreType.DMA],
    )(arr)


def _sibling_send_other_half(arrs, tag):
    n = len(arrs)

    def body(*refs):
        ins, outs = refs[:n], refs[n:2 * n]
        ssem, rsem = refs[2 * n:]
        x, y, c = _me()
        cps = []
        for a in range(n):
            half = ins[a].shape[1] // 2
            src = ins[a].at[:, pl.ds((1 - c) * half, half), :]
            cp = pltpu.make_async_remote_copy(src, outs[a], ssem.at[a], rsem.at[a],
                                              device_id=(x, y, 1 - c), device_id_type=MESH)
            cp.start()
            cps.append(cp)
        for cp in cps:
            cp.wait()

    return pl.pallas_call(
        body, name=f"grad_sibling_swap_{tag}", in_specs=[ANY] * n, out_specs=[ANY] * n,
        out_shape=[jax.ShapeDtypeStruct((4, a.shape[1] // 2, a.shape[2]), a.dtype) for a in arrs],
        scratch_shapes=[pltpu.SemaphoreType.DMA((n,)), pltpu.SemaphoreType.DMA((n,))],
    )(*arrs)


def _chip_scatter(arrs, tag):
    n = len(arrs)

    def body(*refs):
        cps = _scatter_copies(refs[:n], refs[n:2 * n], *refs[2 * n:])
        for cp in cps:
            cp.start()
        for cp in cps:
            cp.wait()

    return pl.pallas_call(
        body, name=f"grad_chip_scatter_{tag}", in_specs=[ANY] * n, out_specs=[ANY] * n,
        out_shape=[jax.ShapeDtypeStruct((3,) + a.shape[1:], a.dtype) for a in arrs],
        scratch_shapes=[pltpu.SemaphoreType.DMA((n, 3)), pltpu.SemaphoreType.DMA((n, 3))],
    )(*arrs)


def _sibling_exchange(arrs):
    n = len(arrs)

    def body(*refs):
        ins, outs = refs[:n], refs[n:2 * n]
        ssem, rsem = refs[2 * n:]
        x, y, c = _me()
        cps = [pltpu.make_async_remote_copy(ins[a], outs[a], ssem.at[a], rsem.at[a], device_id=(x, y, 1 - c),
                                            device_id_type=MESH) for a in range(n)]
        for cp in cps:
            cp.start()
        for cp in cps:
            cp.wait()

    return pl.pallas_call(
        body, name="grad_sibling_exchange", in_specs=[ANY] * n, out_specs=[ANY] * n,
        out_shape=[jax.ShapeDtypeStruct(a.shape, a.dtype) for a in arrs],
        scratch_shapes=[pltpu.SemaphoreType.DMA((n,)), pltpu.SemaphoreType.DMA((n,))],
    )(*arrs)


def _pair_sums(grads, tag):
    got = _sibling_send_other_half(grads, tag)
    res = [_pair_add(g, t, f"grad_pair_add_{tag}{i}") for i, (g, t) in enumerate(zip(grads, got))]
    return [r[0] for r in res], [r[1] for r in res]


def _mesh_scalar(v):
    return jnp.asarray(v, jnp.int32).reshape(1)


def _pair_add(g, t, name):
    _, rows, cols = g.shape
    half = rows // 2
    tm = _tile(half, cap=(2 * 1024 * 1024) // (4 * cols))

    def body(c_ref, g_ref, t_ref, o_ref, ob_ref):
        s = g_ref[0, 0] + t_ref[0]
        o_ref[0] = s
        ob_ref[0] = s.astype(ob_ref.dtype)

    spec = pl.BlockSpec((1, tm, cols), lambda j, i, c: (j, i, 0))
    return pl.pallas_call(
        body, name=name,
        grid_spec=pltpu.PrefetchScalarGridSpec(
            num_scalar_prefetch=1, grid=(4, half // tm),
            in_specs=[pl.BlockSpec((1, 1, tm, cols), lambda j, i, c: (j, c[0], i, 0)), spec],
            out_specs=[spec, spec]),
        out_shape=[jax.ShapeDtypeStruct(t.shape, F32), jax.ShapeDtypeStruct(t.shape, BF16)],
        compiler_params=_cparams(("parallel", "parallel")),
    )(_mesh_scalar(lax.axis_index("c")), g.reshape(4, 2, half, cols), t)


def _chip_sums(pair, recv, tag):
    x, y, _ = _me()
    out = []
    for n, (p, r) in enumerate(zip(pair, recv)):
        _, half, cols = p.shape
        tm = _tile(half, cap=(2 * 1024 * 1024) // (4 * cols))

        def body(j_ref, p_ref, r_ref, o_ref):
            o_ref[...] = p_ref[0] + r_ref[0].astype(F32) + r_ref[1].astype(F32) + r_ref[2].astype(F32)

        out.append(pl.pallas_call(
            body, name=f"grad_chip_add_{tag}{n}",
            grid_spec=pltpu.PrefetchScalarGridSpec(
                num_scalar_prefetch=1, grid=(half // tm,),
                in_specs=[pl.BlockSpec((1, tm, cols), lambda i, j: (j[0], i, 0)),
                          pl.BlockSpec((3, tm, cols), lambda i, j: (0, i, 0))],
                out_specs=pl.BlockSpec((tm, cols), lambda i, j: (i, 0))),
            out_shape=jax.ShapeDtypeStruct((half, cols), F32),
            compiler_params=_cparams(("parallel",)),
        )(_mesh_scalar(2 * x + y), p, r))
    return out


SMALL_ROWS = 32


def _small_finalize(gathered, hlb, fold64, fold128):
    def body(g_ref, hlb_ref, f64_ref, f128_ref, rows_ref, qk_ref, gg_ref, lb_ref):
        tot = g_ref[0]
        for d in range(1, 8):
            tot = tot + g_ref[d]
        rows_ref[...] = tot
        qk_ref[...] = jnp.dot(rows_ref[6:8, :], f64_ref[...], precision=HIGHEST, preferred_element_type=F32)
        gg_ref[...] = jnp.dot(rows_ref[9:10, :], f128_ref[...], precision=HIGHEST, preferred_element_type=F32)
        h0, h1 = hlb_ref[0:1, :], hlb_ref[1:2, :]
        mx = jnp.maximum(h0, h1)
        e0, e1 = jnp.exp(h0 - mx), jnp.exp(h1 - mx)
        lb = e1 / (e0 + e1)
        d1 = rows_ref[5:6, :] * lb * (1.0 - lb)
        lb_ref[...] = jnp.where(_iota((2, 1), 0) == 0, -d1, d1)

    return pl.pallas_call(
        body, name="small_finalize",
        out_shape=[jax.ShapeDtypeStruct((SMALL_ROWS, D), F32), jax.ShapeDtypeStruct((2, FOX_DH), F32),
                   jax.ShapeDtypeStruct((1, HG_D), F32), jax.ShapeDtypeStruct((2, D), F32)],
    )(gathered, hlb, fold64, fold128)


FOX_NAMES = ("fox_w_in", "fox_w_out")
LATE_NAMES = ("hgrn_w_in", "hgrn_w_out", "ffn_w_in", "ffn_w_out")
BIG_NAMES = FOX_NAMES + LATE_NAMES
GATHER_LATE = ("fox_w_out",) + LATE_NAMES
COL_SHARDED = ("fox_w_in", "hgrn_w_in", "ffn_w_in")


def _shard2d(name, a):
    return a.reshape(-1, a.shape[-1])


def _to_shards(name, g):
    layers = g.shape[0]
    if name in COL_SHARDED:
        k, n = g.shape[1], g.shape[2] // 4
        return g.reshape(layers, k, 4, n).transpose(2, 0, 1, 3).reshape(4, layers * k, n)
    r = g.shape[1] // 4
    return g.reshape(layers, 4, r, g.shape[2]).transpose(1, 0, 2, 3).reshape(4, layers * r, g.shape[2])


def kernel(x, meta_tokens, attn_norm, ffn_norm, final_norm, fox_w_in, fox_b_f, fox_q_norm, fox_k_norm, fox_w_out, hgrn_w_in, hgrn_lower_bounds, hgrn_g_norm, hgrn_w_out, ffn_w_in, ffn_w_out, loss_target, m_meta_tokens, m_attn_norm, m_ffn_norm, m_final_norm, m_fox_w_in, m_fox_b_f, m_fox_q_norm, m_fox_k_norm, m_fox_w_out, m_hgrn_w_in, m_hgrn_lower_bounds, m_hgrn_g_norm, m_hgrn_w_out, m_ffn_w_in, m_ffn_w_out, v_meta_tokens, v_attn_norm, v_ffn_norm, v_final_norm, v_fox_w_in, v_fox_b_f, v_fox_q_norm, v_fox_k_norm, v_fox_w_out, v_hgrn_w_in, v_hgrn_lower_bounds, v_hgrn_g_norm, v_hgrn_w_out, v_ffn_w_in, v_ffn_w_out):
    params = dict(meta_tokens=meta_tokens, attn_norm=attn_norm, ffn_norm=ffn_norm, final_norm=final_norm,
                  fox_w_in=fox_w_in, fox_b_f=fox_b_f, fox_q_norm=fox_q_norm, fox_k_norm=fox_k_norm,
                  fox_w_out=fox_w_out, hgrn_w_in=hgrn_w_in, hgrn_lower_bounds=hgrn_lower_bounds,
                  hgrn_g_norm=hgrn_g_norm, hgrn_w_out=hgrn_w_out, ffn_w_in=ffn_w_in, ffn_w_out=ffn_w_out)
    mom_m = dict(meta_tokens=m_meta_tokens, attn_norm=m_attn_norm, ffn_norm=m_ffn_norm, final_norm=m_final_norm,
                 fox_w_in=m_fox_w_in, fox_b_f=m_fox_b_f, fox_q_norm=m_fox_q_norm, fox_k_norm=m_fox_k_norm,
                 fox_w_out=m_fox_w_out, hgrn_w_in=m_hgrn_w_in, hgrn_lower_bounds=m_hgrn_lower_bounds,
                 hgrn_g_norm=m_hgrn_g_norm, hgrn_w_out=m_hgrn_w_out, ffn_w_in=m_ffn_w_in, ffn_w_out=m_ffn_w_out)
    mom_v = dict(meta_tokens=v_meta_tokens, attn_norm=v_attn_norm, ffn_norm=v_ffn_norm, final_norm=v_final_norm,
                 fox_w_in=v_fox_w_in, fox_b_f=v_fox_b_f, fox_q_norm=v_fox_q_norm, fox_k_norm=v_fox_k_norm,
                 fox_w_out=v_fox_w_out, hgrn_w_in=v_hgrn_w_in, hgrn_lower_bounds=v_hgrn_lower_bounds,
                 hgrn_g_norm=v_hgrn_g_norm, hgrn_w_out=v_hgrn_w_out, ffn_w_in=v_ffn_w_in, ffn_w_out=v_ffn_w_out)
    names = list(params)
    xi, yi, _ = _me()

    shards = {n: _shard2d(n, params[n]).astype(BF16) for n in BIG_NAMES}
    w_in_g, meta_g = _chip_allgather([shards["fox_w_in"], meta_tokens])
    w = dict(fox_w_in=jnp.concatenate(list(w_in_g), axis=1))
    meta_full = jnp.concatenate(list(meta_g), axis=1)
    w.update(attn_norm=attn_norm, ffn_norm=ffn_norm, final_norm=final_norm.reshape(1, D), fox_b_f=fox_b_f,
             fox_q_norm=fox_q_norm, fox_k_norm=fox_k_norm, hgrn_lower_bounds=hgrn_lower_bounds,
             hgrn_g_norm=hgrn_g_norm)

    h0 = jnp.concatenate([jnp.zeros((ROW0, D), F32), meta_full, x[0]], axis=0)
    loss, dh0, big, small = _local_step(h0, loss_target[0], w, {n: shards[n] for n in GATHER_LATE})
    loss = lax.psum(loss, ("x", "y", "c"))
    grad_x = dh0[PAD:][None]
    grads = {}

    rows = jnp.concatenate([small["attn_norm"], small["ffn_norm"], small["final_norm"], small["lb_raw"],
                            small["q_gain"], small["k_gain"],
                            jnp.pad(small["b_f"], ((0, 0), (0, D - LANES))), small["g_gain"],
                            dh0[ROW0:PAD], jnp.zeros((SMALL_ROWS - 10 - N_META, D), F32)], axis=0)
    allrows = _device_allgather(rows)
    fold64 = jnp.asarray(np.tile(np.eye(FOX_DH, dtype=np.float32), (FOX_H, 1)))
    fold128 = jnp.asarray(np.tile(np.eye(HG_D, dtype=np.float32), (HG_H, 1)))
    tot, qk, gg, dlb = _small_finalize(allrows, hgrn_lower_bounds, fold64, fold128)
    grads.update(attn_norm=tot[0:2], ffn_norm=tot[2:4], final_norm=tot[4], hgrn_lower_bounds=dlb,
                 fox_q_norm=qk[0:1], fox_k_norm=qk[1:2], fox_b_f=tot[8:9, :FOX_H], hgrn_g_norm=gg,
                 meta_tokens=lax.dynamic_slice_in_dim(tot[10:10 + N_META], (2 * xi + yi) * (D // 4), D // 4, axis=1))

    delta, new_m, new_v = {}, {}, {}
    for n in BIG_NAMES:
        res = _adamw_halves(_shard2d(n, params[n]), *big[n], _shard2d(n, mom_m[n]), _shard2d(n, mom_v[n]),
                            f"adamw_{n}")
        grads[n], delta[n], new_m[n], new_v[n] = (t.reshape(params[n].shape) for t in res)
    delta["meta_tokens"], new_m["meta_tokens"], new_v["meta_tokens"] = _adamw(
        meta_tokens, grads["meta_tokens"], m_meta_tokens, v_meta_tokens, "adamw_meta_tokens")
    small_names = [n for n in names if n not in BIG_NAMES and n != "meta_tokens"]

    def pack(d):
        return jnp.concatenate([jnp.pad(d[n].reshape(-1, d[n].shape[-1]), ((0, 0), (0, D - d[n].shape[-1])))
                                for n in small_names], axis=0)

    packed = [pack(t) for t in (params, grads, mom_m, mom_v)]
    n_rows = packed[0].shape[0]
    packed = [jnp.pad(t, ((0, 16 - n_rows), (0, 0))) for t in packed]
    res = _adamw(*packed, "adamw_small")
    r0 = 0
    for n in small_names:
        nr = params[n].reshape(-1, params[n].shape[-1]).shape[0]
        for dst, src in zip((delta, new_m, new_v), res):
            dst[n] = src[r0:r0 + nr, :params[n].shape[-1]].reshape(params[n].shape)
        r0 += nr

    return (loss, grad_x, *[grads[n] for n in names], *[delta[n] for n in names],
            *[new_m[n] for n in names], *[new_v[n] for n in names])
```

```python
import functools

import numpy as np
import jax
import jax.numpy as jnp
from jax import lax
from jax.experimental import pallas as pl
from jax.experimental.pallas import tpu as pltpu

F32, BF16 = jnp.float32, jnp.bfloat16
HIGHEST = lax.Precision.HIGHEST

D = 1024
N_META = 16
PAD = 128
ROW0 = PAD - N_META
FOX_H, FOX_DH = 16, 64
HG_H, HG_D = 8, 128
HG_C = 128
HG_LEV = 7
HG_HPS = 8
HG_SKEW = 0
FFN = 2816
EPS = 1e-6
BIG = 1e30
LOG2E = 1.4426950408889634
LANES = 128
MXU_N = 256
VMEM_LIMIT = 48 * 1024 * 1024
ROW_TILES = (640, 512, 384, 320, 256, 128, 64, 32, 16, 8)
ATTN_TILES = (640, 512, 256, 128)
FOX_HPS_FWD = 8
FOX_HPS_BWD = 4

ADAM_LR, ADAM_B1, ADAM_B2, ADAM_EPS, ADAM_WD, ADAM_STEP = 0.001, 0.9, 0.999, 1e-08, 0.01, 10

MESH = pl.DeviceIdType.MESH
ANY = pl.BlockSpec(memory_space=pl.ANY)
NT = (((1,), (1,)), ((), ()))
TN = (((0,), (0,)), ((), ()))


def _tile(n, cands=ROW_TILES, cap=None):
    for c in cands:
        if n % c == 0 and (cap is None or c <= cap):
            return c
    return n


def _cparams(sem):
    return pltpu.CompilerParams(dimension_semantics=sem, vmem_limit_bytes=VMEM_LIMIT)


def _sigmoid(x):
    return jax.nn.sigmoid(x)


def _log_sigmoid(x):
    return jnp.minimum(x, 0.0) - jnp.log(1.0 + jnp.exp(-jnp.abs(x)))


def _iota(shape, dim):
    return lax.broadcasted_iota(jnp.int32, shape, dim)


def _matmul(a, b, *, ta=False, tb=False, out_dtype=F32, add=None, name):
    if ta:
        kdim, m = a.shape
    else:
        m, kdim = a.shape
    n = b.shape[0] if tb else b.shape[1]
    if ta:
        tm = m if m <= 1024 else _tile(m, (1408, 1024, 512, 256, 128))
        tk = _tile(kdim, (1664,) + ROW_TILES)
    else:
        tm = _tile(m)
        tk = kdim if kdim <= 4096 else _tile(kdim, (2048, 1024, 512))
    tn = n if n <= 1024 else _tile(n, (1408, 1024, 512, 256, 128))
    nk = kdim // tk
    dn = (((0 if ta else 1,), (1 if tb else 0,)), ((), ()))

    def body(*refs):
        if add is None:
            a_ref, b_ref, o_ref, acc_ref = refs
        else:
            a_ref, b_ref, add_ref, o_ref, acc_ref = refs
        k = pl.program_id(2)

        @pl.when(k == 0)
        def _():
            acc_ref[...] = jnp.zeros_like(acc_ref)

        acc_ref[...] += lax.dot_general(a_ref[...].astype(BF16), b_ref[...].astype(BF16), dn,
                                        preferred_element_type=F32)

        @pl.when(k == nk - 1)
        def _():
            r = acc_ref[...]
            if add is not None:
                r = r + add_ref[...].astype(F32)
            o_ref[...] = r.astype(o_ref.dtype)

    a_spec = pl.BlockSpec((tk, tm), lambda j, i, k: (k, i)) if ta else pl.BlockSpec((tm, tk), lambda j, i, k: (i, k))
    b_spec = pl.BlockSpec((tn, tk), lambda j, i, k: (j, k)) if tb else pl.BlockSpec((tk, tn), lambda j, i, k: (k, j))
    o_spec = pl.BlockSpec((tm, tn), lambda j, i, k: (i, j))
    ins, specs = [a, b], [a_spec, b_spec]
    if add is not None:
        ins.append(add)
        specs.append(o_spec)
    return pl.pallas_call(
        body, name=name, grid=(n // tn, m // tm, nk), in_specs=specs, out_specs=o_spec,
        out_shape=jax.ShapeDtypeStruct((m, n), out_dtype),
        scratch_shapes=[pltpu.VMEM((tm, tn), F32)],
        compiler_params=_cparams(("parallel", "parallel", "arbitrary")),
    )(*ins)


def _rowwise(fn, ins, bcast, outs, accs, *, name, reverse=False, carry=None, as_refs=False):
    rows = ins[0].shape[0]
    per_row = sum(x.shape[1] * x.dtype.itemsize for x in ins) + sum(c * jnp.dtype(d).itemsize for c, d in outs)
    tm = _tile(rows, cap=max(8, (10 * 1024 * 1024) // per_row))
    n = rows // tm
    n_in, n_b, n_o, n_a = len(ins), len(bcast), len(outs), len(accs)

    def body(*refs):
        in_refs = refs[:n_in]
        b_refs = refs[n_in:n_in + n_b]
        o_refs = refs[n_in + n_b:n_in + n_b + n_o]
        a_refs = refs[n_in + n_b + n_o:n_in + n_b + n_o + n_a]
        c_refs = refs[n_in + n_b + n_o + n_a:]
        i = pl.program_id(0)
        blk = (n - 1 - i) if reverse else i
        if c_refs:
            @pl.when(i == 0)
            def _():
                c_refs[0][...] = jnp.zeros_like(c_refs[0])
        args = (list(in_refs) if as_refs else [r[...] for r in in_refs], [r[...] for r in b_refs])
        o_vals, a_vals = fn(blk * tm, *args, *c_refs)
        for r, v in zip(o_refs, o_vals):
            r[...] = v.astype(r.dtype)
        if n_a:
            @pl.when(i == 0)
            def _():
                for r in a_refs:
                    r[...] = jnp.zeros_like(r)
            for r, v in zip(a_refs, a_vals):
                r[...] += v

    def row_map(i):
        return ((n - 1 - i) if reverse else i, 0)

    in_specs = [pl.BlockSpec((tm, x.shape[1]), row_map) for x in ins]
    in_specs += [pl.BlockSpec(x.shape, lambda i, nd=x.ndim: (0,) * nd) for x in bcast]
    out_specs = [pl.BlockSpec((tm, c), row_map) for c, _ in outs]
    out_specs += [pl.BlockSpec(s, lambda i: (0, 0)) for s in accs]
    out_shape = [jax.ShapeDtypeStruct((rows, c), d) for c, d in outs]
    out_shape += [jax.ShapeDtypeStruct(s, F32) for s in accs]
    res = pl.pallas_call(
        body, name=name, grid=(n,), in_specs=in_specs, out_specs=out_specs, out_shape=out_shape,
        scratch_shapes=[pltpu.VMEM(carry, F32)] if carry else [],
        compiler_params=_cparams(("arbitrary",)),
    )(*ins, *bcast)
    return res[:n_o], res[n_o:]


def _row_ids(row0, tm):
    return row0 + _iota((tm, 1), 0)


def _rms_fwd(x, gain, name):
    def fn(row0, ins, bc):
        (xv,), (g,) = ins, bc
        r = lax.rsqrt(jnp.mean(xv * xv, axis=-1, keepdims=True) + EPS)
        return [xv * r * g], []
    return _rowwise(fn, [x], [gain], [(D, BF16)], [], name=name)[0][0]


def _rms_bwd_math(xv, dy, g):
    r = lax.rsqrt(jnp.mean(xv * xv, axis=-1, keepdims=True) + EPS)
    xh = xv * r
    dxh = dy * g
    dx = r * (dxh - xh * jnp.mean(dxh * xh, axis=-1, keepdims=True))
    return dx, jnp.sum(dy * xh, axis=0, keepdims=True)


def _loss_bwd(h, tgt, gain):
    tm = PAD
    n = h.shape[0] // tm

    def body(x_ref, t_ref, g_ref, dx_ref, loss_ref, dg_ref):
        i = pl.program_id(0)

        @pl.when(i == 0)
        def _():
            loss_ref[...] = jnp.zeros_like(loss_ref)
            dg_ref[...] = jnp.zeros_like(dg_ref)

        xv, g = x_ref[...], g_ref[...]
        r = lax.rsqrt(jnp.mean(xv * xv, axis=-1, keepdims=True) + EPS)
        xh = xv * r
        err = jnp.where(i >= 1, xh * g - t_ref[...], 0.0)
        per_row = jnp.mean(err * err, axis=-1, keepdims=True)
        loss_ref[...] += jnp.broadcast_to(0.5 * jnp.sum(per_row, axis=0, keepdims=True), (1, LANES))
        dy = err * (1.0 / D)
        dxh = dy * g
        dx_ref[...] = r * (dxh - xh * jnp.mean(dxh * xh, axis=-1, keepdims=True))
        dg_ref[...] += jnp.sum(dy * xh, axis=0, keepdims=True)

    dh, loss, dgain = pl.pallas_call(
        body, name="loss_bwd", grid=(n,),
        in_specs=[pl.BlockSpec((tm, D), lambda i: (i, 0)), pl.BlockSpec((tm, D), lambda i: (jnp.maximum(i - 1, 0), 0)),
                  pl.BlockSpec((1, D), lambda i: (0, 0))],
        out_specs=[pl.BlockSpec((tm, D), lambda i: (i, 0)), pl.BlockSpec((1, LANES), lambda i: (0, 0)),
                   pl.BlockSpec((1, D), lambda i: (0, 0))],
        out_shape=[jax.ShapeDtypeStruct(h.shape, F32), jax.ShapeDtypeStruct((1, LANES), F32),
                   jax.ShapeDtypeStruct((1, D), F32)],
        compiler_params=_cparams(("arbitrary",)),
    )(h, tgt, gain)
    return loss[0, 0], dh, dgain


FFN_TILES = dict(rows=(320, 256, 128), cols=(1408, 1024, 512, 256, 128))


def _ffn_in(hn, wg, wu, name):
    m, kdim = hn.shape
    n = wg.shape[1]
    tm, tn = _tile(m, FFN_TILES["rows"]), _tile(n, FFN_TILES["cols"])

    def body(a_ref, wg_ref, wu_ref, g_ref, u_ref, act_ref):
        a = a_ref[...]
        g = jnp.dot(a, wg_ref[...], preferred_element_type=F32)
        u = jnp.dot(a, wu_ref[...], preferred_element_type=F32)
        g_ref[...] = g.astype(g_ref.dtype)
        u_ref[...] = u.astype(u_ref.dtype)
        act_ref[...] = (g * _sigmoid(g) * u).astype(act_ref.dtype)

    wspec = pl.BlockSpec((kdim, tn), lambda j, i: (0, j))
    ospec = pl.BlockSpec((tm, tn), lambda j, i: (i, j))
    return pl.pallas_call(
        body, name=name, grid=(n // tn, m // tm),
        in_specs=[pl.BlockSpec((tm, kdim), lambda j, i: (i, 0)), wspec, wspec], out_specs=[ospec] * 3,
        out_shape=[jax.ShapeDtypeStruct((m, n), BF16)] * 3,
        compiler_params=_cparams(("parallel", "parallel")),
    )(hn, wg, wu)


def _dhn_norm(dys, ws, h, dh_up, gain, name, slabs=()):
    m = h.shape[0]
    tm = _tile(m, FFN_TILES["rows"])
    n, ns = len(dys), len(slabs)
    steps = m // tm

    def body(*refs):
        dy_refs, w_refs = refs[:n], refs[n:2 * n]
        h_ref, up_ref, g_ref = refs[2 * n:2 * n + 3]
        sl_in = refs[2 * n + 3:2 * n + 3 + ns]
        dh_ref, dgain_ref = refs[2 * n + 3 + ns:2 * n + 5 + ns]
        sl_out = refs[2 * n + 5 + ns:2 * n + 5 + 2 * ns]
        sems = refs[2 * n + 5 + 2 * ns:]
        i = pl.program_id(0)

        @pl.when(i == 0)
        def _():
            dgain_ref[...] = jnp.zeros_like(dgain_ref)
            for cp in _scatter_copies(sl_in, sl_out, *sems) if ns else ():
                cp.start()

        dy = sum(lax.dot_general(a[...], w[...], NT, preferred_element_type=F32) for a, w in zip(dy_refs, w_refs))
        dx, dgain = _rms_bwd_math(h_ref[...], dy, g_ref[...])
        keep = _row_ids(i * tm, tm) >= ROW0
        dh_ref[...] = jnp.where(keep, up_ref[...] + dx, 0.0)
        dgain_ref[...] += dgain

        if ns:
            @pl.when(i == steps - 1)
            def _():
                for cp in _scatter_copies(sl_in, sl_out, *sems):
                    cp.wait()

    rows = lambda c: pl.BlockSpec((tm, c), lambda i: (i, 0))
    whole = lambda a: pl.BlockSpec(a.shape, lambda i: (0, 0), pipeline_mode=pl.Buffered(1))
    res = pl.pallas_call(
        body, name=name, grid=(steps,),
        in_specs=[rows(a.shape[1]) for a in dys] + [whole(w) for w in ws]
        + [rows(D), rows(D), pl.BlockSpec((1, D), lambda i: (0, 0))] + [ANY] * ns,
        out_specs=[rows(D), pl.BlockSpec((1, D), lambda i: (0, 0))] + [ANY] * ns,
        out_shape=[jax.ShapeDtypeStruct((m, D), F32), jax.ShapeDtypeStruct((1, D), F32)]
        + [jax.ShapeDtypeStruct((3,) + a.shape[1:], a.dtype) for a in slabs],
        scratch_shapes=[pltpu.SemaphoreType.DMA((ns, 3)), pltpu.SemaphoreType.DMA((ns, 3))] if ns else [],
        compiler_params=_cparams(("arbitrary",)),
    )(*dys, *ws, h, dh_up, gain, *slabs)
    return res[0], res[1], res[2:]


def _ffn_dact(dh, wo, g, u, name):
    m, kdim = dh.shape
    n = wo.shape[0]
    tm, tn = _tile(m, FFN_TILES["rows"]), _tile(n, FFN_TILES["cols"])

    def body(a_ref, w_ref, g_ref, u_ref, dg_ref, du_ref):
        da = lax.dot_general(a_ref[...].astype(BF16), w_ref[...], NT, preferred_element_type=F32)
        gv, uv = g_ref[...].astype(F32), u_ref[...].astype(F32)
        s = _sigmoid(gv)
        dg_ref[...] = (da * uv * (s * (1.0 + gv * (1.0 - s)))).astype(dg_ref.dtype)
        du_ref[...] = (da * gv * s).astype(du_ref.dtype)

    ospec = pl.BlockSpec((tm, tn), lambda j, i: (i, j))
    return pl.pallas_call(
        body, name=name, grid=(n // tn, m // tm),
        in_specs=[pl.BlockSpec((tm, kdim), lambda j, i: (i, 0)), pl.BlockSpec((tn, kdim), lambda j, i: (j, 0)),
                  ospec, ospec],
        out_specs=[ospec] * 2, out_shape=[jax.ShapeDtypeStruct((m, n), BF16)] * 2,
        compiler_params=_cparams(("parallel", "parallel")),
    )(dh, wo, g, u)


def _adamw_math(wv, gv, mv, vv):
    mn = ADAM_B1 * mv + (1.0 - ADAM_B1) * gv
    vn = ADAM_B2 * vv + (1.0 - ADAM_B2) * (gv * gv)
    m_hat = mn / (1.0 - ADAM_B1 ** ADAM_STEP)
    v_hat = vn / (1.0 - ADAM_B2 ** ADAM_STEP)
    return -ADAM_LR * (m_hat / (jnp.sqrt(v_hat) + ADAM_EPS) + ADAM_WD * wv), mn, vn


def _adamw(w, g, m, v, name):
    def fn(row0, ins, bc):
        return list(_adamw_math(*ins)), []
    c = w.shape[1]
    return _rowwise(fn, [w, g, m, v], [], [(c, F32)] * 3, [], name=name)[0]


def _adamw_halves(w, mine, theirs, m, v, name):
    rows, cols = w.shape
    half = rows // 2
    tm = _tile(half, cap=(10 * 1024 * 1024) // (9 * 4 * cols))
    nb = half // tm

    def body(c_ref, w_ref, g1_ref, g2_ref, m_ref, v_ref, g_out, d_out, m_out, v_out):
        own = (pl.program_id(0) // nb) == c_ref[0]
        g = jnp.where(own, g1_ref[...], g2_ref[...])
        delta, mn, vn = _adamw_math(w_ref[...], g, m_ref[...], v_ref[...])
        g_out[...] = g
        d_out[...] = delta
        m_out[...] = mn
        v_out[...] = vn

    full = pl.BlockSpec((tm, cols), lambda i, c: (i, 0))
    part = pl.BlockSpec((tm, cols), lambda i, c: (lax.rem(i, nb), 0))
    return pl.pallas_call(
        body, name=name,
        grid_spec=pltpu.PrefetchScalarGridSpec(num_scalar_prefetch=1, grid=(2 * nb,),
                                               in_specs=[full, part, part, full, full], out_specs=[full] * 4),
        out_shape=[jax.ShapeDtypeStruct((rows, cols), F32)] * 4,
        compiler_params=_cparams(("parallel",)),
    )(_mesh_scalar(lax.axis_index("c")), w, mine, theirs, m, v)


def _head_sum(x, gmat):
    hi = x.astype(BF16)
    lo = (x - hi.astype(F32)).astype(BF16)
    w = gmat.shape[0]
    return jnp.concatenate(
        [jnp.dot(hi[:, b:b + w], gmat, preferred_element_type=F32) + jnp.dot(lo[:, b:b + w], gmat,
                                                                             preferred_element_type=F32)
         for b in range(0, x.shape[1], w)], axis=1)


def _split3(x):
    hi = x.astype(BF16).astype(F32)
    r = x - hi
    mid = r.astype(BF16).astype(F32)
    return hi, mid, r - mid


def _extra_base(hh):
    return FOX_DH * (1 - hh)


def _data_mask(hh):
    lane = _iota((1, LANES), 1)
    return (lane >= FOX_DH * hh) & (lane < FOX_DH * (hh + 1))


def _with_extras(data, hh, vals):
    lane = _iota((1, LANES), 1)
    x = jnp.zeros_like(data)
    for e, v in enumerate(vals):
        x = jnp.where(lane == _extra_base(hh) + e, v, x)
    return jnp.where(_data_mask(hh), data, x)


def _fox_pack_fwd(q_raw, k_raw, v, cq, ck, qg, kg, gmat):
    scale2 = FOX_DH ** -0.5 * LOG2E

    def fn(row0, refs, bc):
        q_ref, k_ref, v_ref, cq_ref, ck_ref = refs
        g_q, g_k, gm = bc
        qv, kv = q_ref[...], k_ref[...]
        qn = qv * lax.rsqrt(_head_sum(qv * qv, gm) * (1.0 / FOX_DH) + EPS) * (g_q * scale2)
        kn = kv * lax.rsqrt(_head_sum(kv * kv, gm) * (1.0 / FOX_DH) + EPS) * g_k
        qs, ks, vs = [], [], []
        for h in range(FOX_H):
            p, hh = divmod(h, 2)
            sl = slice(p * LANES, (p + 1) * LANES)
            cq3 = _split3(cq_ref[:, h:h + 1] * LOG2E)
            ck3 = _split3(ck_ref[:, h:h + 1] * (-LOG2E))
            qs.append(_with_extras(qn[:, sl], hh, [*cq3, 1.0, 1.0, 1.0]))
            ks.append(_with_extras(kn[:, sl], hh, [1.0, 1.0, 1.0, *ck3]))
            vs.append(_with_extras(v_ref[:, sl].astype(F32), hh, [1.0, 1.0]))
        return [jnp.concatenate(qs, axis=1), jnp.concatenate(ks, axis=1), jnp.concatenate(vs, axis=1)], []

    w = FOX_H * LANES
    return _rowwise(fn, [q_raw, k_raw, v, cq, ck], [qg, kg, gmat], [(w, BF16)] * 3, [], name="fox_pack_fwd",
                    as_refs=True)[0]


def _fox_pack_bias(qp, cq, lse2):
    def fn(row0, refs, bc):
        q_ref, cq_ref, lse_ref = refs
        lane = _iota((1, LANES), 1)
        outs = []
        for h in range(FOX_H):
            blk = q_ref[:, h * LANES:(h + 1) * LANES].astype(F32)
            for e, part in enumerate(_split3(cq_ref[:, h:h + 1] * LOG2E - lse_ref[:, h:h + 1])):
                blk = jnp.where(lane == _extra_base(h % 2) + e, part, blk)
            outs.append(blk)
        return [jnp.concatenate(outs, axis=1)], []
    return _rowwise(fn, [qp, cq, lse2], [], [(FOX_H * LANES, BF16)], [], name="fox_pack_bias", as_refs=True)[0][0]


def _fox_pack_bwd(dog, o, gate):
    def fn(row0, refs, bc):
        d_ref, o_ref, g_ref = refs
        dos, dgs = [], []
        for p in range(FOX_H // 2):
            sl = slice(p * LANES, (p + 1) * LANES)
            dv, ov, gv = (r[:, sl].astype(F32) for r in (d_ref, o_ref, g_ref))
            s = _sigmoid(gv)
            do = dv * s
            dgs.append(dv * ov * s * (1.0 - s))
            od = ov * do
            for hh in range(2):
                delta = jnp.sum(jnp.where(_data_mask(hh), od, 0.0), axis=-1, keepdims=True)
                hi = delta.astype(BF16).astype(F32)
                dos.append(_with_extras(do, hh, [-hi, hi - delta]))
        return [jnp.concatenate(dos, axis=1), jnp.concatenate(dgs, axis=1)], []
    return _rowwise(fn, [dog, o, gate], [], [(FOX_H * LANES, BF16), (D, BF16)], [], name="fox_pack_bwd",
                    as_refs=True)[0]


def _fox_unpack_bwd(q_raw, k_raw, dqp, dk, qg, kg, gmat):
    scale = FOX_DH ** -0.5

    def fn(row0, refs, bc):
        q_ref, k_ref, dq_ref, dk_ref = refs
        g_q, g_k, gm = bc
        lane = _iota((1, LANES), 1)
        dqs = []
        dcq = jnp.zeros((q_ref.shape[0], LANES), F32)
        for p in range(FOX_H // 2):
            even = dq_ref[:, (2 * p) * LANES:(2 * p + 1) * LANES]
            odd = dq_ref[:, (2 * p + 1) * LANES:(2 * p + 2) * LANES]
            dqs.append(jnp.where(_data_mask(0), even, odd) * scale)
            for hh in range(2):
                col = (2 * p + hh) * LANES + _extra_base(hh)
                dcq = jnp.where(lane == 2 * p + hh, dq_ref[:, col:col + 1], dcq)
        outs, accs = [], []
        for xv, dy, g in ((q_ref[...], jnp.concatenate(dqs, axis=1), g_q), (k_ref[...], dk_ref[...] * (1.0 / LOG2E), g_k)):
            r = lax.rsqrt(_head_sum(xv * xv, gm) * (1.0 / FOX_DH) + EPS)
            xh = xv * r
            dxh = dy * g
            outs.append(r * (dxh - xh * (_head_sum(dxh * xh, gm) * (1.0 / FOX_DH))))
            accs.append(jnp.sum(dy * xh, axis=0, keepdims=True))
        return outs + [dcq], accs
    return _rowwise(fn, [q_raw, k_raw, dqp, dk], [qg, kg, gmat], [(D, BF16), (D, BF16), (LANES, F32)],
                    [(1, D), (1, D)], name="fox_unpack_bwd", as_refs=True)


def _fox_cumsum_fwd(flog, bf):
    def fn(row0, ins, bc, carry):
        (f,), (b,) = ins, bc
        tm = f.shape[0]
        keep = _row_ids(row0, tm) >= ROW0
        lf = jnp.where(keep, _log_sigmoid(f + b), 0.0)
        tri = (_iota((tm, tm), 0) >= _iota((tm, tm), 1)).astype(F32)
        c = jnp.dot(tri, lf, precision=HIGHEST, preferred_element_type=F32) + carry[...]
        carry[...] = carry[...] + jnp.sum(lf, axis=0, keepdims=True)
        return [c, jnp.where(keep, c, BIG)], []
    return _rowwise(fn, [flog], [bf], [(LANES, F32), (LANES, F32)], [], name="fox_cumsum_fwd",
                    carry=(1, LANES))[0]


def _fox_cumsum_bwd(dc_q, dc_k, flog, bf):
    def fn(row0, ins, bc, carry):
        (dq, dk, f), (b,) = ins, bc
        d = dq + dk
        tm = f.shape[0]
        keep = _row_ids(row0, tm) >= ROW0
        triu = (_iota((tm, tm), 0) <= _iota((tm, tm), 1)).astype(F32)
        dlf = jnp.dot(triu, d, precision=HIGHEST, preferred_element_type=F32) + carry[...]
        carry[...] = carry[...] + jnp.sum(d, axis=0, keepdims=True)
        dfl = jnp.where(keep, dlf * _sigmoid(-(f + b)), 0.0)
        return [dfl], [jnp.sum(dfl, axis=0, keepdims=True)]
    (dflog,), (dbf,) = _rowwise(fn, [dc_q, dc_k, flog], [bf], [(LANES, F32)], [(1, LANES)], name="fox_cumsum_bwd",
                                reverse=True, carry=(1, LANES))
    return dflog, dbf


def _causal_steps(n, key_major):
    if key_major:
        pairs = [(i, j) for j in range(n) for i in range(j, n)]
    else:
        pairs = [(i, j) for i in range(n) for j in range(i + 1)]
    return (jnp.asarray(np.array([p[0] for p in pairs], np.int32)),
            jnp.asarray(np.array([p[1] for p in pairs], np.int32)))


def _fox_attn_fwd(qp, kp, vp, gate, shards):
    L = qp.shape[0]
    t = _tile(L, ATTN_TILES)
    n = L // t
    hps = FOX_HPS_FWD
    P = FOX_H // hps
    it, jt = _causal_steps(n, False)
    n_steps = it.shape[0]
    ns = len(shards)

    def body(it_ref, jt_ref, q_ref, k_ref, v_ref, g_ref, *rest):
        sh_in, (o_ref, og_ref, lse_ref), sh_out = rest[:ns], rest[ns:ns + 3], rest[ns + 3:2 * ns + 3]
        m_sc, acc, ssem, rsem, lsem = rest[2 * ns + 3:]
        step = pl.program_id(1)
        i, j = it_ref[step], jt_ref[step]
        first = (pl.program_id(0) == 0) & (step == 0)
        last = (pl.program_id(0) == P - 1) & (step == n_steps - 1)

        @pl.when(first)
        def _():
            _gather_start(sh_in, sh_out, ssem, rsem, lsem)

        @pl.when(j == 0)
        def _():
            m_sc[...] = jnp.full_like(m_sc, -3.0e38)
            acc[...] = jnp.zeros_like(acc)

        def update(masked):
            def head(hh):
                sl = slice(hh * LANES, (hh + 1) * LANES)
                s2 = lax.dot_general(k_ref[:, sl], q_ref[:, sl], NT, preferred_element_type=F32)
                if masked:
                    s2 = jnp.where(_iota((t, t), 0) <= _iota((t, t), 1), s2, -jnp.inf)
                yield
                m_old = m_sc[hh]
                m_new = jnp.maximum(m_old, jnp.max(s2, axis=0, keepdims=True))
                p = jnp.exp2(s2 - m_new).astype(BF16)
                yield
                acc[hh] = jnp.exp2(m_old - m_new) * acc[hh] + lax.dot_general(v_ref[:, sl], p, TN,
                                                                              preferred_element_type=F32)
                m_sc[hh] = m_new

            _interleave((head(hh) for hh in range(hps)), skew=1)

        @pl.when(j < i)
        def _():
            update(False)

        @pl.when(j == i)
        def _():
            update(True)
            outs = []
            for hh in range(hps):
                base = _extra_base(hh % 2)
                l = acc[hh, base:base + 1, :]
                outs.append((acc[hh] / l).T)
                lse_ref[0, hh:hh + 1, :] = m_sc[hh] + jnp.log2(l)
            o = jnp.concatenate([jnp.where(_data_mask(0), outs[a], outs[a + 1]) for a in range(0, hps, 2)], axis=1)
            o_ref[...] = o.astype(o_ref.dtype)
            og_ref[...] = (o * _sigmoid(g_ref[...].astype(F32))).astype(og_ref.dtype)

        @pl.when(last)
        def _():
            _gather_wait(sh_in, sh_out, ssem, rsem, lsem)

    qspec = pl.BlockSpec((t, hps * LANES), lambda p, s, it, jt: (it[s], p))
    kspec = pl.BlockSpec((t, hps * LANES), lambda p, s, it, jt: (jt[s], p))
    ospec = pl.BlockSpec((t, hps * FOX_DH), lambda p, s, it, jt: (it[s], p))
    lspec = pl.BlockSpec((1, hps, t), lambda p, s, it, jt: (p, 0, it[s]))
    res = pl.pallas_call(
        body, name="fox_attn_fwd",
        grid_spec=pltpu.PrefetchScalarGridSpec(
            num_scalar_prefetch=2, grid=(P, n_steps),
            in_specs=[qspec, kspec, kspec, ospec] + [ANY] * ns, out_specs=[ospec, ospec, lspec] + [ANY] * ns,
            scratch_shapes=[pltpu.VMEM((hps, 1, t), F32), pltpu.VMEM((hps, LANES, t), F32)] + _gather_sems(ns)),
        out_shape=[jax.ShapeDtypeStruct((L, D), BF16), jax.ShapeDtypeStruct((L, D), BF16),
                   jax.ShapeDtypeStruct((P, hps, L), F32)]
        + [jax.ShapeDtypeStruct((4,) + a.shape, a.dtype) for a in shards],
        compiler_params=_cparams(("arbitrary", "arbitrary")),
    )(it, jt, qp, kp, vp, gate, *shards)
    return res[0], res[1], res[2], res[3:]


def _fox_attn_bwd(qb, kp, vp, dop, slabs):
    L = qb.shape[0]
    t = _tile(L, ATTN_TILES)
    n = L // t
    hps = FOX_HPS_BWD
    P = FOX_H // hps
    it, jt = _causal_steps(n, True)
    n_steps = it.shape[0]
    ns = len(slabs)

    def body(it_ref, jt_ref, q_ref, k_ref, v_ref, do_ref, *rest):
        sl_in, (dq_ref, dk_ref, dv_ref, dck_ref), sl_out = rest[:ns], rest[ns:ns + 4], rest[ns + 4:2 * ns + 4]
        dk_acc, dv_acc, ssem, rsem = rest[2 * ns + 4:]
        step = pl.program_id(1)
        i, j = it_ref[step], jt_ref[step]

        @pl.when((pl.program_id(0) == 0) & (step == 0))
        def _():
            for cp in _scatter_copies(sl_in, sl_out, ssem, rsem):
                cp.start()

        @pl.when(step == 0)
        def _():
            dq_ref[...] = jnp.zeros_like(dq_ref)

        @pl.when(i == j)
        def _():
            dk_acc[...] = jnp.zeros_like(dk_acc)
            dv_acc[...] = jnp.zeros_like(dv_acc)

        def update(masked):
            rows = pl.ds(pl.multiple_of(i * t, LANES), t)
            for hh in range(hps):
                sl = slice(hh * LANES, (hh + 1) * LANES)
                q, k, dov = q_ref[:, sl], k_ref[:, sl], do_ref[:, sl]
                s2 = lax.dot_general(k, q, NT, preferred_element_type=F32)
                if masked:
                    s2 = jnp.where(_iota((t, t), 0) <= _iota((t, t), 1), s2, -jnp.inf)
                p = jnp.exp2(s2)
                ds = (p * lax.dot_general(v_ref[:, sl], dov, NT, preferred_element_type=F32)).astype(BF16)
                dv_acc[hh] += jnp.dot(p.astype(BF16), dov, preferred_element_type=F32)
                dk_acc[hh] += jnp.dot(ds, q, preferred_element_type=F32)
                dq_ref[rows, sl] += lax.dot_general(ds, k, TN, preferred_element_type=F32)

        @pl.when(i > j)
        def _():
            update(False)

        @pl.when(i == j)
        def _():
            update(True)

        @pl.when(i == n - 1)
        def _():
            pairs = range(0, hps, 2)
            dk_ref[...] = jnp.concatenate([jnp.where(_data_mask(0), dk_acc[a], dk_acc[a + 1]) for a in pairs], axis=1)
            dv_ref[...] = jnp.concatenate([jnp.where(_data_mask(0), dv_acc[a], dv_acc[a + 1]) for a in pairs],
                                          axis=1).astype(dv_ref.dtype)
            lane = _iota((1, hps), 1)
            col_sums = jnp.zeros((t, hps), F32)
            for hh in range(hps):
                base = _extra_base(hh % 2) + 3
                col_sums = jnp.where(lane == hh, dk_acc[hh, :, base:base + 1], col_sums)
            dck_ref[0] = -col_sums

        @pl.when((pl.program_id(0) == P - 1) & (step == n_steps - 1))
        def _():
            for cp in _scatter_copies(sl_in, sl_out, ssem, rsem):
                cp.wait()

    qspec = pl.BlockSpec((t, hps * LANES), lambda p, s, it, jt: (it[s], p))
    kspec = pl.BlockSpec((t, hps * LANES), lambda p, s, it, jt: (jt[s], p))
    ospec = pl.BlockSpec((t, hps * FOX_DH), lambda p, s, it, jt: (jt[s], p))
    slab = pl.BlockSpec((L, hps * LANES), lambda p, s, it, jt: (0, p), pipeline_mode=pl.Buffered(1))
    res = pl.pallas_call(
        body, name="fox_attn_bwd",
        grid_spec=pltpu.PrefetchScalarGridSpec(
            num_scalar_prefetch=2, grid=(P, n_steps),
            in_specs=[qspec, kspec, kspec, qspec] + [ANY] * ns,
            out_specs=[slab, ospec, ospec,
                       pl.BlockSpec((1, t, hps), lambda p, s, it, jt: (p, jt[s], 0))] + [ANY] * ns,
            scratch_shapes=[pltpu.VMEM((hps, t, LANES), F32), pltpu.VMEM((hps, t, LANES), F32),
                            pltpu.SemaphoreType.DMA((ns, 3)), pltpu.SemaphoreType.DMA((ns, 3))]),
        out_shape=[jax.ShapeDtypeStruct((L, FOX_H * LANES), F32), jax.ShapeDtypeStruct((L, D), F32),
                   jax.ShapeDtypeStruct((L, D), BF16), jax.ShapeDtypeStruct((P, L, hps), F32)]
        + [jax.ShapeDtypeStruct((3,) + a.shape[1:], a.dtype) for a in slabs],
        compiler_params=_cparams(("arbitrary", "arbitrary")),
    )(it, jt, qb, kp, vp, dop, *slabs)
    return res[0], res[1], res[2], res[3], res[4:]


def _hgrn_consts():
    C = HG_C
    r = np.arange(C)[:, None]
    j = np.arange(C)[None, :]
    mats = [j <= r, j > r]
    masks = []
    n = C
    while n >= 2:
        half = n // 2
        mid = (r // n) * n + half - 1
        second = (r % n) >= half
        mats.append(np.where(second, (j > mid) & (j <= r), (j > r) & (j <= mid)))
        masks.append(((r // n) == (j // n)) & ((r % n) >= half) & ((j % n) < half))
        n //= 2
    return (jnp.asarray(np.concatenate(mats, 0).astype(np.float32), BF16),
            jnp.asarray(np.stack(masks).astype(np.float32), F32))


def _hg_pre(hq, hz, h0, h1):
    mx = jnp.maximum(h0, h1)
    e0, e1 = jnp.exp(h0 - mx), jnp.exp(h1 - mx)
    lb = e1 / (e0 + e1)
    sq = _sigmoid(hq)
    sz = _sigmoid(hz)
    snz = 1.0 - sz
    k = (1.0 - lb) * snz
    g = jnp.maximum(jnp.log(lb + (1.0 - lb) * sz), -BIG)
    return lb, hq * sq, sq, k, sz, snz, g


def _hg_decays(g, rmat):
    hi = g.astype(BF16)
    lo = (g - hi.astype(F32)).astype(BF16)
    d = jnp.dot(rmat, jnp.concatenate([hi, lo], axis=1), preferred_element_type=F32)
    return jnp.exp(d[:, :HG_D] + d[:, HG_D:])


def _interleave(programs, skew=0):
    progs = list(programs)
    done = [False] * len(progs)
    tick = 0
    while not all(done):
        for n, g in enumerate(progs):
            if not done[n] and tick >= n * skew:
                try:
                    next(g)
                except StopIteration:
                    done[n] = True
        tick += 1


def _hg_intra_levels(q, k, fall, masks):
    C = HG_C
    eye = _iota((C, C), 0) == _iota((C, C), 1)
    a = jnp.where(eye, jnp.sum(q * k, axis=-1, keepdims=True), 0.0)
    for l in range(HG_LEV):
        f = fall[(2 + l) * C:(3 + l) * C]
        a = a + masks[l] * lax.dot_general((q * f).astype(BF16), (k * f).astype(BF16), NT,
                                           preferred_element_type=F32)
        yield a


def _hgrn_specs(n_chunks, reverse):
    C = HG_C
    w = HG_HPS * HG_D

    def col(first_head):
        off = first_head // HG_HPS
        if reverse:
            return pl.BlockSpec((C, w), lambda h, c: (n_chunks - 1 - c, off + h))
        return pl.BlockSpec((C, w), lambda h, c: (c, off + h))

    st = pl.BlockSpec((HG_HPS, 1, HG_D, HG_D),
                      (lambda h, c: (h, n_chunks - 1 - c, 0, 0)) if reverse else (lambda h, c: (h, c, 0, 0)))
    consts = [pl.BlockSpec((2, w), lambda h, c: (0, h)), pl.BlockSpec((1, HG_D), lambda h, c: (0, 0)),
              pl.BlockSpec(((2 + HG_LEV) * C, C), lambda h, c: (0, 0)),
              pl.BlockSpec((HG_LEV, C, C), lambda h, c: (0, 0, 0))]
    return col, st, consts


def _hgrn_fwd(proj, hlb, gg, rmat, masks):
    L = proj.shape[0]
    C = HG_C
    nc = L // C
    col, st, consts = _hgrn_specs(nc, False)

    def body(hq_ref, hz_ref, hi_ref, hg_ref, hlb_ref, gg_ref, r_ref, m_ref, og_ref, st_ref, state):
        c = pl.program_id(1)

        @pl.when(c == 0)
        def _():
            state[...] = jnp.zeros_like(state)

        def head(hh):
            sl = slice(hh * HG_D, (hh + 1) * HG_D)
            v, hg = hi_ref[:, sl], hg_ref[:, sl]
            _, q, _, k, _, _, g = _hg_pre(hq_ref[:, sl], hz_ref[:, sl], hlb_ref[0:1, sl], hlb_ref[1:2, sl])
            yield
            fall = _hg_decays(g, r_ref[...])
            fb, fe = fall[0:C], fall[C:2 * C]
            st0 = state[hh]
            st_ref[hh, 0] = st0
            yield
            for a in _hg_intra_levels(q, k, fall, m_ref[...]):
                yield
            vb = v.astype(BF16)
            o = jnp.dot(a.astype(BF16), vb, preferred_element_type=F32)
            o = o + lax.dot_general((q * fb).astype(BF16), st0.astype(BF16), NT, preferred_element_type=F32)
            yield
            ebc = jnp.exp(jnp.sum(g, axis=0, keepdims=True))
            state[hh] = st0 * ebc + lax.dot_general(vb, (k * fe).astype(BF16), TN, preferred_element_type=F32)
            r = lax.rsqrt(jnp.mean(o * o, axis=-1, keepdims=True) + EPS)
            og_ref[:, sl] = (o * r * gg_ref[...] * (hg * _sigmoid(hg))).astype(og_ref.dtype)

        _interleave((head(hh) for hh in range(HG_HPS)), skew=HG_SKEW)

    return pl.pallas_call(
        body, name="hgrn_fwd", grid=(HG_H // HG_HPS, nc),
        in_specs=[col(0), col(HG_H), col(2 * HG_H), col(3 * HG_H)] + consts,
        out_specs=[col(0), st],
        out_shape=[jax.ShapeDtypeStruct((L, D), BF16), jax.ShapeDtypeStruct((HG_H, nc, HG_D, HG_D), F32)],
        scratch_shapes=[pltpu.VMEM((HG_HPS, HG_D, HG_D), F32)],
        compiler_params=_cparams(("parallel", "arbitrary")),
    )(proj, proj, proj, proj, hlb, gg, rmat, masks)


def _hgrn_bwd(proj, dog, states, hlb, gg, rmat, masks):
    L = proj.shape[0]
    C = HG_C
    nc = L // C
    col, st, consts = _hgrn_specs(nc, True)

    def body(hq_ref, hz_ref, hi_ref, hg_ref, do_ref, hlb_ref, gg_ref, r_ref, m_ref, st_ref,
             dq_ref, dz_ref, di_ref, dg_ref, dlb_ref, dgg_ref, dstate):
        c = pl.program_id(1)

        @pl.when(c == 0)
        def _():
            dstate[...] = jnp.zeros_like(dstate)
            dlb_ref[...] = jnp.zeros_like(dlb_ref)
            dgg_ref[...] = jnp.zeros_like(dgg_ref)

        _interleave([bwd_head(c, hh, slice(hh * HG_D, (hh + 1) * HG_D), hq_ref, hz_ref, hi_ref, hg_ref, do_ref, hlb_ref,
                              gg_ref, r_ref, m_ref, st_ref, dq_ref, dz_ref, di_ref, dg_ref, dlb_ref, dgg_ref, dstate)
                     for hh in range(HG_HPS)], skew=HG_SKEW)

    def bwd_head(c, hh, sl, hq_ref, hz_ref, hi_ref, hg_ref, do_ref, hlb_ref, gg_ref, r_ref, m_ref, st_ref,
                 dq_ref, dz_ref, di_ref, dg_ref, dlb_ref, dgg_ref, dstate):
        hq, hz, v, hg = hq_ref[:, sl], hz_ref[:, sl], hi_ref[:, sl], hg_ref[:, sl]
        dout = do_ref[:, sl].astype(F32)
        gain = gg_ref[...]
        masks_v = m_ref[...]
        lb, q, sq, k, sz, snz, g = _hg_pre(hq, hz, hlb_ref[0:1, sl], hlb_ref[1:2, sl])
        yield
        fall = _hg_decays(g, r_ref[...])
        fb, fe = fall[0:C], fall[C:2 * C]
        yield
        for a in _hg_intra_levels(q, k, fall, masks_v):
            yield
        st0 = st_ref[hh, 0]
        st0b = st0.astype(BF16)
        ebc = jnp.exp(jnp.sum(g, axis=0, keepdims=True))
        qb, ke, vb = (q * fb).astype(BF16), (k * fe).astype(BF16), v.astype(BF16)
        ab = a.astype(BF16)
        o = jnp.dot(ab, vb, preferred_element_type=F32) + lax.dot_general(qb, st0b, NT, preferred_element_type=F32)
        yield
        r = lax.rsqrt(jnp.mean(o * o, axis=-1, keepdims=True) + EPS)
        oh = o * r
        sg = _sigmoid(hg)
        d_on = dout * (hg * sg)
        dhg = dout * (oh * gain) * (sg * (1.0 + hg * (1.0 - sg)))
        dgg_ref[hh] += jnp.sum(d_on * oh, axis=0, keepdims=True)
        dxh = d_on * gain
        do = r * (dxh - oh * jnp.mean(dxh * oh, axis=-1, keepdims=True))
        dob = do.astype(BF16)
        yield
        dsp = dstate[hh]
        dspb = dsp.astype(BF16)
        causal = _iota((C, C), 0) >= _iota((C, C), 1)
        da = jnp.where(causal, lax.dot_general(dob, vb, NT, preferred_element_type=F32), 0.0)
        diag = jnp.sum(do * v, axis=-1, keepdims=True)
        dv = lax.dot_general(ab, dob, TN, preferred_element_type=F32)
        dv = dv + lax.dot_general(ke, dspb, NT, preferred_element_type=F32)
        xq = jnp.dot(dob, st0b, preferred_element_type=F32)
        xk = jnp.dot(vb, dspb, preferred_element_type=F32)
        dq = diag * k + fb * xq
        dk = diag * q + fe * xk
        ke_xk = ke.astype(F32) * xk
        db = qb.astype(F32) * xq - ke_xk
        yield
        for l in range(HG_LEV):
            f = fall[(2 + l) * C:(3 + l) * C]
            dal = (masks_v[l] * da).astype(BF16)
            ql, kl = (q * f).astype(BF16), (k * f).astype(BF16)
            xq = jnp.dot(dal, kl, preferred_element_type=F32)
            xk = lax.dot_general(dal, ql, TN, preferred_element_type=F32)
            dq = dq + f * xq
            dk = dk + f * xk
            db = db + ql.astype(F32) * xq - kl.astype(F32) * xk
            yield
        dstate[hh] = dsp * ebc + lax.dot_general(dob, qb, TN, preferred_element_type=F32)
        triu = (_iota((C, C), 0) <= _iota((C, C), 1)).astype(F32)
        dg = jnp.dot(triu, db, precision=HIGHEST, preferred_element_type=F32)
        dg = dg + jnp.sum(st0 * ebc * dsp, axis=0, keepdims=True) + jnp.sum(ke_xk, axis=0, keepdims=True)
        keep = _row_ids((nc - 1 - c) * C, C) >= ROW0
        dg = jnp.where(keep, dg, 0.0)
        dk = jnp.where(keep, dk, 0.0)
        f_gate = lb + (1.0 - lb) * sz
        dfdz = (1.0 - lb) * sz * snz
        dz_ref[:, sl] = (dg * dfdz / f_gate - dk * dfdz).astype(dz_ref.dtype)
        dlb_ref[:, sl] += jnp.sum(dg * snz / f_gate - dk * snz, axis=0, keepdims=True)
        dq_ref[:, sl] = jnp.where(keep, dq * (sq * (1.0 + hq * (1.0 - sq))), 0.0).astype(dq_ref.dtype)
        di_ref[:, sl] = jnp.where(keep, dv, 0.0).astype(di_ref.dtype)
        dg_ref[:, sl] = jnp.where(keep, dhg, 0.0).astype(dg_ref.dtype)

    w = HG_HPS * HG_D
    outs = pl.pallas_call(
        body, name="hgrn_bwd", grid=(HG_H // HG_HPS, nc),
        in_specs=[col(0), col(HG_H), col(2 * HG_H), col(3 * HG_H), col(0)] + consts + [st],
        out_specs=[col(0), col(0), col(0), col(0), pl.BlockSpec((1, w), lambda h, c: (0, h)),
                   pl.BlockSpec((HG_HPS, 1, HG_D), lambda h, c: (h, 0, 0))],
        out_shape=[jax.ShapeDtypeStruct((L, D), BF16)] * 4 + [jax.ShapeDtypeStruct((1, D), F32),
                                                              jax.ShapeDtypeStruct((HG_H, 1, HG_D), F32)],
        scratch_shapes=[pltpu.VMEM((HG_HPS, HG_D, HG_D), F32)],
        compiler_params=_cparams(("parallel", "arbitrary")),
    )(proj, proj, proj, proj, dog, hlb, gg, rmat, masks, states)
    return outs


def _ffn_fwd(h, norm_gain, wg, wu, wo, tag):
    hn = _rms_fwd(h, norm_gain, f"{tag}_norm")
    g, u, act = _ffn_in(hn, wg, wu, f"{tag}_in")
    h_out = _matmul(act, wo, add=h, name=f"{tag}_out")
    return h_out, (h, hn, g, u, act)


def _ffn_bwd(dh, saved, norm_gain, wg, wu, wo, tag):
    h, hn, g, u, act = saved
    dg, du = _ffn_dact(dh, wo, g, u, f"{tag}_dact")
    d_wo = _matmul(act, dh, ta=True, name=f"{tag}_dwo")
    d_wg = _matmul(hn, dg, ta=True, name=f"{tag}_dwg")
    d_wu = _matmul(hn, du, ta=True, name=f"{tag}_dwu")
    dh, d_gain, _ = _dhn_norm([dg, du], [wg, wu], h, dh, norm_gain, f"{tag}_dhn_norm")
    return dh, d_gain, (d_wg, d_wu, d_wo)


def _local_step(h0, tgt, w, late_shards):
    L = h0.shape[0]
    gmat = jnp.asarray(np.kron(np.eye(MXU_N // FOX_DH), np.ones((FOX_DH, FOX_DH))).astype(np.float32), BF16)
    rmat, lmasks = _hgrn_consts()
    an, fn_ = w["attn_norm"], w["ffn_norm"]
    qg = jnp.tile(w["fox_q_norm"], (1, FOX_H))
    kg = jnp.tile(w["fox_k_norm"], (1, FOX_H))
    bf = jnp.pad(w["fox_b_f"], ((0, 0), (0, LANES - FOX_H)))
    fw = w["fox_w_in"]
    f_wq, f_wk, f_wv, f_wg = (fw[:, i * D:(i + 1) * D] for i in range(4))
    f_wf = jnp.pad(fw[:, 4 * D:], ((0, 0), (0, LANES - FOX_H)))

    hn0 = _rms_fwd(h0, an[0:1], "fox_norm")
    q_raw = _matmul(hn0, f_wq, name="fox_q")
    k_raw = _matmul(hn0, f_wk, name="fox_k")
    v = _matmul(hn0, f_wv, out_dtype=BF16, name="fox_v")
    gate = _matmul(hn0, f_wg, out_dtype=BF16, name="fox_gate")
    flog = _matmul(hn0, f_wf, name="fox_flog")
    cq, ck = _fox_cumsum_fwd(flog, bf)
    qp, kp, vp = _fox_pack_fwd(q_raw, k_raw, v, cq, ck, qg, kg, gmat)
    o, og, lse2, gathered = _fox_attn_fwd(qp, kp, vp, gate, [late_shards[n] for n in GATHER_LATE])
    late = dict(zip(GATHER_LATE, gathered))
    f_wo = late["fox_w_out"].reshape(D, D)
    h_wo = late["hgrn_w_out"].reshape(D, D)
    h_wi = jnp.concatenate(list(late["hgrn_w_in"]), axis=1)
    g_in, g_out = late["ffn_w_in"], late["ffn_w_out"]
    ffw = []
    for i in range(2):
        rows_in, rows_out = slice(i * D, (i + 1) * D), slice(i * FFN // 4, (i + 1) * FFN // 4)
        ffw.append((jnp.concatenate([g_in[0, rows_in], g_in[1, rows_in]], axis=1),
                    jnp.concatenate([g_in[2, rows_in], g_in[3, rows_in]], axis=1),
                    jnp.concatenate([g_out[j, rows_out] for j in range(4)], axis=0)))
    h1 = _matmul(og, f_wo, add=h0, name="fox_out")
    h2, ffn0 = _ffn_fwd(h1, fn_[0:1], *ffw[0], "ffn0")

    hn2 = _rms_fwd(h2, an[1:2], "hgrn_norm")
    proj = _matmul(hn2, h_wi, name="hgrn_in")
    og1, states = _hgrn_fwd(proj, w["hgrn_lower_bounds"], w["hgrn_g_norm"], rmat, lmasks)
    h3 = _matmul(og1, h_wo, add=h2, name="hgrn_out")
    h4, ffn1 = _ffn_fwd(h3, fn_[1:2], *ffw[1], "ffn1")

    loss, dh, d_final = _loss_bwd(h4, tgt, w["final_norm"])

    dh, d_fn1, d_ffn1 = _ffn_bwd(dh, ffn1, fn_[1:2], *ffw[1], "ffn1")
    dog1 = _matmul(dh, h_wo, tb=True, out_dtype=BF16, name="hgrn_dog")
    d_h_wo = _matmul(og1, dh, ta=True, name="hgrn_dwo")
    dpq, dpz, dpi, dpg, d_lb, d_gg = _hgrn_bwd(proj, dog1, states, w["hgrn_lower_bounds"], w["hgrn_g_norm"],
                                               rmat, lmasks)
    dproj = jnp.concatenate([dpq, dpz, dpi, dpg], axis=1)
    d_h_wi = _matmul(hn2, dproj, ta=True, name="hgrn_dwi")
    dh, d_an1, _ = _dhn_norm([dproj], [h_wi], h2, dh, an[1:2], "hgrn_dhn_norm")

    dh, d_fn0, d_ffn0 = _ffn_bwd(dh, ffn0, fn_[0:1], *ffw[0], "ffn0")
    n_in, n_out = 2 * FFN // 4, FFN // 4
    d_ffn = [d_ffn0, d_ffn1]
    late_grads = dict(
        hgrn_w_in=_to_shards("hgrn_w_in", d_h_wi[None]), hgrn_w_out=d_h_wo.reshape(4, D // 4, D),
        ffn_w_in=jnp.stack([jnp.concatenate([d[j // 2][:, (j % 2) * n_in:(j % 2 + 1) * n_in] for d in d_ffn], axis=0)
                            for j in range(4)]),
        ffn_w_out=jnp.stack([jnp.concatenate([d[2][j * n_out:(j + 1) * n_out] for d in d_ffn], axis=0)
                             for j in range(4)]))
    pair_late, send_late = _pair_sums([late_grads[n] for n in LATE_NAMES], "late")

    dog = _matmul(dh, f_wo, tb=True, out_dtype=BF16, name="fox_dog")
    d_f_wo = _matmul(og, dh, ta=True, name="fox_dwo")

    def by_head(a):
        return jnp.pad(a.transpose(1, 0, 2).reshape(L, FOX_H), ((0, 0), (0, LANES - FOX_H)))

    qb = _fox_pack_bias(qp, cq, by_head(lse2.transpose(0, 2, 1)))
    dop, dgate = _fox_pack_bwd(dog, o, gate)
    dqp, dk, dv, dck, recv_late = _fox_attn_bwd(qb, kp, vp, dop, send_late)
    (dq_raw, dk_raw, dc_q), (d_qg, d_kg) = _fox_unpack_bwd(q_raw, k_raw, dqp, dk, qg, kg, gmat)
    dflog, d_bf = _fox_cumsum_bwd(dc_q, by_head(dck), flog, bf)
    dproj0 = jnp.concatenate([dq_raw, dk_raw, dv, dgate, dflog.astype(BF16)], axis=1)
    f_wall = jnp.concatenate([f_wq, f_wk, f_wv, f_wg, f_wf], axis=1)
    d_f_wall = _matmul(hn0, dproj0, ta=True, name="fox_dwi")
    d_f_wi = d_f_wall[:, :4 * D + FOX_H]
    fox_grads = dict(fox_w_in=d_f_wi[None], fox_w_out=d_f_wo[None])
    pair_fox, send_fox = _pair_sums([_to_shards(n, fox_grads[n]) for n in FOX_NAMES], "fox")
    dh, d_an0, recv_fox = _dhn_norm([dproj0], [f_wall], h0, dh, an[0:1], "fox_dhn_norm", slabs=send_fox)
    halves = _chip_sums(pair_fox, recv_fox, "fox") + _chip_sums(pair_late, recv_late, "late")
    theirs = _sibling_exchange(halves)
    big = {n: (m, t) for n, m, t in zip(FOX_NAMES + LATE_NAMES, halves, theirs)}
    small = dict(attn_norm=jnp.concatenate([d_an0, d_an1]), ffn_norm=jnp.concatenate([d_fn0, d_fn1]),
                 final_norm=d_final, lb_raw=d_lb, q_gain=d_qg, k_gain=d_kg, b_f=d_bf,
                 g_gain=d_gg.reshape(1, D))
    return loss, dh, big, small


def _me():
    return lax.axis_index("x"), lax.axis_index("y"), lax.axis_index("c")


def _flip(v, bit):
    return 1 - v if bit else v


def _chip_allgather_split(big, small):
    half = big.shape[0] // 2

    def body(big_in, small_in, big_out, small_out, ssem, rsem, fs_sem, fr_sem, ssem2, rsem2, lsem):
        x, y, c = _me()
        sib = (x, y, 1 - c)
        peers = _chip_peers()
        mine, other = pl.ds(c * half, half), pl.ds((1 - c) * half, half)
        local = [pltpu.make_async_copy(big_in, big_out.at[2 * x + y], lsem.at[0]),
                 pltpu.make_async_copy(small_in, small_out.at[2 * x + y], lsem.at[1])]
        sends = []
        for k, peer in enumerate(peers):
            sends.append(pltpu.make_async_remote_copy(big_in.at[mine], big_out.at[2 * x + y, mine], ssem.at[k],
                                                      rsem.at[k], device_id=peer, device_id_type=MESH))
            sends.append(pltpu.make_async_remote_copy(small_in, small_out.at[2 * x + y], ssem2.at[k], rsem2.at[k],
                                                      device_id=peer, device_id_type=MESH))
        for cp in local + sends:
            cp.start()
        forwards = []
        for k, peer in enumerate(peers):
            landed = big_out.at[2 * peer[0] + peer[1], mine]
            pltpu.make_async_remote_copy(big_in.at[mine], landed, ssem.at[k], rsem.at[k],
                                         device_id=peer, device_id_type=MESH).wait_recv()
            fwd = pltpu.make_async_remote_copy(landed, landed, fs_sem.at[k], fr_sem.at[k],
                                               device_id=sib, device_id_type=MESH)
            fwd.start()
            forwards.append(fwd)
        for k, peer in enumerate(peers):
            theirs = big_out.at[2 * peer[0] + peer[1], other]
            pltpu.make_async_remote_copy(theirs, theirs, fs_sem.at[k], fr_sem.at[k],
                                         device_id=sib, device_id_type=MESH).wait_recv()
            pltpu.make_async_remote_copy(small_in, small_out.at[2 * peer[0] + peer[1]], ssem2.at[k], rsem2.at[k],
                                         device_id=peer, device_id_type=MESH).wait_recv()
        for cp in sends + forwards:
            cp.wait_send()
        for cp in local:
            cp.wait()

    three = pltpu.SemaphoreType.DMA((3,))
    return pl.pallas_call(
        body, name="chip_allgather", in_specs=[ANY, ANY], out_specs=[ANY, ANY],
        out_shape=[jax.ShapeDtypeStruct((4,) + big.shape, big.dtype),
                   jax.ShapeDtypeStruct((4,) + small.shape, small.dtype)],
        scratch_shapes=[three, three, three, three, three, three, pltpu.SemaphoreType.DMA((2,))],
    )(big, small)


def _chip_peers():
    x, y, c = _me()
    return [(1 - x, y, c), (x, 1 - y, c), (1 - x, 1 - y, c)]


def _gather_sems(n):
    return [pltpu.SemaphoreType.DMA((n, 3)), pltpu.SemaphoreType.DMA((n, 3)), pltpu.SemaphoreType.DMA((n,))]


def _gather_copies(ins, outs, ssem, rsem, lsem, with_recvs):
    x, y, _ = _me()
    local, sends, recvs = [], [], []
    for a in range(len(ins)):
        local.append(pltpu.make_async_copy(ins[a], outs[a].at[2 * x + y], lsem.at[a]))
        for k, peer in enumerate(_chip_peers()):
            sends.append(pltpu.make_async_remote_copy(ins[a], outs[a].at[2 * x + y], ssem.at[a, k], rsem.at[a, k],
                                                      device_id=peer, device_id_type=MESH))
            if with_recvs:
                recvs.append(pltpu.make_async_remote_copy(ins[a], outs[a].at[2 * peer[0] + peer[1]], ssem.at[a, k],
                                                          rsem.at[a, k], device_id=peer, device_id_type=MESH))
    return local, sends, recvs


def _gather_start(ins, outs, ssem, rsem, lsem):
    local, sends, _ = _gather_copies(ins, outs, ssem, rsem, lsem, False)
    for cp in local + sends:
        cp.start()


def _gather_wait(ins, outs, ssem, rsem, lsem):
    local, sends, recvs = _gather_copies(ins, outs, ssem, rsem, lsem, True)
    for cp in local:
        cp.wait()
    for cp in sends:
        cp.wait_send()
    for cp in recvs:
        cp.wait_recv()


def _scatter_copies(ins, outs, ssem, rsem):
    copies = []
    for a in range(len(ins)):
        for k, peer in enumerate(_chip_peers()):
            copies.append(pltpu.make_async_remote_copy(ins[a].at[2 * peer[0] + peer[1]], outs[a].at[k], ssem.at[a, k],
                                                       rsem.at[a, k], device_id=peer, device_id_type=MESH))
    return copies


def _device_allgather(arr):
    def body(in_ref, out_ref, ssem, rsem, lsem):
        x, y, c = _me()
        me = 4 * x + 2 * y + c
        peers = [(_flip(x, k & 4), _flip(y, k & 2), _flip(c, k & 1)) for k in range(1, 8)]
        local = pltpu.make_async_copy(in_ref, out_ref.at[me], lsem)
        local.start()
        sends = []
        for k, peer in enumerate(peers):
            cp = pltpu.make_async_remote_copy(in_ref, out_ref.at[me], ssem.at[k], rsem.at[k],
                                              device_id=peer, device_id_type=MESH)
            cp.start()
            sends.append(cp)
        local.wait()
        for cp in sends:
            cp.wait_send()
        for k, peer in enumerate(peers):
            pltpu.make_async_remote_copy(in_ref, out_ref.at[4 * peer[0] + 2 * peer[1] + peer[2]], ssem.at[k],
                                         rsem.at[k], device_id=peer, device_id_type=MESH).wait_recv()

    return pl.pallas_call(
        body, name="device_allgather", in_specs=[ANY], out_specs=ANY,
        out_shape=jax.ShapeDtypeStruct((8,) + arr.shape, arr.dtype),
        scratch_shapes=[pltpu.SemaphoreType.DMA((7,)), pltpu.SemaphoreType.DMA((7,)), pltpu.SemaphoreType.DMA],
    )(arr)


def _sibling_send_other_half(arrs, tag):
    n = len(arrs)

    def body(*refs):
        ins, outs = refs[:n], refs[n:2 * n]
        ssem, rsem = refs[2 * n:]
        x, y, c = _me()
        cps = []
        for a in range(n):
            half = ins[a].shape[1] // 2
            src = ins[a].at[:, pl.ds((1 - c) * half, half), :]
            cp = pltpu.make_async_remote_copy(src, outs[a], ssem.at[a], rsem.at[a],
                                              device_id=(x, y, 1 - c), device_id_type=MESH)
            cp.start()
            cps.append(cp)
        for cp in cps:
            cp.wait()

    return pl.pallas_call(
        body, name=f"grad_sibling_swap_{tag}", in_specs=[ANY] * n, out_specs=[ANY] * n,
        out_shape=[jax.ShapeDtypeStruct((4, a.shape[1] // 2, a.shape[2]), a.dtype) for a in arrs],
        scratch_shapes=[pltpu.SemaphoreType.DMA((n,)), pltpu.SemaphoreType.DMA((n,))],
    )(*arrs)


def _sibling_exchange(arrs):
    n = len(arrs)

    def body(*refs):
        ins, outs = refs[:n], refs[n:2 * n]
        ssem, rsem = refs[2 * n:]
        x, y, c = _me()
        cps = [pltpu.make_async_remote_copy(ins[a], outs[a], ssem.at[a], rsem.at[a], device_id=(x, y, 1 - c),
                                            device_id_type=MESH) for a in range(n)]
        for cp in cps:
            cp.start()
        for cp in cps:
            cp.wait()

    return pl.pallas_call(
        body, name="grad_sibling_exchange", in_specs=[ANY] * n, out_specs=[ANY] * n,
        out_shape=[jax.ShapeDtypeStruct(a.shape, a.dtype) for a in arrs],
        scratch_shapes=[pltpu.SemaphoreType.DMA((n,)), pltpu.SemaphoreType.DMA((n,))],
    )(*arrs)


def _pair_sums(grads, tag):
    got = _sibling_send_other_half(grads, tag)
    res = [_pair_add(g, t, f"grad_pair_add_{tag}{i}") for i, (g, t) in enumerate(zip(grads, got))]
    return [r[0] for r in res], [r[1] for r in res]


def _mesh_scalar(v):
    return jnp.asarray(v, jnp.int32).reshape(1)


def _pair_add(g, t, name):
    _, rows, cols = g.shape
    half = rows // 2
    tm = _tile(half, cap=(2 * 1024 * 1024) // (4 * cols))

    def body(c_ref, g_ref, t_ref, o_ref, ob_ref):
        s = g_ref[0, 0] + t_ref[0]
        o_ref[0] = s
        ob_ref[0] = s.astype(ob_ref.dtype)

    spec = pl.BlockSpec((1, tm, cols), lambda j, i, c: (j, i, 0))
    return pl.pallas_call(
        body, name=name,
        grid_spec=pltpu.PrefetchScalarGridSpec(
            num_scalar_prefetch=1, grid=(4, half // tm),
            in_specs=[pl.BlockSpec((1, 1, tm, cols), lambda j, i, c: (j, c[0], i, 0)), spec],
            out_specs=[spec, spec]),
        out_shape=[jax.ShapeDtypeStruct(t.shape, F32), jax.ShapeDtypeStruct(t.shape, BF16)],
        compiler_params=_cparams(("parallel", "parallel")),
    )(_mesh_scalar(lax.axis_index("c")), g.reshape(4, 2, half, cols), t)


def _chip_sums(pair, recv, tag):
    x, y, _ = _me()
    out = []
    for n, (p, r) in enumerate(zip(pair, recv)):
        _, half, cols = p.shape
        tm = _tile(half, cap=(2 * 1024 * 1024) // (4 * cols))

        def body(j_ref, p_ref, r_ref, o_ref):
            o_ref[...] = p_ref[0] + r_ref[0].astype(F32) + r_ref[1].astype(F32) + r_ref[2].astype(F32)

        out.append(pl.pallas_call(
            body, name=f"grad_chip_add_{tag}{n}",
            grid_spec=pltpu.PrefetchScalarGridSpec(
                num_scalar_prefetch=1, grid=(half // tm,),
                in_specs=[pl.BlockSpec((1, tm, cols), lambda i, j: (j[0], i, 0)),
                          pl.BlockSpec((3, tm, cols), lambda i, j: (0, i, 0))],
                out_specs=pl.BlockSpec((tm, cols), lambda i, j: (i, 0))),
            out_shape=jax.ShapeDtypeStruct((half, cols), F32),
            compiler_params=_cparams(("parallel",)),
        )(_mesh_scalar(2 * x + y), p, r))
    return out


SMALL_ROWS = 32


def _small_finalize(gathered, hlb, fold64, fold128):
    def body(g_ref, hlb_ref, f64_ref, f128_ref, rows_ref, qk_ref, gg_ref, lb_ref):
        tot = g_ref[0]
        for d in range(1, 8):
            tot = tot + g_ref[d]
        rows_ref[...] = tot
        qk_ref[...] = jnp.dot(rows_ref[6:8, :], f64_ref[...], precision=HIGHEST, preferred_element_type=F32)
        gg_ref[...] = jnp.dot(rows_ref[9:10, :], f128_ref[...], precision=HIGHEST, preferred_element_type=F32)
        h0, h1 = hlb_ref[0:1, :], hlb_ref[1:2, :]
        mx = jnp.maximum(h0, h1)
        e0, e1 = jnp.exp(h0 - mx), jnp.exp(h1 - mx)
        lb = e1 / (e0 + e1)
        d1 = rows_ref[5:6, :] * lb * (1.0 - lb)
        lb_ref[...] = jnp.where(_iota((2, 1), 0) == 0, -d1, d1)

    return pl.pallas_call(
        body, name="small_finalize",
        out_shape=[jax.ShapeDtypeStruct((SMALL_ROWS, D), F32), jax.ShapeDtypeStruct((2, FOX_DH), F32),
                   jax.ShapeDtypeStruct((1, HG_D), F32), jax.ShapeDtypeStruct((2, D), F32)],
    )(gathered, hlb, fold64, fold128)


FOX_NAMES = ("fox_w_in", "fox_w_out")
LATE_NAMES = ("hgrn_w_in", "hgrn_w_out", "ffn_w_in", "ffn_w_out")
BIG_NAMES = FOX_NAMES + LATE_NAMES
GATHER_LATE = ("fox_w_out",) + LATE_NAMES
COL_SHARDED = ("fox_w_in", "hgrn_w_in", "ffn_w_in")


def _shard2d(name, a):
    return a.reshape(-1, a.shape[-1])


def _to_shards(name, g):
    layers = g.shape[0]
    if name in COL_SHARDED:
        k, n = g.shape[1], g.shape[2] // 4
        return g.reshape(layers, k, 4, n).transpose(2, 0, 1, 3).reshape(4, layers * k, n)
    r = g.shape[1] // 4
    return g.reshape(layers, 4, r, g.shape[2]).transpose(1, 0, 2, 3).reshape(4, layers * r, g.shape[2])


def kernel(x, meta_tokens, attn_norm, ffn_norm, final_norm, fox_w_in, fox_b_f, fox_q_norm, fox_k_norm, fox_w_out, hgrn_w_in, hgrn_lower_bounds, hgrn_g_norm, hgrn_w_out, ffn_w_in, ffn_w_out, loss_target, m_meta_tokens, m_attn_norm, m_ffn_norm, m_final_norm, m_fox_w_in, m_fox_b_f, m_fox_q_norm, m_fox_k_norm, m_fox_w_out, m_hgrn_w_in, m_hgrn_lower_bounds, m_hgrn_g_norm, m_hgrn_w_out, m_ffn_w_in, m_ffn_w_out, v_meta_tokens, v_attn_norm, v_ffn_norm, v_final_norm, v_fox_w_in, v_fox_b_f, v_fox_q_norm, v_fox_k_norm, v_fox_w_out, v_hgrn_w_in, v_hgrn_lower_bounds, v_hgrn_g_norm, v_hgrn_w_out, v_ffn_w_in, v_ffn_w_out):
    params = dict(meta_tokens=meta_tokens, attn_norm=attn_norm, ffn_norm=ffn_norm, final_norm=final_norm,
                  fox_w_in=fox_w_in, fox_b_f=fox_b_f, fox_q_norm=fox_q_norm, fox_k_norm=fox_k_norm,
                  fox_w_out=fox_w_out, hgrn_w_in=hgrn_w_in, hgrn_lower_bounds=hgrn_lower_bounds,
                  hgrn_g_norm=hgrn_g_norm, hgrn_w_out=hgrn_w_out, ffn_w_in=ffn_w_in, ffn_w_out=ffn_w_out)
    mom_m = dict(meta_tokens=m_meta_tokens, attn_norm=m_attn_norm, ffn_norm=m_ffn_norm, final_norm=m_final_norm,
                 fox_w_in=m_fox_w_in, fox_b_f=m_fox_b_f, fox_q_norm=m_fox_q_norm, fox_k_norm=m_fox_k_norm,
                 fox_w_out=m_fox_w_out, hgrn_w_in=m_hgrn_w_in, hgrn_lower_bounds=m_hgrn_lower_bounds,
                 hgrn_g_norm=m_hgrn_g_norm, hgrn_w_out=m_hgrn_w_out, ffn_w_in=m_ffn_w_in, ffn_w_out=m_ffn_w_out)
    mom_v = dict(meta_tokens=v_meta_tokens, attn_norm=v_attn_norm, ffn_norm=v_ffn_norm, final_norm=v_final_norm,
                 fox_w_in=v_fox_w_in, fox_b_f=v_fox_b_f, fox_q_norm=v_fox_q_norm, fox_k_norm=v_fox_k_norm,
                 fox_w_out=v_fox_w_out, hgrn_w_in=v_hgrn_w_in, hgrn_lower_bounds=v_hgrn_lower_bounds,
                 hgrn_g_norm=v_hgrn_g_norm, hgrn_w_out=v_hgrn_w_out, ffn_w_in=v_ffn_w_in, ffn_w_out=v_ffn_w_out)
    names = list(params)
    xi, yi, _ = _me()

    shards = {n: _shard2d(n, params[n]).astype(BF16) for n in BIG_NAMES}
    w_in_g, meta_g = _chip_allgather_split(shards["fox_w_in"], meta_tokens)
    w = dict(fox_w_in=jnp.concatenate(list(w_in_g), axis=1))
    meta_full = jnp.concatenate(list(meta_g), axis=1)
    w.update(attn_norm=attn_norm, ffn_norm=ffn_norm, final_norm=final_norm.reshape(1, D), fox_b_f=fox_b_f,
             fox_q_norm=fox_q_norm, fox_k_norm=fox_k_norm, hgrn_lower_bounds=hgrn_lower_bounds,
             hgrn_g_norm=hgrn_g_norm)

    h0 = jnp.concatenate([jnp.zeros((ROW0, D), F32), meta_full, x[0]], axis=0)
    loss, dh0, big, small = _local_step(h0, loss_target[0], w, {n: shards[n] for n in GATHER_LATE})
    loss = lax.psum(loss, ("x", "y", "c"))
    grad_x = dh0[PAD:][None]
    grads = {}

    rows = jnp.concatenate([small["attn_norm"], small["ffn_norm"], small["final_norm"], small["lb_raw"],
                            small["q_gain"], small["k_gain"],
                            jnp.pad(small["b_f"], ((0, 0), (0, D - LANES))), small["g_gain"],
                            dh0[ROW0:PAD], jnp.zeros((SMALL_ROWS - 10 - N_META, D), F32)], axis=0)
    allrows = _device_allgather(rows)
    fold64 = jnp.asarray(np.tile(np.eye(FOX_DH, dtype=np.float32), (FOX_H, 1)))
    fold128 = jnp.asarray(np.tile(np.eye(HG_D, dtype=np.float32), (HG_H, 1)))
    tot, qk, gg, dlb = _small_finalize(allrows, hgrn_lower_bounds, fold64, fold128)
    grads.update(attn_norm=tot[0:2], ffn_norm=tot[2:4], final_norm=tot[4], hgrn_lower_bounds=dlb,
                 fox_q_norm=qk[0:1], fox_k_norm=qk[1:2], fox_b_f=tot[8:9, :FOX_H], hgrn_g_norm=gg,
                 meta_tokens=lax.dynamic_slice_in_dim(tot[10:10 + N_META], (2 * xi + yi) * (D // 4), D // 4, axis=1))

    delta, new_m, new_v = {}, {}, {}
    for n in BIG_NAMES:
        res = _adamw_halves(_shard2d(n, params[n]), *big[n], _shard2d(n, mom_m[n]), _shard2d(n, mom_v[n]),
                            f"adamw_{n}")
        grads[n], delta[n], new_m[n], new_v[n] = (t.reshape(params[n].shape) for t in res)
    delta["meta_tokens"], new_m["meta_tokens"], new_v["meta_tokens"] = _adamw(
        meta_tokens, grads["meta_tokens"], m_meta_tokens, v_meta_tokens, "adamw_meta_tokens")
    small_names = [n for n in names if n not in BIG_NAMES and n != "meta_tokens"]

    def pack(d):
        return jnp.concatenate([jnp.pad(d[n].reshape(-1, d[n].shape[-1]), ((0, 0), (0, D - d[n].shape[-1])))
                                for n in small_names], axis=0)

    packed = [pack(t) for t in (params, grads, mom_m, mom_v)]
    n_rows = packed[0].shape[0]
    packed = [jnp.pad(t, ((0, 16 - n_rows), (0, 0))) for t in packed]
    res = _adamw(*packed, "adamw_small")
    r0 = 0
    for n in small_names:
        nr = params[n].reshape(-1, params[n].shape[-1]).shape[0]
        for dst, src in zip((delta, new_m, new_v), res):
            dst[n] = src[r0:r0 + nr, :params[n].shape[-1]].reshape(params[n].shape)
        r0 += nr

    return (loss, grad_x, *[grads[n] for n in names], *[delta[n] for n in names],
            *[new_m[n] for n in names], *[new_v[n] for n in names])
```

```python
import functools

import numpy as np
import jax
import jax.numpy as jnp
from jax import lax
from jax.experimental import pallas as pl
from jax.experimental.pallas import tpu as pltpu

F32, BF16 = jnp.float32, jnp.bfloat16
HIGHEST = lax.Precision.HIGHEST

D = 1024
N_META = 16
PAD = 128
ROW0 = PAD - N_META
FOX_H, FOX_DH = 16, 64
HG_H, HG_D = 8, 128
HG_C = 128
HG_LEV = 7
HG_HPS = 8
HG_SKEW = 0
FFN = 2816
EPS = 1e-6
BIG = 1e30
LOG2E = 1.4426950408889634
LANES = 128
MXU_N = 256
VMEM_LIMIT = 48 * 1024 * 1024
ROW_TILES = (640, 512, 384, 320, 256, 128, 64, 32, 16, 8)
ATTN_TILES = (640, 512, 256, 128)
FOX_HPS_FWD = 8
FOX_HPS_BWD = 4

ADAM_LR, ADAM_B1, ADAM_B2, ADAM_EPS, ADAM_WD, ADAM_STEP = 0.001, 0.9, 0.999, 1e-08, 0.01, 10

MESH = pl.DeviceIdType.MESH
ANY = pl.BlockSpec(memory_space=pl.ANY)
NT = (((1,), (1,)), ((), ()))
TN = (((0,), (0,)), ((), ()))


def _tile(n, cands=ROW_TILES, cap=None):
    for c in cands:
        if n % c == 0 and (cap is None or c <= cap):
            return c
    return n


def _cparams(sem):
    return pltpu.CompilerParams(dimension_semantics=sem, vmem_limit_bytes=VMEM_LIMIT)


def _sigmoid(x):
    return jax.nn.sigmoid(x)


def _log_sigmoid(x):
    return jnp.minimum(x, 0.0) - jnp.log(1.0 + jnp.exp(-jnp.abs(x)))


def _iota(shape, dim):
    return lax.broadcasted_iota(jnp.int32, shape, dim)


def _matmul(a, b, *, ta=False, tb=False, out_dtype=F32, add=None, name):
    if ta:
        kdim, m = a.shape
    else:
        m, kdim = a.shape
    n = b.shape[0] if tb else b.shape[1]
    if ta:
        tm = m if m <= 1024 else _tile(m, (1408, 1024, 512, 256, 128))
        tk = _tile(kdim, (1664,) + ROW_TILES)
    else:
        tm = _tile(m)
        tk = kdim if kdim <= 4096 else _tile(kdim, (2048, 1024, 512))
    tn = n if n <= 1024 else _tile(n, (1408, 1024, 512, 256, 128))
    nk = kdim // tk
    dn = (((0 if ta else 1,), (1 if tb else 0,)), ((), ()))

    def body(*refs):
        if add is None:
            a_ref, b_ref, o_ref, acc_ref = refs
        else:
            a_ref, b_ref, add_ref, o_ref, acc_ref = refs
        k = pl.program_id(2)

        @pl.when(k == 0)
        def _():
            acc_ref[...] = jnp.zeros_like(acc_ref)

        acc_ref[...] += lax.dot_general(a_ref[...].astype(BF16), b_ref[...].astype(BF16), dn,
                                        preferred_element_type=F32)

        @pl.when(k == nk - 1)
        def _():
            r = acc_ref[...]
            if add is not None:
                r = r + add_ref[...].astype(F32)
            o_ref[...] = r.astype(o_ref.dtype)

    a_spec = pl.BlockSpec((tk, tm), lambda j, i, k: (k, i)) if ta else pl.BlockSpec((tm, tk), lambda j, i, k: (i, k))
    b_spec = pl.BlockSpec((tn, tk), lambda j, i, k: (j, k)) if tb else pl.BlockSpec((tk, tn), lambda j, i, k: (k, j))
    o_spec = pl.BlockSpec((tm, tn), lambda j, i, k: (i, j))
    ins, specs = [a, b], [a_spec, b_spec]
    if add is not None:
        ins.append(add)
        specs.append(o_spec)
    return pl.pallas_call(
        body, name=name, grid=(n // tn, m // tm, nk), in_specs=specs, out_specs=o_spec,
        out_shape=jax.ShapeDtypeStruct((m, n), out_dtype),
        scratch_shapes=[pltpu.VMEM((tm, tn), F32)],
        compiler_params=_cparams(("parallel", "parallel", "arbitrary")),
    )(*ins)


def _rowwise(fn, ins, bcast, outs, accs, *, name, reverse=False, carry=None, as_refs=False):
    rows = ins[0].shape[0]
    per_row = sum(x.shape[1] * x.dtype.itemsize for x in ins) + sum(c * jnp.dtype(d).itemsize for c, d in outs)
    tm = _tile(rows, cap=max(8, (10 * 1024 * 1024) // per_row))
    n = rows // tm
    n_in, n_b, n_o, n_a = len(ins), len(bcast), len(outs), len(accs)

    def body(*refs):
        in_refs = refs[:n_in]
        b_refs = refs[n_in:n_in + n_b]
        o_refs = refs[n_in + n_b:n_in + n_b + n_o]
        a_refs = refs[n_in + n_b + n_o:n_in + n_b + n_o + n_a]
        c_refs = refs[n_in + n_b + n_o + n_a:]
        i = pl.program_id(0)
        blk = (n - 1 - i) if reverse else i
        if c_refs:
            @pl.when(i == 0)
            def _():
                c_refs[0][...] = jnp.zeros_like(c_refs[0])
        args = (list(in_refs) if as_refs else [r[...] for r in in_refs], [r[...] for r in b_refs])
        o_vals, a_vals = fn(blk * tm, *args, *c_refs)
        for r, v in zip(o_refs, o_vals):
            r[...] = v.astype(r.dtype)
        if n_a:
            @pl.when(i == 0)
            def _():
                for r in a_refs:
                    r[...] = jnp.zeros_like(r)
            for r, v in zip(a_refs, a_vals):
                r[...] += v

    def row_map(i):
        return ((n - 1 - i) if reverse else i, 0)

    in_specs = [pl.BlockSpec((tm, x.shape[1]), row_map) for x in ins]
    in_specs += [pl.BlockSpec(x.shape, lambda i, nd=x.ndim: (0,) * nd) for x in bcast]
    out_specs = [pl.BlockSpec((tm, c), row_map) for c, _ in outs]
    out_specs += [pl.BlockSpec(s, lambda i: (0, 0)) for s in accs]
    out_shape = [jax.ShapeDtypeStruct((rows, c), d) for c, d in outs]
    out_shape += [jax.ShapeDtypeStruct(s, F32) for s in accs]
    res = pl.pallas_call(
        body, name=name, grid=(n,), in_specs=in_specs, out_specs=out_specs, out_shape=out_shape,
        scratch_shapes=[pltpu.VMEM(carry, F32)] if carry else [],
        compiler_params=_cparams(("arbitrary",)),
    )(*ins, *bcast)
    return res[:n_o], res[n_o:]


def _row_ids(row0, tm):
    return row0 + _iota((tm, 1), 0)


def _rms_bwd_math(xv, dy, g):
    r = lax.rsqrt(jnp.mean(xv * xv, axis=-1, keepdims=True) + EPS)
    xh = xv * r
    dxh = dy * g
    dx = r * (dxh - xh * jnp.mean(dxh * xh, axis=-1, keepdims=True))
    return dx, jnp.sum(dy * xh, axis=0, keepdims=True)


def _loss_bwd(h, tgt, gain):
    tm = PAD
    n = h.shape[0] // tm

    def body(x_ref, t_ref, g_ref, dx_ref, loss_ref, dg_ref):
        i = pl.program_id(0)

        @pl.when(i == 0)
        def _():
            loss_ref[...] = jnp.zeros_like(loss_ref)
            dg_ref[...] = jnp.zeros_like(dg_ref)

        xv, g = x_ref[...], g_ref[...]
        r = lax.rsqrt(jnp.mean(xv * xv, axis=-1, keepdims=True) + EPS)
        xh = xv * r
        err = jnp.where(i >= 1, xh * g - t_ref[...], 0.0)
        per_row = jnp.mean(err * err, axis=-1, keepdims=True)
        loss_ref[...] += jnp.broadcast_to(0.5 * jnp.sum(per_row, axis=0, keepdims=True), (1, LANES))
        dy = err * (1.0 / D)
        dxh = dy * g
        dx_ref[...] = r * (dxh - xh * jnp.mean(dxh * xh, axis=-1, keepdims=True))
        dg_ref[...] += jnp.sum(dy * xh, axis=0, keepdims=True)

    dh, loss, dgain = pl.pallas_call(
        body, name="loss_bwd", grid=(n,),
        in_specs=[pl.BlockSpec((tm, D), lambda i: (i, 0)), pl.BlockSpec((tm, D), lambda i: (jnp.maximum(i - 1, 0), 0)),
                  pl.BlockSpec((1, D), lambda i: (0, 0))],
        out_specs=[pl.BlockSpec((tm, D), lambda i: (i, 0)), pl.BlockSpec((1, LANES), lambda i: (0, 0)),
                   pl.BlockSpec((1, D), lambda i: (0, 0))],
        out_shape=[jax.ShapeDtypeStruct(h.shape, F32), jax.ShapeDtypeStruct((1, LANES), F32),
                   jax.ShapeDtypeStruct((1, D), F32)],
        compiler_params=_cparams(("arbitrary",)),
    )(h, tgt, gain)
    return loss[0, 0], dh, dgain


FFN_TILES = dict(rows=(320, 256, 128), cols=(1408, 1024, 512, 256, 128))


def _norm_proj(h, gain, ws, dtypes, name, swiglu=False):
    m = h.shape[0]
    tm = _tile(m, FFN_TILES["rows"])
    n = len(ws)

    def body(h_ref, g_ref, *refs):
        w_refs, hn_ref, o_refs = refs[:n], refs[n], refs[n + 1:]
        x = h_ref[...]
        hn = (x * lax.rsqrt(jnp.mean(x * x, axis=-1, keepdims=True) + EPS) * g_ref[...]).astype(BF16)
        hn_ref[...] = hn
        prods = [jnp.dot(hn, w[...], preferred_element_type=F32) for w in w_refs]
        for o, p in zip(o_refs, prods):
            o[...] = p.astype(o.dtype)
        if swiglu:
            o_refs[n][...] = (prods[0] * _sigmoid(prods[0]) * prods[1]).astype(o_refs[n].dtype)

    rows = lambda c: pl.BlockSpec((tm, c), lambda i: (i, 0))
    cols = [w.shape[1] for w in ws] + ([ws[0].shape[1]] if swiglu else [])
    dts = list(dtypes) + ([BF16] if swiglu else [])
    return pl.pallas_call(
        body, name=name, grid=(m // tm,),
        in_specs=[rows(D), pl.BlockSpec((1, D), lambda i: (0, 0))]
        + [pl.BlockSpec(w.shape, lambda i: (0, 0), pipeline_mode=pl.Buffered(1)) for w in ws],
        out_specs=[rows(D)] + [rows(c) for c in cols],
        out_shape=[jax.ShapeDtypeStruct((m, D), BF16)] + [jax.ShapeDtypeStruct((m, c), d) for c, d in zip(cols, dts)],
        compiler_params=_cparams(("parallel",)),
    )(h, gain, *ws)


def _dhn_norm(dys, ws, h, dh_up, gain, name, slabs=()):
    m = h.shape[0]
    tm = _tile(m, FFN_TILES["rows"])
    n, ns = len(dys), len(slabs)
    steps = m // tm

    def body(*refs):
        dy_refs, w_refs = refs[:n], refs[n:2 * n]
        h_ref, up_ref, g_ref = refs[2 * n:2 * n + 3]
        sl_in = refs[2 * n + 3:2 * n + 3 + ns]
        dh_ref, dgain_ref = refs[2 * n + 3 + ns:2 * n + 5 + ns]
        sl_out = refs[2 * n + 5 + ns:2 * n + 5 + 2 * ns]
        sems = refs[2 * n + 5 + 2 * ns:]
        i = pl.program_id(0)

        @pl.when(i == 0)
        def _():
            dgain_ref[...] = jnp.zeros_like(dgain_ref)
            for cp in _scatter_copies(sl_in, sl_out, *sems) if ns else ():
                cp.start()

        dy = sum(lax.dot_general(a[...], w[...], NT, preferred_element_type=F32) for a, w in zip(dy_refs, w_refs))
        dx, dgain = _rms_bwd_math(h_ref[...], dy, g_ref[...])
        keep = _row_ids(i * tm, tm) >= ROW0
        dh_ref[...] = jnp.where(keep, up_ref[...] + dx, 0.0)
        dgain_ref[...] += dgain

        if ns:
            @pl.when(i == steps - 1)
            def _():
                for cp in _scatter_copies(sl_in, sl_out, *sems):
                    cp.wait()

    rows = lambda c: pl.BlockSpec((tm, c), lambda i: (i, 0))
    whole = lambda a: pl.BlockSpec(a.shape, lambda i: (0, 0), pipeline_mode=pl.Buffered(1))
    res = pl.pallas_call(
        body, name=name, grid=(steps,),
        in_specs=[rows(a.shape[1]) for a in dys] + [whole(w) for w in ws]
        + [rows(D), rows(D), pl.BlockSpec((1, D), lambda i: (0, 0))] + [ANY] * ns,
        out_specs=[rows(D), pl.BlockSpec((1, D), lambda i: (0, 0))] + [ANY] * ns,
        out_shape=[jax.ShapeDtypeStruct((m, D), F32), jax.ShapeDtypeStruct((1, D), F32)]
        + [jax.ShapeDtypeStruct((3,) + a.shape[1:], a.dtype) for a in slabs],
        scratch_shapes=[pltpu.SemaphoreType.DMA((ns, 3)), pltpu.SemaphoreType.DMA((ns, 3))] if ns else [],
        compiler_params=_cparams(("arbitrary",)),
    )(*dys, *ws, h, dh_up, gain, *slabs)
    return res[0], res[1], res[2:]


def _ffn_dact(dh, wo, g, u, name):
    m, kdim = dh.shape
    n = wo.shape[0]
    tm, tn = _tile(m, FFN_TILES["rows"]), _tile(n, FFN_TILES["cols"])

    def body(a_ref, w_ref, g_ref, u_ref, dg_ref, du_ref):
        da = lax.dot_general(a_ref[...].astype(BF16), w_ref[...], NT, preferred_element_type=F32)
        gv, uv = g_ref[...].astype(F32), u_ref[...].astype(F32)
        s = _sigmoid(gv)
        dg_ref[...] = (da * uv * (s * (1.0 + gv * (1.0 - s)))).astype(dg_ref.dtype)
        du_ref[...] = (da * gv * s).astype(du_ref.dtype)

    ospec = pl.BlockSpec((tm, tn), lambda j, i: (i, j))
    return pl.pallas_call(
        body, name=name, grid=(n // tn, m // tm),
        in_specs=[pl.BlockSpec((tm, kdim), lambda j, i: (i, 0)), pl.BlockSpec((tn, kdim), lambda j, i: (j, 0)),
                  ospec, ospec],
        out_specs=[ospec] * 2, out_shape=[jax.ShapeDtypeStruct((m, n), BF16)] * 2,
        compiler_params=_cparams(("parallel", "parallel")),
    )(dh, wo, g, u)


def _adamw_math(wv, gv, mv, vv):
    mn = ADAM_B1 * mv + (1.0 - ADAM_B1) * gv
    vn = ADAM_B2 * vv + (1.0 - ADAM_B2) * (gv * gv)
    m_hat = mn / (1.0 - ADAM_B1 ** ADAM_STEP)
    v_hat = vn / (1.0 - ADAM_B2 ** ADAM_STEP)
    return -ADAM_LR * (m_hat / (jnp.sqrt(v_hat) + ADAM_EPS) + ADAM_WD * wv), mn, vn


def _adamw(w, g, m, v, name):
    def fn(row0, ins, bc):
        return list(_adamw_math(*ins)), []
    c = w.shape[1]
    return _rowwise(fn, [w, g, m, v], [], [(c, F32)] * 3, [], name=name)[0]


def _adamw_halves(w, mine, theirs, m, v, name):
    rows, cols = w.shape
    half = rows // 2
    tm = _tile(half, cap=(10 * 1024 * 1024) // (9 * 4 * cols))
    nb = half // tm

    def body(c_ref, w_ref, g1_ref, g2_ref, m_ref, v_ref, g_out, d_out, m_out, v_out):
        own = (pl.program_id(0) // nb) == c_ref[0]
        g = jnp.where(own, g1_ref[...], g2_ref[...])
        delta, mn, vn = _adamw_math(w_ref[...], g, m_ref[...], v_ref[...])
        g_out[...] = g
        d_out[...] = delta
        m_out[...] = mn
        v_out[...] = vn

    full = pl.BlockSpec((tm, cols), lambda i, c: (i, 0))
    part = pl.BlockSpec((tm, cols), lambda i, c: (lax.rem(i, nb), 0))
    return pl.pallas_call(
        body, name=name,
        grid_spec=pltpu.PrefetchScalarGridSpec(num_scalar_prefetch=1, grid=(2 * nb,),
                                               in_specs=[full, part, part, full, full], out_specs=[full] * 4),
        out_shape=[jax.ShapeDtypeStruct((rows, cols), F32)] * 4,
        compiler_params=_cparams(("parallel",)),
    )(_mesh_scalar(lax.axis_index("c")), w, mine, theirs, m, v)


def _head_sum(x, gmat):
    hi = x.astype(BF16)
    lo = (x - hi.astype(F32)).astype(BF16)
    w = gmat.shape[0]
    return jnp.concatenate(
        [jnp.dot(hi[:, b:b + w], gmat, preferred_element_type=F32) + jnp.dot(lo[:, b:b + w], gmat,
                                                                             preferred_element_type=F32)
         for b in range(0, x.shape[1], w)], axis=1)


def _split3(x):
    hi = x.astype(BF16).astype(F32)
    r = x - hi
    mid = r.astype(BF16).astype(F32)
    return hi, mid, r - mid


def _extra_base(hh):
    return FOX_DH * (1 - hh)


def _data_mask(hh):
    lane = _iota((1, LANES), 1)
    return (lane >= FOX_DH * hh) & (lane < FOX_DH * (hh + 1))


def _with_extras(data, hh, vals):
    lane = _iota((1, LANES), 1)
    x = jnp.zeros_like(data)
    for e, v in enumerate(vals):
        x = jnp.where(lane == _extra_base(hh) + e, v, x)
    return jnp.where(_data_mask(hh), data, x)


def _fox_pack_fwd(q_raw, k_raw, v, cq, ck, qg, kg, gmat):
    scale2 = FOX_DH ** -0.5 * LOG2E

    def fn(row0, refs, bc):
        q_ref, k_ref, v_ref, cq_ref, ck_ref = refs
        g_q, g_k, gm = bc
        qv, kv = q_ref[...], k_ref[...]
        qn = qv * lax.rsqrt(_head_sum(qv * qv, gm) * (1.0 / FOX_DH) + EPS) * (g_q * scale2)
        kn = kv * lax.rsqrt(_head_sum(kv * kv, gm) * (1.0 / FOX_DH) + EPS) * g_k
        qs, ks, vs = [], [], []
        for h in range(FOX_H):
            p, hh = divmod(h, 2)
            sl = slice(p * LANES, (p + 1) * LANES)
            cq3 = _split3(cq_ref[:, h:h + 1] * LOG2E)
            ck3 = _split3(ck_ref[:, h:h + 1] * (-LOG2E))
            qs.append(_with_extras(qn[:, sl], hh, [*cq3, 1.0, 1.0, 1.0]))
            ks.append(_with_extras(kn[:, sl], hh, [1.0, 1.0, 1.0, *ck3]))
            vs.append(_with_extras(v_ref[:, sl].astype(F32), hh, [1.0, 1.0]))
        return [jnp.concatenate(qs, axis=1), jnp.concatenate(ks, axis=1), jnp.concatenate(vs, axis=1)], []

    w = FOX_H * LANES
    return _rowwise(fn, [q_raw, k_raw, v, cq, ck], [qg, kg, gmat], [(w, BF16)] * 3, [], name="fox_pack_fwd",
                    as_refs=True)[0]


def _fox_pack_bias(qp, cq, lse2):
    def fn(row0, refs, bc):
        q_ref, cq_ref, lse_ref = refs
        lane = _iota((1, LANES), 1)
        outs = []
        for h in range(FOX_H):
            blk = q_ref[:, h * LANES:(h + 1) * LANES].astype(F32)
            for e, part in enumerate(_split3(cq_ref[:, h:h + 1] * LOG2E - lse_ref[:, h:h + 1])):
                blk = jnp.where(lane == _extra_base(h % 2) + e, part, blk)
            outs.append(blk)
        return [jnp.concatenate(outs, axis=1)], []
    return _rowwise(fn, [qp, cq, lse2], [], [(FOX_H * LANES, BF16)], [], name="fox_pack_bias", as_refs=True)[0][0]


def _fox_pack_bwd(dog, o, gate):
    def fn(row0, refs, bc):
        d_ref, o_ref, g_ref = refs
        dos, dgs = [], []
        for p in range(FOX_H // 2):
            sl = slice(p * LANES, (p + 1) * LANES)
            dv, ov, gv = (r[:, sl].astype(F32) for r in (d_ref, o_ref, g_ref))
            s = _sigmoid(gv)
            do = dv * s
            dgs.append(dv * ov * s * (1.0 - s))
            od = ov * do
            for hh in range(2):
                delta = jnp.sum(jnp.where(_data_mask(hh), od, 0.0), axis=-1, keepdims=True)
                hi = delta.astype(BF16).astype(F32)
                dos.append(_with_extras(do, hh, [-hi, hi - delta]))
        return [jnp.concatenate(dos, axis=1), jnp.concatenate(dgs, axis=1)], []
    return _rowwise(fn, [dog, o, gate], [], [(FOX_H * LANES, BF16), (D, BF16)], [], name="fox_pack_bwd",
                    as_refs=True)[0]


def _fox_unpack_bwd(q_raw, k_raw, dqp, dk, qg, kg, gmat):
    scale = FOX_DH ** -0.5

    def fn(row0, refs, bc):
        q_ref, k_ref, dq_ref, dk_ref = refs
        g_q, g_k, gm = bc
        lane = _iota((1, LANES), 1)
        dqs = []
        dcq = jnp.zeros((q_ref.shape[0], LANES), F32)
        for p in range(FOX_H // 2):
            even = dq_ref[:, (2 * p) * LANES:(2 * p + 1) * LANES]
            odd = dq_ref[:, (2 * p + 1) * LANES:(2 * p + 2) * LANES]
            dqs.append(jnp.where(_data_mask(0), even, odd) * scale)
            for hh in range(2):
                col = (2 * p + hh) * LANES + _extra_base(hh)
                dcq = jnp.where(lane == 2 * p + hh, dq_ref[:, col:col + 1], dcq)
        outs, accs = [], []
        for xv, dy, g in ((q_ref[...], jnp.concatenate(dqs, axis=1), g_q), (k_ref[...], dk_ref[...] * (1.0 / LOG2E), g_k)):
            r = lax.rsqrt(_head_sum(xv * xv, gm) * (1.0 / FOX_DH) + EPS)
            xh = xv * r
            dxh = dy * g
            outs.append(r * (dxh - xh * (_head_sum(dxh * xh, gm) * (1.0 / FOX_DH))))
            accs.append(jnp.sum(dy * xh, axis=0, keepdims=True))
        return outs + [dcq], accs
    return _rowwise(fn, [q_raw, k_raw, dqp, dk], [qg, kg, gmat], [(D, BF16), (D, BF16), (LANES, F32)],
                    [(1, D), (1, D)], name="fox_unpack_bwd", as_refs=True)


def _fox_cumsum_fwd(flog, bf):
    def fn(row0, ins, bc, carry):
        (f,), (b,) = ins, bc
        tm = f.shape[0]
        keep = _row_ids(row0, tm) >= ROW0
        lf = jnp.where(keep, _log_sigmoid(f + b), 0.0)
        tri = (_iota((tm, tm), 0) >= _iota((tm, tm), 1)).astype(F32)
        c = jnp.dot(tri, lf, precision=HIGHEST, preferred_element_type=F32) + carry[...]
        carry[...] = carry[...] + jnp.sum(lf, axis=0, keepdims=True)
        return [c, jnp.where(keep, c, BIG)], []
    return _rowwise(fn, [flog], [bf], [(LANES, F32), (LANES, F32)], [], name="fox_cumsum_fwd",
                    carry=(1, LANES))[0]


def _fox_cumsum_bwd(dc_q, dc_k, flog, bf):
    def fn(row0, ins, bc, carry):
        (dq, dk, f), (b,) = ins, bc
        d = dq + dk
        tm = f.shape[0]
        keep = _row_ids(row0, tm) >= ROW0
        triu = (_iota((tm, tm), 0) <= _iota((tm, tm), 1)).astype(F32)
        dlf = jnp.dot(triu, d, precision=HIGHEST, preferred_element_type=F32) + carry[...]
        carry[...] = carry[...] + jnp.sum(d, axis=0, keepdims=True)
        dfl = jnp.where(keep, dlf * _sigmoid(-(f + b)), 0.0)
        return [dfl], [jnp.sum(dfl, axis=0, keepdims=True)]
    (dflog,), (dbf,) = _rowwise(fn, [dc_q, dc_k, flog], [bf], [(LANES, F32)], [(1, LANES)], name="fox_cumsum_bwd",
                                reverse=True, carry=(1, LANES))
    return dflog, dbf


def _causal_steps(n, key_major):
    if key_major:
        pairs = [(i, j) for j in range(n) for i in range(j, n)]
    else:
        pairs = [(i, j) for i in range(n) for j in range(i + 1)]
    return (jnp.asarray(np.array([p[0] for p in pairs], np.int32)),
            jnp.asarray(np.array([p[1] for p in pairs], np.int32)))


def _fox_attn_fwd(qp, kp, vp, gate, shards):
    L = qp.shape[0]
    t = _tile(L, ATTN_TILES)
    n = L // t
    hps = FOX_HPS_FWD
    P = FOX_H // hps
    it, jt = _causal_steps(n, False)
    n_steps = it.shape[0]
    ns = len(shards)

    def body(it_ref, jt_ref, q_ref, k_ref, v_ref, g_ref, *rest):
        sh_in, (o_ref, og_ref, lse_ref), sh_out = rest[:ns], rest[ns:ns + 3], rest[ns + 3:2 * ns + 3]
        m_sc, acc, ssem, rsem, lsem = rest[2 * ns + 3:]
        step = pl.program_id(1)
        i, j = it_ref[step], jt_ref[step]
        first = (pl.program_id(0) == 0) & (step == 0)
        last = (pl.program_id(0) == P - 1) & (step == n_steps - 1)

        @pl.when(first)
        def _():
            _gather_start(sh_in, sh_out, ssem, rsem, lsem)

        @pl.when(j == 0)
        def _():
            m_sc[...] = jnp.full_like(m_sc, -3.0e38)
            acc[...] = jnp.zeros_like(acc)

        def update(masked):
            def head(hh):
                sl = slice(hh * LANES, (hh + 1) * LANES)
                s2 = lax.dot_general(k_ref[:, sl], q_ref[:, sl], NT, preferred_element_type=F32)
                if masked:
                    s2 = jnp.where(_iota((t, t), 0) <= _iota((t, t), 1), s2, -jnp.inf)
                yield
                m_old = m_sc[hh]
                m_new = jnp.maximum(m_old, jnp.max(s2, axis=0, keepdims=True))
                p = jnp.exp2(s2 - m_new).astype(BF16)
                yield
                acc[hh] = jnp.exp2(m_old - m_new) * acc[hh] + lax.dot_general(v_ref[:, sl], p, TN,
                                                                              preferred_element_type=F32)
                m_sc[hh] = m_new

            _interleave((head(hh) for hh in range(hps)), skew=1)

        @pl.when(j < i)
        def _():
            update(False)

        @pl.when(j == i)
        def _():
            update(True)
            outs = []
            for hh in range(hps):
                base = _extra_base(hh % 2)
                l = acc[hh, base:base + 1, :]
                outs.append((acc[hh] / l).T)
                lse_ref[0, hh:hh + 1, :] = m_sc[hh] + jnp.log2(l)
            o = jnp.concatenate([jnp.where(_data_mask(0), outs[a], outs[a + 1]) for a in range(0, hps, 2)], axis=1)
            o_ref[...] = o.astype(o_ref.dtype)
            og_ref[...] = (o * _sigmoid(g_ref[...].astype(F32))).astype(og_ref.dtype)

        @pl.when(last)
        def _():
            _gather_wait(sh_in, sh_out, ssem, rsem, lsem)

    qspec = pl.BlockSpec((t, hps * LANES), lambda p, s, it, jt: (it[s], p))
    kspec = pl.BlockSpec((t, hps * LANES), lambda p, s, it, jt: (jt[s], p))
    ospec = pl.BlockSpec((t, hps * FOX_DH), lambda p, s, it, jt: (it[s], p))
    lspec = pl.BlockSpec((1, hps, t), lambda p, s, it, jt: (p, 0, it[s]))
    res = pl.pallas_call(
        body, name="fox_attn_fwd",
        grid_spec=pltpu.PrefetchScalarGridSpec(
            num_scalar_prefetch=2, grid=(P, n_steps),
            in_specs=[qspec, kspec, kspec, ospec] + [ANY] * ns, out_specs=[ospec, ospec, lspec] + [ANY] * ns,
            scratch_shapes=[pltpu.VMEM((hps, 1, t), F32), pltpu.VMEM((hps, LANES, t), F32)] + _gather_sems(ns)),
        out_shape=[jax.ShapeDtypeStruct((L, D), BF16), jax.ShapeDtypeStruct((L, D), BF16),
                   jax.ShapeDtypeStruct((P, hps, L), F32)]
        + [jax.ShapeDtypeStruct((4,) + a.shape, a.dtype) for a in shards],
        compiler_params=_cparams(("arbitrary", "arbitrary")),
    )(it, jt, qp, kp, vp, gate, *shards)
    return res[0], res[1], res[2], res[3:]


def _fox_attn_bwd(qb, kp, vp, dop, slabs):
    L = qb.shape[0]
    t = _tile(L, ATTN_TILES)
    n = L // t
    hps = FOX_HPS_BWD
    P = FOX_H // hps
    it, jt = _causal_steps(n, True)
    n_steps = it.shape[0]
    ns = len(slabs)

    def body(it_ref, jt_ref, q_ref, k_ref, v_ref, do_ref, *rest):
        sl_in, (dq_ref, dk_ref, dv_ref, dck_ref), sl_out = rest[:ns], rest[ns:ns + 4], rest[ns + 4:2 * ns + 4]
        dk_acc, dv_acc, ssem, rsem = rest[2 * ns + 4:]
        step = pl.program_id(1)
        i, j = it_ref[step], jt_ref[step]

        @pl.when((pl.program_id(0) == 0) & (step == 0))
        def _():
            for cp in _scatter_copies(sl_in, sl_out, ssem, rsem):
                cp.start()

        @pl.when(step == 0)
        def _():
            dq_ref[...] = jnp.zeros_like(dq_ref)

        @pl.when(i == j)
        def _():
            dk_acc[...] = jnp.zeros_like(dk_acc)
            dv_acc[...] = jnp.zeros_like(dv_acc)

        def update(masked):
            rows = pl.ds(pl.multiple_of(i * t, LANES), t)
            for hh in range(hps):
                sl = slice(hh * LANES, (hh + 1) * LANES)
                q, k, dov = q_ref[:, sl], k_ref[:, sl], do_ref[:, sl]
                s2 = lax.dot_general(k, q, NT, preferred_element_type=F32)
                if masked:
                    s2 = jnp.where(_iota((t, t), 0) <= _iota((t, t), 1), s2, -jnp.inf)
                p = jnp.exp2(s2)
                ds = (p * lax.dot_general(v_ref[:, sl], dov, NT, preferred_element_type=F32)).astype(BF16)
                dv_acc[hh] += jnp.dot(p.astype(BF16), dov, preferred_element_type=F32)
                dk_acc[hh] += jnp.dot(ds, q, preferred_element_type=F32)
                dq_ref[rows, sl] += lax.dot_general(ds, k, TN, preferred_element_type=F32)

        @pl.when(i > j)
        def _():
            update(False)

        @pl.when(i == j)
        def _():
            update(True)

        @pl.when(i == n - 1)
        def _():
            pairs = range(0, hps, 2)
            dk_ref[...] = jnp.concatenate([jnp.where(_data_mask(0), dk_acc[a], dk_acc[a + 1]) for a in pairs], axis=1)
            dv_ref[...] = jnp.concatenate([jnp.where(_data_mask(0), dv_acc[a], dv_acc[a + 1]) for a in pairs],
                                          axis=1).astype(dv_ref.dtype)
            lane = _iota((1, hps), 1)
            col_sums = jnp.zeros((t, hps), F32)
            for hh in range(hps):
                base = _extra_base(hh % 2) + 3
                col_sums = jnp.where(lane == hh, dk_acc[hh, :, base:base + 1], col_sums)
            dck_ref[0] = -col_sums

        @pl.when((pl.program_id(0) == P - 1) & (step == n_steps - 1))
        def _():
            for cp in _scatter_copies(sl_in, sl_out, ssem, rsem):
                cp.wait()

    qspec = pl.BlockSpec((t, hps * LANES), lambda p, s, it, jt: (it[s], p))
    kspec = pl.BlockSpec((t, hps * LANES), lambda p, s, it, jt: (jt[s], p))
    ospec = pl.BlockSpec((t, hps * FOX_DH), lambda p, s, it, jt: (jt[s], p))
    slab = pl.BlockSpec((L, hps * LANES), lambda p, s, it, jt: (0, p), pipeline_mode=pl.Buffered(1))
    res = pl.pallas_call(
        body, name="fox_attn_bwd",
        grid_spec=pltpu.PrefetchScalarGridSpec(
            num_scalar_prefetch=2, grid=(P, n_steps),
            in_specs=[qspec, kspec, kspec, qspec] + [ANY] * ns,
            out_specs=[slab, ospec, ospec,
                       pl.BlockSpec((1, t, hps), lambda p, s, it, jt: (p, jt[s], 0))] + [ANY] * ns,
            scratch_shapes=[pltpu.VMEM((hps, t, LANES), F32), pltpu.VMEM((hps, t, LANES), F32),
                            pltpu.SemaphoreType.DMA((ns, 3)), pltpu.SemaphoreType.DMA((ns, 3))]),
        out_shape=[jax.ShapeDtypeStruct((L, FOX_H * LANES), F32), jax.ShapeDtypeStruct((L, D), F32),
                   jax.ShapeDtypeStruct((L, D), BF16), jax.ShapeDtypeStruct((P, L, hps), F32)]
        + [jax.ShapeDtypeStruct((3,) + a.shape[1:], a.dtype) for a in slabs],
        compiler_params=_cparams(("arbitrary", "arbitrary")),
    )(it, jt, qb, kp, vp, dop, *slabs)
    return res[0], res[1], res[2], res[3], res[4:]


def _hgrn_consts():
    C = HG_C
    r = np.arange(C)[:, None]
    j = np.arange(C)[None, :]
    mats = [j <= r, j > r]
    masks = []
    n = C
    while n >= 2:
        half = n // 2
        mid = (r // n) * n + half - 1
        second = (r % n) >= half
        mats.append(np.where(second, (j > mid) & (j <= r), (j > r) & (j <= mid)))
        masks.append(((r // n) == (j // n)) & ((r % n) >= half) & ((j % n) < half))
        n //= 2
    return (jnp.asarray(np.concatenate(mats, 0).astype(np.float32), BF16),
            jnp.asarray(np.stack(masks).astype(np.float32), F32))


def _hg_pre(hq, hz, h0, h1):
    mx = jnp.maximum(h0, h1)
    e0, e1 = jnp.exp(h0 - mx), jnp.exp(h1 - mx)
    lb = e1 / (e0 + e1)
    sq = _sigmoid(hq)
    sz = _sigmoid(hz)
    snz = 1.0 - sz
    k = (1.0 - lb) * snz
    g = jnp.maximum(jnp.log(lb + (1.0 - lb) * sz), -BIG)
    return lb, hq * sq, sq, k, sz, snz, g


def _hg_decays(g, rmat):
    hi = g.astype(BF16)
    lo = (g - hi.astype(F32)).astype(BF16)
    d = jnp.dot(rmat, jnp.concatenate([hi, lo], axis=1), preferred_element_type=F32)
    return jnp.exp(d[:, :HG_D] + d[:, HG_D:])


def _interleave(programs, skew=0):
    progs = list(programs)
    done = [False] * len(progs)
    tick = 0
    while not all(done):
        for n, g in enumerate(progs):
            if not done[n] and tick >= n * skew:
                try:
                    next(g)
                except StopIteration:
                    done[n] = True
        tick += 1


def _hg_intra_levels(q, k, fall, masks):
    C = HG_C
    eye = _iota((C, C), 0) == _iota((C, C), 1)
    a = jnp.where(eye, jnp.sum(q * k, axis=-1, keepdims=True), 0.0)
    for l in range(HG_LEV):
        f = fall[(2 + l) * C:(3 + l) * C]
        a = a + masks[l] * lax.dot_general((q * f).astype(BF16), (k * f).astype(BF16), NT,
                                           preferred_element_type=F32)
        yield a


def _hgrn_specs(n_chunks, reverse):
    C = HG_C
    w = HG_HPS * HG_D

    def col(first_head):
        off = first_head // HG_HPS
        if reverse:
            return pl.BlockSpec((C, w), lambda h, c: (n_chunks - 1 - c, off + h))
        return pl.BlockSpec((C, w), lambda h, c: (c, off + h))

    st = pl.BlockSpec((HG_HPS, 1, HG_D, HG_D),
                      (lambda h, c: (h, n_chunks - 1 - c, 0, 0)) if reverse else (lambda h, c: (h, c, 0, 0)))
    consts = [pl.BlockSpec((2, w), lambda h, c: (0, h)), pl.BlockSpec((1, HG_D), lambda h, c: (0, 0)),
              pl.BlockSpec(((2 + HG_LEV) * C, C), lambda h, c: (0, 0)),
              pl.BlockSpec((HG_LEV, C, C), lambda h, c: (0, 0, 0))]
    return col, st, consts


def _hgrn_fwd(proj, hlb, gg, rmat, masks):
    L = proj.shape[0]
    C = HG_C
    nc = L // C
    col, st, consts = _hgrn_specs(nc, False)

    def body(hq_ref, hz_ref, hi_ref, hg_ref, hlb_ref, gg_ref, r_ref, m_ref, og_ref, st_ref, state):
        c = pl.program_id(1)

        @pl.when(c == 0)
        def _():
            state[...] = jnp.zeros_like(state)

        def head(hh):
            sl = slice(hh * HG_D, (hh + 1) * HG_D)
            v, hg = hi_ref[:, sl], hg_ref[:, sl]
            _, q, _, k, _, _, g = _hg_pre(hq_ref[:, sl], hz_ref[:, sl], hlb_ref[0:1, sl], hlb_ref[1:2, sl])
            yield
            fall = _hg_decays(g, r_ref[...])
            fb, fe = fall[0:C], fall[C:2 * C]
            st0 = state[hh]
            st_ref[hh, 0] = st0
            yield
            for a in _hg_intra_levels(q, k, fall, m_ref[...]):
                yield
            vb = v.astype(BF16)
            o = jnp.dot(a.astype(BF16), vb, preferred_element_type=F32)
            o = o + lax.dot_general((q * fb).astype(BF16), st0.astype(BF16), NT, preferred_element_type=F32)
            yield
            ebc = jnp.exp(jnp.sum(g, axis=0, keepdims=True))
            state[hh] = st0 * ebc + lax.dot_general(vb, (k * fe).astype(BF16), TN, preferred_element_type=F32)
            r = lax.rsqrt(jnp.mean(o * o, axis=-1, keepdims=True) + EPS)
            og_ref[:, sl] = (o * r * gg_ref[...] * (hg * _sigmoid(hg))).astype(og_ref.dtype)

        _interleave((head(hh) for hh in range(HG_HPS)), skew=HG_SKEW)

    return pl.pallas_call(
        body, name="hgrn_fwd", grid=(HG_H // HG_HPS, nc),
        in_specs=[col(0), col(HG_H), col(2 * HG_H), col(3 * HG_H)] + consts,
        out_specs=[col(0), st],
        out_shape=[jax.ShapeDtypeStruct((L, D), BF16), jax.ShapeDtypeStruct((HG_H, nc, HG_D, HG_D), F32)],
        scratch_shapes=[pltpu.VMEM((HG_HPS, HG_D, HG_D), F32)],
        compiler_params=_cparams(("parallel", "arbitrary")),
    )(proj, proj, proj, proj, hlb, gg, rmat, masks)


def _hgrn_bwd(proj, dog, states, hlb, gg, rmat, masks):
    L = proj.shape[0]
    C = HG_C
    nc = L // C
    col, st, consts = _hgrn_specs(nc, True)

    def body(hq_ref, hz_ref, hi_ref, hg_ref, do_ref, hlb_ref, gg_ref, r_ref, m_ref, st_ref,
             dq_ref, dz_ref, di_ref, dg_ref, dlb_ref, dgg_ref, dstate):
        c = pl.program_id(1)

        @pl.when(c == 0)
        def _():
            dstate[...] = jnp.zeros_like(dstate)
            dlb_ref[...] = jnp.zeros_like(dlb_ref)
            dgg_ref[...] = jnp.zeros_like(dgg_ref)

        _interleave([bwd_head(c, hh, slice(hh * HG_D, (hh + 1) * HG_D), hq_ref, hz_ref, hi_ref, hg_ref, do_ref, hlb_ref,
                              gg_ref, r_ref, m_ref, st_ref, dq_ref, dz_ref, di_ref, dg_ref, dlb_ref, dgg_ref, dstate)
                     for hh in range(HG_HPS)], skew=HG_SKEW)

    def bwd_head(c, hh, sl, hq_ref, hz_ref, hi_ref, hg_ref, do_ref, hlb_ref, gg_ref, r_ref, m_ref, st_ref,
                 dq_ref, dz_ref, di_ref, dg_ref, dlb_ref, dgg_ref, dstate):
        hq, hz, v, hg = hq_ref[:, sl], hz_ref[:, sl], hi_ref[:, sl], hg_ref[:, sl]
        dout = do_ref[:, sl].astype(F32)
        gain = gg_ref[...]
        masks_v = m_ref[...]
        lb, q, sq, k, sz, snz, g = _hg_pre(hq, hz, hlb_ref[0:1, sl], hlb_ref[1:2, sl])
        yield
        fall = _hg_decays(g, r_ref[...])
        fb, fe = fall[0:C], fall[C:2 * C]
        yield
        for a in _hg_intra_levels(q, k, fall, masks_v):
            yield
        st0 = st_ref[hh, 0]
        st0b = st0.astype(BF16)
        ebc = jnp.exp(jnp.sum(g, axis=0, keepdims=True))
        qb, ke, vb = (q * fb).astype(BF16), (k * fe).astype(BF16), v.astype(BF16)
        ab = a.astype(BF16)
        o = jnp.dot(ab, vb, preferred_element_type=F32) + lax.dot_general(qb, st0b, NT, preferred_element_type=F32)
        yield
        r = lax.rsqrt(jnp.mean(o * o, axis=-1, keepdims=True) + EPS)
        oh = o * r
        sg = _sigmoid(hg)
        d_on = dout * (hg * sg)
        dhg = dout * (oh * gain) * (sg * (1.0 + hg * (1.0 - sg)))
        dgg_ref[hh] += jnp.sum(d_on * oh, axis=0, keepdims=True)
        dxh = d_on * gain
        do = r * (dxh - oh * jnp.mean(dxh * oh, axis=-1, keepdims=True))
        dob = do.astype(BF16)
        yield
        dsp = dstate[hh]
        dspb = dsp.astype(BF16)
        causal = _iota((C, C), 0) >= _iota((C, C), 1)
        da = jnp.where(causal, lax.dot_general(dob, vb, NT, preferred_element_type=F32), 0.0)
        diag = jnp.sum(do * v, axis=-1, keepdims=True)
        dv = lax.dot_general(ab, dob, TN, preferred_element_type=F32)
        dv = dv + lax.dot_general(ke, dspb, NT, preferred_element_type=F32)
        xq = jnp.dot(dob, st0b, preferred_element_type=F32)
        xk = jnp.dot(vb, dspb, preferred_element_type=F32)
        dq = diag * k + fb * xq
        dk = diag * q + fe * xk
        ke_xk = ke.astype(F32) * xk
        db = qb.astype(F32) * xq - ke_xk
        yield
        for l in range(HG_LEV):
            f = fall[(2 + l) * C:(3 + l) * C]
            dal = (masks_v[l] * da).astype(BF16)
            ql, kl = (q * f).astype(BF16), (k * f).astype(BF16)
            xq = jnp.dot(dal, kl, preferred_element_type=F32)
            xk = lax.dot_general(dal, ql, TN, preferred_element_type=F32)
            dq = dq + f * xq
            dk = dk + f * xk
            db = db + ql.astype(F32) * xq - kl.astype(F32) * xk
            yield
        dstate[hh] = dsp * ebc + lax.dot_general(dob, qb, TN, preferred_element_type=F32)
        triu = (_iota((C, C), 0) <= _iota((C, C), 1)).astype(F32)
        dg = jnp.dot(triu, db, precision=HIGHEST, preferred_element_type=F32)
        dg = dg + jnp.sum(st0 * ebc * dsp, axis=0, keepdims=True) + jnp.sum(ke_xk, axis=0, keepdims=True)
        keep = _row_ids((nc - 1 - c) * C, C) >= ROW0
        dg = jnp.where(keep, dg, 0.0)
        dk = jnp.where(keep, dk, 0.0)
        f_gate = lb + (1.0 - lb) * sz
        dfdz = (1.0 - lb) * sz * snz
        dz_ref[:, sl] = (dg * dfdz / f_gate - dk * dfdz).astype(dz_ref.dtype)
        dlb_ref[:, sl] += jnp.sum(dg * snz / f_gate - dk * snz, axis=0, keepdims=True)
        dq_ref[:, sl] = jnp.where(keep, dq * (sq * (1.0 + hq * (1.0 - sq))), 0.0).astype(dq_ref.dtype)
        di_ref[:, sl] = jnp.where(keep, dv, 0.0).astype(di_ref.dtype)
        dg_ref[:, sl] = jnp.where(keep, dhg, 0.0).astype(dg_ref.dtype)

    w = HG_HPS * HG_D
    outs = pl.pallas_call(
        body, name="hgrn_bwd", grid=(HG_H // HG_HPS, nc),
        in_specs=[col(0), col(HG_H), col(2 * HG_H), col(3 * HG_H), col(0)] + consts + [st],
        out_specs=[col(0), col(0), col(0), col(0), pl.BlockSpec((1, w), lambda h, c: (0, h)),
                   pl.BlockSpec((HG_HPS, 1, HG_D), lambda h, c: (h, 0, 0))],
        out_shape=[jax.ShapeDtypeStruct((L, D), BF16)] * 4 + [jax.ShapeDtypeStruct((1, D), F32),
                                                              jax.ShapeDtypeStruct((HG_H, 1, HG_D), F32)],
        scratch_shapes=[pltpu.VMEM((HG_HPS, HG_D, HG_D), F32)],
        compiler_params=_cparams(("parallel", "arbitrary")),
    )(proj, proj, proj, proj, dog, hlb, gg, rmat, masks, states)
    return outs


def _ffn_fwd(h, norm_gain, wg, wu, wo, tag):
    hn, g, u, act = _norm_proj(h, norm_gain, [wg, wu], [BF16, BF16], f"{tag}_in", swiglu=True)
    h_out = _matmul(act, wo, add=h, name=f"{tag}_out")
    return h_out, (h, hn, g, u, act)


def _ffn_bwd(dh, saved, norm_gain, wg, wu, wo, tag):
    h, hn, g, u, act = saved
    dg, du = _ffn_dact(dh, wo, g, u, f"{tag}_dact")
    d_wo = _matmul(act, dh, ta=True, name=f"{tag}_dwo")
    d_wg = _matmul(hn, dg, ta=True, name=f"{tag}_dwg")
    d_wu = _matmul(hn, du, ta=True, name=f"{tag}_dwu")
    dh, d_gain, _ = _dhn_norm([dg, du], [wg, wu], h, dh, norm_gain, f"{tag}_dhn_norm")
    return dh, d_gain, (d_wg, d_wu, d_wo)


def _local_step(h0, tgt, w, late_shards):
    L = h0.shape[0]
    gmat = jnp.asarray(np.kron(np.eye(MXU_N // FOX_DH), np.ones((FOX_DH, FOX_DH))).astype(np.float32), BF16)
    rmat, lmasks = _hgrn_consts()
    an, fn_ = w["attn_norm"], w["ffn_norm"]
    qg = jnp.tile(w["fox_q_norm"], (1, FOX_H))
    kg = jnp.tile(w["fox_k_norm"], (1, FOX_H))
    bf = jnp.pad(w["fox_b_f"], ((0, 0), (0, LANES - FOX_H)))
    fw = w["fox_w_in"]
    f_wq, f_wk, f_wv, f_wg = (fw[:, i * D:(i + 1) * D] for i in range(4))
    f_wf = jnp.pad(fw[:, 4 * D:], ((0, 0), (0, LANES - FOX_H)))

    hn0, q_raw, k_raw, v, gate, flog = _norm_proj(h0, an[0:1], [f_wq, f_wk, f_wv, f_wg, f_wf],
                                                  [F32, F32, BF16, BF16, F32], "fox_in")
    cq, ck = _fox_cumsum_fwd(flog, bf)
    qp, kp, vp = _fox_pack_fwd(q_raw, k_raw, v, cq, ck, qg, kg, gmat)
    o, og, lse2, gathered = _fox_attn_fwd(qp, kp, vp, gate, [late_shards[n] for n in GATHER_LATE])
    late = dict(zip(GATHER_LATE, gathered))
    f_wo = late["fox_w_out"].reshape(D, D)
    h_wo = late["hgrn_w_out"].reshape(D, D)
    h_wi = jnp.concatenate(list(late["hgrn_w_in"]), axis=1)
    g_in, g_out = late["ffn_w_in"], late["ffn_w_out"]
    ffw = []
    for i in range(2):
        rows_in, rows_out = slice(i * D, (i + 1) * D), slice(i * FFN // 4, (i + 1) * FFN // 4)
        ffw.append((jnp.concatenate([g_in[0, rows_in], g_in[1, rows_in]], axis=1),
                    jnp.concatenate([g_in[2, rows_in], g_in[3, rows_in]], axis=1),
                    jnp.concatenate([g_out[j, rows_out] for j in range(4)], axis=0)))
    h1 = _matmul(og, f_wo, add=h0, name="fox_out")
    h2, ffn0 = _ffn_fwd(h1, fn_[0:1], *ffw[0], "ffn0")

    hn2, proj = _norm_proj(h2, an[1:2], [h_wi], [F32], "hgrn_in")
    og1, states = _hgrn_fwd(proj, w["hgrn_lower_bounds"], w["hgrn_g_norm"], rmat, lmasks)
    h3 = _matmul(og1, h_wo, add=h2, name="hgrn_out")
    h4, ffn1 = _ffn_fwd(h3, fn_[1:2], *ffw[1], "ffn1")

    loss, dh, d_final = _loss_bwd(h4, tgt, w["final_norm"])

    dh, d_fn1, d_ffn1 = _ffn_bwd(dh, ffn1, fn_[1:2], *ffw[1], "ffn1")
    dog1 = _matmul(dh, h_wo, tb=True, out_dtype=BF16, name="hgrn_dog")
    d_h_wo = _matmul(og1, dh, ta=True, name="hgrn_dwo")
    dpq, dpz, dpi, dpg, d_lb, d_gg = _hgrn_bwd(proj, dog1, states, w["hgrn_lower_bounds"], w["hgrn_g_norm"],
                                               rmat, lmasks)
    dproj = jnp.concatenate([dpq, dpz, dpi, dpg], axis=1)
    d_h_wi = _matmul(hn2, dproj, ta=True, name="hgrn_dwi")
    dh, d_an1, _ = _dhn_norm([dproj], [h_wi], h2, dh, an[1:2], "hgrn_dhn_norm")

    dh, d_fn0, d_ffn0 = _ffn_bwd(dh, ffn0, fn_[0:1], *ffw[0], "ffn0")
    n_in, n_out = 2 * FFN // 4, FFN // 4
    d_ffn = [d_ffn0, d_ffn1]
    late_grads = dict(
        hgrn_w_in=_to_shards("hgrn_w_in", d_h_wi[None]), hgrn_w_out=d_h_wo.reshape(4, D // 4, D),
        ffn_w_in=jnp.stack([jnp.concatenate([d[j // 2][:, (j % 2) * n_in:(j % 2 + 1) * n_in] for d in d_ffn], axis=0)
                            for j in range(4)]),
        ffn_w_out=jnp.stack([jnp.concatenate([d[2][j * n_out:(j + 1) * n_out] for d in d_ffn], axis=0)
                             for j in range(4)]))
    pair_late, send_late = _pair_sums([late_grads[n] for n in LATE_NAMES], "late")

    dog = _matmul(dh, f_wo, tb=True, out_dtype=BF16, name="fox_dog")
    d_f_wo = _matmul(og, dh, ta=True, name="fox_dwo")

    def by_head(a):
        return jnp.pad(a.transpose(1, 0, 2).reshape(L, FOX_H), ((0, 0), (0, LANES - FOX_H)))

    qb = _fox_pack_bias(qp, cq, by_head(lse2.transpose(0, 2, 1)))
    dop, dgate = _fox_pack_bwd(dog, o, gate)
    dqp, dk, dv, dck, recv_late = _fox_attn_bwd(qb, kp, vp, dop, send_late)
    (dq_raw, dk_raw, dc_q), (d_qg, d_kg) = _fox_unpack_bwd(q_raw, k_raw, dqp, dk, qg, kg, gmat)
    dflog, d_bf = _fox_cumsum_bwd(dc_q, by_head(dck), flog, bf)
    dproj0 = jnp.concatenate([dq_raw, dk_raw, dv, dgate, dflog.astype(BF16)], axis=1)
    f_wall = jnp.concatenate([f_wq, f_wk, f_wv, f_wg, f_wf], axis=1)
    d_f_wall = _matmul(hn0, dproj0, ta=True, name="fox_dwi")
    d_f_wi = d_f_wall[:, :4 * D + FOX_H]
    fox_grads = dict(fox_w_in=d_f_wi[None], fox_w_out=d_f_wo[None])
    pair_fox, send_fox = _pair_sums([_to_shards(n, fox_grads[n]) for n in FOX_NAMES], "fox")
    dh, d_an0, recv_fox = _dhn_norm([dproj0], [f_wall], h0, dh, an[0:1], "fox_dhn_norm", slabs=send_fox)
    halves = _chip_sums(pair_fox, recv_fox, "fox") + _chip_sums(pair_late, recv_late, "late")
    theirs = _sibling_exchange(halves)
    big = {n: (m, t) for n, m, t in zip(FOX_NAMES + LATE_NAMES, halves, theirs)}
    small = dict(attn_norm=jnp.concatenate([d_an0, d_an1]), ffn_norm=jnp.concatenate([d_fn0, d_fn1]),
                 final_norm=d_final, lb_raw=d_lb, q_gain=d_qg, k_gain=d_kg, b_f=d_bf,
                 g_gain=d_gg.reshape(1, D))
    return loss, dh, big, small


def _me():
    return lax.axis_index("x"), lax.axis_index("y"), lax.axis_index("c")


def _flip(v, bit):
    return 1 - v if bit else v


def _chip_allgather_split(big, small):
    half = big.shape[0] // 2

    def body(big_in, small_in, big_out, small_out, ssem, rsem, fs_sem, fr_sem, ssem2, rsem2, lsem):
        x, y, c = _me()
        sib = (x, y, 1 - c)
        peers = _chip_peers()
        mine, other = pl.ds(c * half, half), pl.ds((1 - c) * half, half)
        local = [pltpu.make_async_copy(big_in, big_out.at[2 * x + y], lsem.at[0]),
                 pltpu.make_async_copy(small_in, small_out.at[2 * x + y], lsem.at[1])]
        sends = []
        for k, peer in enumerate(peers):
            sends.append(pltpu.make_async_remote_copy(big_in.at[mine], big_out.at[2 * x + y, mine], ssem.at[k],
                                                      rsem.at[k], device_id=peer, device_id_type=MESH))
            sends.append(pltpu.make_async_remote_copy(small_in, small_out.at[2 * x + y], ssem2.at[k], rsem2.at[k],
                                                      device_id=peer, device_id_type=MESH))
        for cp in local + sends:
            cp.start()
        forwards = []
        for k, peer in enumerate(peers):
            landed = big_out.at[2 * peer[0] + peer[1], mine]
            pltpu.make_async_remote_copy(big_in.at[mine], landed, ssem.at[k], rsem.at[k],
                                         device_id=peer, device_id_type=MESH).wait_recv()
            fwd = pltpu.make_async_remote_copy(landed, landed, fs_sem.at[k], fr_sem.at[k],
                                               device_id=sib, device_id_type=MESH)
            fwd.start()
            forwards.append(fwd)
        for k, peer in enumerate(peers):
            theirs = big_out.at[2 * peer[0] + peer[1], other]
            pltpu.make_async_remote_copy(theirs, theirs, fs_sem.at[k], fr_sem.at[k],
                                         device_id=sib, device_id_type=MESH).wait_recv()
            pltpu.make_async_remote_copy(small_in, small_out.at[2 * peer[0] + peer[1]], ssem2.at[k], rsem2.at[k],
                                         device_id=peer, device_id_type=MESH).wait_recv()
        for cp in sends + forwards:
            cp.wait_send()
        for cp in local:
            cp.wait()

    three = pltpu.SemaphoreType.DMA((3,))
    return pl.pallas_call(
        body, name="chip_allgather", in_specs=[ANY, ANY], out_specs=[ANY, ANY],
        out_shape=[jax.ShapeDtypeStruct((4,) + big.shape, big.dtype),
                   jax.ShapeDtypeStruct((4,) + small.shape, small.dtype)],
        scratch_shapes=[three, three, three, three, three, three, pltpu.SemaphoreType.DMA((2,))],
    )(big, small)


def _chip_peers():
    x, y, c = _me()
    return [(1 - x, y, c), (x, 1 - y, c), (1 - x, 1 - y, c)]


def _gather_sems(n):
    return [pltpu.SemaphoreType.DMA((n, 3)), pltpu.SemaphoreType.DMA((n, 3)), pltpu.SemaphoreType.DMA((n,))]


def _gather_copies(ins, outs, ssem, rsem, lsem, with_recvs):
    x, y, _ = _me()
    local, sends, recvs = [], [], []
    for a in range(len(ins)):
        local.append(pltpu.make_async_copy(ins[a], outs[a].at[2 * x + y], lsem.at[a]))
        for k, peer in enumerate(_chip_peers()):
            sends.append(pltpu.make_async_remote_copy(ins[a], outs[a].at[2 * x + y], ssem.at[a, k], rsem.at[a, k],
                                                      device_id=peer, device_id_type=MESH))
            if with_recvs:
                recvs.append(pltpu.make_async_remote_copy(ins[a], outs[a].at[2 * peer[0] + peer[1]], ssem.at[a, k],
                                                          rsem.at[a, k], device_id=peer, device_id_type=MESH))
    return local, sends, recvs


def _gather_start(ins, outs, ssem, rsem, lsem):
    local, sends, _ = _gather_copies(ins, outs, ssem, rsem, lsem, False)
    for cp in local + sends:
        cp.start()


def _gather_wait(ins, outs, ssem, rsem, lsem):
    local, sends, recvs = _gather_copies(ins, outs, ssem, rsem, lsem, True)
    for cp in local:
        cp.wait()
    for cp in sends:
        cp.wait_send()
    for cp in recvs:
        cp.wait_recv()


def _scatter_copies(ins, outs, ssem, rsem):
    copies = []
    for a in range(len(ins)):
        for k, peer in enumerate(_chip_peers()):
            copies.append(pltpu.make_async_remote_copy(ins[a].at[2 * peer[0] + peer[1]], outs[a].at[k], ssem.at[a, k],
                                                       rsem.at[a, k], device_id=peer, device_id_type=MESH))
    return copies


def _device_allgather(arr):
    def body(in_ref, out_ref, ssem, rsem, lsem):
        x, y, c = _me()
        me = 4 * x + 2 * y + c
        peers = [(_flip(x, k & 4), _flip(y, k & 2), _flip(c, k & 1)) for k in range(1, 8)]
        local = pltpu.make_async_copy(in_ref, out_ref.at[me], lsem)
        local.start()
        sends = []
        for k, peer in enumerate(peers):
            cp = pltpu.make_async_remote_copy(in_ref, out_ref.at[me], ssem.at[k], rsem.at[k],
                                              device_id=peer, device_id_type=MESH)
            cp.start()
            sends.append(cp)
        local.wait()
        for cp in sends:
            cp.wait_send()
        for k, peer in enumerate(peers):
            pltpu.make_async_remote_copy(in_ref, out_ref.at[4 * peer[0] + 2 * peer[1] + peer[2]], ssem.at[k],
                                         rsem.at[k], device_id=peer, device_id_type=MESH).wait_recv()

    return pl.pallas_call(
        body, name="device_allgather", in_specs=[ANY], out_specs=ANY,
        out_shape=jax.ShapeDtypeStruct((8,) + arr.shape, arr.dtype),
        scratch_shapes=[pltpu.SemaphoreType.DMA((7,)), pltpu.SemaphoreType.DMA((7,)), pltpu.SemaphoreType.DMA],
    )(arr)


def _sibling_send_other_half(arrs, tag):
    n = len(arrs)

    def body(*refs):
        ins, outs = refs[:n], refs[n:2 * n]
        ssem, rsem = refs[2 * n:]
        x, y, c = _me()
        cps = []
        for a in range(n):
            half = ins[a].shape[1] // 2
            src = ins[a].at[:, pl.ds((1 - c) * half, half), :]
            cp = pltpu.make_async_remote_copy(src, outs[a], ssem.at[a], rsem.at[a],
                                              device_id=(x, y, 1 - c), device_id_type=MESH)
            cp.start()
            cps.append(cp)
        for cp in cps:
            cp.wait()

    return pl.pallas_call(
        body, name=f"grad_sibling_swap_{tag}", in_specs=[ANY] * n, out_specs=[ANY] * n,
        out_shape=[jax.ShapeDtypeStruct((4, a.shape[1] // 2, a.shape[2]), a.dtype) for a in arrs],
        scratch_shapes=[pltpu.SemaphoreType.DMA((n,)), pltpu.SemaphoreType.DMA((n,))],
    )(*arrs)


def _sibling_exchange(arrs):
    n = len(arrs)

    def body(*refs):
        ins, outs = refs[:n], refs[n:2 * n]
        ssem, rsem = refs[2 * n:]
        x, y, c = _me()
        cps = [pltpu.make_async_remote_copy(ins[a], outs[a], ssem.at[a], rsem.at[a], device_id=(x, y, 1 - c),
                                            device_id_type=MESH) for a in range(n)]
        for cp in cps:
            cp.start()
        for cp in cps:
            cp.wait()

    return pl.pallas_call(
        body, name="grad_sibling_exchange", in_specs=[ANY] * n, out_specs=[ANY] * n,
        out_shape=[jax.ShapeDtypeStruct(a.shape, a.dtype) for a in arrs],
        scratch_shapes=[pltpu.SemaphoreType.DMA((n,)), pltpu.SemaphoreType.DMA((n,))],
    )(*arrs)


def _pair_sums(grads, tag):
    got = _sibling_send_other_half(grads, tag)
    res = [_pair_add(g, t, f"grad_pair_add_{tag}{i}") for i, (g, t) in enumerate(zip(grads, got))]
    return [r[0] for r in res], [r[1] for r in res]


def _mesh_scalar(v):
    return jnp.asarray(v, jnp.int32).reshape(1)


def _pair_add(g, t, name):
    _, rows, cols = g.shape
    half = rows // 2
    tm = _tile(half, cap=(2 * 1024 * 1024) // (4 * cols))

    def body(c_ref, g_ref, t_ref, o_ref, ob_ref):
        s = g_ref[0, 0] + t_ref[0]
        o_ref[0] = s
        ob_ref[0] = s.astype(ob_ref.dtype)

    spec = pl.BlockSpec((1, tm, cols), lambda j, i, c: (j, i, 0))
    return pl.pallas_call(
        body, name=name,
        grid_spec=pltpu.PrefetchScalarGridSpec(
            num_scalar_prefetch=1, grid=(4, half // tm),
            in_specs=[pl.BlockSpec((1, 1, tm, cols), lambda j, i, c: (j, c[0], i, 0)), spec],
            out_specs=[spec, spec]),
        out_shape=[jax.ShapeDtypeStruct(t.shape, F32), jax.ShapeDtypeStruct(t.shape, BF16)],
        compiler_params=_cparams(("parallel", "parallel")),
    )(_mesh_scalar(lax.axis_index("c")), g.reshape(4, 2, half, cols), t)


def _chip_sums(pair, recv, tag):
    x, y, _ = _me()
    out = []
    for n, (p, r) in enumerate(zip(pair, recv)):
        _, half, cols = p.shape
        tm = _tile(half, cap=(2 * 1024 * 1024) // (4 * cols))

        def body(j_ref, p_ref, r_ref, o_ref):
            o_ref[...] = p_ref[0] + r_ref[0].astype(F32) + r_ref[1].astype(F32) + r_ref[2].astype(F32)

        out.append(pl.pallas_call(
            body, name=f"grad_chip_add_{tag}{n}",
            grid_spec=pltpu.PrefetchScalarGridSpec(
                num_scalar_prefetch=1, grid=(half // tm,),
                in_specs=[pl.BlockSpec((1, tm, cols), lambda i, j: (j[0], i, 0)),
                          pl.BlockSpec((3, tm, cols), lambda i, j: (0, i, 0))],
                out_specs=pl.BlockSpec((tm, cols), lambda i, j: (i, 0))),
            out_shape=jax.ShapeDtypeStruct((half, cols), F32),
            compiler_params=_cparams(("parallel",)),
        )(_mesh_scalar(2 * x + y), p, r))
    return out


SMALL_ROWS = 32


def _small_finalize(gathered, hlb, fold64, fold128):
    def body(g_ref, hlb_ref, f64_ref, f128_ref, rows_ref, qk_ref, gg_ref, lb_ref):
        tot = g_ref[0]
        for d in range(1, 8):
            tot = tot + g_ref[d]
        rows_ref[...] = tot
        qk_ref[...] = jnp.dot(rows_ref[6:8, :], f64_ref[...], precision=HIGHEST, preferred_element_type=F32)
        gg_ref[...] = jnp.dot(rows_ref[9:10, :], f128_ref[...], precision=HIGHEST, preferred_element_type=F32)
        h0, h1 = hlb_ref[0:1, :], hlb_ref[1:2, :]
        mx = jnp.maximum(h0, h1)
        e0, e1 = jnp.exp(h0 - mx), jnp.exp(h1 - mx)
        lb = e1 / (e0 + e1)
        d1 = rows_ref[5:6, :] * lb * (1.0 - lb)
        lb_ref[...] = jnp.where(_iota((2, 1), 0) == 0, -d1, d1)

    return pl.pallas_call(
        body, name="small_finalize",
        out_shape=[jax.ShapeDtypeStruct((SMALL_ROWS, D), F32), jax.ShapeDtypeStruct((2, FOX_DH), F32),
                   jax.ShapeDtypeStruct((1, HG_D), F32), jax.ShapeDtypeStruct((2, D), F32)],
    )(gathered, hlb, fold64, fold128)


FOX_NAMES = ("fox_w_in", "fox_w_out")
LATE_NAMES = ("hgrn_w_in", "hgrn_w_out", "ffn_w_in", "ffn_w_out")
BIG_NAMES = FOX_NAMES + LATE_NAMES
GATHER_LATE = ("fox_w_out",) + LATE_NAMES
COL_SHARDED = ("fox_w_in", "hgrn_w_in", "ffn_w_in")


def _shard2d(name, a):
    return a.reshape(-1, a.shape[-1])


def _to_shards(name, g):
    layers = g.shape[0]
    if name in COL_SHARDED:
        k, n = g.shape[1], g.shape[2] // 4
        return g.reshape(layers, k, 4, n).transpose(2, 0, 1, 3).reshape(4, layers * k, n)
    r = g.shape[1] // 4
    return g.reshape(layers, 4, r, g.shape[2]).transpose(1, 0, 2, 3).reshape(4, layers * r, g.shape[2])


def kernel(x, meta_tokens, attn_norm, ffn_norm, final_norm, fox_w_in, fox_b_f, fox_q_norm, fox_k_norm, fox_w_out, hgrn_w_in, hgrn_lower_bounds, hgrn_g_norm, hgrn_w_out, ffn_w_in, ffn_w_out, loss_target, m_meta_tokens, m_attn_norm, m_ffn_norm, m_final_norm, m_fox_w_in, m_fox_b_f, m_fox_q_norm, m_fox_k_norm, m_fox_w_out, m_hgrn_w_in, m_hgrn_lower_bounds, m_hgrn_g_norm, m_hgrn_w_out, m_ffn_w_in, m_ffn_w_out, v_meta_tokens, v_attn_norm, v_ffn_norm, v_final_norm, v_fox_w_in, v_fox_b_f, v_fox_q_norm, v_fox_k_norm, v_fox_w_out, v_hgrn_w_in, v_hgrn_lower_bounds, v_hgrn_g_norm, v_hgrn_w_out, v_ffn_w_in, v_ffn_w_out):
    params = dict(meta_tokens=meta_tokens, attn_norm=attn_norm, ffn_norm=ffn_norm, final_norm=final_norm,
                  fox_w_in=fox_w_in, fox_b_f=fox_b_f, fox_q_norm=fox_q_norm, fox_k_norm=fox_k_norm,
                  fox_w_out=fox_w_out, hgrn_w_in=hgrn_w_in, hgrn_lower_bounds=hgrn_lower_bounds,
                  hgrn_g_norm=hgrn_g_norm, hgrn_w_out=hgrn_w_out, ffn_w_in=ffn_w_in, ffn_w_out=ffn_w_out)
    mom_m = dict(meta_tokens=m_meta_tokens, attn_norm=m_attn_norm, ffn_norm=m_ffn_norm, final_norm=m_final_norm,
                 fox_w_in=m_fox_w_in, fox_b_f=m_fox_b_f, fox_q_norm=m_fox_q_norm, fox_k_norm=m_fox_k_norm,
                 fox_w_out=m_fox_w_out, hgrn_w_in=m_hgrn_w_in, hgrn_lower_bounds=m_hgrn_lower_bounds,
                 hgrn_g_norm=m_hgrn_g_norm, hgrn_w_out=m_hgrn_w_out, ffn_w_in=m_ffn_w_in, ffn_w_out=m_ffn_w_out)
    mom_v = dict(meta_tokens=v_meta_tokens, attn_norm=v_attn_norm, ffn_norm=v_ffn_norm, final_norm=v_final_norm,
                 fox_w_in=v_fox_w_in, fox_b_f=v_fox_b_f, fox_q_norm=v_fox_q_norm, fox_k_norm=v_fox_k_norm,
                 fox_w_out=v_fox_w_out, hgrn_w_in=v_hgrn_w_in, hgrn_lower_bounds=v_hgrn_lower_bounds,
                 hgrn_g_norm=v_hgrn_g_norm, hgrn_w_out=v_hgrn_w_out, ffn_w_in=v_ffn_w_in, ffn_w_out=v_ffn_w_out)
    names = list(params)
    xi, yi, _ = _me()

    shards = {n: _shard2d(n, params[n]).astype(BF16) for n in BIG_NAMES}
    w_in_g, meta_g = _chip_allgather_split(shards["fox_w_in"], meta_tokens)
    w = dict(fox_w_in=jnp.concatenate(list(w_in_g), axis=1))
    meta_full = jnp.concatenate(list(meta_g), axis=1)
    w.update(attn_norm=attn_norm, ffn_norm=ffn_norm, final_norm=final_norm.reshape(1, D), fox_b_f=fox_b_f,
             fox_q_norm=fox_q_norm, fox_k_norm=fox_k_norm, hgrn_lower_bounds=hgrn_lower_bounds,
             hgrn_g_norm=hgrn_g_norm)

    h0 = jnp.concatenate([jnp.zeros((ROW0, D), F32), meta_full, x[0]], axis=0)
    loss, dh0, big, small = _local_step(h0, loss_target[0], w, {n: shards[n] for n in GATHER_LATE})
    loss = lax.psum(loss, ("x", "y", "c"))
    grad_x = dh0[PAD:][None]
    grads = {}

    rows = jnp.concatenate([small["attn_norm"], small["ffn_norm"], small["final_norm"], small["lb_raw"],
                            small["q_gain"], small["k_gain"],
                            jnp.pad(small["b_f"], ((0, 0), (0, D - LANES))), small["g_gain"],
                            dh0[ROW0:PAD], jnp.zeros((SMALL_ROWS - 10 - N_META, D), F32)], axis=0)
    allrows = _device_allgather(rows)
    fold64 = jnp.asarray(np.tile(np.eye(FOX_DH, dtype=np.float32), (FOX_H, 1)))
    fold128 = jnp.asarray(np.tile(np.eye(HG_D, dtype=np.float32), (HG_H, 1)))
    tot, qk, gg, dlb = _small_finalize(allrows, hgrn_lower_bounds, fold64, fold128)
    grads.update(attn_norm=tot[0:2], ffn_norm=tot[2:4], final_norm=tot[4], hgrn_lower_bounds=dlb,
                 fox_q_norm=qk[0:1], fox_k_norm=qk[1:2], fox_b_f=tot[8:9, :FOX_H], hgrn_g_norm=gg,
                 meta_tokens=lax.dynamic_slice_in_dim(tot[10:10 + N_META], (2 * xi + yi) * (D // 4), D // 4, axis=1))

    delta, new_m, new_v = {}, {}, {}
    for n in BIG_NAMES:
        res = _adamw_halves(_shard2d(n, params[n]), *big[n], _shard2d(n, mom_m[n]), _shard2d(n, mom_v[n]),
                            f"adamw_{n}")
        grads[n], delta[n], new_m[n], new_v[n] = (t.reshape(params[n].shape) for t in res)
    delta["meta_tokens"], new_m["meta_tokens"], new_v["meta_tokens"] = _adamw(
        meta_tokens, grads["meta_tokens"], m_meta_tokens, v_meta_tokens, "adamw_meta_tokens")
    small_names = [n for n in names if n not in BIG_NAMES and n != "meta_tokens"]

    def pack(d):
        return jnp.concatenate([jnp.pad(d[n].reshape(-1, d[n].shape[-1]), ((0, 0), (0, D - d[n].shape[-1])))
                                for n in small_names], axis=0)

    packed = [pack(t) for t in (params, grads, mom_m, mom_v)]
    n_rows = packed[0].shape[0]
    packed = [jnp.pad(t, ((0, 16 - n_rows), (0, 0))) for t in packed]
    res = _adamw(*packed, "adamw_small")
    r0 = 0
    for n in small_names:
        nr = params[n].reshape(-1, params[n].shape[-1]).shape[0]
        for dst, src in zip((delta, new_m, new_v), res):
            dst[n] = src[r0:r0 + nr, :params[n].shape[-1]].reshape(params[n].shape)
        r0 += nr

    return (loss, grad_x, *[grads[n] for n in names], *[delta[n] for n in names],
            *[new_m[n] for n in names], *[new_v[n] for n in names])
```

```python
import functools

import numpy as np
import jax
import jax.numpy as jnp
from jax import lax
from jax.experimental import pallas as pl
from jax.experimental.pallas import tpu as pltpu

F32, BF16 = jnp.float32, jnp.bfloat16
HIGHEST = lax.Precision.HIGHEST

D = 1024
N_META = 16
PAD = 128
ROW0 = PAD - N_META
FOX_H, FOX_DH = 16, 64
HG_H, HG_D = 8, 128
HG_C = 128
HG_LEV = 7
HG_HPS = 8
HG_SKEW = 0
FFN = 2816
EPS = 1e-6
BIG = 1e30
LOG2E = 1.4426950408889634
LANES = 128
MXU_N = 256
VMEM_LIMIT = 48 * 1024 * 1024
ROW_TILES = (640, 512, 384, 320, 256, 128, 64, 32, 16, 8)
ATTN_TILES = (640, 512, 256, 128)
FOX_HPS_FWD = 8
FOX_HPS_BWD = 4

ADAM_LR, ADAM_B1, ADAM_B2, ADAM_EPS, ADAM_WD, ADAM_STEP = 0.001, 0.9, 0.999, 1e-08, 0.01, 10

MESH = pl.DeviceIdType.MESH
ANY = pl.BlockSpec(memory_space=pl.ANY)
NT = (((1,), (1,)), ((), ()))
TN = (((0,), (0,)), ((), ()))


def _tile(n, cands=ROW_TILES, cap=None):
    for c in cands:
        if n % c == 0 and (cap is None or c <= cap):
            return c
    return n


def _cparams(sem):
    return pltpu.CompilerParams(dimension_semantics=sem, vmem_limit_bytes=VMEM_LIMIT)


def _sigmoid(x):
    return jax.nn.sigmoid(x)


def _log_sigmoid(x):
    return jnp.minimum(x, 0.0) - jnp.log(1.0 + jnp.exp(-jnp.abs(x)))


def _iota(shape, dim):
    return lax.broadcasted_iota(jnp.int32, shape, dim)


def _matmul(a, b, *, ta=False, tb=False, out_dtype=F32, add=None, name):
    if ta:
        kdim, m = a.shape
    else:
        m, kdim = a.shape
    n = b.shape[0] if tb else b.shape[1]
    if ta:
        tm = m if m <= 1024 else _tile(m, (1408, 1024, 512, 256, 128))
        tk = _tile(kdim, (1664,) + ROW_TILES)
    else:
        tm = _tile(m)
        tk = kdim if kdim <= 4096 else _tile(kdim, (2048, 1024, 512))
    tn = n if n <= 1024 else _tile(n, (1408, 1024, 512, 256, 128))
    nk = kdim // tk
    dn = (((0 if ta else 1,), (1 if tb else 0,)), ((), ()))

    def body(*refs):
        if add is None:
            a_ref, b_ref, o_ref, acc_ref = refs
        else:
            a_ref, b_ref, add_ref, o_ref, acc_ref = refs
        k = pl.program_id(2)

        @pl.when(k == 0)
        def _():
            acc_ref[...] = jnp.zeros_like(acc_ref)

        acc_ref[...] += lax.dot_general(a_ref[...].astype(BF16), b_ref[...].astype(BF16), dn,
                                        preferred_element_type=F32)

        @pl.when(k == nk - 1)
        def _():
            r = acc_ref[...]
            if add is not None:
                r = r + add_ref[...].astype(F32)
            o_ref[...] = r.astype(o_ref.dtype)

    a_spec = pl.BlockSpec((tk, tm), lambda j, i, k: (k, i)) if ta else pl.BlockSpec((tm, tk), lambda j, i, k: (i, k))
    b_spec = pl.BlockSpec((tn, tk), lambda j, i, k: (j, k)) if tb else pl.BlockSpec((tk, tn), lambda j, i, k: (k, j))
    o_spec = pl.BlockSpec((tm, tn), lambda j, i, k: (i, j))
    ins, specs = [a, b], [a_spec, b_spec]
    if add is not None:
        ins.append(add)
        specs.append(o_spec)
    return pl.pallas_call(
        body, name=name, grid=(n // tn, m // tm, nk), in_specs=specs, out_specs=o_spec,
        out_shape=jax.ShapeDtypeStruct((m, n), out_dtype),
        scratch_shapes=[pltpu.VMEM((tm, tn), F32)],
        compiler_params=_cparams(("parallel", "parallel", "arbitrary")),
    )(*ins)


def _rowwise(fn, ins, bcast, outs, accs, *, name, reverse=False, carry=None, as_refs=False):
    rows = ins[0].shape[0]
    per_row = sum(x.shape[1] * x.dtype.itemsize for x in ins) + sum(c * jnp.dtype(d).itemsize for c, d in outs)
    tm = _tile(rows, cap=max(8, (10 * 1024 * 1024) // per_row))
    n = rows // tm
    n_in, n_b, n_o, n_a = len(ins), len(bcast), len(outs), len(accs)

    def body(*refs):
        in_refs = refs[:n_in]
        b_refs = refs[n_in:n_in + n_b]
        o_refs = refs[n_in + n_b:n_in + n_b + n_o]
        a_refs = refs[n_in + n_b + n_o:n_in + n_b + n_o + n_a]
        c_refs = refs[n_in + n_b + n_o + n_a:]
        i = pl.program_id(0)
        blk = (n - 1 - i) if reverse else i
        if c_refs:
            @pl.when(i == 0)
            def _():
                c_refs[0][...] = jnp.zeros_like(c_refs[0])
        args = (list(in_refs) if as_refs else [r[...] for r in in_refs], [r[...] for r in b_refs])
        o_vals, a_vals = fn(blk * tm, *args, *c_refs)
        for r, v in zip(o_refs, o_vals):
            r[...] = v.astype(r.dtype)
        if n_a:
            @pl.when(i == 0)
            def _():
                for r in a_refs:
                    r[...] = jnp.zeros_like(r)
            for r, v in zip(a_refs, a_vals):
                r[...] += v

    def row_map(i):
        return ((n - 1 - i) if reverse else i, 0)

    in_specs = [pl.BlockSpec((tm, x.shape[1]), row_map) for x in ins]
    in_specs += [pl.BlockSpec(x.shape, lambda i, nd=x.ndim: (0,) * nd) for x in bcast]
    out_specs = [pl.BlockSpec((tm, c), row_map) for c, _ in outs]
    out_specs += [pl.BlockSpec(s, lambda i: (0, 0)) for s in accs]
    out_shape = [jax.ShapeDtypeStruct((rows, c), d) for c, d in outs]
    out_shape += [jax.ShapeDtypeStruct(s, F32) for s in accs]
    res = pl.pallas_call(
        body, name=name, grid=(n,), in_specs=in_specs, out_specs=out_specs, out_shape=out_shape,
        scratch_shapes=[pltpu.VMEM(carry, F32)] if carry else [],
        compiler_params=_cparams(("arbitrary",)),
    )(*ins, *bcast)
    return res[:n_o], res[n_o:]


def _row_ids(row0, tm):
    return row0 + _iota((tm, 1), 0)


def _rms_bwd_math(xv, dy, g):
    r = lax.rsqrt(jnp.mean(xv * xv, axis=-1, keepdims=True) + EPS)
    xh = xv * r
    dxh = dy * g
    dx = r * (dxh - xh * jnp.mean(dxh * xh, axis=-1, keepdims=True))
    return dx, jnp.sum(dy * xh, axis=0, keepdims=True)


def _loss_bwd(h, tgt, gain):
    tm = _tile(h.shape[0], (5 * PAD, PAD))
    n = h.shape[0] // tm
    pieces = tm // PAD

    def body(x_ref, *refs):
        t_refs, (g_ref, dx_ref, loss_ref, dg_ref) = refs[:pieces], refs[pieces:]
        i = pl.program_id(0)

        @pl.when(i == 0)
        def _():
            loss_ref[...] = jnp.zeros_like(loss_ref)
            dg_ref[...] = jnp.zeros_like(dg_ref)

        xv, g = x_ref[...], g_ref[...]
        r = lax.rsqrt(jnp.mean(xv * xv, axis=-1, keepdims=True) + EPS)
        xh = xv * r
        tv = jnp.concatenate([t[...] for t in t_refs], axis=0)
        err = jnp.where(_row_ids(i * tm, tm) >= PAD, xh * g - tv, 0.0)
        per_row = jnp.mean(err * err, axis=-1, keepdims=True)
        loss_ref[...] += jnp.broadcast_to(0.5 * jnp.sum(per_row, axis=0, keepdims=True), (1, LANES))
        dy = err * (1.0 / D)
        dxh = dy * g
        dx_ref[...] = r * (dxh - xh * jnp.mean(dxh * xh, axis=-1, keepdims=True))
        dg_ref[...] += jnp.sum(dy * xh, axis=0, keepdims=True)

    dh, loss, dgain = pl.pallas_call(
        body, name="loss_bwd", grid=(n,),
        in_specs=[pl.BlockSpec((tm, D), lambda i: (i, 0))]
        + [pl.BlockSpec((PAD, D), lambda i, k=k: (jnp.maximum(i * pieces + k - 1, 0), 0)) for k in range(pieces)]
        + [pl.BlockSpec((1, D), lambda i: (0, 0))],
        out_specs=[pl.BlockSpec((tm, D), lambda i: (i, 0)), pl.BlockSpec((1, LANES), lambda i: (0, 0)),
                   pl.BlockSpec((1, D), lambda i: (0, 0))],
        out_shape=[jax.ShapeDtypeStruct(h.shape, F32), jax.ShapeDtypeStruct((1, LANES), F32),
                   jax.ShapeDtypeStruct((1, D), F32)],
        compiler_params=_cparams(("arbitrary",)),
    )(h, *([tgt] * pieces), gain)
    return loss[0, 0], dh, dgain


FFN_TILES = dict(rows=(320, 256, 128), cols=(1408, 1024, 512, 256, 128))


def _norm_proj(h, gain, ws, dtypes, name, swiglu=False):
    m = h.shape[0]
    tm = _tile(m, FFN_TILES["rows"])
    n = len(ws)

    def body(h_ref, g_ref, *refs):
        w_refs, hn_ref, o_refs = refs[:n], refs[n], refs[n + 1:]
        x = h_ref[...]
        hn = (x * lax.rsqrt(jnp.mean(x * x, axis=-1, keepdims=True) + EPS) * g_ref[...]).astype(BF16)
        hn_ref[...] = hn
        prods = [jnp.dot(hn, w[...], preferred_element_type=F32) for w in w_refs]
        for o, p in zip(o_refs, prods):
            o[...] = p.astype(o.dtype)
        if swiglu:
            o_refs[n][...] = (prods[0] * _sigmoid(prods[0]) * prods[1]).astype(o_refs[n].dtype)

    rows = lambda c: pl.BlockSpec((tm, c), lambda i: (i, 0))
    cols = [w.shape[1] for w in ws] + ([ws[0].shape[1]] if swiglu else [])
    dts = list(dtypes) + ([BF16] if swiglu else [])
    return pl.pallas_call(
        body, name=name, grid=(m // tm,),
        in_specs=[rows(D), pl.BlockSpec((1, D), lambda i: (0, 0))]
        + [pl.BlockSpec(w.shape, lambda i: (0, 0), pipeline_mode=pl.Buffered(1)) for w in ws],
        out_specs=[rows(D)] + [rows(c) for c in cols],
        out_shape=[jax.ShapeDtypeStruct((m, D), BF16)] + [jax.ShapeDtypeStruct((m, c), d) for c, d in zip(cols, dts)],
        compiler_params=_cparams(("parallel",)),
    )(h, gain, *ws)


def _dhn_norm(dys, ws, h, dh_up, gain, name, slabs=()):
    m = h.shape[0]
    tm = _tile(m, FFN_TILES["rows"])
    n, ns = len(dys), len(slabs)
    steps = m // tm

    def body(*refs):
        dy_refs, w_refs = refs[:n], refs[n:2 * n]
        h_ref, up_ref, g_ref = refs[2 * n:2 * n + 3]
        sl_in = refs[2 * n + 3:2 * n + 3 + ns]
        dh_ref, dgain_ref = refs[2 * n + 3 + ns:2 * n + 5 + ns]
        sl_out = refs[2 * n + 5 + ns:2 * n + 5 + 2 * ns]
        sems = refs[2 * n + 5 + 2 * ns:]
        i = pl.program_id(0)

        @pl.when(i == 0)
        def _():
            dgain_ref[...] = jnp.zeros_like(dgain_ref)
            for cp in _scatter_copies(sl_in, sl_out, *sems) if ns else ():
                cp.start()

        dy = sum(lax.dot_general(a[...], w[...], NT, preferred_element_type=F32) for a, w in zip(dy_refs, w_refs))
        dx, dgain = _rms_bwd_math(h_ref[...], dy, g_ref[...])
        keep = _row_ids(i * tm, tm) >= ROW0
        dh_ref[...] = jnp.where(keep, up_ref[...] + dx, 0.0)
        dgain_ref[...] += dgain

        if ns:
            @pl.when(i == steps - 1)
            def _():
                for cp in _scatter_copies(sl_in, sl_out, *sems):
                    cp.wait()

    rows = lambda c: pl.BlockSpec((tm, c), lambda i: (i, 0))
    whole = lambda a: pl.BlockSpec(a.shape, lambda i: (0, 0), pipeline_mode=pl.Buffered(1))
    res = pl.pallas_call(
        body, name=name, grid=(steps,),
        in_specs=[rows(a.shape[1]) for a in dys] + [whole(w) for w in ws]
        + [rows(D), rows(D), pl.BlockSpec((1, D), lambda i: (0, 0))] + [ANY] * ns,
        out_specs=[rows(D), pl.BlockSpec((1, D), lambda i: (0, 0))] + [ANY] * ns,
        out_shape=[jax.ShapeDtypeStruct((m, D), F32), jax.ShapeDtypeStruct((1, D), F32)]
        + [jax.ShapeDtypeStruct((3,) + a.shape[1:], a.dtype) for a in slabs],
        scratch_shapes=[pltpu.SemaphoreType.DMA((ns, 3)), pltpu.SemaphoreType.DMA((ns, 3))] if ns else [],
        compiler_params=_cparams(("arbitrary",)),
    )(*dys, *ws, h, dh_up, gain, *slabs)
    return res[0], res[1], res[2:]


def _ffn_dact(dh, wo, g, u, name):
    m, kdim = dh.shape
    n = wo.shape[0]
    tm, tn = _tile(m, FFN_TILES["rows"]), _tile(n, FFN_TILES["cols"])

    def body(a_ref, w_ref, g_ref, u_ref, dg_ref, du_ref):
        da = lax.dot_general(a_ref[...].astype(BF16), w_ref[...], NT, preferred_element_type=F32)
        gv, uv = g_ref[...].astype(F32), u_ref[...].astype(F32)
        s = _sigmoid(gv)
        dg_ref[...] = (da * uv * (s * (1.0 + gv * (1.0 - s)))).astype(dg_ref.dtype)
        du_ref[...] = (da * gv * s).astype(du_ref.dtype)

    ospec = pl.BlockSpec((tm, tn), lambda j, i: (i, j))
    return pl.pallas_call(
        body, name=name, grid=(n // tn, m // tm),
        in_specs=[pl.BlockSpec((tm, kdim), lambda j, i: (i, 0)), pl.BlockSpec((tn, kdim), lambda j, i: (j, 0)),
                  ospec, ospec],
        out_specs=[ospec] * 2, out_shape=[jax.ShapeDtypeStruct((m, n), BF16)] * 2,
        compiler_params=_cparams(("parallel", "parallel")),
    )(dh, wo, g, u)


def _adamw_math(wv, gv, mv, vv):
    mn = ADAM_B1 * mv + (1.0 - ADAM_B1) * gv
    vn = ADAM_B2 * vv + (1.0 - ADAM_B2) * (gv * gv)
    m_hat = mn / (1.0 - ADAM_B1 ** ADAM_STEP)
    v_hat = vn / (1.0 - ADAM_B2 ** ADAM_STEP)
    return -ADAM_LR * (m_hat / (jnp.sqrt(v_hat) + ADAM_EPS) + ADAM_WD * wv), mn, vn


def _adamw(w, g, m, v, name):
    def fn(row0, ins, bc):
        return list(_adamw_math(*ins)), []
    c = w.shape[1]
    return _rowwise(fn, [w, g, m, v], [], [(c, F32)] * 3, [], name=name)[0]


def _adamw_halves(w, mine, theirs, m, v, name):
    rows, cols = w.shape
    half = rows // 2
    tm = _tile(half, cap=(10 * 1024 * 1024) // (9 * 4 * cols))
    nb = half // tm

    def body(c_ref, w_ref, g1_ref, g2_ref, m_ref, v_ref, g_out, d_out, m_out, v_out):
        own = (pl.program_id(0) // nb) == c_ref[0]
        g = jnp.where(own, g1_ref[...], g2_ref[...])
        delta, mn, vn = _adamw_math(w_ref[...], g, m_ref[...], v_ref[...])
        g_out[...] = g
        d_out[...] = delta
        m_out[...] = mn
        v_out[...] = vn

    full = pl.BlockSpec((tm, cols), lambda i, c: (i, 0))
    part = pl.BlockSpec((tm, cols), lambda i, c: (lax.rem(i, nb), 0))
    return pl.pallas_call(
        body, name=name,
        grid_spec=pltpu.PrefetchScalarGridSpec(num_scalar_prefetch=1, grid=(2 * nb,),
                                               in_specs=[full, part, part, full, full], out_specs=[full] * 4),
        out_shape=[jax.ShapeDtypeStruct((rows, cols), F32)] * 4,
        compiler_params=_cparams(("parallel",)),
    )(_mesh_scalar(lax.axis_index("c")), w, mine, theirs, m, v)


def _head_sum(x, gmat):
    hi = x.astype(BF16)
    lo = (x - hi.astype(F32)).astype(BF16)
    w = gmat.shape[0]
    return jnp.concatenate(
        [jnp.dot(hi[:, b:b + w], gmat, preferred_element_type=F32) + jnp.dot(lo[:, b:b + w], gmat,
                                                                             preferred_element_type=F32)
         for b in range(0, x.shape[1], w)], axis=1)


def _split3(x):
    hi = x.astype(BF16).astype(F32)
    r = x - hi
    mid = r.astype(BF16).astype(F32)
    return hi, mid, r - mid


def _extra_base(hh):
    return FOX_DH * (1 - hh)


def _data_mask(hh):
    lane = _iota((1, LANES), 1)
    return (lane >= FOX_DH * hh) & (lane < FOX_DH * (hh + 1))


def _with_extras(data, hh, vals):
    lane = _iota((1, LANES), 1)
    x = jnp.zeros_like(data)
    for e, v in enumerate(vals):
        x = jnp.where(lane == _extra_base(hh) + e, v, x)
    return jnp.where(_data_mask(hh), data, x)


def _fox_pack_fwd(q_raw, k_raw, v, cq, ck, qg, kg, gmat):
    scale2 = FOX_DH ** -0.5 * LOG2E

    def fn(row0, refs, bc):
        q_ref, k_ref, v_ref, cq_ref, ck_ref = refs
        g_q, g_k, gm = bc
        qv, kv = q_ref[...], k_ref[...]
        qn = qv * lax.rsqrt(_head_sum(qv * qv, gm) * (1.0 / FOX_DH) + EPS) * (g_q * scale2)
        kn = kv * lax.rsqrt(_head_sum(kv * kv, gm) * (1.0 / FOX_DH) + EPS) * g_k
        qs, ks, vs = [], [], []
        for h in range(FOX_H):
            p, hh = divmod(h, 2)
            sl = slice(p * LANES, (p + 1) * LANES)
            cq3 = _split3(cq_ref[:, h:h + 1] * LOG2E)
            ck3 = _split3(ck_ref[:, h:h + 1] * (-LOG2E))
            qs.append(_with_extras(qn[:, sl], hh, [*cq3, 1.0, 1.0, 1.0]))
            ks.append(_with_extras(kn[:, sl], hh, [1.0, 1.0, 1.0, *ck3]))
            vs.append(_with_extras(v_ref[:, sl].astype(F32), hh, [1.0, 1.0]))
        return [jnp.concatenate(qs, axis=1), jnp.concatenate(ks, axis=1), jnp.concatenate(vs, axis=1)], []

    w = FOX_H * LANES
    return _rowwise(fn, [q_raw, k_raw, v, cq, ck], [qg, kg, gmat], [(w, BF16)] * 3, [], name="fox_pack_fwd",
                    as_refs=True)[0]


def _fox_pack_bias(qp, cq, lse2):
    def fn(row0, refs, bc):
        q_ref, cq_ref, lse_ref = refs
        lane = _iota((1, LANES), 1)
        outs = []
        for h in range(FOX_H):
            blk = q_ref[:, h * LANES:(h + 1) * LANES].astype(F32)
            for e, part in enumerate(_split3(cq_ref[:, h:h + 1] * LOG2E - lse_ref[:, h:h + 1])):
                blk = jnp.where(lane == _extra_base(h % 2) + e, part, blk)
            outs.append(blk)
        return [jnp.concatenate(outs, axis=1)], []
    return _rowwise(fn, [qp, cq, lse2], [], [(FOX_H * LANES, BF16)], [], name="fox_pack_bias", as_refs=True)[0][0]


def _fox_pack_bwd(dog, o, gate):
    def fn(row0, refs, bc):
        d_ref, o_ref, g_ref = refs
        dos, dgs = [], []
        for p in range(FOX_H // 2):
            sl = slice(p * LANES, (p + 1) * LANES)
            dv, ov, gv = (r[:, sl].astype(F32) for r in (d_ref, o_ref, g_ref))
            s = _sigmoid(gv)
            do = dv * s
            dgs.append(dv * ov * s * (1.0 - s))
            od = ov * do
            for hh in range(2):
                delta = jnp.sum(jnp.where(_data_mask(hh), od, 0.0), axis=-1, keepdims=True)
                hi = delta.astype(BF16).astype(F32)
                dos.append(_with_extras(do, hh, [-hi, hi - delta]))
        return [jnp.concatenate(dos, axis=1), jnp.concatenate(dgs, axis=1)], []
    return _rowwise(fn, [dog, o, gate], [], [(FOX_H * LANES, BF16), (D, BF16)], [], name="fox_pack_bwd",
                    as_refs=True)[0]


def _fox_unpack_bwd(q_raw, k_raw, dqp, dk, qg, kg, gmat):
    scale = FOX_DH ** -0.5

    def fn(row0, refs, bc):
        q_ref, k_ref, dq_ref, dk_ref = refs
        g_q, g_k, gm = bc
        lane = _iota((1, LANES), 1)
        dqs = []
        dcq = jnp.zeros((q_ref.shape[0], LANES), F32)
        for p in range(FOX_H // 2):
            even = dq_ref[:, (2 * p) * LANES:(2 * p + 1) * LANES]
            odd = dq_ref[:, (2 * p + 1) * LANES:(2 * p + 2) * LANES]
            dqs.append(jnp.where(_data_mask(0), even, odd) * scale)
            for hh in range(2):
                col = (2 * p + hh) * LANES + _extra_base(hh)
                dcq = jnp.where(lane == 2 * p + hh, dq_ref[:, col:col + 1], dcq)
        outs, accs = [], []
        for xv, dy, g in ((q_ref[...], jnp.concatenate(dqs, axis=1), g_q), (k_ref[...], dk_ref[...] * (1.0 / LOG2E), g_k)):
            r = lax.rsqrt(_head_sum(xv * xv, gm) * (1.0 / FOX_DH) + EPS)
            xh = xv * r
            dxh = dy * g
            outs.append(r * (dxh - xh * (_head_sum(dxh * xh, gm) * (1.0 / FOX_DH))))
            accs.append(jnp.sum(dy * xh, axis=0, keepdims=True))
        return outs + [dcq], accs
    return _rowwise(fn, [q_raw, k_raw, dqp, dk], [qg, kg, gmat], [(D, BF16), (D, BF16), (LANES, F32)],
                    [(1, D), (1, D)], name="fox_unpack_bwd", as_refs=True)


def _fox_cumsum_fwd(flog, bf):
    def fn(row0, ins, bc, carry):
        (f,), (b,) = ins, bc
        tm = f.shape[0]
        keep = _row_ids(row0, tm) >= ROW0
        lf = jnp.where(keep, _log_sigmoid(f + b), 0.0)
        tri = (_iota((tm, tm), 0) >= _iota((tm, tm), 1)).astype(F32)
        c = jnp.dot(tri, lf, precision=HIGHEST, preferred_element_type=F32) + carry[...]
        carry[...] = carry[...] + jnp.sum(lf, axis=0, keepdims=True)
        return [c, jnp.where(keep, c, BIG)], []
    return _rowwise(fn, [flog], [bf], [(LANES, F32), (LANES, F32)], [], name="fox_cumsum_fwd",
                    carry=(1, LANES))[0]


def _fox_cumsum_bwd(dc_q, dc_k, flog, bf):
    def fn(row0, ins, bc, carry):
        (dq, dk, f), (b,) = ins, bc
        d = dq + dk
        tm = f.shape[0]
        keep = _row_ids(row0, tm) >= ROW0
        triu = (_iota((tm, tm), 0) <= _iota((tm, tm), 1)).astype(F32)
        dlf = jnp.dot(triu, d, precision=HIGHEST, preferred_element_type=F32) + carry[...]
        carry[...] = carry[...] + jnp.sum(d, axis=0, keepdims=True)
        dfl = jnp.where(keep, dlf * _sigmoid(-(f + b)), 0.0)
        return [dfl], [jnp.sum(dfl, axis=0, keepdims=True)]
    (dflog,), (dbf,) = _rowwise(fn, [dc_q, dc_k, flog], [bf], [(LANES, F32)], [(1, LANES)], name="fox_cumsum_bwd",
                                reverse=True, carry=(1, LANES))
    return dflog, dbf


def _causal_steps(n, key_major):
    if key_major:
        pairs = [(i, j) for j in range(n) for i in range(j, n)]
    else:
        pairs = [(i, j) for i in range(n) for j in range(i + 1)]
    return (jnp.asarray(np.array([p[0] for p in pairs], np.int32)),
            jnp.asarray(np.array([p[1] for p in pairs], np.int32)))


def _fox_attn_fwd(qp, kp, vp, gate, shards):
    L = qp.shape[0]
    t = _tile(L, ATTN_TILES)
    n = L // t
    hps = FOX_HPS_FWD
    P = FOX_H // hps
    it, jt = _causal_steps(n, False)
    n_steps = it.shape[0]
    ns = len(shards)

    def body(it_ref, jt_ref, q_ref, k_ref, v_ref, g_ref, *rest):
        sh_in, (o_ref, og_ref, lse_ref), sh_out = rest[:ns], rest[ns:ns + 3], rest[ns + 3:2 * ns + 3]
        m_sc, acc, ssem, rsem, lsem = rest[2 * ns + 3:]
        step = pl.program_id(1)
        i, j = it_ref[step], jt_ref[step]
        first = (pl.program_id(0) == 0) & (step == 0)
        last = (pl.program_id(0) == P - 1) & (step == n_steps - 1)

        @pl.when(first)
        def _():
            _gather_start(sh_in, sh_out, ssem, rsem, lsem)

        @pl.when(j == 0)
        def _():
            m_sc[...] = jnp.full_like(m_sc, -3.0e38)
            acc[...] = jnp.zeros_like(acc)

        def update(masked):
            def head(hh):
                sl = slice(hh * LANES, (hh + 1) * LANES)
                s2 = lax.dot_general(k_ref[:, sl], q_ref[:, sl], NT, preferred_element_type=F32)
                if masked:
                    s2 = jnp.where(_iota((t, t), 0) <= _iota((t, t), 1), s2, -jnp.inf)
                yield
                m_old = m_sc[hh]
                m_new = jnp.maximum(m_old, jnp.max(s2, axis=0, keepdims=True))
                yield
                p = jnp.exp2(s2 - m_new).astype(BF16)
                yield
                acc[hh] = jnp.exp2(m_old - m_new) * acc[hh] + lax.dot_general(v_ref[:, sl], p, TN,
                                                                              preferred_element_type=F32)
                m_sc[hh] = m_new

            _interleave((head(hh) for hh in range(hps)), skew=1)

        @pl.when(j < i)
        def _():
            update(False)

        @pl.when(j == i)
        def _():
            update(True)
            outs = []
            for hh in range(hps):
                base = _extra_base(hh % 2)
                l = acc[hh, base:base + 1, :]
                outs.append((acc[hh] / l).T)
                lse_ref[0, hh:hh + 1, :] = m_sc[hh] + jnp.log2(l)
            o = jnp.concatenate([jnp.where(_data_mask(0), outs[a], outs[a + 1]) for a in range(0, hps, 2)], axis=1)
            o_ref[...] = o.astype(o_ref.dtype)
            og_ref[...] = (o * _sigmoid(g_ref[...].astype(F32))).astype(og_ref.dtype)

        @pl.when(last)
        def _():
            _gather_wait(sh_in, sh_out, ssem, rsem, lsem)

    qspec = pl.BlockSpec((t, hps * LANES), lambda p, s, it, jt: (it[s], p))
    kspec = pl.BlockSpec((t, hps * LANES), lambda p, s, it, jt: (jt[s], p))
    ospec = pl.BlockSpec((t, hps * FOX_DH), lambda p, s, it, jt: (it[s], p))
    lspec = pl.BlockSpec((1, hps, t), lambda p, s, it, jt: (p, 0, it[s]))
    res = pl.pallas_call(
        body, name="fox_attn_fwd",
        grid_spec=pltpu.PrefetchScalarGridSpec(
            num_scalar_prefetch=2, grid=(P, n_steps),
            in_specs=[qspec, kspec, kspec, ospec] + [ANY] * ns, out_specs=[ospec, ospec, lspec] + [ANY] * ns,
            scratch_shapes=[pltpu.VMEM((hps, 1, t), F32), pltpu.VMEM((hps, LANES, t), F32)] + _gather_sems(ns)),
        out_shape=[jax.ShapeDtypeStruct((L, D), BF16), jax.ShapeDtypeStruct((L, D), BF16),
                   jax.ShapeDtypeStruct((P, hps, L), F32)]
        + [jax.ShapeDtypeStruct((4,) + a.shape, a.dtype) for a in shards],
        compiler_params=_cparams(("arbitrary", "arbitrary")),
    )(it, jt, qp, kp, vp, gate, *shards)
    return res[0], res[1], res[2], res[3:]


def _fox_attn_bwd(qb, kp, vp, dop, slabs):
    L = qb.shape[0]
    t = _tile(L, ATTN_TILES)
    n = L // t
    hps = FOX_HPS_BWD
    P = FOX_H // hps
    it, jt = _causal_steps(n, True)
    n_steps = it.shape[0]
    ns = len(slabs)

    def body(it_ref, jt_ref, q_ref, k_ref, v_ref, do_ref, *rest):
        sl_in, (dq_ref, dk_ref, dv_ref, dck_ref), sl_out = rest[:ns], rest[ns:ns + 4], rest[ns + 4:2 * ns + 4]
        dk_acc, dv_acc, ssem, rsem = rest[2 * ns + 4:]
        step = pl.program_id(1)
        i, j = it_ref[step], jt_ref[step]

        @pl.when((pl.program_id(0) == 0) & (step == 0))
        def _():
            for cp in _scatter_copies(sl_in, sl_out, ssem, rsem):
                cp.start()

        @pl.when(step == 0)
        def _():
            dq_ref[...] = jnp.zeros_like(dq_ref)

        @pl.when(i == j)
        def _():
            dk_acc[...] = jnp.zeros_like(dk_acc)
            dv_acc[...] = jnp.zeros_like(dv_acc)

        def update(masked):
            rows = pl.ds(pl.multiple_of(i * t, LANES), t)

            for hh in range(hps):
                sl = slice(hh * LANES, (hh + 1) * LANES)
                q, k, dov = q_ref[:, sl], k_ref[:, sl], do_ref[:, sl]
                s2 = lax.dot_general(k, q, NT, preferred_element_type=F32)
                if masked:
                    s2 = jnp.where(_iota((t, t), 0) <= _iota((t, t), 1), s2, -jnp.inf)
                p = jnp.exp2(s2)
                ds = (p * lax.dot_general(v_ref[:, sl], dov, NT, preferred_element_type=F32)).astype(BF16)
                dv_acc[hh] += jnp.dot(p.astype(BF16), dov, preferred_element_type=F32)
                dk_acc[hh] += jnp.dot(ds, q, preferred_element_type=F32)
                dq_ref[rows, sl] += lax.dot_general(ds, k, TN, preferred_element_type=F32)

        @pl.when(i > j)
        def _():
            update(False)

        @pl.when(i == j)
        def _():
            update(True)

        @pl.when(i == n - 1)
        def _():
            pairs = range(0, hps, 2)
            dk_ref[...] = jnp.concatenate([jnp.where(_data_mask(0), dk_acc[a], dk_acc[a + 1]) for a in pairs], axis=1)
            dv_ref[...] = jnp.concatenate([jnp.where(_data_mask(0), dv_acc[a], dv_acc[a + 1]) for a in pairs],
                                          axis=1).astype(dv_ref.dtype)
            lane = _iota((1, hps), 1)
            col_sums = jnp.zeros((t, hps), F32)
            for hh in range(hps):
                base = _extra_base(hh % 2) + 3
                col_sums = jnp.where(lane == hh, dk_acc[hh, :, base:base + 1], col_sums)
            dck_ref[0] = -col_sums

        @pl.when((pl.program_id(0) == P - 1) & (step == n_steps - 1))
        def _():
            for cp in _scatter_copies(sl_in, sl_out, ssem, rsem):
                cp.wait()

    qspec = pl.BlockSpec((t, hps * LANES), lambda p, s, it, jt: (it[s], p))
    kspec = pl.BlockSpec((t, hps * LANES), lambda p, s, it, jt: (jt[s], p))
    ospec = pl.BlockSpec((t, hps * FOX_DH), lambda p, s, it, jt: (jt[s], p))
    slab = pl.BlockSpec((L, hps * LANES), lambda p, s, it, jt: (0, p), pipeline_mode=pl.Buffered(1))
    res = pl.pallas_call(
        body, name="fox_attn_bwd",
        grid_spec=pltpu.PrefetchScalarGridSpec(
            num_scalar_prefetch=2, grid=(P, n_steps),
            in_specs=[qspec, kspec, kspec, qspec] + [ANY] * ns,
            out_specs=[slab, ospec, ospec,
                       pl.BlockSpec((1, t, hps), lambda p, s, it, jt: (p, jt[s], 0))] + [ANY] * ns,
            scratch_shapes=[pltpu.VMEM((hps, t, LANES), F32), pltpu.VMEM((hps, t, LANES), F32),
                            pltpu.SemaphoreType.DMA((ns, 3)), pltpu.SemaphoreType.DMA((ns, 3))]),
        out_shape=[jax.ShapeDtypeStruct((L, FOX_H * LANES), F32), jax.ShapeDtypeStruct((L, D), F32),
                   jax.ShapeDtypeStruct((L, D), BF16), jax.ShapeDtypeStruct((P, L, hps), F32)]
        + [jax.ShapeDtypeStruct((3,) + a.shape[1:], a.dtype) for a in slabs],
        compiler_params=_cparams(("arbitrary", "arbitrary")),
    )(it, jt, qb, kp, vp, dop, *slabs)
    return res[0], res[1], res[2], res[3], res[4:]


def _hgrn_consts():
    C = HG_C
    r = np.arange(C)[:, None]
    j = np.arange(C)[None, :]
    mats = [j <= r, j > r]
    masks = []
    n = C
    while n >= 2:
        half = n // 2
        mid = (r // n) * n + half - 1
        second = (r % n) >= half
        mats.append(np.where(second, (j > mid) & (j <= r), (j > r) & (j <= mid)))
        masks.append(((r // n) == (j // n)) & ((r % n) >= half) & ((j % n) < half))
        n //= 2
    return (jnp.asarray(np.concatenate(mats, 0).astype(np.float32), BF16),
            jnp.asarray(np.stack(masks).astype(np.float32), F32))


def _hg_pre(hq, hz, h0, h1):
    mx = jnp.maximum(h0, h1)
    e0, e1 = jnp.exp(h0 - mx), jnp.exp(h1 - mx)
    lb = e1 / (e0 + e1)
    sq = _sigmoid(hq)
    sz = _sigmoid(hz)
    snz = 1.0 - sz
    k = (1.0 - lb) * snz
    g = jnp.maximum(jnp.log(lb + (1.0 - lb) * sz), -BIG)
    return lb, hq * sq, sq, k, sz, snz, g


def _hg_decays(g, rmat):
    hi = g.astype(BF16)
    lo = (g - hi.astype(F32)).astype(BF16)
    d = jnp.dot(rmat, jnp.concatenate([hi, lo], axis=1), preferred_element_type=F32)
    return jnp.exp(d[:, :HG_D] + d[:, HG_D:])


def _interleave(programs, skew=0):
    progs = list(programs)
    done = [False] * len(progs)
    tick = 0
    while not all(done):
        for n, g in enumerate(progs):
            if not done[n] and tick >= n * skew:
                try:
                    next(g)
                except StopIteration:
                    done[n] = True
        tick += 1


def _hg_intra_levels(q, k, fall, masks):
    C = HG_C
    eye = _iota((C, C), 0) == _iota((C, C), 1)
    a = jnp.where(eye, jnp.sum(q * k, axis=-1, keepdims=True), 0.0)
    for l in range(HG_LEV):
        f = fall[(2 + l) * C:(3 + l) * C]
        a = a + masks[l] * lax.dot_general((q * f).astype(BF16), (k * f).astype(BF16), NT,
                                           preferred_element_type=F32)
        yield a


def _hgrn_specs(n_chunks, reverse):
    C = HG_C
    w = HG_HPS * HG_D

    def col(first_head):
        off = first_head // HG_HPS
        if reverse:
            return pl.BlockSpec((C, w), lambda h, c: (n_chunks - 1 - c, off + h))
        return pl.BlockSpec((C, w), lambda h, c: (c, off + h))

    st = pl.BlockSpec((HG_HPS, 1, HG_D, HG_D),
                      (lambda h, c: (h, n_chunks - 1 - c, 0, 0)) if reverse else (lambda h, c: (h, c, 0, 0)))
    consts = [pl.BlockSpec((2, w), lambda h, c: (0, h)), pl.BlockSpec((1, HG_D), lambda h, c: (0, 0)),
              pl.BlockSpec(((2 + HG_LEV) * C, C), lambda h, c: (0, 0)),
              pl.BlockSpec((HG_LEV, C, C), lambda h, c: (0, 0, 0))]
    return col, st, consts


def _hgrn_fwd(proj, hlb, gg, rmat, masks):
    L = proj.shape[0]
    C = HG_C
    nc = L // C
    col, st, consts = _hgrn_specs(nc, False)

    def body(hq_ref, hz_ref, hi_ref, hg_ref, hlb_ref, gg_ref, r_ref, m_ref, og_ref, st_ref, state):
        c = pl.program_id(1)

        @pl.when(c == 0)
        def _():
            state[...] = jnp.zeros_like(state)

        def head(hh):
            sl = slice(hh * HG_D, (hh + 1) * HG_D)
            v, hg = hi_ref[:, sl], hg_ref[:, sl]
            _, q, _, k, _, _, g = _hg_pre(hq_ref[:, sl], hz_ref[:, sl], hlb_ref[0:1, sl], hlb_ref[1:2, sl])
            yield
            fall = _hg_decays(g, r_ref[...])
            fb, fe = fall[0:C], fall[C:2 * C]
            st0 = state[hh]
            st_ref[hh, 0] = st0
            yield
            for a in _hg_intra_levels(q, k, fall, m_ref[...]):
                yield
            vb = v.astype(BF16)
            o = jnp.dot(a.astype(BF16), vb, preferred_element_type=F32)
            o = o + lax.dot_general((q * fb).astype(BF16), st0.astype(BF16), NT, preferred_element_type=F32)
            yield
            ebc = jnp.exp(jnp.sum(g, axis=0, keepdims=True))
            state[hh] = st0 * ebc + lax.dot_general(vb, (k * fe).astype(BF16), TN, preferred_element_type=F32)
            r = lax.rsqrt(jnp.mean(o * o, axis=-1, keepdims=True) + EPS)
            og_ref[:, sl] = (o * r * gg_ref[...] * (hg * _sigmoid(hg))).astype(og_ref.dtype)

        _interleave((head(hh) for hh in range(HG_HPS)), skew=HG_SKEW)

    return pl.pallas_call(
        body, name="hgrn_fwd", grid=(HG_H // HG_HPS, nc),
        in_specs=[col(0), col(HG_H), col(2 * HG_H), col(3 * HG_H)] + consts,
        out_specs=[col(0), st],
        out_shape=[jax.ShapeDtypeStruct((L, D), BF16), jax.ShapeDtypeStruct((HG_H, nc, HG_D, HG_D), F32)],
        scratch_shapes=[pltpu.VMEM((HG_HPS, HG_D, HG_D), F32)],
        compiler_params=_cparams(("parallel", "arbitrary")),
    )(proj, proj, proj, proj, hlb, gg, rmat, masks)


def _hgrn_bwd(proj, dog, states, hlb, gg, rmat, masks):
    L = proj.shape[0]
    C = HG_C
    nc = L // C
    col, st, consts = _hgrn_specs(nc, True)

    def body(hq_ref, hz_ref, hi_ref, hg_ref, do_ref, hlb_ref, gg_ref, r_ref, m_ref, st_ref,
             dq_ref, dz_ref, di_ref, dg_ref, dlb_ref, dgg_ref, dstate):
        c = pl.program_id(1)

        @pl.when(c == 0)
        def _():
            dstate[...] = jnp.zeros_like(dstate)
            dlb_ref[...] = jnp.zeros_like(dlb_ref)
            dgg_ref[...] = jnp.zeros_like(dgg_ref)

        _interleave([bwd_head(c, hh, slice(hh * HG_D, (hh + 1) * HG_D), hq_ref, hz_ref, hi_ref, hg_ref, do_ref, hlb_ref,
                              gg_ref, r_ref, m_ref, st_ref, dq_ref, dz_ref, di_ref, dg_ref, dlb_ref, dgg_ref, dstate)
                     for hh in range(HG_HPS)], skew=HG_SKEW)

    def bwd_head(c, hh, sl, hq_ref, hz_ref, hi_ref, hg_ref, do_ref, hlb_ref, gg_ref, r_ref, m_ref, st_ref,
                 dq_ref, dz_ref, di_ref, dg_ref, dlb_ref, dgg_ref, dstate):
        hq, hz, v, hg = hq_ref[:, sl], hz_ref[:, sl], hi_ref[:, sl], hg_ref[:, sl]
        dout = do_ref[:, sl].astype(F32)
        gain = gg_ref[...]
        masks_v = m_ref[...]
        lb, q, sq, k, sz, snz, g = _hg_pre(hq, hz, hlb_ref[0:1, sl], hlb_ref[1:2, sl])
        yield
        fall = _hg_decays(g, r_ref[...])
        fb, fe = fall[0:C], fall[C:2 * C]
        yield
        for a in _hg_intra_levels(q, k, fall, masks_v):
            yield
        st0 = st_ref[hh, 0]
        st0b = st0.astype(BF16)
        ebc = jnp.exp(jnp.sum(g, axis=0, keepdims=True))
        qb, ke, vb = (q * fb).astype(BF16), (k * fe).astype(BF16), v.astype(BF16)
        ab = a.astype(BF16)
        o = jnp.dot(ab, vb, preferred_element_type=F32) + lax.dot_general(qb, st0b, NT, preferred_element_type=F32)
        yield
        r = lax.rsqrt(jnp.mean(o * o, axis=-1, keepdims=True) + EPS)
        oh = o * r
        sg = _sigmoid(hg)
        d_on = dout * (hg * sg)
        dhg = dout * (oh * gain) * (sg * (1.0 + hg * (1.0 - sg)))
        dgg_ref[hh] += jnp.sum(d_on * oh, axis=0, keepdims=True)
        dxh = d_on * gain
        do = r * (dxh - oh * jnp.mean(dxh * oh, axis=-1, keepdims=True))
        dob = do.astype(BF16)
        yield
        dsp = dstate[hh]
        dspb = dsp.astype(BF16)
        causal = _iota((C, C), 0) >= _iota((C, C), 1)
        da = jnp.where(causal, lax.dot_general(dob, vb, NT, preferred_element_type=F32), 0.0)
        diag = jnp.sum(do * v, axis=-1, keepdims=True)
        yield
        dv = lax.dot_general(ab, dob, TN, preferred_element_type=F32)
        dv = dv + lax.dot_general(ke, dspb, NT, preferred_element_type=F32)
        yield
        xq = jnp.dot(dob, st0b, preferred_element_type=F32)
        xk = jnp.dot(vb, dspb, preferred_element_type=F32)
        dq = diag * k + fb * xq
        dk = diag * q + fe * xk
        ke_xk = ke.astype(F32) * xk
        db = qb.astype(F32) * xq - ke_xk
        yield
        for l in range(HG_LEV):
            f = fall[(2 + l) * C:(3 + l) * C]
            dal = (masks_v[l] * da).astype(BF16)
            ql, kl = (q * f).astype(BF16), (k * f).astype(BF16)
            xq = jnp.dot(dal, kl, preferred_element_type=F32)
            xk = lax.dot_general(dal, ql, TN, preferred_element_type=F32)
            dq = dq + f * xq
            dk = dk + f * xk
            db = db + ql.astype(F32) * xq - kl.astype(F32) * xk
            yield
        dstate[hh] = dsp * ebc + lax.dot_general(dob, qb, TN, preferred_element_type=F32)
        triu = (_iota((C, C), 0) <= _iota((C, C), 1)).astype(F32)
        dg = jnp.dot(triu, db, precision=HIGHEST, preferred_element_type=F32)
        dg = dg + jnp.sum(st0 * ebc * dsp, axis=0, keepdims=True) + jnp.sum(ke_xk, axis=0, keepdims=True)
        keep = _row_ids((nc - 1 - c) * C, C) >= ROW0
        dg = jnp.where(keep, dg, 0.0)
        dk = jnp.where(keep, dk, 0.0)
        f_gate = lb + (1.0 - lb) * sz
        dfdz = (1.0 - lb) * sz * snz
        dz_ref[:, sl] = (dg * dfdz / f_gate - dk * dfdz).astype(dz_ref.dtype)
        dlb_ref[:, sl] += jnp.sum(dg * snz / f_gate - dk * snz, axis=0, keepdims=True)
        dq_ref[:, sl] = jnp.where(keep, dq * (sq * (1.0 + hq * (1.0 - sq))), 0.0).astype(dq_ref.dtype)
        di_ref[:, sl] = jnp.where(keep, dv, 0.0).astype(di_ref.dtype)
        dg_ref[:, sl] = jnp.where(keep, dhg, 0.0).astype(dg_ref.dtype)

    w = HG_HPS * HG_D
    outs = pl.pallas_call(
        body, name="hgrn_bwd", grid=(HG_H // HG_HPS, nc),
        in_specs=[col(0), col(HG_H), col(2 * HG_H), col(3 * HG_H), col(0)] + consts + [st],
        out_specs=[col(0), col(0), col(0), col(0), pl.BlockSpec((1, w), lambda h, c: (0, h)),
                   pl.BlockSpec((HG_HPS, 1, HG_D), lambda h, c: (h, 0, 0))],
        out_shape=[jax.ShapeDtypeStruct((L, D), BF16)] * 4 + [jax.ShapeDtypeStruct((1, D), F32),
                                                              jax.ShapeDtypeStruct((HG_H, 1, HG_D), F32)],
        scratch_shapes=[pltpu.VMEM((HG_HPS, HG_D, HG_D), F32)],
        compiler_params=_cparams(("parallel", "arbitrary")),
    )(proj, proj, proj, proj, dog, hlb, gg, rmat, masks, states)
    return outs


def _ffn_fwd(h, norm_gain, wg, wu, wo, tag):
    hn, g, u, act = _norm_proj(h, norm_gain, [wg, wu], [BF16, BF16], f"{tag}_in", swiglu=True)
    h_out = _matmul(act, wo, add=h, name=f"{tag}_out")
    return h_out, (h, hn, g, u, act)


def _ffn_bwd(dh, saved, norm_gain, wg, wu, wo, tag):
    h, hn, g, u, act = saved
    dg, du = _ffn_dact(dh, wo, g, u, f"{tag}_dact")
    d_wo = _matmul(act, dh, ta=True, name=f"{tag}_dwo")
    d_wg = _matmul(hn, dg, ta=True, name=f"{tag}_dwg")
    d_wu = _matmul(hn, du, ta=True, name=f"{tag}_dwu")
    dh, d_gain, _ = _dhn_norm([dg, du], [wg, wu], h, dh, norm_gain, f"{tag}_dhn_norm")
    return dh, d_gain, (d_wg, d_wu, d_wo)


def _local_step(h0, tgt, w, late_shards):
    L = h0.shape[0]
    gmat = jnp.asarray(np.kron(np.eye(MXU_N // FOX_DH), np.ones((FOX_DH, FOX_DH))).astype(np.float32), BF16)
    rmat, lmasks = _hgrn_consts()
    an, fn_ = w["attn_norm"], w["ffn_norm"]
    qg = jnp.tile(w["fox_q_norm"], (1, FOX_H))
    kg = jnp.tile(w["fox_k_norm"], (1, FOX_H))
    bf = jnp.pad(w["fox_b_f"], ((0, 0), (0, LANES - FOX_H)))
    fw = w["fox_w_in"]
    f_wq, f_wk, f_wv, f_wg = (fw[:, i * D:(i + 1) * D] for i in range(4))
    f_wf = jnp.pad(fw[:, 4 * D:], ((0, 0), (0, LANES - FOX_H)))

    hn0, q_raw, k_raw, v, gate, flog = _norm_proj(h0, an[0:1], [f_wq, f_wk, f_wv, f_wg, f_wf],
                                                  [F32, F32, BF16, BF16, F32], "fox_in")
    cq, ck = _fox_cumsum_fwd(flog, bf)
    qp, kp, vp = _fox_pack_fwd(q_raw, k_raw, v, cq, ck, qg, kg, gmat)
    o, og, lse2, gathered = _fox_attn_fwd(qp, kp, vp, gate, [late_shards[n] for n in GATHER_LATE])
    late = dict(zip(GATHER_LATE, gathered))
    f_wo = late["fox_w_out"].reshape(D, D)
    h_wo = late["hgrn_w_out"].reshape(D, D)
    h_wi = jnp.concatenate(list(late["hgrn_w_in"]), axis=1)
    g_in, g_out = late["ffn_w_in"], late["ffn_w_out"]
    ffw = []
    for i in range(2):
        rows_in, rows_out = slice(i * D, (i + 1) * D), slice(i * FFN // 4, (i + 1) * FFN // 4)
        ffw.append((jnp.concatenate([g_in[0, rows_in], g_in[1, rows_in]], axis=1),
                    jnp.concatenate([g_in[2, rows_in], g_in[3, rows_in]], axis=1),
                    jnp.concatenate([g_out[j, rows_out] for j in range(4)], axis=0)))
    h1 = _matmul(og, f_wo, add=h0, name="fox_out")
    h2, ffn0 = _ffn_fwd(h1, fn_[0:1], *ffw[0], "ffn0")

    hn2, proj = _norm_proj(h2, an[1:2], [h_wi], [F32], "hgrn_in")
    og1, states = _hgrn_fwd(proj, w["hgrn_lower_bounds"], w["hgrn_g_norm"], rmat, lmasks)
    h3 = _matmul(og1, h_wo, add=h2, name="hgrn_out")
    h4, ffn1 = _ffn_fwd(h3, fn_[1:2], *ffw[1], "ffn1")

    loss, dh, d_final = _loss_bwd(h4, tgt, w["final_norm"])

    dh, d_fn1, d_ffn1 = _ffn_bwd(dh, ffn1, fn_[1:2], *ffw[1], "ffn1")
    dog1 = _matmul(dh, h_wo, tb=True, out_dtype=BF16, name="hgrn_dog")
    d_h_wo = _matmul(og1, dh, ta=True, name="hgrn_dwo")
    dpq, dpz, dpi, dpg, d_lb, d_gg = _hgrn_bwd(proj, dog1, states, w["hgrn_lower_bounds"], w["hgrn_g_norm"],
                                               rmat, lmasks)
    dproj = jnp.concatenate([dpq, dpz, dpi, dpg], axis=1)
    d_h_wi = _matmul(hn2, dproj, ta=True, name="hgrn_dwi")
    dh, d_an1, _ = _dhn_norm([dproj], [h_wi], h2, dh, an[1:2], "hgrn_dhn_norm")

    dh, d_fn0, d_ffn0 = _ffn_bwd(dh, ffn0, fn_[0:1], *ffw[0], "ffn0")
    n_in, n_out = 2 * FFN // 4, FFN // 4
    d_ffn = [d_ffn0, d_ffn1]
    late_grads = dict(
        hgrn_w_in=_to_shards("hgrn_w_in", d_h_wi[None]), hgrn_w_out=d_h_wo.reshape(4, D // 4, D),
        ffn_w_in=jnp.stack([jnp.concatenate([d[j // 2][:, (j % 2) * n_in:(j % 2 + 1) * n_in] for d in d_ffn], axis=0)
                            for j in range(4)]),
        ffn_w_out=jnp.stack([jnp.concatenate([d[2][j * n_out:(j + 1) * n_out] for d in d_ffn], axis=0)
                             for j in range(4)]))
    pair_late, send_late = _pair_sums([late_grads[n] for n in LATE_NAMES], "late")

    dog = _matmul(dh, f_wo, tb=True, out_dtype=BF16, name="fox_dog")
    d_f_wo = _matmul(og, dh, ta=True, name="fox_dwo")

    def by_head(a):
        return jnp.pad(a.transpose(1, 0, 2).reshape(L, FOX_H), ((0, 0), (0, LANES - FOX_H)))

    qb = _fox_pack_bias(qp, cq, by_head(lse2.transpose(0, 2, 1)))
    dop, dgate = _fox_pack_bwd(dog, o, gate)
    dqp, dk, dv, dck, recv_late = _fox_attn_bwd(qb, kp, vp, dop, send_late)
    (dq_raw, dk_raw, dc_q), (d_qg, d_kg) = _fox_unpack_bwd(q_raw, k_raw, dqp, dk, qg, kg, gmat)
    dflog, d_bf = _fox_cumsum_bwd(dc_q, by_head(dck), flog, bf)
    dproj0 = jnp.concatenate([dq_raw, dk_raw, dv, dgate, dflog.astype(BF16)], axis=1)
    f_wall = jnp.concatenate([f_wq, f_wk, f_wv, f_wg, f_wf], axis=1)
    d_f_wall = _matmul(hn0, dproj0, ta=True, name="fox_dwi")
    d_f_wi = d_f_wall[:, :4 * D + FOX_H]
    fox_grads = dict(fox_w_in=d_f_wi[None], fox_w_out=d_f_wo[None])
    pair_fox, send_fox = _pair_sums([_to_shards(n, fox_grads[n]) for n in FOX_NAMES], "fox")
    dh, d_an0, recv_fox = _dhn_norm([dproj0], [f_wall], h0, dh, an[0:1], "fox_dhn_norm", slabs=send_fox)
    halves = _chip_sums(pair_fox, recv_fox, "fox") + _chip_sums(pair_late, recv_late, "late")
    theirs = _sibling_exchange(halves)
    big = {n: (m, t) for n, m, t in zip(FOX_NAMES + LATE_NAMES, halves, theirs)}
    small = dict(attn_norm=jnp.concatenate([d_an0, d_an1]), ffn_norm=jnp.concatenate([d_fn0, d_fn1]),
                 final_norm=d_final, lb_raw=d_lb, q_gain=d_qg, k_gain=d_kg, b_f=d_bf,
                 g_gain=d_gg.reshape(1, D))
    return loss, dh, big, small


def _me():
    return lax.axis_index("x"), lax.axis_index("y"), lax.axis_index("c")


def _flip(v, bit):
    return 1 - v if bit else v


def _chip_allgather_split(big, small):
    half = big.shape[0] // 2

    def body(big_in, small_in, big_out, small_out, ssem, rsem, fs_sem, fr_sem, ssem2, rsem2, lsem):
        x, y, c = _me()
        sib = (x, y, 1 - c)
        peers = _chip_peers()
        mine, other = pl.ds(c * half, half), pl.ds((1 - c) * half, half)
        local = [pltpu.make_async_copy(big_in, big_out.at[2 * x + y], lsem.at[0]),
                 pltpu.make_async_copy(small_in, small_out.at[2 * x + y], lsem.at[1])]
        sends = []
        for k, peer in enumerate(peers):
            sends.append(pltpu.make_async_remote_copy(big_in.at[mine], big_out.at[2 * x + y, mine], ssem.at[k],
                                                      rsem.at[k], device_id=peer, device_id_type=MESH))
            sends.append(pltpu.make_async_remote_copy(small_in, small_out.at[2 * x + y], ssem2.at[k], rsem2.at[k],
                                                      device_id=peer, device_id_type=MESH))
        for cp in local + sends:
            cp.start()
        forwards = []
        for k, peer in enumerate(peers):
            landed = big_out.at[2 * peer[0] + peer[1], mine]
            pltpu.make_async_remote_copy(big_in.at[mine], landed, ssem.at[k], rsem.at[k],
                                         device_id=peer, device_id_type=MESH).wait_recv()
            fwd = pltpu.make_async_remote_copy(landed, landed, fs_sem.at[k], fr_sem.at[k],
                                               device_id=sib, device_id_type=MESH)
            fwd.start()
            forwards.append(fwd)
        for k, peer in enumerate(peers):
            theirs = big_out.at[2 * peer[0] + peer[1], other]
            pltpu.make_async_remote_copy(theirs, theirs, fs_sem.at[k], fr_sem.at[k],
                                         device_id=sib, device_id_type=MESH).wait_recv()
            pltpu.make_async_remote_copy(small_in, small_out.at[2 * peer[0] + peer[1]], ssem2.at[k], rsem2.at[k],
                                         device_id=peer, device_id_type=MESH).wait_recv()
        for cp in sends + forwards:
            cp.wait_send()
        for cp in local:
            cp.wait()

    three = pltpu.SemaphoreType.DMA((3,))
    return pl.pallas_call(
        body, name="chip_allgather", in_specs=[ANY, ANY], out_specs=[ANY, ANY],
        out_shape=[jax.ShapeDtypeStruct((4,) + big.shape, big.dtype),
                   jax.ShapeDtypeStruct((4,) + small.shape, small.dtype)],
        scratch_shapes=[three, three, three, three, three, three, pltpu.SemaphoreType.DMA((2,))],
    )(big, small)


def _chip_peers():
    x, y, c = _me()
    return [(1 - x, y, c), (x, 1 - y, c), (1 - x, 1 - y, c)]


def _gather_sems(n):
    return [pltpu.SemaphoreType.DMA((n, 3)), pltpu.SemaphoreType.DMA((n, 3)), pltpu.SemaphoreType.DMA((n,))]


def _gather_copies(ins, outs, ssem, rsem, lsem, with_recvs):
    x, y, _ = _me()
    local, sends, recvs = [], [], []
    for a in range(len(ins)):
        local.append(pltpu.make_async_copy(ins[a], outs[a].at[2 * x + y], lsem.at[a]))
        for k, peer in enumerate(_chip_peers()):
            sends.append(pltpu.make_async_remote_copy(ins[a], outs[a].at[2 * x + y], ssem.at[a, k], rsem.at[a, k],
                                                      device_id=peer, device_id_type=MESH))
            if with_recvs:
                recvs.append(pltpu.make_async_remote_copy(ins[a], outs[a].at[2 * peer[0] + peer[1]], ssem.at[a, k],
                                                          rsem.at[a, k], device_id=peer, device_id_type=MESH))
    return local, sends, recvs


def _gather_start(ins, outs, ssem, rsem, lsem):
    local, sends, _ = _gather_copies(ins, outs, ssem, rsem, lsem, False)
    for cp in local + sends:
        cp.start()


def _gather_wait(ins, outs, ssem, rsem, lsem):
    local, sends, recvs = _gather_copies(ins, outs, ssem, rsem, lsem, True)
    for cp in local:
        cp.wait()
    for cp in sends:
        cp.wait_send()
    for cp in recvs:
        cp.wait_recv()


def _scatter_copies(ins, outs, ssem, rsem):
    copies = []
    for a in range(len(ins)):
        for k, peer in enumerate(_chip_peers()):
            copies.append(pltpu.make_async_remote_copy(ins[a].at[2 * peer[0] + peer[1]], outs[a].at[k], ssem.at[a, k],
                                                       rsem.at[a, k], device_id=peer, device_id_type=MESH))
    return copies


def _device_allgather(arr):
    def body(in_ref, out_ref, ssem, rsem, lsem):
        x, y, c = _me()
        me = 4 * x + 2 * y + c
        peers = [(_flip(x, k & 4), _flip(y, k & 2), _flip(c, k & 1)) for k in range(1, 8)]
        local = pltpu.make_async_copy(in_ref, out_ref.at[me], lsem)
        local.start()
        sends = []
        for k, peer in enumerate(peers):
            cp = pltpu.make_async_remote_copy(in_ref, out_ref.at[me], ssem.at[k], rsem.at[k],
                                              device_id=peer, device_id_type=MESH)
            cp.start()
            sends.append(cp)
        local.wait()
        for cp in sends:
            cp.wait_send()
        for k, peer in enumerate(peers):
            pltpu.make_async_remote_copy(in_ref, out_ref.at[4 * peer[0] + 2 * peer[1] + peer[2]], ssem.at[k],
                                         rsem.at[k], device_id=peer, device_id_type=MESH).wait_recv()

    return pl.pallas_call(
        body, name="device_allgather", in_specs=[ANY], out_specs=ANY,
        out_shape=jax.ShapeDtypeStruct((8,) + arr.shape, arr.dtype),
        scratch_shapes=[pltpu.SemaphoreType.DMA((7,)), pltpu.SemaphoreType.DMA((7,)), pltpu.SemaphoreType.DMA],
    )(arr)


def _sibling_send_other_half(arrs, tag):
    n = len(arrs)

    def body(*refs):
        ins, outs = refs[:n], refs[n:2 * n]
        ssem, rsem = refs[2 * n:]
        x, y, c = _me()
        cps = []
        for a in range(n):
            half = ins[a].shape[1] // 2
            src = ins[a].at[:, pl.ds((1 - c) * half, half), :]
            cp = pltpu.make_async_remote_copy(src, outs[a], ssem.at[a], rsem.at[a],
                                              device_id=(x, y, 1 - c), device_id_type=MESH)
            cp.start()
            cps.append(cp)
        for cp in cps:
            cp.wait()

    return pl.pallas_call(
        body, name=f"grad_sibling_swap_{tag}", in_specs=[ANY] * n, out_specs=[ANY] * n,
        out_shape=[jax.ShapeDtypeStruct((4, a.shape[1] // 2, a.shape[2]), a.dtype) for a in arrs],
        scratch_shapes=[pltpu.SemaphoreType.DMA((n,)), pltpu.SemaphoreType.DMA((n,))],
    )(*arrs)


def _sibling_exchange(arrs):
    n = len(arrs)

    def body(*refs):
        ins, outs = refs[:n], refs[n:2 * n]
        ssem, rsem = refs[2 * n:]
        x, y, c = _me()
        cps = [pltpu.make_async_remote_copy(ins[a], outs[a], ssem.at[a], rsem.at[a], device_id=(x, y, 1 - c),
                                            device_id_type=MESH) for a in range(n)]
        for cp in cps:
            cp.start()
        for cp in cps:
            cp.wait()

    return pl.pallas_call(
        body, name="grad_sibling_exchange", in_specs=[ANY] * n, out_specs=[ANY] * n,
        out_shape=[jax.ShapeDtypeStruct(a.shape, a.dtype) for a in arrs],
        scratch_shapes=[pltpu.SemaphoreType.DMA((n,)), pltpu.SemaphoreType.DMA((n,))],
    )(*arrs)


def _pair_sums(grads, tag):
    got = _sibling_send_other_half(grads, tag)
    res = [_pair_add(g, t, f"grad_pair_add_{tag}{i}") for i, (g, t) in enumerate(zip(grads, got))]
    return [r[0] for r in res], [r[1] for r in res]


def _mesh_scalar(v):
    return jnp.asarray(v, jnp.int32).reshape(1)


def _pair_add(g, t, name):
    _, rows, cols = g.shape
    half = rows // 2
    tm = _tile(half, cap=(2 * 1024 * 1024) // (4 * cols))

    def body(c_ref, g_ref, t_ref, o_ref, ob_ref):
        s = g_ref[0, 0] + t_ref[0]
        o_ref[0] = s
        ob_ref[0] = s.astype(ob_ref.dtype)

    spec = pl.BlockSpec((1, tm, cols), lambda j, i, c: (j, i, 0))
    return pl.pallas_call(
        body, name=name,
        grid_spec=pltpu.PrefetchScalarGridSpec(
            num_scalar_prefetch=1, grid=(4, half // tm),
            in_specs=[pl.BlockSpec((1, 1, tm, cols), lambda j, i, c: (j, c[0], i, 0)), spec],
            out_specs=[spec, spec]),
        out_shape=[jax.ShapeDtypeStruct(t.shape, F32), jax.ShapeDtypeStruct(t.shape, BF16)],
        compiler_params=_cparams(("parallel", "parallel")),
    )(_mesh_scalar(lax.axis_index("c")), g.reshape(4, 2, half, cols), t)


def _chip_sums(pair, recv, tag):
    x, y, _ = _me()
    out = []
    for n, (p, r) in enumerate(zip(pair, recv)):
        _, half, cols = p.shape
        tm = _tile(half, cap=(2 * 1024 * 1024) // (4 * cols))

        def body(j_ref, p_ref, r_ref, o_ref):
            o_ref[...] = p_ref[0] + r_ref[0].astype(F32) + r_ref[1].astype(F32) + r_ref[2].astype(F32)

        out.append(pl.pallas_call(
            body, name=f"grad_chip_add_{tag}{n}",
            grid_spec=pltpu.PrefetchScalarGridSpec(
                num_scalar_prefetch=1, grid=(half // tm,),
                in_specs=[pl.BlockSpec((1, tm, cols), lambda i, j: (j[0], i, 0)),
                          pl.BlockSpec((3, tm, cols), lambda i, j: (0, i, 0))],
                out_specs=pl.BlockSpec((tm, cols), lambda i, j: (i, 0))),
            out_shape=jax.ShapeDtypeStruct((half, cols), F32),
            compiler_params=_cparams(("parallel",)),
        )(_mesh_scalar(2 * x + y), p, r))
    return out


SMALL_ROWS = 32


def _small_finalize(gathered, hlb, fold64, fold128):
    def body(g_ref, hlb_ref, f64_ref, f128_ref, rows_ref, qk_ref, gg_ref, lb_ref):
        tot = g_ref[0]
        for d in range(1, 8):
            tot = tot + g_ref[d]
        rows_ref[...] = tot
        qk_ref[...] = jnp.dot(rows_ref[6:8, :], f64_ref[...], precision=HIGHEST, preferred_element_type=F32)
        gg_ref[...] = jnp.dot(rows_ref[9:10, :], f128_ref[...], precision=HIGHEST, preferred_element_type=F32)
        h0, h1 = hlb_ref[0:1, :], hlb_ref[1:2, :]
        mx = jnp.maximum(h0, h1)
        e0, e1 = jnp.exp(h0 - mx), jnp.exp(h1 - mx)
        lb = e1 / (e0 + e1)
        d1 = rows_ref[5:6, :] * lb * (1.0 - lb)
        lb_ref[...] = jnp.where(_iota((2, 1), 0) == 0, -d1, d1)

    return pl.pallas_call(
        body, name="small_finalize",
        out_shape=[jax.ShapeDtypeStruct((SMALL_ROWS, D), F32), jax.ShapeDtypeStruct((2, FOX_DH), F32),
                   jax.ShapeDtypeStruct((1, HG_D), F32), jax.ShapeDtypeStruct((2, D), F32)],
    )(gathered, hlb, fold64, fold128)


FOX_NAMES = ("fox_w_in", "fox_w_out")
LATE_NAMES = ("hgrn_w_in", "hgrn_w_out", "ffn_w_in", "ffn_w_out")
BIG_NAMES = FOX_NAMES + LATE_NAMES
GATHER_LATE = ("fox_w_out",) + LATE_NAMES
COL_SHARDED = ("fox_w_in", "hgrn_w_in", "ffn_w_in")


def _shard2d(name, a):
    return a.reshape(-1, a.shape[-1])


def _to_shards(name, g):
    layers = g.shape[0]
    if name in COL_SHARDED:
        k, n = g.shape[1], g.shape[2] // 4
        return g.reshape(layers, k, 4, n).transpose(2, 0, 1, 3).reshape(4, layers * k, n)
    r = g.shape[1] // 4
    return g.reshape(layers, 4, r, g.shape[2]).transpose(1, 0, 2, 3).reshape(4, layers * r, g.shape[2])


def kernel(x, meta_tokens, attn_norm, ffn_norm, final_norm, fox_w_in, fox_b_f, fox_q_norm, fox_k_norm, fox_w_out, hgrn_w_in, hgrn_lower_bounds, hgrn_g_norm, hgrn_w_out, ffn_w_in, ffn_w_out, loss_target, m_meta_tokens, m_attn_norm, m_ffn_norm, m_final_norm, m_fox_w_in, m_fox_b_f, m_fox_q_norm, m_fox_k_norm, m_fox_w_out, m_hgrn_w_in, m_hgrn_lower_bounds, m_hgrn_g_norm, m_hgrn_w_out, m_ffn_w_in, m_ffn_w_out, v_meta_tokens, v_attn_norm, v_ffn_norm, v_final_norm, v_fox_w_in, v_fox_b_f, v_fox_q_norm, v_fox_k_norm, v_fox_w_out, v_hgrn_w_in, v_hgrn_lower_bounds, v_hgrn_g_norm, v_hgrn_w_out, v_ffn_w_in, v_ffn_w_out):
    params = dict(meta_tokens=meta_tokens, attn_norm=attn_norm, ffn_norm=ffn_norm, final_norm=final_norm,
                  fox_w_in=fox_w_in, fox_b_f=fox_b_f, fox_q_norm=fox_q_norm, fox_k_norm=fox_k_norm,
                  fox_w_out=fox_w_out, hgrn_w_in=hgrn_w_in, hgrn_lower_bounds=hgrn_lower_bounds,
                  hgrn_g_norm=hgrn_g_norm, hgrn_w_out=hgrn_w_out, ffn_w_in=ffn_w_in, ffn_w_out=ffn_w_out)
    mom_m = dict(meta_tokens=m_meta_tokens, attn_norm=m_attn_norm, ffn_norm=m_ffn_norm, final_norm=m_final_norm,
                 fox_w_in=m_fox_w_in, fox_b_f=m_fox_b_f, fox_q_norm=m_fox_q_norm, fox_k_norm=m_fox_k_norm,
                 fox_w_out=m_fox_w_out, hgrn_w_in=m_hgrn_w_in, hgrn_lower_bounds=m_hgrn_lower_bounds,
                 hgrn_g_norm=m_hgrn_g_norm, hgrn_w_out=m_hgrn_w_out, ffn_w_in=m_ffn_w_in, ffn_w_out=m_ffn_w_out)
    mom_v = dict(meta_tokens=v_meta_tokens, attn_norm=v_attn_norm, ffn_norm=v_ffn_norm, final_norm=v_final_norm,
                 fox_w_in=v_fox_w_in, fox_b_f=v_fox_b_f, fox_q_norm=v_fox_q_norm, fox_k_norm=v_fox_k_norm,
                 fox_w_out=v_fox_w_out, hgrn_w_in=v_hgrn_w_in, hgrn_lower_bounds=v_hgrn_lower_bounds,
                 hgrn_g_norm=v_hgrn_g_norm, hgrn_w_out=v_hgrn_w_out, ffn_w_in=v_ffn_w_in, ffn_w_out=v_ffn_w_out)
    names = list(params)
    xi, yi, _ = _me()

    shards = {n: _shard2d(n, params[n]).astype(BF16) for n in BIG_NAMES}
    w_in_g, meta_g = _chip_allgather_split(shards["fox_w_in"], meta_tokens)
    w = dict(fox_w_in=jnp.concatenate(list(w_in_g), axis=1))
    meta_full = jnp.concatenate(list(meta_g), axis=1)
    w.update(attn_norm=attn_norm, ffn_norm=ffn_norm, final_norm=final_norm.reshape(1, D), fox_b_f=fox_b_f,
             fox_q_norm=fox_q_norm, fox_k_norm=fox_k_norm, hgrn_lower_bounds=hgrn_lower_bounds,
             hgrn_g_norm=hgrn_g_norm)

    h0 = jnp.concatenate([jnp.zeros((ROW0, D), F32), meta_full, x[0]], axis=0)
    loss, dh0, big, small = _local_step(h0, loss_target[0], w, {n: shards[n] for n in GATHER_LATE})
    loss = lax.psum(loss, ("x", "y", "c"))
    grad_x = dh0[PAD:][None]
    grads = {}

    rows = jnp.concatenate([small["attn_norm"], small["ffn_norm"], small["final_norm"], small["lb_raw"],
                            small["q_gain"], small["k_gain"],
                            jnp.pad(small["b_f"], ((0, 0), (0, D - LANES))), small["g_gain"],
                            dh0[ROW0:PAD], jnp.zeros((SMALL_ROWS - 10 - N_META, D), F32)], axis=0)
    allrows = _device_allgather(rows)
    fold64 = jnp.asarray(np.tile(np.eye(FOX_DH, dtype=np.float32), (FOX_H, 1)))
    fold128 = jnp.asarray(np.tile(np.eye(HG_D, dtype=np.float32), (HG_H, 1)))
    tot, qk, gg, dlb = _small_finalize(allrows, hgrn_lower_bounds, fold64, fold128)
    grads.update(attn_norm=tot[0:2], ffn_norm=tot[2:4], final_norm=tot[4], hgrn_lower_bounds=dlb,
                 fox_q_norm=qk[0:1], fox_k_norm=qk[1:2], fox_b_f=tot[8:9, :FOX_H], hgrn_g_norm=gg,
                 meta_tokens=lax.dynamic_slice_in_dim(tot[10:10 + N_META], (2 * xi + yi) * (D // 4), D // 4, axis=1))

    delta, new_m, new_v = {}, {}, {}
    for n in BIG_NAMES:
        res = _adamw_halves(_shard2d(n, params[n]), *big[n], _shard2d(n, mom_m[n]), _shard2d(n, mom_v[n]),
                            f"adamw_{n}")
        grads[n], delta[n], new_m[n], new_v[n] = (t.reshape(params[n].shape) for t in res)
    delta["meta_tokens"], new_m["meta_tokens"], new_v["meta_tokens"] = _adamw(
        meta_tokens, grads["meta_tokens"], m_meta_tokens, v_meta_tokens, "adamw_meta_tokens")
    small_names = [n for n in names if n not in BIG_NAMES and n != "meta_tokens"]

    def pack(d):
        return jnp.concatenate([jnp.pad(d[n].reshape(-1, d[n].shape[-1]), ((0, 0), (0, D - d[n].shape[-1])))
                                for n in small_names], axis=0)

    packed = [pack(t) for t in (params, grads, mom_m, mom_v)]
    n_rows = packed[0].shape[0]
    packed = [jnp.pad(t, ((0, 16 - n_rows), (0, 0))) for t in packed]
    res = _adamw(*packed, "adamw_small")
    r0 = 0
    for n in small_names:
        nr = params[n].reshape(-1, params[n].shape[-1]).shape[0]
        for dst, src in zip((delta, new_m, new_v), res):
            dst[n] = src[r0:r0 + nr, :params[n].shape[-1]].reshape(params[n].shape)
        r0 += nr

    return (loss, grad_x, *[grads[n] for n in names], *[delta[n] for n in names],
            *[new_m[n] for n in names], *[new_v[n] for n in names])
```

```python
import functools

import numpy as np
import jax
import jax.numpy as jnp
from jax import lax
from jax.experimental import pallas as pl
from jax.experimental.pallas import tpu as pltpu

F32, BF16 = jnp.float32, jnp.bfloat16
HIGHEST = lax.Precision.HIGHEST

D = 1024
N_META = 16
PAD = 128
ROW0 = PAD - N_META
FOX_H, FOX_DH = 16, 64
HG_H, HG_D = 8, 128
HG_C = 128
HG_LEV = 7
HG_HPS = 8
HG_SKEW = 0
FFN = 2816
EPS = 1e-6
BIG = 1e30
LOG2E = 1.4426950408889634
LANES = 128
MXU_N = 256
VMEM_LIMIT = 48 * 1024 * 1024
ROW_TILES = (640, 512, 384, 320, 256, 128, 64, 32, 16, 8)
ATTN_TILES = (640, 512, 256, 128)
FOX_HPS_FWD = 8
FOX_HPS_BWD = 4

ADAM_LR, ADAM_B1, ADAM_B2, ADAM_EPS, ADAM_WD, ADAM_STEP = 0.001, 0.9, 0.999, 1e-08, 0.01, 10

MESH = pl.DeviceIdType.MESH
ANY = pl.BlockSpec(memory_space=pl.ANY)
NT = (((1,), (1,)), ((), ()))
TN = (((0,), (0,)), ((), ()))


def _tile(n, cands=ROW_TILES, cap=None):
    for c in cands:
        if n % c == 0 and (cap is None or c <= cap):
            return c
    return n


def _cparams(sem):
    return pltpu.CompilerParams(dimension_semantics=sem, vmem_limit_bytes=VMEM_LIMIT)


def _sigmoid(x):
    return jax.nn.sigmoid(x)


def _log_sigmoid(x):
    return jnp.minimum(x, 0.0) - jnp.log(1.0 + jnp.exp(-jnp.abs(x)))


def _iota(shape, dim):
    return lax.broadcasted_iota(jnp.int32, shape, dim)


def _matmul(a, b, *, ta=False, tb=False, out_dtype=F32, add=None, name):
    if ta:
        kdim, m = a.shape
    else:
        m, kdim = a.shape
    n = b.shape[0] if tb else b.shape[1]
    if ta:
        tm = m if m <= 1024 else _tile(m, (1408, 1024, 512, 256, 128))
        tk = _tile(kdim, (1664,) + ROW_TILES)
    else:
        tm = _tile(m)
        tk = kdim if kdim <= 4096 else _tile(kdim, (2048, 1024, 512))
    tn = n if n <= 1024 else _tile(n, (1408, 1024, 512, 256, 128))
    nk = kdim // tk
    dn = (((0 if ta else 1,), (1 if tb else 0,)), ((), ()))

    def body(*refs):
        if add is None:
            a_ref, b_ref, o_ref, acc_ref = refs
        else:
            a_ref, b_ref, add_ref, o_ref, acc_ref = refs
        k = pl.program_id(2)

        @pl.when(k == 0)
        def _():
            acc_ref[...] = jnp.zeros_like(acc_ref)

        acc_ref[...] += lax.dot_general(a_ref[...].astype(BF16), b_ref[...].astype(BF16), dn,
                                        preferred_element_type=F32)

        @pl.when(k == nk - 1)
        def _():
            r = acc_ref[...]
            if add is not None:
                r = r + add_ref[...].astype(F32)
            o_ref[...] = r.astype(o_ref.dtype)

    a_spec = pl.BlockSpec((tk, tm), lambda j, i, k: (k, i)) if ta else pl.BlockSpec((tm, tk), lambda j, i, k: (i, k))
    b_spec = pl.BlockSpec((tn, tk), lambda j, i, k: (j, k)) if tb else pl.BlockSpec((tk, tn), lambda j, i, k: (k, j))
    o_spec = pl.BlockSpec((tm, tn), lambda j, i, k: (i, j))
    ins, specs = [a, b], [a_spec, b_spec]
    if add is not None:
        ins.append(add)
        specs.append(o_spec)
    return pl.pallas_call(
        body, name=name, grid=(n // tn, m // tm, nk), in_specs=specs, out_specs=o_spec,
        out_shape=jax.ShapeDtypeStruct((m, n), out_dtype),
        scratch_shapes=[pltpu.VMEM((tm, tn), F32)],
        compiler_params=_cparams(("parallel", "parallel", "arbitrary")),
    )(*ins)


def _rowwise(fn, ins, bcast, outs, accs, *, name, reverse=False, carry=None, as_refs=False):
    rows = ins[0].shape[0]
    per_row = sum(x.shape[1] * x.dtype.itemsize for x in ins) + sum(c * jnp.dtype(d).itemsize for c, d in outs)
    tm = _tile(rows, cap=max(8, (10 * 1024 * 1024) // per_row))
    n = rows // tm
    n_in, n_b, n_o, n_a = len(ins), len(bcast), len(outs), len(accs)

    def body(*refs):
        in_refs = refs[:n_in]
        b_refs = refs[n_in:n_in + n_b]
        o_refs = refs[n_in + n_b:n_in + n_b + n_o]
        a_refs = refs[n_in + n_b + n_o:n_in + n_b + n_o + n_a]
        c_refs = refs[n_in + n_b + n_o + n_a:]
        i = pl.program_id(0)
        blk = (n - 1 - i) if reverse else i
        if c_refs:
            @pl.when(i == 0)
            def _():
                c_refs[0][...] = jnp.zeros_like(c_refs[0])
        args = (list(in_refs) if as_refs else [r[...] for r in in_refs], [r[...] for r in b_refs])
        o_vals, a_vals = fn(blk * tm, *args, *c_refs)
        for r, v in zip(o_refs, o_vals):
            r[...] = v.astype(r.dtype)
        if n_a:
            @pl.when(i == 0)
            def _():
                for r in a_refs:
                    r[...] = jnp.zeros_like(r)
            for r, v in zip(a_refs, a_vals):
                r[...] += v

    def row_map(i):
        return ((n - 1 - i) if reverse else i, 0)

    in_specs = [pl.BlockSpec((tm, x.shape[1]), row_map) for x in ins]
    in_specs += [pl.BlockSpec(x.shape, lambda i, nd=x.ndim: (0,) * nd) for x in bcast]
    out_specs = [pl.BlockSpec((tm, c), row_map) for c, _ in outs]
    out_specs += [pl.BlockSpec(s, lambda i: (0, 0)) for s in accs]
    out_shape = [jax.ShapeDtypeStruct((rows, c), d) for c, d in outs]
    out_shape += [jax.ShapeDtypeStruct(s, F32) for s in accs]
    res = pl.pallas_call(
        body, name=name, grid=(n,), in_specs=in_specs, out_specs=out_specs, out_shape=out_shape,
        scratch_shapes=[pltpu.VMEM(carry, F32)] if carry else [],
        compiler_params=_cparams(("arbitrary",)),
    )(*ins, *bcast)
    return res[:n_o], res[n_o:]


def _row_ids(row0, tm):
    return row0 + _iota((tm, 1), 0)


def _rms_bwd_math(xv, dy, g):
    r = lax.rsqrt(jnp.mean(xv * xv, axis=-1, keepdims=True) + EPS)
    xh = xv * r
    dxh = dy * g
    dx = r * (dxh - xh * jnp.mean(dxh * xh, axis=-1, keepdims=True))
    return dx, jnp.sum(dy * xh, axis=0, keepdims=True)


def _loss_bwd(h, tgt, gain):
    tm = _tile(h.shape[0], (5 * PAD, PAD))
    n = h.shape[0] // tm
    pieces = tm // PAD

    def body(x_ref, *refs):
        t_refs, (g_ref, dx_ref, loss_ref, dg_ref) = refs[:pieces], refs[pieces:]
        i = pl.program_id(0)

        @pl.when(i == 0)
        def _():
            loss_ref[...] = jnp.zeros_like(loss_ref)
            dg_ref[...] = jnp.zeros_like(dg_ref)

        xv, g = x_ref[...], g_ref[...]
        r = lax.rsqrt(jnp.mean(xv * xv, axis=-1, keepdims=True) + EPS)
        xh = xv * r
        tv = jnp.concatenate([t[...] for t in t_refs], axis=0)
        err = jnp.where(_row_ids(i * tm, tm) >= PAD, xh * g - tv, 0.0)
        per_row = jnp.mean(err * err, axis=-1, keepdims=True)
        loss_ref[...] += jnp.broadcast_to(0.5 * jnp.sum(per_row, axis=0, keepdims=True), (1, LANES))
        dy = err * (1.0 / D)
        dxh = dy * g
        dx_ref[...] = r * (dxh - xh * jnp.mean(dxh * xh, axis=-1, keepdims=True))
        dg_ref[...] += jnp.sum(dy * xh, axis=0, keepdims=True)

    dh, loss, dgain = pl.pallas_call(
        body, name="loss_bwd", grid=(n,),
        in_specs=[pl.BlockSpec((tm, D), lambda i: (i, 0))]
        + [pl.BlockSpec((PAD, D), lambda i, k=k: (jnp.maximum(i * pieces + k - 1, 0), 0)) for k in range(pieces)]
        + [pl.BlockSpec((1, D), lambda i: (0, 0))],
        out_specs=[pl.BlockSpec((tm, D), lambda i: (i, 0)), pl.BlockSpec((1, LANES), lambda i: (0, 0)),
                   pl.BlockSpec((1, D), lambda i: (0, 0))],
        out_shape=[jax.ShapeDtypeStruct(h.shape, F32), jax.ShapeDtypeStruct((1, LANES), F32),
                   jax.ShapeDtypeStruct((1, D), F32)],
        compiler_params=_cparams(("arbitrary",)),
    )(h, *([tgt] * pieces), gain)
    return loss[0, 0], dh, dgain


FFN_TILES = dict(rows=(320, 256, 128), cols=(1408, 1024, 512, 256, 128))


def _norm_proj(h, gain, ws, dtypes, name, swiglu=False):
    m = h.shape[0]
    tm = _tile(m, FFN_TILES["rows"])
    n = len(ws)

    def body(h_ref, g_ref, *refs):
        w_refs, hn_ref, o_refs = refs[:n], refs[n], refs[n + 1:]
        x = h_ref[...]
        hn = (x * lax.rsqrt(jnp.mean(x * x, axis=-1, keepdims=True) + EPS) * g_ref[...]).astype(BF16)
        hn_ref[...] = hn
        prods = [jnp.dot(hn, w[...], preferred_element_type=F32) for w in w_refs]
        for o, p in zip(o_refs, prods):
            o[...] = p.astype(o.dtype)
        if swiglu:
            o_refs[n][...] = (prods[0] * _sigmoid(prods[0]) * prods[1]).astype(o_refs[n].dtype)

    rows = lambda c: pl.BlockSpec((tm, c), lambda i: (i, 0))
    cols = [w.shape[1] for w in ws] + ([ws[0].shape[1]] if swiglu else [])
    dts = list(dtypes) + ([BF16] if swiglu else [])
    return pl.pallas_call(
        body, name=name, grid=(m // tm,),
        in_specs=[rows(D), pl.BlockSpec((1, D), lambda i: (0, 0))]
        + [pl.BlockSpec(w.shape, lambda i: (0, 0), pipeline_mode=pl.Buffered(1)) for w in ws],
        out_specs=[rows(D)] + [rows(c) for c in cols],
        out_shape=[jax.ShapeDtypeStruct((m, D), BF16)] + [jax.ShapeDtypeStruct((m, c), d) for c, d in zip(cols, dts)],
        compiler_params=_cparams(("parallel",)),
    )(h, gain, *ws)


def _dhn_norm(dys, ws, h, dh_up, gain, name, slabs=()):
    m = h.shape[0]
    tm = _tile(m, FFN_TILES["rows"])
    n, ns = len(dys), len(slabs)
    steps = m // tm

    def body(*refs):
        dy_refs, w_refs = refs[:n], refs[n:2 * n]
        h_ref, up_ref, g_ref = refs[2 * n:2 * n + 3]
        sl_in = refs[2 * n + 3:2 * n + 3 + ns]
        dh_ref, dgain_ref = refs[2 * n + 3 + ns:2 * n + 5 + ns]
        sl_out = refs[2 * n + 5 + ns:2 * n + 5 + 2 * ns]
        sems = refs[2 * n + 5 + 2 * ns:]
        i = pl.program_id(0)

        @pl.when(i == 0)
        def _():
            dgain_ref[...] = jnp.zeros_like(dgain_ref)
            for cp in _scatter_copies(sl_in, sl_out, *sems) if ns else ():
                cp.start()

        dy = sum(lax.dot_general(a[...], w[...], NT, preferred_element_type=F32) for a, w in zip(dy_refs, w_refs))
        dx, dgain = _rms_bwd_math(h_ref[...], dy, g_ref[...])
        keep = _row_ids(i * tm, tm) >= ROW0
        dh_ref[...] = jnp.where(keep, up_ref[...] + dx, 0.0)
        dgain_ref[...] += dgain

        if ns:
            @pl.when(i == steps - 1)
            def _():
                for cp in _scatter_copies(sl_in, sl_out, *sems):
                    cp.wait()

    rows = lambda c: pl.BlockSpec((tm, c), lambda i: (i, 0))
    whole = lambda a: pl.BlockSpec(a.shape, lambda i: (0, 0), pipeline_mode=pl.Buffered(1))
    res = pl.pallas_call(
        body, name=name, grid=(steps,),
        in_specs=[rows(a.shape[1]) for a in dys] + [whole(w) for w in ws]
        + [rows(D), rows(D), pl.BlockSpec((1, D), lambda i: (0, 0))] + [ANY] * ns,
        out_specs=[rows(D), pl.BlockSpec((1, D), lambda i: (0, 0))] + [ANY] * ns,
        out_shape=[jax.ShapeDtypeStruct((m, D), F32), jax.ShapeDtypeStruct((1, D), F32)]
        + [jax.ShapeDtypeStruct((3,) + a.shape[1:], a.dtype) for a in slabs],
        scratch_shapes=[pltpu.SemaphoreType.DMA((ns, 3)), pltpu.SemaphoreType.DMA((ns, 3))] if ns else [],
        compiler_params=_cparams(("arbitrary",)),
    )(*dys, *ws, h, dh_up, gain, *slabs)
    return res[0], res[1], res[2:]


def _ffn_dact(dh, wo, g, u, name):
    m, kdim = dh.shape
    n = wo.shape[0]
    tm, tn = _tile(m, FFN_TILES["rows"]), _tile(n, FFN_TILES["cols"])

    def body(a_ref, w_ref, g_ref, u_ref, dg_ref, du_ref):
        da = lax.dot_general(a_ref[...].astype(BF16), w_ref[...], NT, preferred_element_type=F32)
        gv, uv = g_ref[...].astype(F32), u_ref[...].astype(F32)
        s = _sigmoid(gv)
        dg_ref[...] = (da * uv * (s * (1.0 + gv * (1.0 - s)))).astype(dg_ref.dtype)
        du_ref[...] = (da * gv * s).astype(du_ref.dtype)

    ospec = pl.BlockSpec((tm, tn), lambda j, i: (i, j))
    return pl.pallas_call(
        body, name=name, grid=(n // tn, m // tm),
        in_specs=[pl.BlockSpec((tm, kdim), lambda j, i: (i, 0)), pl.BlockSpec((tn, kdim), lambda j, i: (j, 0)),
                  ospec, ospec],
        out_specs=[ospec] * 2, out_shape=[jax.ShapeDtypeStruct((m, n), BF16)] * 2,
        compiler_params=_cparams(("parallel", "parallel")),
    )(dh, wo, g, u)


def _adamw_math(wv, gv, mv, vv):
    mn = ADAM_B1 * mv + (1.0 - ADAM_B1) * gv
    vn = ADAM_B2 * vv + (1.0 - ADAM_B2) * (gv * gv)
    m_hat = mn / (1.0 - ADAM_B1 ** ADAM_STEP)
    v_hat = vn / (1.0 - ADAM_B2 ** ADAM_STEP)
    return -ADAM_LR * (m_hat / (jnp.sqrt(v_hat) + ADAM_EPS) + ADAM_WD * wv), mn, vn


def _adamw(w, g, m, v, name):
    def fn(row0, ins, bc):
        return list(_adamw_math(*ins)), []
    c = w.shape[1]
    return _rowwise(fn, [w, g, m, v], [], [(c, F32)] * 3, [], name=name)[0]


def _adamw_halves(w, mine, theirs, m, v, name):
    rows, cols = w.shape
    half = rows // 2
    tm = _tile(half, cap=(10 * 1024 * 1024) // (9 * 4 * cols))
    nb = half // tm

    def body(c_ref, w_ref, g1_ref, g2_ref, m_ref, v_ref, g_out, d_out, m_out, v_out):
        own = (pl.program_id(0) // nb) == c_ref[0]
        g = jnp.where(own, g1_ref[...], g2_ref[...])
        delta, mn, vn = _adamw_math(w_ref[...], g, m_ref[...], v_ref[...])
        g_out[...] = g
        d_out[...] = delta
        m_out[...] = mn
        v_out[...] = vn

    full = pl.BlockSpec((tm, cols), lambda i, c: (i, 0))
    part = pl.BlockSpec((tm, cols), lambda i, c: (lax.rem(i, nb), 0))
    return pl.pallas_call(
        body, name=name,
        grid_spec=pltpu.PrefetchScalarGridSpec(num_scalar_prefetch=1, grid=(2 * nb,),
                                               in_specs=[full, part, part, full, full], out_specs=[full] * 4),
        out_shape=[jax.ShapeDtypeStruct((rows, cols), F32)] * 4,
        compiler_params=_cparams(("parallel",)),
    )(_mesh_scalar(lax.axis_index("c")), w, mine, theirs, m, v)


def _head_sum(x, gmat):
    hi = x.astype(BF16)
    lo = (x - hi.astype(F32)).astype(BF16)
    w = gmat.shape[0]
    return jnp.concatenate(
        [jnp.dot(hi[:, b:b + w], gmat, preferred_element_type=F32) + jnp.dot(lo[:, b:b + w], gmat,
                                                                             preferred_element_type=F32)
         for b in range(0, x.shape[1], w)], axis=1)


def _split3(x):
    hi = x.astype(BF16).astype(F32)
    r = x - hi
    mid = r.astype(BF16).astype(F32)
    return hi, mid, r - mid


def _extra_base(hh):
    return FOX_DH * (1 - hh)


def _data_mask(hh):
    lane = _iota((1, LANES), 1)
    return (lane >= FOX_DH * hh) & (lane < FOX_DH * (hh + 1))


def _with_extras(data, hh, vals):
    lane = _iota((1, LANES), 1)
    x = jnp.zeros_like(data)
    for e, v in enumerate(vals):
        x = jnp.where(lane == _extra_base(hh) + e, v, x)
    return jnp.where(_data_mask(hh), data, x)


def _fox_pack_fwd(q_raw, k_raw, v, cq, ck, qg, kg, gmat):
    scale2 = FOX_DH ** -0.5 * LOG2E

    def fn(row0, refs, bc):
        q_ref, k_ref, v_ref, cq_ref, ck_ref = refs
        g_q, g_k, gm = bc
        qv, kv = q_ref[...], k_ref[...]
        qn = qv * lax.rsqrt(_head_sum(qv * qv, gm) * (1.0 / FOX_DH) + EPS) * (g_q * scale2)
        kn = kv * lax.rsqrt(_head_sum(kv * kv, gm) * (1.0 / FOX_DH) + EPS) * g_k
        qs, ks, vs = [], [], []
        for h in range(FOX_H):
            p, hh = divmod(h, 2)
            sl = slice(p * LANES, (p + 1) * LANES)
            cq3 = _split3(cq_ref[:, h:h + 1] * LOG2E)
            ck3 = _split3(ck_ref[:, h:h + 1] * (-LOG2E))
            qs.append(_with_extras(qn[:, sl], hh, [*cq3, 1.0, 1.0, 1.0]))
            ks.append(_with_extras(kn[:, sl], hh, [1.0, 1.0, 1.0, *ck3]))
            vs.append(_with_extras(v_ref[:, sl].astype(F32), hh, [1.0, 1.0]))
        return [jnp.concatenate(qs, axis=1), jnp.concatenate(ks, axis=1), jnp.concatenate(vs, axis=1)], []

    w = FOX_H * LANES
    return _rowwise(fn, [q_raw, k_raw, v, cq, ck], [qg, kg, gmat], [(w, BF16)] * 3, [], name="fox_pack_fwd",
                    as_refs=True)[0]


def _fox_pack_bias(qp, cq, lse2):
    def fn(row0, refs, bc):
        q_ref, cq_ref, lse_ref = refs
        lane = _iota((1, LANES), 1)
        outs = []
        for h in range(FOX_H):
            blk = q_ref[:, h * LANES:(h + 1) * LANES].astype(F32)
            for e, part in enumerate(_split3(cq_ref[:, h:h + 1] * LOG2E - lse_ref[:, h:h + 1])):
                blk = jnp.where(lane == _extra_base(h % 2) + e, part, blk)
            outs.append(blk)
        return [jnp.concatenate(outs, axis=1)], []
    return _rowwise(fn, [qp, cq, lse2], [], [(FOX_H * LANES, BF16)], [], name="fox_pack_bias", as_refs=True)[0][0]


def _fox_pack_bwd(dog, o, gate):
    def fn(row0, refs, bc):
        d_ref, o_ref, g_ref = refs
        dos, dgs = [], []
        for p in range(FOX_H // 2):
            sl = slice(p * LANES, (p + 1) * LANES)
            dv, ov, gv = (r[:, sl].astype(F32) for r in (d_ref, o_ref, g_ref))
            s = _sigmoid(gv)
            do = dv * s
            dgs.append(dv * ov * s * (1.0 - s))
            od = ov * do
            for hh in range(2):
                delta = jnp.sum(jnp.where(_data_mask(hh), od, 0.0), axis=-1, keepdims=True)
                hi = delta.astype(BF16).astype(F32)
                dos.append(_with_extras(do, hh, [-hi, hi - delta]))
        return [jnp.concatenate(dos, axis=1), jnp.concatenate(dgs, axis=1)], []
    return _rowwise(fn, [dog, o, gate], [], [(FOX_H * LANES, BF16), (D, BF16)], [], name="fox_pack_bwd",
                    as_refs=True)[0]


def _fox_unpack_bwd(q_raw, k_raw, dqp, dk, qg, kg, gmat):
    scale = FOX_DH ** -0.5

    def fn(row0, refs, bc):
        q_ref, k_ref, dq_ref, dk_ref = refs
        g_q, g_k, gm = bc
        lane = _iota((1, LANES), 1)
        dqs = []
        dcq = jnp.zeros((q_ref.shape[0], LANES), F32)
        for p in range(FOX_H // 2):
            even = dq_ref[:, (2 * p) * LANES:(2 * p + 1) * LANES]
            odd = dq_ref[:, (2 * p + 1) * LANES:(2 * p + 2) * LANES]
            dqs.append(jnp.where(_data_mask(0), even, odd) * scale)
            for hh in range(2):
                col = (2 * p + hh) * LANES + _extra_base(hh)
                dcq = jnp.where(lane == 2 * p + hh, dq_ref[:, col:col + 1], dcq)
        outs, accs = [], []
        for xv, dy, g in ((q_ref[...], jnp.concatenate(dqs, axis=1), g_q), (k_ref[...], dk_ref[...] * (1.0 / LOG2E), g_k)):
            r = lax.rsqrt(_head_sum(xv * xv, gm) * (1.0 / FOX_DH) + EPS)
            xh = xv * r
            dxh = dy * g
            outs.append(r * (dxh - xh * (_head_sum(dxh * xh, gm) * (1.0 / FOX_DH))))
            accs.append(jnp.sum(dy * xh, axis=0, keepdims=True))
        return outs + [dcq], accs
    return _rowwise(fn, [q_raw, k_raw, dqp, dk], [qg, kg, gmat], [(D, BF16), (D, BF16), (LANES, F32)],
                    [(1, D), (1, D)], name="fox_unpack_bwd", as_refs=True)


def _fox_cumsum_fwd(flog, bf):
    def fn(row0, ins, bc, carry):
        (f,), (b,) = ins, bc
        tm = f.shape[0]
        keep = _row_ids(row0, tm) >= ROW0
        lf = jnp.where(keep, _log_sigmoid(f + b), 0.0)
        tri = (_iota((tm, tm), 0) >= _iota((tm, tm), 1)).astype(F32)
        c = jnp.dot(tri, lf, precision=HIGHEST, preferred_element_type=F32) + carry[...]
        carry[...] = carry[...] + jnp.sum(lf, axis=0, keepdims=True)
        return [c, jnp.where(keep, c, BIG)], []
    return _rowwise(fn, [flog], [bf], [(LANES, F32), (LANES, F32)], [], name="fox_cumsum_fwd",
                    carry=(1, LANES))[0]


def _fox_cumsum_bwd(dc_q, dc_k, flog, bf):
    def fn(row0, ins, bc, carry):
        (dq, dk, f), (b,) = ins, bc
        d = dq + dk
        tm = f.shape[0]
        keep = _row_ids(row0, tm) >= ROW0
        triu = (_iota((tm, tm), 0) <= _iota((tm, tm), 1)).astype(F32)
        dlf = jnp.dot(triu, d, precision=HIGHEST, preferred_element_type=F32) + carry[...]
        carry[...] = carry[...] + jnp.sum(d, axis=0, keepdims=True)
        dfl = jnp.where(keep, dlf * _sigmoid(-(f + b)), 0.0)
        return [dfl], [jnp.sum(dfl, axis=0, keepdims=True)]
    (dflog,), (dbf,) = _rowwise(fn, [dc_q, dc_k, flog], [bf], [(LANES, F32)], [(1, LANES)], name="fox_cumsum_bwd",
                                reverse=True, carry=(1, LANES))
    return dflog, dbf


def _causal_steps(n, key_major):
    if key_major:
        pairs = [(i, j) for j in range(n) for i in range(j, n)]
    else:
        pairs = [(i, j) for i in range(n) for j in range(i + 1)]
    return (jnp.asarray(np.array([p[0] for p in pairs], np.int32)),
            jnp.asarray(np.array([p[1] for p in pairs], np.int32)))


def _fox_attn_fwd(qp, kp, vp, gate, shards):
    L = qp.shape[0]
    t = _tile(L, ATTN_TILES)
    n = L // t
    hps = FOX_HPS_FWD
    P = FOX_H // hps
    it, jt = _causal_steps(n, False)
    n_steps = it.shape[0]
    ns = len(shards)

    def body(it_ref, jt_ref, q_ref, k_ref, v_ref, g_ref, *rest):
        sh_in, (o_ref, og_ref, lse_ref), sh_out = rest[:ns], rest[ns:ns + 3], rest[ns + 3:2 * ns + 3]
        m_sc, acc, ssem, rsem, lsem = rest[2 * ns + 3:]
        step = pl.program_id(1)
        i, j = it_ref[step], jt_ref[step]
        first = (pl.program_id(0) == 0) & (step == 0)
        last = (pl.program_id(0) == P - 1) & (step == n_steps - 1)

        @pl.when(first)
        def _():
            _gather_start(sh_in, sh_out, ssem, rsem, lsem)

        @pl.when(j == 0)
        def _():
            m_sc[...] = jnp.full_like(m_sc, -3.0e38)
            acc[...] = jnp.zeros_like(acc)

        def update(masked):
            cut = MXU_N if t > MXU_N else t
            blocks = [(0, cut, cut), (cut, t, t)] if masked and cut < t else [(0, t, t)]

            def head(hh, q0, q1, k1):
                sl = slice(hh * LANES, (hh + 1) * LANES)
                s2 = lax.dot_general(k_ref[0:k1, sl], q_ref[q0:q1, sl], NT, preferred_element_type=F32)
                if masked:
                    s2 = jnp.where(_iota((k1, q1 - q0), 0) <= _iota((k1, q1 - q0), 1) + q0, s2, -jnp.inf)
                yield
                m_old = m_sc[hh, :, q0:q1]
                m_new = jnp.maximum(m_old, jnp.max(s2, axis=0, keepdims=True))
                yield
                p = jnp.exp2(s2 - m_new).astype(BF16)
                yield
                acc[hh, :, q0:q1] = jnp.exp2(m_old - m_new) * acc[hh, :, q0:q1] + lax.dot_general(
                    v_ref[0:k1, sl], p, TN, preferred_element_type=F32)
                m_sc[hh, :, q0:q1] = m_new

            _interleave((head(hh, *b) for hh in range(hps) for b in blocks), skew=1)

        @pl.when(j < i)
        def _():
            update(False)

        @pl.when(j == i)
        def _():
            update(True)
            outs = []
            for hh in range(hps):
                base = _extra_base(hh % 2)
                l = acc[hh, base:base + 1, :]
                outs.append((acc[hh] / l).T)
                lse_ref[0, hh:hh + 1, :] = m_sc[hh] + jnp.log2(l)
            o = jnp.concatenate([jnp.where(_data_mask(0), outs[a], outs[a + 1]) for a in range(0, hps, 2)], axis=1)
            o_ref[...] = o.astype(o_ref.dtype)
            og_ref[...] = (o * _sigmoid(g_ref[...].astype(F32))).astype(og_ref.dtype)

        @pl.when(last)
        def _():
            _gather_wait(sh_in, sh_out, ssem, rsem, lsem)

    qspec = pl.BlockSpec((t, hps * LANES), lambda p, s, it, jt: (it[s], p))
    kspec = pl.BlockSpec((t, hps * LANES), lambda p, s, it, jt: (jt[s], p))
    ospec = pl.BlockSpec((t, hps * FOX_DH), lambda p, s, it, jt: (it[s], p))
    lspec = pl.BlockSpec((1, hps, t), lambda p, s, it, jt: (p, 0, it[s]))
    res = pl.pallas_call(
        body, name="fox_attn_fwd",
        grid_spec=pltpu.PrefetchScalarGridSpec(
            num_scalar_prefetch=2, grid=(P, n_steps),
            in_specs=[qspec, kspec, kspec, ospec] + [ANY] * ns, out_specs=[ospec, ospec, lspec] + [ANY] * ns,
            scratch_shapes=[pltpu.VMEM((hps, 1, t), F32), pltpu.VMEM((hps, LANES, t), F32)] + _gather_sems(ns)),
        out_shape=[jax.ShapeDtypeStruct((L, D), BF16), jax.ShapeDtypeStruct((L, D), BF16),
                   jax.ShapeDtypeStruct((P, hps, L), F32)]
        + [jax.ShapeDtypeStruct((4,) + a.shape, a.dtype) for a in shards],
        compiler_params=_cparams(("arbitrary", "arbitrary")),
    )(it, jt, qp, kp, vp, gate, *shards)
    return res[0], res[1], res[2], res[3:]


def _fox_attn_bwd(qb, kp, vp, dop, slabs):
    L = qb.shape[0]
    t = _tile(L, ATTN_TILES)
    n = L // t
    hps = FOX_HPS_BWD
    P = FOX_H // hps
    it, jt = _causal_steps(n, True)
    n_steps = it.shape[0]
    ns = len(slabs)

    def body(it_ref, jt_ref, q_ref, k_ref, v_ref, do_ref, *rest):
        sl_in, (dq_ref, dk_ref, dv_ref, dck_ref), sl_out = rest[:ns], rest[ns:ns + 4], rest[ns + 4:2 * ns + 4]
        dk_acc, dv_acc, ssem, rsem = rest[2 * ns + 4:]
        step = pl.program_id(1)
        i, j = it_ref[step], jt_ref[step]

        @pl.when((pl.program_id(0) == 0) & (step == 0))
        def _():
            for cp in _scatter_copies(sl_in, sl_out, ssem, rsem):
                cp.start()

        @pl.when(step == 0)
        def _():
            dq_ref[...] = jnp.zeros_like(dq_ref)

        @pl.when(i == j)
        def _():
            dk_acc[...] = jnp.zeros_like(dk_acc)
            dv_acc[...] = jnp.zeros_like(dv_acc)

        def update(masked):
            cut = MXU_N if t > MXU_N else t
            blocks = [(0, cut, cut), (cut, t, t)] if masked and cut < t else [(0, t, t)]
            for hh in range(hps):
                sl = slice(hh * LANES, (hh + 1) * LANES)
                for q0, q1, k1 in blocks:
                    q, dov, k = q_ref[q0:q1, sl], do_ref[q0:q1, sl], k_ref[0:k1, sl]
                    s2 = lax.dot_general(k, q, NT, preferred_element_type=F32)
                    if masked:
                        s2 = jnp.where(_iota((k1, q1 - q0), 0) <= _iota((k1, q1 - q0), 1) + q0, s2, -jnp.inf)
                    p = jnp.exp2(s2)
                    ds = (p * lax.dot_general(v_ref[0:k1, sl], dov, NT, preferred_element_type=F32)).astype(BF16)
                    dv_acc[hh, 0:k1] += jnp.dot(p.astype(BF16), dov, preferred_element_type=F32)
                    dk_acc[hh, 0:k1] += jnp.dot(ds, q, preferred_element_type=F32)
                    rows = pl.ds(pl.multiple_of(i * t + q0, LANES), q1 - q0)
                    dq_ref[rows, sl] += lax.dot_general(ds, k, TN, preferred_element_type=F32)

        @pl.when(i > j)
        def _():
            update(False)

        @pl.when(i == j)
        def _():
            update(True)

        @pl.when(i == n - 1)
        def _():
            pairs = range(0, hps, 2)
            dk_ref[...] = jnp.concatenate([jnp.where(_data_mask(0), dk_acc[a], dk_acc[a + 1]) for a in pairs], axis=1)
            dv_ref[...] = jnp.concatenate([jnp.where(_data_mask(0), dv_acc[a], dv_acc[a + 1]) for a in pairs],
                                          axis=1).astype(dv_ref.dtype)
            lane = _iota((1, hps), 1)
            col_sums = jnp.zeros((t, hps), F32)
            for hh in range(hps):
                base = _extra_base(hh % 2) + 3
                col_sums = jnp.where(lane == hh, dk_acc[hh, :, base:base + 1], col_sums)
            dck_ref[0] = -col_sums

        @pl.when((pl.program_id(0) == P - 1) & (step == n_steps - 1))
        def _():
            for cp in _scatter_copies(sl_in, sl_out, ssem, rsem):
                cp.wait()

    qspec = pl.BlockSpec((t, hps * LANES), lambda p, s, it, jt: (it[s], p))
    kspec = pl.BlockSpec((t, hps * LANES), lambda p, s, it, jt: (jt[s], p))
    ospec = pl.BlockSpec((t, hps * FOX_DH), lambda p, s, it, jt: (jt[s], p))
    slab = pl.BlockSpec((L, hps * LANES), lambda p, s, it, jt: (0, p), pipeline_mode=pl.Buffered(1))
    res = pl.pallas_call(
        body, name="fox_attn_bwd",
        grid_spec=pltpu.PrefetchScalarGridSpec(
            num_scalar_prefetch=2, grid=(P, n_steps),
            in_specs=[qspec, kspec, kspec, qspec] + [ANY] * ns,
            out_specs=[slab, ospec, ospec,
                       pl.BlockSpec((1, t, hps), lambda p, s, it, jt: (p, jt[s], 0))] + [ANY] * ns,
            scratch_shapes=[pltpu.VMEM((hps, t, LANES), F32), pltpu.VMEM((hps, t, LANES), F32),
                            pltpu.SemaphoreType.DMA((ns, 3)), pltpu.SemaphoreType.DMA((ns, 3))]),
        out_shape=[jax.ShapeDtypeStruct((L, FOX_H * LANES), F32), jax.ShapeDtypeStruct((L, D), F32),
                   jax.ShapeDtypeStruct((L, D), BF16), jax.ShapeDtypeStruct((P, L, hps), F32)]
        + [jax.ShapeDtypeStruct((3,) + a.shape[1:], a.dtype) for a in slabs],
        compiler_params=_cparams(("arbitrary", "arbitrary")),
    )(it, jt, qb, kp, vp, dop, *slabs)
    return res[0], res[1], res[2], res[3], res[4:]


def _hgrn_consts():
    C = HG_C
    r = np.arange(C)[:, None]
    j = np.arange(C)[None, :]
    mats = [j <= r, j > r]
    masks = []
    n = C
    while n >= 2:
        half = n // 2
        mid = (r // n) * n + half - 1
        second = (r % n) >= half
        mats.append(np.where(second, (j > mid) & (j <= r), (j > r) & (j <= mid)))
        masks.append(((r // n) == (j // n)) & ((r % n) >= half) & ((j % n) < half))
        n //= 2
    return (jnp.asarray(np.concatenate(mats, 0).astype(np.float32), BF16),
            jnp.asarray(np.stack(masks).astype(np.float32), F32))


def _hg_pre(hq, hz, h0, h1):
    mx = jnp.maximum(h0, h1)
    e0, e1 = jnp.exp(h0 - mx), jnp.exp(h1 - mx)
    lb = e1 / (e0 + e1)
    sq = _sigmoid(hq)
    sz = _sigmoid(hz)
    snz = 1.0 - sz
    k = (1.0 - lb) * snz
    g = jnp.maximum(jnp.log(lb + (1.0 - lb) * sz), -BIG)
    return lb, hq * sq, sq, k, sz, snz, g


def _hg_decays(g, rmat):
    hi = g.astype(BF16)
    lo = (g - hi.astype(F32)).astype(BF16)
    d = jnp.dot(rmat, jnp.concatenate([hi, lo], axis=1), preferred_element_type=F32)
    return jnp.exp(d[:, :HG_D] + d[:, HG_D:])


def _interleave(programs, skew=0):
    progs = list(programs)
    done = [False] * len(progs)
    tick = 0
    while not all(done):
        for n, g in enumerate(progs):
            if not done[n] and tick >= n * skew:
                try:
                    next(g)
                except StopIteration:
                    done[n] = True
        tick += 1


def _hg_intra_levels(q, k, fall, masks):
    C = HG_C
    eye = _iota((C, C), 0) == _iota((C, C), 1)
    a = jnp.where(eye, jnp.sum(q * k, axis=-1, keepdims=True), 0.0)
    for l in range(HG_LEV):
        f = fall[(2 + l) * C:(3 + l) * C]
        a = a + masks[l] * lax.dot_general((q * f).astype(BF16), (k * f).astype(BF16), NT,
                                           preferred_element_type=F32)
        yield a


def _hgrn_specs(n_chunks, reverse):
    C = HG_C
    w = HG_HPS * HG_D

    def col(first_head):
        off = first_head // HG_HPS
        if reverse:
            return pl.BlockSpec((C, w), lambda h, c: (n_chunks - 1 - c, off + h))
        return pl.BlockSpec((C, w), lambda h, c: (c, off + h))

    st = pl.BlockSpec((HG_HPS, 1, HG_D, HG_D),
                      (lambda h, c: (h, n_chunks - 1 - c, 0, 0)) if reverse else (lambda h, c: (h, c, 0, 0)))
    consts = [pl.BlockSpec((2, w), lambda h, c: (0, h)), pl.BlockSpec((1, HG_D), lambda h, c: (0, 0)),
              pl.BlockSpec(((2 + HG_LEV) * C, C), lambda h, c: (0, 0)),
              pl.BlockSpec((HG_LEV, C, C), lambda h, c: (0, 0, 0))]
    return col, st, consts


def _hgrn_fwd(proj, hlb, gg, rmat, masks):
    L = proj.shape[0]
    C = HG_C
    nc = L // C
    col, st, consts = _hgrn_specs(nc, False)

    def body(hq_ref, hz_ref, hi_ref, hg_ref, hlb_ref, gg_ref, r_ref, m_ref, og_ref, st_ref, state):
        c = pl.program_id(1)

        @pl.when(c == 0)
        def _():
            state[...] = jnp.zeros_like(state)

        def head(hh):
            sl = slice(hh * HG_D, (hh + 1) * HG_D)
            v, hg = hi_ref[:, sl], hg_ref[:, sl]
            _, q, _, k, _, _, g = _hg_pre(hq_ref[:, sl], hz_ref[:, sl], hlb_ref[0:1, sl], hlb_ref[1:2, sl])
            yield
            fall = _hg_decays(g, r_ref[...])
            fb, fe = fall[0:C], fall[C:2 * C]
            st0 = state[hh]
            st_ref[hh, 0] = st0
            yield
            for a in _hg_intra_levels(q, k, fall, m_ref[...]):
                yield
            vb = v.astype(BF16)
            o = jnp.dot(a.astype(BF16), vb, preferred_element_type=F32)
            o = o + lax.dot_general((q * fb).astype(BF16), st0.astype(BF16), NT, preferred_element_type=F32)
            yield
            ebc = jnp.exp(jnp.sum(g, axis=0, keepdims=True))
            state[hh] = st0 * ebc + lax.dot_general(vb, (k * fe).astype(BF16), TN, preferred_element_type=F32)
            r = lax.rsqrt(jnp.mean(o * o, axis=-1, keepdims=True) + EPS)
            og_ref[:, sl] = (o * r * gg_ref[...] * (hg * _sigmoid(hg))).astype(og_ref.dtype)

        _interleave((head(hh) for hh in range(HG_HPS)), skew=HG_SKEW)

    return pl.pallas_call(
        body, name="hgrn_fwd", grid=(HG_H // HG_HPS, nc),
        in_specs=[col(0), col(HG_H), col(2 * HG_H), col(3 * HG_H)] + consts,
        out_specs=[col(0), st],
        out_shape=[jax.ShapeDtypeStruct((L, D), BF16), jax.ShapeDtypeStruct((HG_H, nc, HG_D, HG_D), F32)],
        scratch_shapes=[pltpu.VMEM((HG_HPS, HG_D, HG_D), F32)],
        compiler_params=_cparams(("parallel", "arbitrary")),
    )(proj, proj, proj, proj, hlb, gg, rmat, masks)


def _hgrn_bwd(proj, dog, states, hlb, gg, rmat, masks):
    L = proj.shape[0]
    C = HG_C
    nc = L // C
    col, st, consts = _hgrn_specs(nc, True)

    def body(hq_ref, hz_ref, hi_ref, hg_ref, do_ref, hlb_ref, gg_ref, r_ref, m_ref, st_ref,
             dq_ref, dz_ref, di_ref, dg_ref, dlb_ref, dgg_ref, dstate):
        c = pl.program_id(1)

        @pl.when(c == 0)
        def _():
            dstate[...] = jnp.zeros_like(dstate)
            dlb_ref[...] = jnp.zeros_like(dlb_ref)
            dgg_ref[...] = jnp.zeros_like(dgg_ref)

        _interleave([bwd_head(c, hh, slice(hh * HG_D, (hh + 1) * HG_D), hq_ref, hz_ref, hi_ref, hg_ref, do_ref, hlb_ref,
                              gg_ref, r_ref, m_ref, st_ref, dq_ref, dz_ref, di_ref, dg_ref, dlb_ref, dgg_ref, dstate)
                     for hh in range(HG_HPS)], skew=HG_SKEW)

    def bwd_head(c, hh, sl, hq_ref, hz_ref, hi_ref, hg_ref, do_ref, hlb_ref, gg_ref, r_ref, m_ref, st_ref,
                 dq_ref, dz_ref, di_ref, dg_ref, dlb_ref, dgg_ref, dstate):
        hq, hz, v, hg = hq_ref[:, sl], hz_ref[:, sl], hi_ref[:, sl], hg_ref[:, sl]
        dout = do_ref[:, sl].astype(F32)
        gain = gg_ref[...]
        masks_v = m_ref[...]
        lb, q, sq, k, sz, snz, g = _hg_pre(hq, hz, hlb_ref[0:1, sl], hlb_ref[1:2, sl])
        yield
        fall = _hg_decays(g, r_ref[...])
        fb, fe = fall[0:C], fall[C:2 * C]
        yield
        for a in _hg_intra_levels(q, k, fall, masks_v):
            yield
        st0 = st_ref[hh, 0]
        st0b = st0.astype(BF16)
        ebc = jnp.exp(jnp.sum(g, axis=0, keepdims=True))
        qb, ke, vb = (q * fb).astype(BF16), (k * fe).astype(BF16), v.astype(BF16)
        ab = a.astype(BF16)
        o = jnp.dot(ab, vb, preferred_element_type=F32) + lax.dot_general(qb, st0b, NT, preferred_element_type=F32)
        yield
        r = lax.rsqrt(jnp.mean(o * o, axis=-1, keepdims=True) + EPS)
        oh = o * r
        sg = _sigmoid(hg)
        d_on = dout * (hg * sg)
        dhg = dout * (oh * gain) * (sg * (1.0 + hg * (1.0 - sg)))
        dgg_ref[hh] += jnp.sum(d_on * oh, axis=0, keepdims=True)
        dxh = d_on * gain
        do = r * (dxh - oh * jnp.mean(dxh * oh, axis=-1, keepdims=True))
        dob = do.astype(BF16)
        yield
        dsp = dstate[hh]
        dspb = dsp.astype(BF16)
        causal = _iota((C, C), 0) >= _iota((C, C), 1)
        da = jnp.where(causal, lax.dot_general(dob, vb, NT, preferred_element_type=F32), 0.0)
        diag = jnp.sum(do * v, axis=-1, keepdims=True)
        yield
        dv = lax.dot_general(ab, dob, TN, preferred_element_type=F32)
        dv = dv + lax.dot_general(ke, dspb, NT, preferred_element_type=F32)
        yield
        xq = jnp.dot(dob, st0b, preferred_element_type=F32)
        xk = jnp.dot(vb, dspb, preferred_element_type=F32)
        dq = diag * k + fb * xq
        dk = diag * q + fe * xk
        ke_xk = ke.astype(F32) * xk
        db = qb.astype(F32) * xq - ke_xk
        yield
        for l in range(HG_LEV):
            f = fall[(2 + l) * C:(3 + l) * C]
            dal = (masks_v[l] * da).astype(BF16)
            ql, kl = (q * f).astype(BF16), (k * f).astype(BF16)
            xq = jnp.dot(dal, kl, preferred_element_type=F32)
            xk = lax.dot_general(dal, ql, TN, preferred_element_type=F32)
            dq = dq + f * xq
            dk = dk + f * xk
            db = db + ql.astype(F32) * xq - kl.astype(F32) * xk
            yield
        dstate[hh] = dsp * ebc + lax.dot_general(dob, qb, TN, preferred_element_type=F32)
        triu = (_iota((C, C), 0) <= _iota((C, C), 1)).astype(F32)
        dg = jnp.dot(triu, db, precision=HIGHEST, preferred_element_type=F32)
        dg = dg + jnp.sum(st0 * ebc * dsp, axis=0, keepdims=True) + jnp.sum(ke_xk, axis=0, keepdims=True)
        keep = _row_ids((nc - 1 - c) * C, C) >= ROW0
        dg = jnp.where(keep, dg, 0.0)
        dk = jnp.where(keep, dk, 0.0)
        f_gate = lb + (1.0 - lb) * sz
        dfdz = (1.0 - lb) * sz * snz
        dz_ref[:, sl] = (dg * dfdz / f_gate - dk * dfdz).astype(dz_ref.dtype)
        dlb_ref[:, sl] += jnp.sum(dg * snz / f_gate - dk * snz, axis=0, keepdims=True)
        dq_ref[:, sl] = jnp.where(keep, dq * (sq * (1.0 + hq * (1.0 - sq))), 0.0).astype(dq_ref.dtype)
        di_ref[:, sl] = jnp.where(keep, dv, 0.0).astype(di_ref.dtype)
        dg_ref[:, sl] = jnp.where(keep, dhg, 0.0).astype(dg_ref.dtype)

    w = HG_HPS * HG_D
    outs = pl.pallas_call(
        body, name="hgrn_bwd", grid=(HG_H // HG_HPS, nc),
        in_specs=[col(0), col(HG_H), col(2 * HG_H), col(3 * HG_H), col(0)] + consts + [st],
        out_specs=[col(0), col(0), col(0), col(0), pl.BlockSpec((1, w), lambda h, c: (0, h)),
                   pl.BlockSpec((HG_HPS, 1, HG_D), lambda h, c: (h, 0, 0))],
        out_shape=[jax.ShapeDtypeStruct((L, D), BF16)] * 4 + [jax.ShapeDtypeStruct((1, D), F32),
                                                              jax.ShapeDtypeStruct((HG_H, 1, HG_D), F32)],
        scratch_shapes=[pltpu.VMEM((HG_HPS, HG_D, HG_D), F32)],
        compiler_params=_cparams(("parallel", "arbitrary")),
    )(proj, proj, proj, proj, dog, hlb, gg, rmat, masks, states)
    return outs


def _ffn_fwd(h, norm_gain, wg, wu, wo, tag):
    hn, g, u, act = _norm_proj(h, norm_gain, [wg, wu], [BF16, BF16], f"{tag}_in", swiglu=True)
    h_out = _matmul(act, wo, add=h, name=f"{tag}_out")
    return h_out, (h, hn, g, u, act)


def _ffn_bwd(dh, saved, norm_gain, wg, wu, wo, tag):
    h, hn, g, u, act = saved
    dg, du = _ffn_dact(dh, wo, g, u, f"{tag}_dact")
    d_wo = _matmul(act, dh, ta=True, name=f"{tag}_dwo")
    d_wg = _matmul(hn, dg, ta=True, name=f"{tag}_dwg")
    d_wu = _matmul(hn, du, ta=True, name=f"{tag}_dwu")
    dh, d_gain, _ = _dhn_norm([dg, du], [wg, wu], h, dh, norm_gain, f"{tag}_dhn_norm")
    return dh, d_gain, (d_wg, d_wu, d_wo)


def _local_step(h0, tgt, w, late_shards):
    L = h0.shape[0]
    gmat = jnp.asarray(np.kron(np.eye(MXU_N // FOX_DH), np.ones((FOX_DH, FOX_DH))).astype(np.float32), BF16)
    rmat, lmasks = _hgrn_consts()
    an, fn_ = w["attn_norm"], w["ffn_norm"]
    qg = jnp.tile(w["fox_q_norm"], (1, FOX_H))
    kg = jnp.tile(w["fox_k_norm"], (1, FOX_H))
    bf = jnp.pad(w["fox_b_f"], ((0, 0), (0, LANES - FOX_H)))
    fw = w["fox_w_in"]
    f_wq, f_wk, f_wv, f_wg = (fw[:, i * D:(i + 1) * D] for i in range(4))
    f_wf = jnp.pad(fw[:, 4 * D:], ((0, 0), (0, LANES - FOX_H)))

    hn0, q_raw, k_raw, v, gate, flog = _norm_proj(h0, an[0:1], [f_wq, f_wk, f_wv, f_wg, f_wf],
                                                  [F32, F32, BF16, BF16, F32], "fox_in")
    cq, ck = _fox_cumsum_fwd(flog, bf)
    qp, kp, vp = _fox_pack_fwd(q_raw, k_raw, v, cq, ck, qg, kg, gmat)
    o, og, lse2, gathered = _fox_attn_fwd(qp, kp, vp, gate, [late_shards[n] for n in GATHER_LATE])
    late = dict(zip(GATHER_LATE, gathered))
    f_wo = late["fox_w_out"].reshape(D, D)
    h_wo = late["hgrn_w_out"].reshape(D, D)
    h_wi = jnp.concatenate(list(late["hgrn_w_in"]), axis=1)
    g_in, g_out = late["ffn_w_in"], late["ffn_w_out"]
    ffw = []
    for i in range(2):
        rows_in, rows_out = slice(i * D, (i + 1) * D), slice(i * FFN // 4, (i + 1) * FFN // 4)
        ffw.append((jnp.concatenate([g_in[0, rows_in], g_in[1, rows_in]], axis=1),
                    jnp.concatenate([g_in[2, rows_in], g_in[3, rows_in]], axis=1),
                    jnp.concatenate([g_out[j, rows_out] for j in range(4)], axis=0)))
    h1 = _matmul(og, f_wo, add=h0, name="fox_out")
    h2, ffn0 = _ffn_fwd(h1, fn_[0:1], *ffw[0], "ffn0")

    hn2, proj = _norm_proj(h2, an[1:2], [h_wi], [F32], "hgrn_in")
    og1, states = _hgrn_fwd(proj, w["hgrn_lower_bounds"], w["hgrn_g_norm"], rmat, lmasks)
    h3 = _matmul(og1, h_wo, add=h2, name="hgrn_out")
    h4, ffn1 = _ffn_fwd(h3, fn_[1:2], *ffw[1], "ffn1")

    loss, dh, d_final = _loss_bwd(h4, tgt, w["final_norm"])

    dh, d_fn1, d_ffn1 = _ffn_bwd(dh, ffn1, fn_[1:2], *ffw[1], "ffn1")
    dog1 = _matmul(dh, h_wo, tb=True, out_dtype=BF16, name="hgrn_dog")
    d_h_wo = _matmul(og1, dh, ta=True, name="hgrn_dwo")
    dpq, dpz, dpi, dpg, d_lb, d_gg = _hgrn_bwd(proj, dog1, states, w["hgrn_lower_bounds"], w["hgrn_g_norm"],
                                               rmat, lmasks)
    dproj = jnp.concatenate([dpq, dpz, dpi, dpg], axis=1)
    d_h_wi = _matmul(hn2, dproj, ta=True, name="hgrn_dwi")
    dh, d_an1, _ = _dhn_norm([dproj], [h_wi], h2, dh, an[1:2], "hgrn_dhn_norm")

    dh, d_fn0, d_ffn0 = _ffn_bwd(dh, ffn0, fn_[0:1], *ffw[0], "ffn0")
    n_in, n_out = 2 * FFN // 4, FFN // 4
    d_ffn = [d_ffn0, d_ffn1]
    late_grads = dict(
        hgrn_w_in=_to_shards("hgrn_w_in", d_h_wi[None]), hgrn_w_out=d_h_wo.reshape(4, D // 4, D),
        ffn_w_in=jnp.stack([jnp.concatenate([d[j // 2][:, (j % 2) * n_in:(j % 2 + 1) * n_in] for d in d_ffn], axis=0)
                            for j in range(4)]),
        ffn_w_out=jnp.stack([jnp.concatenate([d[2][j * n_out:(j + 1) * n_out] for d in d_ffn], axis=0)
                             for j in range(4)]))
    pair_late, send_late = _pair_sums([late_grads[n] for n in LATE_NAMES], "late")

    dog = _matmul(dh, f_wo, tb=True, out_dtype=BF16, name="fox_dog")
    d_f_wo = _matmul(og, dh, ta=True, name="fox_dwo")

    def by_head(a):
        return jnp.pad(a.transpose(1, 0, 2).reshape(L, FOX_H), ((0, 0), (0, LANES - FOX_H)))

    qb = _fox_pack_bias(qp, cq, by_head(lse2.transpose(0, 2, 1)))
    dop, dgate = _fox_pack_bwd(dog, o, gate)
    dqp, dk, dv, dck, recv_late = _fox_attn_bwd(qb, kp, vp, dop, send_late)
    (dq_raw, dk_raw, dc_q), (d_qg, d_kg) = _fox_unpack_bwd(q_raw, k_raw, dqp, dk, qg, kg, gmat)
    dflog, d_bf = _fox_cumsum_bwd(dc_q, by_head(dck), flog, bf)
    dproj0 = jnp.concatenate([dq_raw, dk_raw, dv, dgate, dflog.astype(BF16)], axis=1)
    f_wall = jnp.concatenate([f_wq, f_wk, f_wv, f_wg, f_wf], axis=1)
    d_f_wall = _matmul(hn0, dproj0, ta=True, name="fox_dwi")
    d_f_wi = d_f_wall[:, :4 * D + FOX_H]
    fox_grads = dict(fox_w_in=d_f_wi[None], fox_w_out=d_f_wo[None])
    pair_fox, send_fox = _pair_sums([_to_shards(n, fox_grads[n]) for n in FOX_NAMES], "fox")
    dh, d_an0, recv_fox = _dhn_norm([dproj0], [f_wall], h0, dh, an[0:1], "fox_dhn_norm", slabs=send_fox)
    halves = _chip_sums(pair_fox, recv_fox, "fox") + _chip_sums(pair_late, recv_late, "late")
    theirs = _sibling_exchange(halves)
    big = {n: (m, t) for n, m, t in zip(FOX_NAMES + LATE_NAMES, halves, theirs)}
    small = dict(attn_norm=jnp.concatenate([d_an0, d_an1]), ffn_norm=jnp.concatenate([d_fn0, d_fn1]),
                 final_norm=d_final, lb_raw=d_lb, q_gain=d_qg, k_gain=d_kg, b_f=d_bf,
                 g_gain=d_gg.reshape(1, D))
    return loss, dh, big, small


def _me():
    return lax.axis_index("x"), lax.axis_index("y"), lax.axis_index("c")


def _flip(v, bit):
    return 1 - v if bit else v


def _chip_allgather_split(big, small):
    half = big.shape[0] // 2

    def body(big_in, small_in, big_out, small_out, ssem, rsem, fs_sem, fr_sem, ssem2, rsem2, lsem):
        x, y, c = _me()
        sib = (x, y, 1 - c)
        peers = _chip_peers()
        mine, other = pl.ds(c * half, half), pl.ds((1 - c) * half, half)
        local = [pltpu.make_async_copy(big_in, big_out.at[2 * x + y], lsem.at[0]),
                 pltpu.make_async_copy(small_in, small_out.at[2 * x + y], lsem.at[1])]
        sends = []
        for k, peer in enumerate(peers):
            sends.append(pltpu.make_async_remote_copy(big_in.at[mine], big_out.at[2 * x + y, mine], ssem.at[k],
                                                      rsem.at[k], device_id=peer, device_id_type=MESH))
            sends.append(pltpu.make_async_remote_copy(small_in, small_out.at[2 * x + y], ssem2.at[k], rsem2.at[k],
                                                      device_id=peer, device_id_type=MESH))
        for cp in local + sends:
            cp.start()
        forwards = []
        for k, peer in enumerate(peers):
            landed = big_out.at[2 * peer[0] + peer[1], mine]
            pltpu.make_async_remote_copy(big_in.at[mine], landed, ssem.at[k], rsem.at[k],
                                         device_id=peer, device_id_type=MESH).wait_recv()
            fwd = pltpu.make_async_remote_copy(landed, landed, fs_sem.at[k], fr_sem.at[k],
                                               device_id=sib, device_id_type=MESH)
            fwd.start()
            forwards.append(fwd)
        for k, peer in enumerate(peers):
            theirs = big_out.at[2 * peer[0] + peer[1], other]
            pltpu.make_async_remote_copy(theirs, theirs, fs_sem.at[k], fr_sem.at[k],
                                         device_id=sib, device_id_type=MESH).wait_recv()
            pltpu.make_async_remote_copy(small_in, small_out.at[2 * peer[0] + peer[1]], ssem2.at[k], rsem2.at[k],
                                         device_id=peer, device_id_type=MESH).wait_recv()
        for cp in sends + forwards:
            cp.wait_send()
        for cp in local:
            cp.wait()

    three = pltpu.SemaphoreType.DMA((3,))
    return pl.pallas_call(
        body, name="chip_allgather", in_specs=[ANY, ANY], out_specs=[ANY, ANY],
        out_shape=[jax.ShapeDtypeStruct((4,) + big.shape, big.dtype),
                   jax.ShapeDtypeStruct((4,) + small.shape, small.dtype)],
        scratch_shapes=[three, three, three, three, three, three, pltpu.SemaphoreType.DMA((2,))],
    )(big, small)


def _chip_peers():
    x, y, c = _me()
    return [(1 - x, y, c), (x, 1 - y, c), (1 - x, 1 - y, c)]


def _gather_sems(n):
    return [pltpu.SemaphoreType.DMA((n, 3)), pltpu.SemaphoreType.DMA((n, 3)), pltpu.SemaphoreType.DMA((n,))]


def _gather_copies(ins, outs, ssem, rsem, lsem, with_recvs):
    x, y, _ = _me()
    local, sends, recvs = [], [], []
    for a in range(len(ins)):
        local.append(pltpu.make_async_copy(ins[a], outs[a].at[2 * x + y], lsem.at[a]))
        for k, peer in enumerate(_chip_peers()):
            sends.append(pltpu.make_async_remote_copy(ins[a], outs[a].at[2 * x + y], ssem.at[a, k], rsem.at[a, k],
                                                      device_id=peer, device_id_type=MESH))
            if with_recvs:
                recvs.append(pltpu.make_async_remote_copy(ins[a], outs[a].at[2 * peer[0] + peer[1]], ssem.at[a, k],
                                                          rsem.at[a, k], device_id=peer, device_id_type=MESH))
    return local, sends, recvs


def _gather_start(ins, outs, ssem, rsem, lsem):
    local, sends, _ = _gather_copies(ins, outs, ssem, rsem, lsem, False)
    for cp in local + sends:
        cp.start()


def _gather_wait(ins, outs, ssem, rsem, lsem):
    local, sends, recvs = _gather_copies(ins, outs, ssem, rsem, lsem, True)
    for cp in local:
        cp.wait()
    for cp in sends:
        cp.wait_send()
    for cp in recvs:
        cp.wait_recv()


def _scatter_copies(ins, outs, ssem, rsem):
    copies = []
    for a in range(len(ins)):
        for k, peer in enumerate(_chip_peers()):
            copies.append(pltpu.make_async_remote_copy(ins[a].at[2 * peer[0] + peer[1]], outs[a].at[k], ssem.at[a, k],
                                                       rsem.at[a, k], device_id=peer, device_id_type=MESH))
    return copies


def _device_allgather(arr):
    def body(in_ref, out_ref, ssem, rsem, lsem):
        x, y, c = _me()
        me = 4 * x + 2 * y + c
        peers = [(_flip(x, k & 4), _flip(y, k & 2), _flip(c, k & 1)) for k in range(1, 8)]
        local = pltpu.make_async_copy(in_ref, out_ref.at[me], lsem)
        local.start()
        sends = []
        for k, peer in enumerate(peers):
            cp = pltpu.make_async_remote_copy(in_ref, out_ref.at[me], ssem.at[k], rsem.at[k],
                                              device_id=peer, device_id_type=MESH)
            cp.start()
            sends.append(cp)
        local.wait()
        for cp in sends:
            cp.wait_send()
        for k, peer in enumerate(peers):
            pltpu.make_async_remote_copy(in_ref, out_ref.at[4 * peer[0] + 2 * peer[1] + peer[2]], ssem.at[k],
                                         rsem.at[k], device_id=peer, device_id_type=MESH).wait_recv()

    return pl.pallas_call(
        body, name="device_allgather", in_specs=[ANY], out_specs=ANY,
        out_shape=jax.ShapeDtypeStruct((8,) + arr.shape, arr.dtype),
        scratch_shapes=[pltpu.SemaphoreType.DMA((7,)), pltpu.SemaphoreType.DMA((7,)), pltpu.SemaphoreType.DMA],
    )(arr)


def _sibling_send_other_half(arrs, tag):
    n = len(arrs)

    def body(*refs):
        ins, outs = refs[:n], refs[n:2 * n]
        ssem, rsem = refs[2 * n:]
        x, y, c = _me()
        cps = []
        for a in range(n):
            half = ins[a].shape[1] // 2
            src = ins[a].at[:, pl.ds((1 - c) * half, half), :]
            cp = pltpu.make_async_remote_copy(src, outs[a], ssem.at[a], rsem.at[a],
                                              device_id=(x, y, 1 - c), device_id_type=MESH)
            cp.start()
            cps.append(cp)
        for cp in cps:
            cp.wait()

    return pl.pallas_call(
        body, name=f"grad_sibling_swap_{tag}", in_specs=[ANY] * n, out_specs=[ANY] * n,
        out_shape=[jax.ShapeDtypeStruct((4, a.shape[1] // 2, a.shape[2]), a.dtype) for a in arrs],
        scratch_shapes=[pltpu.SemaphoreType.DMA((n,)), pltpu.SemaphoreType.DMA((n,))],
    )(*arrs)


def _sibling_exchange(arrs):
    n = len(arrs)

    def body(*refs):
        ins, outs = refs[:n], refs[n:2 * n]
        ssem, rsem = refs[2 * n:]
        x, y, c = _me()
        cps = [pltpu.make_async_remote_copy(ins[a], outs[a], ssem.at[a], rsem.at[a], device_id=(x, y, 1 - c),
                                            device_id_type=MESH) for a in range(n)]
        for cp in cps:
            cp.start()
        for cp in cps:
            cp.wait()

    return pl.pallas_call(
        body, name="grad_sibling_exchange", in_specs=[ANY] * n, out_specs=[ANY] * n,
        out_shape=[jax.ShapeDtypeStruct(a.shape, a.dtype) for a in arrs],
        scratch_shapes=[pltpu.SemaphoreType.DMA((n,)), pltpu.SemaphoreType.DMA((n,))],
    )(*arrs)


def _pair_sums(grads, tag):
    got = _sibling_send_other_half(grads, tag)
    res = [_pair_add(g, t, f"grad_pair_add_{tag}{i}") for i, (g, t) in enumerate(zip(grads, got))]
    return [r[0] for r in res], [r[1] for r in res]


def _mesh_scalar(v):
    return jnp.asarray(v, jnp.int32).reshape(1)


def _pair_add(g, t, name):
    _, rows, cols = g.shape
    half = rows // 2
    tm = _tile(half, cap=(2 * 1024 * 1024) // (4 * cols))

    def body(c_ref, g_ref, t_ref, o_ref, ob_ref):
        s = g_ref[0, 0] + t_ref[0]
        o_ref[0] = s
        ob_ref[0] = s.astype(ob_ref.dtype)

    spec = pl.BlockSpec((1, tm, cols), lambda j, i, c: (j, i, 0))
    return pl.pallas_call(
        body, name=name,
        grid_spec=pltpu.PrefetchScalarGridSpec(
            num_scalar_prefetch=1, grid=(4, half // tm),
            in_specs=[pl.BlockSpec((1, 1, tm, cols), lambda j, i, c: (j, c[0], i, 0)), spec],
            out_specs=[spec, spec]),
        out_shape=[jax.ShapeDtypeStruct(t.shape, F32), jax.ShapeDtypeStruct(t.shape, BF16)],
        compiler_params=_cparams(("parallel", "parallel")),
    )(_mesh_scalar(lax.axis_index("c")), g.reshape(4, 2, half, cols), t)


def _chip_sums(pair, recv, tag):
    x, y, _ = _me()
    out = []
    for n, (p, r) in enumerate(zip(pair, recv)):
        _, half, cols = p.shape
        tm = _tile(half, cap=(2 * 1024 * 1024) // (4 * cols))

        def body(j_ref, p_ref, r_ref, o_ref):
            o_ref[...] = p_ref[0] + r_ref[0].astype(F32) + r_ref[1].astype(F32) + r_ref[2].astype(F32)

        out.append(pl.pallas_call(
            body, name=f"grad_chip_add_{tag}{n}",
            grid_spec=pltpu.PrefetchScalarGridSpec(
                num_scalar_prefetch=1, grid=(half // tm,),
                in_specs=[pl.BlockSpec((1, tm, cols), lambda i, j: (j[0], i, 0)),
                          pl.BlockSpec((3, tm, cols), lambda i, j: (0, i, 0))],
                out_specs=pl.BlockSpec((tm, cols), lambda i, j: (i, 0))),
            out_shape=jax.ShapeDtypeStruct((half, cols), F32),
            compiler_params=_cparams(("parallel",)),
        )(_mesh_scalar(2 * x + y), p, r))
    return out


SMALL_ROWS = 32


def _small_finalize(gathered, hlb, fold64, fold128):
    def body(g_ref, hlb_ref, f64_ref, f128_ref, rows_ref, qk_ref, gg_ref, lb_ref):
        tot = g_ref[0]
        for d in range(1, 8):
            tot = tot + g_ref[d]
        rows_ref[...] = tot
        qk_ref[...] = jnp.dot(rows_ref[6:8, :], f64_ref[...], precision=HIGHEST, preferred_element_type=F32)
        gg_ref[...] = jnp.dot(rows_ref[9:10, :], f128_ref[...], precision=HIGHEST, preferred_element_type=F32)
        h0, h1 = hlb_ref[0:1, :], hlb_ref[1:2, :]
        mx = jnp.maximum(h0, h1)
        e0, e1 = jnp.exp(h0 - mx), jnp.exp(h1 - mx)
        lb = e1 / (e0 + e1)
        d1 = rows_ref[5:6, :] * lb * (1.0 - lb)
        lb_ref[...] = jnp.where(_iota((2, 1), 0) == 0, -d1, d1)

    return pl.pallas_call(
        body, name="small_finalize",
        out_shape=[jax.ShapeDtypeStruct((SMALL_ROWS, D), F32), jax.ShapeDtypeStruct((2, FOX_DH), F32),
                   jax.ShapeDtypeStruct((1, HG_D), F32), jax.ShapeDtypeStruct((2, D), F32)],
    )(gathered, hlb, fold64, fold128)


FOX_NAMES = ("fox_w_in", "fox_w_out")
LATE_NAMES = ("hgrn_w_in", "hgrn_w_out", "ffn_w_in", "ffn_w_out")
BIG_NAMES = FOX_NAMES + LATE_NAMES
GATHER_LATE = ("fox_w_out",) + LATE_NAMES
COL_SHARDED = ("fox_w_in", "hgrn_w_in", "ffn_w_in")


def _shard2d(name, a):
    return a.reshape(-1, a.shape[-1])


def _to_shards(name, g):
    layers = g.shape[0]
    if name in COL_SHARDED:
        k, n = g.shape[1], g.shape[2] // 4
        return g.reshape(layers, k, 4, n).transpose(2, 0, 1, 3).reshape(4, layers * k, n)
    r = g.shape[1] // 4
    return g.reshape(layers, 4, r, g.shape[2]).transpose(1, 0, 2, 3).reshape(4, layers * r, g.shape[2])


def kernel(x, meta_tokens, attn_norm, ffn_norm, final_norm, fox_w_in, fox_b_f, fox_q_norm, fox_k_norm, fox_w_out, hgrn_w_in, hgrn_lower_bounds, hgrn_g_norm, hgrn_w_out, ffn_w_in, ffn_w_out, loss_target, m_meta_tokens, m_attn_norm, m_ffn_norm, m_final_norm, m_fox_w_in, m_fox_b_f, m_fox_q_norm, m_fox_k_norm, m_fox_w_out, m_hgrn_w_in, m_hgrn_lower_bounds, m_hgrn_g_norm, m_hgrn_w_out, m_ffn_w_in, m_ffn_w_out, v_meta_tokens, v_attn_norm, v_ffn_norm, v_final_norm, v_fox_w_in, v_fox_b_f, v_fox_q_norm, v_fox_k_norm, v_fox_w_out, v_hgrn_w_in, v_hgrn_lower_bounds, v_hgrn_g_norm, v_hgrn_w_out, v_ffn_w_in, v_ffn_w_out):
    params = dict(meta_tokens=meta_tokens, attn_norm=attn_norm, ffn_norm=ffn_norm, final_norm=final_norm,
                  fox_w_in=fox_w_in, fox_b_f=fox_b_f, fox_q_norm=fox_q_norm, fox_k_norm=fox_k_norm,
                  fox_w_out=fox_w_out, hgrn_w_in=hgrn_w_in, hgrn_lower_bounds=hgrn_lower_bounds,
                  hgrn_g_norm=hgrn_g_norm, hgrn_w_out=hgrn_w_out, ffn_w_in=ffn_w_in, ffn_w_out=ffn_w_out)
    mom_m = dict(meta_tokens=m_meta_tokens, attn_norm=m_attn_norm, ffn_norm=m_ffn_norm, final_norm=m_final_norm,
                 fox_w_in=m_fox_w_in, fox_b_f=m_fox_b_f, fox_q_norm=m_fox_q_norm, fox_k_norm=m_fox_k_norm,
                 fox_w_out=m_fox_w_out, hgrn_w_in=m_hgrn_w_in, hgrn_lower_bounds=m_hgrn_lower_bounds,
                 hgrn_g_norm=m_hgrn_g_norm, hgrn_w_out=m_hgrn_w_out, ffn_w_in=m_ffn_w_in, ffn_w_out=m_ffn_w_out)
    mom_v = dict(meta_tokens=v_meta_tokens, attn_norm=v_attn_norm, ffn_norm=v_ffn_norm, final_norm=v_final_norm,
                 fox_w_in=v_fox_w_in, fox_b_f=v_fox_b_f, fox_q_norm=v_fox_q_norm, fox_k_norm=v_fox_k_norm,
                 fox_w_out=v_fox_w_out, hgrn_w_in=v_hgrn_w_in, hgrn_lower_bounds=v_hgrn_lower_bounds,
                 hgrn_g_norm=v_hgrn_g_norm, hgrn_w_out=v_hgrn_w_out, ffn_w_in=v_ffn_w_in, ffn_w_out=v_ffn_w_out)
    names = list(params)
    xi, yi, _ = _me()

    shards = {n: _shard2d(n, params[n]).astype(BF16) for n in BIG_NAMES}
    w_in_g, meta_g = _chip_allgather_split(shards["fox_w_in"], meta_tokens)
    w = dict(fox_w_in=jnp.concatenate(list(w_in_g), axis=1))
    meta_full = jnp.concatenate(list(meta_g), axis=1)
    w.update(attn_norm=attn_norm, ffn_norm=ffn_norm, final_norm=final_norm.reshape(1, D), fox_b_f=fox_b_f,
             fox_q_norm=fox_q_norm, fox_k_norm=fox_k_norm, hgrn_lower_bounds=hgrn_lower_bounds,
             hgrn_g_norm=hgrn_g_norm)

    h0 = jnp.concatenate([jnp.zeros((ROW0, D), F32), meta_full, x[0]], axis=0)
    loss, dh0, big, small = _local_step(h0, loss_target[0], w, {n: shards[n] for n in GATHER_LATE})
    loss = lax.psum(loss, ("x", "y", "c"))
    grad_x = dh0[PAD:][None]
    grads = {}

    rows = jnp.concatenate([small["attn_norm"], small["ffn_norm"], small["final_norm"], small["lb_raw"],
                            small["q_gain"], small["k_gain"],
                            jnp.pad(small["b_f"], ((0, 0), (0, D - LANES))), small["g_gain"],
                            dh0[ROW0:PAD], jnp.zeros((SMALL_ROWS - 10 - N_META, D), F32)], axis=0)
    allrows = _device_allgather(rows)
    fold64 = jnp.asarray(np.tile(np.eye(FOX_DH, dtype=np.float32), (FOX_H, 1)))
    fold128 = jnp.asarray(np.tile(np.eye(HG_D, dtype=np.float32), (HG_H, 1)))
    tot, qk, gg, dlb = _small_finalize(allrows, hgrn_lower_bounds, fold64, fold128)
    grads.update(attn_norm=tot[0:2], ffn_norm=tot[2:4], final_norm=tot[4], hgrn_lower_bounds=dlb,
                 fox_q_norm=qk[0:1], fox_k_norm=qk[1:2], fox_b_f=tot[8:9, :FOX_H], hgrn_g_norm=gg,
                 meta_tokens=lax.dynamic_slice_in_dim(tot[10:10 + N_META], (2 * xi + yi) * (D // 4), D // 4, axis=1))

    delta, new_m, new_v = {}, {}, {}
    for n in BIG_NAMES:
        res = _adamw_halves(_shard2d(n, params[n]), *big[n], _shard2d(n, mom_m[n]), _shard2d(n, mom_v[n]),
                            f"adamw_{n}")
        grads[n], delta[n], new_m[n], new_v[n] = (t.reshape(params[n].shape) for t in res)
    delta["meta_tokens"], new_m["meta_tokens"], new_v["meta_tokens"] = _adamw(
        meta_tokens, grads["meta_tokens"], m_meta_tokens, v_meta_tokens, "adamw_meta_tokens")
    small_names = [n for n in names if n not in BIG_NAMES and n != "meta_tokens"]

    def pack(d):
        return jnp.concatenate([jnp.pad(d[n].reshape(-1, d[n].shape[-1]), ((0, 0), (0, D - d[n].shape[-1])))
                                for n in small_names], axis=0)

    packed = [pack(t) for t in (params, grads, mom_m, mom_v)]
    n_rows = packed[0].shape[0]
    packed = [jnp.pad(t, ((0, 16 - n_rows), (0, 0))) for t in packed]
    res = _adamw(*packed, "adamw_small")
    r0 = 0
    for n in small_names:
        nr = params[n].reshape(-1, params[n].shape[-1]).shape[0]
        for dst, src in zip((delta, new_m, new_v), res):
            dst[n] = src[r0:r0 + nr, :params[n].shape[-1]].reshape(params[n].shape)
        r0 += nr

    return (loss, grad_x, *[grads[n] for n in names], *[delta[n] for n in names],
            *[new_m[n] for n in names], *[new_v[n] for n in names])
```

```python
import functools

import numpy as np
import jax
import jax.numpy as jnp
from jax import lax
from jax.experimental import pallas as pl
from jax.experimental.pallas import tpu as pltpu

F32, BF16 = jnp.float32, jnp.bfloat16
HIGHEST = lax.Precision.HIGHEST

D = 1024
N_META = 16
PAD = 128
ROW0 = PAD - N_META
FOX_H, FOX_DH = 16, 64
HG_H, HG_D = 8, 128
HG_C = 128
HG_LEV = 7
HG_HPS = 8
HG_SKEW = 0
FFN = 2816
EPS = 1e-6
BIG = 1e30
LOG2E = 1.4426950408889634
LANES = 128
MXU_N = 256
VMEM_LIMIT = 48 * 1024 * 1024
ROW_TILES = (640, 512, 384, 320, 256, 128, 64, 32, 16, 8)
ATTN_TILES = (640, 512, 256, 128)
FOX_HPS_FWD = 8
FOX_HPS_BWD = 4

ADAM_LR, ADAM_B1, ADAM_B2, ADAM_EPS, ADAM_WD, ADAM_STEP = 0.001, 0.9, 0.999, 1e-08, 0.01, 10

MESH = pl.DeviceIdType.MESH
ANY = pl.BlockSpec(memory_space=pl.ANY)
NT = (((1,), (1,)), ((), ()))
TN = (((0,), (0,)), ((), ()))


def _tile(n, cands=ROW_TILES, cap=None):
    for c in cands:
        if n % c == 0 and (cap is None or c <= cap):
            return c
    return n


def _cparams(sem):
    return pltpu.CompilerParams(dimension_semantics=sem, vmem_limit_bytes=VMEM_LIMIT)


def _sigmoid(x):
    return jax.nn.sigmoid(x)


def _log_sigmoid(x):
    return jnp.minimum(x, 0.0) - jnp.log(1.0 + jnp.exp(-jnp.abs(x)))


def _iota(shape, dim):
    return lax.broadcasted_iota(jnp.int32, shape, dim)


def _matmul(a, b, *, ta=False, tb=False, out_dtype=F32, add=None, name):
    if ta:
        kdim, m = a.shape
    else:
        m, kdim = a.shape
    n = b.shape[0] if tb else b.shape[1]
    if ta:
        tm = m if m <= 1024 else _tile(m, (1408, 1024, 512, 256, 128))
        tk = _tile(kdim, (1664,) + ROW_TILES)
    else:
        tm = _tile(m)
        tk = kdim if kdim <= 4096 else _tile(kdim, (2048, 1024, 512))
    tn = n if n <= 1024 else _tile(n, (1408, 1024, 512, 256, 128))
    nk = kdim // tk
    dn = (((0 if ta else 1,), (1 if tb else 0,)), ((), ()))

    def body(*refs):
        if add is None:
            a_ref, b_ref, o_ref, acc_ref = refs
        else:
            a_ref, b_ref, add_ref, o_ref, acc_ref = refs
        k = pl.program_id(2)

        @pl.when(k == 0)
        def _():
            acc_ref[...] = jnp.zeros_like(acc_ref)

        acc_ref[...] += lax.dot_general(a_ref[...].astype(BF16), b_ref[...].astype(BF16), dn,
                                        preferred_element_type=F32)

        @pl.when(k == nk - 1)
        def _():
            r = acc_ref[...]
            if add is not None:
                r = r + add_ref[...].astype(F32)
            o_ref[...] = r.astype(o_ref.dtype)

    a_spec = pl.BlockSpec((tk, tm), lambda j, i, k: (k, i)) if ta else pl.BlockSpec((tm, tk), lambda j, i, k: (i, k))
    b_spec = pl.BlockSpec((tn, tk), lambda j, i, k: (j, k)) if tb else pl.BlockSpec((tk, tn), lambda j, i, k: (k, j))
    o_spec = pl.BlockSpec((tm, tn), lambda j, i, k: (i, j))
    ins, specs = [a, b], [a_spec, b_spec]
    if add is not None:
        ins.append(add)
        specs.append(o_spec)
    return pl.pallas_call(
        body, name=name, grid=(n // tn, m // tm, nk), in_specs=specs, out_specs=o_spec,
        out_shape=jax.ShapeDtypeStruct((m, n), out_dtype),
        scratch_shapes=[pltpu.VMEM((tm, tn), F32)],
        compiler_params=_cparams(("parallel", "parallel", "arbitrary")),
    )(*ins)


def _rowwise(fn, ins, bcast, outs, accs, *, name, reverse=False, carry=None, as_refs=False):
    rows = ins[0].shape[0]
    per_row = sum(x.shape[1] * x.dtype.itemsize for x in ins) + sum(c * jnp.dtype(d).itemsize for c, d in outs)
    tm = _tile(rows, cap=max(8, (10 * 1024 * 1024) // per_row))
    n = rows // tm
    n_in, n_b, n_o, n_a = len(ins), len(bcast), len(outs), len(accs)

    def body(*refs):
        in_refs = refs[:n_in]
        b_refs = refs[n_in:n_in + n_b]
        o_refs = refs[n_in + n_b:n_in + n_b + n_o]
        a_refs = refs[n_in + n_b + n_o:n_in + n_b + n_o + n_a]
        c_refs = refs[n_in + n_b + n_o + n_a:]
        i = pl.program_id(0)
        blk = (n - 1 - i) if reverse else i
        if c_refs:
            @pl.when(i == 0)
            def _():
                c_refs[0][...] = jnp.zeros_like(c_refs[0])
        args = (list(in_refs) if as_refs else [r[...] for r in in_refs], [r[...] for r in b_refs])
        o_vals, a_vals = fn(blk * tm, *args, *c_refs)
        for r, v in zip(o_refs, o_vals):
            r[...] = v.astype(r.dtype)
        if n_a:
            @pl.when(i == 0)
            def _():
                for r in a_refs:
                    r[...] = jnp.zeros_like(r)
            for r, v in zip(a_refs, a_vals):
                r[...] += v

    def row_map(i):
        return ((n - 1 - i) if reverse else i, 0)

    in_specs = [pl.BlockSpec((tm, x.shape[1]), row_map) for x in ins]
    in_specs += [pl.BlockSpec(x.shape, lambda i, nd=x.ndim: (0,) * nd) for x in bcast]
    out_specs = [pl.BlockSpec((tm, c), row_map) for c, _ in outs]
    out_specs += [pl.BlockSpec(s, lambda i: (0, 0)) for s in accs]
    out_shape = [jax.ShapeDtypeStruct((rows, c), d) for c, d in outs]
    out_shape += [jax.ShapeDtypeStruct(s, F32) for s in accs]
    res = pl.pallas_call(
        body, name=name, grid=(n,), in_specs=in_specs, out_specs=out_specs, out_shape=out_shape,
        scratch_shapes=[pltpu.VMEM(carry, F32)] if carry else [],
        compiler_params=_cparams(("arbitrary",)),
    )(*ins, *bcast)
    return res[:n_o], res[n_o:]


def _row_ids(row0, tm):
    return row0 + _iota((tm, 1), 0)


def _rms_bwd_math(xv, dy, g):
    r = lax.rsqrt(jnp.mean(xv * xv, axis=-1, keepdims=True) + EPS)
    xh = xv * r
    dxh = dy * g
    dx = r * (dxh - xh * jnp.mean(dxh * xh, axis=-1, keepdims=True))
    return dx, jnp.sum(dy * xh, axis=0, keepdims=True)


def _loss_bwd(h, tgt, gain):
    tm = _tile(h.shape[0], (5 * PAD, PAD))
    n = h.shape[0] // tm
    pieces = tm // PAD

    def body(x_ref, *refs):
        t_refs, (g_ref, dx_ref, loss_ref, dg_ref) = refs[:pieces], refs[pieces:]
        i = pl.program_id(0)

        @pl.when(i == 0)
        def _():
            loss_ref[...] = jnp.zeros_like(loss_ref)
            dg_ref[...] = jnp.zeros_like(dg_ref)

        xv, g = x_ref[...], g_ref[...]
        r = lax.rsqrt(jnp.mean(xv * xv, axis=-1, keepdims=True) + EPS)
        xh = xv * r
        tv = jnp.concatenate([t[...] for t in t_refs], axis=0)
        err = jnp.where(_row_ids(i * tm, tm) >= PAD, xh * g - tv, 0.0)
        per_row = jnp.mean(err * err, axis=-1, keepdims=True)
        loss_ref[...] += jnp.broadcast_to(0.5 * jnp.sum(per_row, axis=0, keepdims=True), (1, LANES))
        dy = err * (1.0 / D)
        dxh = dy * g
        dx_ref[...] = r * (dxh - xh * jnp.mean(dxh * xh, axis=-1, keepdims=True))
        dg_ref[...] += jnp.sum(dy * xh, axis=0, keepdims=True)

    dh, loss, dgain = pl.pallas_call(
        body, name="loss_bwd", grid=(n,),
        in_specs=[pl.BlockSpec((tm, D), lambda i: (i, 0))]
        + [pl.BlockSpec((PAD, D), lambda i, k=k: (jnp.maximum(i * pieces + k - 1, 0), 0)) for k in range(pieces)]
        + [pl.BlockSpec((1, D), lambda i: (0, 0))],
        out_specs=[pl.BlockSpec((tm, D), lambda i: (i, 0)), pl.BlockSpec((1, LANES), lambda i: (0, 0)),
                   pl.BlockSpec((1, D), lambda i: (0, 0))],
        out_shape=[jax.ShapeDtypeStruct(h.shape, F32), jax.ShapeDtypeStruct((1, LANES), F32),
                   jax.ShapeDtypeStruct((1, D), F32)],
        compiler_params=_cparams(("arbitrary",)),
    )(h, *([tgt] * pieces), gain)
    return loss[0, 0], dh, dgain


FFN_TILES = dict(rows=(320, 256, 128), cols=(1408, 1024, 512, 256, 128))


def _norm_proj(h, gain, ws, dtypes, name, swiglu=False):
    m = h.shape[0]
    tm = _tile(m, FFN_TILES["rows"])
    n = len(ws)

    def body(h_ref, g_ref, *refs):
        w_refs, hn_ref, o_refs = refs[:n], refs[n], refs[n + 1:]
        x = h_ref[...]
        hn = (x * lax.rsqrt(jnp.mean(x * x, axis=-1, keepdims=True) + EPS) * g_ref[...]).astype(BF16)
        hn_ref[...] = hn
        prods = [jnp.dot(hn, w[...], preferred_element_type=F32) for w in w_refs]
        for o, p in zip(o_refs, prods):
            o[...] = p.astype(o.dtype)
        if swiglu:
            o_refs[n][...] = (prods[0] * _sigmoid(prods[0]) * prods[1]).astype(o_refs[n].dtype)

    rows = lambda c: pl.BlockSpec((tm, c), lambda i: (i, 0))
    cols = [w.shape[1] for w in ws] + ([ws[0].shape[1]] if swiglu else [])
    dts = list(dtypes) + ([BF16] if swiglu else [])
    return pl.pallas_call(
        body, name=name, grid=(m // tm,),
        in_specs=[rows(D), pl.BlockSpec((1, D), lambda i: (0, 0))]
        + [pl.BlockSpec(w.shape, lambda i: (0, 0), pipeline_mode=pl.Buffered(1)) for w in ws],
        out_specs=[rows(D)] + [rows(c) for c in cols],
        out_shape=[jax.ShapeDtypeStruct((m, D), BF16)] + [jax.ShapeDtypeStruct((m, c), d) for c, d in zip(cols, dts)],
        compiler_params=_cparams(("parallel",)),
    )(h, gain, *ws)


def _dhn_norm(dys, ws, h, dh_up, gain, name, slabs=()):
    m = h.shape[0]
    tm = _tile(m, FFN_TILES["rows"])
    n, ns = len(dys), len(slabs)
    steps = m // tm

    def body(*refs):
        dy_refs, w_refs = refs[:n], refs[n:2 * n]
        h_ref, up_ref, g_ref = refs[2 * n:2 * n + 3]
        sl_in = refs[2 * n + 3:2 * n + 3 + ns]
        dh_ref, dgain_ref = refs[2 * n + 3 + ns:2 * n + 5 + ns]
        sl_out = refs[2 * n + 5 + ns:2 * n + 5 + 2 * ns]
        sems = refs[2 * n + 5 + 2 * ns:]
        i = pl.program_id(0)

        @pl.when(i == 0)
        def _():
            dgain_ref[...] = jnp.zeros_like(dgain_ref)
            for cp in _scatter_copies(sl_in, sl_out, *sems) if ns else ():
                cp.start()

        dy = sum(lax.dot_general(a[...], w[...], NT, preferred_element_type=F32) for a, w in zip(dy_refs, w_refs))
        dx, dgain = _rms_bwd_math(h_ref[...], dy, g_ref[...])
        keep = _row_ids(i * tm, tm) >= ROW0
        dh_ref[...] = jnp.where(keep, up_ref[...] + dx, 0.0)
        dgain_ref[...] += dgain

        if ns:
            @pl.when(i == steps - 1)
            def _():
                for cp in _scatter_copies(sl_in, sl_out, *sems):
                    cp.wait()

    rows = lambda c: pl.BlockSpec((tm, c), lambda i: (i, 0))
    whole = lambda a: pl.BlockSpec(a.shape, lambda i: (0, 0), pipeline_mode=pl.Buffered(1))
    res = pl.pallas_call(
        body, name=name, grid=(steps,),
        in_specs=[rows(a.shape[1]) for a in dys] + [whole(w) for w in ws]
        + [rows(D), rows(D), pl.BlockSpec((1, D), lambda i: (0, 0))] + [ANY] * ns,
        out_specs=[rows(D), pl.BlockSpec((1, D), lambda i: (0, 0))] + [ANY] * ns,
        out_shape=[jax.ShapeDtypeStruct((m, D), F32), jax.ShapeDtypeStruct((1, D), F32)]
        + [jax.ShapeDtypeStruct((3,) + a.shape[1:], a.dtype) for a in slabs],
        scratch_shapes=[pltpu.SemaphoreType.DMA((ns, 3)), pltpu.SemaphoreType.DMA((ns, 3))] if ns else [],
        compiler_params=_cparams(("arbitrary",)),
    )(*dys, *ws, h, dh_up, gain, *slabs)
    return res[0], res[1], res[2:]


def _ffn_dact(dh, wo, g, u, name):
    m, kdim = dh.shape
    n = wo.shape[0]
    tm, tn = _tile(m, FFN_TILES["rows"]), _tile(n, FFN_TILES["cols"])

    def body(a_ref, w_ref, g_ref, u_ref, dg_ref, du_ref):
        da = lax.dot_general(a_ref[...].astype(BF16), w_ref[...], NT, preferred_element_type=F32)
        gv, uv = g_ref[...].astype(F32), u_ref[...].astype(F32)
        s = _sigmoid(gv)
        dg_ref[...] = (da * uv * (s * (1.0 + gv * (1.0 - s)))).astype(dg_ref.dtype)
        du_ref[...] = (da * gv * s).astype(du_ref.dtype)

    ospec = pl.BlockSpec((tm, tn), lambda j, i: (i, j))
    return pl.pallas_call(
        body, name=name, grid=(n // tn, m // tm),
        in_specs=[pl.BlockSpec((tm, kdim), lambda j, i: (i, 0)), pl.BlockSpec((tn, kdim), lambda j, i: (j, 0)),
                  ospec, ospec],
        out_specs=[ospec] * 2, out_shape=[jax.ShapeDtypeStruct((m, n), BF16)] * 2,
        compiler_params=_cparams(("parallel", "parallel")),
    )(dh, wo, g, u)


def _adamw_math(wv, gv, mv, vv):
    mn = ADAM_B1 * mv + (1.0 - ADAM_B1) * gv
    vn = ADAM_B2 * vv + (1.0 - ADAM_B2) * (gv * gv)
    m_hat = mn / (1.0 - ADAM_B1 ** ADAM_STEP)
    v_hat = vn / (1.0 - ADAM_B2 ** ADAM_STEP)
    return -ADAM_LR * (m_hat / (jnp.sqrt(v_hat) + ADAM_EPS) + ADAM_WD * wv), mn, vn


def _adamw(w, g, m, v, name):
    def fn(row0, ins, bc):
        return list(_adamw_math(*ins)), []
    c = w.shape[1]
    return _rowwise(fn, [w, g, m, v], [], [(c, F32)] * 3, [], name=name)[0]


def _adamw_halves(w, mine, theirs, m, v, name):
    rows, cols = w.shape
    half = rows // 2
    tm = _tile(half, cap=(10 * 1024 * 1024) // (9 * 4 * cols))
    nb = half // tm

    def body(c_ref, w_ref, g1_ref, g2_ref, m_ref, v_ref, g_out, d_out, m_out, v_out):
        own = (pl.program_id(0) // nb) == c_ref[0]
        g = jnp.where(own, g1_ref[...], g2_ref[...])
        delta, mn, vn = _adamw_math(w_ref[...], g, m_ref[...], v_ref[...])
        g_out[...] = g
        d_out[...] = delta
        m_out[...] = mn
        v_out[...] = vn

    full = pl.BlockSpec((tm, cols), lambda i, c: (i, 0))
    part = pl.BlockSpec((tm, cols), lambda i, c: (lax.rem(i, nb), 0))
    return pl.pallas_call(
        body, name=name,
        grid_spec=pltpu.PrefetchScalarGridSpec(num_scalar_prefetch=1, grid=(2 * nb,),
                                               in_specs=[full, part, part, full, full], out_specs=[full] * 4),
        out_shape=[jax.ShapeDtypeStruct((rows, cols), F32)] * 4,
        compiler_params=_cparams(("parallel",)),
    )(_mesh_scalar(lax.axis_index("c")), w, mine, theirs, m, v)


def _head_sum(x, gmat):
    hi = x.astype(BF16)
    lo = (x - hi.astype(F32)).astype(BF16)
    w = gmat.shape[0]
    return jnp.concatenate(
        [jnp.dot(hi[:, b:b + w], gmat, preferred_element_type=F32) + jnp.dot(lo[:, b:b + w], gmat,
                                                                             preferred_element_type=F32)
         for b in range(0, x.shape[1], w)], axis=1)


def _split3(x):
    hi = x.astype(BF16).astype(F32)
    r = x - hi
    mid = r.astype(BF16).astype(F32)
    return hi, mid, r - mid


def _extra_base(hh):
    return FOX_DH * (1 - hh)


def _data_mask(hh):
    lane = _iota((1, LANES), 1)
    return (lane >= FOX_DH * hh) & (lane < FOX_DH * (hh + 1))


def _with_extras(data, hh, vals):
    lane = _iota((1, LANES), 1)
    x = jnp.zeros((1, LANES), data.dtype)
    for e, v in enumerate(vals):
        if isinstance(v, float):
            x = jnp.where(lane == _extra_base(hh) + e, v, x)
    for e, v in enumerate(vals):
        if not isinstance(v, float):
            x = jnp.where(lane == _extra_base(hh) + e, v, x)
    return jnp.where(_data_mask(hh), data, x)


def _fox_pack_fwd(q_raw, k_raw, v, cq, ck, qg, kg, gmat):
    scale2 = FOX_DH ** -0.5 * LOG2E

    def fn(row0, refs, bc):
        q_ref, k_ref, v_ref, cq_ref, ck_ref = refs
        g_q, g_k, gm = bc
        qv, kv = q_ref[...], k_ref[...]
        qn = qv * lax.rsqrt(_head_sum(qv * qv, gm) * (1.0 / FOX_DH) + EPS) * (g_q * scale2)
        kn = kv * lax.rsqrt(_head_sum(kv * kv, gm) * (1.0 / FOX_DH) + EPS) * g_k
        qs, ks, vs = [], [], []
        for h in range(FOX_H):
            p, hh = divmod(h, 2)
            sl = slice(p * LANES, (p + 1) * LANES)
            cq3 = _split3(cq_ref[:, h:h + 1] * LOG2E)
            ck3 = _split3(ck_ref[:, h:h + 1] * (-LOG2E))
            qs.append(_with_extras(qn[:, sl], hh, [*cq3, 1.0, 1.0, 1.0]))
            ks.append(_with_extras(kn[:, sl], hh, [1.0, 1.0, 1.0, *ck3]))
            vs.append(_with_extras(v_ref[:, sl].astype(F32), hh, [1.0, 1.0]))
        return [jnp.concatenate(qs, axis=1), jnp.concatenate(ks, axis=1), jnp.concatenate(vs, axis=1)], []

    w = FOX_H * LANES
    return _rowwise(fn, [q_raw, k_raw, v, cq, ck], [qg, kg, gmat], [(w, BF16)] * 3, [], name="fox_pack_fwd",
                    as_refs=True)[0]


def _fox_pack_bias(qp, cq, lse2):
    def fn(row0, refs, bc):
        q_ref, cq_ref, lse_ref = refs
        lane = _iota((1, LANES), 1)
        outs = []
        for h in range(FOX_H):
            blk = q_ref[:, h * LANES:(h + 1) * LANES].astype(F32)
            for e, part in enumerate(_split3(cq_ref[:, h:h + 1] * LOG2E - lse_ref[:, h:h + 1])):
                blk = jnp.where(lane == _extra_base(h % 2) + e, part, blk)
            outs.append(blk)
        return [jnp.concatenate(outs, axis=1)], []
    return _rowwise(fn, [qp, cq, lse2], [], [(FOX_H * LANES, BF16)], [], name="fox_pack_bias", as_refs=True)[0][0]


def _fox_pack_bwd(dog, o, gate):
    def fn(row0, refs, bc):
        d_ref, o_ref, g_ref = refs
        dos, dgs = [], []
        for p in range(FOX_H // 2):
            sl = slice(p * LANES, (p + 1) * LANES)
            dv, ov, gv = (r[:, sl].astype(F32) for r in (d_ref, o_ref, g_ref))
            s = _sigmoid(gv)
            do = dv * s
            dgs.append(dv * ov * s * (1.0 - s))
            od = ov * do
            for hh in range(2):
                delta = jnp.sum(jnp.where(_data_mask(hh), od, 0.0), axis=-1, keepdims=True)
                hi = delta.astype(BF16).astype(F32)
                dos.append(_with_extras(do, hh, [-hi, hi - delta]))
        return [jnp.concatenate(dos, axis=1), jnp.concatenate(dgs, axis=1)], []
    return _rowwise(fn, [dog, o, gate], [], [(FOX_H * LANES, BF16), (D, BF16)], [], name="fox_pack_bwd",
                    as_refs=True)[0]


def _fox_unpack_bwd(q_raw, k_raw, dqp, dk, qg, kg, gmat):
    scale = FOX_DH ** -0.5

    def fn(row0, refs, bc):
        q_ref, k_ref, dq_ref, dk_ref = refs
        g_q, g_k, gm = bc
        lane = _iota((1, LANES), 1)
        dqs = []
        dcq = jnp.zeros((q_ref.shape[0], LANES), F32)
        for p in range(FOX_H // 2):
            even = dq_ref[:, (2 * p) * LANES:(2 * p + 1) * LANES]
            odd = dq_ref[:, (2 * p + 1) * LANES:(2 * p + 2) * LANES]
            dqs.append(jnp.where(_data_mask(0), even, odd) * scale)
            for hh in range(2):
                col = (2 * p + hh) * LANES + _extra_base(hh)
                dcq = jnp.where(lane == 2 * p + hh, dq_ref[:, col:col + 1], dcq)
        outs, accs = [], []
        for xv, dy, g in ((q_ref[...], jnp.concatenate(dqs, axis=1), g_q), (k_ref[...], dk_ref[...] * (1.0 / LOG2E), g_k)):
            r = lax.rsqrt(_head_sum(xv * xv, gm) * (1.0 / FOX_DH) + EPS)
            xh = xv * r
            dxh = dy * g
            outs.append(r * (dxh - xh * (_head_sum(dxh * xh, gm) * (1.0 / FOX_DH))))
            accs.append(jnp.sum(dy * xh, axis=0, keepdims=True))
        return outs + [dcq], accs
    return _rowwise(fn, [q_raw, k_raw, dqp, dk], [qg, kg, gmat], [(D, BF16), (D, BF16), (LANES, F32)],
                    [(1, D), (1, D)], name="fox_unpack_bwd", as_refs=True)


def _fox_cumsum_fwd(flog, bf):
    def fn(row0, ins, bc, carry):
        (f,), (b,) = ins, bc
        tm = f.shape[0]
        keep = _row_ids(row0, tm) >= ROW0
        lf = jnp.where(keep, _log_sigmoid(f + b), 0.0)
        tri = (_iota((tm, tm), 0) >= _iota((tm, tm), 1)).astype(F32)
        c = jnp.dot(tri, lf, precision=HIGHEST, preferred_element_type=F32) + carry[...]
        carry[...] = carry[...] + jnp.sum(lf, axis=0, keepdims=True)
        return [c, jnp.where(keep, c, BIG)], []
    return _rowwise(fn, [flog], [bf], [(LANES, F32), (LANES, F32)], [], name="fox_cumsum_fwd",
                    carry=(1, LANES))[0]


def _fox_cumsum_bwd(dc_q, dc_k, flog, bf):
    def fn(row0, ins, bc, carry):
        (dq, dk, f), (b,) = ins, bc
        d = dq + dk
        tm = f.shape[0]
        keep = _row_ids(row0, tm) >= ROW0
        triu = (_iota((tm, tm), 0) <= _iota((tm, tm), 1)).astype(F32)
        dlf = jnp.dot(triu, d, precision=HIGHEST, preferred_element_type=F32) + carry[...]
        carry[...] = carry[...] + jnp.sum(d, axis=0, keepdims=True)
        dfl = jnp.where(keep, dlf * _sigmoid(-(f + b)), 0.0)
        return [dfl], [jnp.sum(dfl, axis=0, keepdims=True)]
    (dflog,), (dbf,) = _rowwise(fn, [dc_q, dc_k, flog], [bf], [(LANES, F32)], [(1, LANES)], name="fox_cumsum_bwd",
                                reverse=True, carry=(1, LANES))
    return dflog, dbf


def _causal_steps(n, key_major):
    if key_major:
        pairs = [(i, j) for j in range(n) for i in range(j, n)]
    else:
        pairs = [(i, j) for i in range(n) for j in range(i + 1)]
    return (jnp.asarray(np.array([p[0] for p in pairs], np.int32)),
            jnp.asarray(np.array([p[1] for p in pairs], np.int32)))


def _fox_attn_fwd(qp, kp, vp, gate, shards):
    L = qp.shape[0]
    t = _tile(L, ATTN_TILES)
    n = L // t
    hps = FOX_HPS_FWD
    P = FOX_H // hps
    it, jt = _causal_steps(n, False)
    n_steps = it.shape[0]
    ns = len(shards)

    def body(it_ref, jt_ref, q_ref, k_ref, v_ref, g_ref, *rest):
        sh_in, (o_ref, og_ref, lse_ref), sh_out = rest[:ns], rest[ns:ns + 3], rest[ns + 3:2 * ns + 3]
        m_sc, acc, ssem, rsem, lsem = rest[2 * ns + 3:]
        step = pl.program_id(1)
        i, j = it_ref[step], jt_ref[step]
        first = (pl.program_id(0) == 0) & (step == 0)
        last = (pl.program_id(0) == P - 1) & (step == n_steps - 1)

        @pl.when(first)
        def _():
            _gather_start(sh_in, sh_out, ssem, rsem, lsem)

        @pl.when(j == 0)
        def _():
            m_sc[...] = jnp.full_like(m_sc, -3.0e38)
            acc[...] = jnp.zeros_like(acc)

        def update(masked):
            cut = MXU_N if t > MXU_N else t
            blocks = [(0, cut, cut), (cut, t, t)] if masked and cut < t else [(0, t, t)]

            def head(hh, q0, q1, k1):
                sl = slice(hh * LANES, (hh + 1) * LANES)
                s2 = lax.dot_general(k_ref[0:k1, sl], q_ref[q0:q1, sl], NT, preferred_element_type=F32)
                if masked:
                    s2 = jnp.where(_iota((k1, q1 - q0), 0) <= _iota((k1, q1 - q0), 1) + q0, s2, -jnp.inf)
                yield
                m_old = m_sc[hh, :, q0:q1]
                m_new = jnp.maximum(m_old, jnp.max(s2, axis=0, keepdims=True))
                yield
                p = jnp.exp2(s2 - m_new).astype(BF16)
                yield
                acc[hh, :, q0:q1] = jnp.exp2(m_old - m_new) * acc[hh, :, q0:q1] + lax.dot_general(
                    v_ref[0:k1, sl], p, TN, preferred_element_type=F32)
                m_sc[hh, :, q0:q1] = m_new

            _interleave((head(hh, *b) for hh in range(hps) for b in blocks), skew=1)

        @pl.when(j < i)
        def _():
            update(False)

        @pl.when(j == i)
        def _():
            update(True)
            outs = []
            for hh in range(hps):
                base = _extra_base(hh % 2)
                l = acc[hh, base:base + 1, :]
                outs.append((acc[hh] / l).T)
                lse_ref[0, hh:hh + 1, :] = m_sc[hh] + jnp.log2(l)
            o = jnp.concatenate([jnp.where(_data_mask(0), outs[a], outs[a + 1]) for a in range(0, hps, 2)], axis=1)
            o_ref[...] = o.astype(o_ref.dtype)
            og_ref[...] = (o * _sigmoid(g_ref[...].astype(F32))).astype(og_ref.dtype)

        @pl.when(last)
        def _():
            _gather_wait(sh_in, sh_out, ssem, rsem, lsem)

    qspec = pl.BlockSpec((t, hps * LANES), lambda p, s, it, jt: (it[s], p))
    kspec = pl.BlockSpec((t, hps * LANES), lambda p, s, it, jt: (jt[s], p))
    ospec = pl.BlockSpec((t, hps * FOX_DH), lambda p, s, it, jt: (it[s], p))
    lspec = pl.BlockSpec((1, hps, t), lambda p, s, it, jt: (p, 0, it[s]))
    res = pl.pallas_call(
        body, name="fox_attn_fwd",
        grid_spec=pltpu.PrefetchScalarGridSpec(
            num_scalar_prefetch=2, grid=(P, n_steps),
            in_specs=[qspec, kspec, kspec, ospec] + [ANY] * ns, out_specs=[ospec, ospec, lspec] + [ANY] * ns,
            scratch_shapes=[pltpu.VMEM((hps, 1, t), F32), pltpu.VMEM((hps, LANES, t), F32)] + _gather_sems(ns)),
        out_shape=[jax.ShapeDtypeStruct((L, D), BF16), jax.ShapeDtypeStruct((L, D), BF16),
                   jax.ShapeDtypeStruct((P, hps, L), F32)]
        + [jax.ShapeDtypeStruct((4,) + a.shape, a.dtype) for a in shards],
        compiler_params=_cparams(("arbitrary", "arbitrary")),
    )(it, jt, qp, kp, vp, gate, *shards)
    return res[0], res[1], res[2], res[3:]


def _fox_attn_bwd(qb, kp, vp, dop, slabs):
    L = qb.shape[0]
    t = _tile(L, ATTN_TILES)
    n = L // t
    hps = FOX_HPS_BWD
    P = FOX_H // hps
    it, jt = _causal_steps(n, True)
    n_steps = it.shape[0]
    ns = len(slabs)

    def body(it_ref, jt_ref, q_ref, k_ref, v_ref, do_ref, *rest):
        sl_in, (dq_ref, dk_ref, dv_ref, dck_ref), sl_out = rest[:ns], rest[ns:ns + 4], rest[ns + 4:2 * ns + 4]
        dk_acc, dv_acc, ssem, rsem = rest[2 * ns + 4:]
        step = pl.program_id(1)
        i, j = it_ref[step], jt_ref[step]

        @pl.when((pl.program_id(0) == 0) & (step == 0))
        def _():
            for cp in _scatter_copies(sl_in, sl_out, ssem, rsem):
                cp.start()

        @pl.when(step == 0)
        def _():
            dq_ref[...] = jnp.zeros_like(dq_ref)

        @pl.when(i == j)
        def _():
            dk_acc[...] = jnp.zeros_like(dk_acc)
            dv_acc[...] = jnp.zeros_like(dv_acc)

        def update(masked):
            cut = MXU_N if t > MXU_N else t
            blocks = [(0, cut, cut), (cut, t, t)] if masked and cut < t else [(0, t, t)]
            for hh in range(hps):
                sl = slice(hh * LANES, (hh + 1) * LANES)
                for q0, q1, k1 in blocks:
                    q, dov, k = q_ref[q0:q1, sl], do_ref[q0:q1, sl], k_ref[0:k1, sl]
                    s2 = lax.dot_general(k, q, NT, preferred_element_type=F32)
                    if masked:
                        s2 = jnp.where(_iota((k1, q1 - q0), 0) <= _iota((k1, q1 - q0), 1) + q0, s2, -jnp.inf)
                    p = jnp.exp2(s2)
                    ds = (p * lax.dot_general(v_ref[0:k1, sl], dov, NT, preferred_element_type=F32)).astype(BF16)
                    dv_acc[hh, 0:k1] += jnp.dot(p.astype(BF16), dov, preferred_element_type=F32)
                    dk_acc[hh, 0:k1] += jnp.dot(ds, q, preferred_element_type=F32)
                    rows = pl.ds(pl.multiple_of(i * t + q0, LANES), q1 - q0)
                    dq_ref[rows, sl] += lax.dot_general(ds, k, TN, preferred_element_type=F32)

        @pl.when(i > j)
        def _():
            update(False)

        @pl.when(i == j)
        def _():
            update(True)

        @pl.when(i == n - 1)
        def _():
            pairs = range(0, hps, 2)
            dk_ref[...] = jnp.concatenate([jnp.where(_data_mask(0), dk_acc[a], dk_acc[a + 1]) for a in pairs], axis=1)
            dv_ref[...] = jnp.concatenate([jnp.where(_data_mask(0), dv_acc[a], dv_acc[a + 1]) for a in pairs],
                                          axis=1).astype(dv_ref.dtype)
            lane = _iota((1, hps), 1)
            col_sums = jnp.zeros((t, hps), F32)
            for hh in range(hps):
                base = _extra_base(hh % 2) + 3
                col_sums = jnp.where(lane == hh, dk_acc[hh, :, base:base + 1], col_sums)
            dck_ref[0] = -col_sums

        @pl.when((pl.program_id(0) == P - 1) & (step == n_steps - 1))
        def _():
            for cp in _scatter_copies(sl_in, sl_out, ssem, rsem):
                cp.wait()

    qspec = pl.BlockSpec((t, hps * LANES), lambda p, s, it, jt: (it[s], p))
    kspec = pl.BlockSpec((t, hps * LANES), lambda p, s, it, jt: (jt[s], p))
    ospec = pl.BlockSpec((t, hps * FOX_DH), lambda p, s, it, jt: (jt[s], p))
    slab = pl.BlockSpec((L, hps * LANES), lambda p, s, it, jt: (0, p), pipeline_mode=pl.Buffered(1))
    res = pl.pallas_call(
        body, name="fox_attn_bwd",
        grid_spec=pltpu.PrefetchScalarGridSpec(
            num_scalar_prefetch=2, grid=(P, n_steps),
            in_specs=[qspec, kspec, kspec, qspec] + [ANY] * ns,
            out_specs=[slab, ospec, ospec,
                       pl.BlockSpec((1, t, hps), lambda p, s, it, jt: (p, jt[s], 0))] + [ANY] * ns,
            scratch_shapes=[pltpu.VMEM((hps, t, LANES), F32), pltpu.VMEM((hps, t, LANES), F32),
                            pltpu.SemaphoreType.DMA((ns, 3)), pltpu.SemaphoreType.DMA((ns, 3))]),
        out_shape=[jax.ShapeDtypeStruct((L, FOX_H * LANES), F32), jax.ShapeDtypeStruct((L, D), F32),
                   jax.ShapeDtypeStruct((L, D), BF16), jax.ShapeDtypeStruct((P, L, hps), F32)]
        + [jax.ShapeDtypeStruct((3,) + a.shape[1:], a.dtype) for a in slabs],
        compiler_params=_cparams(("arbitrary", "arbitrary")),
    )(it, jt, qb, kp, vp, dop, *slabs)
    return res[0], res[1], res[2], res[3], res[4:]


def _hgrn_consts():
    C = HG_C
    r = np.arange(C)[:, None]
    j = np.arange(C)[None, :]
    mats = [j <= r, j > r]
    masks = []
    n = C
    while n >= 2:
        half = n // 2
        mid = (r // n) * n + half - 1
        second = (r % n) >= half
        mats.append(np.where(second, (j > mid) & (j <= r), (j > r) & (j <= mid)))
        masks.append(((r // n) == (j // n)) & ((r % n) >= half) & ((j % n) < half))
        n //= 2
    return (jnp.asarray(np.concatenate(mats, 0).astype(np.float32), BF16),
            jnp.asarray(np.stack(masks).astype(np.float32), F32))


def _hg_pre(hq, hz, h0, h1):
    mx = jnp.maximum(h0, h1)
    e0, e1 = jnp.exp(h0 - mx), jnp.exp(h1 - mx)
    lb = e1 / (e0 + e1)
    sq = _sigmoid(hq)
    sz = _sigmoid(hz)
    snz = 1.0 - sz
    k = (1.0 - lb) * snz
    g = jnp.maximum(jnp.log(lb + (1.0 - lb) * sz), -BIG)
    return lb, hq * sq, sq, k, sz, snz, g


def _hg_decays(g, rmat):
    hi = g.astype(BF16)
    lo = (g - hi.astype(F32)).astype(BF16)
    d = jnp.dot(rmat, jnp.concatenate([hi, lo], axis=1), preferred_element_type=F32)
    return jnp.exp(d[:, :HG_D] + d[:, HG_D:])


def _interleave(programs, skew=0):
    progs = list(programs)
    done = [False] * len(progs)
    tick = 0
    while not all(done):
        for n, g in enumerate(progs):
            if not done[n] and tick >= n * skew:
                try:
                    next(g)
                except StopIteration:
                    done[n] = True
        tick += 1


def _hg_intra_levels(q, k, fall, masks):
    C = HG_C
    eye = _iota((C, C), 0) == _iota((C, C), 1)
    a = jnp.where(eye, jnp.sum(q * k, axis=-1, keepdims=True), 0.0)
    for l in range(HG_LEV):
        f = fall[(2 + l) * C:(3 + l) * C]
        a = a + masks[l] * lax.dot_general((q * f).astype(BF16), (k * f).astype(BF16), NT,
                                           preferred_element_type=F32)
        yield a


def _hgrn_specs(n_chunks, reverse):
    C = HG_C
    w = HG_HPS * HG_D

    def col(first_head):
        off = first_head // HG_HPS
        if reverse:
            return pl.BlockSpec((C, w), lambda h, c: (n_chunks - 1 - c, off + h))
        return pl.BlockSpec((C, w), lambda h, c: (c, off + h))

    st = pl.BlockSpec((HG_HPS, 1, HG_D, HG_D),
                      (lambda h, c: (h, n_chunks - 1 - c, 0, 0)) if reverse else (lambda h, c: (h, c, 0, 0)))
    consts = [pl.BlockSpec((2, w), lambda h, c: (0, h)), pl.BlockSpec((1, HG_D), lambda h, c: (0, 0)),
              pl.BlockSpec(((2 + HG_LEV) * C, C), lambda h, c: (0, 0)),
              pl.BlockSpec((HG_LEV, C, C), lambda h, c: (0, 0, 0))]
    return col, st, consts


def _hgrn_fwd(proj, hlb, gg, rmat, masks):
    L = proj.shape[0]
    C = HG_C
    nc = L // C
    col, st, consts = _hgrn_specs(nc, False)

    def body(hq_ref, hz_ref, hi_ref, hg_ref, hlb_ref, gg_ref, r_ref, m_ref, og_ref, st_ref, state):
        c = pl.program_id(1)

        @pl.when(c == 0)
        def _():
            state[...] = jnp.zeros_like(state)

        def head(hh):
            sl = slice(hh * HG_D, (hh + 1) * HG_D)
            v, hg = hi_ref[:, sl], hg_ref[:, sl]
            _, q, _, k, _, _, g = _hg_pre(hq_ref[:, sl], hz_ref[:, sl], hlb_ref[0:1, sl], hlb_ref[1:2, sl])
            yield
            fall = _hg_decays(g, r_ref[...])
            fb, fe = fall[0:C], fall[C:2 * C]
            st0 = state[hh]
            st_ref[hh, 0] = st0
            yield
            for a in _hg_intra_levels(q, k, fall, m_ref[...]):
                yield
            vb = v.astype(BF16)
            o = jnp.dot(a.astype(BF16), vb, preferred_element_type=F32)
            o = o + lax.dot_general((q * fb).astype(BF16), st0.astype(BF16), NT, preferred_element_type=F32)
            yield
            ebc = jnp.exp(jnp.sum(g, axis=0, keepdims=True))
            state[hh] = st0 * ebc + lax.dot_general(vb, (k * fe).astype(BF16), TN, preferred_element_type=F32)
            r = lax.rsqrt(jnp.mean(o * o, axis=-1, keepdims=True) + EPS)
            og_ref[:, sl] = (o * r * gg_ref[...] * (hg * _sigmoid(hg))).astype(og_ref.dtype)

        _interleave((head(hh) for hh in range(HG_HPS)), skew=HG_SKEW)

    return pl.pallas_call(
        body, name="hgrn_fwd", grid=(HG_H // HG_HPS, nc),
        in_specs=[col(0), col(HG_H), col(2 * HG_H), col(3 * HG_H)] + consts,
        out_specs=[col(0), st],
        out_shape=[jax.ShapeDtypeStruct((L, D), BF16), jax.ShapeDtypeStruct((HG_H, nc, HG_D, HG_D), F32)],
        scratch_shapes=[pltpu.VMEM((HG_HPS, HG_D, HG_D), F32)],
        compiler_params=_cparams(("parallel", "arbitrary")),
    )(proj, proj, proj, proj, hlb, gg, rmat, masks)


def _hgrn_bwd(proj, dog, states, hlb, gg, rmat, masks):
    L = proj.shape[0]
    C = HG_C
    nc = L // C
    col, st, consts = _hgrn_specs(nc, True)

    def body(hq_ref, hz_ref, hi_ref, hg_ref, do_ref, hlb_ref, gg_ref, r_ref, m_ref, st_ref,
             dq_ref, dz_ref, di_ref, dg_ref, dlb_ref, dgg_ref, dstate):
        c = pl.program_id(1)

        @pl.when(c == 0)
        def _():
            dstate[...] = jnp.zeros_like(dstate)
            dlb_ref[...] = jnp.zeros_like(dlb_ref)
            dgg_ref[...] = jnp.zeros_like(dgg_ref)

        _interleave([bwd_head(c, hh, slice(hh * HG_D, (hh + 1) * HG_D), hq_ref, hz_ref, hi_ref, hg_ref, do_ref, hlb_ref,
                              gg_ref, r_ref, m_ref, st_ref, dq_ref, dz_ref, di_ref, dg_ref, dlb_ref, dgg_ref, dstate)
                     for hh in range(HG_HPS)], skew=HG_SKEW)

    def bwd_head(c, hh, sl, hq_ref, hz_ref, hi_ref, hg_ref, do_ref, hlb_ref, gg_ref, r_ref, m_ref, st_ref,
                 dq_ref, dz_ref, di_ref, dg_ref, dlb_ref, dgg_ref, dstate):
        hq, hz, v, hg = hq_ref[:, sl], hz_ref[:, sl], hi_ref[:, sl], hg_ref[:, sl]
        dout = do_ref[:, sl].astype(F32)
        gain = gg_ref[...]
        masks_v = m_ref[...]
        lb, q, sq, k, sz, snz, g = _hg_pre(hq, hz, hlb_ref[0:1, sl], hlb_ref[1:2, sl])
        yield
        fall = _hg_decays(g, r_ref[...])
        fb, fe = fall[0:C], fall[C:2 * C]
        yield
        for a in _hg_intra_levels(q, k, fall, masks_v):
            yield
        st0 = st_ref[hh, 0]
        st0b = st0.astype(BF16)
        ebc = jnp.exp(jnp.sum(g, axis=0, keepdims=True))
        qb, ke, vb = (q * fb).astype(BF16), (k * fe).astype(BF16), v.astype(BF16)
        ab = a.astype(BF16)
        o = jnp.dot(ab, vb, preferred_element_type=F32) + lax.dot_general(qb, st0b, NT, preferred_element_type=F32)
        yield
        r = lax.rsqrt(jnp.mean(o * o, axis=-1, keepdims=True) + EPS)
        oh = o * r
        sg = _sigmoid(hg)
        d_on = dout * (hg * sg)
        dhg = dout * (oh * gain) * (sg * (1.0 + hg * (1.0 - sg)))
        dgg_ref[hh] += jnp.sum(d_on * oh, axis=0, keepdims=True)
        dxh = d_on * gain
        do = r * (dxh - oh * jnp.mean(dxh * oh, axis=-1, keepdims=True))
        dob = do.astype(BF16)
        yield
        dsp = dstate[hh]
        dspb = dsp.astype(BF16)
        causal = _iota((C, C), 0) >= _iota((C, C), 1)
        da = jnp.where(causal, lax.dot_general(dob, vb, NT, preferred_element_type=F32), 0.0)
        diag = jnp.sum(do * v, axis=-1, keepdims=True)
        yield
        dv = lax.dot_general(ab, dob, TN, preferred_element_type=F32)
        dv = dv + lax.dot_general(ke, dspb, NT, preferred_element_type=F32)
        yield
        xq = jnp.dot(dob, st0b, preferred_element_type=F32)
        xk = jnp.dot(vb, dspb, preferred_element_type=F32)
        dq = diag * k + fb * xq
        dk = diag * q + fe * xk
        ke_xk = ke.astype(F32) * xk
        db = qb.astype(F32) * xq - ke_xk
        yield
        for l in range(HG_LEV):
            f = fall[(2 + l) * C:(3 + l) * C]
            dal = (masks_v[l] * da).astype(BF16)
            ql, kl = (q * f).astype(BF16), (k * f).astype(BF16)
            xq = jnp.dot(dal, kl, preferred_element_type=F32)
            xk = lax.dot_general(dal, ql, TN, preferred_element_type=F32)
            dq = dq + f * xq
            dk = dk + f * xk
            db = db + ql.astype(F32) * xq - kl.astype(F32) * xk
            yield
        dstate[hh] = dsp * ebc + lax.dot_general(dob, qb, TN, preferred_element_type=F32)
        triu = (_iota((C, C), 0) <= _iota((C, C), 1)).astype(F32)
        dg = jnp.dot(triu, db, precision=HIGHEST, preferred_element_type=F32)
        dg = dg + jnp.sum(st0 * ebc * dsp, axis=0, keepdims=True) + jnp.sum(ke_xk, axis=0, keepdims=True)
        keep = _row_ids((nc - 1 - c) * C, C) >= ROW0
        dg = jnp.where(keep, dg, 0.0)
        dk = jnp.where(keep, dk, 0.0)
        f_gate = lb + (1.0 - lb) * sz
        dfdz = (1.0 - lb) * sz * snz
        dz_ref[:, sl] = (dg * dfdz / f_gate - dk * dfdz).astype(dz_ref.dtype)
        dlb_ref[:, sl] += jnp.sum(dg * snz / f_gate - dk * snz, axis=0, keepdims=True)
        dq_ref[:, sl] = jnp.where(keep, dq * (sq * (1.0 + hq * (1.0 - sq))), 0.0).astype(dq_ref.dtype)
        di_ref[:, sl] = jnp.where(keep, dv, 0.0).astype(di_ref.dtype)
        dg_ref[:, sl] = jnp.where(keep, dhg, 0.0).astype(dg_ref.dtype)

    w = HG_HPS * HG_D
    outs = pl.pallas_call(
        body, name="hgrn_bwd", grid=(HG_H // HG_HPS, nc),
        in_specs=[col(0), col(HG_H), col(2 * HG_H), col(3 * HG_H), col(0)] + consts + [st],
        out_specs=[col(0), col(0), col(0), col(0), pl.BlockSpec((1, w), lambda h, c: (0, h)),
                   pl.BlockSpec((HG_HPS, 1, HG_D), lambda h, c: (h, 0, 0))],
        out_shape=[jax.ShapeDtypeStruct((L, D), BF16)] * 4 + [jax.ShapeDtypeStruct((1, D), F32),
                                                              jax.ShapeDtypeStruct((HG_H, 1, HG_D), F32)],
        scratch_shapes=[pltpu.VMEM((HG_HPS, HG_D, HG_D), F32)],
        compiler_params=_cparams(("parallel", "arbitrary")),
    )(proj, proj, proj, proj, dog, hlb, gg, rmat, masks, states)
    return outs


def _ffn_fwd(h, norm_gain, wg, wu, wo, tag):
    hn, g, u, act = _norm_proj(h, norm_gain, [wg, wu], [BF16, BF16], f"{tag}_in", swiglu=True)
    h_out = _matmul(act, wo, add=h, name=f"{tag}_out")
    return h_out, (h, hn, g, u, act)


def _ffn_bwd(dh, saved, norm_gain, wg, wu, wo, tag):
    h, hn, g, u, act = saved
    dg, du = _ffn_dact(dh, wo, g, u, f"{tag}_dact")
    d_wo = _matmul(act, dh, ta=True, name=f"{tag}_dwo")
    d_wg = _matmul(hn, dg, ta=True, name=f"{tag}_dwg")
    d_wu = _matmul(hn, du, ta=True, name=f"{tag}_dwu")
    dh, d_gain, _ = _dhn_norm([dg, du], [wg, wu], h, dh, norm_gain, f"{tag}_dhn_norm")
    return dh, d_gain, (d_wg, d_wu, d_wo)


def _local_step(h0, tgt, w, late_shards):
    L = h0.shape[0]
    gmat = jnp.asarray(np.kron(np.eye(MXU_N // FOX_DH), np.ones((FOX_DH, FOX_DH))).astype(np.float32), BF16)
    rmat, lmasks = _hgrn_consts()
    an, fn_ = w["attn_norm"], w["ffn_norm"]
    qg = jnp.tile(w["fox_q_norm"], (1, FOX_H))
    kg = jnp.tile(w["fox_k_norm"], (1, FOX_H))
    bf = jnp.pad(w["fox_b_f"], ((0, 0), (0, LANES - FOX_H)))
    fw = w["fox_w_in"]
    f_wq, f_wk, f_wv, f_wg = (fw[:, i * D:(i + 1) * D] for i in range(4))
    f_wf = jnp.pad(fw[:, 4 * D:], ((0, 0), (0, LANES - FOX_H)))

    hn0, q_raw, k_raw, v, gate, flog = _norm_proj(h0, an[0:1], [f_wq, f_wk, f_wv, f_wg, f_wf],
                                                  [F32, F32, BF16, BF16, F32], "fox_in")
    cq, ck = _fox_cumsum_fwd(flog, bf)
    qp, kp, vp = _fox_pack_fwd(q_raw, k_raw, v, cq, ck, qg, kg, gmat)
    o, og, lse2, gathered = _fox_attn_fwd(qp, kp, vp, gate, [late_shards[n] for n in GATHER_LATE])
    late = dict(zip(GATHER_LATE, gathered))
    f_wo = late["fox_w_out"].reshape(D, D)
    h_wo = late["hgrn_w_out"].reshape(D, D)
    h_wi = jnp.concatenate(list(late["hgrn_w_in"]), axis=1)
    g_in, g_out = late["ffn_w_in"], late["ffn_w_out"]
    ffw = []
    for i in range(2):
        rows_in, rows_out = slice(i * D, (i + 1) * D), slice(i * FFN // 4, (i + 1) * FFN // 4)
        ffw.append((jnp.concatenate([g_in[0, rows_in], g_in[1, rows_in]], axis=1),
                    jnp.concatenate([g_in[2, rows_in], g_in[3, rows_in]], axis=1),
                    jnp.concatenate([g_out[j, rows_out] for j in range(4)], axis=0)))
    h1 = _matmul(og, f_wo, add=h0, name="fox_out")
    h2, ffn0 = _ffn_fwd(h1, fn_[0:1], *ffw[0], "ffn0")

    hn2, proj = _norm_proj(h2, an[1:2], [h_wi], [F32], "hgrn_in")
    og1, states = _hgrn_fwd(proj, w["hgrn_lower_bounds"], w["hgrn_g_norm"], rmat, lmasks)
    h3 = _matmul(og1, h_wo, add=h2, name="hgrn_out")
    h4, ffn1 = _ffn_fwd(h3, fn_[1:2], *ffw[1], "ffn1")

    loss, dh, d_final = _loss_bwd(h4, tgt, w["final_norm"])

    dh, d_fn1, d_ffn1 = _ffn_bwd(dh, ffn1, fn_[1:2], *ffw[1], "ffn1")
    dog1 = _matmul(dh, h_wo, tb=True, out_dtype=BF16, name="hgrn_dog")
    d_h_wo = _matmul(og1, dh, ta=True, name="hgrn_dwo")
    dpq, dpz, dpi, dpg, d_lb, d_gg = _hgrn_bwd(proj, dog1, states, w["hgrn_lower_bounds"], w["hgrn_g_norm"],
                                               rmat, lmasks)
    dproj = jnp.concatenate([dpq, dpz, dpi, dpg], axis=1)
    d_h_wi = _matmul(hn2, dproj, ta=True, name="hgrn_dwi")
    dh, d_an1, _ = _dhn_norm([dproj], [h_wi], h2, dh, an[1:2], "hgrn_dhn_norm")

    dh, d_fn0, d_ffn0 = _ffn_bwd(dh, ffn0, fn_[0:1], *ffw[0], "ffn0")
    n_in, n_out = 2 * FFN // 4, FFN // 4
    d_ffn = [d_ffn0, d_ffn1]
    late_grads = dict(
        hgrn_w_in=_to_shards("hgrn_w_in", d_h_wi[None]), hgrn_w_out=d_h_wo.reshape(4, D // 4, D),
        ffn_w_in=jnp.stack([jnp.concatenate([d[j // 2][:, (j % 2) * n_in:(j % 2 + 1) * n_in] for d in d_ffn], axis=0)
                            for j in range(4)]),
        ffn_w_out=jnp.stack([jnp.concatenate([d[2][j * n_out:(j + 1) * n_out] for d in d_ffn], axis=0)
                             for j in range(4)]))
    pair_late, send_late = _pair_sums([late_grads[n] for n in LATE_NAMES], "late")

    dog = _matmul(dh, f_wo, tb=True, out_dtype=BF16, name="fox_dog")
    d_f_wo = _matmul(og, dh, ta=True, name="fox_dwo")

    def by_head(a):
        return jnp.pad(a.transpose(1, 0, 2).reshape(L, FOX_H), ((0, 0), (0, LANES - FOX_H)))

    qb = _fox_pack_bias(qp, cq, by_head(lse2.transpose(0, 2, 1)))
    dop, dgate = _fox_pack_bwd(dog, o, gate)
    dqp, dk, dv, dck, recv_late = _fox_attn_bwd(qb, kp, vp, dop, send_late)
    (dq_raw, dk_raw, dc_q), (d_qg, d_kg) = _fox_unpack_bwd(q_raw, k_raw, dqp, dk, qg, kg, gmat)
    dflog, d_bf = _fox_cumsum_bwd(dc_q, by_head(dck), flog, bf)
    dproj0 = jnp.concatenate([dq_raw, dk_raw, dv, dgate, dflog.astype(BF16)], axis=1)
    f_wall = jnp.concatenate([f_wq, f_wk, f_wv, f_wg, f_wf], axis=1)
    d_f_wall = _matmul(hn0, dproj0, ta=True, name="fox_dwi")
    d_f_wi = d_f_wall[:, :4 * D + FOX_H]
    fox_grads = dict(fox_w_in=d_f_wi[None], fox_w_out=d_f_wo[None])
    pair_fox, send_fox = _pair_sums([_to_shards(n, fox_grads[n]) for n in FOX_NAMES], "fox")
    dh, d_an0, recv_fox = _dhn_norm([dproj0], [f_wall], h0, dh, an[0:1], "fox_dhn_norm", slabs=send_fox)
    halves = _chip_sums(pair_fox, recv_fox, "fox") + _chip_sums(pair_late, recv_late, "late")
    theirs = _sibling_exchange(halves)
    big = {n: (m, t) for n, m, t in zip(FOX_NAMES + LATE_NAMES, halves, theirs)}
    small = dict(attn_norm=jnp.concatenate([d_an0, d_an1]), ffn_norm=jnp.concatenate([d_fn0, d_fn1]),
                 final_norm=d_final, lb_raw=d_lb, q_gain=d_qg, k_gain=d_kg, b_f=d_bf,
                 g_gain=d_gg.reshape(1, D))
    return loss, dh, big, small


def _me():
    return lax.axis_index("x"), lax.axis_index("y"), lax.axis_index("c")


def _flip(v, bit):
    return 1 - v if bit else v


def _chip_allgather_split(big, small):
    half = big.shape[0] // 2

    def body(big_in, small_in, big_out, small_out, ssem, rsem, fs_sem, fr_sem, ssem2, rsem2, lsem):
        x, y, c = _me()
        sib = (x, y, 1 - c)
        peers = _chip_peers()
        mine, other = pl.ds(c * half, half), pl.ds((1 - c) * half, half)
        local = [pltpu.make_async_copy(big_in, big_out.at[2 * x + y], lsem.at[0]),
                 pltpu.make_async_copy(small_in, small_out.at[2 * x + y], lsem.at[1])]
        sends = []
        for k, peer in enumerate(peers):
            sends.append(pltpu.make_async_remote_copy(big_in.at[mine], big_out.at[2 * x + y, mine], ssem.at[k],
                                                      rsem.at[k], device_id=peer, device_id_type=MESH))
            sends.append(pltpu.make_async_remote_copy(small_in, small_out.at[2 * x + y], ssem2.at[k], rsem2.at[k],
                                                      device_id=peer, device_id_type=MESH))
        for cp in local + sends:
            cp.start()
        forwards = []
        for k, peer in enumerate(peers):
            landed = big_out.at[2 * peer[0] + peer[1], mine]
            pltpu.make_async_remote_copy(big_in.at[mine], landed, ssem.at[k], rsem.at[k],
                                         device_id=peer, device_id_type=MESH).wait_recv()
            fwd = pltpu.make_async_remote_copy(landed, landed, fs_sem.at[k], fr_sem.at[k],
                                               device_id=sib, device_id_type=MESH)
            fwd.start()
            forwards.append(fwd)
        for k, peer in enumerate(peers):
            theirs = big_out.at[2 * peer[0] + peer[1], other]
            pltpu.make_async_remote_copy(theirs, theirs, fs_sem.at[k], fr_sem.at[k],
                                         device_id=sib, device_id_type=MESH).wait_recv()
            pltpu.make_async_remote_copy(small_in, small_out.at[2 * peer[0] + peer[1]], ssem2.at[k], rsem2.at[k],
                                         device_id=peer, device_id_type=MESH).wait_recv()
        for cp in sends + forwards:
            cp.wait_send()
        for cp in local:
            cp.wait()

    three = pltpu.SemaphoreType.DMA((3,))
    return pl.pallas_call(
        body, name="chip_allgather", in_specs=[ANY, ANY], out_specs=[ANY, ANY],
        out_shape=[jax.ShapeDtypeStruct((4,) + big.shape, big.dtype),
                   jax.ShapeDtypeStruct((4,) + small.shape, small.dtype)],
        scratch_shapes=[three, three, three, three, three, three, pltpu.SemaphoreType.DMA((2,))],
    )(big, small)


def _chip_peers():
    x, y, c = _me()
    return [(1 - x, y, c), (x, 1 - y, c), (1 - x, 1 - y, c)]


def _gather_sems(n):
    return [pltpu.SemaphoreType.DMA((n, 3)), pltpu.SemaphoreType.DMA((n, 3)), pltpu.SemaphoreType.DMA((n,))]


def _gather_copies(ins, outs, ssem, rsem, lsem, with_recvs):
    x, y, _ = _me()
    local, sends, recvs = [], [], []
    for a in range(len(ins)):
        local.append(pltpu.make_async_copy(ins[a], outs[a].at[2 * x + y], lsem.at[a]))
        for k, peer in enumerate(_chip_peers()):
            sends.append(pltpu.make_async_remote_copy(ins[a], outs[a].at[2 * x + y], ssem.at[a, k], rsem.at[a, k],
                                                      device_id=peer, device_id_type=MESH))
            if with_recvs:
                recvs.append(pltpu.make_async_remote_copy(ins[a], outs[a].at[2 * peer[0] + peer[1]], ssem.at[a, k],
                                                          rsem.at[a, k], device_id=peer, device_id_type=MESH))
    return local, sends, recvs


def _gather_start(ins, outs, ssem, rsem, lsem):
    local, sends, _ = _gather_copies(ins, outs, ssem, rsem, lsem, False)
    for cp in local + sends:
        cp.start()


def _gather_wait(ins, outs, ssem, rsem, lsem):
    local, sends, recvs = _gather_copies(ins, outs, ssem, rsem, lsem, True)
    for cp in local:
        cp.wait()
    for cp in sends:
        cp.wait_send()
    for cp in recvs:
        cp.wait_recv()


def _scatter_copies(ins, outs, ssem, rsem):
    copies = []
    for a in range(len(ins)):
        for k, peer in enumerate(_chip_peers()):
            copies.append(pltpu.make_async_remote_copy(ins[a].at[2 * peer[0] + peer[1]], outs[a].at[k], ssem.at[a, k],
                                                       rsem.at[a, k], device_id=peer, device_id_type=MESH))
    return copies


def _device_allgather(arr):
    def body(in_ref, out_ref, ssem, rsem, lsem):
        x, y, c = _me()
        me = 4 * x + 2 * y + c
        peers = [(_flip(x, k & 4), _flip(y, k & 2), _flip(c, k & 1)) for k in range(1, 8)]
        local = pltpu.make_async_copy(in_ref, out_ref.at[me], lsem)
        local.start()
        sends = []
        for k, peer in enumerate(peers):
            cp = pltpu.make_async_remote_copy(in_ref, out_ref.at[me], ssem.at[k], rsem.at[k],
                                              device_id=peer, device_id_type=MESH)
            cp.start()
            sends.append(cp)
        local.wait()
        for cp in sends:
            cp.wait_send()
        for k, peer in enumerate(peers):
            pltpu.make_async_remote_copy(in_ref, out_ref.at[4 * peer[0] + 2 * peer[1] + peer[2]], ssem.at[k],
                                         rsem.at[k], device_id=peer, device_id_type=MESH).wait_recv()

    return pl.pallas_call(
        body, name="device_allgather", in_specs=[ANY], out_specs=ANY,
        out_shape=jax.ShapeDtypeStruct((8,) + arr.shape, arr.dtype),
        scratch_shapes=[pltpu.SemaphoreType.DMA((7,)), pltpu.SemaphoreType.DMA((7,)), pltpu.SemaphoreType.DMA],
    )(arr)


def _sibling_send_other_half(arrs, tag):
    n = len(arrs)

    def body(*refs):
        ins, outs = refs[:n], refs[n:2 * n]
        ssem, rsem = refs[2 * n:]
        x, y, c = _me()
        cps = []
        for a in range(n):
            half = ins[a].shape[1] // 2
            src = ins[a].at[:, pl.ds((1 - c) * half, half), :]
            cp = pltpu.make_async_remote_copy(src, outs[a], ssem.at[a], rsem.at[a],
                                              device_id=(x, y, 1 - c), device_id_type=MESH)
            cp.start()
            cps.append(cp)
        for cp in cps:
            cp.wait()

    return pl.pallas_call(
        body, name=f"grad_sibling_swap_{tag}", in_specs=[ANY] * n, out_specs=[ANY] * n,
        out_shape=[jax.ShapeDtypeStruct((4, a.shape[1] // 2, a.shape[2]), a.dtype) for a in arrs],
        scratch_shapes=[pltpu.SemaphoreType.DMA((n,)), pltpu.SemaphoreType.DMA((n,))],
    )(*arrs)


def _sibling_exchange(arrs):
    n = len(arrs)

    def body(*refs):
        ins, outs = refs[:n], refs[n:2 * n]
        ssem, rsem = refs[2 * n:]
        x, y, c = _me()
        cps = [pltpu.make_async_remote_copy(ins[a], outs[a], ssem.at[a], rsem.at[a], device_id=(x, y, 1 - c),
                                            device_id_type=MESH) for a in range(n)]
        for cp in cps:
            cp.start()
        for cp in cps:
            cp.wait()

    return pl.pallas_call(
        body, name="grad_sibling_exchange", in_specs=[ANY] * n, out_specs=[ANY] * n,
        out_shape=[jax.ShapeDtypeStruct(a.shape, a.dtype) for a in arrs],
        scratch_shapes=[pltpu.SemaphoreType.DMA((n,)), pltpu.SemaphoreType.DMA((n,))],
    )(*arrs)


def _pair_sums(grads, tag):
    got = _sibling_send_other_half(grads, tag)
    res = [_pair_add(g, t, f"grad_pair_add_{tag}{i}") for i, (g, t) in enumerate(zip(grads, got))]
    return [r[0] for r in res], [r[1] for r in res]


def _mesh_scalar(v):
    return jnp.asarray(v, jnp.int32).reshape(1)


def _pair_add(g, t, name):
    _, rows, cols = g.shape
    half = rows // 2
    tm = _tile(half, cap=(2 * 1024 * 1024) // (4 * cols))

    def body(c_ref, g_ref, t_ref, o_ref, ob_ref):
        s = g_ref[0, 0] + t_ref[0]
        o_ref[0] = s
        ob_ref[0] = s.astype(ob_ref.dtype)

    spec = pl.BlockSpec((1, tm, cols), lambda j, i, c: (j, i, 0))
    return pl.pallas_call(
        body, name=name,
        grid_spec=pltpu.PrefetchScalarGridSpec(
            num_scalar_prefetch=1, grid=(4, half // tm),
            in_specs=[pl.BlockSpec((1, 1, tm, cols), lambda j, i, c: (j, c[0], i, 0)), spec],
            out_specs=[spec, spec]),
        out_shape=[jax.ShapeDtypeStruct(t.shape, F32), jax.ShapeDtypeStruct(t.shape, BF16)],
        compiler_params=_cparams(("parallel", "parallel")),
    )(_mesh_scalar(lax.axis_index("c")), g.reshape(4, 2, half, cols), t)


def _chip_sums(pair, recv, tag):
    x, y, _ = _me()
    out = []
    for n, (p, r) in enumerate(zip(pair, recv)):
        _, half, cols = p.shape
        tm = _tile(half, cap=(2 * 1024 * 1024) // (4 * cols))

        def body(j_ref, p_ref, r_ref, o_ref):
            o_ref[...] = p_ref[0] + r_ref[0].astype(F32) + r_ref[1].astype(F32) + r_ref[2].astype(F32)

        out.append(pl.pallas_call(
            body, name=f"grad_chip_add_{tag}{n}",
            grid_spec=pltpu.PrefetchScalarGridSpec(
                num_scalar_prefetch=1, grid=(half // tm,),
                in_specs=[pl.BlockSpec((1, tm, cols), lambda i, j: (j[0], i, 0)),
                          pl.BlockSpec((3, tm, cols), lambda i, j: (0, i, 0))],
                out_specs=pl.BlockSpec((tm, cols), lambda i, j: (i, 0))),
            out_shape=jax.ShapeDtypeStruct((half, cols), F32),
            compiler_params=_cparams(("parallel",)),
        )(_mesh_scalar(2 * x + y), p, r))
    return out


SMALL_ROWS = 32


def _small_finalize(gathered, hlb, fold64, fold128):
    def body(g_ref, hlb_ref, f64_ref, f128_ref, rows_ref, qk_ref, gg_ref, lb_ref):
        tot = g_ref[0]
        for d in range(1, 8):
            tot = tot + g_ref[d]
        rows_ref[...] = tot
        qk_ref[...] = jnp.dot(rows_ref[6:8, :], f64_ref[...], precision=HIGHEST, preferred_element_type=F32)
        gg_ref[...] = jnp.dot(rows_ref[9:10, :], f128_ref[...], precision=HIGHEST, preferred_element_type=F32)
        h0, h1 = hlb_ref[0:1, :], hlb_ref[1:2, :]
        mx = jnp.maximum(h0, h1)
        e0, e1 = jnp.exp(h0 - mx), jnp.exp(h1 - mx)
        lb = e1 / (e0 + e1)
        d1 = rows_ref[5:6, :] * lb * (1.0 - lb)
        lb_ref[...] = jnp.where(_iota((2, 1), 0) == 0, -d1, d1)

    return pl.pallas_call(
        body, name="small_finalize",
        out_shape=[jax.ShapeDtypeStruct((SMALL_ROWS, D), F32), jax.ShapeDtypeStruct((2, FOX_DH), F32),
                   jax.ShapeDtypeStruct((1, HG_D), F32), jax.ShapeDtypeStruct((2, D), F32)],
    )(gathered, hlb, fold64, fold128)


FOX_NAMES = ("fox_w_in", "fox_w_out")
LATE_NAMES = ("hgrn_w_in", "hgrn_w_out", "ffn_w_in", "ffn_w_out")
BIG_NAMES = FOX_NAMES + LATE_NAMES
GATHER_LATE = ("fox_w_out",) + LATE_NAMES
COL_SHARDED = ("fox_w_in", "hgrn_w_in", "ffn_w_in")


def _shard2d(name, a):
    return a.reshape(-1, a.shape[-1])


def _to_shards(name, g):
    layers = g.shape[0]
    if name in COL_SHARDED:
        k, n = g.shape[1], g.shape[2] // 4
        return g.reshape(layers, k, 4, n).transpose(2, 0, 1, 3).reshape(4, layers * k, n)
    r = g.shape[1] // 4
    return g.reshape(layers, 4, r, g.shape[2]).transpose(1, 0, 2, 3).reshape(4, layers * r, g.shape[2])


def kernel(x, meta_tokens, attn_norm, ffn_norm, final_norm, fox_w_in, fox_b_f, fox_q_norm, fox_k_norm, fox_w_out, hgrn_w_in, hgrn_lower_bounds, hgrn_g_norm, hgrn_w_out, ffn_w_in, ffn_w_out, loss_target, m_meta_tokens, m_attn_norm, m_ffn_norm, m_final_norm, m_fox_w_in, m_fox_b_f, m_fox_q_norm, m_fox_k_norm, m_fox_w_out, m_hgrn_w_in, m_hgrn_lower_bounds, m_hgrn_g_norm, m_hgrn_w_out, m_ffn_w_in, m_ffn_w_out, v_meta_tokens, v_attn_norm, v_ffn_norm, v_final_norm, v_fox_w_in, v_fox_b_f, v_fox_q_norm, v_fox_k_norm, v_fox_w_out, v_hgrn_w_in, v_hgrn_lower_bounds, v_hgrn_g_norm, v_hgrn_w_out, v_ffn_w_in, v_ffn_w_out):
    params = dict(meta_tokens=meta_tokens, attn_norm=attn_norm, ffn_norm=ffn_norm, final_norm=final_norm,
                  fox_w_in=fox_w_in, fox_b_f=fox_b_f, fox_q_norm=fox_q_norm, fox_k_norm=fox_k_norm,
                  fox_w_out=fox_w_out, hgrn_w_in=hgrn_w_in, hgrn_lower_bounds=hgrn_lower_bounds,
                  hgrn_g_norm=hgrn_g_norm, hgrn_w_out=hgrn_w_out, ffn_w_in=ffn_w_in, ffn_w_out=ffn_w_out)
    mom_m = dict(meta_tokens=m_meta_tokens, attn_norm=m_attn_norm, ffn_norm=m_ffn_norm, final_norm=m_final_norm,
                 fox_w_in=m_fox_w_in, fox_b_f=m_fox_b_f, fox_q_norm=m_fox_q_norm, fox_k_norm=m_fox_k_norm,
                 fox_w_out=m_fox_w_out, hgrn_w_in=m_hgrn_w_in, hgrn_lower_bounds=m_hgrn_lower_bounds,
                 hgrn_g_norm=m_hgrn_g_norm, hgrn_w_out=m_hgrn_w_out, ffn_w_in=m_ffn_w_in, ffn_w_out=m_ffn_w_out)
    mom_v = dict(meta_tokens=v_meta_tokens, attn_norm=v_attn_norm, ffn_norm=v_ffn_norm, final_norm=v_final_norm,
                 fox_w_in=v_fox_w_in, fox_b_f=v_fox_b_f, fox_q_norm=v_fox_q_norm, fox_k_norm=v_fox_k_norm,
                 fox_w_out=v_fox_w_out, hgrn_w_in=v_hgrn_w_in, hgrn_lower_bounds=v_hgrn_lower_bounds,
                 hgrn_g_norm=v_hgrn_g_norm, hgrn_w_out=v_hgrn_w_out, ffn_w_in=v_ffn_w_in, ffn_w_out=v_ffn_w_out)
    names = list(params)
    xi, yi, _ = _me()

    shards = {n: _shard2d(n, params[n]).astype(BF16) for n in BIG_NAMES}
    w_in_g, meta_g = _chip_allgather_split(shards["fox_w_in"], meta_tokens)
    w = dict(fox_w_in=jnp.concatenate(list(w_in_g), axis=1))
    meta_full = jnp.concatenate(list(meta_g), axis=1)
    w.update(attn_norm=attn_norm, ffn_norm=ffn_norm, final_norm=final_norm.reshape(1, D), fox_b_f=fox_b_f,
             fox_q_norm=fox_q_norm, fox_k_norm=fox_k_norm, hgrn_lower_bounds=hgrn_lower_bounds,
             hgrn_g_norm=hgrn_g_norm)

    h0 = jnp.concatenate([jnp.zeros((ROW0, D), F32), meta_full, x[0]], axis=0)
    loss, dh0, big, small = _local_step(h0, loss_target[0], w, {n: shards[n] for n in GATHER_LATE})
    loss = lax.psum(loss, ("x", "y", "c"))
    grad_x = dh0[PAD:][None]
    grads = {}

    rows = jnp.concatenate([small["attn_norm"], small["ffn_norm"], small["final_norm"], small["lb_raw"],
                            small["q_gain"], small["k_gain"],
                            jnp.pad(small["b_f"], ((0, 0), (0, D - LANES))), small["g_gain"],
                            dh0[ROW0:PAD], jnp.zeros((SMALL_ROWS - 10 - N_META, D), F32)], axis=0)
    allrows = _device_allgather(rows)
    fold64 = jnp.asarray(np.tile(np.eye(FOX_DH, dtype=np.float32), (FOX_H, 1)))
    fold128 = jnp.asarray(np.tile(np.eye(HG_D, dtype=np.float32), (HG_H, 1)))
    tot, qk, gg, dlb = _small_finalize(allrows, hgrn_lower_bounds, fold64, fold128)
    grads.update(attn_norm=tot[0:2], ffn_norm=tot[2:4], final_norm=tot[4], hgrn_lower_bounds=dlb,
                 fox_q_norm=qk[0:1], fox_k_norm=qk[1:2], fox_b_f=tot[8:9, :FOX_H], hgrn_g_norm=gg,
                 meta_tokens=lax.dynamic_slice_in_dim(tot[10:10 + N_META], (2 * xi + yi) * (D // 4), D // 4, axis=1))

    delta, new_m, new_v = {}, {}, {}
    for n in BIG_NAMES:
        res = _adamw_halves(_shard2d(n, params[n]), *big[n], _shard2d(n, mom_m[n]), _shard2d(n, mom_v[n]),
                            f"adamw_{n}")
        grads[n], delta[n], new_m[n], new_v[n] = (t.reshape(params[n].shape) for t in res)
    delta["meta_tokens"], new_m["meta_tokens"], new_v["meta_tokens"] = _adamw(
        meta_tokens, grads["meta_tokens"], m_meta_tokens, v_meta_tokens, "adamw_meta_tokens")
    small_names = [n for n in names if n not in BIG_NAMES and n != "meta_tokens"]

    def pack(d):
        return jnp.concatenate([jnp.pad(d[n].reshape(-1, d[n].shape[-1]), ((0, 0), (0, D - d[n].shape[-1])))
                                for n in small_names], axis=0)

    packed = [pack(t) for t in (params, grads, mom_m, mom_v)]
    n_rows = packed[0].shape[0]
    packed = [jnp.pad(t, ((0, 16 - n_rows), (0, 0))) for t in packed]
    res = _adamw(*packed, "adamw_small")
    r0 = 0
    for n in small_names:
        nr = params[n].reshape(-1, params[n].shape[-1]).shape[0]
        for dst, src in zip((delta, new_m, new_v), res):
            dst[n] = src[r0:r0 + nr, :params[n].shape[-1]].reshape(params[n].shape)
        r0 += nr

    return (loss, grad_x, *[grads[n] for n in names], *[delta[n] for n in names],
            *[new_m[n] for n in names], *[new_v[n] for n in names])
```
